```python
import math
import jax
import jax.numpy as jnp
from jax import lax
import numpy as np

D_MODEL = 1024
BATCH = 8
SEQ = 4096
DEPTH = 2

N_META = 16
CONV_K = 4
RMS_EPS = 1e-6
L2_EPS = 1e-6
D_FF = 4 * D_MODEL
N_BRANCH = 3

GDN_HEADS = 8
GDN_DK = 128
GDN_DV = 128
GDN_CHUNK = 64
GDN_QK_W = GDN_HEADS * GDN_DK
GDN_V_W = GDN_HEADS * GDN_DV

SSD_HEADS = 16
SSD_HEAD_DIM = 64
SSD_INNER = SSD_HEADS * SSD_HEAD_DIM
SSD_GROUPS = 4
SSD_HPG = SSD_HEADS // SSD_GROUPS
SSD_STATE = 128
SSD_CHUNK = 128
SSD_CONV_DIM = SSD_INNER + 2 * SSD_GROUPS * SSD_STATE

SWA_Q_HEADS = 16
SWA_KV_HEADS = 4
SWA_REP = SWA_Q_HEADS // SWA_KV_HEADS
SWA_HEAD_DIM = 64
SWA_WINDOW = 128
SWA_Q_W = SWA_Q_HEADS * SWA_HEAD_DIM
SWA_KV_W = SWA_KV_HEADS * SWA_HEAD_DIM

IN_SIZES = (GDN_QK_W, GDN_QK_W, GDN_V_W, GDN_V_W, GDN_HEADS, GDN_HEADS,
            SSD_INNER, SSD_CONV_DIM, SSD_HEADS,
            SWA_Q_W, SWA_KV_W, SWA_KV_W,
            N_BRANCH * D_MODEL)
IN_W = sum(IN_SIZES)

kernel_name = "hybrid_gdn_ssd_swa_sink_block"


def rmsnorm(x, w):
    xf = x.astype(jnp.float32)
    y = xf * lax.rsqrt(jnp.mean(xf * xf, axis=-1, keepdims=True) + RMS_EPS)
    return (y * w.astype(jnp.float32)).astype(x.dtype)


def l2norm(x):
    return x * lax.rsqrt(jnp.sum(x * x, axis=-1, keepdims=True) + L2_EPS)


def split_in(u):
    offs = np.cumsum(np.array(IN_SIZES))[:-1].tolist()
    return jnp.split(u, offs, axis=-1)


def causal_dwconv(x, w, b=None):
    y = lax.conv_general_dilated(
        x, w[:, None, :].astype(x.dtype), window_strides=(1,), padding=[(CONV_K - 1, 0)],
        dimension_numbers=("NWC", "WIO", "NWC"), feature_group_count=x.shape[-1])
    if b is not None:
        y = y + b.astype(x.dtype)
    return y


def pad_front(t, pad):
    return jnp.pad(t, [(0, 0), (pad, 0)] + [(0, 0)] * (t.ndim - 2))


def softmax_with_sink(s, sink):
    m = jnp.maximum(jnp.max(s, axis=-1, keepdims=True), sink)
    e = jnp.exp(s - m)
    return e / (jnp.sum(e, axis=-1, keepdims=True) + jnp.exp(sink - m))


def gated_delta_chunked(q, k, v, g, beta):
    bsz, t_len, nh, dk = q.shape
    dv = v.shape[-1]
    c = GDN_CHUNK
    nc = t_len // c
    q = q.reshape(bsz, nc, c, nh, dk).transpose(0, 3, 1, 2, 4) * (dk ** -0.5)
    k = k.reshape(bsz, nc, c, nh, dk).transpose(0, 3, 1, 2, 4)
    v = v.reshape(bsz, nc, c, nh, dv).transpose(0, 3, 1, 2, 4)
    g = g.reshape(bsz, nc, c, nh).transpose(0, 3, 1, 2)
    beta = beta.reshape(bsz, nc, c, nh).transpose(0, 3, 1, 2)
    gam = jnp.cumsum(g, axis=-1)
    tri_incl = jnp.tril(jnp.ones((c, c), dtype=bool))
    tri_strict = jnp.tril(jnp.ones((c, c), dtype=bool), -1)
    decay = jnp.exp(jnp.where(tri_incl, gam[..., :, None] - gam[..., None, :], -jnp.inf))
    kb = k * beta[..., None]
    a_low = jnp.where(tri_strict, jnp.einsum("bhncd,bhnsd->bhncs", kb, k) * decay, 0.0)
    eye = jnp.eye(c, dtype=jnp.float32)
    rhs = jnp.concatenate([v * beta[..., None], kb * jnp.exp(gam)[..., None]], axis=-1)
    sol = lax.linalg.triangular_solve(a_low + eye, rhs, left_side=True, lower=True, unit_diagonal=True)
    u = sol[..., :dv]
    w = sol[..., dv:]
    attn_qk = jnp.einsum("bhncd,bhnsd->bhncs", q, k) * decay
    q_dec = q * jnp.exp(gam)[..., None]
    g_last = gam[..., -1]
    k_tail = k * jnp.exp(g_last[..., None] - gam)[..., None]

    def step(s_state, inp):
        u_c, w_c, qd_c, a_c, kt_c, gl_c = inp
        v_new = u_c - jnp.einsum("bhcd,bhde->bhce", w_c, s_state)
        o_c = jnp.einsum("bhcd,bhde->bhce", qd_c, s_state) + jnp.einsum("bhcs,bhse->bhce", a_c, v_new)
        s_state = s_state * jnp.exp(gl_c)[..., None, None] + jnp.einsum("bhcd,bhce->bhde", kt_c, v_new)
        return s_state, o_c

    xs = (jnp.moveaxis(u, 2, 0), jnp.moveaxis(w, 2, 0), jnp.moveaxis(q_dec, 2, 0),
          jnp.moveaxis(attn_qk, 2, 0), jnp.moveaxis(k_tail, 2, 0), jnp.moveaxis(g_last, -1, 0))
    s0 = jnp.zeros((bsz, nh, dk, dv), jnp.float32)
    _, o = lax.scan(step, s0, xs)
    return o.transpose(1, 0, 3, 2, 4).reshape(bsz, t_len, nh, dv)


def gdn_branch(q, k, v, gate, b, a, conv_w, a_log, dt_bias, norm_w):
    dtype = q.dtype
    bsz, seq_len = q.shape[:2]
    qkv = jax.nn.silu(causal_dwconv(jnp.concatenate([q, k, v], axis=-1), conv_w))
    q, k, v = jnp.split(qkv, [GDN_QK_W, 2 * GDN_QK_W], axis=-1)
    q = l2norm(q.astype(jnp.float32).reshape(bsz, seq_len, GDN_HEADS, GDN_DK))
    k = l2norm(k.astype(jnp.float32).reshape(bsz, seq_len, GDN_HEADS, GDN_DK))
    v = v.astype(jnp.float32).reshape(bsz, seq_len, GDN_HEADS, GDN_DV)
    beta = jax.nn.sigmoid(b.astype(jnp.float32))
    g = -jnp.exp(a_log.astype(jnp.float32)) * jax.nn.softplus(a.astype(jnp.float32) + dt_bias.astype(jnp.float32))
    pad = GDN_CHUNK - N_META
    o = gated_delta_chunked(pad_front(q, pad), pad_front(k, pad), pad_front(v, pad),
                            pad_front(g, pad), pad_front(beta, pad))[:, pad:]
    gate = gate.astype(jnp.float32).reshape(bsz, seq_len, GDN_HEADS, GDN_DV)
    o = rmsnorm(o, norm_w) * jax.nn.silu(gate)
    return o.reshape(bsz, seq_len, GDN_V_W).astype(dtype)


def ssd_chunked(xdt, adt, bm, cm):
    bsz, t_len, ng, nr, hp = xdt.shape
    c = SSD_CHUNK
    nc = t_len // c
    xdt = xdt.reshape(bsz, nc, c, ng, nr, hp)
    bm = bm.reshape(bsz, nc, c, ng, -1)
    cm = cm.reshape(bsz, nc, c, ng, -1)
    acum = jnp.cumsum(adt.reshape(bsz, nc, c, ng, nr).transpose(0, 3, 4, 1, 2), axis=-1)
    tri = jnp.tril(jnp.ones((c, c), dtype=bool))
    lmat = jnp.exp(jnp.where(tri, acum[..., :, None] - acum[..., None, :], -jnp.inf))
    cb = jnp.einsum("bclgn,bcsgn->bgcls", cm, bm)
    y_diag = jnp.einsum("bgcls,bgrcls,bcsgrp->bclgrp", cb, lmat, xdt)
    decay_states = jnp.exp(acum[..., -1:] - acum)
    states = jnp.einsum("bclgn,bgrcl,bclgrp->bcgrpn", bm, decay_states, xdt)
    chunk_decay = jnp.exp(acum[..., -1])

    def step(h, inp):
        st, dec = inp
        return h * dec[..., None, None] + st, h

    h0 = jnp.zeros((bsz, ng, nr, hp, states.shape[-1]), jnp.float32)
    _, h_in = lax.scan(step, h0, (jnp.moveaxis(states, 1, 0), jnp.moveaxis(chunk_decay, -1, 0)))
    h_in = jnp.moveaxis(h_in, 0, 1)
    y_off = jnp.einsum("bclgn,bcgrpn,bgrcl->bclgrp", cm, h_in, jnp.exp(acum))
    return (y_diag + y_off).reshape(bsz, t_len, ng, nr, hp)


def ssd_branch(z, xbc, dt, conv_w, conv_b, dt_bias, a_log, d_skip, norm_w):
    dtype = z.dtype
    bsz, seq_len = z.shape[:2]
    xbc = jax.nn.silu(causal_dwconv(xbc, conv_w, conv_b)).astype(jnp.float32)
    xs, bm, cm = jnp.split(xbc, [SSD_INNER, SSD_INNER + SSD_GROUPS * SSD_STATE], axis=-1)
    xs = xs.reshape(bsz, seq_len, SSD_GROUPS, SSD_HPG, SSD_HEAD_DIM)
    bm = bm.reshape(bsz, seq_len, SSD_GROUPS, SSD_STATE)
    cm = cm.reshape(bsz, seq_len, SSD_GROUPS, SSD_STATE)
    dtp = jax.nn.softplus(dt.astype(jnp.float32) + dt_bias.astype(jnp.float32))
    dtp = dtp.reshape(bsz, seq_len, SSD_GROUPS, SSD_HPG)
    a = -jnp.exp(a_log.astype(jnp.float32)).reshape(SSD_GROUPS, SSD_HPG)
    pad = SSD_CHUNK - N_META
    y = ssd_chunked(pad_front(xs * dtp[..., None], pad), pad_front(dtp * a, pad),
                    pad_front(bm, pad), pad_front(cm, pad))[:, pad:]
    y = y + d_skip.astype(jnp.float32).reshape(SSD_GROUPS, SSD_HPG)[:, :, None] * xs
    y = y.reshape(bsz, seq_len, SSD_INNER) * jax.nn.silu(z.astype(jnp.float32))
    y = y.reshape(bsz, seq_len, SSD_GROUPS, SSD_INNER // SSD_GROUPS)
    y = y * lax.rsqrt(jnp.mean(y * y, axis=-1, keepdims=True) + RMS_EPS)
    y = y * norm_w.astype(jnp.float32).reshape(SSD_GROUPS, SSD_INNER // SSD_GROUPS)
    return y.reshape(bsz, seq_len, SSD_INNER).astype(dtype)


def swa_branch(q, k, v, sinks):
    dtype = q.dtype
    bsz, seq_len = q.shape[:2]
    s_len = seq_len - N_META
    wdw = SWA_WINDOW
    nb = s_len // wdw
    q = q.astype(jnp.float32).reshape(bsz, seq_len, SWA_KV_HEADS, SWA_REP, SWA_HEAD_DIM) * (SWA_HEAD_DIM ** -0.5)
    k = k.astype(jnp.float32).reshape(bsz, seq_len, SWA_KV_HEADS, SWA_HEAD_DIM)
    v = v.astype(jnp.float32).reshape(bsz, seq_len, SWA_KV_HEADS, SWA_HEAD_DIM)
    sink = sinks.astype(jnp.float32).reshape(SWA_KV_HEADS, SWA_REP)
    qm, km, vm = q[:, :N_META], k[:, :N_META], v[:, :N_META]
    s_m = jnp.einsum("bqhrd,bkhd->bhrqk", qm, km)
    s_m = jnp.where(jnp.tril(jnp.ones((N_META, N_META), dtype=bool)), s_m, -jnp.inf)
    p_m = softmax_with_sink(s_m, sink[None, :, :, None, None])
    o_m = jnp.einsum("bhrqk,bkhd->bqhrd", p_m, vm).reshape(bsz, N_META, SWA_Q_W)
    qr = q[:, N_META:].reshape(bsz, nb, wdw, SWA_KV_HEADS, SWA_REP, SWA_HEAD_DIM)
    kr = k[:, N_META:].reshape(bsz, nb, wdw, SWA_KV_HEADS, SWA_HEAD_DIM)
    vr = v[:, N_META:].reshape(bsz, nb, wdw, SWA_KV_HEADS, SWA_HEAD_DIM)
    zpad = [(0, 0), (1, 0), (0, 0), (0, 0), (0, 0)]
    kband = jnp.concatenate([jnp.pad(kr, zpad)[:, :-1], kr], axis=2)
    vband = jnp.concatenate([jnp.pad(vr, zpad)[:, :-1], vr], axis=2)
    qpos = jnp.arange(wdw)[:, None] + wdw
    kpos = jnp.arange(2 * wdw)[None, :]
    band = (kpos <= qpos) & (kpos > qpos - wdw)
    prev_ok = (jnp.arange(nb)[:, None] > 0) | (jnp.arange(2 * wdw)[None, :] >= wdw)
    mask = band[None, :, :] & prev_ok[:, None, :]
    s_loc = jnp.einsum("bnqhrd,bnkhd->bnhrqk", qr, kband)
    s_loc = jnp.where(mask[None, :, None, None, :, :], s_loc, -jnp.inf)
    s_meta = jnp.einsum("bnqhrd,bkhd->bnhrqk", qr, km)
    p = softmax_with_sink(jnp.concatenate([s_loc, s_meta], axis=-1), sink[None, None, :, :, None, None])
    o_r = (jnp.einsum("bnhrqk,bnkhd->bnqhrd", p[..., :2 * wdw], vband)
           + jnp.einsum("bnhrqk,bkhd->bnqhrd", p[..., 2 * wdw:], vm))
    o_r = o_r.reshape(bsz, s_len, SWA_Q_W)
    return jnp.concatenate([o_m, o_r], axis=1).astype(dtype)


def _fwd_setup_inputs(seed: int = 0) -> dict:
    key = jax.random.key(seed)
    ks = jax.random.split(key, 23)
    f32 = jnp.float32

    def nrm(k, shape, scale):
        return jax.random.normal(k, shape, f32) * scale

    def gain(k, shape):
        return 1.0 + 0.02 * jax.random.normal(k, shape, f32)

    def dt_bias(k, shape):
        dt = jnp.exp(jax.random.uniform(k, shape, f32, math.log(1e-3), math.log(1e-1)))
        return dt + jnp.log(-jnp.expm1(-dt))

    def a_log(k, shape):
        return jnp.log(jax.random.uniform(k, shape, f32, 1.0, 16.0))

    return {
        "x": nrm(ks[0], (BATCH, SEQ, D_MODEL), 1.0),
        "meta_tokens": nrm(ks[1], (N_META, D_MODEL), 1.0),
        "norm1_w": gain(ks[2], (DEPTH, D_MODEL)),
        "w_in": nrm(ks[3], (DEPTH, D_MODEL, IN_W), D_MODEL ** -0.5),
        "gdn_conv_w": nrm(ks[4], (DEPTH, CONV_K, 2 * GDN_QK_W + GDN_V_W), CONV_K ** -0.5),
        "gdn_a_log": a_log(ks[5], (DEPTH, GDN_HEADS)),
        "gdn_dt_bias": dt_bias(ks[6], (DEPTH, GDN_HEADS)),
        "gdn_norm_w": gain(ks[7], (DEPTH, GDN_DV)),
        "ssd_conv_w": nrm(ks[8], (DEPTH, CONV_K, SSD_CONV_DIM), CONV_K ** -0.5),
        "ssd_conv_b": nrm(ks[9], (DEPTH, SSD_CONV_DIM), 0.02),
        "ssd_dt_bias": dt_bias(ks[10], (DEPTH, SSD_HEADS)),
        "ssd_a_log": a_log(ks[11], (DEPTH, SSD_HEADS)),
        "ssd_d": 1.0 + 0.1 * jax.random.normal(ks[12], (DEPTH, SSD_HEADS), f32),
        "ssd_norm_w": gain(ks[13], (DEPTH, SSD_INNER)),
        "swa_sinks": nrm(ks[14], (DEPTH, SWA_Q_HEADS), 0.5),
        "w_proj_gdn": nrm(ks[15], (DEPTH, GDN_V_W, D_MODEL), GDN_V_W ** -0.5),
        "w_proj_ssd": nrm(ks[16], (DEPTH, SSD_INNER, D_MODEL), SSD_INNER ** -0.5),
        "w_proj_swa": nrm(ks[17], (DEPTH, SWA_Q_W, D_MODEL), SWA_Q_W ** -0.5),
        "w_out": nrm(ks[18], (DEPTH, D_MODEL, D_MODEL), D_MODEL ** -0.5),
        "norm2_w": gain(ks[19], (DEPTH, D_MODEL)),
        "w_up": nrm(ks[20], (DEPTH, D_MODEL, D_FF), D_MODEL ** -0.5),
        "w_down": nrm(ks[21], (DEPTH, D_FF, D_MODEL), D_FF ** -0.5),
        "final_norm_w": gain(ks[22], (D_MODEL,)),
    }


def _fwd_reference(x, meta_tokens, norm1_w, w_in, gdn_conv_w, gdn_a_log, gdn_dt_bias, gdn_norm_w,
              ssd_conv_w, ssd_conv_b, ssd_dt_bias, ssd_a_log, ssd_d, ssd_norm_w, swa_sinks,
              w_proj_gdn, w_proj_ssd, w_proj_swa, w_out, norm2_w, w_up, w_down, final_norm_w):
    bsz = x.shape[0]
    meta = jnp.broadcast_to(meta_tokens.astype(x.dtype)[None], (bsz, N_META, D_MODEL))
    h = jnp.concatenate([meta, x], axis=1)
    for l in range(DEPTH):
        u = rmsnorm(h, norm1_w[l]) @ w_in[l]
        (a_q, a_k, a_v, a_gate, a_b, a_a, b_z, b_xbc, b_dt, c_q, c_k, c_v, gate_logits) = split_in(u)
        y_gdn = gdn_branch(a_q, a_k, a_v, a_gate, a_b, a_a, gdn_conv_w[l], gdn_a_log[l], gdn_dt_bias[l], gdn_norm_w[l])
        y_ssd = ssd_branch(b_z, b_xbc, b_dt, ssd_conv_w[l], ssd_conv_b[l], ssd_dt_bias[l], ssd_a_log[l], ssd_d[l], ssd_norm_w[l])
        y_swa = swa_branch(c_q, c_k, c_v, swa_sinks[l])
        gates = jax.nn.sigmoid(gate_logits.astype(jnp.float32)).astype(h.dtype)
        g_a, g_b, g_c = jnp.split(gates, N_BRANCH, axis=-1)
        merged = (g_a * (y_gdn @ w_proj_gdn[l]) + g_b * (y_ssd @ w_proj_ssd[l])
                  + g_c * (y_swa @ w_proj_swa[l]))
        h = h + merged @ w_out[l]
        hn = rmsnorm(h, norm2_w[l])
        h = h + jnp.square(jax.nn.relu(hn @ w_up[l])) @ w_down[l]
    return rmsnorm(h, final_norm_w)[:, N_META:]


import jax as _jax
import jax.numpy as _jnp

TWIN_FORMAT = 'train_step'
FWD_PARAMS = ['x', 'meta_tokens', 'norm1_w', 'w_in', 'gdn_conv_w', 'gdn_a_log', 'gdn_dt_bias', 'gdn_norm_w', 'ssd_conv_w', 'ssd_conv_b', 'ssd_dt_bias', 'ssd_a_log', 'ssd_d', 'ssd_norm_w', 'swa_sinks', 'w_proj_gdn', 'w_proj_ssd', 'w_proj_swa', 'w_out', 'norm2_w', 'w_up', 'w_down', 'final_norm_w']
TWIN_WEIGHTS = ['meta_tokens', 'norm1_w', 'w_in', 'gdn_conv_w', 'gdn_a_log', 'gdn_dt_bias', 'gdn_norm_w', 'ssd_conv_w', 'ssd_conv_b', 'ssd_dt_bias', 'ssd_a_log', 'ssd_d', 'ssd_norm_w', 'swa_sinks', 'w_proj_gdn', 'w_proj_ssd', 'w_proj_swa', 'w_out', 'norm2_w', 'w_up', 'w_down', 'final_norm_w']
TWIN_DIFF_INPUT = 'x'
TWIN_INPUTS = ['x', 'meta_tokens', 'norm1_w', 'w_in', 'gdn_conv_w', 'gdn_a_log', 'gdn_dt_bias', 'gdn_norm_w', 'ssd_conv_w', 'ssd_conv_b', 'ssd_dt_bias', 'ssd_a_log', 'ssd_d', 'ssd_norm_w', 'swa_sinks', 'w_proj_gdn', 'w_proj_ssd', 'w_proj_swa', 'w_out', 'norm2_w', 'w_up', 'w_down', 'final_norm_w', 'loss_target', 'm_meta_tokens', 'm_norm1_w', 'm_w_in', 'm_gdn_conv_w', 'm_gdn_a_log', 'm_gdn_dt_bias', 'm_gdn_norm_w', 'm_ssd_conv_w', 'm_ssd_conv_b', 'm_ssd_dt_bias', 'm_ssd_a_log', 'm_ssd_d', 'm_ssd_norm_w', 'm_swa_sinks', 'm_w_proj_gdn', 'm_w_proj_ssd', 'm_w_proj_swa', 'm_w_out', 'm_norm2_w', 'm_w_up', 'm_w_down', 'm_final_norm_w', 'v_meta_tokens', 'v_norm1_w', 'v_w_in', 'v_gdn_conv_w', 'v_gdn_a_log', 'v_gdn_dt_bias', 'v_gdn_norm_w', 'v_ssd_conv_w', 'v_ssd_conv_b', 'v_ssd_dt_bias', 'v_ssd_a_log', 'v_ssd_d', 'v_ssd_norm_w', 'v_swa_sinks', 'v_w_proj_gdn', 'v_w_proj_ssd', 'v_w_proj_swa', 'v_w_out', 'v_norm2_w', 'v_w_up', 'v_w_down', 'v_final_norm_w']
TWIN_OUTPUTS = ['loss', 'grad_x', 'grad_meta_tokens', 'grad_norm1_w', 'grad_w_in', 'grad_gdn_conv_w', 'grad_gdn_a_log', 'grad_gdn_dt_bias', 'grad_gdn_norm_w', 'grad_ssd_conv_w', 'grad_ssd_conv_b', 'grad_ssd_dt_bias', 'grad_ssd_a_log', 'grad_ssd_d', 'grad_ssd_norm_w', 'grad_swa_sinks', 'grad_w_proj_gdn', 'grad_w_proj_ssd', 'grad_w_proj_swa', 'grad_w_out', 'grad_norm2_w', 'grad_w_up', 'grad_w_down', 'grad_final_norm_w', 'delta_meta_tokens', 'delta_norm1_w', 'delta_w_in', 'delta_gdn_conv_w', 'delta_gdn_a_log', 'delta_gdn_dt_bias', 'delta_gdn_norm_w', 'delta_ssd_conv_w', 'delta_ssd_conv_b', 'delta_ssd_dt_bias', 'delta_ssd_a_log', 'delta_ssd_d', 'delta_ssd_norm_w', 'delta_swa_sinks', 'delta_w_proj_gdn', 'delta_w_proj_ssd', 'delta_w_proj_swa', 'delta_w_out', 'delta_norm2_w', 'delta_w_up', 'delta_w_down', 'delta_final_norm_w', 'new_m_meta_tokens', 'new_m_norm1_w', 'new_m_w_in', 'new_m_gdn_conv_w', 'new_m_gdn_a_log', 'new_m_gdn_dt_bias', 'new_m_gdn_norm_w', 'new_m_ssd_conv_w', 'new_m_ssd_conv_b', 'new_m_ssd_dt_bias', 'new_m_ssd_a_log', 'new_m_ssd_d', 'new_m_ssd_norm_w', 'new_m_swa_sinks', 'new_m_w_proj_gdn', 'new_m_w_proj_ssd', 'new_m_w_proj_swa', 'new_m_w_out', 'new_m_norm2_w', 'new_m_w_up', 'new_m_w_down', 'new_m_final_norm_w', 'new_v_meta_tokens', 'new_v_norm1_w', 'new_v_w_in', 'new_v_gdn_conv_w', 'new_v_gdn_a_log', 'new_v_gdn_dt_bias', 'new_v_gdn_norm_w', 'new_v_ssd_conv_w', 'new_v_ssd_conv_b', 'new_v_ssd_dt_bias', 'new_v_ssd_a_log', 'new_v_ssd_d', 'new_v_ssd_norm_w', 'new_v_swa_sinks', 'new_v_w_proj_gdn', 'new_v_w_proj_ssd', 'new_v_w_proj_swa', 'new_v_w_out', 'new_v_norm2_w', 'new_v_w_up', 'new_v_w_down', 'new_v_final_norm_w']
TWIN_LEAF_KINDS = {'loss': 'loss', 'grad_x': 'grad_x', 'grad_meta_tokens': 'grad_w', 'grad_norm1_w': 'grad_w', 'grad_w_in': 'grad_w', 'grad_gdn_conv_w': 'grad_w', 'grad_gdn_a_log': 'grad_w', 'grad_gdn_dt_bias': 'grad_w', 'grad_gdn_norm_w': 'grad_w', 'grad_ssd_conv_w': 'grad_w', 'grad_ssd_conv_b': 'grad_w', 'grad_ssd_dt_bias': 'grad_w', 'grad_ssd_a_log': 'grad_w', 'grad_ssd_d': 'grad_w', 'grad_ssd_norm_w': 'grad_w', 'grad_swa_sinks': 'grad_w', 'grad_w_proj_gdn': 'grad_w', 'grad_w_proj_ssd': 'grad_w', 'grad_w_proj_swa': 'grad_w', 'grad_w_out': 'grad_w', 'grad_norm2_w': 'grad_w', 'grad_w_up': 'grad_w', 'grad_w_down': 'grad_w', 'grad_final_norm_w': 'grad_w', 'delta_meta_tokens': 'delta_w', 'delta_norm1_w': 'delta_w', 'delta_w_in': 'delta_w', 'delta_gdn_conv_w': 'delta_w', 'delta_gdn_a_log': 'delta_w', 'delta_gdn_dt_bias': 'delta_w', 'delta_gdn_norm_w': 'delta_w', 'delta_ssd_conv_w': 'delta_w', 'delta_ssd_conv_b': 'delta_w', 'delta_ssd_dt_bias': 'delta_w', 'delta_ssd_a_log': 'delta_w', 'delta_ssd_d': 'delta_w', 'delta_ssd_norm_w': 'delta_w', 'delta_swa_sinks': 'delta_w', 'delta_w_proj_gdn': 'delta_w', 'delta_w_proj_ssd': 'delta_w', 'delta_w_proj_swa': 'delta_w', 'delta_w_out': 'delta_w', 'delta_norm2_w': 'delta_w', 'delta_w_up': 'delta_w', 'delta_w_down': 'delta_w', 'delta_final_norm_w': 'delta_w', 'new_m_meta_tokens': 'new_m', 'new_m_norm1_w': 'new_m', 'new_m_w_in': 'new_m', 'new_m_gdn_conv_w': 'new_m', 'new_m_gdn_a_log': 'new_m', 'new_m_gdn_dt_bias': 'new_m', 'new_m_gdn_norm_w': 'new_m', 'new_m_ssd_conv_w': 'new_m', 'new_m_ssd_conv_b': 'new_m', 'new_m_ssd_dt_bias': 'new_m', 'new_m_ssd_a_log': 'new_m', 'new_m_ssd_d': 'new_m', 'new_m_ssd_norm_w': 'new_m', 'new_m_swa_sinks': 'new_m', 'new_m_w_proj_gdn': 'new_m', 'new_m_w_proj_ssd': 'new_m', 'new_m_w_proj_swa': 'new_m', 'new_m_w_out': 'new_m', 'new_m_norm2_w': 'new_m', 'new_m_w_up': 'new_m', 'new_m_w_down': 'new_m', 'new_m_final_norm_w': 'new_m', 'new_v_meta_tokens': 'new_v', 'new_v_norm1_w': 'new_v', 'new_v_w_in': 'new_v', 'new_v_gdn_conv_w': 'new_v', 'new_v_gdn_a_log': 'new_v', 'new_v_gdn_dt_bias': 'new_v', 'new_v_gdn_norm_w': 'new_v', 'new_v_ssd_conv_w': 'new_v', 'new_v_ssd_conv_b': 'new_v', 'new_v_ssd_dt_bias': 'new_v', 'new_v_ssd_a_log': 'new_v', 'new_v_ssd_d': 'new_v', 'new_v_ssd_norm_w': 'new_v', 'new_v_swa_sinks': 'new_v', 'new_v_w_proj_gdn': 'new_v', 'new_v_w_proj_ssd': 'new_v', 'new_v_w_proj_swa': 'new_v', 'new_v_w_out': 'new_v', 'new_v_norm2_w': 'new_v', 'new_v_w_up': 'new_v', 'new_v_w_down': 'new_v', 'new_v_final_norm_w': 'new_v'}


def _forward(args):
    return _fwd_reference(*[args[k] for k in FWD_PARAMS])


def _output_shape():
    out = _jax.eval_shape(lambda: _forward(_fwd_setup_inputs(0)))
    return out.shape, out.dtype

N_MICROBATCH = 1
ADAM_LR = 0.001
ADAM_B1 = 0.9
ADAM_B2 = 0.999
ADAM_EPS = 1e-08
ADAM_WD = 0.01
ADAM_STEP = 10
PER_EXAMPLE_BATCH_AXIS = {'x': 0, 'loss_target': 0}
SHARED_INPUTS = []
_WEIGHT_DTYPES = {'meta_tokens': _jnp.float32, 'norm1_w': _jnp.float32, 'w_in': _jnp.float32, 'gdn_conv_w': _jnp.float32, 'gdn_a_log': _jnp.float32, 'gdn_dt_bias': _jnp.float32, 'gdn_norm_w': _jnp.float32, 'ssd_conv_w': _jnp.float32, 'ssd_conv_b': _jnp.float32, 'ssd_dt_bias': _jnp.float32, 'ssd_a_log': _jnp.float32, 'ssd_d': _jnp.float32, 'ssd_norm_w': _jnp.float32, 'swa_sinks': _jnp.float32, 'w_proj_gdn': _jnp.float32, 'w_proj_ssd': _jnp.float32, 'w_proj_swa': _jnp.float32, 'w_out': _jnp.float32, 'norm2_w': _jnp.float32, 'w_up': _jnp.float32, 'w_down': _jnp.float32, 'final_norm_w': _jnp.float32}
MOMENT_SCALE = {'meta_tokens': 5.827359e-03, 'norm1_w': 1.506076e-01, 'w_in': 4.493732e-02, 'gdn_conv_w': 3.680570e-02, 'gdn_a_log': 1.756579e-01, 'gdn_dt_bias': 1.634124e-01, 'gdn_norm_w': 1.404351e-01, 'ssd_conv_w': 6.175956e-02, 'ssd_conv_b': 9.026831e-02, 'ssd_dt_bias': 1.708552e-01, 'ssd_a_log': 1.212700e-01, 'ssd_d': 3.684731e-01, 'ssd_norm_w': 8.439484e-02, 'swa_sinks': 3.212931e-03, 'w_proj_gdn': 4.722877e-02, 'w_proj_ssd': 8.330702e-02, 'w_proj_swa': 2.092294e-02, 'w_out': 9.762651e-02, 'norm2_w': 1.453542e-01, 'w_up': 7.233321e-02, 'w_down': 1.421360e-01, 'final_norm_w': 3.245536e+01}


def _to_microbatches(a, axis):
    t = _jnp.moveaxis(a, axis, 0)
    t = t.reshape((N_MICROBATCH, t.shape[0] // N_MICROBATCH) + t.shape[1:])
    return _jnp.moveaxis(t, 1, axis + 1)


def setup_inputs(seed: int = 0) -> dict:
    inp = _fwd_setup_inputs(seed)
    key = _jax.random.fold_in(_jax.random.key(seed), 7919)
    shape, _ = _output_shape()
    out = dict(inp)
    out["loss_target"] = _jax.random.normal(_jax.random.fold_in(key, 0), shape, _jnp.float32)
    for i, name in enumerate(TWIN_WEIGHTS):
        w = inp[name].astype(_jnp.float32)
        if MOMENT_SCALE is None:
            s = _jnp.sqrt(_jnp.mean(_jnp.square(w)) + 1e-30)
        else:
            s = MOMENT_SCALE[name]
        km, kv = _jax.random.split(_jax.random.fold_in(key, i + 1))
        out[name] = w
        out["m_" + name] = s * _jax.random.normal(km, w.shape, _jnp.float32)
        out["v_" + name] = (s * s) * _jax.random.uniform(kv, w.shape, _jnp.float32, 0.5, 1.5)
    if N_MICROBATCH > 1:
        for name, axis in PER_EXAMPLE_BATCH_AXIS.items():
            out[name] = _to_microbatches(out[name], axis)
    return {'x': out['x'], 'meta_tokens': out['meta_tokens'], 'norm1_w': out['norm1_w'], 'w_in': out['w_in'], 'gdn_conv_w': out['gdn_conv_w'], 'gdn_a_log': out['gdn_a_log'], 'gdn_dt_bias': out['gdn_dt_bias'], 'gdn_norm_w': out['gdn_norm_w'], 'ssd_conv_w': out['ssd_conv_w'], 'ssd_conv_b': out['ssd_conv_b'], 'ssd_dt_bias': out['ssd_dt_bias'], 'ssd_a_log': out['ssd_a_log'], 'ssd_d': out['ssd_d'], 'ssd_norm_w': out['ssd_norm_w'], 'swa_sinks': out['swa_sinks'], 'w_proj_gdn': out['w_proj_gdn'], 'w_proj_ssd': out['w_proj_ssd'], 'w_proj_swa': out['w_proj_swa'], 'w_out': out['w_out'], 'norm2_w': out['norm2_w'], 'w_up': out['w_up'], 'w_down': out['w_down'], 'final_norm_w': out['final_norm_w'], 'loss_target': out['loss_target'], 'm_meta_tokens': out['m_meta_tokens'], 'm_norm1_w': out['m_norm1_w'], 'm_w_in': out['m_w_in'], 'm_gdn_conv_w': out['m_gdn_conv_w'], 'm_gdn_a_log': out['m_gdn_a_log'], 'm_gdn_dt_bias': out['m_gdn_dt_bias'], 'm_gdn_norm_w': out['m_gdn_norm_w'], 'm_ssd_conv_w': out['m_ssd_conv_w'], 'm_ssd_conv_b': out['m_ssd_conv_b'], 'm_ssd_dt_bias': out['m_ssd_dt_bias'], 'm_ssd_a_log': out['m_ssd_a_log'], 'm_ssd_d': out['m_ssd_d'], 'm_ssd_norm_w': out['m_ssd_norm_w'], 'm_swa_sinks': out['m_swa_sinks'], 'm_w_proj_gdn': out['m_w_proj_gdn'], 'm_w_proj_ssd': out['m_w_proj_ssd'], 'm_w_proj_swa': out['m_w_proj_swa'], 'm_w_out': out['m_w_out'], 'm_norm2_w': out['m_norm2_w'], 'm_w_up': out['m_w_up'], 'm_w_down': out['m_w_down'], 'm_final_norm_w': out['m_final_norm_w'], 'v_meta_tokens': out['v_meta_tokens'], 'v_norm1_w': out['v_norm1_w'], 'v_w_in': out['v_w_in'], 'v_gdn_conv_w': out['v_gdn_conv_w'], 'v_gdn_a_log': out['v_gdn_a_log'], 'v_gdn_dt_bias': out['v_gdn_dt_bias'], 'v_gdn_norm_w': out['v_gdn_norm_w'], 'v_ssd_conv_w': out['v_ssd_conv_w'], 'v_ssd_conv_b': out['v_ssd_conv_b'], 'v_ssd_dt_bias': out['v_ssd_dt_bias'], 'v_ssd_a_log': out['v_ssd_a_log'], 'v_ssd_d': out['v_ssd_d'], 'v_ssd_norm_w': out['v_ssd_norm_w'], 'v_swa_sinks': out['v_swa_sinks'], 'v_w_proj_gdn': out['v_w_proj_gdn'], 'v_w_proj_ssd': out['v_w_proj_ssd'], 'v_w_proj_swa': out['v_w_proj_swa'], 'v_w_out': out['v_w_out'], 'v_norm2_w': out['v_norm2_w'], 'v_w_up': out['v_w_up'], 'v_w_down': out['v_w_down'], 'v_final_norm_w': out['v_final_norm_w']}


def _loss(weights, diff, rest, loss_target):
    with _jax.named_scope("forward"):
        args = {**rest, TWIN_DIFF_INPUT: diff, **{k: w.astype(_WEIGHT_DTYPES[k]) for k, w in weights.items()}}
        y = _forward(args)
    with _jax.named_scope("loss_head"):
        err = _jnp.square(y.astype(_jnp.float32) - loss_target)
        return 0.5 * _jnp.sum(_jnp.mean(err, axis=-1)) if err.ndim else 0.5 * err


def _adamw(w, g, m, v):
    m = ADAM_B1 * m + (1.0 - ADAM_B1) * g
    v = ADAM_B2 * v + (1.0 - ADAM_B2) * _jnp.square(g)
    m_hat = m / (1.0 - ADAM_B1 ** ADAM_STEP)
    v_hat = v / (1.0 - ADAM_B2 ** ADAM_STEP)
    delta = -ADAM_LR * (m_hat / (_jnp.sqrt(v_hat) + ADAM_EPS) + ADAM_WD * w)
    return delta, m, v


def reference(x, meta_tokens, norm1_w, w_in, gdn_conv_w, gdn_a_log, gdn_dt_bias, gdn_norm_w, ssd_conv_w, ssd_conv_b, ssd_dt_bias, ssd_a_log, ssd_d, ssd_norm_w, swa_sinks, w_proj_gdn, w_proj_ssd, w_proj_swa, w_out, norm2_w, w_up, w_down, final_norm_w, loss_target, m_meta_tokens, m_norm1_w, m_w_in, m_gdn_conv_w, m_gdn_a_log, m_gdn_dt_bias, m_gdn_norm_w, m_ssd_conv_w, m_ssd_conv_b, m_ssd_dt_bias, m_ssd_a_log, m_ssd_d, m_ssd_norm_w, m_swa_sinks, m_w_proj_gdn, m_w_proj_ssd, m_w_proj_swa, m_w_out, m_norm2_w, m_w_up, m_w_down, m_final_norm_w, v_meta_tokens, v_norm1_w, v_w_in, v_gdn_conv_w, v_gdn_a_log, v_gdn_dt_bias, v_gdn_norm_w, v_ssd_conv_w, v_ssd_conv_b, v_ssd_dt_bias, v_ssd_a_log, v_ssd_d, v_ssd_norm_w, v_swa_sinks, v_w_proj_gdn, v_w_proj_ssd, v_w_proj_swa, v_w_out, v_norm2_w, v_w_up, v_w_down, v_final_norm_w):
    given = dict(x=x, meta_tokens=meta_tokens, norm1_w=norm1_w, w_in=w_in, gdn_conv_w=gdn_conv_w, gdn_a_log=gdn_a_log, gdn_dt_bias=gdn_dt_bias, gdn_norm_w=gdn_norm_w, ssd_conv_w=ssd_conv_w, ssd_conv_b=ssd_conv_b, ssd_dt_bias=ssd_dt_bias, ssd_a_log=ssd_a_log, ssd_d=ssd_d, ssd_norm_w=ssd_norm_w, swa_sinks=swa_sinks, w_proj_gdn=w_proj_gdn, w_proj_ssd=w_proj_ssd, w_proj_swa=w_proj_swa, w_out=w_out, norm2_w=norm2_w, w_up=w_up, w_down=w_down, final_norm_w=final_norm_w, loss_target=loss_target, m_meta_tokens=m_meta_tokens, m_norm1_w=m_norm1_w, m_w_in=m_w_in, m_gdn_conv_w=m_gdn_conv_w, m_gdn_a_log=m_gdn_a_log, m_gdn_dt_bias=m_gdn_dt_bias, m_gdn_norm_w=m_gdn_norm_w, m_ssd_conv_w=m_ssd_conv_w, m_ssd_conv_b=m_ssd_conv_b, m_ssd_dt_bias=m_ssd_dt_bias, m_ssd_a_log=m_ssd_a_log, m_ssd_d=m_ssd_d, m_ssd_norm_w=m_ssd_norm_w, m_swa_sinks=m_swa_sinks, m_w_proj_gdn=m_w_proj_gdn, m_w_proj_ssd=m_w_proj_ssd, m_w_proj_swa=m_w_proj_swa, m_w_out=m_w_out, m_norm2_w=m_norm2_w, m_w_up=m_w_up, m_w_down=m_w_down, m_final_norm_w=m_final_norm_w, v_meta_tokens=v_meta_tokens, v_norm1_w=v_norm1_w, v_w_in=v_w_in, v_gdn_conv_w=v_gdn_conv_w, v_gdn_a_log=v_gdn_a_log, v_gdn_dt_bias=v_gdn_dt_bias, v_gdn_norm_w=v_gdn_norm_w, v_ssd_conv_w=v_ssd_conv_w, v_ssd_conv_b=v_ssd_conv_b, v_ssd_dt_bias=v_ssd_dt_bias, v_ssd_a_log=v_ssd_a_log, v_ssd_d=v_ssd_d, v_ssd_norm_w=v_ssd_norm_w, v_swa_sinks=v_swa_sinks, v_w_proj_gdn=v_w_proj_gdn, v_w_proj_ssd=v_w_proj_ssd, v_w_proj_swa=v_w_proj_swa, v_w_out=v_w_out, v_norm2_w=v_norm2_w, v_w_up=v_w_up, v_w_down=v_w_down, v_final_norm_w=v_final_norm_w)
    weights = {n: given[n] for n in TWIN_WEIGHTS}
    shared = {n: given[n] for n in SHARED_INPUTS}
    per_example = {n: given[n] for n in ['x']}
    grad_fn = _jax.value_and_grad(_loss, argnums=(0, 1))

    def one_microbatch(ex, loss_target):
        ex = dict(ex)
        diff = ex.pop(TWIN_DIFF_INPUT)
        return grad_fn(weights, diff, {**shared, **ex}, loss_target)

    if N_MICROBATCH == 1:
        loss, (grad_w, grad_x) = one_microbatch(per_example, given["loss_target"])
    else:
        def body(carry, xs):
            loss_sum, grad_sum = carry
            l_k, (gw_k, gx_k) = one_microbatch(xs[0], xs[1])
            with _jax.named_scope("update"):
                return (loss_sum + l_k, _jax.tree.map(_jnp.add, grad_sum, gw_k)), gx_k

        init = (_jnp.zeros((), _jnp.float32), _jax.tree.map(_jnp.zeros_like, weights))
        (loss, grad_w), grad_x = _jax.lax.scan(body, init, (per_example, given["loss_target"]))
    with _jax.named_scope("update"):
        delta_w, new_m, new_v = {}, {}, {}
        for n in TWIN_WEIGHTS:
            delta_w[n], new_m[n], new_v[n] = _adamw(weights[n], grad_w[n], given["m_" + n], given["v_" + n])
    return (loss, grad_x, *[grad_w[n] for n in TWIN_WEIGHTS], *[delta_w[n] for n in TWIN_WEIGHTS],
            *[new_m[n] for n in TWIN_WEIGHTS], *[new_v[n] for n in TWIN_WEIGHTS])
```

```python
import functools

import jax
import jax.numpy as jnp
from jax import lax
from jax.experimental import pallas as pl
from jax.experimental.pallas import tpu as pltpu

f32 = jnp.float32
bf16 = jnp.bfloat16
HI = lax.Precision.HIGHEST

D_MODEL = 1024
N_META = 16
PAD = 112
HEAD_ROWS = PAD + N_META
RMS_EPS = 1e-6
L2_EPS = 1e-6
D_FF = 4 * D_MODEL

GDN_HEADS = 8
GDN_D = 128
GDN_CHUNK = 64
SSD_HEADS = 16
SSD_P = 64
SSD_GROUPS = 4
SSD_HPG = 4
SSD_N = 128
SSD_CHUNK = 128
SWA_Q_HEADS = 16
SWA_KV_HEADS = 4
SWA_REP = 4
SWA_D = 64
SWA_W = 128

C_GATE = 0
C_GQ, C_GK, C_GV, C_GG = 3072, 4096, 5120, 6144
C_SZ = 7168
C_SX, C_SB, C_SC = 8192, 9216, 9728
C_WQ, C_WK, C_WV = 10240, 11264, 11520
C_BA = 11776
C_SDT = 11904
IN_WP = 12032
IN_W = 11808

ADAM_LR, ADAM_B1, ADAM_B2, ADAM_EPS, ADAM_WD, ADAM_STEP = 0.001, 0.9, 0.999, 1e-08, 0.01, 10

VMEM_LIMIT = 56 * 1024 * 1024

NN = (((1,), (0,)), ((), ()))
NT = (((1,), (1,)), ((), ()))
TN = (((0,), (0,)), ((), ()))


def _dot(a, b, dims=NN):
    return lax.dot_general(a.astype(bf16), b.astype(bf16), dims, preferred_element_type=f32)


def _dotx(a, b, dims=NN):
    return lax.dot_general(a, b, dims, preferred_element_type=f32, precision=HI)


def _iota(shape, axis):
    return lax.broadcasted_iota(jnp.int32, shape, axis)


def _softplus(x):
    return jnp.maximum(x, 0.0) + jnp.log1p(jnp.exp(-jnp.abs(x)))


def _silu(x):
    return x * jax.nn.sigmoid(x)


def _params(sem):
    return pltpu.CompilerParams(dimension_semantics=sem, vmem_limit_bytes=VMEM_LIMIT)


@functools.partial(jax.custom_vjp, nondiff_argnums=(1,))
def _window(x_ext, off):
    n = x_ext.shape[0] - 8
    if off == 8:
        return x_ext[8:]
    return pltpu.roll(x_ext, 8 - off, 0)[8:]


def _window_fwd(x_ext, off):
    return _window(x_ext, off), None


def _window_bwd(off, _, g):
    n, w = g.shape
    g_ext = jnp.concatenate([jnp.zeros((8, w), g.dtype), g], axis=0)
    if off == 8:
        return (g_ext,)
    return (pltpu.roll(g_ext, n + off, 0),)


_window.defvjp(_window_fwd, _window_bwd)


def _conv4(x, halo, taps):
    x_ext = jnp.concatenate([halo, x], axis=0)
    y = taps[3] * x
    for j in range(3):
        y = y + taps[j] * _window(x_ext, 5 + j)
    return y


def _blockinv_impl(a):
    n = a.shape[0]
    ri, ci = _iota((n, n), 0), _iota((n, n), 1)
    t = (ri == ci).astype(f32)
    k = 0
    while (1 << k) < n:
        sel = ((ri >> (k + 1)) == (ci >> (k + 1))) & (((ri >> k) & 1) == 1) & (((ci >> k) & 1) == 0)
        o = jnp.where(sel, a, 0.0)
        t = t - _dotx(_dotx(t, o), t)
        k += 1
    return t


@jax.custom_vjp
def _blockinv(a):
    return _blockinv_impl(a)


def _blockinv_fwd(a):
    t = _blockinv_impl(a)
    return t, t


def _blockinv_bwd(t, dt):
    return (-_dotx(_dotx(t, dt, TN), t, NT),)


_blockinv.defvjp(_blockinv_fwd, _blockinv_bwd)


def _gdn_chunk(xq, xk, xv, hq, hk, hv, gate, ba, tq, tk, tv, alog, dtb, nw, s, *, head, row0):
    c = GDN_CHUNK
    q = _silu(_conv4(xq, hq, tq))
    k = _silu(_conv4(xk, hk, tk))
    v = _silu(_conv4(xv, hv, tv))
    q = q * lax.rsqrt(jnp.sum(q * q, axis=1, keepdims=True) + L2_EPS) * (GDN_D ** -0.5)
    k = k * lax.rsqrt(jnp.sum(k * k, axis=1, keepdims=True) + L2_EPS)

    valid = (row0 + _iota((c, 1), 0)) >= PAD
    beta16 = jax.nn.sigmoid(ba)
    g16 = -jnp.exp(alog) * _softplus(ba + dtb)
    sel_b = (_iota((128, 128), 0) == head).astype(f32)
    sel_g = (_iota((128, 128), 0) == head + 8).astype(f32)
    sel_g64 = (_iota((128, c), 0) == head + 8).astype(f32)
    beta = jnp.where(valid, _dotx(beta16, sel_b), 0.0)
    g = jnp.where(valid, _dotx(g16, sel_g), 0.0)
    g64 = jnp.where(valid, _dotx(g16, sel_g64), 0.0)

    ri, ci = _iota((c, c), 0), _iota((c, c), 1)
    incl = ci <= ri
    gam = _dotx(incl.astype(f32), g)
    gam_i = _dotx(incl.astype(f32), g64)
    gam_j = _dotx(jnp.ones((c, c), f32), jnp.where(ri <= ci, g64, 0.0))
    decay = jnp.where(incl, jnp.exp(jnp.where(incl, gam_i - gam_j, 0.0)), 0.0)

    kb = k * beta
    a = jnp.where(ci < ri, _dot(kb, k, NT) * decay, 0.0)
    t = _blockinv(a)
    egam = jnp.exp(gam)
    u = _dotx(t, v * beta)
    w = _dotx(t, kb * egam)
    attn = _dot(q, k, NT) * decay
    gl = jnp.sum(g, axis=0, keepdims=True)
    kt = k * jnp.exp(gl - gam)
    v_new = u - _dot(w, s)
    o = _dot(q * egam, s) + _dot(attn, v_new)
    s_out = s * jnp.exp(gl) + _dot(kt, v_new, TN)

    y = o * lax.rsqrt(jnp.mean(o * o, axis=1, keepdims=True) + RMS_EPS) * nw * _silu(gate)
    return y, s_out


def _gdn_specs(hb, nc, rev):
    w = hb * GDN_D
    cw = D_MODEL // w

    def cidx(c):
        return (nc - 1 - c) if rev else c

    def col(base):
        return pl.BlockSpec((GDN_CHUNK, w), lambda h, c: (cidx(c), base // w + h))

    def halo(base):
        return pl.BlockSpec((8, w), lambda h, c: (jnp.maximum(cidx(c) * (GDN_CHUNK // 8) - 1, 0), base // w + h))

    def taps(base):
        return pl.BlockSpec((8, w), lambda h, c: (0, base // w + h))

    ba = pl.BlockSpec((GDN_CHUNK, 128), lambda h, c: (cidx(c), C_BA // 128))
    row = pl.BlockSpec((8, 128), lambda h, c: (0, 0))
    y = pl.BlockSpec((GDN_CHUNK, w), lambda h, c: (cidx(c), h))
    st = pl.BlockSpec((1, hb, GDN_D, GDN_D), lambda h, c: (cidx(c), h, 0, 0))
    in_specs = [col(C_GQ), col(C_GK), col(C_GV), halo(C_GQ), halo(C_GK), halo(C_GV), col(C_GG), ba,
                taps(0), taps(1024), taps(2048), row, row, row]
    return in_specs, y, st, taps, row, col, ba


def _gdn_load(refs, i, first):
    xq, xk, xv, hq, hk, hv, gate, ba, tq, tk, tv, alog, dtb, nw = refs
    sl = slice(i * GDN_D, (i + 1) * GDN_D)

    def halo(r):
        return jnp.where(first, 0.0, r[:, sl])

    def taps(r):
        return tuple(r[j:j + 1, sl] for j in range(4))

    return (xq[:, sl], xk[:, sl], xv[:, sl], halo(hq), halo(hk), halo(hv), gate[:, sl], ba[...],
            taps(tq), taps(tk), taps(tv), alog[0:1, :], dtb[0:1, :], nw[0:1, :])


def gdn_fwd(u, conv_w8, alog8, dtb8, nw8, *, hb=2):
    t_rows = u.shape[0]
    nc = t_rows // GDN_CHUNK
    in_specs, y_spec, st_spec, *_ = _gdn_specs(hb, nc, False)

    def body(*refs):
        ins, (y_ref, st_ref), (s_scr,) = refs[:14], refs[14:16], refs[16:]
        hblk, c = pl.program_id(0), pl.program_id(1)

        @pl.when(c == 0)
        def _():
            s_scr[...] = jnp.zeros_like(s_scr)

        for i in range(hb):
            s = s_scr[i]
            st_ref[0, i] = s
            y, s_new = _gdn_chunk(*_gdn_load(ins, i, c == 0), s, head=hblk * hb + i, row0=c * GDN_CHUNK)
            y_ref[:, i * GDN_D:(i + 1) * GDN_D] = y.astype(bf16)
            s_scr[i] = s_new

    return pl.pallas_call(
        body, name="gdn_fwd", grid=(GDN_HEADS // hb, nc),
        in_specs=in_specs, out_specs=(y_spec, st_spec),
        out_shape=(jax.ShapeDtypeStruct((t_rows, D_MODEL), bf16),
                   jax.ShapeDtypeStruct((nc, GDN_HEADS, GDN_D, GDN_D), f32)),
        scratch_shapes=[pltpu.VMEM((hb, GDN_D, GDN_D), f32)],
        compiler_params=_params(("arbitrary", "arbitrary")),
    )(u, u, u, u, u, u, u, u, conv_w8, conv_w8, conv_w8, alog8, dtb8, nw8)


def gdn_bwd(u, conv_w8, alog8, dtb8, nw8, states, dy, *, hb=2):
    t_rows = u.shape[0]
    nc = t_rows // GDN_CHUNK
    w = hb * GDN_D
    in_specs, y_spec, st_spec, taps, row, col, ba = _gdn_specs(hb, nc, True)
    nhb = GDN_HEADS // hb

    def body(*refs):
        ins, st_ref, dy_ref = refs[:14], refs[14], refs[15]
        dq_ref, dk_ref, dv_ref, dgate_ref, dba_ref, dtq_ref, dtk_ref, dtv_ref, dalog_ref, ddtb_ref, dnw_ref = refs[16:27]
        ds_scr, dh_scr = refs[27:]
        hblk, cc = pl.program_id(0), pl.program_id(1)
        c = nc - 1 - cc

        @pl.when(cc == 0)
        def _():
            ds_scr[...] = jnp.zeros_like(ds_scr)
            dh_scr[...] = jnp.zeros_like(dh_scr)
            dtq_ref[...] = jnp.zeros_like(dtq_ref)
            dtk_ref[...] = jnp.zeros_like(dtk_ref)
            dtv_ref[...] = jnp.zeros_like(dtv_ref)

        @pl.when((cc == 0) & (hblk == 0))
        def _():
            dalog_ref[...] = jnp.zeros_like(dalog_ref)
            ddtb_ref[...] = jnp.zeros_like(ddtb_ref)
            dnw_ref[...] = jnp.zeros_like(dnw_ref)

        dba_acc = jnp.zeros((GDN_CHUNK, 128), f32)
        for i in range(hb):
            sl = slice(i * GDN_D, (i + 1) * GDN_D)
            args = _gdn_load(ins, i, c == 0)
            s = st_ref[0, i]
            fn = functools.partial(_gdn_chunk, head=hblk * hb + i, row0=c * GDN_CHUNK)
            _, vjp = jax.vjp(fn, *args, s)
            (dxq, dxk, dxv, dhq, dhk, dhv, dgate, dba, dtq, dtk, dtv, dalog, ddtb, dnw, ds) = vjp(
                (dy_ref[:, sl].astype(f32), ds_scr[i]))
            ds_scr[i] = ds
            zeros = jnp.zeros((GDN_CHUNK - 8, GDN_D), f32)
            for j, (dx, dh, out) in enumerate(((dxq, dhq, dq_ref), (dxk, dhk, dk_ref), (dxv, dhv, dv_ref))):
                out[:, sl] = (dx + jnp.concatenate([zeros, dh_scr[j, :, sl]], axis=0)).astype(bf16)
                dh_scr[j, :, sl] = dh
            dgate_ref[:, sl] = dgate.astype(bf16)
            dba_acc = dba_acc + dba
            for dt_ref, dtaps in ((dtq_ref, dtq), (dtk_ref, dtk), (dtv_ref, dtv)):
                for j in range(4):
                    dt_ref[j:j + 1, sl] += dtaps[j]
            dalog_ref[0:1, :] += dalog
            ddtb_ref[0:1, :] += ddtb
            dnw_ref[0:1, :] += dnw
        dba_ref[0] = dba_acc

    def colr(base):
        return pl.BlockSpec((GDN_CHUNK, w), lambda h, c: (nc - 1 - c, base // w + h))

    out_specs = (colr(0), colr(0), colr(0), colr(0),
                 pl.BlockSpec((1, GDN_CHUNK, 128), lambda h, c: (h, nc - 1 - c, 0)),
                 taps(0), taps(0), taps(0), row, row, row)
    out_shape = (jax.ShapeDtypeStruct((t_rows, D_MODEL), bf16),) * 4 + (
        jax.ShapeDtypeStruct((nhb, t_rows, 128), f32),
        jax.ShapeDtypeStruct((8, D_MODEL), f32), jax.ShapeDtypeStruct((8, D_MODEL), f32),
        jax.ShapeDtypeStruct((8, D_MODEL), f32),
        jax.ShapeDtypeStruct((8, 128), f32), jax.ShapeDtypeStruct((8, 128), f32), jax.ShapeDtypeStruct((8, 128), f32))
    return pl.pallas_call(
        body, name="gdn_bwd", grid=(nhb, nc),
        in_specs=in_specs + [st_spec, y_spec], out_specs=out_specs, out_shape=out_shape,
        scratch_shapes=[pltpu.VMEM((hb, GDN_D, GDN_D), f32), pltpu.VMEM((3, 8, w), f32)],
        compiler_params=_params(("arbitrary", "arbitrary")),
    )(u, u, u, u, u, u, u, u, conv_w8, conv_w8, conv_w8, alog8, dtb8, nw8, states, dy)


def _ssd_chunk(xs_r, b_r, c_r, hx, hbm, hcm, z, dt, tx, tb, tc, bx, bb, bc, dtb, alog, dsk, nw, h, *, group, row0):
    n = SSD_CHUNK
    valid = (row0 + _iota((n, 1), 0)) >= PAD
    xs = jnp.where(valid, _silu(_conv4(xs_r, hx, tx) + bx), 0.0)
    bm = jnp.where(valid, _silu(_conv4(b_r, hbm, tb) + bb), 0.0)
    cm = jnp.where(valid, _silu(_conv4(c_r, hcm, tc) + bc), 0.0)
    dtp16 = _softplus(dt + dtb)
    adt16 = -jnp.exp(alog) * dtp16
    head0 = group * SSD_HPG
    sel = (_iota((128, 256), 0) == head0 + (_iota((128, 256), 1) >> 6)).astype(f32)
    dtp = jnp.where(valid, _dotx(dtp16, sel), 0.0)
    adt = jnp.where(valid, _dotx(adt16, sel), 0.0)
    dlane = _dotx(jnp.broadcast_to(dsk, (n, 128)), sel)

    ri, ci = _iota((n, n), 0), _iota((n, n), 1)
    incl = ci <= ri
    inclf = incl.astype(f32)
    acum = _dotx(inclf, adt)
    al = jnp.sum(adt, axis=0, keepdims=True)
    xdt = xs * dtp
    cb = _dot(cm, bm, NT)
    lane_r = _iota((1, 256), 1) >> 6
    y = _dot(cm, h) * jnp.exp(acum) + dlane * xs
    for r in range(SSD_HPG):
        selr = (_iota((128, n), 0) == head0 + r).astype(f32)
        ab = jnp.where(valid, _dotx(adt16, selr), 0.0)
        ai = _dotx(inclf, ab)
        aj = _dotx(jnp.ones((n, n), f32), jnp.where(ri <= ci, ab, 0.0))
        lm = jnp.where(incl, jnp.exp(jnp.where(incl, ai - aj, 0.0)), 0.0)
        y = y + _dot(cb * lm, jnp.where(lane_r == r, xdt, 0.0))
    h_out = h * jnp.exp(al) + _dot(bm, jnp.exp(al - acum) * xdt, TN)
    y = y * _silu(z)
    y = y * lax.rsqrt(jnp.mean(y * y, axis=1, keepdims=True) + RMS_EPS) * nw
    return y, h_out


def _ssd_specs(nc, rev):
    n = SSD_CHUNK

    def cidx(c):
        return (nc - 1 - c) if rev else c

    def col(base, w):
        return pl.BlockSpec((n, w), lambda g, c: (cidx(c), base // w + g))

    def halo(base, w):
        return pl.BlockSpec((8, w), lambda g, c: (jnp.maximum(cidx(c) * (n // 8) - 1, 0), base // w + g))

    def taps(base, w):
        return pl.BlockSpec((8, w), lambda g, c: (0, base // w + g))

    dt = pl.BlockSpec((n, 128), lambda g, c: (cidx(c), C_SDT // 128))
    row = pl.BlockSpec((8, 128), lambda g, c: (0, 0))
    in_specs = [col(C_SX, 256), col(C_SB, 128), col(C_SC, 128), halo(C_SX, 256), halo(C_SB, 128), halo(C_SC, 128),
                col(C_SZ, 256), dt, taps(0, 256), taps(1024, 128), taps(1536, 128), row, row, row, taps(0, 256)]
    y = pl.BlockSpec((n, 256), lambda g, c: (cidx(c), g))
    st = pl.BlockSpec((1, SSD_N, 256), lambda g, c: (cidx(c), 0, g))
    return in_specs, y, st, col, taps, row


def _ssd_load(refs, first):
    xs, bm, cm, hx, hbm, hcm, z, dt, tx, tb, tc, dtb, alog, dsk, nw = refs

    def halo(r):
        return jnp.where(first, 0.0, r[...])

    def taps(r):
        return tuple(r[j:j + 1, :] for j in range(4))

    return (xs[...], bm[...], cm[...], halo(hx), halo(hbm), halo(hcm), z[...], dt[...], taps(tx), taps(tb), taps(tc),
            tx[4:5, :], tb[4:5, :], tc[4:5, :], dtb[0:1, :], alog[0:1, :], dsk[0:1, :], nw[0:1, :])


def ssd_fwd(u, conv_w8, dtb8, alog8, d8, nw8):
    t_rows = u.shape[0]
    nc = t_rows // SSD_CHUNK
    in_specs, y_spec, st_spec, *_ = _ssd_specs(nc, False)

    def body(*refs):
        ins, (y_ref, st_ref), (h_scr,) = refs[:15], refs[15:17], refs[17:]
        g, c = pl.program_id(0), pl.program_id(1)

        @pl.when(c == 0)
        def _():
            h_scr[...] = jnp.zeros_like(h_scr)

        h = h_scr[...]
        st_ref[0] = h
        y, h_new = _ssd_chunk(*_ssd_load(ins, c == 0), h, group=g, row0=c * SSD_CHUNK)
        y_ref[...] = y.astype(bf16)
        h_scr[...] = h_new

    return pl.pallas_call(
        body, name="ssd_fwd", grid=(SSD_GROUPS, nc), in_specs=in_specs, out_specs=(y_spec, st_spec),
        out_shape=(jax.ShapeDtypeStruct((t_rows, D_MODEL), bf16), jax.ShapeDtypeStruct((nc, SSD_N, D_MODEL), f32)),
        scratch_shapes=[pltpu.VMEM((SSD_N, 256), f32)],
        compiler_params=_params(("arbitrary", "arbitrary")),
    )(u, u, u, u, u, u, u, u, conv_w8, conv_w8, conv_w8, dtb8, alog8, d8, nw8)


def ssd_bwd(u, conv_w8, dtb8, alog8, d8, nw8, states, dy):
    t_rows = u.shape[0]
    nc = t_rows // SSD_CHUNK
    n = SSD_CHUNK
    in_specs, y_spec, st_spec, col, taps, row = _ssd_specs(nc, True)

    def body(*refs):
        ins, st_ref, dy_ref = refs[:15], refs[15], refs[16]
        (dz_ref, dxs_ref, db_ref, dc_ref, ddt_ref, dtx_ref, dtb_ref, dtc_ref, ddtb_ref, dalog_ref, ddsk_ref,
         dnw_ref) = refs[17:29]
        dh_scr, hx_scr, hb_scr, hc_scr = refs[29:]
        g, cc = pl.program_id(0), pl.program_id(1)
        c = nc - 1 - cc

        @pl.when(cc == 0)
        def _():
            for r in (dh_scr, hx_scr, hb_scr, hc_scr, dtx_ref, dtb_ref, dtc_ref, dnw_ref):
                r[...] = jnp.zeros_like(r)

        @pl.when((cc == 0) & (g == 0))
        def _():
            for r in (ddtb_ref, dalog_ref, ddsk_ref):
                r[...] = jnp.zeros_like(r)

        args = _ssd_load(ins, c == 0)
        fn = functools.partial(_ssd_chunk, group=g, row0=c * n)
        _, vjp = jax.vjp(fn, *args, st_ref[0])
        (dxs, dbm, dcm, dhx, dhb, dhc, dz, ddt, dtx, dtb, dtc, dbx, dbb, dbc, ddtb, dalog, ddsk, dnw, dh) = vjp(
            (dy_ref[...].astype(f32), dh_scr[...]))
        dh_scr[...] = dh
        for dx, dhalo, scr, out in ((dxs, dhx, hx_scr, dxs_ref), (dbm, dhb, hb_scr, db_ref), (dcm, dhc, hc_scr, dc_ref)):
            zeros = jnp.zeros((n - 8, dx.shape[1]), f32)
            out[...] = (dx + jnp.concatenate([zeros, scr[...]], axis=0)).astype(bf16)
            scr[...] = dhalo
        dz_ref[...] = dz.astype(bf16)
        ddt_ref[0] = ddt
        for ref, dtaps, dbias in ((dtx_ref, dtx, dbx), (dtb_ref, dtb, dbb), (dtc_ref, dtc, dbc)):
            for j in range(4):
                ref[j:j + 1, :] += dtaps[j]
            ref[4:5, :] += dbias
        ddtb_ref[0:1, :] += ddtb
        dalog_ref[0:1, :] += dalog
        ddsk_ref[0:1, :] += ddsk
        dnw_ref[0:1, :] += dnw

    def out_col(w):
        return pl.BlockSpec((n, w), lambda g, c: (nc - 1 - c, g))

    out_specs = (out_col(256), out_col(256), out_col(128), out_col(128),
                 pl.BlockSpec((1, n, 128), lambda g, c: (g, nc - 1 - c, 0)),
                 taps(0, 256), taps(0, 128), taps(0, 128), row, row, row, taps(0, 256))
    out_shape = (jax.ShapeDtypeStruct((t_rows, D_MODEL), bf16), jax.ShapeDtypeStruct((t_rows, D_MODEL), bf16),
                 jax.ShapeDtypeStruct((t_rows, 512), bf16), jax.ShapeDtypeStruct((t_rows, 512), bf16),
                 jax.ShapeDtypeStruct((SSD_GROUPS, t_rows, 128), f32),
                 jax.ShapeDtypeStruct((8, D_MODEL), f32), jax.ShapeDtypeStruct((8, 512), f32),
                 jax.ShapeDtypeStruct((8, 512), f32),
                 jax.ShapeDtypeStruct((8, 128), f32), jax.ShapeDtypeStruct((8, 128), f32),
                 jax.ShapeDtypeStruct((8, 128), f32), jax.ShapeDtypeStruct((8, D_MODEL), f32))
    return pl.pallas_call(
        body, name="ssd_bwd", grid=(SSD_GROUPS, nc), in_specs=in_specs + [st_spec, y_spec],
        out_specs=out_specs, out_shape=out_shape,
        scratch_shapes=[pltpu.VMEM((SSD_N, 256), f32), pltpu.VMEM((8, 256), f32), pltpu.VMEM((8, 128), f32),
                        pltpu.VMEM((8, 128), f32)],
        compiler_params=_params(("arbitrary", "arbitrary")),
    )(u, u, u, u, u, u, u, u, conv_w8, conv_w8, conv_w8, dtb8, alog8, d8, nw8, states, dy)


NEG = -1e30


def _swa_block(q4, kc, kp, km, vc, vp, vm, sink16, *, kvh, n):
    w = SWA_W
    ri, ci = _iota((w, w), 0), _iota((w, w), 1)
    causal = ci <= ri
    m_cur = causal & ((n >= 1) | ((ci >= PAD) & (ri >= PAD)))
    m_prev = (n >= 2) & (ci > ri)
    m_meta = (n >= 1) & (ci >= PAD)
    lane = _iota((1, 128), 1)
    outs = []
    for r in range(SWA_REP):
        q = q4[r] * (SWA_D ** -0.5)
        sc = jnp.where(m_cur, _dot(q, kc, NT), NEG)
        sp = jnp.where(m_prev, _dot(q, kp, NT), NEG)
        sm = jnp.where(m_meta, _dot(q, km, NT), NEG)
        sink = jnp.sum(jnp.where(lane == kvh * SWA_REP + r, sink16, 0.0), axis=1, keepdims=True)
        mx = jnp.maximum(jnp.maximum(jnp.max(sc, axis=1, keepdims=True), jnp.max(sp, axis=1, keepdims=True)),
                         jnp.maximum(jnp.max(sm, axis=1, keepdims=True), sink))
        mx = lax.stop_gradient(mx)
        ec, ep, em = jnp.exp(sc - mx), jnp.exp(sp - mx), jnp.exp(sm - mx)
        den = (jnp.sum(ec, axis=1, keepdims=True) + jnp.sum(ep, axis=1, keepdims=True)
               + jnp.sum(em, axis=1, keepdims=True) + jnp.exp(sink - mx))
        outs.append((_dot(ec, vc) + _dot(ep, vp) + _dot(em, vm)) / den)
    return tuple(outs)


def _swa_specs(nb, rev):
    def bidx(n):
        return (nb - 1 - n) if rev else n

    q = pl.BlockSpec((SWA_REP, SWA_W, SWA_D), lambda h, n: (h, bidx(n), 0))
    cur = pl.BlockSpec((1, SWA_W, SWA_D), lambda h, n: (h, bidx(n), 0))
    prev = pl.BlockSpec((1, SWA_W, SWA_D), lambda h, n: (h, jnp.maximum(bidx(n) - 1, 0), 0))
    meta = pl.BlockSpec((1, SWA_W, SWA_D), lambda h, n: (h, 0, 0))
    row = pl.BlockSpec((8, 128), lambda h, n: (0, 0))
    return [q, cur, prev, meta, cur, prev, meta, row], q, cur, row


def swa_fwd(q, k, v, sink8):
    t_rows = q.shape[1]
    nb = t_rows // SWA_W
    in_specs, q_spec, _, _ = _swa_specs(nb, False)

    def body(q_ref, kc, kp, km, vc, vp, vm, sink_ref, o_ref):
        outs = _swa_block(q_ref[...], kc[0], kp[0], km[0], vc[0], vp[0], vm[0], sink_ref[0:1, :],
                          kvh=pl.program_id(0), n=pl.program_id(1))
        for r in range(SWA_REP):
            o_ref[r] = outs[r].astype(bf16)

    return pl.pallas_call(
        body, name="swa_fwd", grid=(SWA_KV_HEADS, nb), in_specs=in_specs, out_specs=q_spec,
        out_shape=jax.ShapeDtypeStruct(q.shape, bf16),
        compiler_params=_params(("arbitrary", "arbitrary")),
    )(q, k, k, k, v, v, v, sink8)


def swa_bwd(q, k, v, sink8, do):
    t_rows = q.shape[1]
    nb = t_rows // SWA_W
    in_specs, q_spec, kv_spec, row = _swa_specs(nb, True)

    def body(q_ref, kc, kp, km, vc, vp, vm, sink_ref, do_ref, dq_ref, dk_ref, dv_ref, dsink_ref,
             dkp_scr, dvp_scr, dkm_scr, dvm_scr):
        h, nn = pl.program_id(0), pl.program_id(1)
        n = nb - 1 - nn

        @pl.when(nn == 0)
        def _():
            for r in (dkp_scr, dvp_scr, dkm_scr, dvm_scr):
                r[...] = jnp.zeros_like(r)

        @pl.when((nn == 0) & (h == 0))
        def _():
            dsink_ref[...] = jnp.zeros_like(dsink_ref)

        fn = functools.partial(_swa_block, kvh=h, n=n)
        _, vjp = jax.vjp(fn, q_ref[...], kc[0], kp[0], km[0], vc[0], vp[0], vm[0], sink_ref[0:1, :])
        dq, dkc, dkp, dkm, dvc, dvp, dvm, dsink = vjp(tuple(do_ref[r].astype(f32) for r in range(SWA_REP)))
        dq_ref[...] = dq.astype(bf16)
        dkm_scr[...] += dkm
        dvm_scr[...] += dvm
        first = n == 0
        dk_ref[0] = (dkc + dkp_scr[...] + jnp.where(first, dkm_scr[...], 0.0)).astype(bf16)
        dv_ref[0] = (dvc + dvp_scr[...] + jnp.where(first, dvm_scr[...], 0.0)).astype(bf16)
        dkp_scr[...] = dkp
        dvp_scr[...] = dvp
        dsink_ref[0:1, :] += dsink

    kv_shape = jax.ShapeDtypeStruct(k.shape, bf16)
    return pl.pallas_call(
        body, name="swa_bwd", grid=(SWA_KV_HEADS, nb), in_specs=in_specs + [q_spec],
        out_specs=(q_spec, kv_spec, kv_spec, row),
        out_shape=(jax.ShapeDtypeStruct(q.shape, bf16), kv_shape, kv_shape, jax.ShapeDtypeStruct((8, 128), f32)),
        scratch_shapes=[pltpu.VMEM((SWA_W, SWA_D), f32)] * 4,
        compiler_params=_params(("arbitrary", "arbitrary")),
    )(q, k, k, k, v, v, v, sink8, do)


def _tile(dim, prefs):
    for p in prefs:
        if dim % p == 0:
            return p
    return dim


def mm(a, b, *, out_dtype, name, resid=None):
    m, k = a.shape
    n = b.shape[1]
    tm = _tile(m, (1408, 1024, 512, 384, 256, 128))
    tn = _tile(n, (512, 256, 128))
    tk = _tile(k, (1408, 1024, 512, 384, 256, 128))
    nk = k // tk

    def body(*refs):
        if resid is None:
            a_ref, b_ref, o_ref, acc = refs
        else:
            a_ref, b_ref, r_ref, o_ref, acc = refs
        kk = pl.program_id(2)

        @pl.when(kk == 0)
        def _():
            acc[...] = jnp.zeros_like(acc)

        acc[...] += _dot(a_ref[...], b_ref[...])

        @pl.when(kk == nk - 1)
        def _():
            o = acc[...]
            if resid is not None:
                o = o + r_ref[...]
            o_ref[...] = o.astype(out_dtype)

    in_specs = [pl.BlockSpec((tm, tk), lambda i, j, kk: (i, kk)), pl.BlockSpec((tk, tn), lambda i, j, kk: (kk, j))]
    args = [a, b]
    if resid is not None:
        in_specs.append(pl.BlockSpec((tm, tn), lambda i, j, kk: (i, j)))
        args.append(resid)
    return pl.pallas_call(
        body, name=name, grid=(m // tm, n // tn, nk), in_specs=in_specs,
        out_specs=pl.BlockSpec((tm, tn), lambda i, j, kk: (i, j)),
        out_shape=jax.ShapeDtypeStruct((m, n), out_dtype),
        scratch_shapes=[pltpu.VMEM((tm, tn), f32)],
        compiler_params=_params(("parallel", "parallel", "arbitrary")),
    )(*args)


def _rows(t_rows):
    return _tile(t_rows, (384, 256, 128))


def _rmsnorm(h, w):
    return h * lax.rsqrt(jnp.mean(h * h, axis=1, keepdims=True) + RMS_EPS) * w


def rmsnorm_fwd(h, w8, *, name):
    t_rows, d = h.shape
    tr = _rows(t_rows)

    def body(h_ref, w_ref, o_ref):
        o_ref[...] = _rmsnorm(h_ref[...], w_ref[0:1, :]).astype(bf16)

    blk = pl.BlockSpec((tr, d), lambda i: (i, 0))
    return pl.pallas_call(
        body, name=name, grid=(t_rows // tr,), in_specs=[blk, pl.BlockSpec((8, d), lambda i: (0, 0))], out_specs=blk,
        out_shape=jax.ShapeDtypeStruct((t_rows, d), bf16), compiler_params=_params(("arbitrary",)),
    )(h, w8)


def rmsnorm_bwd(h, w8, dhn, dres, *, name):
    t_rows, d = h.shape
    tr = _rows(t_rows)

    def body(h_ref, w_ref, dhn_ref, dres_ref, dh_ref, dw_ref):
        @pl.when(pl.program_id(0) == 0)
        def _():
            dw_ref[...] = jnp.zeros_like(dw_ref)

        _, vjp = jax.vjp(_rmsnorm, h_ref[...], w_ref[0:1, :])
        dh, dw = vjp(dhn_ref[...])
        dh_ref[...] = dh + dres_ref[...]
        dw_ref[0:1, :] += dw

    blk = pl.BlockSpec((tr, d), lambda i: (i, 0))
    wblk = pl.BlockSpec((8, d), lambda i: (0, 0))
    return pl.pallas_call(
        body, name=name, grid=(t_rows // tr,), in_specs=[blk, wblk, blk, blk], out_specs=(blk, wblk),
        out_shape=(jax.ShapeDtypeStruct((t_rows, d), f32), jax.ShapeDtypeStruct((8, d), f32)),
        compiler_params=_params(("arbitrary",)),
    )(h, w8, dhn, dres)


def _merge(pg, ps, pw, la, lb, lc):
    return jax.nn.sigmoid(la) * pg + jax.nn.sigmoid(lb) * ps + jax.nn.sigmoid(lc) * pw


def _merge_specs(t_rows):
    tr = _rows(t_rows)
    blk = pl.BlockSpec((tr, D_MODEL), lambda i: (i, 0))
    gate = [pl.BlockSpec((tr, D_MODEL), functools.partial(lambda i, j: (i, j), j=C_GATE // D_MODEL + j)) for j in range(3)]
    return tr, blk, gate


def merge_fwd(pg, ps, pw, u):
    t_rows = pg.shape[0]
    tr, blk, gate = _merge_specs(t_rows)

    def body(pg_ref, ps_ref, pw_ref, la, lb, lc, o_ref):
        o_ref[...] = _merge(pg_ref[...], ps_ref[...], pw_ref[...], la[...], lb[...], lc[...]).astype(bf16)

    return pl.pallas_call(
        body, name="merge_fwd", grid=(t_rows // tr,), in_specs=[blk, blk, blk] + gate, out_specs=blk,
        out_shape=jax.ShapeDtypeStruct((t_rows, D_MODEL), bf16), compiler_params=_params(("arbitrary",)),
    )(pg, ps, pw, u, u, u)


def merge_bwd(pg, ps, pw, u, dmerged):
    t_rows = pg.shape[0]
    tr, blk, gate = _merge_specs(t_rows)

    def body(pg_ref, ps_ref, pw_ref, la, lb, lc, dm_ref, dpg_ref, dps_ref, dpw_ref, dl_ref):
        _, vjp = jax.vjp(_merge, pg_ref[...], ps_ref[...], pw_ref[...], la[...], lb[...], lc[...])
        dpg, dps, dpw, dla, dlb, dlc = vjp(dm_ref[...])
        dpg_ref[...] = dpg.astype(bf16)
        dps_ref[...] = dps.astype(bf16)
        dpw_ref[...] = dpw.astype(bf16)
        for j, dl in enumerate((dla, dlb, dlc)):
            dl_ref[:, j * D_MODEL:(j + 1) * D_MODEL] = dl.astype(bf16)

    act = jax.ShapeDtypeStruct((t_rows, D_MODEL), bf16)
    return pl.pallas_call(
        body, name="merge_bwd", grid=(t_rows // tr,), in_specs=[blk, blk, blk] + gate + [blk],
        out_specs=(blk, blk, blk, pl.BlockSpec((tr, 3 * D_MODEL), lambda i: (i, 0))),
        out_shape=(act, act, act, jax.ShapeDtypeStruct((t_rows, 3 * D_MODEL), bf16)),
        compiler_params=_params(("arbitrary",)),
    )(pg, ps, pw, u, u, u, dmerged)


def relu2_fwd(a):
    t_rows, d = a.shape
    tr = _rows(t_rows)

    def body(a_ref, o_ref):
        r = jnp.maximum(a_ref[...], 0.0)
        o_ref[...] = (r * r).astype(bf16)

    blk = pl.BlockSpec((tr, d), lambda i: (i, 0))
    return pl.pallas_call(
        body, name="relu2_fwd", grid=(t_rows // tr,), in_specs=[blk], out_specs=blk,
        out_shape=jax.ShapeDtypeStruct((t_rows, d), bf16), compiler_params=_params(("arbitrary",)),
    )(a)


def relu2_bwd(a, dr):
    t_rows, d = a.shape
    tr = _rows(t_rows)

    def body(a_ref, dr_ref, o_ref):
        o_ref[...] = (dr_ref[...] * 2.0 * jnp.maximum(a_ref[...], 0.0)).astype(bf16)

    blk = pl.BlockSpec((tr, d), lambda i: (i, 0))
    return pl.pallas_call(
        body, name="relu2_bwd", grid=(t_rows // tr,), in_specs=[blk, blk], out_specs=blk,
        out_shape=jax.ShapeDtypeStruct((t_rows, d), bf16), compiler_params=_params(("arbitrary",)),
    )(a, dr)


def loss_head(h, w8, target):
    t_rows, d = h.shape
    tr = HEAD_ROWS

    def loss_fn(hb, w, tgt):
        err = _rmsnorm(hb, w) - tgt
        return 0.5 * jnp.sum(err * err) / d

    def body(h_ref, w_ref, t_ref, loss_ref, dh_ref, dw_ref):
        i = pl.program_id(0)

        @pl.when(i == 0)
        def _():
            loss_ref[...] = jnp.zeros_like(loss_ref)
            dw_ref[...] = jnp.zeros_like(dw_ref)
            dh_ref[...] = jnp.zeros_like(dh_ref)

        @pl.when(i > 0)
        def _():
            val, (dh, dw) = jax.value_and_grad(loss_fn, argnums=(0, 1))(h_ref[...], w_ref[0:1, :], t_ref[...])
            loss_ref[...] += val
            dh_ref[...] = dh
            dw_ref[0:1, :] += dw

    blk = pl.BlockSpec((tr, d), lambda i: (i, 0))
    wblk = pl.BlockSpec((8, d), lambda i: (0, 0))
    return pl.pallas_call(
        body, name="loss_head", grid=(t_rows // tr,),
        in_specs=[blk, wblk, pl.BlockSpec((tr, d), lambda i: (jnp.maximum(i - 1, 0), 0))],
        out_specs=(pl.BlockSpec((8, 128), lambda i: (0, 0)), blk, wblk),
        out_shape=(jax.ShapeDtypeStruct((8, 128), f32), jax.ShapeDtypeStruct((t_rows, d), f32),
                   jax.ShapeDtypeStruct((8, d), f32)),
        compiler_params=_params(("arbitrary",)),
    )(h, w8, target)


def adamw(w, m, v, p_mine, p_sib, row_off, *, name):
    rows, d = w.shape
    tr = _tile(rows, (512, 256, 144, 128, 64, 48, 16, 8))
    assert row_off % tr == 0
    off = row_off // tr
    c1 = 1.0 - ADAM_B1 ** ADAM_STEP
    c2 = 1.0 - ADAM_B2 ** ADAM_STEP

    def body(w_ref, m_ref, v_ref, pa_ref, pb_ref, g_ref, d_ref, mo_ref, vo_ref):
        g = pa_ref[...] + pb_ref[...]
        m_new = ADAM_B1 * m_ref[...] + (1.0 - ADAM_B1) * g
        v_new = ADAM_B2 * v_ref[...] + (1.0 - ADAM_B2) * (g * g)
        g_ref[...] = g
        d_ref[...] = -ADAM_LR * ((m_new / c1) / (jnp.sqrt(v_new / c2) + ADAM_EPS) + ADAM_WD * w_ref[...])
        mo_ref[...] = m_new
        vo_ref[...] = v_new

    blk = pl.BlockSpec((tr, d), lambda i: (i, 0))
    pblk = pl.BlockSpec((tr, d), lambda i: (off + i, 0))
    out = jax.ShapeDtypeStruct((rows, d), f32)
    return pl.pallas_call(
        body, name=name, grid=(rows // tr,), in_specs=[blk, blk, blk, pblk, pblk], out_specs=(blk,) * 4,
        out_shape=(out,) * 4, compiler_params=_params(("arbitrary",)),
    )(w, m, v, p_mine, p_sib)


def reduce4(parts, *, name):
    _, rows, d = parts.shape
    tr = _tile(rows, (512, 256, 128, 64, 8))

    def body(p_ref, o_ref):
        acc = p_ref[0].astype(f32)
        for s in range(1, 4):
            acc = acc + p_ref[s].astype(f32)
        o_ref[...] = acc

    return pl.pallas_call(
        body, name=name, grid=(rows // tr,), in_specs=[pl.BlockSpec((4, tr, d), lambda i: (0, i, 0))],
        out_specs=pl.BlockSpec((tr, d), lambda i: (i, 0)), out_shape=jax.ShapeDtypeStruct((rows, d), f32),
        compiler_params=_params(("arbitrary",)),
    )(parts)


ANY = pl.BlockSpec(memory_space=pl.ANY)
MESH = pl.DeviceIdType.MESH
CHIP_FLIPS = ((0, 1), (1, 0), (1, 1))


def chip_exchange(big, small, *, scatter, name):
    rows = big.shape[-2]

    def body(big_ref, small_ref, obig_ref, osmall_ref, send_sems, recv_sems, local_sems):
        x, y, c = lax.axis_index("x"), lax.axis_index("y"), lax.axis_index("c")
        me = 2 * x + y
        mine = big_ref.at[me] if scatter else big_ref
        local = [pltpu.make_async_copy(mine, obig_ref.at[me], local_sems.at[0]),
                 pltpu.make_async_copy(small_ref, osmall_ref.at[me], local_sems.at[1])]
        for cp in local:
            cp.start()
        sends, recvs = [], []
        for k, (fx, fy) in enumerate(CHIP_FLIPS):
            px = 1 - x if fx else x
            py = 1 - y if fy else y
            chip = 2 * px + py
            for j, (src, dst) in enumerate((((big_ref.at[chip] if scatter else big_ref), obig_ref),
                                            (small_ref, osmall_ref))):
                sems = dict(send_sem=send_sems.at[2 * k + j], recv_sem=recv_sems.at[2 * k + j],
                            device_id=(px, py, c), device_id_type=MESH)
                sends.append(pltpu.make_async_remote_copy(src_ref=src, dst_ref=dst.at[me], **sems))
                recvs.append(pltpu.make_async_remote_copy(src_ref=src, dst_ref=dst.at[chip], **sems))
        for cp in sends:
            cp.start()
        for cp in recvs:
            cp.wait_recv()
        for cp in sends:
            cp.wait_send()
        for cp in local:
            cp.wait()

    return pl.pallas_call(
        body, name=name, in_specs=[ANY, ANY], out_specs=(ANY, ANY),
        out_shape=(jax.ShapeDtypeStruct((4, rows, big.shape[-1]), big.dtype),
                   jax.ShapeDtypeStruct((4,) + small.shape, small.dtype)),
        scratch_shapes=[pltpu.SemaphoreType.DMA((6,)), pltpu.SemaphoreType.DMA((6,)), pltpu.SemaphoreType.DMA((2,))],
        compiler_params=pltpu.CompilerParams(has_side_effects=True),
    )(big, small)


def sibling_swap(big, small, *, name):
    def body(big_ref, small_ref, obig_ref, osmall_ref, send_sems, recv_sems):
        peer = (lax.axis_index("x"), lax.axis_index("y"), 1 - lax.axis_index("c"))
        copies = [pltpu.make_async_remote_copy(src_ref=src, dst_ref=dst, send_sem=send_sems.at[j],
                                               recv_sem=recv_sems.at[j], device_id=peer, device_id_type=MESH)
                  for j, (src, dst) in enumerate(((big_ref, obig_ref), (small_ref, osmall_ref)))]
        for cp in copies:
            cp.start()
        for cp in copies:
            cp.wait_recv()
        for cp in copies:
            cp.wait_send()

    return pl.pallas_call(
        body, name=name, in_specs=[ANY, ANY], out_specs=(ANY, ANY),
        out_shape=(jax.ShapeDtypeStruct(big.shape, big.dtype), jax.ShapeDtypeStruct(small.shape, small.dtype)),
        scratch_shapes=[pltpu.SemaphoreType.DMA((2,)), pltpu.SemaphoreType.DMA((2,))],
        compiler_params=pltpu.CompilerParams(has_side_effects=True),
    )(big, small)


BIG = (
    ("w_in", 5904, 6144), ("w_proj_gdn", 512, 512), ("w_proj_ssd", 512, 512), ("w_proj_swa", 512, 512),
    ("w_out", 512, 512), ("w_up", 2048, 2048), ("w_down", 2048, 2048))
BIG_OFF = {}
_o = 0
for _n, _r, _p in BIG:
    BIG_OFF[_n] = _o
    _o += _p
BIG_ROWS = _o

W_NAMES = ('meta_tokens', 'norm1_w', 'w_in', 'gdn_conv_w', 'gdn_a_log', 'gdn_dt_bias', 'gdn_norm_w', 'ssd_conv_w',
           'ssd_conv_b', 'ssd_dt_bias', 'ssd_a_log', 'ssd_d', 'ssd_norm_w', 'swa_sinks', 'w_proj_gdn', 'w_proj_ssd',
           'w_proj_swa', 'w_out', 'norm2_w', 'w_up', 'w_down', 'final_norm_w')
SMALL_NAMES = tuple(n for n in W_NAMES if n not in BIG_OFF)
SMALL_SHARDED = ("meta_tokens", "gdn_conv_w", "ssd_conv_w")


def _pad_rows(a, rows):
    return jnp.pad(a, ((0, rows - a.shape[0]), (0, 0)))


def _pack_rows(parts, dtype):
    flat = jnp.concatenate([p.reshape(-1).astype(dtype) for p in parts])
    n = -(-flat.shape[0] // 8192) * 8192
    return jnp.pad(flat, (0, n - flat.shape[0])).reshape(-1, D_MODEL)


def _unpack_rows(packed, shapes):
    flat, out, o = packed.reshape(-1), [], 0
    for s in shapes:
        n = 1
        for d in s:
            n *= d
        out.append(flat[o:o + n].reshape(s))
        o += n
    return out


def _split_chips(full, axis):
    s = full.shape
    a = full.reshape(s[:axis] + (4, s[axis] // 4) + s[axis + 1:])
    return jnp.moveaxis(a, axis, 0)


def _join_chips(parts, axis):
    a = jnp.moveaxis(parts, 0, axis)
    s = a.shape
    return a.reshape(s[:axis] + (s[axis] * s[axis + 1],) + s[axis + 2:])


BIG_AXIS = {"w_in": 2, "w_proj_gdn": 1, "w_proj_ssd": 1, "w_proj_swa": 1, "w_out": 1, "w_up": 2, "w_down": 1}


def _w_in_to_padded(w):
    z = jnp.zeros(w.shape[:-1] + (112,), w.dtype)
    return jnp.concatenate([w[..., 8736:11808], w[..., 0:4096], w[..., 4112:7184], w[..., 7200:8736],
                            w[..., 4096:4112], z, w[..., 7184:7200], z], axis=-1)


def _w_in_from_padded(p):
    return jnp.concatenate([p[..., C_GQ:C_SZ], p[..., C_BA:C_BA + 16], p[..., C_SZ:C_WQ], p[..., C_SDT:C_SDT + 16],
                            p[..., C_WQ:C_BA], p[..., 0:C_GQ]], axis=-1)


def _row8(v, lane0=0, width=128):
    return jnp.pad(v[None, :], ((0, 7), (lane0, width - lane0 - v.shape[0])))


def _head_major(a, heads):
    return a.reshape(a.shape[0], heads, SWA_D).transpose(1, 0, 2)


def _from_head_major(a):
    return a.transpose(1, 0, 2).reshape(a.shape[1], -1)


def _layer_fwd(h, p, l):
    tag = f"l{l}"
    hn = rmsnorm_fwd(h, p["n1"], name=f"norm1_fwd_{tag}")
    u = mm(hn, p["w_in"], out_dtype=f32, name=f"mm_in_{tag}")
    yg, stg = gdn_fwd(u, p["gcw"], p["galog"], p["gdtb"], p["gnw"])
    ys, sts = ssd_fwd(u, p["scw"], p["sdtb"], p["salog"], p["sd"], p["snw"])
    qh = _head_major(u[:, C_WQ:C_WK], SWA_Q_HEADS)
    kh = _head_major(u[:, C_WK:C_WV], SWA_KV_HEADS)
    vh = _head_major(u[:, C_WV:C_BA], SWA_KV_HEADS)
    yw = _from_head_major(swa_fwd(qh, kh, vh, p["sink"]))
    pg = mm(yg, p["wpg"], out_dtype=f32, name=f"mm_pg_{tag}")
    ps = mm(ys, p["wps"], out_dtype=f32, name=f"mm_ps_{tag}")
    pw = mm(yw, p["wpw"], out_dtype=f32, name=f"mm_pw_{tag}")
    merged = merge_fwd(pg, ps, pw, u)
    h2 = mm(merged, p["wout"], out_dtype=f32, resid=h, name=f"mm_out_{tag}")
    hn2 = rmsnorm_fwd(h2, p["n2"], name=f"norm2_fwd_{tag}")
    a = mm(hn2, p["wup"], out_dtype=f32, name=f"mm_up_{tag}")
    r = relu2_fwd(a)
    h3 = mm(r, p["wdown"], out_dtype=f32, resid=h2, name=f"mm_down_{tag}")
    saved = dict(h=h, hn=hn, u=u, yg=yg, stg=stg, ys=ys, sts=sts, qh=qh, kh=kh, vh=vh, yw=yw, pg=pg, ps=ps, pw=pw,
                 merged=merged, h2=h2, hn2=hn2, a=a, r=r)
    return h3, saved


def _layer_bwd(dh3, p, s, l):
    tag = f"l{l}"
    g = {}

    def wgrad(act, d, name):
        return mm(act.T, d, out_dtype=bf16, name=f"wg_{name}_{tag}")

    da = relu2_bwd(s["a"], mm(dh3, p["wdown_t"], out_dtype=f32, name=f"dg_down_{tag}"))
    g["w_down"] = wgrad(s["r"], dh3, "down")
    dhn2 = mm(da, p["wup_t"], out_dtype=f32, name=f"dg_up_{tag}")
    g["w_up"] = wgrad(s["hn2"], da, "up")
    dh2, g["norm2_w"] = rmsnorm_bwd(s["h2"], p["n2"], dhn2, dh3, name=f"norm2_bwd_{tag}")
    dmerged = mm(dh2, p["wout_t"], out_dtype=f32, name=f"dg_out_{tag}")
    g["w_out"] = wgrad(s["merged"], dh2, "out")
    dpg, dps, dpw, dgl = merge_bwd(s["pg"], s["ps"], s["pw"], s["u"], dmerged)
    dyg = mm(dpg, p["wpg_t"], out_dtype=f32, name=f"dg_pg_{tag}")
    dys = mm(dps, p["wps_t"], out_dtype=f32, name=f"dg_ps_{tag}")
    dyw = mm(dpw, p["wpw_t"], out_dtype=f32, name=f"dg_pw_{tag}")
    g["w_proj_gdn"] = wgrad(s["yg"], dpg, "pg")
    g["w_proj_ssd"] = wgrad(s["ys"], dps, "ps")
    g["w_proj_swa"] = wgrad(s["yw"], dpw, "pw")

    (dq, dk, dv, dgate, dba, dtq, dtk, dtv, g["gdn_a_log"], g["gdn_dt_bias"], g["gdn_norm_w"]) = gdn_bwd(
        s["u"], p["gcw"], p["galog"], p["gdtb"], p["gnw"], s["stg"], dyg)
    g["gdn_conv_w"] = jnp.concatenate([dtq, dtk, dtv], axis=1)[:4]
    (dz, dxs, dbm, dcm, ddt, dtx, dtb, dtc, g["ssd_dt_bias"], g["ssd_a_log"], g["ssd_d"], g["ssd_norm_w"]) = ssd_bwd(
        s["u"], p["scw"], p["sdtb"], p["salog"], p["sd"], p["snw"], s["sts"], dys)
    dconv = jnp.concatenate([dtx, dtb, dtc], axis=1)
    g["ssd_conv_w"], g["ssd_conv_b"] = dconv[:4], dconv[4]
    dqh, dkh, dvh, g["swa_sinks"] = swa_bwd(s["qh"], s["kh"], s["vh"], p["sink"], _head_major(dyw, SWA_Q_HEADS))
    du = jnp.concatenate([dgl, dq, dk, dv, dgate, dz, dxs, dbm, dcm, _from_head_major(dqh), _from_head_major(dkh),
                          _from_head_major(dvh), dba.sum(0).astype(bf16), ddt.sum(0).astype(bf16)], axis=1)
    dhn = mm(du, p["w_in_t"], out_dtype=f32, name=f"dg_in_{tag}")
    g["w_in"] = wgrad(s["hn"], du, "in")
    dh, g["norm1_w"] = rmsnorm_bwd(s["h"], p["n1"], dhn, dh2, name=f"norm1_bwd_{tag}")
    return dh, g


def kernel(x, meta_tokens, norm1_w, w_in, gdn_conv_w, gdn_a_log, gdn_dt_bias, gdn_norm_w, ssd_conv_w, ssd_conv_b, ssd_dt_bias, ssd_a_log, ssd_d, ssd_norm_w, swa_sinks, w_proj_gdn, w_proj_ssd, w_proj_swa, w_out, norm2_w, w_up, w_down, final_norm_w, loss_target, m_meta_tokens, m_norm1_w, m_w_in, m_gdn_conv_w, m_gdn_a_log, m_gdn_dt_bias, m_gdn_norm_w, m_ssd_conv_w, m_ssd_conv_b, m_ssd_dt_bias, m_ssd_a_log, m_ssd_d, m_ssd_norm_w, m_swa_sinks, m_w_proj_gdn, m_w_proj_ssd, m_w_proj_swa, m_w_out, m_norm2_w, m_w_up, m_w_down, m_final_norm_w, v_meta_tokens, v_norm1_w, v_w_in, v_gdn_conv_w, v_gdn_a_log, v_gdn_dt_bias, v_gdn_norm_w, v_ssd_conv_w, v_ssd_conv_b, v_ssd_dt_bias, v_ssd_a_log, v_ssd_d, v_ssd_norm_w, v_swa_sinks, v_w_proj_gdn, v_w_proj_ssd, v_w_proj_swa, v_w_out, v_norm2_w, v_w_up, v_w_down, v_final_norm_w):
    given = dict(locals())
    depth = norm1_w.shape[0]
    me = 2 * lax.axis_index("x") + lax.axis_index("y")

    wbig = jnp.concatenate([_pad_rows(given[n].reshape(-1, D_MODEL).astype(bf16), p) for n, _, p in BIG])
    wsmall = _pack_rows([given[n] for n in SMALL_SHARDED], f32)
    gbig, gsmall = chip_exchange(wbig, wsmall, scatter=False, name="gather_weights")
    full = {}
    for n, r, _ in BIG:
        parts = gbig[:, BIG_OFF[n]:BIG_OFF[n] + r].reshape((4,) + given[n].shape)
        full[n] = _join_chips(parts, BIG_AXIS[n])
    shard_shapes = [given[n].shape for n in SMALL_SHARDED]
    smalls = [gsmall[s] for s in range(4)]
    per_chip = [_unpack_rows(smalls[s], shard_shapes) for s in range(4)]
    for i, n in enumerate(SMALL_SHARDED):
        full[n] = jnp.concatenate([per_chip[s][i] for s in range(4)], axis=-1)
    w_in_p = _w_in_to_padded(full["w_in"])

    layers = []
    for l in range(depth):
        p = dict(
            n1=_row8(norm1_w[l], width=D_MODEL), n2=_row8(norm2_w[l], width=D_MODEL),
            w_in=w_in_p[l], w_in_t=w_in_p[l].T,
            gcw=jnp.pad(full["gdn_conv_w"][l], ((0, 4), (0, 0))),
            galog=_row8(gdn_a_log[l], 8), gdtb=_row8(gdn_dt_bias[l], 8), gnw=_row8(gdn_norm_w[l]),
            scw=jnp.pad(jnp.concatenate([full["ssd_conv_w"][l], ssd_conv_b[l][None]], axis=0), ((0, 3), (0, 0))),
            sdtb=_row8(ssd_dt_bias[l]), salog=_row8(ssd_a_log[l]), sd=_row8(ssd_d[l]),
            snw=_row8(ssd_norm_w[l], width=D_MODEL), sink=_row8(swa_sinks[l]),
            wpg=full["w_proj_gdn"][l], wps=full["w_proj_ssd"][l], wpw=full["w_proj_swa"][l], wout=full["w_out"][l],
            wup=full["w_up"][l], wdown=full["w_down"][l],
            wpg_t=full["w_proj_gdn"][l].T, wps_t=full["w_proj_ssd"][l].T, wpw_t=full["w_proj_swa"][l].T,
            wout_t=full["w_out"][l].T, wup_t=full["w_up"][l].T, wdown_t=full["w_down"][l].T)
        layers.append(p)

    h = jnp.concatenate([jnp.zeros((PAD, D_MODEL), f32), full["meta_tokens"], x[0]], axis=0)
    saved = []
    for l in range(depth):
        h, s = _layer_fwd(h, layers[l], l)
        saved.append(s)
    loss8, dh, dfw8 = loss_head(h, _row8(final_norm_w, width=D_MODEL), loss_target[0])
    grads = {"final_norm_w": dfw8[0]}
    per_layer = [None] * depth
    for l in reversed(range(depth)):
        dh, per_layer[l] = _layer_bwd(dh, layers[l], saved[l], l)
    grad_x = dh[HEAD_ROWS:][None]
    grads["meta_tokens"] = dh[PAD:HEAD_ROWS]
    lane = {"gdn_a_log": (8, 8), "gdn_dt_bias": (8, 8), "gdn_norm_w": (0, 128), "ssd_dt_bias": (0, 16),
            "ssd_a_log": (0, 16), "ssd_d": (0, 16), "swa_sinks": (0, 16)}
    for n in per_layer[0]:
        parts = [per_layer[l][n] for l in range(depth)]
        if n in lane:
            parts = [q[0, lane[n][0]:lane[n][0] + lane[n][1]] for q in parts]
        elif n in ("norm1_w", "norm2_w", "ssd_norm_w"):
            parts = [q[0] for q in parts]
        elif n == "w_in":
            parts = [_w_in_from_padded(q) for q in parts]
        grads[n] = jnp.stack(parts)
    loss = lax.psum(loss8[0, 0], ("x", "y", "c"))

    gb = jnp.concatenate([jnp.pad(_split_chips(grads[n], BIG_AXIS[n]).reshape(4, r, D_MODEL).astype(bf16),
                                  ((0, 0), (0, p - r), (0, 0))) for n, r, p in BIG], axis=1)
    gs = _pack_rows([grads[n] for n in SMALL_NAMES], f32)
    rb, rs = chip_exchange(gb, gs, scatter=True, name="scatter_grads")
    pb, ps_ = reduce4(rb, name="sum_chips_big"), reduce4(rs, name="sum_chips_small")
    sb, ss = sibling_swap(pb, ps_, name="swap_cores")

    out = {}
    for n, r, _ in BIG:
        shp = given[n].shape
        res = adamw(given[n].reshape(r, D_MODEL), given["m_" + n].reshape(r, D_MODEL),
                    given["v_" + n].reshape(r, D_MODEL), pb, sb, BIG_OFF[n], name=f"adamw_{n}")
        out[n] = [a.reshape(shp) for a in res]
    full_shapes = [grads[n].shape for n in SMALL_NAMES]
    mine_s, sib_s = _unpack_rows(ps_, full_shapes), _unpack_rows(ss, full_shapes)

    def local(parts):
        loc = []
        for n, a in zip(SMALL_NAMES, parts):
            if n in SMALL_SHARDED:
                sz = a.shape[-1] // 4
                a = lax.dynamic_slice_in_dim(a, me * sz, sz, axis=a.ndim - 1)
            loc.append(a)
        return _pack_rows(loc, f32)

    res = adamw(_pack_rows([given[n] for n in SMALL_NAMES], f32), _pack_rows([given["m_" + n] for n in SMALL_NAMES], f32),
                _pack_rows([given["v_" + n] for n in SMALL_NAMES], f32), local(mine_s), local(sib_s), 0,
                name="adamw_small")
    local_shapes = [given[n].shape for n in SMALL_NAMES]
    unpacked = [_unpack_rows(a, local_shapes) for a in res]
    for i, n in enumerate(SMALL_NAMES):
        out[n] = [unpacked[j][i] for j in range(4)]

    return (loss, grad_x) + tuple(out[n][j] for j in range(4) for n in W_NAMES)
```

```python
import functools

import jax
import jax.numpy as jnp
from jax import lax
from jax.experimental import pallas as pl
from jax.experimental.pallas import tpu as pltpu

f32 = jnp.float32
bf16 = jnp.bfloat16
HI = lax.Precision.HIGHEST

D_MODEL = 1024
N_META = 16
PAD = 112
HEAD_ROWS = PAD + N_META
RMS_EPS = 1e-6
L2_EPS = 1e-6
D_FF = 4 * D_MODEL

GDN_HEADS = 8
GDN_D = 128
GDN_CHUNK = 64
SSD_HEADS = 16
SSD_P = 64
SSD_GROUPS = 4
SSD_HPG = 4
SSD_N = 128
SSD_CHUNK = 128
SWA_Q_HEADS = 16
SWA_KV_HEADS = 4
SWA_REP = 4
SWA_D = 64
SWA_W = 128

C_GATE = 0
C_GQ, C_GK, C_GV, C_GG = 3072, 4096, 5120, 6144
C_SZ = 7168
C_SX, C_SB, C_SC = 8192, 9216, 9728
C_WQ, C_WK, C_WV = 10240, 11264, 11520
C_BA = 11776
C_SDT = 11904
IN_WP = 12032
IN_W = 11808

ADAM_LR, ADAM_B1, ADAM_B2, ADAM_EPS, ADAM_WD, ADAM_STEP = 0.001, 0.9, 0.999, 1e-08, 0.01, 10

VMEM_LIMIT = 56 * 1024 * 1024

NN = (((1,), (0,)), ((), ()))
NT = (((1,), (1,)), ((), ()))
TN = (((0,), (0,)), ((), ()))


def _dot(a, b, dims=NN):
    return lax.dot_general(a.astype(bf16), b.astype(bf16), dims, preferred_element_type=f32)


def _dotx(a, b, dims=NN):
    return lax.dot_general(a, b, dims, preferred_element_type=f32, precision=HI)


def _iota(shape, axis):
    return lax.broadcasted_iota(jnp.int32, shape, axis)


def _softplus(x):
    return jnp.maximum(x, 0.0) + jnp.log1p(jnp.exp(-jnp.abs(x)))


def _silu(x):
    return x * jax.nn.sigmoid(x)


def _params(sem):
    return pltpu.CompilerParams(dimension_semantics=sem, vmem_limit_bytes=VMEM_LIMIT)


@functools.partial(jax.custom_vjp, nondiff_argnums=(1,))
def _window(x_ext, off):
    n = x_ext.shape[0] - 8
    if off == 8:
        return x_ext[8:]
    return pltpu.roll(x_ext, 8 - off, 0)[8:]


def _window_fwd(x_ext, off):
    return _window(x_ext, off), None


def _window_bwd(off, _, g):
    n, w = g.shape
    g_ext = jnp.concatenate([jnp.zeros((8, w), g.dtype), g], axis=0)
    if off == 8:
        return (g_ext,)
    return (pltpu.roll(g_ext, n + off, 0),)


_window.defvjp(_window_fwd, _window_bwd)


def _conv4(x, halo, taps):
    x_ext = jnp.concatenate([halo, x], axis=0)
    y = taps[3] * x
    for j in range(3):
        y = y + taps[j] * _window(x_ext, 5 + j)
    return y


def _blockinv_impl(a):
    n = a.shape[0]
    ri, ci = _iota((n, n), 0), _iota((n, n), 1)
    t = (ri == ci).astype(f32)
    k = 0
    while (1 << k) < n:
        sel = ((ri >> (k + 1)) == (ci >> (k + 1))) & (((ri >> k) & 1) == 1) & (((ci >> k) & 1) == 0)
        o = jnp.where(sel, a, 0.0)
        t = t - _dotx(_dotx(t, o), t)
        k += 1
    return t


@jax.custom_vjp
def _blockinv(a):
    return _blockinv_impl(a)


def _blockinv_fwd(a):
    t = _blockinv_impl(a)
    return t, t


def _blockinv_bwd(t, dt):
    return (-_dotx(_dotx(t, dt, TN), t, NT),)


_blockinv.defvjp(_blockinv_fwd, _blockinv_bwd)


def _gdn_act(xq, xk, xv, hq, hk, hv, tq, tk, tv):
    return _silu(_conv4(xq, hq, tq)), _silu(_conv4(xk, hk, tk)), _silu(_conv4(xv, hv, tv))


def _gdn_core(q, k, v, gate, mb, mg, s, ba, alog, dtb, nw, *, row0):
    c = GDN_CHUNK
    q = q * lax.rsqrt(jnp.sum(q * q, axis=1, keepdims=True) + L2_EPS) * (GDN_D ** -0.5)
    k = k * lax.rsqrt(jnp.sum(k * k, axis=1, keepdims=True) + L2_EPS)

    valid = (row0 + _iota((c, 1), 0)) >= PAD
    pick = lambda x, m: jnp.sum(x * m, axis=1, keepdims=True)
    beta = jnp.where(valid, jax.nn.sigmoid(pick(ba, mb)), 0.0)
    g1 = jnp.where(valid, -jnp.exp(pick(alog, mg)) * _softplus(pick(ba, mg) + pick(dtb, mg)), 0.0)
    g = jnp.broadcast_to(g1, (c, GDN_D))
    g64 = jnp.broadcast_to(g1, (c, c))

    ri, ci = _iota((c, c), 0), _iota((c, c), 1)
    incl = ci <= ri
    gam = _dotx(incl.astype(f32), g)
    gam_i = _dotx(incl.astype(f32), g64)
    gam_j = _dotx(jnp.ones((c, c), f32), jnp.where(ri <= ci, g64, 0.0))
    decay = jnp.where(incl, jnp.exp(jnp.where(incl, gam_i - gam_j, 0.0)), 0.0)

    kb = k * beta
    a = jnp.where(ci < ri, _dot(kb, k, NT) * decay, 0.0)
    t = _blockinv(a)
    egam = jnp.exp(gam)
    u = _dotx(t, v * beta)
    w = _dotx(t, kb * egam)
    attn = _dot(q, k, NT) * decay
    gl = jnp.sum(g, axis=0, keepdims=True)
    kt = k * jnp.exp(gl - gam)
    v_new = u - _dot(w, s)
    o = _dot(q * egam, s) + _dot(attn, v_new)
    s_out = s * jnp.exp(gl) + _dot(kt, v_new, TN)

    y = o * lax.rsqrt(jnp.mean(o * o, axis=1, keepdims=True) + RMS_EPS) * nw * _silu(gate)
    return y, s_out


def _gdn_specs(hb, nc, rev):
    w = hb * GDN_D
    cw = D_MODEL // w

    def cidx(c):
        return (nc - 1 - c) if rev else c

    def col(base):
        return pl.BlockSpec((GDN_CHUNK, w), lambda h, c: (cidx(c), base // w + h))

    def halo(base):
        return pl.BlockSpec((8, w), lambda h, c: (jnp.maximum(cidx(c) * (GDN_CHUNK // 8) - 1, 0), base // w + h))

    def taps(base):
        return pl.BlockSpec((8, w), lambda h, c: (0, base // w + h))

    ba = pl.BlockSpec((GDN_CHUNK, 128), lambda h, c: (cidx(c), C_BA // 128))
    row = pl.BlockSpec((8, 128), lambda h, c: (0, 0))
    y = pl.BlockSpec((GDN_CHUNK, w), lambda h, c: (cidx(c), h))
    st = pl.BlockSpec((1, hb, GDN_D, GDN_D), lambda h, c: (cidx(c), h, 0, 0))
    in_specs = [col(C_GQ), col(C_GK), col(C_GV), halo(C_GQ), halo(C_GK), halo(C_GV), col(C_GG), ba,
                taps(0), taps(1024), taps(2048), row, row, row]
    return in_specs, y, st, taps, row, col, ba


def _gdn_load(refs, first):
    xq, xk, xv, hq, hk, hv, gate, ba, tq, tk, tv, alog, dtb, nw = refs

    def halo(r):
        return jnp.where(first, 0.0, r[...])

    def taps(r):
        return tuple(r[j:j + 1, :] for j in range(4))

    act = (xq[...], xk[...], xv[...], halo(hq), halo(hk), halo(hv), taps(tq), taps(tk), taps(tv))
    return act, gate[...], (ba[...], alog[0:1, :], dtb[0:1, :], nw[0:1, :])


def _heads(a, hb):
    return jnp.stack([a[:, i * GDN_D:(i + 1) * GDN_D] for i in range(hb)])


def _wide(a):
    return jnp.concatenate([a[i] for i in range(a.shape[0])], axis=1)


def _head_masks(hblk, hb):
    head = hblk * hb + _iota((hb, 1, 128), 0)
    lane = _iota((hb, 1, 128), 2)
    return (lane == head).astype(f32), (lane == head + 8).astype(f32)


def _gdn_core_heads(row0):
    return jax.vmap(functools.partial(_gdn_core, row0=row0), in_axes=(0, 0, 0, 0, 0, 0, 0, None, None, None, None))


def gdn_fwd(u, conv_w8, alog8, dtb8, nw8, *, hb=8):
    t_rows = u.shape[0]
    nc = t_rows // GDN_CHUNK
    in_specs, y_spec, st_spec, *_ = _gdn_specs(hb, nc, False)

    def body(*refs):
        ins, (y_ref, st_ref), (s_scr,) = refs[:14], refs[14:16], refs[16:]
        hblk, c = pl.program_id(0), pl.program_id(1)

        @pl.when(c == 0)
        def _():
            s_scr[...] = jnp.zeros_like(s_scr)

        act, gate, shared = _gdn_load(ins, c == 0)
        s = s_scr[...]
        st_ref[0] = s
        qa, ka, va = _gdn_act(*act)
        mb, mg = _head_masks(hblk, hb)
        y, s_new = _gdn_core_heads(c * GDN_CHUNK)(_heads(qa, hb), _heads(ka, hb), _heads(va, hb), _heads(gate, hb),
                                                  mb, mg, s, *shared)
        y_ref[...] = _wide(y).astype(bf16)
        s_scr[...] = s_new

    return pl.pallas_call(
        body, name="gdn_fwd", grid=(GDN_HEADS // hb, nc),
        in_specs=in_specs, out_specs=(y_spec, st_spec),
        out_shape=(jax.ShapeDtypeStruct((t_rows, D_MODEL), bf16),
                   jax.ShapeDtypeStruct((nc, GDN_HEADS, GDN_D, GDN_D), f32)),
        scratch_shapes=[pltpu.VMEM((hb, GDN_D, GDN_D), f32)],
        compiler_params=_params(("arbitrary", "arbitrary")),
    )(u, u, u, u, u, u, u, u, conv_w8, conv_w8, conv_w8, alog8, dtb8, nw8)


def gdn_bwd(u, conv_w8, alog8, dtb8, nw8, states, dy, *, hb=8):
    t_rows = u.shape[0]
    nc = t_rows // GDN_CHUNK
    w = hb * GDN_D
    in_specs, y_spec, st_spec, taps, row, col, ba = _gdn_specs(hb, nc, True)
    nhb = GDN_HEADS // hb

    def body(*refs):
        ins, st_ref, dy_ref = refs[:14], refs[14], refs[15]
        dq_ref, dk_ref, dv_ref, dgate_ref, dba_ref, dtq_ref, dtk_ref, dtv_ref, dalog_ref, ddtb_ref, dnw_ref = refs[16:27]
        ds_scr, dh_scr = refs[27:]
        hblk, cc = pl.program_id(0), pl.program_id(1)
        c = nc - 1 - cc

        @pl.when(cc == 0)
        def _():
            ds_scr[...] = jnp.zeros_like(ds_scr)
            dh_scr[...] = jnp.zeros_like(dh_scr)
            dtq_ref[...] = jnp.zeros_like(dtq_ref)
            dtk_ref[...] = jnp.zeros_like(dtk_ref)
            dtv_ref[...] = jnp.zeros_like(dtv_ref)

        @pl.when((cc == 0) & (hblk == 0))
        def _():
            dalog_ref[...] = jnp.zeros_like(dalog_ref)
            ddtb_ref[...] = jnp.zeros_like(ddtb_ref)
            dnw_ref[...] = jnp.zeros_like(dnw_ref)

        act, gate, shared = _gdn_load(ins, c == 0)
        (qa, ka, va), vjp_act = jax.vjp(_gdn_act, *act)
        mb, mg = _head_masks(hblk, hb)
        _, vjp_core = jax.vjp(_gdn_core_heads(c * GDN_CHUNK), _heads(qa, hb), _heads(ka, hb), _heads(va, hb),
                              _heads(gate, hb), mb, mg, st_ref[0], *shared)
        dqa, dka, dva, dgate, _, _, ds, dba, dalog, ddtb, dnw = vjp_core(
            (_heads(dy_ref[...].astype(f32), hb), ds_scr[...]))
        ds_scr[...] = ds
        dxq, dxk, dxv, dhq, dhk, dhv, dtq, dtk, dtv = vjp_act((_wide(dqa), _wide(dka), _wide(dva)))
        zeros = jnp.zeros((GDN_CHUNK - 8, w), f32)
        for j, (dx, dh, out) in enumerate(((dxq, dhq, dq_ref), (dxk, dhk, dk_ref), (dxv, dhv, dv_ref))):
            out[...] = (dx + jnp.concatenate([zeros, dh_scr[j]], axis=0)).astype(bf16)
            dh_scr[j] = dh
        dgate_ref[...] = _wide(dgate).astype(bf16)
        dba_ref[0] = dba
        for dt_ref, dtaps in ((dtq_ref, dtq), (dtk_ref, dtk), (dtv_ref, dtv)):
            for j in range(4):
                dt_ref[j:j + 1, :] += dtaps[j]
        dalog_ref[0:1, :] += dalog
        ddtb_ref[0:1, :] += ddtb
        dnw_ref[0:1, :] += dnw

    def colr(base):
        return pl.BlockSpec((GDN_CHUNK, w), lambda h, c: (nc - 1 - c, base // w + h))

    out_specs = (colr(0), colr(0), colr(0), colr(0),
                 pl.BlockSpec((1, GDN_CHUNK, 128), lambda h, c: (h, nc - 1 - c, 0)),
                 taps(0), taps(0), taps(0), row, row, row)
    out_shape = (jax.ShapeDtypeStruct((t_rows, D_MODEL), bf16),) * 4 + (
        jax.ShapeDtypeStruct((nhb, t_rows, 128), f32),
        jax.ShapeDtypeStruct((8, D_MODEL), f32), jax.ShapeDtypeStruct((8, D_MODEL), f32),
        jax.ShapeDtypeStruct((8, D_MODEL), f32),
        jax.ShapeDtypeStruct((8, 128), f32), jax.ShapeDtypeStruct((8, 128), f32), jax.ShapeDtypeStruct((8, 128), f32))
    return pl.pallas_call(
        body, name="gdn_bwd", grid=(nhb, nc),
        in_specs=in_specs + [st_spec, y_spec], out_specs=out_specs, out_shape=out_shape,
        scratch_shapes=[pltpu.VMEM((hb, GDN_D, GDN_D), f32), pltpu.VMEM((3, 8, w), f32)],
        compiler_params=_params(("arbitrary", "arbitrary")),
    )(u, u, u, u, u, u, u, u, conv_w8, conv_w8, conv_w8, alog8, dtb8, nw8, states, dy)


def _ssd_act(xs_r, b_r, c_r, hx, hbm, hcm, tx, tb, tc, bx, bb, bc, *, row0):
    valid = (row0 + _iota((SSD_CHUNK, 1), 0)) >= PAD
    act = lambda x, h, t, b: jnp.where(valid, _silu(_conv4(x, h, t) + b), 0.0)
    return act(xs_r, hx, tx, bx), act(b_r, hbm, tb, bb), act(c_r, hcm, tc, bc)


def _ssd_core(xs, bm, cm, z, nw, m0, m1, m2, m3, h, dt, dtb, alog, dsk, *, row0):
    n = SSD_CHUNK
    valid = (row0 + _iota((n, 1), 0)) >= PAD
    dtp16 = _softplus(dt + dtb)
    a16 = -jnp.exp(alog)
    pick = lambda x, m: jnp.sum(x * m, axis=1, keepdims=True)
    lane_r = _iota((1, 256), 1) >> 6
    dtp = jnp.zeros((n, 256), f32)
    adt = jnp.zeros((n, 256), f32)
    dlane = jnp.zeros((1, 256), f32)
    acols = []
    for r, m in enumerate((m0, m1, m2, m3)):
        dcol = jnp.where(valid, pick(dtp16, m), 0.0)
        acol = dcol * pick(a16, m)
        dtp = jnp.where(lane_r == r, dcol, dtp)
        adt = jnp.where(lane_r == r, acol, adt)
        dlane = jnp.where(lane_r == r, pick(dsk, m), dlane)
        acols.append(acol)

    ri, ci = _iota((n, n), 0), _iota((n, n), 1)
    incl = ci <= ri
    inclf = incl.astype(f32)
    acum = _dotx(inclf, adt)
    al = jnp.sum(adt, axis=0, keepdims=True)
    xdt = xs * dtp
    cb = _dot(cm, bm, NT)
    y = _dot(cm, h) * jnp.exp(acum) + dlane * xs
    for r in range(SSD_HPG):
        ab = jnp.broadcast_to(acols[r], (n, n))
        ai = _dotx(inclf, ab)
        aj = _dotx(jnp.ones((n, n), f32), jnp.where(ri <= ci, ab, 0.0))
        lm = jnp.where(incl, jnp.exp(jnp.where(incl, ai - aj, 0.0)), 0.0)
        y = y + _dot(cb * lm, jnp.where(lane_r == r, xdt, 0.0))
    h_out = h * jnp.exp(al) + _dot(bm, jnp.exp(al - acum) * xdt, TN)
    y = y * _silu(z)
    y = y * lax.rsqrt(jnp.mean(y * y, axis=1, keepdims=True) + RMS_EPS) * nw
    return y, h_out


def _ssd_core_groups(row0):
    return jax.vmap(functools.partial(_ssd_core, row0=row0), in_axes=(0,) * 10 + (None,) * 4)


def _ssd_specs(nc, rev):
    n = SSD_CHUNK

    def cidx(c):
        return (nc - 1 - c) if rev else c

    def col(base, w):
        return pl.BlockSpec((n, w), lambda c: (cidx(c), base // w))

    def halo(base, w):
        return pl.BlockSpec((8, w), lambda c: (jnp.maximum(cidx(c) * (n // 8) - 1, 0), base // w))

    def taps(base, w):
        return pl.BlockSpec((8, w), lambda c: (0, base // w))

    row = pl.BlockSpec((8, 128), lambda c: (0, 0))
    in_specs = [col(C_SX, 1024), col(C_SB, 512), col(C_SC, 512), halo(C_SX, 1024), halo(C_SB, 512), halo(C_SC, 512),
                col(C_SZ, 1024), col(C_SDT, 128), taps(0, 1024), taps(1024, 512), taps(1536, 512), row, row, row,
                taps(0, 1024)]
    y = pl.BlockSpec((n, D_MODEL), lambda c: (cidx(c), 0))
    st = pl.BlockSpec((1, SSD_GROUPS, SSD_N, 256), lambda c: (cidx(c), 0, 0, 0))
    return in_specs, y, st, col, taps, row


def _ssd_load(refs, first):
    xs, bm, cm, hx, hbm, hcm, z, dt, tx, tb, tc, dtb, alog, dsk, nw = refs

    def halo(r):
        return jnp.where(first, 0.0, r[...])

    def taps(r):
        return tuple(r[j:j + 1, :] for j in range(4))

    act = (xs[...], bm[...], cm[...], halo(hx), halo(hbm), halo(hcm), taps(tx), taps(tb), taps(tc),
           tx[4:5, :], tb[4:5, :], tc[4:5, :])
    return act, (z[...], nw[0:1, :]), (dt[...], dtb[0:1, :], alog[0:1, :], dsk[0:1, :])


def _groups(a, w):
    return jnp.stack([a[:, i * w:(i + 1) * w] for i in range(SSD_GROUPS)])


def _ssd_masks():
    head = _iota((SSD_GROUPS, 1, 128), 0) * SSD_HPG
    lane = _iota((SSD_GROUPS, 1, 128), 2)
    return tuple((lane == head + r).astype(f32) for r in range(SSD_HPG))


def ssd_fwd(u, conv_w8, dtb8, alog8, d8, nw8):
    t_rows = u.shape[0]
    nc = t_rows // SSD_CHUNK
    in_specs, y_spec, st_spec, *_ = _ssd_specs(nc, False)

    def body(*refs):
        ins, (y_ref, st_ref), (h_scr,) = refs[:15], refs[15:17], refs[17:]
        c = pl.program_id(0)

        @pl.when(c == 0)
        def _():
            h_scr[...] = jnp.zeros_like(h_scr)

        act, (z, nw), shared = _ssd_load(ins, c == 0)
        h = h_scr[...]
        st_ref[0] = h
        xs, bm, cm = _ssd_act(*act, row0=c * SSD_CHUNK)
        y, h_new = _ssd_core_groups(c * SSD_CHUNK)(_groups(xs, 256), _groups(bm, 128), _groups(cm, 128),
                                                   _groups(z, 256), _groups(nw, 256), *_ssd_masks(), h, *shared)
        y_ref[...] = _wide(y).astype(bf16)
        h_scr[...] = h_new

    return pl.pallas_call(
        body, name="ssd_fwd", grid=(nc,), in_specs=in_specs, out_specs=(y_spec, st_spec),
        out_shape=(jax.ShapeDtypeStruct((t_rows, D_MODEL), bf16),
                   jax.ShapeDtypeStruct((nc, SSD_GROUPS, SSD_N, 256), f32)),
        scratch_shapes=[pltpu.VMEM((SSD_GROUPS, SSD_N, 256), f32)],
        compiler_params=_params(("arbitrary",)),
    )(u, u, u, u, u, u, u, u, conv_w8, conv_w8, conv_w8, dtb8, alog8, d8, nw8)


def ssd_bwd(u, conv_w8, dtb8, alog8, d8, nw8, states, dy):
    t_rows = u.shape[0]
    nc = t_rows // SSD_CHUNK
    n = SSD_CHUNK
    in_specs, y_spec, st_spec, col, taps, row = _ssd_specs(nc, True)

    def body(*refs):
        ins, st_ref, dy_ref = refs[:15], refs[15], refs[16]
        (dz_ref, dxs_ref, db_ref, dc_ref, ddt_ref, dtx_ref, dtb_ref, dtc_ref, ddtb_ref, dalog_ref, ddsk_ref,
         dnw_ref) = refs[17:29]
        dh_scr, hx_scr, hb_scr, hc_scr = refs[29:]
        cc = pl.program_id(0)
        c = nc - 1 - cc

        @pl.when(cc == 0)
        def _():
            for r in (dh_scr, hx_scr, hb_scr, hc_scr, dtx_ref, dtb_ref, dtc_ref, dnw_ref, ddtb_ref, dalog_ref, ddsk_ref):
                r[...] = jnp.zeros_like(r)

        act, (z, nw), shared = _ssd_load(ins, c == 0)
        (xs, bm, cm), vjp_act = jax.vjp(functools.partial(_ssd_act, row0=c * n), *act)
        _, vjp_core = jax.vjp(_ssd_core_groups(c * n), _groups(xs, 256), _groups(bm, 128), _groups(cm, 128),
                              _groups(z, 256), _groups(nw, 256), *_ssd_masks(), st_ref[0], *shared)
        dxa, dba, dca, dz, dnw, _, _, _, _, dh, ddt, ddtb, dalog, ddsk = vjp_core(
            (_groups(dy_ref[...].astype(f32), 256), dh_scr[...]))
        dh_scr[...] = dh
        dxs, dbm, dcm, dhx, dhb, dhc, dtx, dtb, dtc, dbx, dbb, dbc = vjp_act((_wide(dxa), _wide(dba), _wide(dca)))
        for dx, dhalo, scr, out in ((dxs, dhx, hx_scr, dxs_ref), (dbm, dhb, hb_scr, db_ref), (dcm, dhc, hc_scr, dc_ref)):
            zeros = jnp.zeros((n - 8, dx.shape[1]), f32)
            out[...] = (dx + jnp.concatenate([zeros, scr[...]], axis=0)).astype(bf16)
            scr[...] = dhalo
        dz_ref[...] = _wide(dz).astype(bf16)
        ddt_ref[...] = ddt
        for ref, dtaps, dbias in ((dtx_ref, dtx, dbx), (dtb_ref, dtb, dbb), (dtc_ref, dtc, dbc)):
            for j in range(4):
                ref[j:j + 1, :] += dtaps[j]
            ref[4:5, :] += dbias
        ddtb_ref[0:1, :] += ddtb
        dalog_ref[0:1, :] += dalog
        ddsk_ref[0:1, :] += ddsk
        dnw_ref[0:1, :] += _wide(dnw)

    def out_col(w):
        return pl.BlockSpec((n, w), lambda c: (nc - 1 - c, 0))

    out_specs = (out_col(D_MODEL), out_col(D_MODEL), out_col(512), out_col(512), out_col(128),
                 taps(0, D_MODEL), taps(0, 512), taps(0, 512), row, row, row, taps(0, D_MODEL))
    out_shape = (jax.ShapeDtypeStruct((t_rows, D_MODEL), bf16), jax.ShapeDtypeStruct((t_rows, D_MODEL), bf16),
                 jax.ShapeDtypeStruct((t_rows, 512), bf16), jax.ShapeDtypeStruct((t_rows, 512), bf16),
                 jax.ShapeDtypeStruct((t_rows, 128), f32),
                 jax.ShapeDtypeStruct((8, D_MODEL), f32), jax.ShapeDtypeStruct((8, 512), f32),
                 jax.ShapeDtypeStruct((8, 512), f32),
                 jax.ShapeDtypeStruct((8, 128), f32), jax.ShapeDtypeStruct((8, 128), f32),
                 jax.ShapeDtypeStruct((8, 128), f32), jax.ShapeDtypeStruct((8, D_MODEL), f32))
    return pl.pallas_call(
        body, name="ssd_bwd", grid=(nc,), in_specs=in_specs + [st_spec, y_spec],
        out_specs=out_specs, out_shape=out_shape,
        scratch_shapes=[pltpu.VMEM((SSD_GROUPS, SSD_N, 256), f32), pltpu.VMEM((8, D_MODEL), f32),
                        pltpu.VMEM((8, 512), f32), pltpu.VMEM((8, 512), f32)],
        compiler_params=_params(("arbitrary",)),
    )(u, u, u, u, u, u, u, u, conv_w8, conv_w8, conv_w8, dtb8, alog8, d8, nw8, states, dy)


NEG = -1e30


def _swa_core(q, kc, kp, km, vc, vp, vm, sink, *, n):
    rows = SWA_REP * SWA_W
    ri, ci = _iota((rows, SWA_W), 0) & (SWA_W - 1), _iota((rows, SWA_W), 1)
    causal = ci <= ri
    m_cur = causal & ((n >= 1) | ((ci >= PAD) & (ri >= PAD)))
    m_prev = (n >= 2) & (ci > ri)
    m_meta = (n >= 1) & (ci >= PAD)
    q = q * (SWA_D ** -0.5)
    sc = jnp.where(m_cur, _dot(q, kc, NT), NEG)
    sp = jnp.where(m_prev, _dot(q, kp, NT), NEG)
    sm = jnp.where(m_meta, _dot(q, km, NT), NEG)
    mx = jnp.maximum(jnp.maximum(jnp.max(sc, axis=1, keepdims=True), jnp.max(sp, axis=1, keepdims=True)),
                     jnp.maximum(jnp.max(sm, axis=1, keepdims=True), sink))
    mx = lax.stop_gradient(mx)
    ec, ep, em = jnp.exp(sc - mx), jnp.exp(sp - mx), jnp.exp(sm - mx)
    den = (jnp.sum(ec, axis=1, keepdims=True) + jnp.sum(ep, axis=1, keepdims=True)
           + jnp.sum(em, axis=1, keepdims=True) + jnp.exp(sink - mx))
    return (_dot(ec, vc) + _dot(ep, vp) + _dot(em, vm)) / den


def _swa_block(q16, kc, kp, km, vc, vp, vm, sink16, *, n):
    rows = SWA_REP * SWA_W
    lane = _iota((1, 128), 1)
    rep = _iota((rows, 1), 0) >> 7
    cols = []
    for h in range(SWA_KV_HEADS):
        col = jnp.zeros((rows, 1), f32)
        for r in range(SWA_REP):
            s = jnp.sum(jnp.where(lane == h * SWA_REP + r, sink16, 0.0), axis=1, keepdims=True)
            col = jnp.where(rep == r, s, col)
        cols.append(col)
    o = jax.vmap(functools.partial(_swa_core, n=n))(q16.reshape(SWA_KV_HEADS, rows, SWA_D), kc, kp, km, vc, vp, vm,
                                                    jnp.stack(cols))
    return o.reshape(q16.shape)


def _swa_specs(nb, rev):
    def bidx(n):
        return (nb - 1 - n) if rev else n

    q = pl.BlockSpec((SWA_Q_HEADS, SWA_W, SWA_D), lambda n: (0, bidx(n), 0))
    cur = pl.BlockSpec((SWA_KV_HEADS, SWA_W, SWA_D), lambda n: (0, bidx(n), 0))
    prev = pl.BlockSpec((SWA_KV_HEADS, SWA_W, SWA_D), lambda n: (0, jnp.maximum(bidx(n) - 1, 0), 0))
    meta = pl.BlockSpec((SWA_KV_HEADS, SWA_W, SWA_D), lambda n: (0, 0, 0))
    row = pl.BlockSpec((8, 128), lambda n: (0, 0))
    return [q, cur, prev, meta, cur, prev, meta, row], q, cur, row


def swa_fwd(q, k, v, sink8):
    t_rows = q.shape[1]
    nb = t_rows // SWA_W
    in_specs, q_spec, _, _ = _swa_specs(nb, False)

    def body(q_ref, kc, kp, km, vc, vp, vm, sink_ref, o_ref):
        o_ref[...] = _swa_block(q_ref[...], kc[...], kp[...], km[...], vc[...], vp[...], vm[...], sink_ref[0:1, :],
                                n=pl.program_id(0)).astype(bf16)

    return pl.pallas_call(
        body, name="swa_fwd", grid=(nb,), in_specs=in_specs, out_specs=q_spec,
        out_shape=jax.ShapeDtypeStruct(q.shape, bf16),
        compiler_params=_params(("arbitrary",)),
    )(q, k, k, k, v, v, v, sink8)


def swa_bwd(q, k, v, sink8, do):
    t_rows = q.shape[1]
    nb = t_rows // SWA_W
    in_specs, q_spec, kv_spec, row = _swa_specs(nb, True)

    def body(q_ref, kc, kp, km, vc, vp, vm, sink_ref, do_ref, dq_ref, dk_ref, dv_ref, dsink_ref,
             dkp_scr, dvp_scr, dkm_scr, dvm_scr):
        nn = pl.program_id(0)
        n = nb - 1 - nn

        @pl.when(nn == 0)
        def _():
            for r in (dkp_scr, dvp_scr, dkm_scr, dvm_scr, dsink_ref):
                r[...] = jnp.zeros_like(r)

        fn = functools.partial(_swa_block, n=n)
        _, vjp = jax.vjp(fn, q_ref[...], kc[...], kp[...], km[...], vc[...], vp[...], vm[...], sink_ref[0:1, :])
        dq, dkc, dkp, dkm, dvc, dvp, dvm, dsink = vjp(do_ref[...].astype(f32))
        dq_ref[...] = dq.astype(bf16)
        dkm_scr[...] += dkm
        dvm_scr[...] += dvm
        first = n == 0
        dk_ref[...] = (dkc + dkp_scr[...] + jnp.where(first, dkm_scr[...], 0.0)).astype(bf16)
        dv_ref[...] = (dvc + dvp_scr[...] + jnp.where(first, dvm_scr[...], 0.0)).astype(bf16)
        dkp_scr[...] = dkp
        dvp_scr[...] = dvp
        dsink_ref[0:1, :] += dsink

    kv_shape = jax.ShapeDtypeStruct(k.shape, bf16)
    return pl.pallas_call(
        body, name="swa_bwd", grid=(nb,), in_specs=in_specs + [q_spec],
        out_specs=(q_spec, kv_spec, kv_spec, row),
        out_shape=(jax.ShapeDtypeStruct(q.shape, bf16), kv_shape, kv_shape, jax.ShapeDtypeStruct((8, 128), f32)),
        scratch_shapes=[pltpu.VMEM((SWA_KV_HEADS, SWA_W, SWA_D), f32)] * 4,
        compiler_params=_params(("arbitrary",)),
    )(q, k, k, k, v, v, v, sink8, do)


def _tile(dim, prefs):
    for p in prefs:
        if dim % p == 0:
            return p
    return dim


def mm(a, b, *, out_dtype, name, resid=None):
    m, k = a.shape
    n = b.shape[1]
    tm = _tile(m, (1408, 1024, 512, 384, 256, 128))
    tn = _tile(n, (512, 256, 128))
    tk = _tile(k, (1408, 1024, 512, 384, 256, 128))
    nk = k // tk

    def body(*refs):
        if resid is None:
            a_ref, b_ref, o_ref, acc = refs
        else:
            a_ref, b_ref, r_ref, o_ref, acc = refs
        kk = pl.program_id(2)

        @pl.when(kk == 0)
        def _():
            acc[...] = jnp.zeros_like(acc)

        acc[...] += _dot(a_ref[...], b_ref[...])

        @pl.when(kk == nk - 1)
        def _():
            o = acc[...]
            if resid is not None:
                o = o + r_ref[...]
            o_ref[...] = o.astype(out_dtype)

    in_specs = [pl.BlockSpec((tm, tk), lambda i, j, kk: (i, kk)), pl.BlockSpec((tk, tn), lambda i, j, kk: (kk, j))]
    args = [a, b]
    if resid is not None:
        in_specs.append(pl.BlockSpec((tm, tn), lambda i, j, kk: (i, j)))
        args.append(resid)
    return pl.pallas_call(
        body, name=name, grid=(m // tm, n // tn, nk), in_specs=in_specs,
        out_specs=pl.BlockSpec((tm, tn), lambda i, j, kk: (i, j)),
        out_shape=jax.ShapeDtypeStruct((m, n), out_dtype),
        scratch_shapes=[pltpu.VMEM((tm, tn), f32)],
        compiler_params=_params(("parallel", "parallel", "arbitrary")),
    )(*args)


def _rows(t_rows):
    return _tile(t_rows, (384, 256, 128))


def _rmsnorm(h, w):
    return h * lax.rsqrt(jnp.mean(h * h, axis=1, keepdims=True) + RMS_EPS) * w


def rmsnorm_fwd(h, w8, *, name):
    t_rows, d = h.shape
    tr = _rows(t_rows)

    def body(h_ref, w_ref, o_ref):
        o_ref[...] = _rmsnorm(h_ref[...], w_ref[0:1, :]).astype(bf16)

    blk = pl.BlockSpec((tr, d), lambda i: (i, 0))
    return pl.pallas_call(
        body, name=name, grid=(t_rows // tr,), in_specs=[blk, pl.BlockSpec((8, d), lambda i: (0, 0))], out_specs=blk,
        out_shape=jax.ShapeDtypeStruct((t_rows, d), bf16), compiler_params=_params(("arbitrary",)),
    )(h, w8)


def rmsnorm_bwd(h, w8, dhn, dres, *, name):
    t_rows, d = h.shape
    tr = _rows(t_rows)

    def body(h_ref, w_ref, dhn_ref, dres_ref, dh_ref, dw_ref):
        @pl.when(pl.program_id(0) == 0)
        def _():
            dw_ref[...] = jnp.zeros_like(dw_ref)

        _, vjp = jax.vjp(_rmsnorm, h_ref[...], w_ref[0:1, :])
        dh, dw = vjp(dhn_ref[...])
        dh_ref[...] = dh + dres_ref[...]
        dw_ref[0:1, :] += dw

    blk = pl.BlockSpec((tr, d), lambda i: (i, 0))
    wblk = pl.BlockSpec((8, d), lambda i: (0, 0))
    return pl.pallas_call(
        body, name=name, grid=(t_rows // tr,), in_specs=[blk, wblk, blk, blk], out_specs=(blk, wblk),
        out_shape=(jax.ShapeDtypeStruct((t_rows, d), f32), jax.ShapeDtypeStruct((8, d), f32)),
        compiler_params=_params(("arbitrary",)),
    )(h, w8, dhn, dres)


def _merge(pg, ps, pw, la, lb, lc):
    return jax.nn.sigmoid(la) * pg + jax.nn.sigmoid(lb) * ps + jax.nn.sigmoid(lc) * pw


def _merge_specs(t_rows):
    tr = _rows(t_rows)
    blk = pl.BlockSpec((tr, D_MODEL), lambda i: (i, 0))
    gate = [pl.BlockSpec((tr, D_MODEL), functools.partial(lambda i, j: (i, j), j=C_GATE // D_MODEL + j)) for j in range(3)]
    return tr, blk, gate


def merge_fwd(pg, ps, pw, u):
    t_rows = pg.shape[0]
    tr, blk, gate = _merge_specs(t_rows)

    def body(pg_ref, ps_ref, pw_ref, la, lb, lc, o_ref):
        o_ref[...] = _merge(pg_ref[...], ps_ref[...], pw_ref[...], la[...], lb[...], lc[...]).astype(bf16)

    return pl.pallas_call(
        body, name="merge_fwd", grid=(t_rows // tr,), in_specs=[blk, blk, blk] + gate, out_specs=blk,
        out_shape=jax.ShapeDtypeStruct((t_rows, D_MODEL), bf16), compiler_params=_params(("arbitrary",)),
    )(pg, ps, pw, u, u, u)


def merge_bwd(pg, ps, pw, u, dmerged):
    t_rows = pg.shape[0]
    tr, blk, gate = _merge_specs(t_rows)

    def body(pg_ref, ps_ref, pw_ref, la, lb, lc, dm_ref, dpg_ref, dps_ref, dpw_ref, dl_ref):
        _, vjp = jax.vjp(_merge, pg_ref[...], ps_ref[...], pw_ref[...], la[...], lb[...], lc[...])
        dpg, dps, dpw, dla, dlb, dlc = vjp(dm_ref[...])
        dpg_ref[...] = dpg.astype(bf16)
        dps_ref[...] = dps.astype(bf16)
        dpw_ref[...] = dpw.astype(bf16)
        for j, dl in enumerate((dla, dlb, dlc)):
            dl_ref[:, j * D_MODEL:(j + 1) * D_MODEL] = dl.astype(bf16)

    act = jax.ShapeDtypeStruct((t_rows, D_MODEL), bf16)
    return pl.pallas_call(
        body, name="merge_bwd", grid=(t_rows // tr,), in_specs=[blk, blk, blk] + gate + [blk],
        out_specs=(blk, blk, blk, pl.BlockSpec((tr, 3 * D_MODEL), lambda i: (i, 0))),
        out_shape=(act, act, act, jax.ShapeDtypeStruct((t_rows, 3 * D_MODEL), bf16)),
        compiler_params=_params(("arbitrary",)),
    )(pg, ps, pw, u, u, u, dmerged)


def relu2_fwd(a):
    t_rows, d = a.shape
    tr = _rows(t_rows)

    def body(a_ref, o_ref):
        r = jnp.maximum(a_ref[...], 0.0)
        o_ref[...] = (r * r).astype(bf16)

    blk = pl.BlockSpec((tr, d), lambda i: (i, 0))
    return pl.pallas_call(
        body, name="relu2_fwd", grid=(t_rows // tr,), in_specs=[blk], out_specs=blk,
        out_shape=jax.ShapeDtypeStruct((t_rows, d), bf16), compiler_params=_params(("arbitrary",)),
    )(a)


def relu2_bwd(a, dr):
    t_rows, d = a.shape
    tr = _rows(t_rows)

    def body(a_ref, dr_ref, o_ref):
        o_ref[...] = (dr_ref[...] * 2.0 * jnp.maximum(a_ref[...], 0.0)).astype(bf16)

    blk = pl.BlockSpec((tr, d), lambda i: (i, 0))
    return pl.pallas_call(
        body, name="relu2_bwd", grid=(t_rows // tr,), in_specs=[blk, blk], out_specs=blk,
        out_shape=jax.ShapeDtypeStruct((t_rows, d), bf16), compiler_params=_params(("arbitrary",)),
    )(a, dr)


def loss_head(h, w8, target):
    t_rows, d = h.shape
    tr = HEAD_ROWS

    def loss_fn(hb, w, tgt):
        err = _rmsnorm(hb, w) - tgt
        return 0.5 * jnp.sum(err * err) / d

    def body(h_ref, w_ref, t_ref, loss_ref, dh_ref, dw_ref):
        i = pl.program_id(0)

        @pl.when(i == 0)
        def _():
            loss_ref[...] = jnp.zeros_like(loss_ref)
            dw_ref[...] = jnp.zeros_like(dw_ref)
            dh_ref[...] = jnp.zeros_like(dh_ref)

        @pl.when(i > 0)
        def _():
            val, (dh, dw) = jax.value_and_grad(loss_fn, argnums=(0, 1))(h_ref[...], w_ref[0:1, :], t_ref[...])
            loss_ref[...] += val
            dh_ref[...] = dh
            dw_ref[0:1, :] += dw

    blk = pl.BlockSpec((tr, d), lambda i: (i, 0))
    wblk = pl.BlockSpec((8, d), lambda i: (0, 0))
    return pl.pallas_call(
        body, name="loss_head", grid=(t_rows // tr,),
        in_specs=[blk, wblk, pl.BlockSpec((tr, d), lambda i: (jnp.maximum(i - 1, 0), 0))],
        out_specs=(pl.BlockSpec((8, 128), lambda i: (0, 0)), blk, wblk),
        out_shape=(jax.ShapeDtypeStruct((8, 128), f32), jax.ShapeDtypeStruct((t_rows, d), f32),
                   jax.ShapeDtypeStruct((8, d), f32)),
        compiler_params=_params(("arbitrary",)),
    )(h, w8, target)


def adamw(w, m, v, p_mine, p_sib, row_off, *, name):
    rows, d = w.shape
    tr = _tile(rows, (512, 256, 144, 128, 64, 48, 16, 8))
    assert row_off % tr == 0
    off = row_off // tr
    c1 = 1.0 - ADAM_B1 ** ADAM_STEP
    c2 = 1.0 - ADAM_B2 ** ADAM_STEP

    def body(w_ref, m_ref, v_ref, pa_ref, pb_ref, g_ref, d_ref, mo_ref, vo_ref):
        g = pa_ref[...] + pb_ref[...]
        m_new = ADAM_B1 * m_ref[...] + (1.0 - ADAM_B1) * g
        v_new = ADAM_B2 * v_ref[...] + (1.0 - ADAM_B2) * (g * g)
        g_ref[...] = g
        d_ref[...] = -ADAM_LR * ((m_new / c1) / (jnp.sqrt(v_new / c2) + ADAM_EPS) + ADAM_WD * w_ref[...])
        mo_ref[...] = m_new
        vo_ref[...] = v_new

    blk = pl.BlockSpec((tr, d), lambda i: (i, 0))
    pblk = pl.BlockSpec((tr, d), lambda i: (off + i, 0))
    out = jax.ShapeDtypeStruct((rows, d), f32)
    return pl.pallas_call(
        body, name=name, grid=(rows // tr,), in_specs=[blk, blk, blk, pblk, pblk], out_specs=(blk,) * 4,
        out_shape=(out,) * 4, compiler_params=_params(("arbitrary",)),
    )(w, m, v, p_mine, p_sib)


def reduce4(parts, *, name):
    _, rows, d = parts.shape
    tr = _tile(rows, (512, 256, 128, 64, 8))

    def body(p_ref, o_ref):
        acc = p_ref[0].astype(f32)
        for s in range(1, 4):
            acc = acc + p_ref[s].astype(f32)
        o_ref[...] = acc

    return pl.pallas_call(
        body, name=name, grid=(rows // tr,), in_specs=[pl.BlockSpec((4, tr, d), lambda i: (0, i, 0))],
        out_specs=pl.BlockSpec((tr, d), lambda i: (i, 0)), out_shape=jax.ShapeDtypeStruct((rows, d), f32),
        compiler_params=_params(("arbitrary",)),
    )(parts)


ANY = pl.BlockSpec(memory_space=pl.ANY)
MESH = pl.DeviceIdType.MESH
CHIP_FLIPS = ((0, 1), (1, 0), (1, 1))


def chip_exchange(big, small, *, scatter, name):
    rows = big.shape[-2]

    def body(big_ref, small_ref, obig_ref, osmall_ref, send_sems, recv_sems, local_sems):
        x, y, c = lax.axis_index("x"), lax.axis_index("y"), lax.axis_index("c")
        me = 2 * x + y
        mine = big_ref.at[me] if scatter else big_ref
        local = [pltpu.make_async_copy(mine, obig_ref.at[me], local_sems.at[0]),
                 pltpu.make_async_copy(small_ref, osmall_ref.at[me], local_sems.at[1])]
        for cp in local:
            cp.start()
        sends, recvs = [], []
        for k, (fx, fy) in enumerate(CHIP_FLIPS):
            px = 1 - x if fx else x
            py = 1 - y if fy else y
            chip = 2 * px + py
            for j, (src, dst) in enumerate((((big_ref.at[chip] if scatter else big_ref), obig_ref),
                                            (small_ref, osmall_ref))):
                sems = dict(send_sem=send_sems.at[2 * k + j], recv_sem=recv_sems.at[2 * k + j],
                            device_id=(px, py, c), device_id_type=MESH)
                sends.append(pltpu.make_async_remote_copy(src_ref=src, dst_ref=dst.at[me], **sems))
                recvs.append(pltpu.make_async_remote_copy(src_ref=src, dst_ref=dst.at[chip], **sems))
        for cp in sends:
            cp.start()
        for cp in recvs:
            cp.wait_recv()
        for cp in sends:
            cp.wait_send()
        for cp in local:
            cp.wait()

    return pl.pallas_call(
        body, name=name, in_specs=[ANY, ANY], out_specs=(ANY, ANY),
        out_shape=(jax.ShapeDtypeStruct((4, rows, big.shape[-1]), big.dtype),
                   jax.ShapeDtypeStruct((4,) + small.shape, small.dtype)),
        scratch_shapes=[pltpu.SemaphoreType.DMA((6,)), pltpu.SemaphoreType.DMA((6,)), pltpu.SemaphoreType.DMA((2,))],
        compiler_params=pltpu.CompilerParams(has_side_effects=True),
    )(big, small)


def sibling_swap(big, small, *, name):
    def body(big_ref, small_ref, obig_ref, osmall_ref, send_sems, recv_sems):
        peer = (lax.axis_index("x"), lax.axis_index("y"), 1 - lax.axis_index("c"))
        copies = [pltpu.make_async_remote_copy(src_ref=src, dst_ref=dst, send_sem=send_sems.at[j],
                                               recv_sem=recv_sems.at[j], device_id=peer, device_id_type=MESH)
                  for j, (src, dst) in enumerate(((big_ref, obig_ref), (small_ref, osmall_ref)))]
        for cp in copies:
            cp.start()
        for cp in copies:
            cp.wait_recv()
        for cp in copies:
            cp.wait_send()

    return pl.pallas_call(
        body, name=name, in_specs=[ANY, ANY], out_specs=(ANY, ANY),
        out_shape=(jax.ShapeDtypeStruct(big.shape, big.dtype), jax.ShapeDtypeStruct(small.shape, small.dtype)),
        scratch_shapes=[pltpu.SemaphoreType.DMA((2,)), pltpu.SemaphoreType.DMA((2,))],
        compiler_params=pltpu.CompilerParams(has_side_effects=True),
    )(big, small)


BIG = (
    ("w_in", 5904, 6144), ("w_proj_gdn", 512, 512), ("w_proj_ssd", 512, 512), ("w_proj_swa", 512, 512),
    ("w_out", 512, 512), ("w_up", 2048, 2048), ("w_down", 2048, 2048))
BIG_OFF = {}
_o = 0
for _n, _r, _p in BIG:
    BIG_OFF[_n] = _o
    _o += _p
BIG_ROWS = _o

W_NAMES = ('meta_tokens', 'norm1_w', 'w_in', 'gdn_conv_w', 'gdn_a_log', 'gdn_dt_bias', 'gdn_norm_w', 'ssd_conv_w',
           'ssd_conv_b', 'ssd_dt_bias', 'ssd_a_log', 'ssd_d', 'ssd_norm_w', 'swa_sinks', 'w_proj_gdn', 'w_proj_ssd',
           'w_proj_swa', 'w_out', 'norm2_w', 'w_up', 'w_down', 'final_norm_w')
SMALL_NAMES = tuple(n for n in W_NAMES if n not in BIG_OFF)
SMALL_SHARDED = ("meta_tokens", "gdn_conv_w", "ssd_conv_w")


def _pad_rows(a, rows):
    return jnp.pad(a, ((0, rows - a.shape[0]), (0, 0)))


def _pack_rows(parts, dtype):
    flat = jnp.concatenate([p.reshape(-1).astype(dtype) for p in parts])
    n = -(-flat.shape[0] // 8192) * 8192
    return jnp.pad(flat, (0, n - flat.shape[0])).reshape(-1, D_MODEL)


def _unpack_rows(packed, shapes):
    flat, out, o = packed.reshape(-1), [], 0
    for s in shapes:
        n = 1
        for d in s:
            n *= d
        out.append(flat[o:o + n].reshape(s))
        o += n
    return out


def _split_chips(full, axis):
    s = full.shape
    a = full.reshape(s[:axis] + (4, s[axis] // 4) + s[axis + 1:])
    return jnp.moveaxis(a, axis, 0)


def _join_chips(parts, axis):
    a = jnp.moveaxis(parts, 0, axis)
    s = a.shape
    return a.reshape(s[:axis] + (s[axis] * s[axis + 1],) + s[axis + 2:])


BIG_AXIS = {"w_in": 2, "w_proj_gdn": 1, "w_proj_ssd": 1, "w_proj_swa": 1, "w_out": 1, "w_up": 2, "w_down": 1}


def _w_in_to_padded(w):
    z = jnp.zeros(w.shape[:-1] + (112,), w.dtype)
    return jnp.concatenate([w[..., 8736:11808], w[..., 0:4096], w[..., 4112:7184], w[..., 7200:8736],
                            w[..., 4096:4112], z, w[..., 7184:7200], z], axis=-1)


def _w_in_from_padded(p):
    return jnp.concatenate([p[..., C_GQ:C_SZ], p[..., C_BA:C_BA + 16], p[..., C_SZ:C_WQ], p[..., C_SDT:C_SDT + 16],
                            p[..., C_WQ:C_BA], p[..., 0:C_GQ]], axis=-1)


def _row8(v, lane0=0, width=128):
    return jnp.pad(v[None, :], ((0, 7), (lane0, width - lane0 - v.shape[0])))


def _head_major(a, heads):
    return a.reshape(a.shape[0], heads, SWA_D).transpose(1, 0, 2)


def _from_head_major(a):
    return a.transpose(1, 0, 2).reshape(a.shape[1], -1)


def _layer_fwd(h, p, l):
    tag = f"l{l}"
    hn = rmsnorm_fwd(h, p["n1"], name=f"norm1_fwd_{tag}")
    u = mm(hn, p["w_in"], out_dtype=f32, name=f"mm_in_{tag}")
    yg, stg = gdn_fwd(u, p["gcw"], p["galog"], p["gdtb"], p["gnw"])
    ys, sts = ssd_fwd(u, p["scw"], p["sdtb"], p["salog"], p["sd"], p["snw"])
    qh = _head_major(u[:, C_WQ:C_WK], SWA_Q_HEADS)
    kh = _head_major(u[:, C_WK:C_WV], SWA_KV_HEADS)
    vh = _head_major(u[:, C_WV:C_BA], SWA_KV_HEADS)
    yw = _from_head_major(swa_fwd(qh, kh, vh, p["sink"]))
    pg = mm(yg, p["wpg"], out_dtype=f32, name=f"mm_pg_{tag}")
    ps = mm(ys, p["wps"], out_dtype=f32, name=f"mm_ps_{tag}")
    pw = mm(yw, p["wpw"], out_dtype=f32, name=f"mm_pw_{tag}")
    merged = merge_fwd(pg, ps, pw, u)
    h2 = mm(merged, p["wout"], out_dtype=f32, resid=h, name=f"mm_out_{tag}")
    hn2 = rmsnorm_fwd(h2, p["n2"], name=f"norm2_fwd_{tag}")
    a = mm(hn2, p["wup"], out_dtype=f32, name=f"mm_up_{tag}")
    r = relu2_fwd(a)
    h3 = mm(r, p["wdown"], out_dtype=f32, resid=h2, name=f"mm_down_{tag}")
    saved = dict(h=h, hn=hn, u=u, yg=yg, stg=stg, ys=ys, sts=sts, qh=qh, kh=kh, vh=vh, yw=yw, pg=pg, ps=ps, pw=pw,
                 merged=merged, h2=h2, hn2=hn2, a=a, r=r)
    return h3, saved


def _layer_bwd(dh3, p, s, l):
    tag = f"l{l}"
    g = {}

    def wgrad(act, d, name):
        return mm(act.T, d, out_dtype=bf16, name=f"wg_{name}_{tag}")

    da = relu2_bwd(s["a"], mm(dh3, p["wdown_t"], out_dtype=f32, name=f"dg_down_{tag}"))
    g["w_down"] = wgrad(s["r"], dh3, "down")
    dhn2 = mm(da, p["wup_t"], out_dtype=f32, name=f"dg_up_{tag}")
    g["w_up"] = wgrad(s["hn2"], da, "up")
    dh2, g["norm2_w"] = rmsnorm_bwd(s["h2"], p["n2"], dhn2, dh3, name=f"norm2_bwd_{tag}")
    dmerged = mm(dh2, p["wout_t"], out_dtype=f32, name=f"dg_out_{tag}")
    g["w_out"] = wgrad(s["merged"], dh2, "out")
    dpg, dps, dpw, dgl = merge_bwd(s["pg"], s["ps"], s["pw"], s["u"], dmerged)
    dyg = mm(dpg, p["wpg_t"], out_dtype=f32, name=f"dg_pg_{tag}")
    dys = mm(dps, p["wps_t"], out_dtype=f32, name=f"dg_ps_{tag}")
    dyw = mm(dpw, p["wpw_t"], out_dtype=f32, name=f"dg_pw_{tag}")
    g["w_proj_gdn"] = wgrad(s["yg"], dpg, "pg")
    g["w_proj_ssd"] = wgrad(s["ys"], dps, "ps")
    g["w_proj_swa"] = wgrad(s["yw"], dpw, "pw")

    (dq, dk, dv, dgate, dba, dtq, dtk, dtv, g["gdn_a_log"], g["gdn_dt_bias"], g["gdn_norm_w"]) = gdn_bwd(
        s["u"], p["gcw"], p["galog"], p["gdtb"], p["gnw"], s["stg"], dyg)
    g["gdn_conv_w"] = jnp.concatenate([dtq, dtk, dtv], axis=1)[:4]
    (dz, dxs, dbm, dcm, ddt, dtx, dtb, dtc, g["ssd_dt_bias"], g["ssd_a_log"], g["ssd_d"], g["ssd_norm_w"]) = ssd_bwd(
        s["u"], p["scw"], p["sdtb"], p["salog"], p["sd"], p["snw"], s["sts"], dys)
    dconv = jnp.concatenate([dtx, dtb, dtc], axis=1)
    g["ssd_conv_w"], g["ssd_conv_b"] = dconv[:4], dconv[4]
    dqh, dkh, dvh, g["swa_sinks"] = swa_bwd(s["qh"], s["kh"], s["vh"], p["sink"], _head_major(dyw, SWA_Q_HEADS))
    du = jnp.concatenate([dgl, dq, dk, dv, dgate, dz, dxs, dbm, dcm, _from_head_major(dqh), _from_head_major(dkh),
                          _from_head_major(dvh), dba.sum(0).astype(bf16), ddt.astype(bf16)], axis=1)
    dhn = mm(du, p["w_in_t"], out_dtype=f32, name=f"dg_in_{tag}")
    g["w_in"] = wgrad(s["hn"], du, "in")
    dh, g["norm1_w"] = rmsnorm_bwd(s["h"], p["n1"], dhn, dh2, name=f"norm1_bwd_{tag}")
    return dh, g


def kernel(x, meta_tokens, norm1_w, w_in, gdn_conv_w, gdn_a_log, gdn_dt_bias, gdn_norm_w, ssd_conv_w, ssd_conv_b, ssd_dt_bias, ssd_a_log, ssd_d, ssd_norm_w, swa_sinks, w_proj_gdn, w_proj_ssd, w_proj_swa, w_out, norm2_w, w_up, w_down, final_norm_w, loss_target, m_meta_tokens, m_norm1_w, m_w_in, m_gdn_conv_w, m_gdn_a_log, m_gdn_dt_bias, m_gdn_norm_w, m_ssd_conv_w, m_ssd_conv_b, m_ssd_dt_bias, m_ssd_a_log, m_ssd_d, m_ssd_norm_w, m_swa_sinks, m_w_proj_gdn, m_w_proj_ssd, m_w_proj_swa, m_w_out, m_norm2_w, m_w_up, m_w_down, m_final_norm_w, v_meta_tokens, v_norm1_w, v_w_in, v_gdn_conv_w, v_gdn_a_log, v_gdn_dt_bias, v_gdn_norm_w, v_ssd_conv_w, v_ssd_conv_b, v_ssd_dt_bias, v_ssd_a_log, v_ssd_d, v_ssd_norm_w, v_swa_sinks, v_w_proj_gdn, v_w_proj_ssd, v_w_proj_swa, v_w_out, v_norm2_w, v_w_up, v_w_down, v_final_norm_w):
    given = dict(locals())
    depth = norm1_w.shape[0]
    me = 2 * lax.axis_index("x") + lax.axis_index("y")

    wbig = jnp.concatenate([_pad_rows(given[n].reshape(-1, D_MODEL).astype(bf16), p) for n, _, p in BIG])
    wsmall = _pack_rows([given[n] for n in SMALL_SHARDED], f32)
    gbig, gsmall = chip_exchange(wbig, wsmall, scatter=False, name="gather_weights")
    full = {}
    for n, r, _ in BIG:
        parts = gbig[:, BIG_OFF[n]:BIG_OFF[n] + r].reshape((4,) + given[n].shape)
        full[n] = _join_chips(parts, BIG_AXIS[n])
    shard_shapes = [given[n].shape for n in SMALL_SHARDED]
    smalls = [gsmall[s] for s in range(4)]
    per_chip = [_unpack_rows(smalls[s], shard_shapes) for s in range(4)]
    for i, n in enumerate(SMALL_SHARDED):
        full[n] = jnp.concatenate([per_chip[s][i] for s in range(4)], axis=-1)
    w_in_p = _w_in_to_padded(full["w_in"])

    layers = []
    for l in range(depth):
        p = dict(
            n1=_row8(norm1_w[l], width=D_MODEL), n2=_row8(norm2_w[l], width=D_MODEL),
            w_in=w_in_p[l], w_in_t=w_in_p[l].T,
            gcw=jnp.pad(full["gdn_conv_w"][l], ((0, 4), (0, 0))),
            galog=_row8(gdn_a_log[l], 8), gdtb=_row8(gdn_dt_bias[l], 8), gnw=_row8(gdn_norm_w[l]),
            scw=jnp.pad(jnp.concatenate([full["ssd_conv_w"][l], ssd_conv_b[l][None]], axis=0), ((0, 3), (0, 0))),
            sdtb=_row8(ssd_dt_bias[l]), salog=_row8(ssd_a_log[l]), sd=_row8(ssd_d[l]),
            snw=_row8(ssd_norm_w[l], width=D_MODEL), sink=_row8(swa_sinks[l]),
            wpg=full["w_proj_gdn"][l], wps=full["w_proj_ssd"][l], wpw=full["w_proj_swa"][l], wout=full["w_out"][l],
            wup=full["w_up"][l], wdown=full["w_down"][l],
            wpg_t=full["w_proj_gdn"][l].T, wps_t=full["w_proj_ssd"][l].T, wpw_t=full["w_proj_swa"][l].T,
            wout_t=full["w_out"][l].T, wup_t=full["w_up"][l].T, wdown_t=full["w_down"][l].T)
        layers.append(p)

    h = jnp.concatenate([jnp.zeros((PAD, D_MODEL), f32), full["meta_tokens"], x[0]], axis=0)
    saved = []
    for l in range(depth):
        h, s = _layer_fwd(h, layers[l], l)
        saved.append(s)
    loss8, dh, dfw8 = loss_head(h, _row8(final_norm_w, width=D_MODEL), loss_target[0])
    grads = {"final_norm_w": dfw8[0]}
    per_layer = [None] * depth
    for l in reversed(range(depth)):
        dh, per_layer[l] = _layer_bwd(dh, layers[l], saved[l], l)
    grad_x = dh[HEAD_ROWS:][None]
    grads["meta_tokens"] = dh[PAD:HEAD_ROWS]
    lane = {"gdn_a_log": (8, 8), "gdn_dt_bias": (8, 8), "gdn_norm_w": (0, 128), "ssd_dt_bias": (0, 16),
            "ssd_a_log": (0, 16), "ssd_d": (0, 16), "swa_sinks": (0, 16)}
    for n in per_layer[0]:
        parts = [per_layer[l][n] for l in range(depth)]
        if n in lane:
            parts = [q[0, lane[n][0]:lane[n][0] + lane[n][1]] for q in parts]
        elif n in ("norm1_w", "norm2_w", "ssd_norm_w"):
            parts = [q[0] for q in parts]
        elif n == "w_in":
            parts = [_w_in_from_padded(q) for q in parts]
        grads[n] = jnp.stack(parts)
    loss = lax.psum(loss8[0, 0], ("x", "y", "c"))

    gb = jnp.concatenate([jnp.pad(_split_chips(grads[n], BIG_AXIS[n]).reshape(4, r, D_MODEL).astype(bf16),
                                  ((0, 0), (0, p - r), (0, 0))) for n, r, p in BIG], axis=1)
    gs = _pack_rows([grads[n] for n in SMALL_NAMES], f32)
    rb, rs = chip_exchange(gb, gs, scatter=True, name="scatter_grads")
    pb, ps_ = reduce4(rb, name="sum_chips_big"), reduce4(rs, name="sum_chips_small")
    sb, ss = sibling_swap(pb, ps_, name="swap_cores")

    out = {}
    for n, r, _ in BIG:
        shp = given[n].shape
        res = adamw(given[n].reshape(r, D_MODEL), given["m_" + n].reshape(r, D_MODEL),
                    given["v_" + n].reshape(r, D_MODEL), pb, sb, BIG_OFF[n], name=f"adamw_{n}")
        out[n] = [a.reshape(shp) for a in res]
    full_shapes = [grads[n].shape for n in SMALL_NAMES]
    mine_s, sib_s = _unpack_rows(ps_, full_shapes), _unpack_rows(ss, full_shapes)

    def local(parts):
        loc = []
        for n, a in zip(SMALL_NAMES, parts):
            if n in SMALL_SHARDED:
                sz = a.shape[-1] // 4
                a = lax.dynamic_slice_in_dim(a, me * sz, sz, axis=a.ndim - 1)
            loc.append(a)
        return _pack_rows(loc, f32)

    res = adamw(_pack_rows([given[n] for n in SMALL_NAMES], f32), _pack_rows([given["m_" + n] for n in SMALL_NAMES], f32),
                _pack_rows([given["v_" + n] for n in SMALL_NAMES], f32), local(mine_s), local(sib_s), 0,
                name="adamw_small")
    local_shapes = [given[n].shape for n in SMALL_NAMES]
    unpacked = [_unpack_rows(a, local_shapes) for a in res]
    for i, n in enumerate(SMALL_NAMES):
        out[n] = [unpacked[j][i] for j in range(4)]

    return (loss, grad_x) + tuple(out[n][j] for j in range(4) for n in W_NAMES)
```

```python
import functools

import jax
import jax.numpy as jnp
from jax import lax
from jax.experimental import pallas as pl
from jax.experimental.pallas import tpu as pltpu

f32 = jnp.float32
bf16 = jnp.bfloat16
HI = lax.Precision.HIGHEST

D_MODEL = 1024
N_META = 16
PAD = 112
HEAD_ROWS = PAD + N_META
RMS_EPS = 1e-6
L2_EPS = 1e-6
D_FF = 4 * D_MODEL

GDN_HEADS = 8
GDN_D = 128
GDN_CHUNK = 64
SSD_HEADS = 16
SSD_P = 64
SSD_GROUPS = 4
SSD_HPG = 4
SSD_N = 128
SSD_CHUNK = 128
SWA_Q_HEADS = 16
SWA_KV_HEADS = 4
SWA_REP = 4
SWA_D = 64
SWA_W = 128

C_GATE = 0
C_GQ, C_GK, C_GV, C_GG = 3072, 4096, 5120, 6144
C_SZ = 7168
C_SX, C_SB, C_SC = 8192, 9216, 9728
C_WQ, C_WK, C_WV = 10240, 11264, 11520
C_BA = 11776
C_SDT = 11904
IN_WP = 12032
IN_W = 11808

ADAM_LR, ADAM_B1, ADAM_B2, ADAM_EPS, ADAM_WD, ADAM_STEP = 0.001, 0.9, 0.999, 1e-08, 0.01, 10

VMEM_LIMIT = 56 * 1024 * 1024
BLOCK_BYTES = 3 << 19
MM_OPERAND_BYTES = 9 << 20
MM_RESIDENT_BYTES = 13 << 20

NN = (((1,), (0,)), ((), ()))
NT = (((1,), (1,)), ((), ()))
TN = (((0,), (0,)), ((), ()))


def _dot(a, b, dims=NN):
    return lax.dot_general(a.astype(bf16), b.astype(bf16), dims, preferred_element_type=f32)


def _dotx(a, b, dims=NN):
    return lax.dot_general(a, b, dims, preferred_element_type=f32, precision=HI)


def _iota(shape, axis):
    return lax.broadcasted_iota(jnp.int32, shape, axis)


def _softplus(x):
    return jnp.maximum(x, 0.0) + jnp.log1p(jnp.exp(-jnp.abs(x)))


def _silu(x):
    return x * jax.nn.sigmoid(x)


def _params(sem):
    return pltpu.CompilerParams(dimension_semantics=sem, vmem_limit_bytes=VMEM_LIMIT)


@functools.partial(jax.custom_vjp, nondiff_argnums=(1,))
def _window(x_ext, off):
    n = x_ext.shape[0] - 8
    if off == 8:
        return x_ext[8:]
    return pltpu.roll(x_ext, 8 - off, 0)[8:]


def _window_fwd(x_ext, off):
    return _window(x_ext, off), None


def _window_bwd(off, _, g):
    n, w = g.shape
    g_ext = jnp.concatenate([jnp.zeros((8, w), g.dtype), g], axis=0)
    if off == 8:
        return (g_ext,)
    return (pltpu.roll(g_ext, n + off, 0),)


_window.defvjp(_window_fwd, _window_bwd)


def _conv4(x, halo, taps):
    x_ext = jnp.concatenate([halo, x], axis=0)
    y = taps[3] * x
    for j in range(3):
        y = y + taps[j] * _window(x_ext, 5 + j)
    return y


def _blockinv_impl(a):
    n = a.shape[0]
    ri, ci = _iota((n, n), 0), _iota((n, n), 1)
    t = (ri == ci).astype(f32)
    k = 0
    while (1 << k) < n:
        sel = ((ri >> (k + 1)) == (ci >> (k + 1))) & (((ri >> k) & 1) == 1) & (((ci >> k) & 1) == 0)
        o = jnp.where(sel, a, 0.0)
        t = t - _dotx(_dotx(t, o), t)
        k += 1
    return t


@jax.custom_vjp
def _blockinv(a):
    return _blockinv_impl(a)


def _blockinv_fwd(a):
    t = _blockinv_impl(a)
    return t, t


def _blockinv_bwd(t, dt):
    return (-_dotx(_dotx(t, dt, TN), t, NT),)


_blockinv.defvjp(_blockinv_fwd, _blockinv_bwd)


def _gdn_act(xq, xk, xv, hq, hk, hv, tq, tk, tv):
    return _silu(_conv4(xq, hq, tq)), _silu(_conv4(xk, hk, tk)), _silu(_conv4(xv, hv, tv))


def _gdn_core(q, k, v, gate, mb, mg, s, ba, alog, dtb, nw, *, row0):
    c = GDN_CHUNK
    q = q * lax.rsqrt(jnp.sum(q * q, axis=1, keepdims=True) + L2_EPS) * (GDN_D ** -0.5)
    k = k * lax.rsqrt(jnp.sum(k * k, axis=1, keepdims=True) + L2_EPS)

    valid = (row0 + _iota((c, 1), 0)) >= PAD
    pick = lambda x, m: jnp.sum(x * m, axis=1, keepdims=True)
    beta = jnp.where(valid, jax.nn.sigmoid(pick(ba, mb)), 0.0)
    g1 = jnp.where(valid, -jnp.exp(pick(alog, mg)) * _softplus(pick(ba, mg) + pick(dtb, mg)), 0.0)
    g = jnp.broadcast_to(g1, (c, GDN_D))
    g64 = jnp.broadcast_to(g1, (c, c))

    ri, ci = _iota((c, c), 0), _iota((c, c), 1)
    incl = ci <= ri
    gam = _dotx(incl.astype(f32), g)
    gam_i = _dotx(incl.astype(f32), g64)
    gam_j = _dotx(jnp.ones((c, c), f32), jnp.where(ri <= ci, g64, 0.0))
    decay = jnp.where(incl, jnp.exp(jnp.where(incl, gam_i - gam_j, 0.0)), 0.0)

    kb = k * beta
    a = jnp.where(ci < ri, _dot(kb, k, NT) * decay, 0.0)
    t = _blockinv(a)
    egam = jnp.exp(gam)
    u = _dotx(t, v * beta)
    w = _dotx(t, kb * egam)
    attn = _dot(q, k, NT) * decay
    gl = jnp.sum(g, axis=0, keepdims=True)
    kt = k * jnp.exp(gl - gam)
    v_new = u - _dot(w, s)
    o = _dot(q * egam, s) + _dot(attn, v_new)
    s_out = s * jnp.exp(gl) + _dot(kt, v_new, TN)

    y = o * lax.rsqrt(jnp.mean(o * o, axis=1, keepdims=True) + RMS_EPS) * nw * _silu(gate)
    return y, s_out


def _gdn_specs(hb, nc, rev):
    w = hb * GDN_D
    cw = D_MODEL // w

    def cidx(c):
        return (nc - 1 - c) if rev else c

    def col(base):
        return pl.BlockSpec((GDN_CHUNK, w), lambda h, c: (cidx(c), base // w + h))

    def halo(base):
        return pl.BlockSpec((8, w), lambda h, c: (jnp.maximum(cidx(c) * (GDN_CHUNK // 8) - 1, 0), base // w + h))

    def taps(base):
        return pl.BlockSpec((8, w), lambda h, c: (0, base // w + h))

    ba = pl.BlockSpec((GDN_CHUNK, 128), lambda h, c: (cidx(c), C_BA // 128))
    row = pl.BlockSpec((8, 128), lambda h, c: (0, 0))
    y = pl.BlockSpec((GDN_CHUNK, w), lambda h, c: (cidx(c), h))
    st = pl.BlockSpec((1, hb, GDN_D, GDN_D), lambda h, c: (cidx(c), h, 0, 0))
    in_specs = [col(C_GQ), col(C_GK), col(C_GV), halo(C_GQ), halo(C_GK), halo(C_GV), col(C_GG), ba,
                taps(0), taps(1024), taps(2048), row, row, row]
    return in_specs, y, st, taps, row, col, ba


def _gdn_load(refs, first):
    xq, xk, xv, hq, hk, hv, gate, ba, tq, tk, tv, alog, dtb, nw = refs

    def halo(r):
        return jnp.where(first, 0.0, r[...])

    def taps(r):
        return tuple(r[j:j + 1, :] for j in range(4))

    act = (xq[...], xk[...], xv[...], halo(hq), halo(hk), halo(hv), taps(tq), taps(tk), taps(tv))
    return act, gate[...], (ba[...], alog[0:1, :], dtb[0:1, :], nw[0:1, :])


def _heads(a, hb):
    return jnp.stack([a[:, i * GDN_D:(i + 1) * GDN_D] for i in range(hb)])


def _wide(a):
    return jnp.concatenate([a[i] for i in range(a.shape[0])], axis=1)


def _head_masks(hblk, hb):
    head = hblk * hb + _iota((hb, 1, 128), 0)
    lane = _iota((hb, 1, 128), 2)
    return (lane == head).astype(f32), (lane == head + 8).astype(f32)


def _gdn_core_heads(row0):
    return jax.vmap(functools.partial(_gdn_core, row0=row0), in_axes=(0, 0, 0, 0, 0, 0, 0, None, None, None, None))


def gdn_fwd(u, conv_w8, alog8, dtb8, nw8, *, hb=8):
    t_rows = u.shape[0]
    nc = t_rows // GDN_CHUNK
    in_specs, y_spec, st_spec, *_ = _gdn_specs(hb, nc, False)

    def body(*refs):
        ins, (y_ref, st_ref), (s_scr,) = refs[:14], refs[14:16], refs[16:]
        hblk, c = pl.program_id(0), pl.program_id(1)

        @pl.when(c == 0)
        def _():
            s_scr[...] = jnp.zeros_like(s_scr)

        act, gate, shared = _gdn_load(ins, c == 0)
        s = s_scr[...]
        st_ref[0] = s
        qa, ka, va = _gdn_act(*act)
        mb, mg = _head_masks(hblk, hb)
        y, s_new = _gdn_core_heads(c * GDN_CHUNK)(_heads(qa, hb), _heads(ka, hb), _heads(va, hb), _heads(gate, hb),
                                                  mb, mg, s, *shared)
        y_ref[...] = _wide(y).astype(bf16)
        s_scr[...] = s_new

    return pl.pallas_call(
        body, name="gdn_fwd", grid=(GDN_HEADS // hb, nc),
        in_specs=in_specs, out_specs=(y_spec, st_spec),
        out_shape=(jax.ShapeDtypeStruct((t_rows, D_MODEL), bf16),
                   jax.ShapeDtypeStruct((nc, GDN_HEADS, GDN_D, GDN_D), f32)),
        scratch_shapes=[pltpu.VMEM((hb, GDN_D, GDN_D), f32)],
        compiler_params=_params(("arbitrary", "arbitrary")),
    )(u, u, u, u, u, u, u, u, conv_w8, conv_w8, conv_w8, alog8, dtb8, nw8)


def gdn_bwd(u, conv_w8, alog8, dtb8, nw8, states, dy, *, hb=8):
    t_rows = u.shape[0]
    nc = t_rows // GDN_CHUNK
    w = hb * GDN_D
    in_specs, y_spec, st_spec, taps, row, col, ba = _gdn_specs(hb, nc, True)
    nhb = GDN_HEADS // hb

    def body(*refs):
        ins, st_ref, dy_ref = refs[:14], refs[14], refs[15]
        dq_ref, dk_ref, dv_ref, dgate_ref, dba_ref, dtq_ref, dtk_ref, dtv_ref, dalog_ref, ddtb_ref, dnw_ref = refs[16:27]
        ds_scr, dh_scr = refs[27:]
        hblk, cc = pl.program_id(0), pl.program_id(1)
        c = nc - 1 - cc

        @pl.when(cc == 0)
        def _():
            ds_scr[...] = jnp.zeros_like(ds_scr)
            dh_scr[...] = jnp.zeros_like(dh_scr)
            dtq_ref[...] = jnp.zeros_like(dtq_ref)
            dtk_ref[...] = jnp.zeros_like(dtk_ref)
            dtv_ref[...] = jnp.zeros_like(dtv_ref)

        @pl.when((cc == 0) & (hblk == 0))
        def _():
            dalog_ref[...] = jnp.zeros_like(dalog_ref)
            ddtb_ref[...] = jnp.zeros_like(ddtb_ref)
            dnw_ref[...] = jnp.zeros_like(dnw_ref)

        act, gate, shared = _gdn_load(ins, c == 0)
        (qa, ka, va), vjp_act = jax.vjp(_gdn_act, *act)
        mb, mg = _head_masks(hblk, hb)
        _, vjp_core = jax.vjp(_gdn_core_heads(c * GDN_CHUNK), _heads(qa, hb), _heads(ka, hb), _heads(va, hb),
                              _heads(gate, hb), mb, mg, st_ref[0], *shared)
        dqa, dka, dva, dgate, _, _, ds, dba, dalog, ddtb, dnw = vjp_core(
            (_heads(dy_ref[...].astype(f32), hb), ds_scr[...]))
        ds_scr[...] = ds
        dxq, dxk, dxv, dhq, dhk, dhv, dtq, dtk, dtv = vjp_act((_wide(dqa), _wide(dka), _wide(dva)))
        zeros = jnp.zeros((GDN_CHUNK - 8, w), f32)
        for j, (dx, dh, out) in enumerate(((dxq, dhq, dq_ref), (dxk, dhk, dk_ref), (dxv, dhv, dv_ref))):
            out[...] = (dx + jnp.concatenate([zeros, dh_scr[j]], axis=0)).astype(bf16)
            dh_scr[j] = dh
        dgate_ref[...] = _wide(dgate).astype(bf16)
        dba_ref[0] = dba
        for dt_ref, dtaps in ((dtq_ref, dtq), (dtk_ref, dtk), (dtv_ref, dtv)):
            for j in range(4):
                dt_ref[j:j + 1, :] += dtaps[j]
        dalog_ref[0:1, :] += dalog
        ddtb_ref[0:1, :] += ddtb
        dnw_ref[0:1, :] += dnw

    def colr(base):
        return pl.BlockSpec((GDN_CHUNK, w), lambda h, c: (nc - 1 - c, base // w + h))

    out_specs = (colr(0), colr(0), colr(0), colr(0),
                 pl.BlockSpec((1, GDN_CHUNK, 128), lambda h, c: (h, nc - 1 - c, 0)),
                 taps(0), taps(0), taps(0), row, row, row)
    out_shape = (jax.ShapeDtypeStruct((t_rows, D_MODEL), bf16),) * 4 + (
        jax.ShapeDtypeStruct((nhb, t_rows, 128), f32),
        jax.ShapeDtypeStruct((8, D_MODEL), f32), jax.ShapeDtypeStruct((8, D_MODEL), f32),
        jax.ShapeDtypeStruct((8, D_MODEL), f32),
        jax.ShapeDtypeStruct((8, 128), f32), jax.ShapeDtypeStruct((8, 128), f32), jax.ShapeDtypeStruct((8, 128), f32))
    return pl.pallas_call(
        body, name="gdn_bwd", grid=(nhb, nc),
        in_specs=in_specs + [st_spec, y_spec], out_specs=out_specs, out_shape=out_shape,
        scratch_shapes=[pltpu.VMEM((hb, GDN_D, GDN_D), f32), pltpu.VMEM((3, 8, w), f32)],
        compiler_params=_params(("arbitrary", "arbitrary")),
    )(u, u, u, u, u, u, u, u, conv_w8, conv_w8, conv_w8, alog8, dtb8, nw8, states, dy)


def _ssd_act(xs_r, b_r, c_r, hx, hbm, hcm, tx, tb, tc, bx, bb, bc, *, row0):
    valid = (row0 + _iota((SSD_CHUNK, 1), 0)) >= PAD
    act = lambda x, h, t, b: jnp.where(valid, _silu(_conv4(x, h, t) + b), 0.0)
    return act(xs_r, hx, tx, bx), act(b_r, hbm, tb, bb), act(c_r, hcm, tc, bc)


def _ssd_core(xs, bm, cm, z, nw, m0, m1, m2, m3, h, dt, dtb, alog, dsk, *, row0):
    n = SSD_CHUNK
    valid = (row0 + _iota((n, 1), 0)) >= PAD
    dtp16 = _softplus(dt + dtb)
    a16 = -jnp.exp(alog)
    pick = lambda x, m: jnp.sum(x * m, axis=1, keepdims=True)
    lane_r = _iota((1, 256), 1) >> 6
    dtp = jnp.zeros((n, 256), f32)
    adt = jnp.zeros((n, 256), f32)
    dlane = jnp.zeros((1, 256), f32)
    acols = []
    for r, m in enumerate((m0, m1, m2, m3)):
        dcol = jnp.where(valid, pick(dtp16, m), 0.0)
        acol = dcol * pick(a16, m)
        dtp = jnp.where(lane_r == r, dcol, dtp)
        adt = jnp.where(lane_r == r, acol, adt)
        dlane = jnp.where(lane_r == r, pick(dsk, m), dlane)
        acols.append(acol)

    ri, ci = _iota((n, n), 0), _iota((n, n), 1)
    incl = ci <= ri
    inclf = incl.astype(f32)
    acum = _dotx(inclf, adt)
    al = jnp.sum(adt, axis=0, keepdims=True)
    xdt = xs * dtp
    cb = _dot(cm, bm, NT)
    y = _dot(cm, h) * jnp.exp(acum) + dlane * xs
    for r in range(SSD_HPG):
        ab = jnp.broadcast_to(acols[r], (n, n))
        ai = _dotx(inclf, ab)
        aj = _dotx(jnp.ones((n, n), f32), jnp.where(ri <= ci, ab, 0.0))
        lm = jnp.where(incl, jnp.exp(jnp.where(incl, ai - aj, 0.0)), 0.0)
        y = y + _dot(cb * lm, jnp.where(lane_r == r, xdt, 0.0))
    h_out = h * jnp.exp(al) + _dot(bm, jnp.exp(al - acum) * xdt, TN)
    y = y * _silu(z)
    y = y * lax.rsqrt(jnp.mean(y * y, axis=1, keepdims=True) + RMS_EPS) * nw
    return y, h_out


def _ssd_core_groups(row0):
    return jax.vmap(functools.partial(_ssd_core, row0=row0), in_axes=(0,) * 10 + (None,) * 4)


def _ssd_specs(nc, rev):
    n = SSD_CHUNK

    def cidx(c):
        return (nc - 1 - c) if rev else c

    def col(base, w):
        return pl.BlockSpec((n, w), lambda c: (cidx(c), base // w))

    def halo(base, w):
        return pl.BlockSpec((8, w), lambda c: (jnp.maximum(cidx(c) * (n // 8) - 1, 0), base // w))

    def taps(base, w):
        return pl.BlockSpec((8, w), lambda c: (0, base // w))

    row = pl.BlockSpec((8, 128), lambda c: (0, 0))
    in_specs = [col(C_SX, 1024), col(C_SB, 512), col(C_SC, 512), halo(C_SX, 1024), halo(C_SB, 512), halo(C_SC, 512),
                col(C_SZ, 1024), col(C_SDT, 128), taps(0, 1024), taps(1024, 512), taps(1536, 512), row, row, row,
                taps(0, 1024)]
    y = pl.BlockSpec((n, D_MODEL), lambda c: (cidx(c), 0))
    st = pl.BlockSpec((1, SSD_GROUPS, SSD_N, 256), lambda c: (cidx(c), 0, 0, 0))
    return in_specs, y, st, col, taps, row


def _ssd_load(refs, first):
    xs, bm, cm, hx, hbm, hcm, z, dt, tx, tb, tc, dtb, alog, dsk, nw = refs

    def halo(r):
        return jnp.where(first, 0.0, r[...])

    def taps(r):
        return tuple(r[j:j + 1, :] for j in range(4))

    act = (xs[...], bm[...], cm[...], halo(hx), halo(hbm), halo(hcm), taps(tx), taps(tb), taps(tc),
           tx[4:5, :], tb[4:5, :], tc[4:5, :])
    return act, (z[...], nw[0:1, :]), (dt[...], dtb[0:1, :], alog[0:1, :], dsk[0:1, :])


def _groups(a, w):
    return jnp.stack([a[:, i * w:(i + 1) * w] for i in range(SSD_GROUPS)])


def _ssd_masks():
    head = _iota((SSD_GROUPS, 1, 128), 0) * SSD_HPG
    lane = _iota((SSD_GROUPS, 1, 128), 2)
    return tuple((lane == head + r).astype(f32) for r in range(SSD_HPG))


def ssd_fwd(u, conv_w8, dtb8, alog8, d8, nw8):
    t_rows = u.shape[0]
    nc = t_rows // SSD_CHUNK
    in_specs, y_spec, st_spec, *_ = _ssd_specs(nc, False)

    def body(*refs):
        ins, (y_ref, st_ref), (h_scr,) = refs[:15], refs[15:17], refs[17:]
        c = pl.program_id(0)

        @pl.when(c == 0)
        def _():
            h_scr[...] = jnp.zeros_like(h_scr)

        act, (z, nw), shared = _ssd_load(ins, c == 0)
        h = h_scr[...]
        st_ref[0] = h
        xs, bm, cm = _ssd_act(*act, row0=c * SSD_CHUNK)
        y, h_new = _ssd_core_groups(c * SSD_CHUNK)(_groups(xs, 256), _groups(bm, 128), _groups(cm, 128),
                                                   _groups(z, 256), _groups(nw, 256), *_ssd_masks(), h, *shared)
        y_ref[...] = _wide(y).astype(bf16)
        h_scr[...] = h_new

    return pl.pallas_call(
        body, name="ssd_fwd", grid=(nc,), in_specs=in_specs, out_specs=(y_spec, st_spec),
        out_shape=(jax.ShapeDtypeStruct((t_rows, D_MODEL), bf16),
                   jax.ShapeDtypeStruct((nc, SSD_GROUPS, SSD_N, 256), f32)),
        scratch_shapes=[pltpu.VMEM((SSD_GROUPS, SSD_N, 256), f32)],
        compiler_params=_params(("arbitrary",)),
    )(u, u, u, u, u, u, u, u, conv_w8, conv_w8, conv_w8, dtb8, alog8, d8, nw8)


def ssd_bwd(u, conv_w8, dtb8, alog8, d8, nw8, states, dy):
    t_rows = u.shape[0]
    nc = t_rows // SSD_CHUNK
    n = SSD_CHUNK
    in_specs, y_spec, st_spec, col, taps, row = _ssd_specs(nc, True)

    def body(*refs):
        ins, st_ref, dy_ref = refs[:15], refs[15], refs[16]
        (dz_ref, dxs_ref, db_ref, dc_ref, ddt_ref, dtx_ref, dtb_ref, dtc_ref, ddtb_ref, dalog_ref, ddsk_ref,
         dnw_ref) = refs[17:29]
        dh_scr, hx_scr, hb_scr, hc_scr = refs[29:]
        cc = pl.program_id(0)
        c = nc - 1 - cc

        @pl.when(cc == 0)
        def _():
            for r in (dh_scr, hx_scr, hb_scr, hc_scr, dtx_ref, dtb_ref, dtc_ref, dnw_ref, ddtb_ref, dalog_ref, ddsk_ref):
                r[...] = jnp.zeros_like(r)

        act, (z, nw), shared = _ssd_load(ins, c == 0)
        (xs, bm, cm), vjp_act = jax.vjp(functools.partial(_ssd_act, row0=c * n), *act)
        _, vjp_core = jax.vjp(_ssd_core_groups(c * n), _groups(xs, 256), _groups(bm, 128), _groups(cm, 128),
                              _groups(z, 256), _groups(nw, 256), *_ssd_masks(), st_ref[0], *shared)
        dxa, dba, dca, dz, dnw, _, _, _, _, dh, ddt, ddtb, dalog, ddsk = vjp_core(
            (_groups(dy_ref[...].astype(f32), 256), dh_scr[...]))
        dh_scr[...] = dh
        dxs, dbm, dcm, dhx, dhb, dhc, dtx, dtb, dtc, dbx, dbb, dbc = vjp_act((_wide(dxa), _wide(dba), _wide(dca)))
        for dx, dhalo, scr, out in ((dxs, dhx, hx_scr, dxs_ref), (dbm, dhb, hb_scr, db_ref), (dcm, dhc, hc_scr, dc_ref)):
            zeros = jnp.zeros((n - 8, dx.shape[1]), f32)
            out[...] = (dx + jnp.concatenate([zeros, scr[...]], axis=0)).astype(bf16)
            scr[...] = dhalo
        dz_ref[...] = _wide(dz).astype(bf16)
        ddt_ref[...] = ddt
        for ref, dtaps, dbias in ((dtx_ref, dtx, dbx), (dtb_ref, dtb, dbb), (dtc_ref, dtc, dbc)):
            for j in range(4):
                ref[j:j + 1, :] += dtaps[j]
            ref[4:5, :] += dbias
        ddtb_ref[0:1, :] += ddtb
        dalog_ref[0:1, :] += dalog
        ddsk_ref[0:1, :] += ddsk
        dnw_ref[0:1, :] += _wide(dnw)

    def out_col(w):
        return pl.BlockSpec((n, w), lambda c: (nc - 1 - c, 0))

    out_specs = (out_col(D_MODEL), out_col(D_MODEL), out_col(512), out_col(512), out_col(128),
                 taps(0, D_MODEL), taps(0, 512), taps(0, 512), row, row, row, taps(0, D_MODEL))
    out_shape = (jax.ShapeDtypeStruct((t_rows, D_MODEL), bf16), jax.ShapeDtypeStruct((t_rows, D_MODEL), bf16),
                 jax.ShapeDtypeStruct((t_rows, 512), bf16), jax.ShapeDtypeStruct((t_rows, 512), bf16),
                 jax.ShapeDtypeStruct((t_rows, 128), f32),
                 jax.ShapeDtypeStruct((8, D_MODEL), f32), jax.ShapeDtypeStruct((8, 512), f32),
                 jax.ShapeDtypeStruct((8, 512), f32),
                 jax.ShapeDtypeStruct((8, 128), f32), jax.ShapeDtypeStruct((8, 128), f32),
                 jax.ShapeDtypeStruct((8, 128), f32), jax.ShapeDtypeStruct((8, D_MODEL), f32))
    return pl.pallas_call(
        body, name="ssd_bwd", grid=(nc,), in_specs=in_specs + [st_spec, y_spec],
        out_specs=out_specs, out_shape=out_shape,
        scratch_shapes=[pltpu.VMEM((SSD_GROUPS, SSD_N, 256), f32), pltpu.VMEM((8, D_MODEL), f32),
                        pltpu.VMEM((8, 512), f32), pltpu.VMEM((8, 512), f32)],
        compiler_params=_params(("arbitrary",)),
    )(u, u, u, u, u, u, u, u, conv_w8, conv_w8, conv_w8, dtb8, alog8, d8, nw8, states, dy)


NEG = -1e30


def _swa_core(q, kc, kp, km, vc, vp, vm, sink, *, n):
    rows = SWA_REP * SWA_W
    ri, ci = _iota((rows, SWA_W), 0) & (SWA_W - 1), _iota((rows, SWA_W), 1)
    causal = ci <= ri
    m_cur = causal & ((n >= 1) | ((ci >= PAD) & (ri >= PAD)))
    m_prev = (n >= 2) & (ci > ri)
    m_meta = (n >= 1) & (ci >= PAD)
    q = q * (SWA_D ** -0.5)
    sc = jnp.where(m_cur, _dot(q, kc, NT), NEG)
    sp = jnp.where(m_prev, _dot(q, kp, NT), NEG)
    sm = jnp.where(m_meta, _dot(q, km, NT), NEG)
    mx = jnp.maximum(jnp.maximum(jnp.max(sc, axis=1, keepdims=True), jnp.max(sp, axis=1, keepdims=True)),
                     jnp.maximum(jnp.max(sm, axis=1, keepdims=True), sink))
    mx = lax.stop_gradient(mx)
    ec, ep, em = jnp.exp(sc - mx), jnp.exp(sp - mx), jnp.exp(sm - mx)
    den = (jnp.sum(ec, axis=1, keepdims=True) + jnp.sum(ep, axis=1, keepdims=True)
           + jnp.sum(em, axis=1, keepdims=True) + jnp.exp(sink - mx))
    return (_dot(ec, vc) + _dot(ep, vp) + _dot(em, vm)) / den


def _swa_block(q16, kc, kp, km, vc, vp, vm, sink16, *, n):
    rows = SWA_REP * SWA_W
    lane = _iota((1, 128), 1)
    rep = _iota((rows, 1), 0) >> 7
    cols = []
    for h in range(SWA_KV_HEADS):
        col = jnp.zeros((rows, 1), f32)
        for r in range(SWA_REP):
            s = jnp.sum(jnp.where(lane == h * SWA_REP + r, sink16, 0.0), axis=1, keepdims=True)
            col = jnp.where(rep == r, s, col)
        cols.append(col)
    o = jax.vmap(functools.partial(_swa_core, n=n))(q16.reshape(SWA_KV_HEADS, rows, SWA_D), kc, kp, km, vc, vp, vm,
                                                    jnp.concatenate([col[None] for col in cols], axis=0))
    return o.reshape(q16.shape)


def _swa_specs(nb, rev):
    def bidx(n):
        return (nb - 1 - n) if rev else n

    q = pl.BlockSpec((SWA_Q_HEADS, SWA_W, SWA_D), lambda n: (0, bidx(n), 0))
    cur = pl.BlockSpec((SWA_KV_HEADS, SWA_W, SWA_D), lambda n: (0, bidx(n), 0))
    prev = pl.BlockSpec((SWA_KV_HEADS, SWA_W, SWA_D), lambda n: (0, jnp.maximum(bidx(n) - 1, 0), 0))
    meta = pl.BlockSpec((SWA_KV_HEADS, SWA_W, SWA_D), lambda n: (0, 0, 0))
    row = pl.BlockSpec((8, 128), lambda n: (0, 0))
    return [q, cur, prev, meta, cur, prev, meta, row], q, cur, row


def swa_fwd(q, k, v, sink8):
    t_rows = q.shape[1]
    nb = t_rows // SWA_W
    in_specs, q_spec, _, _ = _swa_specs(nb, False)

    def body(q_ref, kc, kp, km, vc, vp, vm, sink_ref, o_ref):
        o_ref[...] = _swa_block(q_ref[...], kc[...], kp[...], km[...], vc[...], vp[...], vm[...], sink_ref[0:1, :],
                                n=pl.program_id(0)).astype(bf16)

    return pl.pallas_call(
        body, name="swa_fwd", grid=(nb,), in_specs=in_specs, out_specs=q_spec,
        out_shape=jax.ShapeDtypeStruct(q.shape, bf16),
        compiler_params=_params(("arbitrary",)),
    )(q, k, k, k, v, v, v, sink8)


def swa_bwd(q, k, v, sink8, do):
    t_rows = q.shape[1]
    nb = t_rows // SWA_W
    in_specs, q_spec, kv_spec, row = _swa_specs(nb, True)

    def body(q_ref, kc, kp, km, vc, vp, vm, sink_ref, do_ref, dq_ref, dk_ref, dv_ref, dsink_ref,
             dkp_scr, dvp_scr, dkm_scr, dvm_scr):
        nn = pl.program_id(0)
        n = nb - 1 - nn

        @pl.when(nn == 0)
        def _():
            for r in (dkp_scr, dvp_scr, dkm_scr, dvm_scr, dsink_ref):
                r[...] = jnp.zeros_like(r)

        fn = functools.partial(_swa_block, n=n)
        _, vjp = jax.vjp(fn, q_ref[...], kc[...], kp[...], km[...], vc[...], vp[...], vm[...], sink_ref[0:1, :])
        dq, dkc, dkp, dkm, dvc, dvp, dvm, dsink = vjp(do_ref[...].astype(f32))
        dq_ref[...] = dq.astype(bf16)
        dkm_scr[...] += dkm
        dvm_scr[...] += dvm
        first = n == 0
        dk_ref[...] = (dkc + dkp_scr[...] + jnp.where(first, dkm_scr[...], 0.0)).astype(bf16)
        dv_ref[...] = (dvc + dvp_scr[...] + jnp.where(first, dvm_scr[...], 0.0)).astype(bf16)
        dkp_scr[...] = dkp
        dvp_scr[...] = dvp
        dsink_ref[0:1, :] += dsink

    kv_shape = jax.ShapeDtypeStruct(k.shape, bf16)
    return pl.pallas_call(
        body, name="swa_bwd", grid=(nb,), in_specs=in_specs + [q_spec],
        out_specs=(q_spec, kv_spec, kv_spec, row),
        out_shape=(jax.ShapeDtypeStruct(q.shape, bf16), kv_shape, kv_shape, jax.ShapeDtypeStruct((8, 128), f32)),
        scratch_shapes=[pltpu.VMEM((SWA_KV_HEADS, SWA_W, SWA_D), f32)] * 4,
        compiler_params=_params(("arbitrary",)),
    )(q, k, k, k, v, v, v, sink8, do)


def _tile(dim, prefs):
    for p in prefs:
        if dim % p == 0:
            return p
    return dim


def mm(a, b, *, out_dtype, name, resid=None):
    m, k = a.shape
    n = b.shape[1]
    rhs_stays = k * 2 * 1024 > MM_OPERAND_BYTES
    if rhs_stays:
        tn = _tile(n, tuple(p for p in (512, 256, 128) if p * k * 2 <= MM_RESIDENT_BYTES))
        tm = _tile(m, tuple(p for p in (512, 384, 256, 128) if p * k * 2 <= MM_OPERAND_BYTES // 2))
        grid = (n // tn, m // tm)
        ij = lambda o, i: (i, o)
    else:
        tm = _tile(m, tuple(p for p in (1408, 1024, 512, 384, 256, 128) if p * k * 2 <= MM_OPERAND_BYTES))
        tn = _tile(n, tuple(p for p in (512, 256, 128) if p * k * 2 <= MM_OPERAND_BYTES // 2))
        grid = (m // tm, n // tn)
        ij = lambda o, i: (o, i)

    def body(*refs):
        a_ref, b_ref = refs[:2]
        o = _dot(a_ref[...], b_ref[...])
        if resid is not None:
            o = o + refs[2][...]
        refs[-1][...] = o.astype(out_dtype)

    in_specs = [pl.BlockSpec((tm, k), lambda o, i: (ij(o, i)[0], 0)), pl.BlockSpec((k, tn), lambda o, i: (0, ij(o, i)[1]))]
    args = [a, b]
    if resid is not None:
        in_specs.append(pl.BlockSpec((tm, tn), ij))
        args.append(resid)
    return pl.pallas_call(
        body, name=name, grid=grid, in_specs=in_specs, out_specs=pl.BlockSpec((tm, tn), ij),
        out_shape=jax.ShapeDtypeStruct((m, n), out_dtype),
        compiler_params=_params(("parallel", "parallel")),
    )(*args)


def _rows(t_rows):
    return _tile(t_rows, (384, 256, 128))


def _rmsnorm(h, w):
    return h * lax.rsqrt(jnp.mean(h * h, axis=1, keepdims=True) + RMS_EPS) * w


def rmsnorm_fwd(h, w8, *, name):
    t_rows, d = h.shape
    tr = _rows(t_rows)

    def body(h_ref, w_ref, o_ref):
        o_ref[...] = _rmsnorm(h_ref[...], w_ref[0:1, :]).astype(bf16)

    blk = pl.BlockSpec((tr, d), lambda i: (i, 0))
    return pl.pallas_call(
        body, name=name, grid=(t_rows // tr,), in_specs=[blk, pl.BlockSpec((8, d), lambda i: (0, 0))], out_specs=blk,
        out_shape=jax.ShapeDtypeStruct((t_rows, d), bf16), compiler_params=_params(("arbitrary",)),
    )(h, w8)


def rmsnorm_bwd(h, w8, dhn, dres, *, name):
    t_rows, d = h.shape
    tr = _rows(t_rows)

    def body(h_ref, w_ref, dhn_ref, dres_ref, dh_ref, dw_ref):
        @pl.when(pl.program_id(0) == 0)
        def _():
            dw_ref[...] = jnp.zeros_like(dw_ref)

        _, vjp = jax.vjp(_rmsnorm, h_ref[...], w_ref[0:1, :])
        dh, dw = vjp(dhn_ref[...])
        dh_ref[...] = dh + dres_ref[...]
        dw_ref[0:1, :] += dw

    blk = pl.BlockSpec((tr, d), lambda i: (i, 0))
    wblk = pl.BlockSpec((8, d), lambda i: (0, 0))
    return pl.pallas_call(
        body, name=name, grid=(t_rows // tr,), in_specs=[blk, wblk, blk, blk], out_specs=(blk, wblk),
        out_shape=(jax.ShapeDtypeStruct((t_rows, d), f32), jax.ShapeDtypeStruct((8, d), f32)),
        compiler_params=_params(("arbitrary",)),
    )(h, w8, dhn, dres)


def _merge(pg, ps, pw, la, lb, lc):
    return jax.nn.sigmoid(la) * pg + jax.nn.sigmoid(lb) * ps + jax.nn.sigmoid(lc) * pw


def _merge_specs(t_rows):
    tr = _rows(t_rows)
    blk = pl.BlockSpec((tr, D_MODEL), lambda i: (i, 0))
    gate = [pl.BlockSpec((tr, D_MODEL), functools.partial(lambda i, j: (i, j), j=C_GATE // D_MODEL + j)) for j in range(3)]
    return tr, blk, gate


def merge_fwd(pg, ps, pw, u):
    t_rows = pg.shape[0]
    tr, blk, gate = _merge_specs(t_rows)

    def body(pg_ref, ps_ref, pw_ref, la, lb, lc, o_ref):
        o_ref[...] = _merge(pg_ref[...], ps_ref[...], pw_ref[...], la[...], lb[...], lc[...]).astype(bf16)

    return pl.pallas_call(
        body, name="merge_fwd", grid=(t_rows // tr,), in_specs=[blk, blk, blk] + gate, out_specs=blk,
        out_shape=jax.ShapeDtypeStruct((t_rows, D_MODEL), bf16), compiler_params=_params(("arbitrary",)),
    )(pg, ps, pw, u, u, u)


def merge_bwd(pg, ps, pw, u, dmerged):
    t_rows = pg.shape[0]
    tr, blk, gate = _merge_specs(t_rows)

    def body(pg_ref, ps_ref, pw_ref, la, lb, lc, dm_ref, dpg_ref, dps_ref, dpw_ref, dl_ref):
        _, vjp = jax.vjp(_merge, pg_ref[...], ps_ref[...], pw_ref[...], la[...], lb[...], lc[...])
        dpg, dps, dpw, dla, dlb, dlc = vjp(dm_ref[...])
        dpg_ref[...] = dpg.astype(bf16)
        dps_ref[...] = dps.astype(bf16)
        dpw_ref[...] = dpw.astype(bf16)
        for j, dl in enumerate((dla, dlb, dlc)):
            dl_ref[:, j * D_MODEL:(j + 1) * D_MODEL] = dl.astype(bf16)

    act = jax.ShapeDtypeStruct((t_rows, D_MODEL), bf16)
    return pl.pallas_call(
        body, name="merge_bwd", grid=(t_rows // tr,), in_specs=[blk, blk, blk] + gate + [blk],
        out_specs=(blk, blk, blk, pl.BlockSpec((tr, 3 * D_MODEL), lambda i: (i, 0))),
        out_shape=(act, act, act, jax.ShapeDtypeStruct((t_rows, 3 * D_MODEL), bf16)),
        compiler_params=_params(("arbitrary",)),
    )(pg, ps, pw, u, u, u, dmerged)


def relu2_fwd(a):
    t_rows, d = a.shape
    tr = _rows(t_rows)

    def body(a_ref, o_ref):
        r = jnp.maximum(a_ref[...], 0.0)
        o_ref[...] = (r * r).astype(bf16)

    blk = pl.BlockSpec((tr, d), lambda i: (i, 0))
    return pl.pallas_call(
        body, name="relu2_fwd", grid=(t_rows // tr,), in_specs=[blk], out_specs=blk,
        out_shape=jax.ShapeDtypeStruct((t_rows, d), bf16), compiler_params=_params(("arbitrary",)),
    )(a)


def relu2_bwd(a, dr):
    t_rows, d = a.shape
    tr = _rows(t_rows)

    def body(a_ref, dr_ref, o_ref):
        o_ref[...] = (dr_ref[...] * 2.0 * jnp.maximum(a_ref[...], 0.0)).astype(bf16)

    blk = pl.BlockSpec((tr, d), lambda i: (i, 0))
    return pl.pallas_call(
        body, name="relu2_bwd", grid=(t_rows // tr,), in_specs=[blk, blk], out_specs=blk,
        out_shape=jax.ShapeDtypeStruct((t_rows, d), bf16), compiler_params=_params(("arbitrary",)),
    )(a, dr)


def loss_head(h, w8, target):
    t_rows, d = h.shape
    tr = HEAD_ROWS

    def loss_fn(hb, w, tgt):
        err = _rmsnorm(hb, w) - tgt
        return 0.5 * jnp.sum(err * err) / d

    def body(h_ref, w_ref, t_ref, loss_ref, dh_ref, dw_ref):
        i = pl.program_id(0)

        @pl.when(i == 0)
        def _():
            loss_ref[...] = jnp.zeros_like(loss_ref)
            dw_ref[...] = jnp.zeros_like(dw_ref)
            dh_ref[...] = jnp.zeros_like(dh_ref)

        @pl.when(i > 0)
        def _():
            val, (dh, dw) = jax.value_and_grad(loss_fn, argnums=(0, 1))(h_ref[...], w_ref[0:1, :], t_ref[...])
            loss_ref[...] += val
            dh_ref[...] = dh
            dw_ref[0:1, :] += dw

    blk = pl.BlockSpec((tr, d), lambda i: (i, 0))
    wblk = pl.BlockSpec((8, d), lambda i: (0, 0))
    return pl.pallas_call(
        body, name="loss_head", grid=(t_rows // tr,),
        in_specs=[blk, wblk, pl.BlockSpec((tr, d), lambda i: (jnp.maximum(i - 1, 0), 0))],
        out_specs=(pl.BlockSpec((8, 128), lambda i: (0, 0)), blk, wblk),
        out_shape=(jax.ShapeDtypeStruct((8, 128), f32), jax.ShapeDtypeStruct((t_rows, d), f32),
                   jax.ShapeDtypeStruct((8, d), f32)),
        compiler_params=_params(("arbitrary",)),
    )(h, w8, target)


def adamw(w, m, v, p_mine, p_sib, row_off, *, name):
    rows, d = w.shape
    tr = _tile(rows, tuple(p for p in (512, 256, 128, 64, 16, 8) if p * d * 4 <= BLOCK_BYTES))
    assert row_off % tr == 0
    off = row_off // tr
    c1 = 1.0 - ADAM_B1 ** ADAM_STEP
    c2 = 1.0 - ADAM_B2 ** ADAM_STEP

    def body(w_ref, m_ref, v_ref, pa_ref, pb_ref, g_ref, d_ref, mo_ref, vo_ref):
        g = pa_ref[...] + pb_ref[...]
        m_new = ADAM_B1 * m_ref[...] + (1.0 - ADAM_B1) * g
        v_new = ADAM_B2 * v_ref[...] + (1.0 - ADAM_B2) * (g * g)
        g_ref[...] = g
        d_ref[...] = -ADAM_LR * ((m_new / c1) / (jnp.sqrt(v_new / c2) + ADAM_EPS) + ADAM_WD * w_ref[...])
        mo_ref[...] = m_new
        vo_ref[...] = v_new

    blk = pl.BlockSpec((tr, d), lambda i: (i, 0))
    pblk = pl.BlockSpec((tr, d), lambda i: (off + i, 0))
    out = jax.ShapeDtypeStruct((rows, d), f32)
    return pl.pallas_call(
        body, name=name, grid=(rows // tr,), in_specs=[blk, blk, blk, pblk, pblk], out_specs=(blk,) * 4,
        out_shape=(out,) * 4, compiler_params=_params(("arbitrary",)),
    )(w, m, v, p_mine, p_sib)


def reduce4(parts, *, name):
    _, rows, d = parts.shape
    tr = _tile(rows, tuple(p for p in (512, 256, 128, 64, 8) if p * d * 4 <= BLOCK_BYTES))

    def body(p_ref, o_ref):
        acc = p_ref[0].astype(f32)
        for s in range(1, 4):
            acc = acc + p_ref[s].astype(f32)
        o_ref[...] = acc

    return pl.pallas_call(
        body, name=name, grid=(rows // tr,), in_specs=[pl.BlockSpec((4, tr, d), lambda i: (0, i, 0))],
        out_specs=pl.BlockSpec((tr, d), lambda i: (i, 0)), out_shape=jax.ShapeDtypeStruct((rows, d), f32),
        compiler_params=_params(("arbitrary",)),
    )(parts)


ANY = pl.BlockSpec(memory_space=pl.ANY)
MESH = pl.DeviceIdType.MESH
CHIP_FLIPS = ((0, 1), (1, 0), (1, 1))


def chip_exchange(bufs, scatter, *, name):
    nb = len(bufs)

    def body(*refs):
        ins, outs = refs[:nb], refs[nb:2 * nb]
        send_sems, recv_sems, local_sems = refs[2 * nb:]
        x, y, c = lax.axis_index("x"), lax.axis_index("y"), lax.axis_index("c")
        me = 2 * x + y
        local = [pltpu.make_async_copy(ins[j].at[me] if scatter[j] else ins[j], outs[j].at[me], local_sems.at[j])
                 for j in range(nb)]
        for cp in local:
            cp.start()
        sends, recvs = [], []
        for k, (fx, fy) in enumerate(CHIP_FLIPS):
            px = 1 - x if fx else x
            py = 1 - y if fy else y
            chip = 2 * px + py
            for j in range(nb):
                src = ins[j].at[chip] if scatter[j] else ins[j]
                sems = dict(send_sem=send_sems.at[nb * k + j], recv_sem=recv_sems.at[nb * k + j],
                            device_id=(px, py, c), device_id_type=MESH)
                sends.append(pltpu.make_async_remote_copy(src_ref=src, dst_ref=outs[j].at[me], **sems))
                recvs.append(pltpu.make_async_remote_copy(src_ref=src, dst_ref=outs[j].at[chip], **sems))
        for cp in sends:
            cp.start()
        for cp in recvs:
            cp.wait_recv()
        for cp in sends:
            cp.wait_send()
        for cp in local:
            cp.wait()

    out_shape = tuple(jax.ShapeDtypeStruct(b.shape if s else (4,) + b.shape, b.dtype) for b, s in zip(bufs, scatter))
    return pl.pallas_call(
        body, name=name, in_specs=[ANY] * nb, out_specs=(ANY,) * nb, out_shape=out_shape,
        scratch_shapes=[pltpu.SemaphoreType.DMA((3 * nb,)), pltpu.SemaphoreType.DMA((3 * nb,)),
                        pltpu.SemaphoreType.DMA((nb,))],
        compiler_params=pltpu.CompilerParams(has_side_effects=True),
    )(*bufs)


def sibling_swap(bufs, *, name):
    nb = len(bufs)

    def body(*refs):
        ins, outs, (send_sems, recv_sems) = refs[:nb], refs[nb:2 * nb], refs[2 * nb:]
        peer = (lax.axis_index("x"), lax.axis_index("y"), 1 - lax.axis_index("c"))
        copies = [pltpu.make_async_remote_copy(src_ref=ins[j], dst_ref=outs[j], send_sem=send_sems.at[j],
                                               recv_sem=recv_sems.at[j], device_id=peer, device_id_type=MESH)
                  for j in range(nb)]
        for cp in copies:
            cp.start()
        for cp in copies:
            cp.wait_recv()
        for cp in copies:
            cp.wait_send()

    return pl.pallas_call(
        body, name=name, in_specs=[ANY] * nb, out_specs=(ANY,) * nb,
        out_shape=tuple(jax.ShapeDtypeStruct(b.shape, b.dtype) for b in bufs),
        scratch_shapes=[pltpu.SemaphoreType.DMA((nb,)), pltpu.SemaphoreType.DMA((nb,))],
        compiler_params=pltpu.CompilerParams(has_side_effects=True),
    )(*bufs)


BIG = (
    ("w_proj_gdn", 512), ("w_proj_ssd", 512), ("w_proj_swa", 512), ("w_out", 512), ("w_up", 2048), ("w_down", 2048))
BIG_OFF = {}
_o = 0
for _n, _r in BIG:
    BIG_OFF[_n] = _o
    _o += _r
BIG_ROWS = _o
W_IN_SHARD = IN_W // 4

W_NAMES = ('meta_tokens', 'norm1_w', 'w_in', 'gdn_conv_w', 'gdn_a_log', 'gdn_dt_bias', 'gdn_norm_w', 'ssd_conv_w',
           'ssd_conv_b', 'ssd_dt_bias', 'ssd_a_log', 'ssd_d', 'ssd_norm_w', 'swa_sinks', 'w_proj_gdn', 'w_proj_ssd',
           'w_proj_swa', 'w_out', 'norm2_w', 'w_up', 'w_down', 'final_norm_w')
SMALL_NAMES = tuple(n for n in W_NAMES if n not in BIG_OFF and n != "w_in")
SMALL_SHARDED = ("meta_tokens", "gdn_conv_w", "ssd_conv_w")


def _pad_rows(a, rows):
    return jnp.pad(a, ((0, rows - a.shape[0]), (0, 0)))


def _pack_rows(parts, dtype):
    flat = jnp.concatenate([p.reshape(-1).astype(dtype) for p in parts])
    n = -(-flat.shape[0] // 8192) * 8192
    return jnp.pad(flat, (0, n - flat.shape[0])).reshape(-1, D_MODEL)


def _unpack_rows(packed, shapes):
    flat, out, o = packed.reshape(-1), [], 0
    for s in shapes:
        n = 1
        for d in s:
            n *= d
        out.append(flat[o:o + n].reshape(s))
        o += n
    return out


def _split_chips(full, axis):
    s = full.shape
    a = full.reshape(s[:axis] + (4, s[axis] // 4) + s[axis + 1:])
    return jnp.moveaxis(a, axis, 0)


def _join_chips(parts, axis):
    a = jnp.moveaxis(parts, 0, axis)
    s = a.shape
    return a.reshape(s[:axis] + (s[axis] * s[axis + 1],) + s[axis + 2:])


BIG_AXIS = {"w_in": 2, "w_proj_gdn": 1, "w_proj_ssd": 1, "w_proj_swa": 1, "w_out": 1, "w_up": 2, "w_down": 1}


def _w_in_to_padded(w):
    z = jnp.zeros(w.shape[:-1] + (112,), w.dtype)
    return jnp.concatenate([w[..., 8736:11808], w[..., 0:4096], w[..., 4112:7184], w[..., 7200:8736],
                            w[..., 4096:4112], z, w[..., 7184:7200], z], axis=-1)


def _w_in_from_padded(p):
    return jnp.concatenate([p[..., C_GQ:C_SZ], p[..., C_BA:C_BA + 16], p[..., C_SZ:C_WQ], p[..., C_SDT:C_SDT + 16],
                            p[..., C_WQ:C_BA], p[..., 0:C_GQ]], axis=-1)


def _row8(v, lane0=0, width=128):
    return jnp.pad(v[None, :], ((0, 7), (lane0, width - lane0 - v.shape[0])))


def _head_major(a, heads):
    return a.reshape(a.shape[0], heads, SWA_D).transpose(1, 0, 2)


def _from_head_major(a):
    return a.transpose(1, 0, 2).reshape(a.shape[1], -1)


def _layer_fwd(h, p, l):
    tag = f"l{l}"
    hn = rmsnorm_fwd(h, p["n1"], name=f"norm1_fwd_{tag}")
    u = mm(hn, p["w_in"], out_dtype=f32, name=f"mm_in_{tag}")
    yg, stg = gdn_fwd(u, p["gcw"], p["galog"], p["gdtb"], p["gnw"])
    ys, sts = ssd_fwd(u, p["scw"], p["sdtb"], p["salog"], p["sd"], p["snw"])
    qh = _head_major(u[:, C_WQ:C_WK], SWA_Q_HEADS)
    kh = _head_major(u[:, C_WK:C_WV], SWA_KV_HEADS)
    vh = _head_major(u[:, C_WV:C_BA], SWA_KV_HEADS)
    yw = _from_head_major(swa_fwd(qh, kh, vh, p["sink"]))
    pg = mm(yg, p["wpg"], out_dtype=f32, name=f"mm_pg_{tag}")
    ps = mm(ys, p["wps"], out_dtype=f32, name=f"mm_ps_{tag}")
    pw = mm(yw, p["wpw"], out_dtype=f32, name=f"mm_pw_{tag}")
    merged = merge_fwd(pg, ps, pw, u)
    h2 = mm(merged, p["wout"], out_dtype=f32, resid=h, name=f"mm_out_{tag}")
    hn2 = rmsnorm_fwd(h2, p["n2"], name=f"norm2_fwd_{tag}")
    a = mm(hn2, p["wup"], out_dtype=f32, name=f"mm_up_{tag}")
    r = relu2_fwd(a)
    h3 = mm(r, p["wdown"], out_dtype=f32, resid=h2, name=f"mm_down_{tag}")
    saved = dict(h=h, hn=hn, u=u, yg=yg, stg=stg, ys=ys, sts=sts, qh=qh, kh=kh, vh=vh, yw=yw, pg=pg, ps=ps, pw=pw,
                 merged=merged, h2=h2, hn2=hn2, a=a, r=r)
    return h3, saved


def _layer_bwd(dh3, p, s, l):
    tag = f"l{l}"
    g = {}

    def wgrad(act, d, name):
        return mm(act.T, d, out_dtype=bf16, name=f"wg_{name}_{tag}")

    da = relu2_bwd(s["a"], mm(dh3, p["wdown_t"], out_dtype=f32, name=f"dg_down_{tag}"))
    g["w_down"] = wgrad(s["r"], dh3, "down")
    dhn2 = mm(da, p["wup_t"], out_dtype=f32, name=f"dg_up_{tag}")
    g["w_up"] = wgrad(s["hn2"], da, "up")
    dh2, g["norm2_w"] = rmsnorm_bwd(s["h2"], p["n2"], dhn2, dh3, name=f"norm2_bwd_{tag}")
    dmerged = mm(dh2, p["wout_t"], out_dtype=f32, name=f"dg_out_{tag}")
    g["w_out"] = wgrad(s["merged"], dh2, "out")
    dpg, dps, dpw, dgl = merge_bwd(s["pg"], s["ps"], s["pw"], s["u"], dmerged)
    dyg = mm(dpg, p["wpg_t"], out_dtype=f32, name=f"dg_pg_{tag}")
    dys = mm(dps, p["wps_t"], out_dtype=f32, name=f"dg_ps_{tag}")
    dyw = mm(dpw, p["wpw_t"], out_dtype=f32, name=f"dg_pw_{tag}")
    g["w_proj_gdn"] = wgrad(s["yg"], dpg, "pg")
    g["w_proj_ssd"] = wgrad(s["ys"], dps, "ps")
    g["w_proj_swa"] = wgrad(s["yw"], dpw, "pw")

    (dq, dk, dv, dgate, dba, dtq, dtk, dtv, g["gdn_a_log"], g["gdn_dt_bias"], g["gdn_norm_w"]) = gdn_bwd(
        s["u"], p["gcw"], p["galog"], p["gdtb"], p["gnw"], s["stg"], dyg)
    g["gdn_conv_w"] = jnp.concatenate([dtq, dtk, dtv], axis=1)[:4]
    (dz, dxs, dbm, dcm, ddt, dtx, dtb, dtc, g["ssd_dt_bias"], g["ssd_a_log"], g["ssd_d"], g["ssd_norm_w"]) = ssd_bwd(
        s["u"], p["scw"], p["sdtb"], p["salog"], p["sd"], p["snw"], s["sts"], dys)
    dconv = jnp.concatenate([dtx, dtb, dtc], axis=1)
    g["ssd_conv_w"], g["ssd_conv_b"] = dconv[:4], dconv[4]
    dqh, dkh, dvh, g["swa_sinks"] = swa_bwd(s["qh"], s["kh"], s["vh"], p["sink"], _head_major(dyw, SWA_Q_HEADS))
    du = jnp.concatenate([dgl, dq, dk, dv, dgate, dz, dxs, dbm, dcm, _from_head_major(dqh), _from_head_major(dkh),
                          _from_head_major(dvh), dba.sum(0).astype(bf16), ddt.astype(bf16)], axis=1)
    dhn = mm(du, p["w_in_t"], out_dtype=f32, name=f"dg_in_{tag}")
    g["w_in"] = wgrad(s["hn"], du, "in")
    dh, g["norm1_w"] = rmsnorm_bwd(s["h"], p["n1"], dhn, dh2, name=f"norm1_bwd_{tag}")
    return dh, g


def kernel(x, meta_tokens, norm1_w, w_in, gdn_conv_w, gdn_a_log, gdn_dt_bias, gdn_norm_w, ssd_conv_w, ssd_conv_b, ssd_dt_bias, ssd_a_log, ssd_d, ssd_norm_w, swa_sinks, w_proj_gdn, w_proj_ssd, w_proj_swa, w_out, norm2_w, w_up, w_down, final_norm_w, loss_target, m_meta_tokens, m_norm1_w, m_w_in, m_gdn_conv_w, m_gdn_a_log, m_gdn_dt_bias, m_gdn_norm_w, m_ssd_conv_w, m_ssd_conv_b, m_ssd_dt_bias, m_ssd_a_log, m_ssd_d, m_ssd_norm_w, m_swa_sinks, m_w_proj_gdn, m_w_proj_ssd, m_w_proj_swa, m_w_out, m_norm2_w, m_w_up, m_w_down, m_final_norm_w, v_meta_tokens, v_norm1_w, v_w_in, v_gdn_conv_w, v_gdn_a_log, v_gdn_dt_bias, v_gdn_norm_w, v_ssd_conv_w, v_ssd_conv_b, v_ssd_dt_bias, v_ssd_a_log, v_ssd_d, v_ssd_norm_w, v_swa_sinks, v_w_proj_gdn, v_w_proj_ssd, v_w_proj_swa, v_w_out, v_norm2_w, v_w_up, v_w_down, v_final_norm_w):
    given = dict(locals())
    depth = norm1_w.shape[0]
    me = 2 * lax.axis_index("x") + lax.axis_index("y")

    w_in_rows = depth * D_MODEL
    wa = w_in.reshape(w_in_rows, W_IN_SHARD).astype(bf16)
    wbig = jnp.concatenate([given[n].reshape(-1, D_MODEL).astype(bf16) for n, _ in BIG])
    wsmall = _pack_rows([given[n] for n in SMALL_SHARDED], f32)
    ga, gbig, gsmall = chip_exchange([wa, wbig, wsmall], (False,) * 3, name="gather_weights")
    full = {"w_in": _join_chips(ga.reshape((4,) + w_in.shape), 2)}
    for n, r in BIG:
        parts = gbig[:, BIG_OFF[n]:BIG_OFF[n] + r].reshape((4,) + given[n].shape)
        full[n] = _join_chips(parts, BIG_AXIS[n])
    shard_shapes = [given[n].shape for n in SMALL_SHARDED]
    smalls = [gsmall[s] for s in range(4)]
    per_chip = [_unpack_rows(smalls[s], shard_shapes) for s in range(4)]
    for i, n in enumerate(SMALL_SHARDED):
        full[n] = jnp.concatenate([per_chip[s][i] for s in range(4)], axis=-1)
    w_in_p = _w_in_to_padded(full["w_in"])

    layers = []
    for l in range(depth):
        p = dict(
            n1=_row8(norm1_w[l], width=D_MODEL), n2=_row8(norm2_w[l], width=D_MODEL),
            w_in=w_in_p[l], w_in_t=w_in_p[l].T,
            gcw=jnp.pad(full["gdn_conv_w"][l], ((0, 4), (0, 0))),
            galog=_row8(gdn_a_log[l], 8), gdtb=_row8(gdn_dt_bias[l], 8), gnw=_row8(gdn_norm_w[l]),
            scw=jnp.pad(jnp.concatenate([full["ssd_conv_w"][l], ssd_conv_b[l][None]], axis=0), ((0, 3), (0, 0))),
            sdtb=_row8(ssd_dt_bias[l]), salog=_row8(ssd_a_log[l]), sd=_row8(ssd_d[l]),
            snw=_row8(ssd_norm_w[l], width=D_MODEL), sink=_row8(swa_sinks[l]),
            wpg=full["w_proj_gdn"][l], wps=full["w_proj_ssd"][l], wpw=full["w_proj_swa"][l], wout=full["w_out"][l],
            wup=full["w_up"][l], wdown=full["w_down"][l],
            wpg_t=full["w_proj_gdn"][l].T, wps_t=full["w_proj_ssd"][l].T, wpw_t=full["w_proj_swa"][l].T,
            wout_t=full["w_out"][l].T, wup_t=full["w_up"][l].T, wdown_t=full["w_down"][l].T)
        layers.append(p)

    h = jnp.concatenate([jnp.zeros((PAD, D_MODEL), f32), full["meta_tokens"], x[0]], axis=0)
    saved = []
    for l in range(depth):
        h, s = _layer_fwd(h, layers[l], l)
        saved.append(s)
    loss8, dh, dfw8 = loss_head(h, _row8(final_norm_w, width=D_MODEL), loss_target[0])
    grads = {"final_norm_w": dfw8[0]}
    per_layer = [None] * depth
    for l in reversed(range(depth)):
        dh, per_layer[l] = _layer_bwd(dh, layers[l], saved[l], l)
    grad_x = dh[HEAD_ROWS:][None]
    grads["meta_tokens"] = dh[PAD:HEAD_ROWS]
    lane = {"gdn_a_log": (8, 8), "gdn_dt_bias": (8, 8), "gdn_norm_w": (0, 128), "ssd_dt_bias": (0, 16),
            "ssd_a_log": (0, 16), "ssd_d": (0, 16), "swa_sinks": (0, 16)}
    for n in per_layer[0]:
        parts = [per_layer[l][n] for l in range(depth)]
        if n in lane:
            parts = [q[0, lane[n][0]:lane[n][0] + lane[n][1]] for q in parts]
        elif n in ("norm1_w", "norm2_w", "ssd_norm_w"):
            parts = [q[0] for q in parts]
        elif n == "w_in":
            parts = [_w_in_from_padded(q) for q in parts]
        grads[n] = jnp.stack(parts)
    loss = lax.psum(loss8[0, 0], ("x", "y", "c"))

    gwa = _split_chips(grads["w_in"], 2).reshape(4, w_in_rows, W_IN_SHARD)
    gb = jnp.concatenate([_split_chips(grads[n], BIG_AXIS[n]).reshape(4, r, D_MODEL) for n, r in BIG], axis=1)
    gs = _pack_rows([grads[n] for n in SMALL_NAMES], f32)
    ra, rb, rs = chip_exchange([gwa, gb, gs], (True, True, False), name="scatter_grads")
    pa = reduce4(ra, name="sum_chips_w_in")
    pb, ps_ = reduce4(rb, name="sum_chips_big"), reduce4(rs, name="sum_chips_small")
    sa, sb, ss = sibling_swap([pa, pb, ps_], name="swap_cores")

    out = {}
    res = adamw(*[given[pre + "w_in"].reshape(w_in_rows, W_IN_SHARD) for pre in ("", "m_", "v_")], pa, sa, 0,
                name="adamw_w_in")
    out["w_in"] = [a.reshape(w_in.shape) for a in res]
    for n, r in BIG:
        shp = given[n].shape
        res = adamw(given[n].reshape(r, D_MODEL), given["m_" + n].reshape(r, D_MODEL),
                    given["v_" + n].reshape(r, D_MODEL), pb, sb, BIG_OFF[n], name=f"adamw_{n}")
        out[n] = [a.reshape(shp) for a in res]
    full_shapes = [grads[n].shape for n in SMALL_NAMES]
    mine_s, sib_s = _unpack_rows(ps_, full_shapes), _unpack_rows(ss, full_shapes)

    def local(parts):
        loc = []
        for n, a in zip(SMALL_NAMES, parts):
            if n in SMALL_SHARDED:
                sz = a.shape[-1] // 4
                a = lax.dynamic_slice_in_dim(a, me * sz, sz, axis=a.ndim - 1)
            loc.append(a)
        return _pack_rows(loc, f32)

    res = adamw(_pack_rows([given[n] for n in SMALL_NAMES], f32), _pack_rows([given["m_" + n] for n in SMALL_NAMES], f32),
                _pack_rows([given["v_" + n] for n in SMALL_NAMES], f32), local(mine_s), local(sib_s), 0,
                name="adamw_small")
    local_shapes = [given[n].shape for n in SMALL_NAMES]
    unpacked = [_unpack_rows(a, local_shapes) for a in res]
    for i, n in enumerate(SMALL_NAMES):
        out[n] = [unpacked[j][i] for j in range(4)]

    return (loss, grad_x) + tuple(out[n][j] for j in range(4) for n in W_NAMES)
```

```python
import functools

import jax
import jax.numpy as jnp
from jax import lax
from jax.experimental import pallas as pl
from jax.experimental.pallas import tpu as pltpu

f32 = jnp.float32
bf16 = jnp.bfloat16
HI = lax.Precision.HIGHEST

D_MODEL = 1024
N_META = 16
PAD = 112
HEAD_ROWS = PAD + N_META
RMS_EPS = 1e-6
L2_EPS = 1e-6
D_FF = 4 * D_MODEL

GDN_HEADS = 8
GDN_D = 128
GDN_CHUNK = 64
SSD_HEADS = 16
SSD_P = 64
SSD_GROUPS = 4
SSD_HPG = 4
SSD_N = 128
SSD_CHUNK = 128
SWA_Q_HEADS = 16
SWA_KV_HEADS = 4
SWA_REP = 4
SWA_D = 64
SWA_W = 128

C_GATE = 0
C_GQ, C_GK, C_GV, C_GG = 3072, 4096, 5120, 6144
C_SZ = 7168
C_SX, C_SB, C_SC = 8192, 9216, 9728
C_WQ, C_WK, C_WV = 10240, 11264, 11520
C_BA = 11776
C_SDT = 11904
IN_WP = 12032
IN_W = 11808

ADAM_LR, ADAM_B1, ADAM_B2, ADAM_EPS, ADAM_WD, ADAM_STEP = 0.001, 0.9, 0.999, 1e-08, 0.01, 10

VMEM_LIMIT = 56 * 1024 * 1024
BLOCK_BYTES = 3 << 19
MM_OPERAND_BYTES = 9 << 20
MM_RESIDENT_BYTES = 13 << 20

NN = (((1,), (0,)), ((), ()))
NT = (((1,), (1,)), ((), ()))
TN = (((0,), (0,)), ((), ()))


def _dot(a, b, dims=NN):
    return lax.dot_general(a.astype(bf16), b.astype(bf16), dims, preferred_element_type=f32)


def _dotx(a, b, dims=NN):
    return lax.dot_general(a, b, dims, preferred_element_type=f32, precision=HI)


def _iota(shape, axis):
    return lax.broadcasted_iota(jnp.int32, shape, axis)


def _softplus(x):
    return jnp.maximum(x, 0.0) + jnp.log1p(jnp.exp(-jnp.abs(x)))


def _silu(x):
    return x * jax.nn.sigmoid(x)


def _params(sem):
    return pltpu.CompilerParams(dimension_semantics=sem, vmem_limit_bytes=VMEM_LIMIT)


@functools.partial(jax.custom_vjp, nondiff_argnums=(1,))
def _window(x_ext, off):
    n = x_ext.shape[0] - 8
    if off == 8:
        return x_ext[8:]
    return pltpu.roll(x_ext, 8 - off, 0)[8:]


def _window_fwd(x_ext, off):
    return _window(x_ext, off), None


def _window_bwd(off, _, g):
    n, w = g.shape
    g_ext = jnp.concatenate([jnp.zeros((8, w), g.dtype), g], axis=0)
    if off == 8:
        return (g_ext,)
    return (pltpu.roll(g_ext, n + off, 0),)


_window.defvjp(_window_fwd, _window_bwd)


def _conv4(x, halo, taps):
    x_ext = jnp.concatenate([halo, x], axis=0)
    y = taps[3] * x
    for j in range(3):
        y = y + taps[j] * _window(x_ext, 5 + j)
    return y


def _blockinv_impl(a):
    n = a.shape[0]
    ri, ci = _iota((n, n), 0), _iota((n, n), 1)
    t = (ri == ci).astype(f32)
    k = 0
    while (1 << k) < n:
        sel = ((ri >> (k + 1)) == (ci >> (k + 1))) & (((ri >> k) & 1) == 1) & (((ci >> k) & 1) == 0)
        o = jnp.where(sel, a, 0.0)
        t = t - _dotx(_dotx(t, o), t)
        k += 1
    return t


@jax.custom_vjp
def _blockinv(a):
    return _blockinv_impl(a)


def _blockinv_fwd(a):
    t = _blockinv_impl(a)
    return t, t


def _blockinv_bwd(t, dt):
    return (-_dotx(_dotx(t, dt, TN), t, NT),)


_blockinv.defvjp(_blockinv_fwd, _blockinv_bwd)


def _gdn_act(xq, xk, xv, hq, hk, hv, tq, tk, tv):
    return _silu(_conv4(xq, hq, tq)), _silu(_conv4(xk, hk, tk)), _silu(_conv4(xv, hv, tv))


def _gdn_core(q, k, v, gate, mb, mg, s, ba, alog, dtb, nw, *, row0):
    c = GDN_CHUNK
    q = q * lax.rsqrt(jnp.sum(q * q, axis=1, keepdims=True) + L2_EPS) * (GDN_D ** -0.5)
    k = k * lax.rsqrt(jnp.sum(k * k, axis=1, keepdims=True) + L2_EPS)

    valid = (row0 + _iota((c, 1), 0)) >= PAD
    pick = lambda x, m: jnp.sum(x * m, axis=1, keepdims=True)
    beta = jnp.where(valid, jax.nn.sigmoid(pick(ba, mb)), 0.0)
    g1 = jnp.where(valid, -jnp.exp(pick(alog, mg)) * _softplus(pick(ba, mg) + pick(dtb, mg)), 0.0)
    g = jnp.broadcast_to(g1, (c, GDN_D))
    g64 = jnp.broadcast_to(g1, (c, c))

    ri, ci = _iota((c, c), 0), _iota((c, c), 1)
    incl = ci <= ri
    gam = _dotx(incl.astype(f32), g)
    gam_i = _dotx(incl.astype(f32), g64)
    gam_j = _dotx(jnp.ones((c, c), f32), jnp.where(ri <= ci, g64, 0.0))
    decay = jnp.where(incl, jnp.exp(jnp.where(incl, gam_i - gam_j, 0.0)), 0.0)

    kb = k * beta
    a = jnp.where(ci < ri, _dot(kb, k, NT) * decay, 0.0)
    t = _blockinv(a)
    egam = jnp.exp(gam)
    u = _dotx(t, v * beta)
    w = _dotx(t, kb * egam)
    attn = _dot(q, k, NT) * decay
    gl = jnp.sum(g, axis=0, keepdims=True)
    kt = k * jnp.exp(gl - gam)
    v_new = u - _dot(w, s)
    o = _dot(q * egam, s) + _dot(attn, v_new)
    s_out = s * jnp.exp(gl) + _dot(kt, v_new, TN)

    y = o * lax.rsqrt(jnp.mean(o * o, axis=1, keepdims=True) + RMS_EPS) * nw * _silu(gate)
    return y, s_out


def _gdn_specs(hb, nc, rev):
    w = hb * GDN_D
    cw = D_MODEL // w

    def cidx(c):
        return (nc - 1 - c) if rev else c

    def col(base):
        return pl.BlockSpec((GDN_CHUNK, w), lambda h, c: (cidx(c), base // w + h))

    def halo(base):
        return pl.BlockSpec((8, w), lambda h, c: (jnp.maximum(cidx(c) * (GDN_CHUNK // 8) - 1, 0), base // w + h))

    def taps(base):
        return pl.BlockSpec((8, w), lambda h, c: (0, base // w + h))

    ba = pl.BlockSpec((GDN_CHUNK, 128), lambda h, c: (cidx(c), C_BA // 128))
    row = pl.BlockSpec((8, 128), lambda h, c: (0, 0))
    y = pl.BlockSpec((GDN_CHUNK, w), lambda h, c: (cidx(c), h))
    st = pl.BlockSpec((1, hb, GDN_D, GDN_D), lambda h, c: (cidx(c), h, 0, 0))
    in_specs = [col(C_GQ), col(C_GK), col(C_GV), halo(C_GQ), halo(C_GK), halo(C_GV), col(C_GG), ba,
                taps(0), taps(1024), taps(2048), row, row, row]
    return in_specs, y, st, taps, row, col, ba


def _gdn_load(refs, first):
    xq, xk, xv, hq, hk, hv, gate, ba, tq, tk, tv, alog, dtb, nw = refs

    def halo(r):
        return jnp.where(first, 0.0, r[...])

    def taps(r):
        return tuple(r[j:j + 1, :] for j in range(4))

    act = (xq[...], xk[...], xv[...], halo(hq), halo(hk), halo(hv), taps(tq), taps(tk), taps(tv))
    return act, gate[...], (ba[...], alog[0:1, :], dtb[0:1, :], nw[0:1, :])


def _heads(a, hb):
    return jnp.stack([a[:, i * GDN_D:(i + 1) * GDN_D] for i in range(hb)])


def _wide(a):
    return jnp.concatenate([a[i] for i in range(a.shape[0])], axis=1)


def _head_masks(hblk, hb):
    head = hblk * hb + _iota((hb, 1, 128), 0)
    lane = _iota((hb, 1, 128), 2)
    return (lane == head).astype(f32), (lane == head + 8).astype(f32)


def _gdn_core_heads(row0):
    return jax.vmap(functools.partial(_gdn_core, row0=row0), in_axes=(0, 0, 0, 0, 0, 0, 0, None, None, None, None))


def gdn_fwd(u, conv_w8, alog8, dtb8, nw8, *, hb=8):
    t_rows = u.shape[0]
    nc = t_rows // GDN_CHUNK
    in_specs, y_spec, st_spec, *_ = _gdn_specs(hb, nc, False)

    def body(*refs):
        ins, (y_ref, st_ref), (s_scr,) = refs[:14], refs[14:16], refs[16:]
        hblk, c = pl.program_id(0), pl.program_id(1)

        @pl.when(c == 0)
        def _():
            s_scr[...] = jnp.zeros_like(s_scr)

        act, gate, shared = _gdn_load(ins, c == 0)
        s = s_scr[...]
        st_ref[0] = s
        qa, ka, va = _gdn_act(*act)
        mb, mg = _head_masks(hblk, hb)
        y, s_new = _gdn_core_heads(c * GDN_CHUNK)(_heads(qa, hb), _heads(ka, hb), _heads(va, hb), _heads(gate, hb),
                                                  mb, mg, s, *shared)
        y_ref[...] = _wide(y).astype(bf16)
        s_scr[...] = s_new

    return pl.pallas_call(
        body, name="gdn_fwd", grid=(GDN_HEADS // hb, nc),
        in_specs=in_specs, out_specs=(y_spec, st_spec),
        out_shape=(jax.ShapeDtypeStruct((t_rows, D_MODEL), bf16),
                   jax.ShapeDtypeStruct((nc, GDN_HEADS, GDN_D, GDN_D), f32)),
        scratch_shapes=[pltpu.VMEM((hb, GDN_D, GDN_D), f32)],
        compiler_params=_params(("arbitrary", "arbitrary")),
    )(u, u, u, u, u, u, u, u, conv_w8, conv_w8, conv_w8, alog8, dtb8, nw8)


def gdn_bwd(u, conv_w8, alog8, dtb8, nw8, states, dy, *, hb=8):
    t_rows = u.shape[0]
    nc = t_rows // GDN_CHUNK
    w = hb * GDN_D
    in_specs, y_spec, st_spec, taps, row, col, ba = _gdn_specs(hb, nc, True)
    nhb = GDN_HEADS // hb

    def body(*refs):
        ins, st_ref, dy_ref = refs[:14], refs[14], refs[15]
        dq_ref, dk_ref, dv_ref, dgate_ref, dba_ref, dtq_ref, dtk_ref, dtv_ref, dalog_ref, ddtb_ref, dnw_ref = refs[16:27]
        ds_scr, dh_scr = refs[27:]
        hblk, cc = pl.program_id(0), pl.program_id(1)
        c = nc - 1 - cc

        @pl.when(cc == 0)
        def _():
            ds_scr[...] = jnp.zeros_like(ds_scr)
            dh_scr[...] = jnp.zeros_like(dh_scr)
            dtq_ref[...] = jnp.zeros_like(dtq_ref)
            dtk_ref[...] = jnp.zeros_like(dtk_ref)
            dtv_ref[...] = jnp.zeros_like(dtv_ref)

        @pl.when((cc == 0) & (hblk == 0))
        def _():
            dalog_ref[...] = jnp.zeros_like(dalog_ref)
            ddtb_ref[...] = jnp.zeros_like(ddtb_ref)
            dnw_ref[...] = jnp.zeros_like(dnw_ref)

        act, gate, shared = _gdn_load(ins, c == 0)
        (qa, ka, va), vjp_act = jax.vjp(_gdn_act, *act)
        mb, mg = _head_masks(hblk, hb)
        _, vjp_core = jax.vjp(_gdn_core_heads(c * GDN_CHUNK), _heads(qa, hb), _heads(ka, hb), _heads(va, hb),
                              _heads(gate, hb), mb, mg, st_ref[0], *shared)
        dqa, dka, dva, dgate, _, _, ds, dba, dalog, ddtb, dnw = vjp_core(
            (_heads(dy_ref[...].astype(f32), hb), ds_scr[...]))
        ds_scr[...] = ds
        dxq, dxk, dxv, dhq, dhk, dhv, dtq, dtk, dtv = vjp_act((_wide(dqa), _wide(dka), _wide(dva)))
        zeros = jnp.zeros((GDN_CHUNK - 8, w), f32)
        for j, (dx, dh, out) in enumerate(((dxq, dhq, dq_ref), (dxk, dhk, dk_ref), (dxv, dhv, dv_ref))):
            out[...] = (dx + jnp.concatenate([zeros, dh_scr[j]], axis=0)).astype(bf16)
            dh_scr[j] = dh
        dgate_ref[...] = _wide(dgate).astype(bf16)
        dba_ref[0] = dba
        for dt_ref, dtaps in ((dtq_ref, dtq), (dtk_ref, dtk), (dtv_ref, dtv)):
            for j in range(4):
                dt_ref[j:j + 1, :] += dtaps[j]
        dalog_ref[0:1, :] += dalog
        ddtb_ref[0:1, :] += ddtb
        dnw_ref[0:1, :] += dnw

    def colr(base):
        return pl.BlockSpec((GDN_CHUNK, w), lambda h, c: (nc - 1 - c, base // w + h))

    out_specs = (colr(0), colr(0), colr(0), colr(0),
                 pl.BlockSpec((1, GDN_CHUNK, 128), lambda h, c: (h, nc - 1 - c, 0)),
                 taps(0), taps(0), taps(0), row, row, row)
    out_shape = (jax.ShapeDtypeStruct((t_rows, D_MODEL), bf16),) * 4 + (
        jax.ShapeDtypeStruct((nhb, t_rows, 128), f32),
        jax.ShapeDtypeStruct((8, D_MODEL), f32), jax.ShapeDtypeStruct((8, D_MODEL), f32),
        jax.ShapeDtypeStruct((8, D_MODEL), f32),
        jax.ShapeDtypeStruct((8, 128), f32), jax.ShapeDtypeStruct((8, 128), f32), jax.ShapeDtypeStruct((8, 128), f32))
    return pl.pallas_call(
        body, name="gdn_bwd", grid=(nhb, nc),
        in_specs=in_specs + [st_spec, y_spec], out_specs=out_specs, out_shape=out_shape,
        scratch_shapes=[pltpu.VMEM((hb, GDN_D, GDN_D), f32), pltpu.VMEM((3, 8, w), f32)],
        compiler_params=_params(("arbitrary", "arbitrary")),
    )(u, u, u, u, u, u, u, u, conv_w8, conv_w8, conv_w8, alog8, dtb8, nw8, states, dy)


def _ssd_act(xs_r, b_r, c_r, hx, hbm, hcm, tx, tb, tc, bx, bb, bc, *, row0):
    valid = (row0 + _iota((SSD_CHUNK, 1), 0)) >= PAD
    act = lambda x, h, t, b: jnp.where(valid, _silu(_conv4(x, h, t) + b), 0.0)
    return act(xs_r, hx, tx, bx), act(b_r, hbm, tb, bb), act(c_r, hcm, tc, bc)


def _ssd_core(xs, bm, cm, z, nw, m0, m1, m2, m3, h, dt, dtb, alog, dsk, *, row0):
    n = SSD_CHUNK
    valid = (row0 + _iota((n, 1), 0)) >= PAD
    dtp16 = _softplus(dt + dtb)
    a16 = -jnp.exp(alog)
    pick = lambda x, m: jnp.sum(x * m, axis=1, keepdims=True)
    lane_r = _iota((1, 256), 1) >> 6
    dtp = jnp.zeros((n, 256), f32)
    adt = jnp.zeros((n, 256), f32)
    dlane = jnp.zeros((1, 256), f32)
    acols = []
    for r, m in enumerate((m0, m1, m2, m3)):
        dcol = jnp.where(valid, pick(dtp16, m), 0.0)
        acol = dcol * pick(a16, m)
        dtp = jnp.where(lane_r == r, dcol, dtp)
        adt = jnp.where(lane_r == r, acol, adt)
        dlane = jnp.where(lane_r == r, pick(dsk, m), dlane)
        acols.append(acol)

    ri, ci = _iota((n, n), 0), _iota((n, n), 1)
    incl = ci <= ri
    inclf = incl.astype(f32)
    acum = _dotx(inclf, adt)
    al = jnp.sum(adt, axis=0, keepdims=True)
    xdt = xs * dtp
    cb = _dot(cm, bm, NT)
    y = _dot(cm, h) * jnp.exp(acum) + dlane * xs
    for r in range(SSD_HPG):
        ab = jnp.broadcast_to(acols[r], (n, n))
        ai = _dotx(inclf, ab)
        aj = _dotx(jnp.ones((n, n), f32), jnp.where(ri <= ci, ab, 0.0))
        lm = jnp.where(incl, jnp.exp(jnp.where(incl, ai - aj, 0.0)), 0.0)
        y = y + _dot(cb * lm, jnp.where(lane_r == r, xdt, 0.0))
    h_out = h * jnp.exp(al) + _dot(bm, jnp.exp(al - acum) * xdt, TN)
    y = y * _silu(z)
    y = y * lax.rsqrt(jnp.mean(y * y, axis=1, keepdims=True) + RMS_EPS) * nw
    return y, h_out


def _ssd_core_groups(row0):
    return jax.vmap(functools.partial(_ssd_core, row0=row0), in_axes=(0,) * 10 + (None,) * 4)


def _ssd_specs(nc, rev):
    n = SSD_CHUNK

    def cidx(c):
        return (nc - 1 - c) if rev else c

    def col(base, w):
        return pl.BlockSpec((n, w), lambda c: (cidx(c), base // w))

    def halo(base, w):
        return pl.BlockSpec((8, w), lambda c: (jnp.maximum(cidx(c) * (n // 8) - 1, 0), base // w))

    def taps(base, w):
        return pl.BlockSpec((8, w), lambda c: (0, base // w))

    row = pl.BlockSpec((8, 128), lambda c: (0, 0))
    in_specs = [col(C_SX, 1024), col(C_SB, 512), col(C_SC, 512), halo(C_SX, 1024), halo(C_SB, 512), halo(C_SC, 512),
                col(C_SZ, 1024), col(C_SDT, 128), taps(0, 1024), taps(1024, 512), taps(1536, 512), row, row, row,
                taps(0, 1024)]
    y = pl.BlockSpec((n, D_MODEL), lambda c: (cidx(c), 0))
    st = pl.BlockSpec((1, SSD_GROUPS, SSD_N, 256), lambda c: (cidx(c), 0, 0, 0))
    return in_specs, y, st, col, taps, row


def _ssd_load(refs, first):
    xs, bm, cm, hx, hbm, hcm, z, dt, tx, tb, tc, dtb, alog, dsk, nw = refs

    def halo(r):
        return jnp.where(first, 0.0, r[...])

    def taps(r):
        return tuple(r[j:j + 1, :] for j in range(4))

    act = (xs[...], bm[...], cm[...], halo(hx), halo(hbm), halo(hcm), taps(tx), taps(tb), taps(tc),
           tx[4:5, :], tb[4:5, :], tc[4:5, :])
    return act, (z[...], nw[0:1, :]), (dt[...], dtb[0:1, :], alog[0:1, :], dsk[0:1, :])


def _groups(a, w):
    return jnp.stack([a[:, i * w:(i + 1) * w] for i in range(SSD_GROUPS)])


def _ssd_masks():
    head = _iota((SSD_GROUPS, 1, 128), 0) * SSD_HPG
    lane = _iota((SSD_GROUPS, 1, 128), 2)
    return tuple((lane == head + r).astype(f32) for r in range(SSD_HPG))


def ssd_fwd(u, conv_w8, dtb8, alog8, d8, nw8):
    t_rows = u.shape[0]
    nc = t_rows // SSD_CHUNK
    in_specs, y_spec, st_spec, *_ = _ssd_specs(nc, False)

    def body(*refs):
        ins, (y_ref, st_ref), (h_scr,) = refs[:15], refs[15:17], refs[17:]
        c = pl.program_id(0)

        @pl.when(c == 0)
        def _():
            h_scr[...] = jnp.zeros_like(h_scr)

        act, (z, nw), shared = _ssd_load(ins, c == 0)
        h = h_scr[...]
        st_ref[0] = h
        xs, bm, cm = _ssd_act(*act, row0=c * SSD_CHUNK)
        y, h_new = _ssd_core_groups(c * SSD_CHUNK)(_groups(xs, 256), _groups(bm, 128), _groups(cm, 128),
                                                   _groups(z, 256), _groups(nw, 256), *_ssd_masks(), h, *shared)
        y_ref[...] = _wide(y).astype(bf16)
        h_scr[...] = h_new

    return pl.pallas_call(
        body, name="ssd_fwd", grid=(nc,), in_specs=in_specs, out_specs=(y_spec, st_spec),
        out_shape=(jax.ShapeDtypeStruct((t_rows, D_MODEL), bf16),
                   jax.ShapeDtypeStruct((nc, SSD_GROUPS, SSD_N, 256), f32)),
        scratch_shapes=[pltpu.VMEM((SSD_GROUPS, SSD_N, 256), f32)],
        compiler_params=_params(("arbitrary",)),
    )(u, u, u, u, u, u, u, u, conv_w8, conv_w8, conv_w8, dtb8, alog8, d8, nw8)


def ssd_bwd(u, conv_w8, dtb8, alog8, d8, nw8, states, dy):
    t_rows = u.shape[0]
    nc = t_rows // SSD_CHUNK
    n = SSD_CHUNK
    in_specs, y_spec, st_spec, col, taps, row = _ssd_specs(nc, True)

    def body(*refs):
        ins, st_ref, dy_ref = refs[:15], refs[15], refs[16]
        (dz_ref, dxs_ref, db_ref, dc_ref, ddt_ref, dtx_ref, dtb_ref, dtc_ref, ddtb_ref, dalog_ref, ddsk_ref,
         dnw_ref) = refs[17:29]
        dh_scr, hx_scr, hb_scr, hc_scr = refs[29:]
        cc = pl.program_id(0)
        c = nc - 1 - cc

        @pl.when(cc == 0)
        def _():
            for r in (dh_scr, hx_scr, hb_scr, hc_scr, dtx_ref, dtb_ref, dtc_ref, dnw_ref, ddtb_ref, dalog_ref, ddsk_ref):
                r[...] = jnp.zeros_like(r)

        act, (z, nw), shared = _ssd_load(ins, c == 0)
        (xs, bm, cm), vjp_act = jax.vjp(functools.partial(_ssd_act, row0=c * n), *act)
        _, vjp_core = jax.vjp(_ssd_core_groups(c * n), _groups(xs, 256), _groups(bm, 128), _groups(cm, 128),
                              _groups(z, 256), _groups(nw, 256), *_ssd_masks(), st_ref[0], *shared)
        dxa, dba, dca, dz, dnw, _, _, _, _, dh, ddt, ddtb, dalog, ddsk = vjp_core(
            (_groups(dy_ref[...].astype(f32), 256), dh_scr[...]))
        dh_scr[...] = dh
        dxs, dbm, dcm, dhx, dhb, dhc, dtx, dtb, dtc, dbx, dbb, dbc = vjp_act((_wide(dxa), _wide(dba), _wide(dca)))
        for dx, dhalo, scr, out in ((dxs, dhx, hx_scr, dxs_ref), (dbm, dhb, hb_scr, db_ref), (dcm, dhc, hc_scr, dc_ref)):
            zeros = jnp.zeros((n - 8, dx.shape[1]), f32)
            out[...] = (dx + jnp.concatenate([zeros, scr[...]], axis=0)).astype(bf16)
            scr[...] = dhalo
        dz_ref[...] = _wide(dz).astype(bf16)
        ddt_ref[...] = ddt
        for ref, dtaps, dbias in ((dtx_ref, dtx, dbx), (dtb_ref, dtb, dbb), (dtc_ref, dtc, dbc)):
            for j in range(4):
                ref[j:j + 1, :] += dtaps[j]
            ref[4:5, :] += dbias
        ddtb_ref[0:1, :] += ddtb
        dalog_ref[0:1, :] += dalog
        ddsk_ref[0:1, :] += ddsk
        dnw_ref[0:1, :] += _wide(dnw)

    def out_col(w):
        return pl.BlockSpec((n, w), lambda c: (nc - 1 - c, 0))

    out_specs = (out_col(D_MODEL), out_col(D_MODEL), out_col(512), out_col(512), out_col(128),
                 taps(0, D_MODEL), taps(0, 512), taps(0, 512), row, row, row, taps(0, D_MODEL))
    out_shape = (jax.ShapeDtypeStruct((t_rows, D_MODEL), bf16), jax.ShapeDtypeStruct((t_rows, D_MODEL), bf16),
                 jax.ShapeDtypeStruct((t_rows, 512), bf16), jax.ShapeDtypeStruct((t_rows, 512), bf16),
                 jax.ShapeDtypeStruct((t_rows, 128), f32),
                 jax.ShapeDtypeStruct((8, D_MODEL), f32), jax.ShapeDtypeStruct((8, 512), f32),
                 jax.ShapeDtypeStruct((8, 512), f32),
                 jax.ShapeDtypeStruct((8, 128), f32), jax.ShapeDtypeStruct((8, 128), f32),
                 jax.ShapeDtypeStruct((8, 128), f32), jax.ShapeDtypeStruct((8, D_MODEL), f32))
    return pl.pallas_call(
        body, name="ssd_bwd", grid=(nc,), in_specs=in_specs + [st_spec, y_spec],
        out_specs=out_specs, out_shape=out_shape,
        scratch_shapes=[pltpu.VMEM((SSD_GROUPS, SSD_N, 256), f32), pltpu.VMEM((8, D_MODEL), f32),
                        pltpu.VMEM((8, 512), f32), pltpu.VMEM((8, 512), f32)],
        compiler_params=_params(("arbitrary",)),
    )(u, u, u, u, u, u, u, u, conv_w8, conv_w8, conv_w8, dtb8, alog8, d8, nw8, states, dy)


NEG = -1e30


def _swa_core(q, kc, kp, km, vc, vp, vm, sink, *, n):
    rows = SWA_REP * SWA_W
    ri, ci = _iota((rows, SWA_W), 0) & (SWA_W - 1), _iota((rows, SWA_W), 1)
    causal = ci <= ri
    m_cur = causal & ((n >= 1) | ((ci >= PAD) & (ri >= PAD)))
    m_prev = (n >= 2) & (ci > ri)
    m_meta = (n >= 1) & (ci >= PAD)
    q = q * (SWA_D ** -0.5)
    sc = jnp.where(m_cur, _dot(q, kc, NT), NEG)
    sp = jnp.where(m_prev, _dot(q, kp, NT), NEG)
    sm = jnp.where(m_meta, _dot(q, km, NT), NEG)
    mx = jnp.maximum(jnp.maximum(jnp.max(sc, axis=1, keepdims=True), jnp.max(sp, axis=1, keepdims=True)),
                     jnp.maximum(jnp.max(sm, axis=1, keepdims=True), sink))
    mx = lax.stop_gradient(mx)
    ec, ep, em = jnp.exp(sc - mx), jnp.exp(sp - mx), jnp.exp(sm - mx)
    den = (jnp.sum(ec, axis=1, keepdims=True) + jnp.sum(ep, axis=1, keepdims=True)
           + jnp.sum(em, axis=1, keepdims=True) + jnp.exp(sink - mx))
    return (_dot(ec, vc) + _dot(ep, vp) + _dot(em, vm)) / den


def _swa_block(q16, kc, kp, km, vc, vp, vm, sink16, *, n):
    rows = SWA_REP * SWA_W
    lane = _iota((1, 128), 1)
    rep = _iota((rows, 1), 0) >> 7
    cols = []
    for h in range(SWA_KV_HEADS):
        col = jnp.zeros((rows, 1), f32)
        for r in range(SWA_REP):
            s = jnp.sum(jnp.where(lane == h * SWA_REP + r, sink16, 0.0), axis=1, keepdims=True)
            col = jnp.where(rep == r, s, col)
        cols.append(col)
    o = jax.vmap(functools.partial(_swa_core, n=n))(q16.reshape(SWA_KV_HEADS, rows, SWA_D), kc, kp, km, vc, vp, vm,
                                                    jnp.concatenate([col[None] for col in cols], axis=0))
    return o.reshape(q16.shape)


def _swa_specs(nb, rev):
    def bidx(n):
        return (nb - 1 - n) if rev else n

    q = pl.BlockSpec((SWA_Q_HEADS, SWA_W, SWA_D), lambda n: (0, bidx(n), 0))
    cur = pl.BlockSpec((SWA_KV_HEADS, SWA_W, SWA_D), lambda n: (0, bidx(n), 0))
    prev = pl.BlockSpec((SWA_KV_HEADS, SWA_W, SWA_D), lambda n: (0, jnp.maximum(bidx(n) - 1, 0), 0))
    meta = pl.BlockSpec((SWA_KV_HEADS, SWA_W, SWA_D), lambda n: (0, 0, 0))
    row = pl.BlockSpec((8, 128), lambda n: (0, 0))
    return [q, cur, prev, meta, cur, prev, meta, row], q, cur, row


def swa_fwd(q, k, v, sink8):
    t_rows = q.shape[1]
    nb = t_rows // SWA_W
    in_specs, q_spec, _, _ = _swa_specs(nb, False)

    def body(q_ref, kc, kp, km, vc, vp, vm, sink_ref, o_ref):
        o_ref[...] = _swa_block(q_ref[...], kc[...], kp[...], km[...], vc[...], vp[...], vm[...], sink_ref[0:1, :],
                                n=pl.program_id(0)).astype(bf16)

    return pl.pallas_call(
        body, name="swa_fwd", grid=(nb,), in_specs=in_specs, out_specs=q_spec,
        out_shape=jax.ShapeDtypeStruct(q.shape, bf16),
        compiler_params=_params(("arbitrary",)),
    )(q, k, k, k, v, v, v, sink8)


def swa_bwd(q, k, v, sink8, do):
    t_rows = q.shape[1]
    nb = t_rows // SWA_W
    in_specs, q_spec, kv_spec, row = _swa_specs(nb, True)

    def body(q_ref, kc, kp, km, vc, vp, vm, sink_ref, do_ref, dq_ref, dk_ref, dv_ref, dsink_ref,
             dkp_scr, dvp_scr, dkm_scr, dvm_scr):
        nn = pl.program_id(0)
        n = nb - 1 - nn

        @pl.when(nn == 0)
        def _():
            for r in (dkp_scr, dvp_scr, dkm_scr, dvm_scr, dsink_ref):
                r[...] = jnp.zeros_like(r)

        fn = functools.partial(_swa_block, n=n)
        _, vjp = jax.vjp(fn, q_ref[...], kc[...], kp[...], km[...], vc[...], vp[...], vm[...], sink_ref[0:1, :])
        dq, dkc, dkp, dkm, dvc, dvp, dvm, dsink = vjp(do_ref[...].astype(f32))
        dq_ref[...] = dq.astype(bf16)
        dkm_scr[...] += dkm
        dvm_scr[...] += dvm
        first = n == 0
        dk_ref[...] = (dkc + dkp_scr[...] + jnp.where(first, dkm_scr[...], 0.0)).astype(bf16)
        dv_ref[...] = (dvc + dvp_scr[...] + jnp.where(first, dvm_scr[...], 0.0)).astype(bf16)
        dkp_scr[...] = dkp
        dvp_scr[...] = dvp
        dsink_ref[0:1, :] += dsink

    kv_shape = jax.ShapeDtypeStruct(k.shape, bf16)
    return pl.pallas_call(
        body, name="swa_bwd", grid=(nb,), in_specs=in_specs + [q_spec],
        out_specs=(q_spec, kv_spec, kv_spec, row),
        out_shape=(jax.ShapeDtypeStruct(q.shape, bf16), kv_shape, kv_shape, jax.ShapeDtypeStruct((8, 128), f32)),
        scratch_shapes=[pltpu.VMEM((SWA_KV_HEADS, SWA_W, SWA_D), f32)] * 4,
        compiler_params=_params(("arbitrary",)),
    )(q, k, k, k, v, v, v, sink8, do)


def _tile(dim, prefs):
    for p in prefs:
        if dim % p == 0:
            return p
    return dim


def mm(a, b, *, out_dtype, name, resid=None):
    m, k = a.shape
    n = b.shape[1]
    rhs_stays = k * 2 * 1024 > MM_OPERAND_BYTES
    if rhs_stays:
        tn = _tile(n, tuple(p for p in (512, 256, 128) if p * k * 2 <= MM_RESIDENT_BYTES))
        tm = _tile(m, tuple(p for p in (512, 384, 256, 128) if p * k * 2 <= MM_OPERAND_BYTES // 2))
        grid = (n // tn, m // tm)
        ij = lambda o, i: (i, o)
    else:
        tm = _tile(m, tuple(p for p in (1408, 1024, 512, 384, 256, 128) if p * k * 2 <= MM_OPERAND_BYTES))
        tn = _tile(n, tuple(p for p in (512, 256, 128) if p * k * 2 <= MM_OPERAND_BYTES // 2))
        grid = (m // tm, n // tn)
        ij = lambda o, i: (o, i)

    def body(*refs):
        a_ref, b_ref = refs[:2]
        o = _dot(a_ref[...], b_ref[...])
        if resid is not None:
            o = o + refs[2][...]
        refs[-1][...] = o.astype(out_dtype)

    in_specs = [pl.BlockSpec((tm, k), lambda o, i: (ij(o, i)[0], 0)), pl.BlockSpec((k, tn), lambda o, i: (0, ij(o, i)[1]))]
    args = [a, b]
    if resid is not None:
        in_specs.append(pl.BlockSpec((tm, tn), ij))
        args.append(resid)
    return pl.pallas_call(
        body, name=name, grid=grid, in_specs=in_specs, out_specs=pl.BlockSpec((tm, tn), ij),
        out_shape=jax.ShapeDtypeStruct((m, n), out_dtype),
        compiler_params=_params(("parallel", "parallel")),
    )(*args)


def _rows(t_rows):
    return _tile(t_rows, (384, 256, 128))


def _rmsnorm(h, w):
    return h * lax.rsqrt(jnp.mean(h * h, axis=1, keepdims=True) + RMS_EPS) * w


def rmsnorm_fwd(h, w8, *, name):
    t_rows, d = h.shape
    tr = _rows(t_rows)

    def body(h_ref, w_ref, o_ref):
        o_ref[...] = _rmsnorm(h_ref[...], w_ref[0:1, :]).astype(bf16)

    blk = pl.BlockSpec((tr, d), lambda i: (i, 0))
    return pl.pallas_call(
        body, name=name, grid=(t_rows // tr,), in_specs=[blk, pl.BlockSpec((8, d), lambda i: (0, 0))], out_specs=blk,
        out_shape=jax.ShapeDtypeStruct((t_rows, d), bf16), compiler_params=_params(("arbitrary",)),
    )(h, w8)


def rmsnorm_bwd(h, w8, dhn, dres, *, name):
    t_rows, d = h.shape
    tr = _rows(t_rows)

    def body(h_ref, w_ref, dhn_ref, dres_ref, dh_ref, dw_ref):
        @pl.when(pl.program_id(0) == 0)
        def _():
            dw_ref[...] = jnp.zeros_like(dw_ref)

        _, vjp = jax.vjp(_rmsnorm, h_ref[...], w_ref[0:1, :])
        dh, dw = vjp(dhn_ref[...])
        dh_ref[...] = dh + dres_ref[...]
        dw_ref[0:1, :] += dw

    blk = pl.BlockSpec((tr, d), lambda i: (i, 0))
    wblk = pl.BlockSpec((8, d), lambda i: (0, 0))
    return pl.pallas_call(
        body, name=name, grid=(t_rows // tr,), in_specs=[blk, wblk, blk, blk], out_specs=(blk, wblk),
        out_shape=(jax.ShapeDtypeStruct((t_rows, d), f32), jax.ShapeDtypeStruct((8, d), f32)),
        compiler_params=_params(("arbitrary",)),
    )(h, w8, dhn, dres)


def _merge(pg, ps, pw, la, lb, lc):
    return jax.nn.sigmoid(la) * pg + jax.nn.sigmoid(lb) * ps + jax.nn.sigmoid(lc) * pw


def _merge_specs(t_rows):
    tr = _rows(t_rows)
    blk = pl.BlockSpec((tr, D_MODEL), lambda i: (i, 0))
    gate = [pl.BlockSpec((tr, D_MODEL), functools.partial(lambda i, j: (i, j), j=C_GATE // D_MODEL + j)) for j in range(3)]
    return tr, blk, gate


def merge_fwd(pg, ps, pw, u):
    t_rows = pg.shape[0]
    tr, blk, gate = _merge_specs(t_rows)

    def body(pg_ref, ps_ref, pw_ref, la, lb, lc, o_ref):
        o_ref[...] = _merge(pg_ref[...], ps_ref[...], pw_ref[...], la[...], lb[...], lc[...]).astype(bf16)

    return pl.pallas_call(
        body, name="merge_fwd", grid=(t_rows // tr,), in_specs=[blk, blk, blk] + gate, out_specs=blk,
        out_shape=jax.ShapeDtypeStruct((t_rows, D_MODEL), bf16), compiler_params=_params(("arbitrary",)),
    )(pg, ps, pw, u, u, u)


def merge_bwd(pg, ps, pw, u, dmerged):
    t_rows = pg.shape[0]
    tr, blk, gate = _merge_specs(t_rows)

    def body(pg_ref, ps_ref, pw_ref, la, lb, lc, dm_ref, dpg_ref, dps_ref, dpw_ref, dl_ref):
        _, vjp = jax.vjp(_merge, pg_ref[...], ps_ref[...], pw_ref[...], la[...], lb[...], lc[...])
        dpg, dps, dpw, dla, dlb, dlc = vjp(dm_ref[...])
        dpg_ref[...] = dpg.astype(bf16)
        dps_ref[...] = dps.astype(bf16)
        dpw_ref[...] = dpw.astype(bf16)
        for j, dl in enumerate((dla, dlb, dlc)):
            dl_ref[:, j * D_MODEL:(j + 1) * D_MODEL] = dl.astype(bf16)

    act = jax.ShapeDtypeStruct((t_rows, D_MODEL), bf16)
    return pl.pallas_call(
        body, name="merge_bwd", grid=(t_rows // tr,), in_specs=[blk, blk, blk] + gate + [blk],
        out_specs=(blk, blk, blk, pl.BlockSpec((tr, 3 * D_MODEL), lambda i: (i, 0))),
        out_shape=(act, act, act, jax.ShapeDtypeStruct((t_rows, 3 * D_MODEL), bf16)),
        compiler_params=_params(("arbitrary",)),
    )(pg, ps, pw, u, u, u, dmerged)


def relu2_fwd(a):
    t_rows, d = a.shape
    tr = _rows(t_rows)

    def body(a_ref, o_ref):
        r = jnp.maximum(a_ref[...], 0.0)
        o_ref[...] = (r * r).astype(bf16)

    blk = pl.BlockSpec((tr, d), lambda i: (i, 0))
    return pl.pallas_call(
        body, name="relu2_fwd", grid=(t_rows // tr,), in_specs=[blk], out_specs=blk,
        out_shape=jax.ShapeDtypeStruct((t_rows, d), bf16), compiler_params=_params(("arbitrary",)),
    )(a)


def relu2_bwd(a, dr):
    t_rows, d = a.shape
    tr = _rows(t_rows)

    def body(a_ref, dr_ref, o_ref):
        o_ref[...] = (dr_ref[...] * 2.0 * jnp.maximum(a_ref[...], 0.0)).astype(bf16)

    blk = pl.BlockSpec((tr, d), lambda i: (i, 0))
    return pl.pallas_call(
        body, name="relu2_bwd", grid=(t_rows // tr,), in_specs=[blk, blk], out_specs=blk,
        out_shape=jax.ShapeDtypeStruct((t_rows, d), bf16), compiler_params=_params(("arbitrary",)),
    )(a, dr)


def loss_head(h, w8, target):
    t_rows, d = h.shape
    tr = HEAD_ROWS

    def loss_fn(hb, w, tgt):
        err = _rmsnorm(hb, w) - tgt
        return 0.5 * jnp.sum(err * err) / d

    def body(h_ref, w_ref, t_ref, loss_ref, dh_ref, dw_ref):
        i = pl.program_id(0)

        @pl.when(i == 0)
        def _():
            loss_ref[...] = jnp.zeros_like(loss_ref)
            dw_ref[...] = jnp.zeros_like(dw_ref)
            dh_ref[...] = jnp.zeros_like(dh_ref)

        @pl.when(i > 0)
        def _():
            val, (dh, dw) = jax.value_and_grad(loss_fn, argnums=(0, 1))(h_ref[...], w_ref[0:1, :], t_ref[...])
            loss_ref[...] += val
            dh_ref[...] = dh
            dw_ref[0:1, :] += dw

    blk = pl.BlockSpec((tr, d), lambda i: (i, 0))
    wblk = pl.BlockSpec((8, d), lambda i: (0, 0))
    return pl.pallas_call(
        body, name="loss_head", grid=(t_rows // tr,),
        in_specs=[blk, wblk, pl.BlockSpec((tr, d), lambda i: (jnp.maximum(i - 1, 0), 0))],
        out_specs=(pl.BlockSpec((8, 128), lambda i: (0, 0)), blk, wblk),
        out_shape=(jax.ShapeDtypeStruct((8, 128), f32), jax.ShapeDtypeStruct((t_rows, d), f32),
                   jax.ShapeDtypeStruct((8, d), f32)),
        compiler_params=_params(("arbitrary",)),
    )(h, w8, target)


def adamw(w, m, v, partials, row_off, *, name):
    rows, d = w.shape
    layers = len(partials)
    per = rows // layers
    tr = _tile(per, tuple(p for p in (512, 256, 128, 64, 16, 8) if p * d * 4 <= BLOCK_BYTES))
    assert row_off % tr == 0
    off, nblk = row_off // tr, per // tr
    c1 = 1.0 - ADAM_B1 ** ADAM_STEP
    c2 = 1.0 - ADAM_B2 ** ADAM_STEP

    def body(w_ref, m_ref, v_ref, *refs):
        p_refs, (g_ref, d_ref, mo_ref, vo_ref) = refs[:2 * layers], refs[2 * layers:]
        g = p_refs[0][...] + p_refs[1][...]
        for l in range(1, layers):
            g = jnp.where(pl.program_id(0) >= l * nblk, p_refs[2 * l][...] + p_refs[2 * l + 1][...], g)
        m_new = ADAM_B1 * m_ref[...] + (1.0 - ADAM_B1) * g
        v_new = ADAM_B2 * v_ref[...] + (1.0 - ADAM_B2) * (g * g)
        g_ref[...] = g
        d_ref[...] = -ADAM_LR * ((m_new / c1) / (jnp.sqrt(v_new / c2) + ADAM_EPS) + ADAM_WD * w_ref[...])
        mo_ref[...] = m_new
        vo_ref[...] = v_new

    blk = pl.BlockSpec((tr, d), lambda i: (i, 0))
    pblks = [pl.BlockSpec((tr, d), functools.partial(lambda i, l: (off + jnp.clip(i - l * nblk, 0, nblk - 1), 0), l=l))
             for l in range(layers) for _ in range(2)]
    out = jax.ShapeDtypeStruct((rows, d), f32)
    return pl.pallas_call(
        body, name=name, grid=(rows // tr,), in_specs=[blk, blk, blk] + pblks, out_specs=(blk,) * 4,
        out_shape=(out,) * 4, compiler_params=_params(("arbitrary",)),
    )(w, m, v, *[p for pair in partials for p in pair])


def reduce4(parts, *, name, own=None, me=None):
    _, rows, d = parts.shape
    tr = _tile(rows, tuple(p for p in (512, 256, 128, 64, 8) if p * d * 4 <= BLOCK_BYTES))

    def body(*refs):
        p_ref, o_ref = refs[0], refs[-1]
        acc = None
        for s in range(4):
            term = p_ref[s].astype(f32)
            if own is not None:
                term = jnp.where(refs[2][0] == s, refs[1][...].astype(f32), term)
            acc = term if acc is None else acc + term
        o_ref[...] = acc

    in_specs = [pl.BlockSpec((4, tr, d), lambda i: (0, i, 0))]
    args = [parts]
    if own is not None:
        in_specs += [pl.BlockSpec((tr, d), lambda i: (i, 0)), pl.BlockSpec(memory_space=pltpu.SMEM)]
        args += [own, me]
    return pl.pallas_call(
        body, name=name, grid=(rows // tr,), in_specs=in_specs,
        out_specs=pl.BlockSpec((tr, d), lambda i: (i, 0)), out_shape=jax.ShapeDtypeStruct((rows, d), f32),
        compiler_params=_params(("arbitrary",)),
    )(*args)


ANY = pl.BlockSpec(memory_space=pl.ANY)
MESH = pl.DeviceIdType.MESH
CHIP_FLIPS = ((0, 1), (1, 0), (1, 1))


def chip_exchange(bufs, scatter, *, name):
    nb = len(bufs)

    def body(*refs):
        ins, outs = refs[:nb], refs[nb:2 * nb]
        send_sems, recv_sems, local_sems = refs[2 * nb:]
        x, y, c = lax.axis_index("x"), lax.axis_index("y"), lax.axis_index("c")
        me = 2 * x + y
        local = [pltpu.make_async_copy(ins[j].at[me] if scatter[j] else ins[j], outs[j].at[me], local_sems.at[j])
                 for j in range(nb)]
        for cp in local:
            cp.start()
        sends, recvs = [], []
        for k, (fx, fy) in enumerate(CHIP_FLIPS):
            px = 1 - x if fx else x
            py = 1 - y if fy else y
            chip = 2 * px + py
            for j in range(nb):
                src = ins[j].at[chip] if scatter[j] else ins[j]
                sems = dict(send_sem=send_sems.at[nb * k + j], recv_sem=recv_sems.at[nb * k + j],
                            device_id=(px, py, c), device_id_type=MESH)
                sends.append(pltpu.make_async_remote_copy(src_ref=src, dst_ref=outs[j].at[me], **sems))
                recvs.append(pltpu.make_async_remote_copy(src_ref=src, dst_ref=outs[j].at[chip], **sems))
        for cp in sends:
            cp.start()
        for cp in recvs:
            cp.wait_recv()
        for cp in sends:
            cp.wait_send()
        for cp in local:
            cp.wait()

    out_shape = tuple(jax.ShapeDtypeStruct(b.shape if s else (4,) + b.shape, b.dtype) for b, s in zip(bufs, scatter))
    return pl.pallas_call(
        body, name=name, in_specs=[ANY] * nb, out_specs=(ANY,) * nb, out_shape=out_shape,
        scratch_shapes=[pltpu.SemaphoreType.DMA((3 * nb,)), pltpu.SemaphoreType.DMA((3 * nb,)),
                        pltpu.SemaphoreType.DMA((nb,))],
        compiler_params=pltpu.CompilerParams(has_side_effects=True),
    )(*bufs)


def sibling_swap(bufs, *, name):
    nb = len(bufs)

    def body(*refs):
        ins, outs, (send_sems, recv_sems) = refs[:nb], refs[nb:2 * nb], refs[2 * nb:]
        peer = (lax.axis_index("x"), lax.axis_index("y"), 1 - lax.axis_index("c"))
        copies = [pltpu.make_async_remote_copy(src_ref=ins[j], dst_ref=outs[j], send_sem=send_sems.at[j],
                                               recv_sem=recv_sems.at[j], device_id=peer, device_id_type=MESH)
                  for j in range(nb)]
        for cp in copies:
            cp.start()
        for cp in copies:
            cp.wait_recv()
        for cp in copies:
            cp.wait_send()

    return pl.pallas_call(
        body, name=name, in_specs=[ANY] * nb, out_specs=(ANY,) * nb,
        out_shape=tuple(jax.ShapeDtypeStruct(b.shape, b.dtype) for b in bufs),
        scratch_shapes=[pltpu.SemaphoreType.DMA((nb,)), pltpu.SemaphoreType.DMA((nb,))],
        compiler_params=pltpu.CompilerParams(has_side_effects=True),
    )(*bufs)


HBM = pl.BlockSpec(memory_space=pltpu.HBM)
SEM = pl.BlockSpec(memory_space=pltpu.SEMAPHORE)
DATAFLOW = pltpu.SideEffectType.DATAFLOW_SIDE_EFFECTING


def _exchange_copies(srcs, lands, send_sems, recv_sems, scatter):
    x, y, c = lax.axis_index("x"), lax.axis_index("y"), lax.axis_index("c")
    me = 2 * x + y
    nb = len(srcs)
    pairs = []
    for k, (fx, fy) in enumerate(CHIP_FLIPS):
        px = 1 - x if fx else x
        py = 1 - y if fy else y
        chip = 2 * px + py
        for j in range(nb):
            src = srcs[j].at[chip] if scatter[j] else srcs[j]
            sems = dict(send_sem=send_sems.at[nb * k + j], recv_sem=recv_sems.at[nb * k + j],
                        device_id=(px, py, c), device_id_type=MESH)
            pairs.append((pltpu.make_async_remote_copy(src_ref=src, dst_ref=lands[j].at[me], **sems),
                          pltpu.make_async_remote_copy(src_ref=src, dst_ref=lands[j].at[chip], **sems)))
    return pairs


def exchange_start(bufs, scatter, after, *, name):
    nb = len(bufs)
    slabs = [b.shape[1:] if s else b.shape for b, s in zip(bufs, scatter)]
    lands = [lax.empty((4,) + shp, b.dtype) for b, shp in zip(bufs, slabs)]

    def body(*refs):
        srcs, zones = refs[:nb], refs[nb:2 * nb]
        send_sems, recv_sems = refs[2 * nb + 1:2 * nb + 3]
        token = refs[-1]
        for send, _ in _exchange_copies(srcs, zones, send_sems, recv_sems, scatter):
            send.start()
        token[...] = jnp.zeros_like(token)

    hbm = lambda a: pltpu.with_memory_space_constraint(a, pltpu.HBM)
    out = pl.pallas_call(
        body, name=name, in_specs=[HBM] * (2 * nb) + [ANY],
        out_specs=(SEM, SEM) + (HBM,) * (2 * nb) + (pl.BlockSpec(memory_space=pltpu.VMEM),),
        out_shape=(pltpu.SemaphoreType.DMA((3 * nb,)), pltpu.SemaphoreType.DMA((3 * nb,)))
        + tuple(pltpu.HBM(a.shape, a.dtype) for a in list(bufs) + lands) + (jax.ShapeDtypeStruct((8, 128), f32),),
        input_output_aliases={i: 2 + i for i in range(2 * nb)},
        compiler_params=pltpu.CompilerParams(has_side_effects=DATAFLOW),
    )(*[hbm(a) for a in list(bufs) + lands], after)
    return (out[:2], out[2:2 + nb], out[2 + nb:2 + 2 * nb], scatter), out[-1]


def exchange_wait(state, after, *, name):
    (send_sems, recv_sems), srcs, lands, scatter = state
    nb = len(srcs)

    def body(*refs):
        src_refs, zones = refs[:nb], refs[nb:2 * nb]
        s_sems, r_sems = refs[2 * nb:2 * nb + 2]
        for send, recv in _exchange_copies(src_refs, zones, s_sems, r_sems, scatter):
            send.wait_send()
            recv.wait_recv()

    out = pl.pallas_call(
        body, name=name, in_specs=[HBM] * (2 * nb) + [SEM, SEM, ANY], out_specs=(HBM,) * (2 * nb),
        out_shape=tuple(pltpu.HBM(a.shape, a.dtype) for a in list(srcs) + list(lands)),
        input_output_aliases={i: i for i in range(2 * nb)},
        compiler_params=pltpu.CompilerParams(has_side_effects=DATAFLOW),
    )(*srcs, *lands, send_sems, recv_sems, after)
    return out[nb:]


BIG = (
    ("w_proj_gdn", 256), ("w_proj_ssd", 256), ("w_proj_swa", 256), ("w_out", 256), ("w_up", 1024), ("w_down", 1024))
BIG_OFF = {}
_o = 0
for _n, _r in BIG:
    BIG_OFF[_n] = _o
    _o += _r
BIG_ROWS = _o
W_IN_SHARD = IN_W // 4

W_NAMES = ('meta_tokens', 'norm1_w', 'w_in', 'gdn_conv_w', 'gdn_a_log', 'gdn_dt_bias', 'gdn_norm_w', 'ssd_conv_w',
           'ssd_conv_b', 'ssd_dt_bias', 'ssd_a_log', 'ssd_d', 'ssd_norm_w', 'swa_sinks', 'w_proj_gdn', 'w_proj_ssd',
           'w_proj_swa', 'w_out', 'norm2_w', 'w_up', 'w_down', 'final_norm_w')
SMALL_NAMES = tuple(n for n in W_NAMES if n not in BIG_OFF and n != "w_in")
SMALL_SHARDED = ("meta_tokens", "gdn_conv_w", "ssd_conv_w")


def _pad_rows(a, rows):
    return jnp.pad(a, ((0, rows - a.shape[0]), (0, 0)))


def _pack_rows(parts, dtype):
    flat = jnp.concatenate([p.reshape(-1).astype(dtype) for p in parts])
    n = -(-flat.shape[0] // 8192) * 8192
    return jnp.pad(flat, (0, n - flat.shape[0])).reshape(-1, D_MODEL)


def _unpack_rows(packed, shapes):
    flat, out, o = packed.reshape(-1), [], 0
    for s in shapes:
        n = 1
        for d in s:
            n *= d
        out.append(flat[o:o + n].reshape(s))
        o += n
    return out


def _split_chips(full, axis):
    s = full.shape
    a = full.reshape(s[:axis] + (4, s[axis] // 4) + s[axis + 1:])
    return jnp.moveaxis(a, axis, 0)


def _join_chips(parts, axis):
    a = jnp.moveaxis(parts, 0, axis)
    s = a.shape
    return a.reshape(s[:axis] + (s[axis] * s[axis + 1],) + s[axis + 2:])


BIG_AXIS = {"w_in": 2, "w_proj_gdn": 1, "w_proj_ssd": 1, "w_proj_swa": 1, "w_out": 1, "w_up": 2, "w_down": 1}


def _w_in_to_padded(w):
    z = jnp.zeros(w.shape[:-1] + (112,), w.dtype)
    return jnp.concatenate([w[..., 8736:11808], w[..., 0:4096], w[..., 4112:7184], w[..., 7200:8736],
                            w[..., 4096:4112], z, w[..., 7184:7200], z], axis=-1)


def _w_in_from_padded(p):
    return jnp.concatenate([p[..., C_GQ:C_SZ], p[..., C_BA:C_BA + 16], p[..., C_SZ:C_WQ], p[..., C_SDT:C_SDT + 16],
                            p[..., C_WQ:C_BA], p[..., 0:C_GQ]], axis=-1)


def _row8(v, lane0=0, width=128):
    return jnp.pad(v[None, :], ((0, 7), (lane0, width - lane0 - v.shape[0])))


def _head_major(a, heads):
    return a.reshape(a.shape[0], heads, SWA_D).transpose(1, 0, 2)


def _from_head_major(a):
    return a.transpose(1, 0, 2).reshape(a.shape[1], -1)


def _layer_fwd(h, p, l):
    tag = f"l{l}"
    hn = rmsnorm_fwd(h, p["n1"], name=f"norm1_fwd_{tag}")
    u = mm(hn, p["w_in"], out_dtype=f32, name=f"mm_in_{tag}")
    yg, stg = gdn_fwd(u, p["gcw"], p["galog"], p["gdtb"], p["gnw"])
    ys, sts = ssd_fwd(u, p["scw"], p["sdtb"], p["salog"], p["sd"], p["snw"])
    qh = _head_major(u[:, C_WQ:C_WK], SWA_Q_HEADS)
    kh = _head_major(u[:, C_WK:C_WV], SWA_KV_HEADS)
    vh = _head_major(u[:, C_WV:C_BA], SWA_KV_HEADS)
    yw = _from_head_major(swa_fwd(qh, kh, vh, p["sink"]))
    pg = mm(yg, p["wpg"], out_dtype=f32, name=f"mm_pg_{tag}")
    ps = mm(ys, p["wps"], out_dtype=f32, name=f"mm_ps_{tag}")
    pw = mm(yw, p["wpw"], out_dtype=f32, name=f"mm_pw_{tag}")
    merged = merge_fwd(pg, ps, pw, u)
    h2 = mm(merged, p["wout"], out_dtype=f32, resid=h, name=f"mm_out_{tag}")
    hn2 = rmsnorm_fwd(h2, p["n2"], name=f"norm2_fwd_{tag}")
    a = mm(hn2, p["wup"], out_dtype=f32, name=f"mm_up_{tag}")
    r = relu2_fwd(a)
    h3 = mm(r, p["wdown"], out_dtype=f32, resid=h2, name=f"mm_down_{tag}")
    saved = dict(h=h, hn=hn, u=u, yg=yg, stg=stg, ys=ys, sts=sts, qh=qh, kh=kh, vh=vh, yw=yw, pg=pg, ps=ps, pw=pw,
                 merged=merged, h2=h2, hn2=hn2, a=a, r=r)
    return h3, saved


def _layer_bwd(dh3, p, s, l):
    tag = f"l{l}"
    g = {}

    def wgrad(act, d, name):
        return mm(act.T, d, out_dtype=bf16, name=f"wg_{name}_{tag}")

    da = relu2_bwd(s["a"], mm(dh3, p["wdown_t"], out_dtype=f32, name=f"dg_down_{tag}"))
    g["w_down"] = wgrad(s["r"], dh3, "down")
    dhn2 = mm(da, p["wup_t"], out_dtype=f32, name=f"dg_up_{tag}")
    g["w_up"] = wgrad(s["hn2"], da, "up")
    dh2, g["norm2_w"] = rmsnorm_bwd(s["h2"], p["n2"], dhn2, dh3, name=f"norm2_bwd_{tag}")
    dmerged = mm(dh2, p["wout_t"], out_dtype=f32, name=f"dg_out_{tag}")
    g["w_out"] = wgrad(s["merged"], dh2, "out")
    dpg, dps, dpw, dgl = merge_bwd(s["pg"], s["ps"], s["pw"], s["u"], dmerged)
    dyg = mm(dpg, p["wpg_t"], out_dtype=f32, name=f"dg_pg_{tag}")
    dys = mm(dps, p["wps_t"], out_dtype=f32, name=f"dg_ps_{tag}")
    dyw = mm(dpw, p["wpw_t"], out_dtype=f32, name=f"dg_pw_{tag}")
    g["w_proj_gdn"] = wgrad(s["yg"], dpg, "pg")
    g["w_proj_ssd"] = wgrad(s["ys"], dps, "ps")
    g["w_proj_swa"] = wgrad(s["yw"], dpw, "pw")

    (dq, dk, dv, dgate, dba, dtq, dtk, dtv, g["gdn_a_log"], g["gdn_dt_bias"], g["gdn_norm_w"]) = gdn_bwd(
        s["u"], p["gcw"], p["galog"], p["gdtb"], p["gnw"], s["stg"], dyg)
    g["gdn_conv_w"] = jnp.concatenate([dtq, dtk, dtv], axis=1)[:4]
    (dz, dxs, dbm, dcm, ddt, dtx, dtb, dtc, g["ssd_dt_bias"], g["ssd_a_log"], g["ssd_d"], g["ssd_norm_w"]) = ssd_bwd(
        s["u"], p["scw"], p["sdtb"], p["salog"], p["sd"], p["snw"], s["sts"], dys)
    dconv = jnp.concatenate([dtx, dtb, dtc], axis=1)
    g["ssd_conv_w"], g["ssd_conv_b"] = dconv[:4], dconv[4]
    dqh, dkh, dvh, g["swa_sinks"] = swa_bwd(s["qh"], s["kh"], s["vh"], p["sink"], _head_major(dyw, SWA_Q_HEADS))
    du = jnp.concatenate([dgl, dq, dk, dv, dgate, dz, dxs, dbm, dcm, _from_head_major(dqh), _from_head_major(dkh),
                          _from_head_major(dvh), dba.sum(0).astype(bf16), ddt.astype(bf16)], axis=1)
    dhn = mm(du, p["w_in_t"], out_dtype=f32, name=f"dg_in_{tag}")
    g["w_in"] = wgrad(s["hn"], du, "in")
    dh, g["norm1_w"] = rmsnorm_bwd(s["h"], p["n1"], dhn, dh2, name=f"norm1_bwd_{tag}")
    return dh, g


def kernel(x, meta_tokens, norm1_w, w_in, gdn_conv_w, gdn_a_log, gdn_dt_bias, gdn_norm_w, ssd_conv_w, ssd_conv_b, ssd_dt_bias, ssd_a_log, ssd_d, ssd_norm_w, swa_sinks, w_proj_gdn, w_proj_ssd, w_proj_swa, w_out, norm2_w, w_up, w_down, final_norm_w, loss_target, m_meta_tokens, m_norm1_w, m_w_in, m_gdn_conv_w, m_gdn_a_log, m_gdn_dt_bias, m_gdn_norm_w, m_ssd_conv_w, m_ssd_conv_b, m_ssd_dt_bias, m_ssd_a_log, m_ssd_d, m_ssd_norm_w, m_swa_sinks, m_w_proj_gdn, m_w_proj_ssd, m_w_proj_swa, m_w_out, m_norm2_w, m_w_up, m_w_down, m_final_norm_w, v_meta_tokens, v_norm1_w, v_w_in, v_gdn_conv_w, v_gdn_a_log, v_gdn_dt_bias, v_gdn_norm_w, v_ssd_conv_w, v_ssd_conv_b, v_ssd_dt_bias, v_ssd_a_log, v_ssd_d, v_ssd_norm_w, v_swa_sinks, v_w_proj_gdn, v_w_proj_ssd, v_w_proj_swa, v_w_out, v_norm2_w, v_w_up, v_w_down, v_final_norm_w):
    given = dict(locals())
    depth = norm1_w.shape[0]
    me = 2 * lax.axis_index("x") + lax.axis_index("y")

    me1 = jnp.reshape(me, (1,)).astype(jnp.int32)
    is_me = (jnp.arange(4, dtype=jnp.int32) == me)[:, None, None]

    def weight_slabs(l):
        return (w_in[l].astype(bf16),
                jnp.concatenate([given[n][l].reshape(-1, D_MODEL).astype(bf16) for n, _ in BIG]))

    slabs = [weight_slabs(l) for l in range(depth)]
    wsmall = _pack_rows([given[n] for n in SMALL_SHARDED], f32)
    ga0, gb0, gsmall = chip_exchange([slabs[0][0], slabs[0][1], wsmall], (False,) * 3, name="gather_l0")
    gathers, started = {}, jnp.zeros((), f32)
    for l in range(1, depth):
        gathers[l], token = exchange_start(slabs[l], (False, False), gsmall, name=f"gather_start_l{l}")
        started = started + token[0, 0]
    shard_shapes = [given[n].shape for n in SMALL_SHARDED]
    per_chip = [_unpack_rows(gsmall[s], shard_shapes) for s in range(4)]
    full = {n: jnp.concatenate([per_chip[s][i] for s in range(4)], axis=-1) for i, n in enumerate(SMALL_SHARDED)}

    def layer_operands(l, ga, gb, order):
        w = {}
        for n, r in BIG:
            parts = gb[:, BIG_OFF[n]:BIG_OFF[n] + r].reshape((4,) + given[n].shape[1:])
            w[n] = _join_chips(parts, BIG_AXIS[n] - 1)
        w_in_p = _w_in_to_padded(_join_chips(ga, 1))
        return dict(
            n1=_row8(norm1_w[l], width=D_MODEL) + order, n2=_row8(norm2_w[l], width=D_MODEL),
            w_in=w_in_p, w_in_t=w_in_p.T,
            gcw=jnp.pad(full["gdn_conv_w"][l], ((0, 4), (0, 0))),
            galog=_row8(gdn_a_log[l], 8), gdtb=_row8(gdn_dt_bias[l], 8), gnw=_row8(gdn_norm_w[l]),
            scw=jnp.pad(jnp.concatenate([full["ssd_conv_w"][l], ssd_conv_b[l][None]], axis=0), ((0, 3), (0, 0))),
            sdtb=_row8(ssd_dt_bias[l]), salog=_row8(ssd_a_log[l]), sd=_row8(ssd_d[l]),
            snw=_row8(ssd_norm_w[l], width=D_MODEL), sink=_row8(swa_sinks[l]),
            wpg=w["w_proj_gdn"], wps=w["w_proj_ssd"], wpw=w["w_proj_swa"], wout=w["w_out"],
            wup=w["w_up"], wdown=w["w_down"],
            wpg_t=w["w_proj_gdn"].T, wps_t=w["w_proj_ssd"].T, wpw_t=w["w_proj_swa"].T,
            wout_t=w["w_out"].T, wup_t=w["w_up"].T, wdown_t=w["w_down"].T)

    h = jnp.concatenate([jnp.zeros((PAD, D_MODEL), f32), full["meta_tokens"], x[0]], axis=0)
    layers, saved = [], []
    for l in range(depth):
        if l == 0:
            p = layer_operands(0, ga0, gb0, started)
        else:
            la, lb = exchange_wait(gathers[l], h, name=f"gather_wait_l{l}")
            p = layer_operands(l, jnp.where(is_me, slabs[l][0][None], la), jnp.where(is_me, slabs[l][1][None], lb), 0.0)
        h, s = _layer_fwd(h, p, l)
        layers.append(p)
        saved.append(s)
    loss8, dh, dfw8 = loss_head(h, _row8(final_norm_w, width=D_MODEL), loss_target[0])
    grads = {"final_norm_w": dfw8[0]}
    per_layer, grad_slabs, scatters = [None] * depth, [None] * depth, {}
    for l in reversed(range(depth)):
        dh, g = _layer_bwd(dh, layers[l], saved[l], l)
        grad_slabs[l] = (_split_chips(_w_in_from_padded(g.pop("w_in")), 1),
                         jnp.concatenate([_split_chips(g.pop(n), BIG_AXIS[n] - 1).reshape(4, r, D_MODEL)
                                          for n, r in BIG], axis=1))
        per_layer[l] = g
        if l > 0:
            scatters[l], token = exchange_start(grad_slabs[l], (True, True), loss8, name=f"scatter_start_l{l}")
            layers[l - 1]["n2"] = layers[l - 1]["n2"] + token[0, 0]
    grad_x = dh[HEAD_ROWS:][None]
    grads["meta_tokens"] = dh[PAD:HEAD_ROWS]
    lane = {"gdn_a_log": (8, 8), "gdn_dt_bias": (8, 8), "gdn_norm_w": (0, 128), "ssd_dt_bias": (0, 16),
            "ssd_a_log": (0, 16), "ssd_d": (0, 16), "swa_sinks": (0, 16)}
    for n in per_layer[0]:
        parts = [per_layer[l][n] for l in range(depth)]
        if n in lane:
            parts = [q[0, lane[n][0]:lane[n][0] + lane[n][1]] for q in parts]
        elif n in ("norm1_w", "norm2_w", "ssd_norm_w"):
            parts = [q[0] for q in parts]
        grads[n] = jnp.stack(parts)
    loss = lax.psum(loss8[0, 0], ("x", "y", "c"))

    gs = _pack_rows([grads[n] for n in SMALL_NAMES], f32)
    sums = [None] * depth
    for l in range(1, depth):
        la, lb = exchange_wait(scatters[l], dh, name=f"scatter_wait_l{l}")
        own = [lax.dynamic_index_in_dim(a, me, 0, keepdims=False) for a in grad_slabs[l]]
        sums[l] = (reduce4(la, own=own[0], me=me1, name=f"sum_chips_w_in_l{l}"),
                   reduce4(lb, own=own[1], me=me1, name=f"sum_chips_big_l{l}"))
    ra, rb, rs = chip_exchange([grad_slabs[0][0], grad_slabs[0][1], gs], (True, True, False), name="scatter_l0")
    sums[0] = (reduce4(ra, name="sum_chips_w_in_l0"), reduce4(rb, name="sum_chips_big_l0"))
    ps_ = reduce4(rs, name="sum_chips_small")
    flat = [a for pair in sums for a in pair] + [ps_]
    sib = sibling_swap(flat, name="swap_cores")
    ss = sib[-1]

    out = {}
    w_in_rows = depth * D_MODEL
    res = adamw(*[given[pre + "w_in"].reshape(w_in_rows, W_IN_SHARD) for pre in ("", "m_", "v_")],
                [(flat[2 * l], sib[2 * l]) for l in range(depth)], 0, name="adamw_w_in")
    out["w_in"] = [a.reshape(w_in.shape) for a in res]
    for n, r in BIG:
        shp = given[n].shape
        res = adamw(*[given[pre + n].reshape(depth * r, D_MODEL) for pre in ("", "m_", "v_")],
                    [(flat[2 * l + 1], sib[2 * l + 1]) for l in range(depth)], BIG_OFF[n], name=f"adamw_{n}")
        out[n] = [a.reshape(shp) for a in res]
    full_shapes = [grads[n].shape for n in SMALL_NAMES]
    mine_s, sib_s = _unpack_rows(ps_, full_shapes), _unpack_rows(ss, full_shapes)

    def local(parts):
        loc = []
        for n, a in zip(SMALL_NAMES, parts):
            if n in SMALL_SHARDED:
                sz = a.shape[-1] // 4
                a = lax.dynamic_slice_in_dim(a, me * sz, sz, axis=a.ndim - 1)
            loc.append(a)
        return _pack_rows(loc, f32)

    res = adamw(_pack_rows([given[n] for n in SMALL_NAMES], f32), _pack_rows([given["m_" + n] for n in SMALL_NAMES], f32),
                _pack_rows([given["v_" + n] for n in SMALL_NAMES], f32), [(local(mine_s), local(sib_s))], 0,
                name="adamw_small")
    local_shapes = [given[n].shape for n in SMALL_NAMES]
    unpacked = [_unpack_rows(a, local_shapes) for a in res]
    for i, n in enumerate(SMALL_NAMES):
        out[n] = [unpacked[j][i] for j in range(4)]

    return (loss, grad_x) + tuple(out[n][j] for j in range(4) for n in W_NAMES)
```

```python
import functools

import jax
import jax.numpy as jnp
from jax import lax
from jax.experimental import pallas as pl
from jax.experimental.pallas import tpu as pltpu

f32 = jnp.float32
bf16 = jnp.bfloat16
HI = lax.Precision.HIGHEST

D_MODEL = 1024
N_META = 16
PAD = 112
HEAD_ROWS = PAD + N_META
RMS_EPS = 1e-6
L2_EPS = 1e-6
D_FF = 4 * D_MODEL

GDN_HEADS = 8
GDN_D = 128
GDN_CHUNK = 64
SSD_HEADS = 16
SSD_P = 64
SSD_GROUPS = 4
SSD_HPG = 4
SSD_N = 128
SSD_CHUNK = 128
SWA_Q_HEADS = 16
SWA_KV_HEADS = 4
SWA_REP = 4
SWA_D = 64
SWA_W = 128

C_GATE = 0
C_GQ, C_GK, C_GV, C_GG = 3072, 4096, 5120, 6144
C_SZ = 7168
C_SX, C_SB, C_SC = 8192, 9216, 9728
C_WQ, C_WK, C_WV = 10240, 11264, 11520
C_BA = 11776
C_SDT = 11904
IN_WP = 12032
IN_W = 11808

ADAM_LR, ADAM_B1, ADAM_B2, ADAM_EPS, ADAM_WD, ADAM_STEP = 0.001, 0.9, 0.999, 1e-08, 0.01, 10

VMEM_LIMIT = 56 * 1024 * 1024
BLOCK_BYTES = 3 << 19
MM_OPERAND_BYTES = 9 << 20
MM_RESIDENT_BYTES = 13 << 20

NN = (((1,), (0,)), ((), ()))
NT = (((1,), (1,)), ((), ()))
TN = (((0,), (0,)), ((), ()))


def _dot(a, b, dims=NN):
    return lax.dot_general(a.astype(bf16), b.astype(bf16), dims, preferred_element_type=f32)


def _dotx(a, b, dims=NN):
    return lax.dot_general(a, b, dims, preferred_element_type=f32, precision=lax.Precision.HIGH)


def _iota(shape, axis):
    return lax.broadcasted_iota(jnp.int32, shape, axis)


def _softplus(x):
    return jnp.maximum(x, 0.0) + jnp.log1p(jnp.exp(-jnp.abs(x)))


def _silu(x):
    return x * jax.nn.sigmoid(x)


def _params(sem):
    return pltpu.CompilerParams(dimension_semantics=sem, vmem_limit_bytes=VMEM_LIMIT)


@functools.partial(jax.custom_vjp, nondiff_argnums=(1,))
def _window(x_ext, off):
    n = x_ext.shape[0] - 8
    if off == 8:
        return x_ext[8:]
    return pltpu.roll(x_ext, 8 - off, 0)[8:]


def _window_fwd(x_ext, off):
    return _window(x_ext, off), None


def _window_bwd(off, _, g):
    n, w = g.shape
    g_ext = jnp.concatenate([jnp.zeros((8, w), g.dtype), g], axis=0)
    if off == 8:
        return (g_ext,)
    return (pltpu.roll(g_ext, n + off, 0),)


_window.defvjp(_window_fwd, _window_bwd)


def _conv4(x, halo, taps):
    x_ext = jnp.concatenate([halo, x], axis=0)
    y = taps[3] * x
    for j in range(3):
        y = y + taps[j] * _window(x_ext, 5 + j)
    return y


def _blockinv_impl(a):
    n = a.shape[0]
    ri, ci = _iota((n, n), 0), _iota((n, n), 1)
    t = (ri == ci).astype(f32)
    k = 0
    while (1 << k) < n:
        sel = ((ri >> (k + 1)) == (ci >> (k + 1))) & (((ri >> k) & 1) == 1) & (((ci >> k) & 1) == 0)
        o = jnp.where(sel, a, 0.0)
        t = t - _dotx(_dotx(t, o), t)
        k += 1
    return t


@jax.custom_vjp
def _blockinv(a):
    return _blockinv_impl(a)


def _blockinv_fwd(a):
    t = _blockinv_impl(a)
    return t, t


def _blockinv_bwd(t, dt):
    return (-_dotx(_dotx(t, dt, TN), t, NT),)


_blockinv.defvjp(_blockinv_fwd, _blockinv_bwd)


def _gdn_act(xq, xk, xv, hq, hk, hv, tq, tk, tv):
    return _silu(_conv4(xq, hq, tq)), _silu(_conv4(xk, hk, tk)), _silu(_conv4(xv, hv, tv))


def _gdn_core(q, k, v, gate, mb, mg, s, ba, alog, dtb, nw, *, row0):
    c = GDN_CHUNK
    q = q * lax.rsqrt(jnp.sum(q * q, axis=1, keepdims=True) + L2_EPS) * (GDN_D ** -0.5)
    k = k * lax.rsqrt(jnp.sum(k * k, axis=1, keepdims=True) + L2_EPS)

    valid = (row0 + _iota((c, 1), 0)) >= PAD
    pick = lambda x, m: jnp.sum(x * m, axis=1, keepdims=True)
    beta = jnp.where(valid, jax.nn.sigmoid(pick(ba, mb)), 0.0)
    g1 = jnp.where(valid, -jnp.exp(pick(alog, mg)) * _softplus(pick(ba, mg) + pick(dtb, mg)), 0.0)
    g = jnp.broadcast_to(g1, (c, GDN_D))
    g64 = jnp.broadcast_to(g1, (c, c))

    ri, ci = _iota((c, c), 0), _iota((c, c), 1)
    incl = ci <= ri
    gam = _dotx(incl.astype(f32), g)
    gam_i = _dotx(incl.astype(f32), g64)
    gam_j = _dotx(jnp.ones((c, c), f32), jnp.where(ri <= ci, g64, 0.0))
    decay = jnp.where(incl, jnp.exp(jnp.where(incl, gam_i - gam_j, 0.0)), 0.0)

    kb = k * beta
    a = jnp.where(ci < ri, _dot(kb, k, NT) * decay, 0.0)
    t = _blockinv(a)
    egam = jnp.exp(gam)
    u = _dotx(t, v * beta)
    w = _dotx(t, kb * egam)
    attn = _dot(q, k, NT) * decay
    gl = jnp.sum(g, axis=0, keepdims=True)
    kt = k * jnp.exp(gl - gam)
    v_new = u - _dot(w, s)
    o = _dot(q * egam, s) + _dot(attn, v_new)
    s_out = s * jnp.exp(gl) + _dot(kt, v_new, TN)

    y = o * lax.rsqrt(jnp.mean(o * o, axis=1, keepdims=True) + RMS_EPS) * nw * _silu(gate)
    return y, s_out


def _gdn_specs(hb, nc, rev):
    w = hb * GDN_D
    cw = D_MODEL // w

    def cidx(c):
        return (nc - 1 - c) if rev else c

    def col(base):
        return pl.BlockSpec((GDN_CHUNK, w), lambda h, c: (cidx(c), base // w + h))

    def halo(base):
        return pl.BlockSpec((8, w), lambda h, c: (jnp.maximum(cidx(c) * (GDN_CHUNK // 8) - 1, 0), base // w + h))

    def taps(base):
        return pl.BlockSpec((8, w), lambda h, c: (0, base // w + h))

    ba = pl.BlockSpec((GDN_CHUNK, 128), lambda h, c: (cidx(c), C_BA // 128))
    row = pl.BlockSpec((8, 128), lambda h, c: (0, 0))
    y = pl.BlockSpec((GDN_CHUNK, w), lambda h, c: (cidx(c), h))
    st = pl.BlockSpec((1, hb, GDN_D, GDN_D), lambda h, c: (cidx(c), h, 0, 0))
    in_specs = [col(C_GQ), col(C_GK), col(C_GV), halo(C_GQ), halo(C_GK), halo(C_GV), col(C_GG), ba,
                taps(0), taps(1024), taps(2048), row, row, row]
    return in_specs, y, st, taps, row, col, ba


def _gdn_load(refs, first):
    xq, xk, xv, hq, hk, hv, gate, ba, tq, tk, tv, alog, dtb, nw = refs

    def halo(r):
        return jnp.where(first, 0.0, r[...])

    def taps(r):
        return tuple(r[j:j + 1, :] for j in range(4))

    act = (xq[...], xk[...], xv[...], halo(hq), halo(hk), halo(hv), taps(tq), taps(tk), taps(tv))
    return act, gate[...], (ba[...], alog[0:1, :], dtb[0:1, :], nw[0:1, :])


def _heads(a, hb):
    return jnp.stack([a[:, i * GDN_D:(i + 1) * GDN_D] for i in range(hb)])


def _wide(a):
    return jnp.concatenate([a[i] for i in range(a.shape[0])], axis=1)


def _head_masks(hblk, hb):
    head = hblk * hb + _iota((hb, 1, 128), 0)
    lane = _iota((hb, 1, 128), 2)
    return (lane == head).astype(f32), (lane == head + 8).astype(f32)


def _gdn_core_heads(row0):
    return jax.vmap(functools.partial(_gdn_core, row0=row0), in_axes=(0, 0, 0, 0, 0, 0, 0, None, None, None, None))


def gdn_fwd(u, conv_w8, alog8, dtb8, nw8, *, hb=8):
    t_rows = u.shape[0]
    nc = t_rows // GDN_CHUNK
    in_specs, y_spec, st_spec, *_ = _gdn_specs(hb, nc, False)

    def body(*refs):
        ins, (y_ref, st_ref), (s_scr,) = refs[:14], refs[14:16], refs[16:]
        hblk, c = pl.program_id(0), pl.program_id(1)

        @pl.when(c == 0)
        def _():
            s_scr[...] = jnp.zeros_like(s_scr)

        act, gate, shared = _gdn_load(ins, c == 0)
        s = s_scr[...]
        st_ref[0] = s
        qa, ka, va = _gdn_act(*act)
        mb, mg = _head_masks(hblk, hb)
        y, s_new = _gdn_core_heads(c * GDN_CHUNK)(_heads(qa, hb), _heads(ka, hb), _heads(va, hb), _heads(gate, hb),
                                                  mb, mg, s, *shared)
        y_ref[...] = _wide(y).astype(bf16)
        s_scr[...] = s_new

    return pl.pallas_call(
        body, name="gdn_fwd", grid=(GDN_HEADS // hb, nc),
        in_specs=in_specs, out_specs=(y_spec, st_spec),
        out_shape=(jax.ShapeDtypeStruct((t_rows, D_MODEL), bf16),
                   jax.ShapeDtypeStruct((nc, GDN_HEADS, GDN_D, GDN_D), f32)),
        scratch_shapes=[pltpu.VMEM((hb, GDN_D, GDN_D), f32)],
        compiler_params=_params(("arbitrary", "arbitrary")),
    )(u, u, u, u, u, u, u, u, conv_w8, conv_w8, conv_w8, alog8, dtb8, nw8)


def gdn_bwd(u, conv_w8, alog8, dtb8, nw8, states, dy, *, hb=8):
    t_rows = u.shape[0]
    nc = t_rows // GDN_CHUNK
    w = hb * GDN_D
    in_specs, y_spec, st_spec, taps, row, col, ba = _gdn_specs(hb, nc, True)
    nhb = GDN_HEADS // hb

    def body(*refs):
        ins, st_ref, dy_ref = refs[:14], refs[14], refs[15]
        dq_ref, dk_ref, dv_ref, dgate_ref, dba_ref, dtq_ref, dtk_ref, dtv_ref, dalog_ref, ddtb_ref, dnw_ref = refs[16:27]
        ds_scr, dh_scr = refs[27:]
        hblk, cc = pl.program_id(0), pl.program_id(1)
        c = nc - 1 - cc

        @pl.when(cc == 0)
        def _():
            ds_scr[...] = jnp.zeros_like(ds_scr)
            dh_scr[...] = jnp.zeros_like(dh_scr)
            dtq_ref[...] = jnp.zeros_like(dtq_ref)
            dtk_ref[...] = jnp.zeros_like(dtk_ref)
            dtv_ref[...] = jnp.zeros_like(dtv_ref)

        @pl.when((cc == 0) & (hblk == 0))
        def _():
            dalog_ref[...] = jnp.zeros_like(dalog_ref)
            ddtb_ref[...] = jnp.zeros_like(ddtb_ref)
            dnw_ref[...] = jnp.zeros_like(dnw_ref)

        act, gate, shared = _gdn_load(ins, c == 0)
        (qa, ka, va), vjp_act = jax.vjp(_gdn_act, *act)
        mb, mg = _head_masks(hblk, hb)
        _, vjp_core = jax.vjp(_gdn_core_heads(c * GDN_CHUNK), _heads(qa, hb), _heads(ka, hb), _heads(va, hb),
                              _heads(gate, hb), mb, mg, st_ref[0], *shared)
        dqa, dka, dva, dgate, _, _, ds, dba, dalog, ddtb, dnw = vjp_core(
            (_heads(dy_ref[...].astype(f32), hb), ds_scr[...]))
        ds_scr[...] = ds
        dxq, dxk, dxv, dhq, dhk, dhv, dtq, dtk, dtv = vjp_act((_wide(dqa), _wide(dka), _wide(dva)))
        zeros = jnp.zeros((GDN_CHUNK - 8, w), f32)
        for j, (dx, dh, out) in enumerate(((dxq, dhq, dq_ref), (dxk, dhk, dk_ref), (dxv, dhv, dv_ref))):
            out[...] = (dx + jnp.concatenate([zeros, dh_scr[j]], axis=0)).astype(bf16)
            dh_scr[j] = dh
        dgate_ref[...] = _wide(dgate).astype(bf16)
        dba_ref[0] = dba
        for dt_ref, dtaps in ((dtq_ref, dtq), (dtk_ref, dtk), (dtv_ref, dtv)):
            for j in range(4):
                dt_ref[j:j + 1, :] += dtaps[j]
        dalog_ref[0:1, :] += dalog
        ddtb_ref[0:1, :] += ddtb
        dnw_ref[0:1, :] += dnw

    def colr(base):
        return pl.BlockSpec((GDN_CHUNK, w), lambda h, c: (nc - 1 - c, base // w + h))

    out_specs = (colr(0), colr(0), colr(0), colr(0),
                 pl.BlockSpec((1, GDN_CHUNK, 128), lambda h, c: (h, nc - 1 - c, 0)),
                 taps(0), taps(0), taps(0), row, row, row)
    out_shape = (jax.ShapeDtypeStruct((t_rows, D_MODEL), bf16),) * 4 + (
        jax.ShapeDtypeStruct((nhb, t_rows, 128), f32),
        jax.ShapeDtypeStruct((8, D_MODEL), f32), jax.ShapeDtypeStruct((8, D_MODEL), f32),
        jax.ShapeDtypeStruct((8, D_MODEL), f32),
        jax.ShapeDtypeStruct((8, 128), f32), jax.ShapeDtypeStruct((8, 128), f32), jax.ShapeDtypeStruct((8, 128), f32))
    return pl.pallas_call(
        body, name="gdn_bwd", grid=(nhb, nc),
        in_specs=in_specs + [st_spec, y_spec], out_specs=out_specs, out_shape=out_shape,
        scratch_shapes=[pltpu.VMEM((hb, GDN_D, GDN_D), f32), pltpu.VMEM((3, 8, w), f32)],
        compiler_params=_params(("arbitrary", "arbitrary")),
    )(u, u, u, u, u, u, u, u, conv_w8, conv_w8, conv_w8, alog8, dtb8, nw8, states, dy)


def _ssd_act(xs_r, b_r, c_r, hx, hbm, hcm, tx, tb, tc, bx, bb, bc, *, row0):
    valid = (row0 + _iota((SSD_CHUNK, 1), 0)) >= PAD
    act = lambda x, h, t, b: jnp.where(valid, _silu(_conv4(x, h, t) + b), 0.0)
    return act(xs_r, hx, tx, bx), act(b_r, hbm, tb, bb), act(c_r, hcm, tc, bc)


def _ssd_core(xs, bm, cm, z, nw, m0, m1, m2, m3, h, dt, dtb, alog, dsk, *, row0):
    n = SSD_CHUNK
    valid = (row0 + _iota((n, 1), 0)) >= PAD
    dtp16 = _softplus(dt + dtb)
    a16 = -jnp.exp(alog)
    pick = lambda x, m: jnp.sum(x * m, axis=1, keepdims=True)
    lane_r = _iota((1, 256), 1) >> 6
    dtp = jnp.zeros((n, 256), f32)
    adt = jnp.zeros((n, 256), f32)
    dlane = jnp.zeros((1, 256), f32)
    acols = []
    for r, m in enumerate((m0, m1, m2, m3)):
        dcol = jnp.where(valid, pick(dtp16, m), 0.0)
        acol = dcol * pick(a16, m)
        dtp = jnp.where(lane_r == r, dcol, dtp)
        adt = jnp.where(lane_r == r, acol, adt)
        dlane = jnp.where(lane_r == r, pick(dsk, m), dlane)
        acols.append(acol)

    ri, ci = _iota((n, n), 0), _iota((n, n), 1)
    incl = ci <= ri
    inclf = incl.astype(f32)
    acum = _dotx(inclf, adt)
    al = jnp.sum(adt, axis=0, keepdims=True)
    xdt = xs * dtp
    cb = _dot(cm, bm, NT)
    y = _dot(cm, h) * jnp.exp(acum) + dlane * xs
    for r in range(SSD_HPG):
        ab = jnp.broadcast_to(acols[r], (n, n))
        ai = _dotx(inclf, ab)
        aj = _dotx(jnp.ones((n, n), f32), jnp.where(ri <= ci, ab, 0.0))
        lm = jnp.where(incl, jnp.exp(jnp.where(incl, ai - aj, 0.0)), 0.0)
        y = y + _dot(cb * lm, jnp.where(lane_r == r, xdt, 0.0))
    h_out = h * jnp.exp(al) + _dot(bm, jnp.exp(al - acum) * xdt, TN)
    y = y * _silu(z)
    y = y * lax.rsqrt(jnp.mean(y * y, axis=1, keepdims=True) + RMS_EPS) * nw
    return y, h_out


def _ssd_core_groups(row0):
    return jax.vmap(functools.partial(_ssd_core, row0=row0), in_axes=(0,) * 10 + (None,) * 4)


def _ssd_specs(nc, rev):
    n = SSD_CHUNK

    def cidx(c):
        return (nc - 1 - c) if rev else c

    def col(base, w):
        return pl.BlockSpec((n, w), lambda c: (cidx(c), base // w))

    def halo(base, w):
        return pl.BlockSpec((8, w), lambda c: (jnp.maximum(cidx(c) * (n // 8) - 1, 0), base // w))

    def taps(base, w):
        return pl.BlockSpec((8, w), lambda c: (0, base // w))

    row = pl.BlockSpec((8, 128), lambda c: (0, 0))
    in_specs = [col(C_SX, 1024), col(C_SB, 512), col(C_SC, 512), halo(C_SX, 1024), halo(C_SB, 512), halo(C_SC, 512),
                col(C_SZ, 1024), col(C_SDT, 128), taps(0, 1024), taps(1024, 512), taps(1536, 512), row, row, row,
                taps(0, 1024)]
    y = pl.BlockSpec((n, D_MODEL), lambda c: (cidx(c), 0))
    st = pl.BlockSpec((1, SSD_GROUPS, SSD_N, 256), lambda c: (cidx(c), 0, 0, 0))
    return in_specs, y, st, col, taps, row


def _ssd_load(refs, first):
    xs, bm, cm, hx, hbm, hcm, z, dt, tx, tb, tc, dtb, alog, dsk, nw = refs

    def halo(r):
        return jnp.where(first, 0.0, r[...])

    def taps(r):
        return tuple(r[j:j + 1, :] for j in range(4))

    act = (xs[...], bm[...], cm[...], halo(hx), halo(hbm), halo(hcm), taps(tx), taps(tb), taps(tc),
           tx[4:5, :], tb[4:5, :], tc[4:5, :])
    return act, (z[...], nw[0:1, :]), (dt[...], dtb[0:1, :], alog[0:1, :], dsk[0:1, :])


def _groups(a, w):
    return jnp.stack([a[:, i * w:(i + 1) * w] for i in range(SSD_GROUPS)])


def _ssd_masks():
    head = _iota((SSD_GROUPS, 1, 128), 0) * SSD_HPG
    lane = _iota((SSD_GROUPS, 1, 128), 2)
    return tuple((lane == head + r).astype(f32) for r in range(SSD_HPG))


def ssd_fwd(u, conv_w8, dtb8, alog8, d8, nw8):
    t_rows = u.shape[0]
    nc = t_rows // SSD_CHUNK
    in_specs, y_spec, st_spec, *_ = _ssd_specs(nc, False)

    def body(*refs):
        ins, (y_ref, st_ref), (h_scr,) = refs[:15], refs[15:17], refs[17:]
        c = pl.program_id(0)

        @pl.when(c == 0)
        def _():
            h_scr[...] = jnp.zeros_like(h_scr)

        act, (z, nw), shared = _ssd_load(ins, c == 0)
        h = h_scr[...]
        st_ref[0] = h
        xs, bm, cm = _ssd_act(*act, row0=c * SSD_CHUNK)
        y, h_new = _ssd_core_groups(c * SSD_CHUNK)(_groups(xs, 256), _groups(bm, 128), _groups(cm, 128),
                                                   _groups(z, 256), _groups(nw, 256), *_ssd_masks(), h, *shared)
        y_ref[...] = _wide(y).astype(bf16)
        h_scr[...] = h_new

    return pl.pallas_call(
        body, name="ssd_fwd", grid=(nc,), in_specs=in_specs, out_specs=(y_spec, st_spec),
        out_shape=(jax.ShapeDtypeStruct((t_rows, D_MODEL), bf16),
                   jax.ShapeDtypeStruct((nc, SSD_GROUPS, SSD_N, 256), f32)),
        scratch_shapes=[pltpu.VMEM((SSD_GROUPS, SSD_N, 256), f32)],
        compiler_params=_params(("arbitrary",)),
    )(u, u, u, u, u, u, u, u, conv_w8, conv_w8, conv_w8, dtb8, alog8, d8, nw8)


def ssd_bwd(u, conv_w8, dtb8, alog8, d8, nw8, states, dy):
    t_rows = u.shape[0]
    nc = t_rows // SSD_CHUNK
    n = SSD_CHUNK
    in_specs, y_spec, st_spec, col, taps, row = _ssd_specs(nc, True)

    def body(*refs):
        ins, st_ref, dy_ref = refs[:15], refs[15], refs[16]
        (dz_ref, dxs_ref, db_ref, dc_ref, ddt_ref, dtx_ref, dtb_ref, dtc_ref, ddtb_ref, dalog_ref, ddsk_ref,
         dnw_ref) = refs[17:29]
        dh_scr, hx_scr, hb_scr, hc_scr = refs[29:]
        cc = pl.program_id(0)
        c = nc - 1 - cc

        @pl.when(cc == 0)
        def _():
            for r in (dh_scr, hx_scr, hb_scr, hc_scr, dtx_ref, dtb_ref, dtc_ref, dnw_ref, ddtb_ref, dalog_ref, ddsk_ref):
                r[...] = jnp.zeros_like(r)

        act, (z, nw), shared = _ssd_load(ins, c == 0)
        (xs, bm, cm), vjp_act = jax.vjp(functools.partial(_ssd_act, row0=c * n), *act)
        _, vjp_core = jax.vjp(_ssd_core_groups(c * n), _groups(xs, 256), _groups(bm, 128), _groups(cm, 128),
                              _groups(z, 256), _groups(nw, 256), *_ssd_masks(), st_ref[0], *shared)
        dxa, dba, dca, dz, dnw, _, _, _, _, dh, ddt, ddtb, dalog, ddsk = vjp_core(
            (_groups(dy_ref[...].astype(f32), 256), dh_scr[...]))
        dh_scr[...] = dh
        dxs, dbm, dcm, dhx, dhb, dhc, dtx, dtb, dtc, dbx, dbb, dbc = vjp_act((_wide(dxa), _wide(dba), _wide(dca)))
        for dx, dhalo, scr, out in ((dxs, dhx, hx_scr, dxs_ref), (dbm, dhb, hb_scr, db_ref), (dcm, dhc, hc_scr, dc_ref)):
            zeros = jnp.zeros((n - 8, dx.shape[1]), f32)
            out[...] = (dx + jnp.concatenate([zeros, scr[...]], axis=0)).astype(bf16)
            scr[...] = dhalo
        dz_ref[...] = _wide(dz).astype(bf16)
        ddt_ref[...] = ddt
        for ref, dtaps, dbias in ((dtx_ref, dtx, dbx), (dtb_ref, dtb, dbb), (dtc_ref, dtc, dbc)):
            for j in range(4):
                ref[j:j + 1, :] += dtaps[j]
            ref[4:5, :] += dbias
        ddtb_ref[0:1, :] += ddtb
        dalog_ref[0:1, :] += dalog
        ddsk_ref[0:1, :] += ddsk
        dnw_ref[0:1, :] += _wide(dnw)

    def out_col(w):
        return pl.BlockSpec((n, w), lambda c: (nc - 1 - c, 0))

    out_specs = (out_col(D_MODEL), out_col(D_MODEL), out_col(512), out_col(512), out_col(128),
                 taps(0, D_MODEL), taps(0, 512), taps(0, 512), row, row, row, taps(0, D_MODEL))
    out_shape = (jax.ShapeDtypeStruct((t_rows, D_MODEL), bf16), jax.ShapeDtypeStruct((t_rows, D_MODEL), bf16),
                 jax.ShapeDtypeStruct((t_rows, 512), bf16), jax.ShapeDtypeStruct((t_rows, 512), bf16),
                 jax.ShapeDtypeStruct((t_rows, 128), f32),
                 jax.ShapeDtypeStruct((8, D_MODEL), f32), jax.ShapeDtypeStruct((8, 512), f32),
                 jax.ShapeDtypeStruct((8, 512), f32),
                 jax.ShapeDtypeStruct((8, 128), f32), jax.ShapeDtypeStruct((8, 128), f32),
                 jax.ShapeDtypeStruct((8, 128), f32), jax.ShapeDtypeStruct((8, D_MODEL), f32))
    return pl.pallas_call(
        body, name="ssd_bwd", grid=(nc,), in_specs=in_specs + [st_spec, y_spec],
        out_specs=out_specs, out_shape=out_shape,
        scratch_shapes=[pltpu.VMEM((SSD_GROUPS, SSD_N, 256), f32), pltpu.VMEM((8, D_MODEL), f32),
                        pltpu.VMEM((8, 512), f32), pltpu.VMEM((8, 512), f32)],
        compiler_params=_params(("arbitrary",)),
    )(u, u, u, u, u, u, u, u, conv_w8, conv_w8, conv_w8, dtb8, alog8, d8, nw8, states, dy)


NEG = -1e30


def _swa_core(q, kc, kp, km, vc, vp, vm, sink, *, n):
    rows = SWA_REP * SWA_W
    ri, ci = _iota((rows, SWA_W), 0) & (SWA_W - 1), _iota((rows, SWA_W), 1)
    causal = ci <= ri
    m_cur = causal & ((n >= 1) | ((ci >= PAD) & (ri >= PAD)))
    m_prev = (n >= 2) & (ci > ri)
    m_meta = (n >= 1) & (ci >= PAD)
    q = q * (SWA_D ** -0.5)
    sc = jnp.where(m_cur, _dot(q, kc, NT), NEG)
    sp = jnp.where(m_prev, _dot(q, kp, NT), NEG)
    sm = jnp.where(m_meta, _dot(q, km, NT), NEG)
    mx = jnp.maximum(jnp.maximum(jnp.max(sc, axis=1, keepdims=True), jnp.max(sp, axis=1, keepdims=True)),
                     jnp.maximum(jnp.max(sm, axis=1, keepdims=True), sink))
    mx = lax.stop_gradient(mx)
    ec, ep, em = jnp.exp(sc - mx), jnp.exp(sp - mx), jnp.exp(sm - mx)
    den = (jnp.sum(ec, axis=1, keepdims=True) + jnp.sum(ep, axis=1, keepdims=True)
           + jnp.sum(em, axis=1, keepdims=True) + jnp.exp(sink - mx))
    return (_dot(ec, vc) + _dot(ep, vp) + _dot(em, vm)) / den


def _swa_block(q16, kc, kp, km, vc, vp, vm, sink16, *, n):
    rows = SWA_REP * SWA_W
    lane = _iota((1, 128), 1)
    rep = _iota((rows, 1), 0) >> 7
    cols = []
    for h in range(SWA_KV_HEADS):
        col = jnp.zeros((rows, 1), f32)
        for r in range(SWA_REP):
            s = jnp.sum(jnp.where(lane == h * SWA_REP + r, sink16, 0.0), axis=1, keepdims=True)
            col = jnp.where(rep == r, s, col)
        cols.append(col)
    o = jax.vmap(functools.partial(_swa_core, n=n))(q16.reshape(SWA_KV_HEADS, rows, SWA_D), kc, kp, km, vc, vp, vm,
                                                    jnp.concatenate([col[None] for col in cols], axis=0))
    return o.reshape(q16.shape)


def _swa_specs(nb, rev):
    def bidx(n):
        return (nb - 1 - n) if rev else n

    q = pl.BlockSpec((SWA_Q_HEADS, SWA_W, SWA_D), lambda n: (0, bidx(n), 0))
    cur = pl.BlockSpec((SWA_KV_HEADS, SWA_W, SWA_D), lambda n: (0, bidx(n), 0))
    prev = pl.BlockSpec((SWA_KV_HEADS, SWA_W, SWA_D), lambda n: (0, jnp.maximum(bidx(n) - 1, 0), 0))
    meta = pl.BlockSpec((SWA_KV_HEADS, SWA_W, SWA_D), lambda n: (0, 0, 0))
    row = pl.BlockSpec((8, 128), lambda n: (0, 0))
    return [q, cur, prev, meta, cur, prev, meta, row], q, cur, row


def swa_fwd(q, k, v, sink8):
    t_rows = q.shape[1]
    nb = t_rows // SWA_W
    in_specs, q_spec, _, _ = _swa_specs(nb, False)

    def body(q_ref, kc, kp, km, vc, vp, vm, sink_ref, o_ref):
        o_ref[...] = _swa_block(q_ref[...], kc[...], kp[...], km[...], vc[...], vp[...], vm[...], sink_ref[0:1, :],
                                n=pl.program_id(0)).astype(bf16)

    return pl.pallas_call(
        body, name="swa_fwd", grid=(nb,), in_specs=in_specs, out_specs=q_spec,
        out_shape=jax.ShapeDtypeStruct(q.shape, bf16),
        compiler_params=_params(("arbitrary",)),
    )(q, k, k, k, v, v, v, sink8)


def swa_bwd(q, k, v, sink8, do):
    t_rows = q.shape[1]
    nb = t_rows // SWA_W
    in_specs, q_spec, kv_spec, row = _swa_specs(nb, True)

    def body(q_ref, kc, kp, km, vc, vp, vm, sink_ref, do_ref, dq_ref, dk_ref, dv_ref, dsink_ref,
             dkp_scr, dvp_scr, dkm_scr, dvm_scr):
        nn = pl.program_id(0)
        n = nb - 1 - nn

        @pl.when(nn == 0)
        def _():
            for r in (dkp_scr, dvp_scr, dkm_scr, dvm_scr, dsink_ref):
                r[...] = jnp.zeros_like(r)

        fn = functools.partial(_swa_block, n=n)
        _, vjp = jax.vjp(fn, q_ref[...], kc[...], kp[...], km[...], vc[...], vp[...], vm[...], sink_ref[0:1, :])
        dq, dkc, dkp, dkm, dvc, dvp, dvm, dsink = vjp(do_ref[...].astype(f32))
        dq_ref[...] = dq.astype(bf16)
        dkm_scr[...] += dkm
        dvm_scr[...] += dvm
        first = n == 0
        dk_ref[...] = (dkc + dkp_scr[...] + jnp.where(first, dkm_scr[...], 0.0)).astype(bf16)
        dv_ref[...] = (dvc + dvp_scr[...] + jnp.where(first, dvm_scr[...], 0.0)).astype(bf16)
        dkp_scr[...] = dkp
        dvp_scr[...] = dvp
        dsink_ref[0:1, :] += dsink

    kv_shape = jax.ShapeDtypeStruct(k.shape, bf16)
    return pl.pallas_call(
        body, name="swa_bwd", grid=(nb,), in_specs=in_specs + [q_spec],
        out_specs=(q_spec, kv_spec, kv_spec, row),
        out_shape=(jax.ShapeDtypeStruct(q.shape, bf16), kv_shape, kv_shape, jax.ShapeDtypeStruct((8, 128), f32)),
        scratch_shapes=[pltpu.VMEM((SWA_KV_HEADS, SWA_W, SWA_D), f32)] * 4,
        compiler_params=_params(("arbitrary",)),
    )(q, k, k, k, v, v, v, sink8, do)


def _tile(dim, prefs):
    for p in prefs:
        if dim % p == 0:
            return p
    return dim


def mm(a, b, *, out_dtype, name, resid=None):
    m, k = a.shape
    n = b.shape[1]
    rhs_stays = k * 2 * 1024 > MM_OPERAND_BYTES
    if rhs_stays:
        tn = _tile(n, tuple(p for p in (512, 256, 128) if p * k * 2 <= MM_RESIDENT_BYTES))
        tm = _tile(m, tuple(p for p in (512, 384, 256, 128) if p * k * 2 <= MM_OPERAND_BYTES // 2))
        grid = (n // tn, m // tm)
        ij = lambda o, i: (i, o)
    else:
        tm = _tile(m, tuple(p for p in (1408, 1024, 512, 384, 256, 128) if p * k * 2 <= MM_OPERAND_BYTES))
        tn = _tile(n, tuple(p for p in (512, 256, 128) if p * k * 2 <= MM_OPERAND_BYTES // 2))
        grid = (m // tm, n // tn)
        ij = lambda o, i: (o, i)

    def body(*refs):
        a_ref, b_ref = refs[:2]
        o = _dot(a_ref[...], b_ref[...])
        if resid is not None:
            o = o + refs[2][...]
        refs[-1][...] = o.astype(out_dtype)

    in_specs = [pl.BlockSpec((tm, k), lambda o, i: (ij(o, i)[0], 0)), pl.BlockSpec((k, tn), lambda o, i: (0, ij(o, i)[1]))]
    args = [a, b]
    if resid is not None:
        in_specs.append(pl.BlockSpec((tm, tn), ij))
        args.append(resid)
    return pl.pallas_call(
        body, name=name, grid=grid, in_specs=in_specs, out_specs=pl.BlockSpec((tm, tn), ij),
        out_shape=jax.ShapeDtypeStruct((m, n), out_dtype),
        compiler_params=_params(("parallel", "parallel")),
    )(*args)


def _rows(t_rows):
    return _tile(t_rows, (384, 256, 128))


def _rmsnorm(h, w):
    return h * lax.rsqrt(jnp.mean(h * h, axis=1, keepdims=True) + RMS_EPS) * w


def rmsnorm_fwd(h, w8, *, name):
    t_rows, d = h.shape
    tr = _rows(t_rows)

    def body(h_ref, w_ref, o_ref):
        o_ref[...] = _rmsnorm(h_ref[...], w_ref[0:1, :]).astype(bf16)

    blk = pl.BlockSpec((tr, d), lambda i: (i, 0))
    return pl.pallas_call(
        body, name=name, grid=(t_rows // tr,), in_specs=[blk, pl.BlockSpec((8, d), lambda i: (0, 0))], out_specs=blk,
        out_shape=jax.ShapeDtypeStruct((t_rows, d), bf16), compiler_params=_params(("arbitrary",)),
    )(h, w8)


def rmsnorm_bwd(h, w8, dhn, dres, *, name):
    t_rows, d = h.shape
    tr = _rows(t_rows)

    def body(h_ref, w_ref, dhn_ref, dres_ref, dh_ref, dw_ref):
        @pl.when(pl.program_id(0) == 0)
        def _():
            dw_ref[...] = jnp.zeros_like(dw_ref)

        _, vjp = jax.vjp(_rmsnorm, h_ref[...], w_ref[0:1, :])
        dh, dw = vjp(dhn_ref[...])
        dh_ref[...] = dh + dres_ref[...]
        dw_ref[0:1, :] += dw

    blk = pl.BlockSpec((tr, d), lambda i: (i, 0))
    wblk = pl.BlockSpec((8, d), lambda i: (0, 0))
    return pl.pallas_call(
        body, name=name, grid=(t_rows // tr,), in_specs=[blk, wblk, blk, blk], out_specs=(blk, wblk),
        out_shape=(jax.ShapeDtypeStruct((t_rows, d), f32), jax.ShapeDtypeStruct((8, d), f32)),
        compiler_params=_params(("arbitrary",)),
    )(h, w8, dhn, dres)


def _merge(pg, ps, pw, la, lb, lc):
    return jax.nn.sigmoid(la) * pg + jax.nn.sigmoid(lb) * ps + jax.nn.sigmoid(lc) * pw


def _merge_specs(t_rows):
    tr = _rows(t_rows)
    blk = pl.BlockSpec((tr, D_MODEL), lambda i: (i, 0))
    gate = [pl.BlockSpec((tr, D_MODEL), functools.partial(lambda i, j: (i, j), j=C_GATE // D_MODEL + j)) for j in range(3)]
    return tr, blk, gate


def merge_fwd(pg, ps, pw, u):
    t_rows = pg.shape[0]
    tr, blk, gate = _merge_specs(t_rows)

    def body(pg_ref, ps_ref, pw_ref, la, lb, lc, o_ref):
        o_ref[...] = _merge(pg_ref[...], ps_ref[...], pw_ref[...], la[...], lb[...], lc[...]).astype(bf16)

    return pl.pallas_call(
        body, name="merge_fwd", grid=(t_rows // tr,), in_specs=[blk, blk, blk] + gate, out_specs=blk,
        out_shape=jax.ShapeDtypeStruct((t_rows, D_MODEL), bf16), compiler_params=_params(("arbitrary",)),
    )(pg, ps, pw, u, u, u)


def merge_bwd(pg, ps, pw, u, dmerged):
    t_rows = pg.shape[0]
    tr, blk, gate = _merge_specs(t_rows)

    def body(pg_ref, ps_ref, pw_ref, la, lb, lc, dm_ref, dpg_ref, dps_ref, dpw_ref, dl_ref):
        _, vjp = jax.vjp(_merge, pg_ref[...], ps_ref[...], pw_ref[...], la[...], lb[...], lc[...])
        dpg, dps, dpw, dla, dlb, dlc = vjp(dm_ref[...])
        dpg_ref[...] = dpg.astype(bf16)
        dps_ref[...] = dps.astype(bf16)
        dpw_ref[...] = dpw.astype(bf16)
        for j, dl in enumerate((dla, dlb, dlc)):
            dl_ref[:, j * D_MODEL:(j + 1) * D_MODEL] = dl.astype(bf16)

    act = jax.ShapeDtypeStruct((t_rows, D_MODEL), bf16)
    return pl.pallas_call(
        body, name="merge_bwd", grid=(t_rows // tr,), in_specs=[blk, blk, blk] + gate + [blk],
        out_specs=(blk, blk, blk, pl.BlockSpec((tr, 3 * D_MODEL), lambda i: (i, 0))),
        out_shape=(act, act, act, jax.ShapeDtypeStruct((t_rows, 3 * D_MODEL), bf16)),
        compiler_params=_params(("arbitrary",)),
    )(pg, ps, pw, u, u, u, dmerged)


def relu2_fwd(a):
    t_rows, d = a.shape
    tr = _rows(t_rows)

    def body(a_ref, o_ref):
        r = jnp.maximum(a_ref[...], 0.0)
        o_ref[...] = (r * r).astype(bf16)

    blk = pl.BlockSpec((tr, d), lambda i: (i, 0))
    return pl.pallas_call(
        body, name="relu2_fwd", grid=(t_rows // tr,), in_specs=[blk], out_specs=blk,
        out_shape=jax.ShapeDtypeStruct((t_rows, d), bf16), compiler_params=_params(("arbitrary",)),
    )(a)


def relu2_bwd(a, dr):
    t_rows, d = a.shape
    tr = _rows(t_rows)

    def body(a_ref, dr_ref, o_ref):
        o_ref[...] = (dr_ref[...] * 2.0 * jnp.maximum(a_ref[...], 0.0)).astype(bf16)

    blk = pl.BlockSpec((tr, d), lambda i: (i, 0))
    return pl.pallas_call(
        body, name="relu2_bwd", grid=(t_rows // tr,), in_specs=[blk, blk], out_specs=blk,
        out_shape=jax.ShapeDtypeStruct((t_rows, d), bf16), compiler_params=_params(("arbitrary",)),
    )(a, dr)


def loss_head(h, w8, target):
    t_rows, d = h.shape
    tr = HEAD_ROWS

    def loss_fn(hb, w, tgt):
        err = _rmsnorm(hb, w) - tgt
        return 0.5 * jnp.sum(err * err) / d

    def body(h_ref, w_ref, t_ref, loss_ref, dh_ref, dw_ref):
        i = pl.program_id(0)

        @pl.when(i == 0)
        def _():
            loss_ref[...] = jnp.zeros_like(loss_ref)
            dw_ref[...] = jnp.zeros_like(dw_ref)
            dh_ref[...] = jnp.zeros_like(dh_ref)

        @pl.when(i > 0)
        def _():
            val, (dh, dw) = jax.value_and_grad(loss_fn, argnums=(0, 1))(h_ref[...], w_ref[0:1, :], t_ref[...])
            loss_ref[...] += val
            dh_ref[...] = dh
            dw_ref[0:1, :] += dw

    blk = pl.BlockSpec((tr, d), lambda i: (i, 0))
    wblk = pl.BlockSpec((8, d), lambda i: (0, 0))
    return pl.pallas_call(
        body, name="loss_head", grid=(t_rows // tr,),
        in_specs=[blk, wblk, pl.BlockSpec((tr, d), lambda i: (jnp.maximum(i - 1, 0), 0))],
        out_specs=(pl.BlockSpec((8, 128), lambda i: (0, 0)), blk, wblk),
        out_shape=(jax.ShapeDtypeStruct((8, 128), f32), jax.ShapeDtypeStruct((t_rows, d), f32),
                   jax.ShapeDtypeStruct((8, d), f32)),
        compiler_params=_params(("arbitrary",)),
    )(h, w8, target)


def adamw(w, m, v, partials, row_off, *, name):
    rows, d = w.shape
    layers = len(partials)
    per = rows // layers
    tr = _tile(per, tuple(p for p in (512, 256, 128, 64, 16, 8) if p * d * 4 <= BLOCK_BYTES))
    assert row_off % tr == 0
    off, nblk = row_off // tr, per // tr
    c1 = 1.0 - ADAM_B1 ** ADAM_STEP
    c2 = 1.0 - ADAM_B2 ** ADAM_STEP

    def body(w_ref, m_ref, v_ref, *refs):
        p_refs, (g_ref, d_ref, mo_ref, vo_ref) = refs[:2 * layers], refs[2 * layers:]
        g = p_refs[0][...] + p_refs[1][...]
        for l in range(1, layers):
            g = jnp.where(pl.program_id(0) >= l * nblk, p_refs[2 * l][...] + p_refs[2 * l + 1][...], g)
        m_new = ADAM_B1 * m_ref[...] + (1.0 - ADAM_B1) * g
        v_new = ADAM_B2 * v_ref[...] + (1.0 - ADAM_B2) * (g * g)
        g_ref[...] = g
        d_ref[...] = -ADAM_LR * ((m_new / c1) / (jnp.sqrt(v_new / c2) + ADAM_EPS) + ADAM_WD * w_ref[...])
        mo_ref[...] = m_new
        vo_ref[...] = v_new

    blk = pl.BlockSpec((tr, d), lambda i: (i, 0))
    pblks = [pl.BlockSpec((tr, d), functools.partial(lambda i, l: (off + jnp.clip(i - l * nblk, 0, nblk - 1), 0), l=l))
             for l in range(layers) for _ in range(2)]
    out = jax.ShapeDtypeStruct((rows, d), f32)
    return pl.pallas_call(
        body, name=name, grid=(rows // tr,), in_specs=[blk, blk, blk] + pblks, out_specs=(blk,) * 4,
        out_shape=(out,) * 4, compiler_params=_params(("arbitrary",)),
    )(w, m, v, *[p for pair in partials for p in pair])


def reduce4(parts, *, name, own=None, me=None):
    _, rows, d = parts.shape
    tr = _tile(rows, tuple(p for p in (512, 256, 128, 64, 8) if p * d * 4 <= BLOCK_BYTES))

    def body(*refs):
        p_ref, o_ref = refs[0], refs[-1]
        acc = None
        for s in range(4):
            term = p_ref[s].astype(f32)
            if own is not None:
                term = jnp.where(refs[2][0] == s, refs[1][...].astype(f32), term)
            acc = term if acc is None else acc + term
        o_ref[...] = acc

    in_specs = [pl.BlockSpec((4, tr, d), lambda i: (0, i, 0))]
    args = [parts]
    if own is not None:
        in_specs += [pl.BlockSpec((tr, d), lambda i: (i, 0)), pl.BlockSpec(memory_space=pltpu.SMEM)]
        args += [own, me]
    return pl.pallas_call(
        body, name=name, grid=(rows // tr,), in_specs=in_specs,
        out_specs=pl.BlockSpec((tr, d), lambda i: (i, 0)), out_shape=jax.ShapeDtypeStruct((rows, d), f32),
        compiler_params=_params(("arbitrary",)),
    )(*args)


ANY = pl.BlockSpec(memory_space=pl.ANY)
MESH = pl.DeviceIdType.MESH
CHIP_FLIPS = ((0, 1), (1, 0), (1, 1))


def chip_exchange(bufs, scatter, *, name):
    nb = len(bufs)

    def body(*refs):
        ins, outs = refs[:nb], refs[nb:2 * nb]
        send_sems, recv_sems, local_sems = refs[2 * nb:]
        x, y, c = lax.axis_index("x"), lax.axis_index("y"), lax.axis_index("c")
        me = 2 * x + y
        local = [pltpu.make_async_copy(ins[j].at[me] if scatter[j] else ins[j], outs[j].at[me], local_sems.at[j])
                 for j in range(nb)]
        for cp in local:
            cp.start()
        sends, recvs = [], []
        for k, (fx, fy) in enumerate(CHIP_FLIPS):
            px = 1 - x if fx else x
            py = 1 - y if fy else y
            chip = 2 * px + py
            for j in range(nb):
                src = ins[j].at[chip] if scatter[j] else ins[j]
                sems = dict(send_sem=send_sems.at[nb * k + j], recv_sem=recv_sems.at[nb * k + j],
                            device_id=(px, py, c), device_id_type=MESH)
                sends.append(pltpu.make_async_remote_copy(src_ref=src, dst_ref=outs[j].at[me], **sems))
                recvs.append(pltpu.make_async_remote_copy(src_ref=src, dst_ref=outs[j].at[chip], **sems))
        for cp in sends:
            cp.start()
        for cp in recvs:
            cp.wait_recv()
        for cp in sends:
            cp.wait_send()
        for cp in local:
            cp.wait()

    out_shape = tuple(jax.ShapeDtypeStruct(b.shape if s else (4,) + b.shape, b.dtype) for b, s in zip(bufs, scatter))
    return pl.pallas_call(
        body, name=name, in_specs=[ANY] * nb, out_specs=(ANY,) * nb, out_shape=out_shape,
        scratch_shapes=[pltpu.SemaphoreType.DMA((3 * nb,)), pltpu.SemaphoreType.DMA((3 * nb,)),
                        pltpu.SemaphoreType.DMA((nb,))],
        compiler_params=pltpu.CompilerParams(has_side_effects=True),
    )(*bufs)


def sibling_swap(bufs, *, name):
    nb = len(bufs)

    def body(*refs):
        ins, outs, (send_sems, recv_sems) = refs[:nb], refs[nb:2 * nb], refs[2 * nb:]
        peer = (lax.axis_index("x"), lax.axis_index("y"), 1 - lax.axis_index("c"))
        copies = [pltpu.make_async_remote_copy(src_ref=ins[j], dst_ref=outs[j], send_sem=send_sems.at[j],
                                               recv_sem=recv_sems.at[j], device_id=peer, device_id_type=MESH)
                  for j in range(nb)]
        for cp in copies:
            cp.start()
        for cp in copies:
            cp.wait_recv()
        for cp in copies:
            cp.wait_send()

    return pl.pallas_call(
        body, name=name, in_specs=[ANY] * nb, out_specs=(ANY,) * nb,
        out_shape=tuple(jax.ShapeDtypeStruct(b.shape, b.dtype) for b in bufs),
        scratch_shapes=[pltpu.SemaphoreType.DMA((nb,)), pltpu.SemaphoreType.DMA((nb,))],
        compiler_params=pltpu.CompilerParams(has_side_effects=True),
    )(*bufs)


HBM = pl.BlockSpec(memory_space=pltpu.HBM)
SEM = pl.BlockSpec(memory_space=pltpu.SEMAPHORE)
DATAFLOW = pltpu.SideEffectType.DATAFLOW_SIDE_EFFECTING


def _exchange_copies(srcs, lands, send_sems, recv_sems, scatter):
    x, y, c = lax.axis_index("x"), lax.axis_index("y"), lax.axis_index("c")
    me = 2 * x + y
    nb = len(srcs)
    pairs = []
    for k, (fx, fy) in enumerate(CHIP_FLIPS):
        px = 1 - x if fx else x
        py = 1 - y if fy else y
        chip = 2 * px + py
        for j in range(nb):
            src = srcs[j].at[chip] if scatter[j] else srcs[j]
            sems = dict(send_sem=send_sems.at[nb * k + j], recv_sem=recv_sems.at[nb * k + j],
                        device_id=(px, py, c), device_id_type=MESH)
            pairs.append((pltpu.make_async_remote_copy(src_ref=src, dst_ref=lands[j].at[me], **sems),
                          pltpu.make_async_remote_copy(src_ref=src, dst_ref=lands[j].at[chip], **sems)))
    return pairs


def exchange_start(bufs, scatter, after, *, name):
    nb = len(bufs)
    slabs = [b.shape[1:] if s else b.shape for b, s in zip(bufs, scatter)]
    lands = [lax.empty((4,) + shp, b.dtype) for b, shp in zip(bufs, slabs)]

    def body(*refs):
        srcs, zones = refs[:nb], refs[nb:2 * nb]
        send_sems, recv_sems = refs[2 * nb + 1:2 * nb + 3]
        token = refs[-1]
        for send, _ in _exchange_copies(srcs, zones, send_sems, recv_sems, scatter):
            send.start()
        token[...] = jnp.zeros_like(token)

    hbm = lambda a: pltpu.with_memory_space_constraint(a, pltpu.HBM)
    out = pl.pallas_call(
        body, name=name, in_specs=[HBM] * (2 * nb) + [ANY],
        out_specs=(SEM, SEM) + (HBM,) * (2 * nb) + (pl.BlockSpec(memory_space=pltpu.VMEM),),
        out_shape=(pltpu.SemaphoreType.DMA((3 * nb,)), pltpu.SemaphoreType.DMA((3 * nb,)))
        + tuple(pltpu.HBM(a.shape, a.dtype) for a in list(bufs) + lands) + (jax.ShapeDtypeStruct((8, 128), f32),),
        input_output_aliases={i: 2 + i for i in range(2 * nb)},
        compiler_params=pltpu.CompilerParams(has_side_effects=DATAFLOW),
    )(*[hbm(a) for a in list(bufs) + lands], after)
    return (out[:2], out[2:2 + nb], out[2 + nb:2 + 2 * nb], scatter), out[-1]


def exchange_wait(state, after, *, name):
    (send_sems, recv_sems), srcs, lands, scatter = state
    nb = len(srcs)

    def body(*refs):
        src_refs, zones = refs[:nb], refs[nb:2 * nb]
        s_sems, r_sems = refs[2 * nb:2 * nb + 2]
        for send, recv in _exchange_copies(src_refs, zones, s_sems, r_sems, scatter):
            send.wait_send()
            recv.wait_recv()

    out = pl.pallas_call(
        body, name=name, in_specs=[HBM] * (2 * nb) + [SEM, SEM, ANY], out_specs=(HBM,) * (2 * nb),
        out_shape=tuple(pltpu.HBM(a.shape, a.dtype) for a in list(srcs) + list(lands)),
        input_output_aliases={i: i for i in range(2 * nb)},
        compiler_params=pltpu.CompilerParams(has_side_effects=DATAFLOW),
    )(*srcs, *lands, send_sems, recv_sems, after)
    return out[nb:]


BIG = (
    ("w_proj_gdn", 256), ("w_proj_ssd", 256), ("w_proj_swa", 256), ("w_out", 256), ("w_up", 1024), ("w_down", 1024))
BIG_OFF = {}
_o = 0
for _n, _r in BIG:
    BIG_OFF[_n] = _o
    _o += _r
BIG_ROWS = _o
W_IN_SHARD = IN_W // 4

W_NAMES = ('meta_tokens', 'norm1_w', 'w_in', 'gdn_conv_w', 'gdn_a_log', 'gdn_dt_bias', 'gdn_norm_w', 'ssd_conv_w',
           'ssd_conv_b', 'ssd_dt_bias', 'ssd_a_log', 'ssd_d', 'ssd_norm_w', 'swa_sinks', 'w_proj_gdn', 'w_proj_ssd',
           'w_proj_swa', 'w_out', 'norm2_w', 'w_up', 'w_down', 'final_norm_w')
SMALL_NAMES = tuple(n for n in W_NAMES if n not in BIG_OFF and n != "w_in")
SMALL_SHARDED = ("meta_tokens", "gdn_conv_w", "ssd_conv_w")


def _pad_rows(a, rows):
    return jnp.pad(a, ((0, rows - a.shape[0]), (0, 0)))


def _pack_rows(parts, dtype):
    flat = jnp.concatenate([p.reshape(-1).astype(dtype) for p in parts])
    n = -(-flat.shape[0] // 8192) * 8192
    return jnp.pad(flat, (0, n - flat.shape[0])).reshape(-1, D_MODEL)


def _unpack_rows(packed, shapes):
    flat, out, o = packed.reshape(-1), [], 0
    for s in shapes:
        n = 1
        for d in s:
            n *= d
        out.append(flat[o:o + n].reshape(s))
        o += n
    return out


def _split_chips(full, axis):
    s = full.shape
    a = full.reshape(s[:axis] + (4, s[axis] // 4) + s[axis + 1:])
    return jnp.moveaxis(a, axis, 0)


def _join_chips(parts, axis):
    a = jnp.moveaxis(parts, 0, axis)
    s = a.shape
    return a.reshape(s[:axis] + (s[axis] * s[axis + 1],) + s[axis + 2:])


BIG_AXIS = {"w_in": 2, "w_proj_gdn": 1, "w_proj_ssd": 1, "w_proj_swa": 1, "w_out": 1, "w_up": 2, "w_down": 1}


def _w_in_to_padded(w):
    z = jnp.zeros(w.shape[:-1] + (112,), w.dtype)
    return jnp.concatenate([w[..., 8736:11808], w[..., 0:4096], w[..., 4112:7184], w[..., 7200:8736],
                            w[..., 4096:4112], z, w[..., 7184:7200], z], axis=-1)


def _w_in_from_padded(p):
    return jnp.concatenate([p[..., C_GQ:C_SZ], p[..., C_BA:C_BA + 16], p[..., C_SZ:C_WQ], p[..., C_SDT:C_SDT + 16],
                            p[..., C_WQ:C_BA], p[..., 0:C_GQ]], axis=-1)


def _row8(v, lane0=0, width=128):
    return jnp.pad(v[None, :], ((0, 7), (lane0, width - lane0 - v.shape[0])))


def _head_major(a, heads):
    return a.reshape(a.shape[0], heads, SWA_D).transpose(1, 0, 2)


def _from_head_major(a):
    return a.transpose(1, 0, 2).reshape(a.shape[1], -1)


def _layer_fwd(h, p, l, late=None):
    tag = f"l{l}"
    hn = rmsnorm_fwd(h, p["n1"], name=f"norm1_fwd_{tag}")
    u = mm(hn, p["w_in"], out_dtype=f32, name=f"mm_in_{tag}")
    yg, stg = gdn_fwd(u, p["gcw"], p["galog"], p["gdtb"], p["gnw"])
    ys, sts = ssd_fwd(u, p["scw"], p["sdtb"], p["salog"], p["sd"], p["snw"])
    qh = _head_major(u[:, C_WQ:C_WK], SWA_Q_HEADS)
    kh = _head_major(u[:, C_WK:C_WV], SWA_KV_HEADS)
    vh = _head_major(u[:, C_WV:C_BA], SWA_KV_HEADS)
    yw = _from_head_major(swa_fwd(qh, kh, vh, p["sink"]))
    if late is not None:
        p.update(late(yw))
    pg = mm(yg, p["wpg"], out_dtype=f32, name=f"mm_pg_{tag}")
    ps = mm(ys, p["wps"], out_dtype=f32, name=f"mm_ps_{tag}")
    pw = mm(yw, p["wpw"], out_dtype=f32, name=f"mm_pw_{tag}")
    merged = merge_fwd(pg, ps, pw, u)
    h2 = mm(merged, p["wout"], out_dtype=f32, resid=h, name=f"mm_out_{tag}")
    hn2 = rmsnorm_fwd(h2, p["n2"], name=f"norm2_fwd_{tag}")
    a = mm(hn2, p["wup"], out_dtype=f32, name=f"mm_up_{tag}")
    r = relu2_fwd(a)
    h3 = mm(r, p["wdown"], out_dtype=f32, resid=h2, name=f"mm_down_{tag}")
    saved = dict(h=h, hn=hn, u=u, yg=yg, stg=stg, ys=ys, sts=sts, qh=qh, kh=kh, vh=vh, yw=yw, pg=pg, ps=ps, pw=pw,
                 merged=merged, h2=h2, hn2=hn2, a=a, r=r)
    return h3, saved


def _layer_bwd(dh3, p, s, l, send_big, send_w_in):
    tag = f"l{l}"
    g = {}

    def wgrad(act, d, name):
        return mm(act.T, d, out_dtype=bf16, name=f"wg_{name}_{tag}")

    da = relu2_bwd(s["a"], mm(dh3, p["wdown_t"], out_dtype=f32, name=f"dg_down_{tag}"))
    g["w_down"] = wgrad(s["r"], dh3, "down")
    dhn2 = mm(da, p["wup_t"], out_dtype=f32, name=f"dg_up_{tag}")
    g["w_up"] = wgrad(s["hn2"], da, "up")
    dh2, g["norm2_w"] = rmsnorm_bwd(s["h2"], p["n2"], dhn2, dh3, name=f"norm2_bwd_{tag}")
    dmerged = mm(dh2, p["wout_t"], out_dtype=f32, name=f"dg_out_{tag}")
    g["w_out"] = wgrad(s["merged"], dh2, "out")
    dpg, dps, dpw, dgl = merge_bwd(s["pg"], s["ps"], s["pw"], s["u"], dmerged)
    dyg = mm(dpg, p["wpg_t"], out_dtype=f32, name=f"dg_pg_{tag}")
    dys = mm(dps, p["wps_t"], out_dtype=f32, name=f"dg_ps_{tag}")
    dyw = mm(dpw, p["wpw_t"], out_dtype=f32, name=f"dg_pw_{tag}")
    g["w_proj_gdn"] = wgrad(s["yg"], dpg, "pg")
    g["w_proj_ssd"] = wgrad(s["ys"], dps, "ps")
    g["w_proj_swa"] = wgrad(s["yw"], dpw, "pw")
    sent = send_big(jnp.concatenate([_split_chips(g.pop(n), BIG_AXIS[n] - 1).reshape(4, r, D_MODEL)
                                     for n, r in BIG], axis=1))

    (dq, dk, dv, dgate, dba, dtq, dtk, dtv, g["gdn_a_log"], g["gdn_dt_bias"], g["gdn_norm_w"]) = gdn_bwd(
        s["u"], p["gcw"] + sent, p["galog"], p["gdtb"], p["gnw"], s["stg"], dyg)
    g["gdn_conv_w"] = jnp.concatenate([dtq, dtk, dtv], axis=1)[:4]
    (dz, dxs, dbm, dcm, ddt, dtx, dtb, dtc, g["ssd_dt_bias"], g["ssd_a_log"], g["ssd_d"], g["ssd_norm_w"]) = ssd_bwd(
        s["u"], p["scw"], p["sdtb"], p["salog"], p["sd"], p["snw"], s["sts"], dys)
    dconv = jnp.concatenate([dtx, dtb, dtc], axis=1)
    g["ssd_conv_w"], g["ssd_conv_b"] = dconv[:4], dconv[4]
    dqh, dkh, dvh, g["swa_sinks"] = swa_bwd(s["qh"], s["kh"], s["vh"], p["sink"], _head_major(dyw, SWA_Q_HEADS))
    du = jnp.concatenate([dgl, dq, dk, dv, dgate, dz, dxs, dbm, dcm, _from_head_major(dqh), _from_head_major(dkh),
                          _from_head_major(dvh), dba.sum(0).astype(bf16), ddt.astype(bf16)], axis=1)
    sent = send_w_in(_split_chips(_w_in_from_padded(wgrad(s["hn"], du, "in")), 1))
    dhn = mm(du, p["w_in_t"], out_dtype=f32, name=f"dg_in_{tag}")
    dh, g["norm1_w"] = rmsnorm_bwd(s["h"], p["n1"] + sent, dhn, dh2, name=f"norm1_bwd_{tag}")
    return dh, g


def kernel(x, meta_tokens, norm1_w, w_in, gdn_conv_w, gdn_a_log, gdn_dt_bias, gdn_norm_w, ssd_conv_w, ssd_conv_b, ssd_dt_bias, ssd_a_log, ssd_d, ssd_norm_w, swa_sinks, w_proj_gdn, w_proj_ssd, w_proj_swa, w_out, norm2_w, w_up, w_down, final_norm_w, loss_target, m_meta_tokens, m_norm1_w, m_w_in, m_gdn_conv_w, m_gdn_a_log, m_gdn_dt_bias, m_gdn_norm_w, m_ssd_conv_w, m_ssd_conv_b, m_ssd_dt_bias, m_ssd_a_log, m_ssd_d, m_ssd_norm_w, m_swa_sinks, m_w_proj_gdn, m_w_proj_ssd, m_w_proj_swa, m_w_out, m_norm2_w, m_w_up, m_w_down, m_final_norm_w, v_meta_tokens, v_norm1_w, v_w_in, v_gdn_conv_w, v_gdn_a_log, v_gdn_dt_bias, v_gdn_norm_w, v_ssd_conv_w, v_ssd_conv_b, v_ssd_dt_bias, v_ssd_a_log, v_ssd_d, v_ssd_norm_w, v_swa_sinks, v_w_proj_gdn, v_w_proj_ssd, v_w_proj_swa, v_w_out, v_norm2_w, v_w_up, v_w_down, v_final_norm_w):
    given = dict(locals())
    depth = norm1_w.shape[0]
    me = 2 * lax.axis_index("x") + lax.axis_index("y")

    me1 = jnp.reshape(me, (1,)).astype(jnp.int32)
    is_me = (jnp.arange(4, dtype=jnp.int32) == me)[:, None, None]

    def weight_slabs(l):
        return (w_in[l].astype(bf16),
                jnp.concatenate([given[n][l].reshape(-1, D_MODEL).astype(bf16) for n, _ in BIG]))

    slabs = [weight_slabs(l) for l in range(depth)]
    wsmall = _pack_rows([given[n] for n in SMALL_SHARDED], f32)
    ga0, gsmall = chip_exchange([slabs[0][0], wsmall], (False, False), name="gather_first")
    gathers, started = {}, jnp.zeros((), f32)
    for l in range(depth):
        for j in range(2):
            if (l, j) != (0, 0):
                gathers[l, j], token = exchange_start([slabs[l][j]], (False,), gsmall, name=f"gather_start_l{l}_{j}")
                started = started + token[0, 0]
    shard_shapes = [given[n].shape for n in SMALL_SHARDED]
    per_chip = [_unpack_rows(gsmall[s], shard_shapes) for s in range(4)]
    full = {n: jnp.concatenate([per_chip[s][i] for s in range(4)], axis=-1) for i, n in enumerate(SMALL_SHARDED)}

    def landed(l, j, after):
        (zone,) = exchange_wait(gathers[l, j], after, name=f"gather_wait_l{l}_{j}")
        return jnp.where(is_me, slabs[l][j][None], zone)

    def first_operands(l, ga, order):
        w_in_p = _w_in_to_padded(_join_chips(ga, 1))
        return dict(
            n1=_row8(norm1_w[l], width=D_MODEL) + order, n2=_row8(norm2_w[l], width=D_MODEL),
            w_in=w_in_p, w_in_t=w_in_p.T,
            gcw=jnp.pad(full["gdn_conv_w"][l], ((0, 4), (0, 0))),
            galog=_row8(gdn_a_log[l], 8), gdtb=_row8(gdn_dt_bias[l], 8), gnw=_row8(gdn_norm_w[l]),
            scw=jnp.pad(jnp.concatenate([full["ssd_conv_w"][l], ssd_conv_b[l][None]], axis=0), ((0, 3), (0, 0))),
            sdtb=_row8(ssd_dt_bias[l]), salog=_row8(ssd_a_log[l]), sd=_row8(ssd_d[l]),
            snw=_row8(ssd_norm_w[l], width=D_MODEL), sink=_row8(swa_sinks[l]))

    def late_operands(l, after):
        gb = landed(l, 1, after)
        w = {}
        for n, r in BIG:
            parts = gb[:, BIG_OFF[n]:BIG_OFF[n] + r].reshape((4,) + given[n].shape[1:])
            w[n] = _join_chips(parts, BIG_AXIS[n] - 1)
        return dict(wpg=w["w_proj_gdn"], wps=w["w_proj_ssd"], wpw=w["w_proj_swa"], wout=w["w_out"],
                    wup=w["w_up"], wdown=w["w_down"],
                    wpg_t=w["w_proj_gdn"].T, wps_t=w["w_proj_ssd"].T, wpw_t=w["w_proj_swa"].T,
                    wout_t=w["w_out"].T, wup_t=w["w_up"].T, wdown_t=w["w_down"].T)

    h = jnp.concatenate([jnp.zeros((PAD, D_MODEL), f32), full["meta_tokens"], x[0]], axis=0)
    layers, saved = [], []
    for l in range(depth):
        p = first_operands(0, ga0, started) if l == 0 else first_operands(l, landed(l, 0, h), 0.0)
        h, s = _layer_fwd(h, p, l, late=functools.partial(late_operands, l))
        layers.append(p)
        saved.append(s)
    loss8, dh, dfw8 = loss_head(h, _row8(final_norm_w, width=D_MODEL), loss_target[0])
    grads = {"final_norm_w": dfw8[0]}
    per_layer, grad_slabs, scatters = [None] * depth, {}, {}

    def send(l, j, slab):
        grad_slabs[l, j] = slab
        scatters[l, j], token = exchange_start([slab], (True,), loss8, name=f"scatter_start_l{l}_{j}")
        return token[0, 0]

    for l in reversed(range(depth)):
        dh, per_layer[l] = _layer_bwd(dh, layers[l], saved[l], l, functools.partial(send, l, 1),
                                      functools.partial(send, l, 0))
    grad_x = dh[HEAD_ROWS:][None]
    grads["meta_tokens"] = dh[PAD:HEAD_ROWS]
    lane = {"gdn_a_log": (8, 8), "gdn_dt_bias": (8, 8), "gdn_norm_w": (0, 128), "ssd_dt_bias": (0, 16),
            "ssd_a_log": (0, 16), "ssd_d": (0, 16), "swa_sinks": (0, 16)}
    for n in per_layer[0]:
        parts = [per_layer[l][n] for l in range(depth)]
        if n in lane:
            parts = [q[0, lane[n][0]:lane[n][0] + lane[n][1]] for q in parts]
        elif n in ("norm1_w", "norm2_w", "ssd_norm_w"):
            parts = [q[0] for q in parts]
        grads[n] = jnp.stack(parts)
    loss = lax.psum(loss8[0, 0], ("x", "y", "c"))

    gs = _pack_rows([grads[n] for n in SMALL_NAMES], f32)
    flat = []
    for l in range(depth):
        for j in range(2):
            (zone,) = exchange_wait(scatters[l, j], dh, name=f"scatter_wait_l{l}_{j}")
            own = lax.dynamic_index_in_dim(grad_slabs[l, j], me, 0, keepdims=False)
            flat.append(reduce4(zone, own=own, me=me1, name=f"sum_chips_l{l}_{j}"))
    (rs,) = chip_exchange([gs], (False,), name="gather_small_grads")
    ps_ = reduce4(rs, name="sum_chips_small")
    flat.append(ps_)
    sib = sibling_swap(flat, name="swap_cores")
    ss = sib[-1]

    out = {}
    w_in_rows = depth * D_MODEL
    res = adamw(*[given[pre + "w_in"].reshape(w_in_rows, W_IN_SHARD) for pre in ("", "m_", "v_")],
                [(flat[2 * l], sib[2 * l]) for l in range(depth)], 0, name="adamw_w_in")
    out["w_in"] = [a.reshape(w_in.shape) for a in res]
    for n, r in BIG:
        shp = given[n].shape
        res = adamw(*[given[pre + n].reshape(depth * r, D_MODEL) for pre in ("", "m_", "v_")],
                    [(flat[2 * l + 1], sib[2 * l + 1]) for l in range(depth)], BIG_OFF[n], name=f"adamw_{n}")
        out[n] = [a.reshape(shp) for a in res]
    full_shapes = [grads[n].shape for n in SMALL_NAMES]
    mine_s, sib_s = _unpack_rows(ps_, full_shapes), _unpack_rows(ss, full_shapes)

    def local(parts):
        loc = []
        for n, a in zip(SMALL_NAMES, parts):
            if n in SMALL_SHARDED:
                sz = a.shape[-1] // 4
                a = lax.dynamic_slice_in_dim(a, me * sz, sz, axis=a.ndim - 1)
            loc.append(a)
        return _pack_rows(loc, f32)

    res = adamw(_pack_rows([given[n] for n in SMALL_NAMES], f32), _pack_rows([given["m_" + n] for n in SMALL_NAMES], f32),
                _pack_rows([given["v_" + n] for n in SMALL_NAMES], f32), [(local(mine_s), local(sib_s))], 0,
                name="adamw_small")
    local_shapes = [given[n].shape for n in SMALL_NAMES]
    unpacked = [_unpack_rows(a, local_shapes) for a in res]
    for i, n in enumerate(SMALL_NAMES):
        out[n] = [unpacked[j][i] for j in range(4)]

    return (loss, grad_x) + tuple(out[n][j] for j in range(4) for n in W_NAMES)
```

```python
import functools

import jax
import jax.numpy as jnp
from jax import lax
from jax.experimental import pallas as pl
from jax.experimental.pallas import tpu as pltpu

f32 = jnp.float32
bf16 = jnp.bfloat16
HI = lax.Precision.HIGHEST

D_MODEL = 1024
N_META = 16
PAD = 112
HEAD_ROWS = PAD + N_META
RMS_EPS = 1e-6
L2_EPS = 1e-6
D_FF = 4 * D_MODEL

GDN_HEADS = 8
GDN_D = 128
GDN_CHUNK = 64
SSD_HEADS = 16
SSD_P = 64
SSD_GROUPS = 4
SSD_HPG = 4
SSD_N = 128
SSD_CHUNK = 128
SWA_Q_HEADS = 16
SWA_KV_HEADS = 4
SWA_REP = 4
SWA_D = 64
SWA_W = 128

C_GATE = 0
C_SZ, C_SX, C_SB, C_SC = 3072, 4096, 5120, 5632
C_WQ, C_WK, C_WV = 6144, 7168, 7424
C_BA = 7680
C_SDT = 7808
C_MID_END = 8192
C_GQ, C_GK, C_GV, C_GG = 8192, 9216, 10240, 11264
IN_WP = 12288
IN_W = 11808

ADAM_LR, ADAM_B1, ADAM_B2, ADAM_EPS, ADAM_WD, ADAM_STEP = 0.001, 0.9, 0.999, 1e-08, 0.01, 10

VMEM_LIMIT = 56 * 1024 * 1024
BLOCK_BYTES = 3 << 19
MM_OPERAND_BYTES = 9 << 20
MM_RESIDENT_BYTES = 13 << 20

NN = (((1,), (0,)), ((), ()))
NT = (((1,), (1,)), ((), ()))
TN = (((0,), (0,)), ((), ()))


def _dot(a, b, dims=NN):
    return lax.dot_general(a.astype(bf16), b.astype(bf16), dims, preferred_element_type=f32)


def _dotx(a, b, dims=NN):
    return lax.dot_general(a, b, dims, preferred_element_type=f32, precision=lax.Precision.HIGH)


def _iota(shape, axis):
    return lax.broadcasted_iota(jnp.int32, shape, axis)


def _softplus(x):
    return jnp.maximum(x, 0.0) + jnp.log1p(jnp.exp(-jnp.abs(x)))


def _silu(x):
    return x * jax.nn.sigmoid(x)


def _params(sem):
    return pltpu.CompilerParams(dimension_semantics=sem, vmem_limit_bytes=VMEM_LIMIT)


@functools.partial(jax.custom_vjp, nondiff_argnums=(1,))
def _window(x_ext, off):
    n = x_ext.shape[0] - 8
    if off == 8:
        return x_ext[8:]
    return pltpu.roll(x_ext, 8 - off, 0)[8:]


def _window_fwd(x_ext, off):
    return _window(x_ext, off), None


def _window_bwd(off, _, g):
    n, w = g.shape
    g_ext = jnp.concatenate([jnp.zeros((8, w), g.dtype), g], axis=0)
    if off == 8:
        return (g_ext,)
    return (pltpu.roll(g_ext, n + off, 0),)


_window.defvjp(_window_fwd, _window_bwd)


def _conv4(x, halo, taps):
    x_ext = jnp.concatenate([halo, x], axis=0)
    y = taps[3] * x
    for j in range(3):
        y = y + taps[j] * _window(x_ext, 5 + j)
    return y


def _blockinv_impl(a):
    n = a.shape[0]
    ri, ci = _iota((n, n), 0), _iota((n, n), 1)
    t = (ri == ci).astype(f32)
    k = 0
    while (1 << k) < n:
        sel = ((ri >> (k + 1)) == (ci >> (k + 1))) & (((ri >> k) & 1) == 1) & (((ci >> k) & 1) == 0)
        o = jnp.where(sel, a, 0.0)
        t = t - _dotx(_dotx(t, o), t)
        k += 1
    return t


@jax.custom_vjp
def _blockinv(a):
    return _blockinv_impl(a)


def _blockinv_fwd(a):
    t = _blockinv_impl(a)
    return t, t


def _blockinv_bwd(t, dt):
    return (-_dotx(_dotx(t, dt, TN), t, NT),)


_blockinv.defvjp(_blockinv_fwd, _blockinv_bwd)


def _gdn_act(xq, xk, xv, hq, hk, hv, tq, tk, tv):
    return _silu(_conv4(xq, hq, tq)), _silu(_conv4(xk, hk, tk)), _silu(_conv4(xv, hv, tv))


def _gdn_core(q, k, v, gate, mb, mg, s, ba, alog, dtb, nw, *, row0):
    c = GDN_CHUNK
    q = q * lax.rsqrt(jnp.sum(q * q, axis=1, keepdims=True) + L2_EPS) * (GDN_D ** -0.5)
    k = k * lax.rsqrt(jnp.sum(k * k, axis=1, keepdims=True) + L2_EPS)

    valid = (row0 + _iota((c, 1), 0)) >= PAD
    pick = lambda x, m: jnp.sum(x * m, axis=1, keepdims=True)
    beta = jnp.where(valid, jax.nn.sigmoid(pick(ba, mb)), 0.0)
    g1 = jnp.where(valid, -jnp.exp(pick(alog, mg)) * _softplus(pick(ba, mg) + pick(dtb, mg)), 0.0)
    g = jnp.broadcast_to(g1, (c, GDN_D))
    g64 = jnp.broadcast_to(g1, (c, c))

    ri, ci = _iota((c, c), 0), _iota((c, c), 1)
    incl = ci <= ri
    gam = _dotx(incl.astype(f32), g)
    gam_i = _dotx(incl.astype(f32), g64)
    gam_j = _dotx(jnp.ones((c, c), f32), jnp.where(ri <= ci, g64, 0.0))
    decay = jnp.where(incl, jnp.exp(jnp.where(incl, gam_i - gam_j, 0.0)), 0.0)

    kb = k * beta
    a = jnp.where(ci < ri, _dot(kb, k, NT) * decay, 0.0)
    t = _blockinv(a)
    egam = jnp.exp(gam)
    u = _dotx(t, v * beta)
    w = _dotx(t, kb * egam)
    attn = _dot(q, k, NT) * decay
    gl = jnp.sum(g, axis=0, keepdims=True)
    kt = k * jnp.exp(gl - gam)
    v_new = u - _dot(w, s)
    o = _dot(q * egam, s) + _dot(attn, v_new)
    s_out = s * jnp.exp(gl) + _dot(kt, v_new, TN)

    y = o * lax.rsqrt(jnp.mean(o * o, axis=1, keepdims=True) + RMS_EPS) * nw * _silu(gate)
    return y, s_out


def _gdn_specs(hb, nc, rev):
    w = hb * GDN_D
    cw = D_MODEL // w

    def cidx(c):
        return (nc - 1 - c) if rev else c

    def col(base):
        return pl.BlockSpec((GDN_CHUNK, w), lambda h, c: (cidx(c), base // w + h))

    def halo(base):
        return pl.BlockSpec((8, w), lambda h, c: (jnp.maximum(cidx(c) * (GDN_CHUNK // 8) - 1, 0), base // w + h))

    def taps(base):
        return pl.BlockSpec((8, w), lambda h, c: (0, base // w + h))

    ba = pl.BlockSpec((GDN_CHUNK, 128), lambda h, c: (cidx(c), C_BA // 128))
    row = pl.BlockSpec((8, 128), lambda h, c: (0, 0))
    y = pl.BlockSpec((GDN_CHUNK, w), lambda h, c: (cidx(c), h))
    st = pl.BlockSpec((1, hb, GDN_D, GDN_D), lambda h, c: (cidx(c), h, 0, 0))
    in_specs = [col(C_GQ), col(C_GK), col(C_GV), halo(C_GQ), halo(C_GK), halo(C_GV), col(C_GG), ba,
                taps(0), taps(1024), taps(2048), row, row, row]
    return in_specs, y, st, taps, row, col, ba


def _gdn_load(refs, first):
    xq, xk, xv, hq, hk, hv, gate, ba, tq, tk, tv, alog, dtb, nw = refs

    def halo(r):
        return jnp.where(first, 0.0, r[...])

    def taps(r):
        return tuple(r[j:j + 1, :] for j in range(4))

    act = (xq[...], xk[...], xv[...], halo(hq), halo(hk), halo(hv), taps(tq), taps(tk), taps(tv))
    return act, gate[...], (ba[...], alog[0:1, :], dtb[0:1, :], nw[0:1, :])


def _heads(a, hb):
    return jnp.stack([a[:, i * GDN_D:(i + 1) * GDN_D] for i in range(hb)])


def _wide(a):
    return jnp.concatenate([a[i] for i in range(a.shape[0])], axis=1)


def _head_masks(hblk, hb):
    head = hblk * hb + _iota((hb, 1, 128), 0)
    lane = _iota((hb, 1, 128), 2)
    return (lane == head).astype(f32), (lane == head + 8).astype(f32)


def _gdn_core_heads(row0):
    return jax.vmap(functools.partial(_gdn_core, row0=row0), in_axes=(0, 0, 0, 0, 0, 0, 0, None, None, None, None))


def gdn_fwd(u, conv_w8, alog8, dtb8, nw8, *, hb=8):
    t_rows = u.shape[0]
    nc = t_rows // GDN_CHUNK
    in_specs, y_spec, st_spec, *_ = _gdn_specs(hb, nc, False)

    def body(*refs):
        ins, (y_ref, st_ref), (s_scr,) = refs[:14], refs[14:16], refs[16:]
        hblk, c = pl.program_id(0), pl.program_id(1)

        @pl.when(c == 0)
        def _():
            s_scr[...] = jnp.zeros_like(s_scr)

        act, gate, shared = _gdn_load(ins, c == 0)
        s = s_scr[...]
        st_ref[0] = s
        qa, ka, va = _gdn_act(*act)
        mb, mg = _head_masks(hblk, hb)
        y, s_new = _gdn_core_heads(c * GDN_CHUNK)(_heads(qa, hb), _heads(ka, hb), _heads(va, hb), _heads(gate, hb),
                                                  mb, mg, s, *shared)
        y_ref[...] = _wide(y).astype(bf16)
        s_scr[...] = s_new

    return pl.pallas_call(
        body, name="gdn_fwd", grid=(GDN_HEADS // hb, nc),
        in_specs=in_specs, out_specs=(y_spec, st_spec),
        out_shape=(jax.ShapeDtypeStruct((t_rows, D_MODEL), bf16),
                   jax.ShapeDtypeStruct((nc, GDN_HEADS, GDN_D, GDN_D), f32)),
        scratch_shapes=[pltpu.VMEM((hb, GDN_D, GDN_D), f32)],
        compiler_params=_params(("arbitrary", "arbitrary")),
    )(u, u, u, u, u, u, u, u, conv_w8, conv_w8, conv_w8, alog8, dtb8, nw8)


def gdn_bwd(u, conv_w8, alog8, dtb8, nw8, states, dy, du):
    t_rows = u.shape[0]
    nc = t_rows // GDN_CHUNK
    hb = GDN_HEADS
    w = hb * GDN_D
    in_specs, y_spec, st_spec, taps, row, col, ba = _gdn_specs(hb, nc, True)
    nhb = GDN_HEADS // hb

    def body(*refs):
        ins, st_ref, dy_ref = refs[:14], refs[14], refs[15]
        du_ref, dba_ref, dtq_ref, dtk_ref, dtv_ref, dalog_ref, ddtb_ref, dnw_ref = refs[17:25]
        ds_scr, dh_scr = refs[25:]
        hblk, cc = pl.program_id(0), pl.program_id(1)
        c = nc - 1 - cc

        @pl.when(cc == 0)
        def _():
            ds_scr[...] = jnp.zeros_like(ds_scr)
            dh_scr[...] = jnp.zeros_like(dh_scr)
            dtq_ref[...] = jnp.zeros_like(dtq_ref)
            dtk_ref[...] = jnp.zeros_like(dtk_ref)
            dtv_ref[...] = jnp.zeros_like(dtv_ref)

        @pl.when((cc == 0) & (hblk == 0))
        def _():
            dalog_ref[...] = jnp.zeros_like(dalog_ref)
            ddtb_ref[...] = jnp.zeros_like(ddtb_ref)
            dnw_ref[...] = jnp.zeros_like(dnw_ref)

        act, gate, shared = _gdn_load(ins, c == 0)
        (qa, ka, va), vjp_act = jax.vjp(_gdn_act, *act)
        mb, mg = _head_masks(hblk, hb)
        _, vjp_core = jax.vjp(_gdn_core_heads(c * GDN_CHUNK), _heads(qa, hb), _heads(ka, hb), _heads(va, hb),
                              _heads(gate, hb), mb, mg, st_ref[0], *shared)
        dqa, dka, dva, dgate, _, _, ds, dba, dalog, ddtb, dnw = vjp_core(
            (_heads(dy_ref[...].astype(f32), hb), ds_scr[...]))
        ds_scr[...] = ds
        dxq, dxk, dxv, dhq, dhk, dhv, dtq, dtk, dtv = vjp_act((_wide(dqa), _wide(dka), _wide(dva)))
        zeros = jnp.zeros((GDN_CHUNK - 8, w), f32)
        for j, (dx, dh) in enumerate(((dxq, dhq), (dxk, dhk), (dxv, dhv))):
            du_ref[:, j * w:(j + 1) * w] = (dx + jnp.concatenate([zeros, dh_scr[j]], axis=0)).astype(bf16)
            dh_scr[j] = dh
        du_ref[:, 3 * w:4 * w] = _wide(dgate).astype(bf16)
        dba_ref[0] = dba
        for dt_ref, dtaps in ((dtq_ref, dtq), (dtk_ref, dtk), (dtv_ref, dtv)):
            for j in range(4):
                dt_ref[j:j + 1, :] += dtaps[j]
        dalog_ref[0:1, :] += dalog
        ddtb_ref[0:1, :] += ddtb
        dnw_ref[0:1, :] += dnw

    out_specs = (pl.BlockSpec((GDN_CHUNK, 4 * w), lambda h, c: (nc - 1 - c, C_GQ // (4 * w))),
                 pl.BlockSpec((1, GDN_CHUNK, 128), lambda h, c: (h, nc - 1 - c, 0)),
                 taps(0), taps(0), taps(0), row, row, row)
    out_shape = (jax.ShapeDtypeStruct(du.shape, du.dtype),
                 jax.ShapeDtypeStruct((nhb, t_rows, 128), f32),
                 jax.ShapeDtypeStruct((8, D_MODEL), f32), jax.ShapeDtypeStruct((8, D_MODEL), f32),
                 jax.ShapeDtypeStruct((8, D_MODEL), f32),
                 jax.ShapeDtypeStruct((8, 128), f32), jax.ShapeDtypeStruct((8, 128), f32), jax.ShapeDtypeStruct((8, 128), f32))
    return pl.pallas_call(
        body, name="gdn_bwd", grid=(nhb, nc),
        in_specs=in_specs + [st_spec, y_spec, ANY], out_specs=out_specs, out_shape=out_shape,
        input_output_aliases={16: 0},
        scratch_shapes=[pltpu.VMEM((hb, GDN_D, GDN_D), f32), pltpu.VMEM((3, 8, w), f32)],
        compiler_params=_params(("arbitrary", "arbitrary")),
    )(u, u, u, u, u, u, u, u, conv_w8, conv_w8, conv_w8, alog8, dtb8, nw8, states, dy, du)


def _ssd_act(xs_r, b_r, c_r, hx, hbm, hcm, tx, tb, tc, bx, bb, bc, *, row0):
    valid = (row0 + _iota((SSD_CHUNK, 1), 0)) >= PAD
    act = lambda x, h, t, b: jnp.where(valid, _silu(_conv4(x, h, t) + b), 0.0)
    return act(xs_r, hx, tx, bx), act(b_r, hbm, tb, bb), act(c_r, hcm, tc, bc)


def _ssd_core(xs, bm, cm, z, nw, m0, m1, m2, m3, h, dt, dtb, alog, dsk, *, row0):
    n = SSD_CHUNK
    valid = (row0 + _iota((n, 1), 0)) >= PAD
    dtp16 = _softplus(dt + dtb)
    a16 = -jnp.exp(alog)
    pick = lambda x, m: jnp.sum(x * m, axis=1, keepdims=True)
    lane_r = _iota((1, 256), 1) >> 6
    dtp = jnp.zeros((n, 256), f32)
    adt = jnp.zeros((n, 256), f32)
    dlane = jnp.zeros((1, 256), f32)
    acols = []
    for r, m in enumerate((m0, m1, m2, m3)):
        dcol = jnp.where(valid, pick(dtp16, m), 0.0)
        acol = dcol * pick(a16, m)
        dtp = jnp.where(lane_r == r, dcol, dtp)
        adt = jnp.where(lane_r == r, acol, adt)
        dlane = jnp.where(lane_r == r, pick(dsk, m), dlane)
        acols.append(acol)

    ri, ci = _iota((n, n), 0), _iota((n, n), 1)
    incl = ci <= ri
    inclf = incl.astype(f32)
    acum = _dotx(inclf, adt)
    al = jnp.sum(adt, axis=0, keepdims=True)
    xdt = xs * dtp
    cb = _dot(cm, bm, NT)
    y = _dot(cm, h) * jnp.exp(acum) + dlane * xs
    for r in range(SSD_HPG):
        ab = jnp.broadcast_to(acols[r], (n, n))
        ai = _dotx(inclf, ab)
        aj = _dotx(jnp.ones((n, n), f32), jnp.where(ri <= ci, ab, 0.0))
        lm = jnp.where(incl, jnp.exp(jnp.where(incl, ai - aj, 0.0)), 0.0)
        y = y + _dot(cb * lm, jnp.where(lane_r == r, xdt, 0.0))
    h_out = h * jnp.exp(al) + _dot(bm, jnp.exp(al - acum) * xdt, TN)
    y = y * _silu(z)
    y = y * lax.rsqrt(jnp.mean(y * y, axis=1, keepdims=True) + RMS_EPS) * nw
    return y, h_out


def _ssd_core_groups(row0):
    return jax.vmap(functools.partial(_ssd_core, row0=row0), in_axes=(0,) * 10 + (None,) * 4)


def _ssd_specs(nc, rev):
    n = SSD_CHUNK

    def cidx(c):
        return (nc - 1 - c) if rev else c

    def col(base, w):
        return pl.BlockSpec((n, w), lambda c: (cidx(c), base // w))

    def halo(base, w):
        return pl.BlockSpec((8, w), lambda c: (jnp.maximum(cidx(c) * (n // 8) - 1, 0), base // w))

    def taps(base, w):
        return pl.BlockSpec((8, w), lambda c: (0, base // w))

    row = pl.BlockSpec((8, 128), lambda c: (0, 0))
    in_specs = [col(C_SX, 1024), col(C_SB, 512), col(C_SC, 512), halo(C_SX, 1024), halo(C_SB, 512), halo(C_SC, 512),
                col(C_SZ, 1024), col(C_SDT, 128), taps(0, 1024), taps(1024, 512), taps(1536, 512), row, row, row,
                taps(0, 1024)]
    y = pl.BlockSpec((n, D_MODEL), lambda c: (cidx(c), 0))
    st = pl.BlockSpec((1, SSD_GROUPS, SSD_N, 256), lambda c: (cidx(c), 0, 0, 0))
    return in_specs, y, st, col, taps, row


def _ssd_load(refs, first):
    xs, bm, cm, hx, hbm, hcm, z, dt, tx, tb, tc, dtb, alog, dsk, nw = refs

    def halo(r):
        return jnp.where(first, 0.0, r[...])

    def taps(r):
        return tuple(r[j:j + 1, :] for j in range(4))

    act = (xs[...], bm[...], cm[...], halo(hx), halo(hbm), halo(hcm), taps(tx), taps(tb), taps(tc),
           tx[4:5, :], tb[4:5, :], tc[4:5, :])
    return act, (z[...], nw[0:1, :]), (dt[...], dtb[0:1, :], alog[0:1, :], dsk[0:1, :])


def _groups(a, w):
    return jnp.stack([a[:, i * w:(i + 1) * w] for i in range(SSD_GROUPS)])


def _ssd_masks():
    head = _iota((SSD_GROUPS, 1, 128), 0) * SSD_HPG
    lane = _iota((SSD_GROUPS, 1, 128), 2)
    return tuple((lane == head + r).astype(f32) for r in range(SSD_HPG))


def ssd_fwd(u, conv_w8, dtb8, alog8, d8, nw8):
    t_rows = u.shape[0]
    nc = t_rows // SSD_CHUNK
    in_specs, y_spec, st_spec, *_ = _ssd_specs(nc, False)

    def body(*refs):
        ins, (y_ref, st_ref), (h_scr,) = refs[:15], refs[15:17], refs[17:]
        c = pl.program_id(0)

        @pl.when(c == 0)
        def _():
            h_scr[...] = jnp.zeros_like(h_scr)

        act, (z, nw), shared = _ssd_load(ins, c == 0)
        h = h_scr[...]
        st_ref[0] = h
        xs, bm, cm = _ssd_act(*act, row0=c * SSD_CHUNK)
        y, h_new = _ssd_core_groups(c * SSD_CHUNK)(_groups(xs, 256), _groups(bm, 128), _groups(cm, 128),
                                                   _groups(z, 256), _groups(nw, 256), *_ssd_masks(), h, *shared)
        y_ref[...] = _wide(y).astype(bf16)
        h_scr[...] = h_new

    return pl.pallas_call(
        body, name="ssd_fwd", grid=(nc,), in_specs=in_specs, out_specs=(y_spec, st_spec),
        out_shape=(jax.ShapeDtypeStruct((t_rows, D_MODEL), bf16),
                   jax.ShapeDtypeStruct((nc, SSD_GROUPS, SSD_N, 256), f32)),
        scratch_shapes=[pltpu.VMEM((SSD_GROUPS, SSD_N, 256), f32)],
        compiler_params=_params(("arbitrary",)),
    )(u, u, u, u, u, u, u, u, conv_w8, conv_w8, conv_w8, dtb8, alog8, d8, nw8)


def ssd_bwd(u, conv_w8, dtb8, alog8, d8, nw8, states, dy, du):
    t_rows = u.shape[0]
    nc = t_rows // SSD_CHUNK
    n = SSD_CHUNK
    in_specs, y_spec, st_spec, col, taps, row = _ssd_specs(nc, True)

    def body(*refs):
        ins, st_ref, dy_ref = refs[:15], refs[15], refs[16]
        du_ref, ddt_ref, dtx_ref, dtb_ref, dtc_ref, ddtb_ref, dalog_ref, ddsk_ref, dnw_ref = refs[18:27]
        dh_scr, hx_scr, hb_scr, hc_scr = refs[27:]
        cc = pl.program_id(0)
        c = nc - 1 - cc

        @pl.when(cc == 0)
        def _():
            for r in (dh_scr, hx_scr, hb_scr, hc_scr, dtx_ref, dtb_ref, dtc_ref, dnw_ref, ddtb_ref, dalog_ref, ddsk_ref):
                r[...] = jnp.zeros_like(r)

        act, (z, nw), shared = _ssd_load(ins, c == 0)
        (xs, bm, cm), vjp_act = jax.vjp(functools.partial(_ssd_act, row0=c * n), *act)
        _, vjp_core = jax.vjp(_ssd_core_groups(c * n), _groups(xs, 256), _groups(bm, 128), _groups(cm, 128),
                              _groups(z, 256), _groups(nw, 256), *_ssd_masks(), st_ref[0], *shared)
        dxa, dba, dca, dz, dnw, _, _, _, _, dh, ddt, ddtb, dalog, ddsk = vjp_core(
            (_groups(dy_ref[...].astype(f32), 256), dh_scr[...]))
        dh_scr[...] = dh
        dxs, dbm, dcm, dhx, dhb, dhc, dtx, dtb, dtc, dbx, dbb, dbc = vjp_act((_wide(dxa), _wide(dba), _wide(dca)))
        du_ref[:, 0:D_MODEL] = _wide(dz).astype(bf16)
        for dx, dhalo, scr, lo in ((dxs, dhx, hx_scr, C_SX), (dbm, dhb, hb_scr, C_SB), (dcm, dhc, hc_scr, C_SC)):
            zeros = jnp.zeros((n - 8, dx.shape[1]), f32)
            du_ref[:, lo - C_SZ:lo - C_SZ + dx.shape[1]] = (dx + jnp.concatenate([zeros, scr[...]], axis=0)).astype(bf16)
            scr[...] = dhalo
        ddt_ref[...] = ddt
        for ref, dtaps, dbias in ((dtx_ref, dtx, dbx), (dtb_ref, dtb, dbb), (dtc_ref, dtc, dbc)):
            for j in range(4):
                ref[j:j + 1, :] += dtaps[j]
            ref[4:5, :] += dbias
        ddtb_ref[0:1, :] += ddtb
        dalog_ref[0:1, :] += dalog
        ddsk_ref[0:1, :] += ddsk
        dnw_ref[0:1, :] += _wide(dnw)

    def out_col(w):
        return pl.BlockSpec((n, w), lambda c: (nc - 1 - c, 0))

    out_specs = (pl.BlockSpec((n, 3 * D_MODEL), lambda c: (nc - 1 - c, C_SZ // (3 * D_MODEL))), out_col(128),
                 taps(0, D_MODEL), taps(0, 512), taps(0, 512), row, row, row, taps(0, D_MODEL))
    out_shape = (jax.ShapeDtypeStruct(du.shape, du.dtype),
                 jax.ShapeDtypeStruct((t_rows, 128), f32),
                 jax.ShapeDtypeStruct((8, D_MODEL), f32), jax.ShapeDtypeStruct((8, 512), f32),
                 jax.ShapeDtypeStruct((8, 512), f32),
                 jax.ShapeDtypeStruct((8, 128), f32), jax.ShapeDtypeStruct((8, 128), f32),
                 jax.ShapeDtypeStruct((8, 128), f32), jax.ShapeDtypeStruct((8, D_MODEL), f32))
    return pl.pallas_call(
        body, name="ssd_bwd", grid=(nc,), in_specs=in_specs + [st_spec, y_spec, ANY],
        out_specs=out_specs, out_shape=out_shape, input_output_aliases={17: 0},
        scratch_shapes=[pltpu.VMEM((SSD_GROUPS, SSD_N, 256), f32), pltpu.VMEM((8, D_MODEL), f32),
                        pltpu.VMEM((8, 512), f32), pltpu.VMEM((8, 512), f32)],
        compiler_params=_params(("arbitrary",)),
    )(u, u, u, u, u, u, u, u, conv_w8, conv_w8, conv_w8, dtb8, alog8, d8, nw8, states, dy, du)


NEG = -1e30


def _swa_core(q, kc, kp, km, vc, vp, vm, sink, *, n):
    rows = SWA_REP * SWA_W
    ri, ci = _iota((rows, SWA_W), 0) & (SWA_W - 1), _iota((rows, SWA_W), 1)
    causal = ci <= ri
    m_cur = causal & ((n >= 1) | ((ci >= PAD) & (ri >= PAD)))
    m_prev = (n >= 2) & (ci > ri)
    m_meta = (n >= 1) & (ci >= PAD)
    q = q * (SWA_D ** -0.5)
    sc = jnp.where(m_cur, _dot(q, kc, NT), NEG)
    sp = jnp.where(m_prev, _dot(q, kp, NT), NEG)
    sm = jnp.where(m_meta, _dot(q, km, NT), NEG)
    mx = jnp.maximum(jnp.maximum(jnp.max(sc, axis=1, keepdims=True), jnp.max(sp, axis=1, keepdims=True)),
                     jnp.maximum(jnp.max(sm, axis=1, keepdims=True), sink))
    mx = lax.stop_gradient(mx)
    ec, ep, em = jnp.exp(sc - mx), jnp.exp(sp - mx), jnp.exp(sm - mx)
    den = (jnp.sum(ec, axis=1, keepdims=True) + jnp.sum(ep, axis=1, keepdims=True)
           + jnp.sum(em, axis=1, keepdims=True) + jnp.exp(sink - mx))
    return (_dot(ec, vc) + _dot(ep, vp) + _dot(em, vm)) / den


def _swa_block(q16, kc, kp, km, vc, vp, vm, sink16, *, n):
    rows = SWA_REP * SWA_W
    lane = _iota((1, 128), 1)
    rep = _iota((rows, 1), 0) >> 7
    cols = []
    for h in range(SWA_KV_HEADS):
        col = jnp.zeros((rows, 1), f32)
        for r in range(SWA_REP):
            s = jnp.sum(jnp.where(lane == h * SWA_REP + r, sink16, 0.0), axis=1, keepdims=True)
            col = jnp.where(rep == r, s, col)
        cols.append(col)
    o = jax.vmap(functools.partial(_swa_core, n=n))(q16.reshape(SWA_KV_HEADS, rows, SWA_D), kc, kp, km, vc, vp, vm,
                                                    jnp.concatenate([col[None] for col in cols], axis=0))
    return o.reshape(q16.shape)


def _swa_specs(nb, rev):
    def bidx(n):
        return (nb - 1 - n) if rev else n

    q = pl.BlockSpec((SWA_Q_HEADS, SWA_W, SWA_D), lambda n: (0, bidx(n), 0))
    cur = pl.BlockSpec((SWA_KV_HEADS, SWA_W, SWA_D), lambda n: (0, bidx(n), 0))
    prev = pl.BlockSpec((SWA_KV_HEADS, SWA_W, SWA_D), lambda n: (0, jnp.maximum(bidx(n) - 1, 0), 0))
    meta = pl.BlockSpec((SWA_KV_HEADS, SWA_W, SWA_D), lambda n: (0, 0, 0))
    row = pl.BlockSpec((8, 128), lambda n: (0, 0))
    return [q, cur, prev, meta, cur, prev, meta, row], q, cur, row


def swa_fwd(q, k, v, sink8):
    t_rows = q.shape[1]
    nb = t_rows // SWA_W
    in_specs, q_spec, _, _ = _swa_specs(nb, False)

    def body(q_ref, kc, kp, km, vc, vp, vm, sink_ref, o_ref):
        o_ref[...] = _swa_block(q_ref[...], kc[...], kp[...], km[...], vc[...], vp[...], vm[...], sink_ref[0:1, :],
                                n=pl.program_id(0)).astype(bf16)

    return pl.pallas_call(
        body, name="swa_fwd", grid=(nb,), in_specs=in_specs, out_specs=q_spec,
        out_shape=jax.ShapeDtypeStruct(q.shape, bf16),
        compiler_params=_params(("arbitrary",)),
    )(q, k, k, k, v, v, v, sink8)


def swa_bwd(q, k, v, sink8, do):
    t_rows = q.shape[1]
    nb = t_rows // SWA_W
    in_specs, q_spec, kv_spec, row = _swa_specs(nb, True)

    def body(q_ref, kc, kp, km, vc, vp, vm, sink_ref, do_ref, dq_ref, dk_ref, dv_ref, dsink_ref,
             dkp_scr, dvp_scr, dkm_scr, dvm_scr):
        nn = pl.program_id(0)
        n = nb - 1 - nn

        @pl.when(nn == 0)
        def _():
            for r in (dkp_scr, dvp_scr, dkm_scr, dvm_scr, dsink_ref):
                r[...] = jnp.zeros_like(r)

        fn = functools.partial(_swa_block, n=n)
        _, vjp = jax.vjp(fn, q_ref[...], kc[...], kp[...], km[...], vc[...], vp[...], vm[...], sink_ref[0:1, :])
        dq, dkc, dkp, dkm, dvc, dvp, dvm, dsink = vjp(do_ref[...].astype(f32))
        dq_ref[...] = dq.astype(bf16)
        dkm_scr[...] += dkm
        dvm_scr[...] += dvm
        first = n == 0
        dk_ref[...] = (dkc + dkp_scr[...] + jnp.where(first, dkm_scr[...], 0.0)).astype(bf16)
        dv_ref[...] = (dvc + dvp_scr[...] + jnp.where(first, dvm_scr[...], 0.0)).astype(bf16)
        dkp_scr[...] = dkp
        dvp_scr[...] = dvp
        dsink_ref[0:1, :] += dsink

    kv_shape = jax.ShapeDtypeStruct(k.shape, bf16)
    return pl.pallas_call(
        body, name="swa_bwd", grid=(nb,), in_specs=in_specs + [q_spec],
        out_specs=(q_spec, kv_spec, kv_spec, row),
        out_shape=(jax.ShapeDtypeStruct(q.shape, bf16), kv_shape, kv_shape, jax.ShapeDtypeStruct((8, 128), f32)),
        scratch_shapes=[pltpu.VMEM((SWA_KV_HEADS, SWA_W, SWA_D), f32)] * 4,
        compiler_params=_params(("arbitrary",)),
    )(q, k, k, k, v, v, v, sink8, do)


def _tile(dim, prefs):
    for p in prefs:
        if dim % p == 0:
            return p
    return dim


def mm(a, b, *, out_dtype, name, resid=None, relu_grad_of=None, relu2_out=False):
    assert resid is None or relu_grad_of is None
    m, k = a.shape
    n = b.shape[1]
    rhs_stays = k * 2 * 1024 > MM_OPERAND_BYTES
    if rhs_stays:
        tn = _tile(n, tuple(p for p in (512, 256, 128) if p * k * 2 <= MM_RESIDENT_BYTES))
        tm = _tile(m, tuple(p for p in (512, 384, 256, 128) if p * k * 2 <= MM_OPERAND_BYTES // 2))
        grid = (n // tn, m // tm)
        ij = lambda o, i: (i, o)
    else:
        tm = _tile(m, tuple(p for p in (1408, 1024, 512, 384, 256, 128) if p * k * 2 <= MM_OPERAND_BYTES))
        tn = _tile(n, tuple(p for p in (512, 256, 128) if p * k * 2 <= MM_OPERAND_BYTES // 2))
        grid = (m // tm, n // tn)
        ij = lambda o, i: (o, i)

    extra = resid if resid is not None else relu_grad_of

    def body(*refs):
        a_ref, b_ref = refs[:2]
        o = _dot(a_ref[...], b_ref[...])
        if resid is not None:
            o = o + refs[2][...]
        if relu_grad_of is not None:
            o = o * (2.0 * jnp.maximum(refs[2][...], 0.0))
        if relu2_out:
            refs[-2][...] = o.astype(out_dtype)
            r = jnp.maximum(o, 0.0)
            refs[-1][...] = (r * r).astype(bf16)
        else:
            refs[-1][...] = o.astype(out_dtype)

    in_specs = [pl.BlockSpec((tm, k), lambda o, i: (ij(o, i)[0], 0)), pl.BlockSpec((k, tn), lambda o, i: (0, ij(o, i)[1]))]
    args = [a, b]
    if extra is not None:
        in_specs.append(pl.BlockSpec((tm, tn), ij))
        args.append(extra)
    out_blk = pl.BlockSpec((tm, tn), ij)
    out = jax.ShapeDtypeStruct((m, n), out_dtype)
    return pl.pallas_call(
        body, name=name, grid=grid, in_specs=in_specs,
        out_specs=(out_blk, out_blk) if relu2_out else out_blk,
        out_shape=(out, jax.ShapeDtypeStruct((m, n), bf16)) if relu2_out else out,
        compiler_params=_params(("parallel", "parallel")),
    )(*args)


def _rows(t_rows):
    return _tile(t_rows, (384, 256, 128))


def _rmsnorm(h, w):
    return h * lax.rsqrt(jnp.mean(h * h, axis=1, keepdims=True) + RMS_EPS) * w


def rmsnorm_fwd(h, w8, *, name):
    t_rows, d = h.shape
    tr = _rows(t_rows)

    def body(h_ref, w_ref, o_ref):
        o_ref[...] = _rmsnorm(h_ref[...], w_ref[0:1, :]).astype(bf16)

    blk = pl.BlockSpec((tr, d), lambda i: (i, 0))
    return pl.pallas_call(
        body, name=name, grid=(t_rows // tr,), in_specs=[blk, pl.BlockSpec((8, d), lambda i: (0, 0))], out_specs=blk,
        out_shape=jax.ShapeDtypeStruct((t_rows, d), bf16), compiler_params=_params(("arbitrary",)),
    )(h, w8)


def rmsnorm_bwd(h, w8, dhn, dres, *, name):
    t_rows, d = h.shape
    tr = _rows(t_rows)

    def body(h_ref, w_ref, dhn_ref, dres_ref, dh_ref, dw_ref):
        @pl.when(pl.program_id(0) == 0)
        def _():
            dw_ref[...] = jnp.zeros_like(dw_ref)

        _, vjp = jax.vjp(_rmsnorm, h_ref[...], w_ref[0:1, :])
        dh, dw = vjp(dhn_ref[...])
        dh_ref[...] = dh + dres_ref[...]
        dw_ref[0:1, :] += dw

    blk = pl.BlockSpec((tr, d), lambda i: (i, 0))
    wblk = pl.BlockSpec((8, d), lambda i: (0, 0))
    return pl.pallas_call(
        body, name=name, grid=(t_rows // tr,), in_specs=[blk, wblk, blk, blk], out_specs=(blk, wblk),
        out_shape=(jax.ShapeDtypeStruct((t_rows, d), f32), jax.ShapeDtypeStruct((8, d), f32)),
        compiler_params=_params(("arbitrary",)),
    )(h, w8, dhn, dres)


def _merge(pg, ps, pw, la, lb, lc):
    return jax.nn.sigmoid(la) * pg + jax.nn.sigmoid(lb) * ps + jax.nn.sigmoid(lc) * pw


def _merge_specs(t_rows):
    tr = _rows(t_rows)
    blk = pl.BlockSpec((tr, D_MODEL), lambda i: (i, 0))
    gate = [pl.BlockSpec((tr, D_MODEL), functools.partial(lambda i, j: (i, j), j=C_GATE // D_MODEL + j)) for j in range(3)]
    return tr, blk, gate


def merge_fwd(pg, ps, pw, u):
    t_rows = pg.shape[0]
    tr, blk, gate = _merge_specs(t_rows)

    def body(pg_ref, ps_ref, pw_ref, la, lb, lc, o_ref):
        o_ref[...] = _merge(pg_ref[...], ps_ref[...], pw_ref[...], la[...], lb[...], lc[...]).astype(bf16)

    return pl.pallas_call(
        body, name="merge_fwd", grid=(t_rows // tr,), in_specs=[blk, blk, blk] + gate, out_specs=blk,
        out_shape=jax.ShapeDtypeStruct((t_rows, D_MODEL), bf16), compiler_params=_params(("arbitrary",)),
    )(pg, ps, pw, u, u, u)


def merge_bwd(pg, ps, pw, u, dmerged, du):
    t_rows = pg.shape[0]
    tr, blk, gate = _merge_specs(t_rows)

    def body(pg_ref, ps_ref, pw_ref, la, lb, lc, dm_ref, _, dpg_ref, dps_ref, dpw_ref, dl_ref):
        _, vjp = jax.vjp(_merge, pg_ref[...], ps_ref[...], pw_ref[...], la[...], lb[...], lc[...])
        dpg, dps, dpw, dla, dlb, dlc = vjp(dm_ref[...])
        dpg_ref[...] = dpg.astype(bf16)
        dps_ref[...] = dps.astype(bf16)
        dpw_ref[...] = dpw.astype(bf16)
        for j, dl in enumerate((dla, dlb, dlc)):
            dl_ref[:, j * D_MODEL:(j + 1) * D_MODEL] = dl.astype(bf16)

    act = jax.ShapeDtypeStruct((t_rows, D_MODEL), bf16)
    return pl.pallas_call(
        body, name="merge_bwd", grid=(t_rows // tr,), in_specs=[blk, blk, blk] + gate + [blk, ANY],
        out_specs=(blk, blk, blk, pl.BlockSpec((tr, 3 * D_MODEL), lambda i: (i, C_GATE // (3 * D_MODEL)))),
        out_shape=(act, act, act, jax.ShapeDtypeStruct(du.shape, du.dtype)),
        input_output_aliases={7: 3},
        compiler_params=_params(("arbitrary",)),
    )(pg, ps, pw, u, u, u, dmerged, du)


def relu2_fwd(a):
    t_rows, d = a.shape
    tr = _rows(t_rows)

    def body(a_ref, o_ref):
        r = jnp.maximum(a_ref[...], 0.0)
        o_ref[...] = (r * r).astype(bf16)

    blk = pl.BlockSpec((tr, d), lambda i: (i, 0))
    return pl.pallas_call(
        body, name="relu2_fwd", grid=(t_rows // tr,), in_specs=[blk], out_specs=blk,
        out_shape=jax.ShapeDtypeStruct((t_rows, d), bf16), compiler_params=_params(("arbitrary",)),
    )(a)


def relu2_bwd(a, dr):
    t_rows, d = a.shape
    tr = _rows(t_rows)

    def body(a_ref, dr_ref, o_ref):
        o_ref[...] = (dr_ref[...] * 2.0 * jnp.maximum(a_ref[...], 0.0)).astype(bf16)

    blk = pl.BlockSpec((tr, d), lambda i: (i, 0))
    return pl.pallas_call(
        body, name="relu2_bwd", grid=(t_rows // tr,), in_specs=[blk, blk], out_specs=blk,
        out_shape=jax.ShapeDtypeStruct((t_rows, d), bf16), compiler_params=_params(("arbitrary",)),
    )(a, dr)


def loss_head(h, w8, target):
    t_rows, d = h.shape
    tr = HEAD_ROWS

    def loss_fn(hb, w, tgt):
        err = _rmsnorm(hb, w) - tgt
        return 0.5 * jnp.sum(err * err) / d

    def body(h_ref, w_ref, t_ref, loss_ref, dh_ref, dw_ref):
        i = pl.program_id(0)

        @pl.when(i == 0)
        def _():
            loss_ref[...] = jnp.zeros_like(loss_ref)
            dw_ref[...] = jnp.zeros_like(dw_ref)
            dh_ref[...] = jnp.zeros_like(dh_ref)

        @pl.when(i > 0)
        def _():
            val, (dh, dw) = jax.value_and_grad(loss_fn, argnums=(0, 1))(h_ref[...], w_ref[0:1, :], t_ref[...])
            loss_ref[...] += val
            dh_ref[...] = dh
            dw_ref[0:1, :] += dw

    blk = pl.BlockSpec((tr, d), lambda i: (i, 0))
    wblk = pl.BlockSpec((8, d), lambda i: (0, 0))
    return pl.pallas_call(
        body, name="loss_head", grid=(t_rows // tr,),
        in_specs=[blk, wblk, pl.BlockSpec((tr, d), lambda i: (jnp.maximum(i - 1, 0), 0))],
        out_specs=(pl.BlockSpec((8, 128), lambda i: (0, 0)), blk, wblk),
        out_shape=(jax.ShapeDtypeStruct((8, 128), f32), jax.ShapeDtypeStruct((t_rows, d), f32),
                   jax.ShapeDtypeStruct((8, d), f32)),
        compiler_params=_params(("arbitrary",)),
    )(h, w8, target)


def adamw(w, m, v, partials, row_off, *, name):
    rows, d = w.shape
    layers = len(partials)
    per = rows // layers
    tr = _tile(per, tuple(p for p in (512, 256, 128, 64, 16, 8) if p * d * 4 <= BLOCK_BYTES))
    assert row_off % tr == 0
    off, nblk = row_off // tr, per // tr
    c1 = 1.0 - ADAM_B1 ** ADAM_STEP
    c2 = 1.0 - ADAM_B2 ** ADAM_STEP

    def body(w_ref, m_ref, v_ref, *refs):
        p_refs, (g_ref, d_ref, mo_ref, vo_ref) = refs[:2 * layers], refs[2 * layers:]
        g = p_refs[0][...] + p_refs[1][...]
        for l in range(1, layers):
            g = jnp.where(pl.program_id(0) >= l * nblk, p_refs[2 * l][...] + p_refs[2 * l + 1][...], g)
        m_new = ADAM_B1 * m_ref[...] + (1.0 - ADAM_B1) * g
        v_new = ADAM_B2 * v_ref[...] + (1.0 - ADAM_B2) * (g * g)
        g_ref[...] = g
        d_ref[...] = -ADAM_LR * ((m_new / c1) / (jnp.sqrt(v_new / c2) + ADAM_EPS) + ADAM_WD * w_ref[...])
        mo_ref[...] = m_new
        vo_ref[...] = v_new

    blk = pl.BlockSpec((tr, d), lambda i: (i, 0))
    pblks = [pl.BlockSpec((tr, d), functools.partial(lambda i, l: (off + jnp.clip(i - l * nblk, 0, nblk - 1), 0), l=l))
             for l in range(layers) for _ in range(2)]
    out = jax.ShapeDtypeStruct((rows, d), f32)
    return pl.pallas_call(
        body, name=name, grid=(rows // tr,), in_specs=[blk, blk, blk] + pblks, out_specs=(blk,) * 4,
        out_shape=(out,) * 4, compiler_params=_params(("arbitrary",)),
    )(w, m, v, *[p for pair in partials for p in pair])


def reduce4(parts, *, name, own=None, me=None):
    _, rows, d = parts.shape
    tr = _tile(rows, tuple(p for p in (512, 256, 128, 64, 8) if p * d * 4 <= BLOCK_BYTES))

    def body(*refs):
        p_ref, o_ref = refs[0], refs[-1]
        acc = None
        for s in range(4):
            term = p_ref[s].astype(f32)
            if own is not None:
                term = jnp.where(refs[2][0] == s, refs[1][...].astype(f32), term)
            acc = term if acc is None else acc + term
        o_ref[...] = acc

    in_specs = [pl.BlockSpec((4, tr, d), lambda i: (0, i, 0))]
    args = [parts]
    if own is not None:
        in_specs += [pl.BlockSpec((tr, d), lambda i: (i, 0)), pl.BlockSpec(memory_space=pltpu.SMEM)]
        args += [own, me]
    return pl.pallas_call(
        body, name=name, grid=(rows // tr,), in_specs=in_specs,
        out_specs=pl.BlockSpec((tr, d), lambda i: (i, 0)), out_shape=jax.ShapeDtypeStruct((rows, d), f32),
        compiler_params=_params(("arbitrary",)),
    )(*args)


ANY = pl.BlockSpec(memory_space=pl.ANY)
MESH = pl.DeviceIdType.MESH
CHIP_FLIPS = ((0, 1), (1, 0), (1, 1))


def chip_exchange(bufs, scatter, *, name):
    nb = len(bufs)

    def body(*refs):
        ins, outs = refs[:nb], refs[nb:2 * nb]
        send_sems, recv_sems, local_sems = refs[2 * nb:]
        x, y, c = lax.axis_index("x"), lax.axis_index("y"), lax.axis_index("c")
        me = 2 * x + y
        local = [pltpu.make_async_copy(ins[j].at[me] if scatter[j] else ins[j], outs[j].at[me], local_sems.at[j])
                 for j in range(nb)]
        for cp in local:
            cp.start()
        sends, recvs = [], []
        for k, (fx, fy) in enumerate(CHIP_FLIPS):
            px = 1 - x if fx else x
            py = 1 - y if fy else y
            chip = 2 * px + py
            for j in range(nb):
                src = ins[j].at[chip] if scatter[j] else ins[j]
                sems = dict(send_sem=send_sems.at[nb * k + j], recv_sem=recv_sems.at[nb * k + j],
                            device_id=(px, py, c), device_id_type=MESH)
                sends.append(pltpu.make_async_remote_copy(src_ref=src, dst_ref=outs[j].at[me], **sems))
                recvs.append(pltpu.make_async_remote_copy(src_ref=src, dst_ref=outs[j].at[chip], **sems))
        for cp in sends:
            cp.start()
        for cp in recvs:
            cp.wait_recv()
        for cp in sends:
            cp.wait_send()
        for cp in local:
            cp.wait()

    out_shape = tuple(jax.ShapeDtypeStruct(b.shape if s else (4,) + b.shape, b.dtype) for b, s in zip(bufs, scatter))
    return pl.pallas_call(
        body, name=name, in_specs=[ANY] * nb, out_specs=(ANY,) * nb, out_shape=out_shape,
        scratch_shapes=[pltpu.SemaphoreType.DMA((3 * nb,)), pltpu.SemaphoreType.DMA((3 * nb,)),
                        pltpu.SemaphoreType.DMA((nb,))],
        compiler_params=pltpu.CompilerParams(has_side_effects=True),
    )(*bufs)


def sibling_swap(bufs, *, name):
    nb = len(bufs)

    def body(*refs):
        ins, outs, (send_sems, recv_sems) = refs[:nb], refs[nb:2 * nb], refs[2 * nb:]
        peer = (lax.axis_index("x"), lax.axis_index("y"), 1 - lax.axis_index("c"))
        copies = [pltpu.make_async_remote_copy(src_ref=ins[j], dst_ref=outs[j], send_sem=send_sems.at[j],
                                               recv_sem=recv_sems.at[j], device_id=peer, device_id_type=MESH)
                  for j in range(nb)]
        for cp in copies:
            cp.start()
        for cp in copies:
            cp.wait_recv()
        for cp in copies:
            cp.wait_send()

    return pl.pallas_call(
        body, name=name, in_specs=[ANY] * nb, out_specs=(ANY,) * nb,
        out_shape=tuple(jax.ShapeDtypeStruct(b.shape, b.dtype) for b in bufs),
        scratch_shapes=[pltpu.SemaphoreType.DMA((nb,)), pltpu.SemaphoreType.DMA((nb,))],
        compiler_params=pltpu.CompilerParams(has_side_effects=True),
    )(*bufs)


HBM = pl.BlockSpec(memory_space=pltpu.HBM)
SEM = pl.BlockSpec(memory_space=pltpu.SEMAPHORE)
DATAFLOW = pltpu.SideEffectType.DATAFLOW_SIDE_EFFECTING


def _exchange_copies(srcs, lands, send_sems, recv_sems, scatter):
    x, y, c = lax.axis_index("x"), lax.axis_index("y"), lax.axis_index("c")
    me = 2 * x + y
    nb = len(srcs)
    pairs = []
    for k, (fx, fy) in enumerate(CHIP_FLIPS):
        px = 1 - x if fx else x
        py = 1 - y if fy else y
        chip = 2 * px + py
        for j in range(nb):
            src = srcs[j].at[chip] if scatter[j] else srcs[j]
            sems = dict(send_sem=send_sems.at[nb * k + j], recv_sem=recv_sems.at[nb * k + j],
                        device_id=(px, py, c), device_id_type=MESH)
            pairs.append((pltpu.make_async_remote_copy(src_ref=src, dst_ref=lands[j].at[me], **sems),
                          pltpu.make_async_remote_copy(src_ref=src, dst_ref=lands[j].at[chip], **sems)))
    return pairs


def exchange_start(bufs, scatter, after, *, name):
    nb = len(bufs)
    slabs = [b.shape[1:] if s else b.shape for b, s in zip(bufs, scatter)]
    lands = [lax.empty((4,) + shp, b.dtype) for b, shp in zip(bufs, slabs)]

    def body(*refs):
        srcs, zones = refs[:nb], refs[nb:2 * nb]
        send_sems, recv_sems = refs[2 * nb + 1:2 * nb + 3]
        token = refs[-1]
        for send, _ in _exchange_copies(srcs, zones, send_sems, recv_sems, scatter):
            send.start()
        token[...] = jnp.zeros_like(token)

    hbm = lambda a: pltpu.with_memory_space_constraint(a, pltpu.HBM)
    out = pl.pallas_call(
        body, name=name, in_specs=[HBM] * (2 * nb) + [ANY],
        out_specs=(SEM, SEM) + (HBM,) * (2 * nb) + (pl.BlockSpec(memory_space=pltpu.VMEM),),
        out_shape=(pltpu.SemaphoreType.DMA((3 * nb,)), pltpu.SemaphoreType.DMA((3 * nb,)))
        + tuple(pltpu.HBM(a.shape, a.dtype) for a in list(bufs) + lands) + (jax.ShapeDtypeStruct((8, 128), f32),),
        input_output_aliases={i: 2 + i for i in range(2 * nb)},
        compiler_params=pltpu.CompilerParams(has_side_effects=DATAFLOW),
    )(*[hbm(a) for a in list(bufs) + lands], after)
    return (out[:2], out[2:2 + nb], out[2 + nb:2 + 2 * nb], scatter), out[-1]


def exchange_wait(state, after, *, name):
    (send_sems, recv_sems), srcs, lands, scatter = state
    nb = len(srcs)

    def body(*refs):
        src_refs, zones = refs[:nb], refs[nb:2 * nb]
        s_sems, r_sems = refs[2 * nb:2 * nb + 2]
        for send, recv in _exchange_copies(src_refs, zones, s_sems, r_sems, scatter):
            send.wait_send()
            recv.wait_recv()

    out = pl.pallas_call(
        body, name=name, in_specs=[HBM] * (2 * nb) + [SEM, SEM, ANY], out_specs=(HBM,) * (2 * nb),
        out_shape=tuple(pltpu.HBM(a.shape, a.dtype) for a in list(srcs) + list(lands)),
        input_output_aliases={i: i for i in range(2 * nb)},
        compiler_params=pltpu.CompilerParams(has_side_effects=DATAFLOW),
    )(*srcs, *lands, send_sems, recv_sems, after)
    return out[nb:]


BIG = (
    ("w_proj_gdn", 256), ("w_proj_ssd", 256), ("w_proj_swa", 256), ("w_out", 256), ("w_up", 1024), ("w_down", 1024))
BIG_OFF = {}
_o = 0
for _n, _r in BIG:
    BIG_OFF[_n] = _o
    _o += _r
BIG_ROWS = _o
W_IN_SHARD = IN_W // 4

W_NAMES = ('meta_tokens', 'norm1_w', 'w_in', 'gdn_conv_w', 'gdn_a_log', 'gdn_dt_bias', 'gdn_norm_w', 'ssd_conv_w',
           'ssd_conv_b', 'ssd_dt_bias', 'ssd_a_log', 'ssd_d', 'ssd_norm_w', 'swa_sinks', 'w_proj_gdn', 'w_proj_ssd',
           'w_proj_swa', 'w_out', 'norm2_w', 'w_up', 'w_down', 'final_norm_w')
SMALL_NAMES = tuple(n for n in W_NAMES if n not in BIG_OFF and n != "w_in")
SMALL_SHARDED = ("meta_tokens", "gdn_conv_w", "ssd_conv_w")


def _pad_rows(a, rows):
    return jnp.pad(a, ((0, rows - a.shape[0]), (0, 0)))


def _pack_rows(parts, dtype):
    flat = jnp.concatenate([p.reshape(-1).astype(dtype) for p in parts])
    n = -(-flat.shape[0] // 8192) * 8192
    return jnp.pad(flat, (0, n - flat.shape[0])).reshape(-1, D_MODEL)


def _unpack_rows(packed, shapes):
    flat, out, o = packed.reshape(-1), [], 0
    for s in shapes:
        n = 1
        for d in s:
            n *= d
        out.append(flat[o:o + n].reshape(s))
        o += n
    return out


def _split_chips(full, axis):
    s = full.shape
    a = full.reshape(s[:axis] + (4, s[axis] // 4) + s[axis + 1:])
    return jnp.moveaxis(a, axis, 0)


def _join_chips(parts, axis):
    a = jnp.moveaxis(parts, 0, axis)
    s = a.shape
    return a.reshape(s[:axis] + (s[axis] * s[axis + 1],) + s[axis + 2:])


BIG_AXIS = {"w_in": 2, "w_proj_gdn": 1, "w_proj_ssd": 1, "w_proj_swa": 1, "w_out": 1, "w_up": 2, "w_down": 1}


def _w_in_to_padded(w):
    z = lambda n: jnp.zeros(w.shape[:-1] + (n,), w.dtype)
    return jnp.concatenate([w[..., 8736:11808], w[..., 4112:7184], w[..., 7200:8736], w[..., 4096:4112], z(112),
                            w[..., 7184:7200], z(112 + C_MID_END - C_SDT - 128), w[..., 0:4096]], axis=-1)


def _w_in_from_padded(p):
    return jnp.concatenate([p[..., C_GQ:IN_WP], p[..., C_BA:C_BA + 16], p[..., C_SZ:C_WQ], p[..., C_SDT:C_SDT + 16],
                            p[..., C_WQ:C_BA], p[..., 0:C_SZ]], axis=-1)


def _row8(v, lane0=0, width=128):
    return jnp.pad(v[None, :], ((0, 7), (lane0, width - lane0 - v.shape[0])))


def _head_major(a, heads):
    return a.reshape(a.shape[0], heads, SWA_D).transpose(1, 0, 2)


def _from_head_major(a):
    return a.transpose(1, 0, 2).reshape(a.shape[1], -1)


def _layer_fwd(h, p, l, late=None):
    tag = f"l{l}"
    hn = rmsnorm_fwd(h, p["n1"], name=f"norm1_fwd_{tag}")
    u = mm(hn, p["w_in"], out_dtype=f32, name=f"mm_in_{tag}")
    yg, stg = gdn_fwd(u, p["gcw"], p["galog"], p["gdtb"], p["gnw"])
    ys, sts = ssd_fwd(u, p["scw"], p["sdtb"], p["salog"], p["sd"], p["snw"])
    qh = _head_major(u[:, C_WQ:C_WK], SWA_Q_HEADS)
    kh = _head_major(u[:, C_WK:C_WV], SWA_KV_HEADS)
    vh = _head_major(u[:, C_WV:C_BA], SWA_KV_HEADS)
    yw = _from_head_major(swa_fwd(qh, kh, vh, p["sink"]))
    if late is not None:
        p.update(late(yw))
    pg = mm(yg, p["wpg"], out_dtype=f32, name=f"mm_pg_{tag}")
    ps = mm(ys, p["wps"], out_dtype=f32, name=f"mm_ps_{tag}")
    pw = mm(yw, p["wpw"], out_dtype=f32, name=f"mm_pw_{tag}")
    merged = merge_fwd(pg, ps, pw, u)
    h2 = mm(merged, p["wout"], out_dtype=f32, resid=h, name=f"mm_out_{tag}")
    hn2 = rmsnorm_fwd(h2, p["n2"], name=f"norm2_fwd_{tag}")
    a, r = mm(hn2, p["wup"], out_dtype=f32, relu2_out=True, name=f"mm_up_{tag}")
    h3 = mm(r, p["wdown"], out_dtype=f32, resid=h2, name=f"mm_down_{tag}")
    saved = dict(h=h, hn=hn, u=u, yg=yg, stg=stg, ys=ys, sts=sts, qh=qh, kh=kh, vh=vh, yw=yw, pg=pg, ps=ps, pw=pw,
                 merged=merged, h2=h2, hn2=hn2, a=a, r=r)
    return h3, saved


def _layer_bwd(dh3, p, s, l, send_big, send_w_in):
    tag = f"l{l}"
    g = {}

    def wgrad(act, d, name):
        return mm(act.T, d, out_dtype=bf16, name=f"wg_{name}_{tag}")

    da = mm(dh3, p["wdown_t"], out_dtype=bf16, relu_grad_of=s["a"], name=f"dg_down_{tag}")
    g["w_down"] = wgrad(s["r"], dh3, "down")
    dhn2 = mm(da, p["wup_t"], out_dtype=f32, name=f"dg_up_{tag}")
    g["w_up"] = wgrad(s["hn2"], da, "up")
    dh2, g["norm2_w"] = rmsnorm_bwd(s["h2"], p["n2"], dhn2, dh3, name=f"norm2_bwd_{tag}")
    dmerged = mm(dh2, p["wout_t"], out_dtype=f32, name=f"dg_out_{tag}")
    g["w_out"] = wgrad(s["merged"], dh2, "out")
    du = lax.empty((dh3.shape[0], IN_WP), bf16)
    dpg, dps, dpw, du = merge_bwd(s["pg"], s["ps"], s["pw"], s["u"], dmerged, du)
    dyg = mm(dpg, p["wpg_t"], out_dtype=f32, name=f"dg_pg_{tag}")
    dys = mm(dps, p["wps_t"], out_dtype=f32, name=f"dg_ps_{tag}")
    dyw = mm(dpw, p["wpw_t"], out_dtype=f32, name=f"dg_pw_{tag}")
    g["w_proj_gdn"] = wgrad(s["yg"], dpg, "pg")
    g["w_proj_ssd"] = wgrad(s["ys"], dps, "ps")
    g["w_proj_swa"] = wgrad(s["yw"], dpw, "pw")
    sent = send_big(jnp.concatenate([_split_chips(g.pop(n), BIG_AXIS[n] - 1).reshape(4, r, D_MODEL)
                                     for n, r in BIG], axis=1))

    (du, dba, dtq, dtk, dtv, g["gdn_a_log"], g["gdn_dt_bias"], g["gdn_norm_w"]) = gdn_bwd(
        s["u"], p["gcw"] + sent, p["galog"], p["gdtb"], p["gnw"], s["stg"], dyg, du)
    g["gdn_conv_w"] = jnp.concatenate([dtq, dtk, dtv], axis=1)[:4]
    (du, ddt, dtx, dtb, dtc, g["ssd_dt_bias"], g["ssd_a_log"], g["ssd_d"], g["ssd_norm_w"]) = ssd_bwd(
        s["u"], p["scw"], p["sdtb"], p["salog"], p["sd"], p["snw"], s["sts"], dys, du)
    dconv = jnp.concatenate([dtx, dtb, dtc], axis=1)
    g["ssd_conv_w"], g["ssd_conv_b"] = dconv[:4], dconv[4]
    dqh, dkh, dvh, g["swa_sinks"] = swa_bwd(s["qh"], s["kh"], s["vh"], p["sink"], _head_major(dyw, SWA_Q_HEADS))
    mid = jnp.concatenate([_from_head_major(dqh), _from_head_major(dkh), _from_head_major(dvh), dba[0].astype(bf16),
                           ddt.astype(bf16), jnp.zeros((du.shape[0], C_MID_END - C_SDT - 128), bf16)], axis=1)
    du = lax.dynamic_update_slice(du, mid, (0, C_WQ))
    sent = send_w_in(_split_chips(_w_in_from_padded(wgrad(s["hn"], du, "in")), 1))
    dhn = mm(du, p["w_in_t"], out_dtype=f32, name=f"dg_in_{tag}")
    dh, g["norm1_w"] = rmsnorm_bwd(s["h"], p["n1"] + sent, dhn, dh2, name=f"norm1_bwd_{tag}")
    return dh, g


def kernel(x, meta_tokens, norm1_w, w_in, gdn_conv_w, gdn_a_log, gdn_dt_bias, gdn_norm_w, ssd_conv_w, ssd_conv_b, ssd_dt_bias, ssd_a_log, ssd_d, ssd_norm_w, swa_sinks, w_proj_gdn, w_proj_ssd, w_proj_swa, w_out, norm2_w, w_up, w_down, final_norm_w, loss_target, m_meta_tokens, m_norm1_w, m_w_in, m_gdn_conv_w, m_gdn_a_log, m_gdn_dt_bias, m_gdn_norm_w, m_ssd_conv_w, m_ssd_conv_b, m_ssd_dt_bias, m_ssd_a_log, m_ssd_d, m_ssd_norm_w, m_swa_sinks, m_w_proj_gdn, m_w_proj_ssd, m_w_proj_swa, m_w_out, m_norm2_w, m_w_up, m_w_down, m_final_norm_w, v_meta_tokens, v_norm1_w, v_w_in, v_gdn_conv_w, v_gdn_a_log, v_gdn_dt_bias, v_gdn_norm_w, v_ssd_conv_w, v_ssd_conv_b, v_ssd_dt_bias, v_ssd_a_log, v_ssd_d, v_ssd_norm_w, v_swa_sinks, v_w_proj_gdn, v_w_proj_ssd, v_w_proj_swa, v_w_out, v_norm2_w, v_w_up, v_w_down, v_final_norm_w):
    given = dict(locals())
    depth = norm1_w.shape[0]
    me = 2 * lax.axis_index("x") + lax.axis_index("y")

    me1 = jnp.reshape(me, (1,)).astype(jnp.int32)
    is_me = (jnp.arange(4, dtype=jnp.int32) == me)[:, None, None]

    def weight_slabs(l):
        return (w_in[l].astype(bf16),
                jnp.concatenate([given[n][l].reshape(-1, D_MODEL).astype(bf16) for n, _ in BIG]))

    slabs = [weight_slabs(l) for l in range(depth)]
    wsmall = _pack_rows([given[n] for n in SMALL_SHARDED], f32)
    ga0, gsmall = chip_exchange([slabs[0][0], wsmall], (False, False), name="gather_first")
    gathers, started = {}, jnp.zeros((), f32)
    for l in range(depth):
        for j in range(2):
            if (l, j) != (0, 0):
                gathers[l, j], token = exchange_start([slabs[l][j]], (False,), gsmall, name=f"gather_start_l{l}_{j}")
                started = started + token[0, 0]
    shard_shapes = [given[n].shape for n in SMALL_SHARDED]
    per_chip = [_unpack_rows(gsmall[s], shard_shapes) for s in range(4)]
    full = {n: jnp.concatenate([per_chip[s][i] for s in range(4)], axis=-1) for i, n in enumerate(SMALL_SHARDED)}

    def landed(l, j, after):
        (zone,) = exchange_wait(gathers[l, j], after, name=f"gather_wait_l{l}_{j}")
        return jnp.where(is_me, slabs[l][j][None], zone)

    def first_operands(l, ga, order):
        w_in_p = _w_in_to_padded(_join_chips(ga, 1))
        return dict(
            n1=_row8(norm1_w[l], width=D_MODEL) + order, n2=_row8(norm2_w[l], width=D_MODEL),
            w_in=w_in_p, w_in_t=w_in_p.T,
            gcw=jnp.pad(full["gdn_conv_w"][l], ((0, 4), (0, 0))),
            galog=_row8(gdn_a_log[l], 8), gdtb=_row8(gdn_dt_bias[l], 8), gnw=_row8(gdn_norm_w[l]),
            scw=jnp.pad(jnp.concatenate([full["ssd_conv_w"][l], ssd_conv_b[l][None]], axis=0), ((0, 3), (0, 0))),
            sdtb=_row8(ssd_dt_bias[l]), salog=_row8(ssd_a_log[l]), sd=_row8(ssd_d[l]),
            snw=_row8(ssd_norm_w[l], width=D_MODEL), sink=_row8(swa_sinks[l]))

    def late_operands(l, after):
        gb = landed(l, 1, after)
        w = {}
        for n, r in BIG:
            parts = gb[:, BIG_OFF[n]:BIG_OFF[n] + r].reshape((4,) + given[n].shape[1:])
            w[n] = _join_chips(parts, BIG_AXIS[n] - 1)
        return dict(wpg=w["w_proj_gdn"], wps=w["w_proj_ssd"], wpw=w["w_proj_swa"], wout=w["w_out"],
                    wup=w["w_up"], wdown=w["w_down"],
                    wpg_t=w["w_proj_gdn"].T, wps_t=w["w_proj_ssd"].T, wpw_t=w["w_proj_swa"].T,
                    wout_t=w["w_out"].T, wup_t=w["w_up"].T, wdown_t=w["w_down"].T)

    h = jnp.concatenate([jnp.zeros((PAD, D_MODEL), f32), full["meta_tokens"], x[0]], axis=0)
    layers, saved = [], []
    for l in range(depth):
        p = first_operands(0, ga0, started) if l == 0 else first_operands(l, landed(l, 0, h), 0.0)
        h, s = _layer_fwd(h, p, l, late=functools.partial(late_operands, l))
        layers.append(p)
        saved.append(s)
    loss8, dh, dfw8 = loss_head(h, _row8(final_norm_w, width=D_MODEL), loss_target[0])
    grads = {"final_norm_w": dfw8[0]}
    per_layer, grad_slabs, scatters = [None] * depth, {}, {}

    def send(l, j, slab):
        grad_slabs[l, j] = slab
        scatters[l, j], token = exchange_start([slab], (True,), loss8, name=f"scatter_start_l{l}_{j}")
        return token[0, 0]

    for l in reversed(range(depth)):
        dh, per_layer[l] = _layer_bwd(dh, layers[l], saved[l], l, functools.partial(send, l, 1),
                                      functools.partial(send, l, 0))
    grad_x = dh[HEAD_ROWS:][None]
    grads["meta_tokens"] = dh[PAD:HEAD_ROWS]
    lane = {"gdn_a_log": (8, 8), "gdn_dt_bias": (8, 8), "gdn_norm_w": (0, 128), "ssd_dt_bias": (0, 16),
            "ssd_a_log": (0, 16), "ssd_d": (0, 16), "swa_sinks": (0, 16)}
    for n in per_layer[0]:
        parts = [per_layer[l][n] for l in range(depth)]
        if n in lane:
            parts = [q[0, lane[n][0]:lane[n][0] + lane[n][1]] for q in parts]
        elif n in ("norm1_w", "norm2_w", "ssd_norm_w"):
            parts = [q[0] for q in parts]
        grads[n] = jnp.stack(parts)
    loss = lax.psum(loss8[0, 0], ("x", "y", "c"))

    gs = _pack_rows([grads[n] for n in SMALL_NAMES], f32)
    flat = []
    for l in range(depth):
        for j in range(2):
            (zone,) = exchange_wait(scatters[l, j], dh, name=f"scatter_wait_l{l}_{j}")
            own = lax.dynamic_index_in_dim(grad_slabs[l, j], me, 0, keepdims=False)
            flat.append(reduce4(zone, own=own, me=me1, name=f"sum_chips_l{l}_{j}"))
    (rs,) = chip_exchange([gs], (False,), name="gather_small_grads")
    ps_ = reduce4(rs, name="sum_chips_small")
    flat.append(ps_)
    sib = sibling_swap(flat, name="swap_cores")
    ss = sib[-1]

    out = {}
    w_in_rows = depth * D_MODEL
    res = adamw(*[given[pre + "w_in"].reshape(w_in_rows, W_IN_SHARD) for pre in ("", "m_", "v_")],
                [(flat[2 * l], sib[2 * l]) for l in range(depth)], 0, name="adamw_w_in")
    out["w_in"] = [a.reshape(w_in.shape) for a in res]
    for n, r in BIG:
        shp = given[n].shape
        res = adamw(*[given[pre + n].reshape(depth * r, D_MODEL) for pre in ("", "m_", "v_")],
                    [(flat[2 * l + 1], sib[2 * l + 1]) for l in range(depth)], BIG_OFF[n], name=f"adamw_{n}")
        out[n] = [a.reshape(shp) for a in res]
    full_shapes = [grads[n].shape for n in SMALL_NAMES]
    mine_s, sib_s = _unpack_rows(ps_, full_shapes), _unpack_rows(ss, full_shapes)

    def local(parts):
        loc = []
        for n, a in zip(SMALL_NAMES, parts):
            if n in SMALL_SHARDED:
                sz = a.shape[-1] // 4
                a = lax.dynamic_slice_in_dim(a, me * sz, sz, axis=a.ndim - 1)
            loc.append(a)
        return _pack_rows(loc, f32)

    res = adamw(_pack_rows([given[n] for n in SMALL_NAMES], f32), _pack_rows([given["m_" + n] for n in SMALL_NAMES], f32),
                _pack_rows([given["v_" + n] for n in SMALL_NAMES], f32), [(local(mine_s), local(sib_s))], 0,
                name="adamw_small")
    local_shapes = [given[n].shape for n in SMALL_NAMES]
    unpacked = [_unpack_rows(a, local_shapes) for a in res]
    for i, n in enumerate(SMALL_NAMES):
        out[n] = [unpacked[j][i] for j in range(4)]

    return (loss, grad_x) + tuple(out[n][j] for j in range(4) for n in W_NAMES)
```

```python
import functools

import jax
import jax.numpy as jnp
from jax import lax
from jax.experimental import pallas as pl
from jax.experimental.pallas import tpu as pltpu

f32 = jnp.float32
bf16 = jnp.bfloat16
HI = lax.Precision.HIGHEST

D_MODEL = 1024
N_META = 16
PAD = 112
HEAD_ROWS = PAD + N_META
RMS_EPS = 1e-6
L2_EPS = 1e-6
D_FF = 4 * D_MODEL

GDN_HEADS = 8
GDN_D = 128
GDN_CHUNK = 64
SSD_HEADS = 16
SSD_P = 64
SSD_GROUPS = 4
SSD_HPG = 4
SSD_N = 128
SSD_CHUNK = 128
SWA_Q_HEADS = 16
SWA_KV_HEADS = 4
SWA_REP = 4
SWA_D = 64
SWA_W = 128

C_GATE = 0
C_SZ, C_SX, C_SB, C_SC = 3072, 4096, 5120, 5632
C_WQ, C_WK, C_WV = 6144, 7168, 7424
C_BA = 7680
C_SDT = 7808
C_MID_END = 8192
C_GQ, C_GK, C_GV, C_GG = 8192, 9216, 10240, 11264
IN_WP = 12288
IN_W = 11808

ADAM_LR, ADAM_B1, ADAM_B2, ADAM_EPS, ADAM_WD, ADAM_STEP = 0.001, 0.9, 0.999, 1e-08, 0.01, 10

VMEM_LIMIT = 56 * 1024 * 1024
BLOCK_BYTES = 3 << 19
MM_OPERAND_BYTES = 9 << 20
MM_RESIDENT_BYTES = 13 << 20

NN = (((1,), (0,)), ((), ()))
NT = (((1,), (1,)), ((), ()))
TN = (((0,), (0,)), ((), ()))


def _dot(a, b, dims=NN):
    return lax.dot_general(a.astype(bf16), b.astype(bf16), dims, preferred_element_type=f32)


def _dotx(a, b, dims=NN):
    return lax.dot_general(a, b, dims, preferred_element_type=f32, precision=lax.Precision.HIGH)


def _iota(shape, axis):
    return lax.broadcasted_iota(jnp.int32, shape, axis)


def _softplus(x):
    return jnp.maximum(x, 0.0) + jnp.log1p(jnp.exp(-jnp.abs(x)))


def _silu(x):
    return x * jax.nn.sigmoid(x)


def _params(sem):
    return pltpu.CompilerParams(dimension_semantics=sem, vmem_limit_bytes=VMEM_LIMIT)


@functools.partial(jax.custom_vjp, nondiff_argnums=(1,))
def _window(x_ext, off):
    n = x_ext.shape[0] - 8
    if off == 8:
        return x_ext[8:]
    return pltpu.roll(x_ext, 8 - off, 0)[8:]


def _window_fwd(x_ext, off):
    return _window(x_ext, off), None


def _window_bwd(off, _, g):
    n, w = g.shape
    g_ext = jnp.concatenate([jnp.zeros((8, w), g.dtype), g], axis=0)
    if off == 8:
        return (g_ext,)
    return (pltpu.roll(g_ext, n + off, 0),)


_window.defvjp(_window_fwd, _window_bwd)


def _conv4(x, halo, taps):
    x_ext = jnp.concatenate([halo, x], axis=0)
    y = taps[3] * x
    for j in range(3):
        y = y + taps[j] * _window(x_ext, 5 + j)
    return y


def _blockinv_impl(a):
    n = a.shape[0]
    ri, ci = _iota((n, n), 0), _iota((n, n), 1)
    t = (ri == ci).astype(f32)
    k = 0
    while (1 << k) < n:
        sel = ((ri >> (k + 1)) == (ci >> (k + 1))) & (((ri >> k) & 1) == 1) & (((ci >> k) & 1) == 0)
        o = jnp.where(sel, a, 0.0)
        t = t - _dotx(_dotx(t, o), t)
        k += 1
    return t


@jax.custom_vjp
def _blockinv(a):
    return _blockinv_impl(a)


def _blockinv_fwd(a):
    t = _blockinv_impl(a)
    return t, t


def _blockinv_bwd(t, dt):
    return (-_dotx(_dotx(t, dt, TN), t, NT),)


_blockinv.defvjp(_blockinv_fwd, _blockinv_bwd)


def _gdn_act(xq, xk, xv, hq, hk, hv, tq, tk, tv):
    return _silu(_conv4(xq, hq, tq)), _silu(_conv4(xk, hk, tk)), _silu(_conv4(xv, hv, tv))


def _gdn_core(q, k, v, gate, mb, mg, s, ba, alog, dtb, nw, *, row0):
    c = GDN_CHUNK
    q = q * lax.rsqrt(jnp.sum(q * q, axis=1, keepdims=True) + L2_EPS) * (GDN_D ** -0.5)
    k = k * lax.rsqrt(jnp.sum(k * k, axis=1, keepdims=True) + L2_EPS)

    valid = (row0 + _iota((c, 1), 0)) >= PAD
    pick = lambda x, m: jnp.sum(x * m, axis=1, keepdims=True)
    beta = jnp.where(valid, jax.nn.sigmoid(pick(ba, mb)), 0.0)
    g1 = jnp.where(valid, -jnp.exp(pick(alog, mg)) * _softplus(pick(ba, mg) + pick(dtb, mg)), 0.0)
    g = jnp.broadcast_to(g1, (c, GDN_D))
    g64 = jnp.broadcast_to(g1, (c, c))

    ri, ci = _iota((c, c), 0), _iota((c, c), 1)
    incl = ci <= ri
    gam = _dotx(incl.astype(f32), g)
    gam_i = _dotx(incl.astype(f32), g64)
    gam_j = _dotx(jnp.ones((c, c), f32), jnp.where(ri <= ci, g64, 0.0))
    decay = jnp.where(incl, jnp.exp(jnp.where(incl, gam_i - gam_j, 0.0)), 0.0)

    kb = k * beta
    a = jnp.where(ci < ri, _dot(kb, k, NT) * decay, 0.0)
    t = _blockinv(a)
    egam = jnp.exp(gam)
    u = _dotx(t, v * beta)
    w = _dotx(t, kb * egam)
    attn = _dot(q, k, NT) * decay
    gl = jnp.sum(g, axis=0, keepdims=True)
    kt = k * jnp.exp(gl - gam)
    v_new = u - _dot(w, s)
    o = _dot(q * egam, s) + _dot(attn, v_new)
    s_out = s * jnp.exp(gl) + _dot(kt, v_new, TN)

    y = o * lax.rsqrt(jnp.mean(o * o, axis=1, keepdims=True) + RMS_EPS) * nw * _silu(gate)
    return y, s_out


def _gdn_specs(hb, nc, rev):
    w = hb * GDN_D
    cw = D_MODEL // w

    def cidx(c):
        return (nc - 1 - c) if rev else c

    def col(base):
        return pl.BlockSpec((GDN_CHUNK, w), lambda h, c: (cidx(c), base // w + h))

    def halo(base):
        return pl.BlockSpec((8, w), lambda h, c: (jnp.maximum(cidx(c) * (GDN_CHUNK // 8) - 1, 0), base // w + h))

    def taps(base):
        return pl.BlockSpec((8, w), lambda h, c: (0, base // w + h))

    ba = pl.BlockSpec((GDN_CHUNK, 128), lambda h, c: (cidx(c), C_BA // 128))
    row = pl.BlockSpec((8, 128), lambda h, c: (0, 0))
    y = pl.BlockSpec((GDN_CHUNK, w), lambda h, c: (cidx(c), h))
    st = pl.BlockSpec((1, hb, GDN_D, GDN_D), lambda h, c: (cidx(c), h, 0, 0))
    in_specs = [col(C_GQ), col(C_GK), col(C_GV), halo(C_GQ), halo(C_GK), halo(C_GV), col(C_GG), ba,
                taps(0), taps(1024), taps(2048), row, row, row]
    return in_specs, y, st, taps, row, col, ba


def _gdn_load(refs, first):
    xq, xk, xv, hq, hk, hv, gate, ba, tq, tk, tv, alog, dtb, nw = refs

    def halo(r):
        return jnp.where(first, 0.0, r[...])

    def taps(r):
        return tuple(r[j:j + 1, :] for j in range(4))

    act = (xq[...], xk[...], xv[...], halo(hq), halo(hk), halo(hv), taps(tq), taps(tk), taps(tv))
    return act, gate[...], (ba[...], alog[0:1, :], dtb[0:1, :], nw[0:1, :])


def _heads(a, hb):
    return jnp.stack([a[:, i * GDN_D:(i + 1) * GDN_D] for i in range(hb)])


def _wide(a):
    return jnp.concatenate([a[i] for i in range(a.shape[0])], axis=1)


def _head_masks(hblk, hb):
    head = hblk * hb + _iota((hb, 1, 128), 0)
    lane = _iota((hb, 1, 128), 2)
    return (lane == head).astype(f32), (lane == head + 8).astype(f32)


def _gdn_core_heads(row0):
    return jax.vmap(functools.partial(_gdn_core, row0=row0), in_axes=(0, 0, 0, 0, 0, 0, 0, None, None, None, None))


def gdn_fwd(u, conv_w8, alog8, dtb8, nw8, *, hb=8):
    t_rows = u.shape[0]
    nc = t_rows // GDN_CHUNK
    in_specs, y_spec, st_spec, *_ = _gdn_specs(hb, nc, False)

    def body(*refs):
        ins, (y_ref, st_ref), (s_scr,) = refs[:14], refs[14:16], refs[16:]
        hblk, c = pl.program_id(0), pl.program_id(1)

        @pl.when(c == 0)
        def _():
            s_scr[...] = jnp.zeros_like(s_scr)

        act, gate, shared = _gdn_load(ins, c == 0)
        s = s_scr[...]
        st_ref[0] = s
        qa, ka, va = _gdn_act(*act)
        mb, mg = _head_masks(hblk, hb)
        y, s_new = _gdn_core_heads(c * GDN_CHUNK)(_heads(qa, hb), _heads(ka, hb), _heads(va, hb), _heads(gate, hb),
                                                  mb, mg, s, *shared)
        y_ref[...] = _wide(y).astype(bf16)
        s_scr[...] = s_new

    return pl.pallas_call(
        body, name="gdn_fwd", grid=(GDN_HEADS // hb, nc),
        in_specs=in_specs, out_specs=(y_spec, st_spec),
        out_shape=(jax.ShapeDtypeStruct((t_rows, D_MODEL), bf16),
                   jax.ShapeDtypeStruct((nc, GDN_HEADS, GDN_D, GDN_D), f32)),
        scratch_shapes=[pltpu.VMEM((hb, GDN_D, GDN_D), f32)],
        compiler_params=_params(("arbitrary", "arbitrary")),
    )(u, u, u, u, u, u, u, u, conv_w8, conv_w8, conv_w8, alog8, dtb8, nw8)


def gdn_bwd(u, conv_w8, alog8, dtb8, nw8, states, dy, du):
    t_rows = u.shape[0]
    nc = t_rows // GDN_CHUNK
    hb = GDN_HEADS
    w = hb * GDN_D
    in_specs, y_spec, st_spec, taps, row, col, ba = _gdn_specs(hb, nc, True)
    nhb = GDN_HEADS // hb

    def body(*refs):
        ins, st_ref, dy_ref = refs[:14], refs[14], refs[15]
        du_ref, dba_ref, dtq_ref, dtk_ref, dtv_ref, dalog_ref, ddtb_ref, dnw_ref = refs[17:25]
        ds_scr, dh_scr = refs[25:]
        hblk, cc = pl.program_id(0), pl.program_id(1)
        c = nc - 1 - cc

        @pl.when(cc == 0)
        def _():
            ds_scr[...] = jnp.zeros_like(ds_scr)
            dh_scr[...] = jnp.zeros_like(dh_scr)
            dtq_ref[...] = jnp.zeros_like(dtq_ref)
            dtk_ref[...] = jnp.zeros_like(dtk_ref)
            dtv_ref[...] = jnp.zeros_like(dtv_ref)

        @pl.when((cc == 0) & (hblk == 0))
        def _():
            dalog_ref[...] = jnp.zeros_like(dalog_ref)
            ddtb_ref[...] = jnp.zeros_like(ddtb_ref)
            dnw_ref[...] = jnp.zeros_like(dnw_ref)

        act, gate, shared = _gdn_load(ins, c == 0)
        (qa, ka, va), vjp_act = jax.vjp(_gdn_act, *act)
        mb, mg = _head_masks(hblk, hb)
        _, vjp_core = jax.vjp(_gdn_core_heads(c * GDN_CHUNK), _heads(qa, hb), _heads(ka, hb), _heads(va, hb),
                              _heads(gate, hb), mb, mg, st_ref[0], *shared)
        dqa, dka, dva, dgate, _, _, ds, dba, dalog, ddtb, dnw = vjp_core(
            (_heads(dy_ref[...].astype(f32), hb), ds_scr[...]))
        ds_scr[...] = ds
        dxq, dxk, dxv, dhq, dhk, dhv, dtq, dtk, dtv = vjp_act((_wide(dqa), _wide(dka), _wide(dva)))
        zeros = jnp.zeros((GDN_CHUNK - 8, w), f32)
        for j, (dx, dh) in enumerate(((dxq, dhq), (dxk, dhk), (dxv, dhv))):
            du_ref[:, j * w:(j + 1) * w] = (dx + jnp.concatenate([zeros, dh_scr[j]], axis=0)).astype(bf16)
            dh_scr[j] = dh
        du_ref[:, 3 * w:4 * w] = _wide(dgate).astype(bf16)
        dba_ref[0] = dba
        for dt_ref, dtaps in ((dtq_ref, dtq), (dtk_ref, dtk), (dtv_ref, dtv)):
            for j in range(4):
                dt_ref[j:j + 1, :] += dtaps[j]
        dalog_ref[0:1, :] += dalog
        ddtb_ref[0:1, :] += ddtb
        dnw_ref[0:1, :] += dnw

    out_specs = (pl.BlockSpec((GDN_CHUNK, 4 * w), lambda h, c: (nc - 1 - c, C_GQ // (4 * w))),
                 pl.BlockSpec((1, GDN_CHUNK, 128), lambda h, c: (h, nc - 1 - c, 0)),
                 taps(0), taps(0), taps(0), row, row, row)
    out_shape = (jax.ShapeDtypeStruct(du.shape, du.dtype),
                 jax.ShapeDtypeStruct((nhb, t_rows, 128), f32),
                 jax.ShapeDtypeStruct((8, D_MODEL), f32), jax.ShapeDtypeStruct((8, D_MODEL), f32),
                 jax.ShapeDtypeStruct((8, D_MODEL), f32),
                 jax.ShapeDtypeStruct((8, 128), f32), jax.ShapeDtypeStruct((8, 128), f32), jax.ShapeDtypeStruct((8, 128), f32))
    return pl.pallas_call(
        body, name="gdn_bwd", grid=(nhb, nc),
        in_specs=in_specs + [st_spec, y_spec, ANY], out_specs=out_specs, out_shape=out_shape,
        input_output_aliases={16: 0},
        scratch_shapes=[pltpu.VMEM((hb, GDN_D, GDN_D), f32), pltpu.VMEM((3, 8, w), f32)],
        compiler_params=_params(("arbitrary", "arbitrary")),
    )(u, u, u, u, u, u, u, u, conv_w8, conv_w8, conv_w8, alog8, dtb8, nw8, states, dy, du)


def _ssd_act(xs_r, b_r, c_r, hx, hbm, hcm, tx, tb, tc, bx, bb, bc, *, row0):
    valid = (row0 + _iota((SSD_CHUNK, 1), 0)) >= PAD
    act = lambda x, h, t, b: jnp.where(valid, _silu(_conv4(x, h, t) + b), 0.0)
    return act(xs_r, hx, tx, bx), act(b_r, hbm, tb, bb), act(c_r, hcm, tc, bc)


def _ssd_core(xs, bm, cm, z, nw, m0, m1, m2, m3, h, dt, dtb, alog, dsk, *, row0):
    n = SSD_CHUNK
    valid = (row0 + _iota((n, 1), 0)) >= PAD
    dtp16 = _softplus(dt + dtb)
    a16 = -jnp.exp(alog)
    pick = lambda x, m: jnp.sum(x * m, axis=1, keepdims=True)
    lane_r = _iota((1, 256), 1) >> 6
    dtp = jnp.zeros((n, 256), f32)
    adt = jnp.zeros((n, 256), f32)
    dlane = jnp.zeros((1, 256), f32)
    acols = []
    for r, m in enumerate((m0, m1, m2, m3)):
        dcol = jnp.where(valid, pick(dtp16, m), 0.0)
        acol = dcol * pick(a16, m)
        dtp = jnp.where(lane_r == r, dcol, dtp)
        adt = jnp.where(lane_r == r, acol, adt)
        dlane = jnp.where(lane_r == r, pick(dsk, m), dlane)
        acols.append(acol)

    ri, ci = _iota((n, n), 0), _iota((n, n), 1)
    incl = ci <= ri
    inclf = incl.astype(f32)
    acum = _dotx(inclf, adt)
    al = jnp.sum(adt, axis=0, keepdims=True)
    xdt = xs * dtp
    cb = _dot(cm, bm, NT)
    y = _dot(cm, h) * jnp.exp(acum) + dlane * xs
    for r in range(SSD_HPG):
        ab = jnp.broadcast_to(acols[r], (n, n))
        ai = _dotx(inclf, ab)
        aj = _dotx(jnp.ones((n, n), f32), jnp.where(ri <= ci, ab, 0.0))
        lm = jnp.where(incl, jnp.exp(jnp.where(incl, ai - aj, 0.0)), 0.0)
        y = y + _dot(cb * lm, jnp.where(lane_r == r, xdt, 0.0))
    h_out = h * jnp.exp(al) + _dot(bm, jnp.exp(al - acum) * xdt, TN)
    y = y * _silu(z)
    y = y * lax.rsqrt(jnp.mean(y * y, axis=1, keepdims=True) + RMS_EPS) * nw
    return y, h_out


def _ssd_core_groups(row0):
    return jax.vmap(functools.partial(_ssd_core, row0=row0), in_axes=(0,) * 10 + (None,) * 4)


def _ssd_specs(nc, rev):
    n = SSD_CHUNK

    def cidx(c):
        return (nc - 1 - c) if rev else c

    def col(base, w):
        return pl.BlockSpec((n, w), lambda c: (cidx(c), base // w))

    def halo(base, w):
        return pl.BlockSpec((8, w), lambda c: (jnp.maximum(cidx(c) * (n // 8) - 1, 0), base // w))

    def taps(base, w):
        return pl.BlockSpec((8, w), lambda c: (0, base // w))

    row = pl.BlockSpec((8, 128), lambda c: (0, 0))
    in_specs = [col(C_SX, 1024), col(C_SB, 512), col(C_SC, 512), halo(C_SX, 1024), halo(C_SB, 512), halo(C_SC, 512),
                col(C_SZ, 1024), col(C_SDT, 128), taps(0, 1024), taps(1024, 512), taps(1536, 512), row, row, row,
                taps(0, 1024)]
    y = pl.BlockSpec((n, D_MODEL), lambda c: (cidx(c), 0))
    st = pl.BlockSpec((1, SSD_GROUPS, SSD_N, 256), lambda c: (cidx(c), 0, 0, 0))
    return in_specs, y, st, col, taps, row


def _ssd_load(refs, first):
    xs, bm, cm, hx, hbm, hcm, z, dt, tx, tb, tc, dtb, alog, dsk, nw = refs

    def halo(r):
        return jnp.where(first, 0.0, r[...])

    def taps(r):
        return tuple(r[j:j + 1, :] for j in range(4))

    act = (xs[...], bm[...], cm[...], halo(hx), halo(hbm), halo(hcm), taps(tx), taps(tb), taps(tc),
           tx[4:5, :], tb[4:5, :], tc[4:5, :])
    return act, (z[...], nw[0:1, :]), (dt[...], dtb[0:1, :], alog[0:1, :], dsk[0:1, :])


def _groups(a, w):
    return jnp.stack([a[:, i * w:(i + 1) * w] for i in range(SSD_GROUPS)])


def _ssd_masks():
    head = _iota((SSD_GROUPS, 1, 128), 0) * SSD_HPG
    lane = _iota((SSD_GROUPS, 1, 128), 2)
    return tuple((lane == head + r).astype(f32) for r in range(SSD_HPG))


def ssd_fwd(u, conv_w8, dtb8, alog8, d8, nw8):
    t_rows = u.shape[0]
    nc = t_rows // SSD_CHUNK
    in_specs, y_spec, st_spec, *_ = _ssd_specs(nc, False)

    def body(*refs):
        ins, (y_ref, st_ref), (h_scr,) = refs[:15], refs[15:17], refs[17:]
        c = pl.program_id(0)

        @pl.when(c == 0)
        def _():
            h_scr[...] = jnp.zeros_like(h_scr)

        act, (z, nw), shared = _ssd_load(ins, c == 0)
        h = h_scr[...]
        st_ref[0] = h
        xs, bm, cm = _ssd_act(*act, row0=c * SSD_CHUNK)
        y, h_new = _ssd_core_groups(c * SSD_CHUNK)(_groups(xs, 256), _groups(bm, 128), _groups(cm, 128),
                                                   _groups(z, 256), _groups(nw, 256), *_ssd_masks(), h, *shared)
        y_ref[...] = _wide(y).astype(bf16)
        h_scr[...] = h_new

    return pl.pallas_call(
        body, name="ssd_fwd", grid=(nc,), in_specs=in_specs, out_specs=(y_spec, st_spec),
        out_shape=(jax.ShapeDtypeStruct((t_rows, D_MODEL), bf16),
                   jax.ShapeDtypeStruct((nc, SSD_GROUPS, SSD_N, 256), f32)),
        scratch_shapes=[pltpu.VMEM((SSD_GROUPS, SSD_N, 256), f32)],
        compiler_params=_params(("arbitrary",)),
    )(u, u, u, u, u, u, u, u, conv_w8, conv_w8, conv_w8, dtb8, alog8, d8, nw8)


def ssd_bwd(u, conv_w8, dtb8, alog8, d8, nw8, states, dy, du):
    t_rows = u.shape[0]
    nc = t_rows // SSD_CHUNK
    n = SSD_CHUNK
    in_specs, y_spec, st_spec, col, taps, row = _ssd_specs(nc, True)

    def body(*refs):
        ins, st_ref, dy_ref = refs[:15], refs[15], refs[16]
        du_ref, ddt_ref, dtx_ref, dtb_ref, dtc_ref, ddtb_ref, dalog_ref, ddsk_ref, dnw_ref = refs[18:27]
        dh_scr, hx_scr, hb_scr, hc_scr = refs[27:]
        cc = pl.program_id(0)
        c = nc - 1 - cc

        @pl.when(cc == 0)
        def _():
            for r in (dh_scr, hx_scr, hb_scr, hc_scr, dtx_ref, dtb_ref, dtc_ref, dnw_ref, ddtb_ref, dalog_ref, ddsk_ref):
                r[...] = jnp.zeros_like(r)

        act, (z, nw), shared = _ssd_load(ins, c == 0)
        (xs, bm, cm), vjp_act = jax.vjp(functools.partial(_ssd_act, row0=c * n), *act)
        _, vjp_core = jax.vjp(_ssd_core_groups(c * n), _groups(xs, 256), _groups(bm, 128), _groups(cm, 128),
                              _groups(z, 256), _groups(nw, 256), *_ssd_masks(), st_ref[0], *shared)
        dxa, dba, dca, dz, dnw, _, _, _, _, dh, ddt, ddtb, dalog, ddsk = vjp_core(
            (_groups(dy_ref[...].astype(f32), 256), dh_scr[...]))
        dh_scr[...] = dh
        dxs, dbm, dcm, dhx, dhb, dhc, dtx, dtb, dtc, dbx, dbb, dbc = vjp_act((_wide(dxa), _wide(dba), _wide(dca)))
        du_ref[:, 0:D_MODEL] = _wide(dz).astype(bf16)
        for dx, dhalo, scr, lo in ((dxs, dhx, hx_scr, C_SX), (dbm, dhb, hb_scr, C_SB), (dcm, dhc, hc_scr, C_SC)):
            zeros = jnp.zeros((n - 8, dx.shape[1]), f32)
            du_ref[:, lo - C_SZ:lo - C_SZ + dx.shape[1]] = (dx + jnp.concatenate([zeros, scr[...]], axis=0)).astype(bf16)
            scr[...] = dhalo
        ddt_ref[...] = ddt
        for ref, dtaps, dbias in ((dtx_ref, dtx, dbx), (dtb_ref, dtb, dbb), (dtc_ref, dtc, dbc)):
            for j in range(4):
                ref[j:j + 1, :] += dtaps[j]
            ref[4:5, :] += dbias
        ddtb_ref[0:1, :] += ddtb
        dalog_ref[0:1, :] += dalog
        ddsk_ref[0:1, :] += ddsk
        dnw_ref[0:1, :] += _wide(dnw)

    def out_col(w):
        return pl.BlockSpec((n, w), lambda c: (nc - 1 - c, 0))

    out_specs = (pl.BlockSpec((n, 3 * D_MODEL), lambda c: (nc - 1 - c, C_SZ // (3 * D_MODEL))), out_col(128),
                 taps(0, D_MODEL), taps(0, 512), taps(0, 512), row, row, row, taps(0, D_MODEL))
    out_shape = (jax.ShapeDtypeStruct(du.shape, du.dtype),
                 jax.ShapeDtypeStruct((t_rows, 128), f32),
                 jax.ShapeDtypeStruct((8, D_MODEL), f32), jax.ShapeDtypeStruct((8, 512), f32),
                 jax.ShapeDtypeStruct((8, 512), f32),
                 jax.ShapeDtypeStruct((8, 128), f32), jax.ShapeDtypeStruct((8, 128), f32),
                 jax.ShapeDtypeStruct((8, 128), f32), jax.ShapeDtypeStruct((8, D_MODEL), f32))
    return pl.pallas_call(
        body, name="ssd_bwd", grid=(nc,), in_specs=in_specs + [st_spec, y_spec, ANY],
        out_specs=out_specs, out_shape=out_shape, input_output_aliases={17: 0},
        scratch_shapes=[pltpu.VMEM((SSD_GROUPS, SSD_N, 256), f32), pltpu.VMEM((8, D_MODEL), f32),
                        pltpu.VMEM((8, 512), f32), pltpu.VMEM((8, 512), f32)],
        compiler_params=_params(("arbitrary",)),
    )(u, u, u, u, u, u, u, u, conv_w8, conv_w8, conv_w8, dtb8, alog8, d8, nw8, states, dy, du)


NEG = -1e30


def _swa_core(q, kc, kp, km, vc, vp, vm, sink, *, n):
    rows = SWA_REP * SWA_W
    ri, ci = _iota((rows, SWA_W), 0) & (SWA_W - 1), _iota((rows, SWA_W), 1)
    causal = ci <= ri
    m_cur = causal & ((n >= 1) | ((ci >= PAD) & (ri >= PAD)))
    m_prev = (n >= 2) & (ci > ri)
    m_meta = (n >= 1) & (ci >= PAD)
    q = q * (SWA_D ** -0.5)
    sc = jnp.where(m_cur, _dot(q, kc, NT), NEG)
    sp = jnp.where(m_prev, _dot(q, kp, NT), NEG)
    sm = jnp.where(m_meta, _dot(q, km, NT), NEG)
    mx = jnp.maximum(jnp.maximum(jnp.max(sc, axis=1, keepdims=True), jnp.max(sp, axis=1, keepdims=True)),
                     jnp.maximum(jnp.max(sm, axis=1, keepdims=True), sink))
    mx = lax.stop_gradient(mx)
    ec, ep, em = jnp.exp(sc - mx), jnp.exp(sp - mx), jnp.exp(sm - mx)
    den = (jnp.sum(ec, axis=1, keepdims=True) + jnp.sum(ep, axis=1, keepdims=True)
           + jnp.sum(em, axis=1, keepdims=True) + jnp.exp(sink - mx))
    return (_dot(ec, vc) + _dot(ep, vp) + _dot(em, vm)) / den


def _swa_block(q16, kc, kp, km, vc, vp, vm, sink16, *, n):
    rows = SWA_REP * SWA_W
    lane = _iota((1, 128), 1)
    rep = _iota((rows, 1), 0) >> 7
    cols = []
    for h in range(SWA_KV_HEADS):
        col = jnp.zeros((rows, 1), f32)
        for r in range(SWA_REP):
            s = jnp.sum(jnp.where(lane == h * SWA_REP + r, sink16, 0.0), axis=1, keepdims=True)
            col = jnp.where(rep == r, s, col)
        cols.append(col)
    o = jax.vmap(functools.partial(_swa_core, n=n))(q16.reshape(SWA_KV_HEADS, rows, SWA_D), kc, kp, km, vc, vp, vm,
                                                    jnp.concatenate([col[None] for col in cols], axis=0))
    return o.reshape(q16.shape)


def _swa_specs(nb, rev):
    def bidx(n):
        return (nb - 1 - n) if rev else n

    q = pl.BlockSpec((SWA_Q_HEADS, SWA_W, SWA_D), lambda n: (0, bidx(n), 0))
    cur = pl.BlockSpec((SWA_KV_HEADS, SWA_W, SWA_D), lambda n: (0, bidx(n), 0))
    prev = pl.BlockSpec((SWA_KV_HEADS, SWA_W, SWA_D), lambda n: (0, jnp.maximum(bidx(n) - 1, 0), 0))
    meta = pl.BlockSpec((SWA_KV_HEADS, SWA_W, SWA_D), lambda n: (0, 0, 0))
    row = pl.BlockSpec((8, 128), lambda n: (0, 0))
    return [q, cur, prev, meta, cur, prev, meta, row], q, cur, row


def swa_fwd(q, k, v, sink8):
    t_rows = q.shape[1]
    nb = t_rows // SWA_W
    in_specs, q_spec, _, _ = _swa_specs(nb, False)

    def body(q_ref, kc, kp, km, vc, vp, vm, sink_ref, o_ref):
        o_ref[...] = _swa_block(q_ref[...], kc[...], kp[...], km[...], vc[...], vp[...], vm[...], sink_ref[0:1, :],
                                n=pl.program_id(0)).astype(bf16)

    return pl.pallas_call(
        body, name="swa_fwd", grid=(nb,), in_specs=in_specs, out_specs=q_spec,
        out_shape=jax.ShapeDtypeStruct(q.shape, bf16),
        compiler_params=_params(("arbitrary",)),
    )(q, k, k, k, v, v, v, sink8)


def swa_bwd(q, k, v, sink8, do):
    t_rows = q.shape[1]
    nb = t_rows // SWA_W
    in_specs, q_spec, kv_spec, row = _swa_specs(nb, True)

    def body(q_ref, kc, kp, km, vc, vp, vm, sink_ref, do_ref, dq_ref, dk_ref, dv_ref, dsink_ref,
             dkp_scr, dvp_scr, dkm_scr, dvm_scr):
        nn = pl.program_id(0)
        n = nb - 1 - nn

        @pl.when(nn == 0)
        def _():
            for r in (dkp_scr, dvp_scr, dkm_scr, dvm_scr, dsink_ref):
                r[...] = jnp.zeros_like(r)

        fn = functools.partial(_swa_block, n=n)
        _, vjp = jax.vjp(fn, q_ref[...], kc[...], kp[...], km[...], vc[...], vp[...], vm[...], sink_ref[0:1, :])
        dq, dkc, dkp, dkm, dvc, dvp, dvm, dsink = vjp(do_ref[...].astype(f32))
        dq_ref[...] = dq.astype(bf16)
        dkm_scr[...] += dkm
        dvm_scr[...] += dvm
        first = n == 0
        dk_ref[...] = (dkc + dkp_scr[...] + jnp.where(first, dkm_scr[...], 0.0)).astype(bf16)
        dv_ref[...] = (dvc + dvp_scr[...] + jnp.where(first, dvm_scr[...], 0.0)).astype(bf16)
        dkp_scr[...] = dkp
        dvp_scr[...] = dvp
        dsink_ref[0:1, :] += dsink

    kv_shape = jax.ShapeDtypeStruct(k.shape, bf16)
    return pl.pallas_call(
        body, name="swa_bwd", grid=(nb,), in_specs=in_specs + [q_spec],
        out_specs=(q_spec, kv_spec, kv_spec, row),
        out_shape=(jax.ShapeDtypeStruct(q.shape, bf16), kv_shape, kv_shape, jax.ShapeDtypeStruct((8, 128), f32)),
        scratch_shapes=[pltpu.VMEM((SWA_KV_HEADS, SWA_W, SWA_D), f32)] * 4,
        compiler_params=_params(("arbitrary",)),
    )(q, k, k, k, v, v, v, sink8, do)


def _tile(dim, prefs):
    for p in prefs:
        if dim % p == 0:
            return p
    return dim


def mm(a, b, *, out_dtype, name, resid=None, relu_grad_of=None, relu2_out=False, ta=False, tb=False):
    assert resid is None or relu_grad_of is None
    k, m = (a.shape if ta else a.shape[::-1])
    n = b.shape[0] if tb else b.shape[1]
    rhs_stays = k * 2 * 1024 > MM_OPERAND_BYTES
    assert not (ta and rhs_stays)
    if rhs_stays:
        tn = _tile(n, tuple(p for p in (512, 256, 128) if p * k * 2 <= MM_RESIDENT_BYTES))
        tm = _tile(m, tuple(p for p in (512, 384, 256, 128) if p * k * 2 <= MM_OPERAND_BYTES // 2))
        grid = (n // tn, m // tm)
        ij = lambda o, i: (i, o)
    else:
        tm = _tile(m, tuple(p for p in (1408, 1024, 512, 384, 256, 128) if p * k * 2 <= MM_OPERAND_BYTES))
        tn = _tile(n, tuple(p for p in (512, 256, 128) if p * k * 2 <= MM_OPERAND_BYTES // 2))
        grid = (m // tm, n // tn)
        ij = lambda o, i: (o, i)

    extra = resid if resid is not None else relu_grad_of

    def body(*refs):
        a_ref, b_ref = refs[:2]
        if ta:
            at_scr = refs[-1]
            refs = refs[:-1]

            @pl.when(pl.program_id(1) == 0)
            def _():
                at_scr[...] = a_ref[...].T

            lhs = at_scr[...]
        else:
            lhs = a_ref[...]
        o = _dot(lhs, b_ref[...], NT if tb else NN)
        if resid is not None:
            o = o + refs[2][...]
        if relu_grad_of is not None:
            o = o * (2.0 * jnp.maximum(refs[2][...], 0.0))
        if relu2_out:
            refs[-2][...] = o.astype(out_dtype)
            r = jnp.maximum(o, 0.0)
            refs[-1][...] = (r * r).astype(bf16)
        else:
            refs[-1][...] = o.astype(out_dtype)

    in_specs = [pl.BlockSpec((k, tm), lambda o, i: (0, ij(o, i)[0])) if ta
                else pl.BlockSpec((tm, k), lambda o, i: (ij(o, i)[0], 0)),
                pl.BlockSpec((tn, k), lambda o, i: (ij(o, i)[1], 0)) if tb
                else pl.BlockSpec((k, tn), lambda o, i: (0, ij(o, i)[1]))]
    args = [a, b]
    if extra is not None:
        in_specs.append(pl.BlockSpec((tm, tn), ij))
        args.append(extra)
    out_blk = pl.BlockSpec((tm, tn), ij)
    out = jax.ShapeDtypeStruct((m, n), out_dtype)
    return pl.pallas_call(
        body, name=name, grid=grid, in_specs=in_specs,
        out_specs=(out_blk, out_blk) if relu2_out else out_blk,
        out_shape=(out, jax.ShapeDtypeStruct((m, n), bf16)) if relu2_out else out,
        scratch_shapes=[pltpu.VMEM((tm, k), bf16)] if ta else [],
        compiler_params=_params(("parallel", "arbitrary" if ta else "parallel")),
    )(*args)


def _rows(t_rows):
    return _tile(t_rows, (384, 256, 128))


def _rmsnorm(h, w):
    return h * lax.rsqrt(jnp.mean(h * h, axis=1, keepdims=True) + RMS_EPS) * w


def rmsnorm_fwd(h, w8, *, name):
    t_rows, d = h.shape
    tr = _rows(t_rows)

    def body(h_ref, w_ref, o_ref):
        o_ref[...] = _rmsnorm(h_ref[...], w_ref[0:1, :]).astype(bf16)

    blk = pl.BlockSpec((tr, d), lambda i: (i, 0))
    return pl.pallas_call(
        body, name=name, grid=(t_rows // tr,), in_specs=[blk, pl.BlockSpec((8, d), lambda i: (0, 0))], out_specs=blk,
        out_shape=jax.ShapeDtypeStruct((t_rows, d), bf16), compiler_params=_params(("arbitrary",)),
    )(h, w8)


def rmsnorm_bwd(h, w8, dhn, dres, *, name):
    t_rows, d = h.shape
    tr = _rows(t_rows)

    def body(h_ref, w_ref, dhn_ref, dres_ref, dh_ref, dw_ref):
        @pl.when(pl.program_id(0) == 0)
        def _():
            dw_ref[...] = jnp.zeros_like(dw_ref)

        _, vjp = jax.vjp(_rmsnorm, h_ref[...], w_ref[0:1, :])
        dh, dw = vjp(dhn_ref[...])
        dh_ref[...] = dh + dres_ref[...]
        dw_ref[0:1, :] += dw

    blk = pl.BlockSpec((tr, d), lambda i: (i, 0))
    wblk = pl.BlockSpec((8, d), lambda i: (0, 0))
    return pl.pallas_call(
        body, name=name, grid=(t_rows // tr,), in_specs=[blk, wblk, blk, blk], out_specs=(blk, wblk),
        out_shape=(jax.ShapeDtypeStruct((t_rows, d), f32), jax.ShapeDtypeStruct((8, d), f32)),
        compiler_params=_params(("arbitrary",)),
    )(h, w8, dhn, dres)


def _merge(pg, ps, pw, la, lb, lc):
    return jax.nn.sigmoid(la) * pg + jax.nn.sigmoid(lb) * ps + jax.nn.sigmoid(lc) * pw


def _merge_specs(t_rows):
    tr = _rows(t_rows)
    blk = pl.BlockSpec((tr, D_MODEL), lambda i: (i, 0))
    gate = [pl.BlockSpec((tr, D_MODEL), functools.partial(lambda i, j: (i, j), j=C_GATE // D_MODEL + j)) for j in range(3)]
    return tr, blk, gate


def merge_fwd(pg, ps, pw, u):
    t_rows = pg.shape[0]
    tr, blk, gate = _merge_specs(t_rows)

    def body(pg_ref, ps_ref, pw_ref, la, lb, lc, o_ref):
        o_ref[...] = _merge(pg_ref[...], ps_ref[...], pw_ref[...], la[...], lb[...], lc[...]).astype(bf16)

    return pl.pallas_call(
        body, name="merge_fwd", grid=(t_rows // tr,), in_specs=[blk, blk, blk] + gate, out_specs=blk,
        out_shape=jax.ShapeDtypeStruct((t_rows, D_MODEL), bf16), compiler_params=_params(("arbitrary",)),
    )(pg, ps, pw, u, u, u)


def merge_bwd(pg, ps, pw, u, dmerged, du):
    t_rows = pg.shape[0]
    tr, blk, gate = _merge_specs(t_rows)

    def body(pg_ref, ps_ref, pw_ref, la, lb, lc, dm_ref, _, dpg_ref, dps_ref, dpw_ref, dl_ref):
        _, vjp = jax.vjp(_merge, pg_ref[...], ps_ref[...], pw_ref[...], la[...], lb[...], lc[...])
        dpg, dps, dpw, dla, dlb, dlc = vjp(dm_ref[...])
        dpg_ref[...] = dpg.astype(bf16)
        dps_ref[...] = dps.astype(bf16)
        dpw_ref[...] = dpw.astype(bf16)
        for j, dl in enumerate((dla, dlb, dlc)):
            dl_ref[:, j * D_MODEL:(j + 1) * D_MODEL] = dl.astype(bf16)

    act = jax.ShapeDtypeStruct((t_rows, D_MODEL), bf16)
    return pl.pallas_call(
        body, name="merge_bwd", grid=(t_rows // tr,), in_specs=[blk, blk, blk] + gate + [blk, ANY],
        out_specs=(blk, blk, blk, pl.BlockSpec((tr, 3 * D_MODEL), lambda i: (i, C_GATE // (3 * D_MODEL)))),
        out_shape=(act, act, act, jax.ShapeDtypeStruct(du.shape, du.dtype)),
        input_output_aliases={7: 3},
        compiler_params=_params(("arbitrary",)),
    )(pg, ps, pw, u, u, u, dmerged, du)


def relu2_fwd(a):
    t_rows, d = a.shape
    tr = _rows(t_rows)

    def body(a_ref, o_ref):
        r = jnp.maximum(a_ref[...], 0.0)
        o_ref[...] = (r * r).astype(bf16)

    blk = pl.BlockSpec((tr, d), lambda i: (i, 0))
    return pl.pallas_call(
        body, name="relu2_fwd", grid=(t_rows // tr,), in_specs=[blk], out_specs=blk,
        out_shape=jax.ShapeDtypeStruct((t_rows, d), bf16), compiler_params=_params(("arbitrary",)),
    )(a)


def relu2_bwd(a, dr):
    t_rows, d = a.shape
    tr = _rows(t_rows)

    def body(a_ref, dr_ref, o_ref):
        o_ref[...] = (dr_ref[...] * 2.0 * jnp.maximum(a_ref[...], 0.0)).astype(bf16)

    blk = pl.BlockSpec((tr, d), lambda i: (i, 0))
    return pl.pallas_call(
        body, name="relu2_bwd", grid=(t_rows // tr,), in_specs=[blk, blk], out_specs=blk,
        out_shape=jax.ShapeDtypeStruct((t_rows, d), bf16), compiler_params=_params(("arbitrary",)),
    )(a, dr)


def loss_head(h, w8, target):
    t_rows, d = h.shape
    tr = HEAD_ROWS

    def loss_fn(hb, w, tgt):
        err = _rmsnorm(hb, w) - tgt
        return 0.5 * jnp.sum(err * err) / d

    def body(h_ref, w_ref, t_ref, loss_ref, dh_ref, dw_ref):
        i = pl.program_id(0)

        @pl.when(i == 0)
        def _():
            loss_ref[...] = jnp.zeros_like(loss_ref)
            dw_ref[...] = jnp.zeros_like(dw_ref)
            dh_ref[...] = jnp.zeros_like(dh_ref)

        @pl.when(i > 0)
        def _():
            val, (dh, dw) = jax.value_and_grad(loss_fn, argnums=(0, 1))(h_ref[...], w_ref[0:1, :], t_ref[...])
            loss_ref[...] += val
            dh_ref[...] = dh
            dw_ref[0:1, :] += dw

    blk = pl.BlockSpec((tr, d), lambda i: (i, 0))
    wblk = pl.BlockSpec((8, d), lambda i: (0, 0))
    return pl.pallas_call(
        body, name="loss_head", grid=(t_rows // tr,),
        in_specs=[blk, wblk, pl.BlockSpec((tr, d), lambda i: (jnp.maximum(i - 1, 0), 0))],
        out_specs=(pl.BlockSpec((8, 128), lambda i: (0, 0)), blk, wblk),
        out_shape=(jax.ShapeDtypeStruct((8, 128), f32), jax.ShapeDtypeStruct((t_rows, d), f32),
                   jax.ShapeDtypeStruct((8, d), f32)),
        compiler_params=_params(("arbitrary",)),
    )(h, w8, target)


def adamw(w, m, v, partials, row_off, *, name):
    rows, d = w.shape
    layers = len(partials)
    per = rows // layers
    tr = _tile(per, tuple(p for p in (512, 256, 128, 64, 16, 8) if p * d * 4 <= BLOCK_BYTES))
    assert row_off % tr == 0
    off, nblk = row_off // tr, per // tr
    c1 = 1.0 - ADAM_B1 ** ADAM_STEP
    c2 = 1.0 - ADAM_B2 ** ADAM_STEP

    def body(w_ref, m_ref, v_ref, *refs):
        p_refs, (g_ref, d_ref, mo_ref, vo_ref) = refs[:2 * layers], refs[2 * layers:]
        g = p_refs[0][...] + p_refs[1][...]
        for l in range(1, layers):
            g = jnp.where(pl.program_id(0) >= l * nblk, p_refs[2 * l][...] + p_refs[2 * l + 1][...], g)
        m_new = ADAM_B1 * m_ref[...] + (1.0 - ADAM_B1) * g
        v_new = ADAM_B2 * v_ref[...] + (1.0 - ADAM_B2) * (g * g)
        g_ref[...] = g
        d_ref[...] = -ADAM_LR * ((m_new / c1) / (jnp.sqrt(v_new / c2) + ADAM_EPS) + ADAM_WD * w_ref[...])
        mo_ref[...] = m_new
        vo_ref[...] = v_new

    blk = pl.BlockSpec((tr, d), lambda i: (i, 0))
    pblks = [pl.BlockSpec((tr, d), functools.partial(lambda i, l: (off + jnp.clip(i - l * nblk, 0, nblk - 1), 0), l=l))
             for l in range(layers) for _ in range(2)]
    out = jax.ShapeDtypeStruct((rows, d), f32)
    return pl.pallas_call(
        body, name=name, grid=(rows // tr,), in_specs=[blk, blk, blk] + pblks, out_specs=(blk,) * 4,
        out_shape=(out,) * 4, compiler_params=_params(("arbitrary",)),
    )(w, m, v, *[p for pair in partials for p in pair])


def reduce4(parts, *, name, own=None, me=None):
    _, rows, d = parts.shape
    tr = _tile(rows, tuple(p for p in (512, 256, 128, 64, 8) if p * d * 4 <= BLOCK_BYTES))

    def body(*refs):
        p_ref, o_ref = refs[0], refs[-1]
        acc = None
        for s in range(4):
            term = p_ref[s].astype(f32)
            if own is not None:
                term = jnp.where(refs[2][0] == s, refs[1][...].astype(f32), term)
            acc = term if acc is None else acc + term
        o_ref[...] = acc

    in_specs = [pl.BlockSpec((4, tr, d), lambda i: (0, i, 0))]
    args = [parts]
    if own is not None:
        in_specs += [pl.BlockSpec((tr, d), lambda i: (i, 0)), pl.BlockSpec(memory_space=pltpu.SMEM)]
        args += [own, me]
    return pl.pallas_call(
        body, name=name, grid=(rows // tr,), in_specs=in_specs,
        out_specs=pl.BlockSpec((tr, d), lambda i: (i, 0)), out_shape=jax.ShapeDtypeStruct((rows, d), f32),
        compiler_params=_params(("arbitrary",)),
    )(*args)


ANY = pl.BlockSpec(memory_space=pl.ANY)
MESH = pl.DeviceIdType.MESH
CHIP_FLIPS = ((0, 1), (1, 0), (1, 1))


def chip_exchange(bufs, scatter, *, name, after=None):
    nb = len(bufs)
    extra = [] if after is None else [after]

    def body(*refs):
        ins, outs = refs[:nb], refs[nb + len(extra):2 * nb + len(extra)]
        send_sems, recv_sems, local_sems = refs[2 * nb + len(extra):]
        x, y, c = lax.axis_index("x"), lax.axis_index("y"), lax.axis_index("c")
        me = 2 * x + y
        local = [pltpu.make_async_copy(ins[j].at[me] if scatter[j] else ins[j], outs[j].at[me], local_sems.at[j])
                 for j in range(nb)]
        for cp in local:
            cp.start()
        sends, recvs = [], []
        for k, (fx, fy) in enumerate(CHIP_FLIPS):
            px = 1 - x if fx else x
            py = 1 - y if fy else y
            chip = 2 * px + py
            for j in range(nb):
                src = ins[j].at[chip] if scatter[j] else ins[j]
                sems = dict(send_sem=send_sems.at[nb * k + j], recv_sem=recv_sems.at[nb * k + j],
                            device_id=(px, py, c), device_id_type=MESH)
                sends.append(pltpu.make_async_remote_copy(src_ref=src, dst_ref=outs[j].at[me], **sems))
                recvs.append(pltpu.make_async_remote_copy(src_ref=src, dst_ref=outs[j].at[chip], **sems))
        for cp in sends:
            cp.start()
        for cp in recvs:
            cp.wait_recv()
        for cp in sends:
            cp.wait_send()
        for cp in local:
            cp.wait()

    out_shape = tuple(jax.ShapeDtypeStruct(b.shape if s else (4,) + b.shape, b.dtype) for b, s in zip(bufs, scatter))
    return pl.pallas_call(
        body, name=name, in_specs=[ANY] * (nb + len(extra)), out_specs=(ANY,) * nb, out_shape=out_shape,
        scratch_shapes=[pltpu.SemaphoreType.DMA((3 * nb,)), pltpu.SemaphoreType.DMA((3 * nb,)),
                        pltpu.SemaphoreType.DMA((nb,))],
        compiler_params=pltpu.CompilerParams(has_side_effects=True),
    )(*bufs, *extra)


def sibling_swap(bufs, *, name):
    nb = len(bufs)

    def body(*refs):
        ins, outs, (send_sems, recv_sems) = refs[:nb], refs[nb:2 * nb], refs[2 * nb:]
        peer = (lax.axis_index("x"), lax.axis_index("y"), 1 - lax.axis_index("c"))
        copies = [pltpu.make_async_remote_copy(src_ref=ins[j], dst_ref=outs[j], send_sem=send_sems.at[j],
                                               recv_sem=recv_sems.at[j], device_id=peer, device_id_type=MESH)
                  for j in range(nb)]
        for cp in copies:
            cp.start()
        for cp in copies:
            cp.wait_recv()
        for cp in copies:
            cp.wait_send()

    return pl.pallas_call(
        body, name=name, in_specs=[ANY] * nb, out_specs=(ANY,) * nb,
        out_shape=tuple(jax.ShapeDtypeStruct(b.shape, b.dtype) for b in bufs),
        scratch_shapes=[pltpu.SemaphoreType.DMA((nb,)), pltpu.SemaphoreType.DMA((nb,))],
        compiler_params=pltpu.CompilerParams(has_side_effects=True),
    )(*bufs)


HBM = pl.BlockSpec(memory_space=pltpu.HBM)
SEM = pl.BlockSpec(memory_space=pltpu.SEMAPHORE)
DATAFLOW = pltpu.SideEffectType.DATAFLOW_SIDE_EFFECTING


def _exchange_copies(srcs, lands, send_sems, recv_sems, scatter):
    x, y, c = lax.axis_index("x"), lax.axis_index("y"), lax.axis_index("c")
    me = 2 * x + y
    nb = len(srcs)
    pairs = []
    for k, (fx, fy) in enumerate(CHIP_FLIPS):
        px = 1 - x if fx else x
        py = 1 - y if fy else y
        chip = 2 * px + py
        for j in range(nb):
            src = srcs[j].at[chip] if scatter[j] else srcs[j]
            sems = dict(send_sem=send_sems.at[nb * k + j], recv_sem=recv_sems.at[nb * k + j],
                        device_id=(px, py, c), device_id_type=MESH)
            pairs.append((pltpu.make_async_remote_copy(src_ref=src, dst_ref=lands[j].at[me], **sems),
                          pltpu.make_async_remote_copy(src_ref=src, dst_ref=lands[j].at[chip], **sems)))
    return pairs


def exchange_start(bufs, scatter, after, *, name):
    nb = len(bufs)
    slabs = [b.shape[1:] if s else b.shape for b, s in zip(bufs, scatter)]
    lands = [lax.empty((4,) + shp, b.dtype) for b, shp in zip(bufs, slabs)]

    def body(*refs):
        srcs, zones = refs[:nb], refs[nb:2 * nb]
        send_sems, recv_sems = refs[2 * nb + 1:2 * nb + 3]
        token = refs[-1]
        for send, _ in _exchange_copies(srcs, zones, send_sems, recv_sems, scatter):
            send.start()
        token[...] = jnp.zeros_like(token)

    hbm = lambda a: pltpu.with_memory_space_constraint(a, pltpu.HBM)
    out = pl.pallas_call(
        body, name=name, in_specs=[HBM] * (2 * nb) + [ANY],
        out_specs=(SEM, SEM) + (HBM,) * (2 * nb) + (pl.BlockSpec(memory_space=pltpu.VMEM),),
        out_shape=(pltpu.SemaphoreType.DMA((3 * nb,)), pltpu.SemaphoreType.DMA((3 * nb,)))
        + tuple(pltpu.HBM(a.shape, a.dtype) for a in list(bufs) + lands) + (jax.ShapeDtypeStruct((8, 128), f32),),
        input_output_aliases={i: 2 + i for i in range(2 * nb)},
        compiler_params=pltpu.CompilerParams(has_side_effects=DATAFLOW),
    )(*[hbm(a) for a in list(bufs) + lands], after)
    return (out[:2], out[2:2 + nb], out[2 + nb:2 + 2 * nb], scatter), out[-1]


def exchange_wait(state, after, *, name):
    (send_sems, recv_sems), srcs, lands, scatter = state
    nb = len(srcs)

    def body(*refs):
        src_refs, zones = refs[:nb], refs[nb:2 * nb]
        s_sems, r_sems = refs[2 * nb:2 * nb + 2]
        for send, recv in _exchange_copies(src_refs, zones, s_sems, r_sems, scatter):
            send.wait_send()
            recv.wait_recv()

    out = pl.pallas_call(
        body, name=name, in_specs=[HBM] * (2 * nb) + [SEM, SEM, ANY], out_specs=(HBM,) * (2 * nb),
        out_shape=tuple(pltpu.HBM(a.shape, a.dtype) for a in list(srcs) + list(lands)),
        input_output_aliases={i: i for i in range(2 * nb)},
        compiler_params=pltpu.CompilerParams(has_side_effects=DATAFLOW),
    )(*srcs, *lands, send_sems, recv_sems, after)
    return out[nb:]


BIG = (
    ("w_proj_gdn", 256), ("w_proj_ssd", 256), ("w_proj_swa", 256), ("w_out", 256), ("w_up", 1024), ("w_down", 1024))
BIG_OFF = {}
_o = 0
for _n, _r in BIG:
    BIG_OFF[_n] = _o
    _o += _r
BIG_ROWS = _o
W_IN_SHARD = IN_W // 4

W_NAMES = ('meta_tokens', 'norm1_w', 'w_in', 'gdn_conv_w', 'gdn_a_log', 'gdn_dt_bias', 'gdn_norm_w', 'ssd_conv_w',
           'ssd_conv_b', 'ssd_dt_bias', 'ssd_a_log', 'ssd_d', 'ssd_norm_w', 'swa_sinks', 'w_proj_gdn', 'w_proj_ssd',
           'w_proj_swa', 'w_out', 'norm2_w', 'w_up', 'w_down', 'final_norm_w')
SMALL_NAMES = tuple(n for n in W_NAMES if n not in BIG_OFF and n != "w_in")
SMALL_SHARDED = ("meta_tokens", "gdn_conv_w", "ssd_conv_w")


def _pad_rows(a, rows):
    return jnp.pad(a, ((0, rows - a.shape[0]), (0, 0)))


def _pack_rows(parts, dtype):
    flat = jnp.concatenate([p.reshape(-1).astype(dtype) for p in parts])
    n = -(-flat.shape[0] // 8192) * 8192
    return jnp.pad(flat, (0, n - flat.shape[0])).reshape(-1, D_MODEL)


def _unpack_rows(packed, shapes):
    flat, out, o = packed.reshape(-1), [], 0
    for s in shapes:
        n = 1
        for d in s:
            n *= d
        out.append(flat[o:o + n].reshape(s))
        o += n
    return out


def _split_chips(full, axis):
    s = full.shape
    a = full.reshape(s[:axis] + (4, s[axis] // 4) + s[axis + 1:])
    return jnp.moveaxis(a, axis, 0)


def _join_chips(parts, axis):
    a = jnp.moveaxis(parts, 0, axis)
    s = a.shape
    return a.reshape(s[:axis] + (s[axis] * s[axis + 1],) + s[axis + 2:])


BIG_AXIS = {"w_in": 2, "w_proj_gdn": 1, "w_proj_ssd": 1, "w_proj_swa": 1, "w_out": 1, "w_up": 2, "w_down": 1}


def _w_in_to_padded(w):
    z = lambda n: jnp.zeros(w.shape[:-1] + (n,), w.dtype)
    return jnp.concatenate([w[..., 8736:11808], w[..., 4112:7184], w[..., 7200:8736], w[..., 4096:4112], z(112),
                            w[..., 7184:7200], z(112 + C_MID_END - C_SDT - 128), w[..., 0:4096]], axis=-1)


def _w_in_from_padded(p):
    return jnp.concatenate([p[..., C_GQ:IN_WP], p[..., C_BA:C_BA + 16], p[..., C_SZ:C_WQ], p[..., C_SDT:C_SDT + 16],
                            p[..., C_WQ:C_BA], p[..., 0:C_SZ]], axis=-1)


def _row8(v, lane0=0, width=128):
    return jnp.pad(v[None, :], ((0, 7), (lane0, width - lane0 - v.shape[0])))


def _head_major(a, heads):
    return a.reshape(a.shape[0], heads, SWA_D).transpose(1, 0, 2)


def _from_head_major(a):
    return a.transpose(1, 0, 2).reshape(a.shape[1], -1)


def _layer_fwd(h, p, l, late=None):
    tag = f"l{l}"
    hn = rmsnorm_fwd(h, p["n1"], name=f"norm1_fwd_{tag}")
    u = mm(hn, p["w_in"], out_dtype=f32, name=f"mm_in_{tag}")
    yg, stg = gdn_fwd(u, p["gcw"], p["galog"], p["gdtb"], p["gnw"])
    ys, sts = ssd_fwd(u, p["scw"], p["sdtb"], p["salog"], p["sd"], p["snw"])
    qh = _head_major(u[:, C_WQ:C_WK], SWA_Q_HEADS)
    kh = _head_major(u[:, C_WK:C_WV], SWA_KV_HEADS)
    vh = _head_major(u[:, C_WV:C_BA], SWA_KV_HEADS)
    yw = _from_head_major(swa_fwd(qh, kh, vh, p["sink"]))
    if late is not None:
        p.update(late(yw))
    pg = mm(yg, p["wpg"], out_dtype=f32, name=f"mm_pg_{tag}")
    ps = mm(ys, p["wps"], out_dtype=f32, name=f"mm_ps_{tag}")
    pw = mm(yw, p["wpw"], out_dtype=f32, name=f"mm_pw_{tag}")
    merged = merge_fwd(pg, ps, pw, u)
    h2 = mm(merged, p["wout"], out_dtype=f32, resid=h, name=f"mm_out_{tag}")
    hn2 = rmsnorm_fwd(h2, p["n2"], name=f"norm2_fwd_{tag}")
    a, r = mm(hn2, p["wup"], out_dtype=f32, relu2_out=True, name=f"mm_up_{tag}")
    h3 = mm(r, p["wdown"], out_dtype=f32, resid=h2, name=f"mm_down_{tag}")
    saved = dict(h=h, hn=hn, u=u, yg=yg, stg=stg, ys=ys, sts=sts, qh=qh, kh=kh, vh=vh, yw=yw, pg=pg, ps=ps, pw=pw,
                 merged=merged, h2=h2, hn2=hn2, a=a, r=r)
    return h3, saved


def _layer_bwd(dh3, p, s, l, send_big, send_w_in):
    tag = f"l{l}"
    g = {}

    def wgrad(act, d, name):
        return mm(act, d, ta=True, out_dtype=bf16, name=f"wg_{name}_{tag}")

    da = mm(dh3, p["wdown"], tb=True, out_dtype=bf16, relu_grad_of=s["a"], name=f"dg_down_{tag}")
    g["w_down"] = wgrad(s["r"], dh3, "down")
    dhn2 = mm(da, p["wup"], tb=True, out_dtype=f32, name=f"dg_up_{tag}")
    g["w_up"] = wgrad(s["hn2"], da, "up")
    dh2, g["norm2_w"] = rmsnorm_bwd(s["h2"], p["n2"], dhn2, dh3, name=f"norm2_bwd_{tag}")
    dmerged = mm(dh2, p["wout"], tb=True, out_dtype=f32, name=f"dg_out_{tag}")
    g["w_out"] = wgrad(s["merged"], dh2, "out")
    du = lax.empty((dh3.shape[0], IN_WP), bf16)
    dpg, dps, dpw, du = merge_bwd(s["pg"], s["ps"], s["pw"], s["u"], dmerged, du)
    dyg = mm(dpg, p["wpg"], tb=True, out_dtype=f32, name=f"dg_pg_{tag}")
    dys = mm(dps, p["wps"], tb=True, out_dtype=f32, name=f"dg_ps_{tag}")
    dyw = mm(dpw, p["wpw"], tb=True, out_dtype=f32, name=f"dg_pw_{tag}")
    g["w_proj_gdn"] = wgrad(s["yg"], dpg, "pg")
    g["w_proj_ssd"] = wgrad(s["ys"], dps, "ps")
    g["w_proj_swa"] = wgrad(s["yw"], dpw, "pw")
    sent = send_big(jnp.concatenate([_split_chips(g.pop(n), BIG_AXIS[n] - 1).reshape(4, r, D_MODEL)
                                     for n, r in BIG], axis=1))

    (du, dba, dtq, dtk, dtv, g["gdn_a_log"], g["gdn_dt_bias"], g["gdn_norm_w"]) = gdn_bwd(
        s["u"], p["gcw"] + sent, p["galog"], p["gdtb"], p["gnw"], s["stg"], dyg, du)
    g["gdn_conv_w"] = jnp.concatenate([dtq, dtk, dtv], axis=1)[:4]
    (du, ddt, dtx, dtb, dtc, g["ssd_dt_bias"], g["ssd_a_log"], g["ssd_d"], g["ssd_norm_w"]) = ssd_bwd(
        s["u"], p["scw"], p["sdtb"], p["salog"], p["sd"], p["snw"], s["sts"], dys, du)
    dconv = jnp.concatenate([dtx, dtb, dtc], axis=1)
    g["ssd_conv_w"], g["ssd_conv_b"] = dconv[:4], dconv[4]
    dqh, dkh, dvh, g["swa_sinks"] = swa_bwd(s["qh"], s["kh"], s["vh"], p["sink"], _head_major(dyw, SWA_Q_HEADS))
    mid = jnp.concatenate([_from_head_major(dqh), _from_head_major(dkh), _from_head_major(dvh), dba[0].astype(bf16),
                           ddt.astype(bf16), jnp.zeros((du.shape[0], C_MID_END - C_SDT - 128), bf16)], axis=1)
    du = lax.dynamic_update_slice(du, mid, (0, C_WQ))
    sent = send_w_in(_split_chips(_w_in_from_padded(wgrad(s["hn"], du, "in")), 1))
    dhn = mm(du, p["w_in_t"], out_dtype=f32, name=f"dg_in_{tag}")
    dh, g["norm1_w"] = rmsnorm_bwd(s["h"], p["n1"] + sent, dhn, dh2, name=f"norm1_bwd_{tag}")
    return dh, g


def kernel(x, meta_tokens, norm1_w, w_in, gdn_conv_w, gdn_a_log, gdn_dt_bias, gdn_norm_w, ssd_conv_w, ssd_conv_b, ssd_dt_bias, ssd_a_log, ssd_d, ssd_norm_w, swa_sinks, w_proj_gdn, w_proj_ssd, w_proj_swa, w_out, norm2_w, w_up, w_down, final_norm_w, loss_target, m_meta_tokens, m_norm1_w, m_w_in, m_gdn_conv_w, m_gdn_a_log, m_gdn_dt_bias, m_gdn_norm_w, m_ssd_conv_w, m_ssd_conv_b, m_ssd_dt_bias, m_ssd_a_log, m_ssd_d, m_ssd_norm_w, m_swa_sinks, m_w_proj_gdn, m_w_proj_ssd, m_w_proj_swa, m_w_out, m_norm2_w, m_w_up, m_w_down, m_final_norm_w, v_meta_tokens, v_norm1_w, v_w_in, v_gdn_conv_w, v_gdn_a_log, v_gdn_dt_bias, v_gdn_norm_w, v_ssd_conv_w, v_ssd_conv_b, v_ssd_dt_bias, v_ssd_a_log, v_ssd_d, v_ssd_norm_w, v_swa_sinks, v_w_proj_gdn, v_w_proj_ssd, v_w_proj_swa, v_w_out, v_norm2_w, v_w_up, v_w_down, v_final_norm_w):
    given = dict(locals())
    depth = norm1_w.shape[0]
    me = 2 * lax.axis_index("x") + lax.axis_index("y")

    me1 = jnp.reshape(me, (1,)).astype(jnp.int32)
    is_me = (jnp.arange(4, dtype=jnp.int32) == me)[:, None, None]

    def weight_slabs(l):
        return (w_in[l].astype(bf16),
                jnp.concatenate([given[n][l].reshape(-1, D_MODEL).astype(bf16) for n, _ in BIG]))

    slabs = [weight_slabs(l) for l in range(depth)]
    wsmall = _pack_rows([given[n] for n in SMALL_SHARDED], f32)
    ga0, gsmall = chip_exchange([slabs[0][0], wsmall], (False, False), name="gather_first")
    gathers, started = {}, jnp.zeros((), f32)
    for l in range(depth):
        for j in range(2):
            if (l, j) != (0, 0):
                gathers[l, j], token = exchange_start([slabs[l][j]], (False,), gsmall, name=f"gather_start_l{l}_{j}")
                started = started + token[0, 0]
    shard_shapes = [given[n].shape for n in SMALL_SHARDED]
    per_chip = [_unpack_rows(gsmall[s], shard_shapes) for s in range(4)]
    full = {n: jnp.concatenate([per_chip[s][i] for s in range(4)], axis=-1) for i, n in enumerate(SMALL_SHARDED)}

    def landed(l, j, after):
        (zone,) = exchange_wait(gathers[l, j], after, name=f"gather_wait_l{l}_{j}")
        return jnp.where(is_me, slabs[l][j][None], zone)

    def first_operands(l, ga, order):
        w_in_p = _w_in_to_padded(_join_chips(ga, 1))
        return dict(
            n1=_row8(norm1_w[l], width=D_MODEL) + order, n2=_row8(norm2_w[l], width=D_MODEL),
            w_in=w_in_p, w_in_t=w_in_p.T,
            gcw=jnp.pad(full["gdn_conv_w"][l], ((0, 4), (0, 0))),
            galog=_row8(gdn_a_log[l], 8), gdtb=_row8(gdn_dt_bias[l], 8), gnw=_row8(gdn_norm_w[l]),
            scw=jnp.pad(jnp.concatenate([full["ssd_conv_w"][l], ssd_conv_b[l][None]], axis=0), ((0, 3), (0, 0))),
            sdtb=_row8(ssd_dt_bias[l]), salog=_row8(ssd_a_log[l]), sd=_row8(ssd_d[l]),
            snw=_row8(ssd_norm_w[l], width=D_MODEL), sink=_row8(swa_sinks[l]))

    def late_operands(l, after):
        gb = landed(l, 1, after)
        w = {}
        for n, r in BIG:
            parts = gb[:, BIG_OFF[n]:BIG_OFF[n] + r].reshape((4,) + given[n].shape[1:])
            w[n] = _join_chips(parts, BIG_AXIS[n] - 1)
        return dict(wpg=w["w_proj_gdn"], wps=w["w_proj_ssd"], wpw=w["w_proj_swa"], wout=w["w_out"],
                    wup=w["w_up"], wdown=w["w_down"])

    h = jnp.concatenate([jnp.zeros((PAD, D_MODEL), f32), full["meta_tokens"], x[0]], axis=0)
    layers, saved = [], []
    for l in range(depth):
        p = first_operands(0, ga0, started) if l == 0 else first_operands(l, landed(l, 0, h), 0.0)
        h, s = _layer_fwd(h, p, l, late=functools.partial(late_operands, l))
        layers.append(p)
        saved.append(s)
    loss8, dh, dfw8 = loss_head(h, _row8(final_norm_w, width=D_MODEL), loss_target[0])
    grads = {"final_norm_w": dfw8[0]}
    per_layer, grad_slabs, scatters = [None] * depth, {}, {}

    def send(l, j, slab):
        grad_slabs[l, j] = slab
        scatters[l, j], token = exchange_start([slab], (True,), loss8, name=f"scatter_start_l{l}_{j}")
        return token[0, 0]

    for l in reversed(range(depth)):
        dh, per_layer[l] = _layer_bwd(dh, layers[l], saved[l], l, functools.partial(send, l, 1),
                                      functools.partial(send, l, 0))
    grad_x = dh[HEAD_ROWS:][None]
    grads["meta_tokens"] = dh[PAD:HEAD_ROWS]
    lane = {"gdn_a_log": (8, 8), "gdn_dt_bias": (8, 8), "gdn_norm_w": (0, 128), "ssd_dt_bias": (0, 16),
            "ssd_a_log": (0, 16), "ssd_d": (0, 16), "swa_sinks": (0, 16)}
    for n in per_layer[0]:
        parts = [per_layer[l][n] for l in range(depth)]
        if n in lane:
            parts = [q[0, lane[n][0]:lane[n][0] + lane[n][1]] for q in parts]
        elif n in ("norm1_w", "norm2_w", "ssd_norm_w"):
            parts = [q[0] for q in parts]
        grads[n] = jnp.stack(parts)
    loss = lax.psum(loss8[0, 0], ("x", "y", "c"))

    gs = _pack_rows([grads[n] for n in SMALL_NAMES], f32)
    def chip_sum(l, j, after):
        (zone,) = exchange_wait(scatters[l, j], after, name=f"scatter_wait_l{l}_{j}")
        own = lax.dynamic_index_in_dim(grad_slabs[l, j], me, 0, keepdims=False)
        return reduce4(zone, own=own, me=me1, name=f"sum_chips_l{l}_{j}")

    early = [(l, j) for l in range(depth) for j in range(2) if (l, j) != (0, 0)]
    mine = {lj: chip_sum(*lj, dh) for lj in early}
    sibs = dict(zip(early, sibling_swap([mine[lj] for lj in early], name="swap_cores_early")))
    out = {}
    for n, r in BIG:
        shp = given[n].shape
        res = adamw(*[given[pre + n].reshape(depth * r, D_MODEL) for pre in ("", "m_", "v_")],
                    [(mine[l, 1], sibs[l, 1]) for l in range(depth)], BIG_OFF[n], name=f"adamw_{n}")
        out[n] = [a.reshape(shp) for a in res]
    mine[0, 0] = chip_sum(0, 0, res[1])
    (rs,) = chip_exchange([gs], (False,), after=mine[0, 0], name="gather_small_grads")
    ps_ = reduce4(rs, name="sum_chips_small")
    sibs[0, 0], ss = sibling_swap([mine[0, 0], ps_], name="swap_cores_last")
    w_in_rows = depth * D_MODEL
    res = adamw(*[given[pre + "w_in"].reshape(w_in_rows, W_IN_SHARD) for pre in ("", "m_", "v_")],
                [(mine[l, 0], sibs[l, 0]) for l in range(depth)], 0, name="adamw_w_in")
    out["w_in"] = [a.reshape(w_in.shape) for a in res]
    full_shapes = [grads[n].shape for n in SMALL_NAMES]
    mine_s, sib_s = _unpack_rows(ps_, full_shapes), _unpack_rows(ss, full_shapes)

    def local(parts):
        loc = []
        for n, a in zip(SMALL_NAMES, parts):
            if n in SMALL_SHARDED:
                sz = a.shape[-1] // 4
                a = lax.dynamic_slice_in_dim(a, me * sz, sz, axis=a.ndim - 1)
            loc.append(a)
        return _pack_rows(loc, f32)

    res = adamw(_pack_rows([given[n] for n in SMALL_NAMES], f32), _pack_rows([given["m_" + n] for n in SMALL_NAMES], f32),
                _pack_rows([given["v_" + n] for n in SMALL_NAMES], f32), [(local(mine_s), local(sib_s))], 0,
                name="adamw_small")
    local_shapes = [given[n].shape for n in SMALL_NAMES]
    unpacked = [_unpack_rows(a, local_shapes) for a in res]
    for i, n in enumerate(SMALL_NAMES):
        out[n] = [unpacked[j][i] for j in range(4)]

    return (loss, grad_x) + tuple(out[n][j] for j in range(4) for n in W_NAMES)
```

```python
import functools

import jax
import jax.numpy as jnp
from jax import lax
from jax.experimental import pallas as pl
from jax.experimental.pallas import tpu as pltpu

f32 = jnp.float32
bf16 = jnp.bfloat16
HI = lax.Precision.HIGHEST

D_MODEL = 1024
N_META = 16
PAD = 112
HEAD_ROWS = PAD + N_META
RMS_EPS = 1e-6
L2_EPS = 1e-6
D_FF = 4 * D_MODEL

GDN_HEADS = 8
GDN_D = 128
GDN_CHUNK = 64
SSD_HEADS = 16
SSD_P = 64
SSD_GROUPS = 4
SSD_HPG = 4
SSD_N = 128
SSD_CHUNK = 128
SWA_Q_HEADS = 16
SWA_KV_HEADS = 4
SWA_REP = 4
SWA_D = 64
SWA_W = 128

C_GATE = 0
C_SZ, C_SX, C_SB, C_SC = 3072, 4096, 5120, 5632
C_WQ, C_WK, C_WV = 6144, 7168, 7424
C_BA = 7680
C_SDT = 7808
C_MID_END = 8192
C_GQ, C_GK, C_GV, C_GG = 8192, 9216, 10240, 11264
IN_WP = 12288
IN_W = 11808

ADAM_LR, ADAM_B1, ADAM_B2, ADAM_EPS, ADAM_WD, ADAM_STEP = 0.001, 0.9, 0.999, 1e-08, 0.01, 10

VMEM_LIMIT = 56 * 1024 * 1024
BLOCK_BYTES = 3 << 19
MM_OPERAND_BYTES = 9 << 20
MM_RESIDENT_BYTES = 13 << 20

NN = (((1,), (0,)), ((), ()))
NT = (((1,), (1,)), ((), ()))
TN = (((0,), (0,)), ((), ()))


def _dot(a, b, dims=NN):
    return lax.dot_general(a.astype(bf16), b.astype(bf16), dims, preferred_element_type=f32)


def _dotx(a, b, dims=NN):
    return lax.dot_general(a, b, dims, preferred_element_type=f32, precision=lax.Precision.HIGH)


def _iota(shape, axis):
    return lax.broadcasted_iota(jnp.int32, shape, axis)


def _softplus(x):
    return jnp.maximum(x, 0.0) + jnp.log1p(jnp.exp(-jnp.abs(x)))


def _silu(x):
    return x * jax.nn.sigmoid(x)


def _params(sem):
    return pltpu.CompilerParams(dimension_semantics=sem, vmem_limit_bytes=VMEM_LIMIT)


@functools.partial(jax.custom_vjp, nondiff_argnums=(1,))
def _window(x_ext, off):
    n = x_ext.shape[0] - 8
    if off == 8:
        return x_ext[8:]
    return pltpu.roll(x_ext, 8 - off, 0)[8:]


def _window_fwd(x_ext, off):
    return _window(x_ext, off), None


def _window_bwd(off, _, g):
    n, w = g.shape
    g_ext = jnp.concatenate([jnp.zeros((8, w), g.dtype), g], axis=0)
    if off == 8:
        return (g_ext,)
    return (pltpu.roll(g_ext, n + off, 0),)


_window.defvjp(_window_fwd, _window_bwd)


def _conv4(x, halo, taps):
    x_ext = jnp.concatenate([halo, x], axis=0)
    y = taps[3] * x
    for j in range(3):
        y = y + taps[j] * _window(x_ext, 5 + j)
    return y


def _blockinv_impl(a):
    n = a.shape[0]
    ri, ci = _iota((n, n), 0), _iota((n, n), 1)
    t = (ri == ci).astype(f32)
    k = 0
    while (1 << k) < n:
        sel = ((ri >> (k + 1)) == (ci >> (k + 1))) & (((ri >> k) & 1) == 1) & (((ci >> k) & 1) == 0)
        o = jnp.where(sel, a, 0.0)
        t = t - _dotx(_dotx(t, o), t)
        k += 1
    return t


@jax.custom_vjp
def _blockinv(a):
    return _blockinv_impl(a)


def _blockinv_fwd(a):
    t = _blockinv_impl(a)
    return t, t


def _blockinv_bwd(t, dt):
    return (-_dotx(_dotx(t, dt, TN), t, NT),)


_blockinv.defvjp(_blockinv_fwd, _blockinv_bwd)


@jax.custom_vjp
def _blockinv_given(a, t):
    return t


_blockinv_given.defvjp(lambda a, t: (t, t), lambda t, dt: _blockinv_bwd(t, dt) + (jnp.zeros_like(t),))


def _scan_rows(x, reverse):
    n = x.shape[0]
    row = _iota(x.shape, 0)
    s = 1
    while s < n:
        if reverse:
            x = x + jnp.where(row < n - s, pltpu.roll(x, n - s, 0), 0.0)
        else:
            x = x + jnp.where(row >= s, pltpu.roll(x, s, 0), 0.0)
        s *= 2
    return x


@jax.custom_vjp
def _cumsum_rows(x):
    return _scan_rows(x, False)


_cumsum_rows.defvjp(lambda x: (_scan_rows(x, False), None), lambda _, g: (_scan_rows(g, True),))


def _gdn_act(xq, xk, xv, hq, hk, hv, tq, tk, tv):
    return _silu(_conv4(xq, hq, tq)), _silu(_conv4(xk, hk, tk)), _silu(_conv4(xv, hv, tv))


def _gdn_core(q, k, v, gate, mb, mg, mr, s, t_given, beta16, g16, gam16, gam16_t, nw):
    c = GDN_CHUNK
    q = q * lax.rsqrt(jnp.sum(q * q, axis=1, keepdims=True) + L2_EPS) * (GDN_D ** -0.5)
    k = k * lax.rsqrt(jnp.sum(k * k, axis=1, keepdims=True) + L2_EPS)

    pick = lambda x, m: jnp.sum(x * m, axis=1, keepdims=True)
    beta = pick(beta16, mb)
    g = jnp.broadcast_to(pick(g16, mg), (c, GDN_D))
    gam1 = pick(gam16, mg)
    gam = jnp.broadcast_to(gam1, (c, GDN_D))
    gam_j = jnp.broadcast_to(jnp.sum(gam16_t * mr, axis=0, keepdims=True), (c, c))

    ri, ci = _iota((c, c), 0), _iota((c, c), 1)
    incl = ci <= ri
    decay = jnp.where(incl, jnp.exp(jnp.where(incl, jnp.broadcast_to(gam1, (c, c)) - gam_j, 0.0)), 0.0)

    kb = k * beta
    a = jnp.where(ci < ri, _dot(kb, k, NT) * decay, 0.0)
    t = _blockinv(a) if t_given is None else _blockinv_given(a, t_given)
    egam = jnp.exp(gam)
    u = _dotx(t, v * beta)
    w = _dotx(t, kb * egam)
    attn = _dot(q, k, NT) * decay
    gl = jnp.sum(g, axis=0, keepdims=True)
    kt = k * jnp.exp(gl - gam)
    v_new = u - _dot(w, s)
    o = _dot(q * egam, s) + _dot(attn, v_new)
    s_out = s * jnp.exp(gl) + _dot(kt, v_new, TN)

    y = o * lax.rsqrt(jnp.mean(o * o, axis=1, keepdims=True) + RMS_EPS) * nw * _silu(gate)
    return y, s_out, t


def _gdn_chunk(q, k, v, gate, s, t_given, ba, alog, dtb, nw, *, masks, row0):
    valid = (row0 + _iota((GDN_CHUNK, 1), 0)) >= PAD
    beta16 = jnp.where(valid, jax.nn.sigmoid(ba), 0.0)
    g16 = jnp.where(valid, -jnp.exp(alog) * _softplus(ba + dtb), 0.0)
    gam16 = _cumsum_rows(g16)
    core = jax.vmap(_gdn_core, in_axes=(0,) * 8 + (None if t_given is None else 0,) + (None,) * 5)
    y, s_out, t = core(q, k, v, gate, *masks, s, t_given, beta16, g16, gam16, gam16.T, nw)
    return (y, s_out, t) if t_given is None else (y, s_out)


def _gdn_specs(hb, nc, rev):
    w = hb * GDN_D
    cw = D_MODEL // w

    def cidx(c):
        return (nc - 1 - c) if rev else c

    def col(base):
        return pl.BlockSpec((GDN_CHUNK, w), lambda h, c: (cidx(c), base // w + h))

    def halo(base):
        return pl.BlockSpec((8, w), lambda h, c: (jnp.maximum(cidx(c) * (GDN_CHUNK // 8) - 1, 0), base // w + h))

    def taps(base):
        return pl.BlockSpec((8, w), lambda h, c: (0, base // w + h))

    ba = pl.BlockSpec((GDN_CHUNK, 128), lambda h, c: (cidx(c), C_BA // 128))
    row = pl.BlockSpec((8, 128), lambda h, c: (0, 0))
    y = pl.BlockSpec((GDN_CHUNK, w), lambda h, c: (cidx(c), h))
    st = pl.BlockSpec((1, hb, GDN_D, GDN_D), lambda h, c: (cidx(c), h, 0, 0))
    in_specs = [col(C_GQ), col(C_GK), col(C_GV), halo(C_GQ), halo(C_GK), halo(C_GV), col(C_GG), ba,
                taps(0), taps(1024), taps(2048), row, row, row]
    return in_specs, y, st, taps, row, col, ba


def _gdn_load(refs, first):
    xq, xk, xv, hq, hk, hv, gate, ba, tq, tk, tv, alog, dtb, nw = refs

    def halo(r):
        return jnp.where(first, 0.0, r[...])

    def taps(r):
        return tuple(r[j:j + 1, :] for j in range(4))

    act = (xq[...], xk[...], xv[...], halo(hq), halo(hk), halo(hv), taps(tq), taps(tk), taps(tv))
    return act, gate[...], (ba[...], alog[0:1, :], dtb[0:1, :], nw[0:1, :])


def _heads(a, hb):
    return jnp.stack([a[:, i * GDN_D:(i + 1) * GDN_D] for i in range(hb)])


def _wide(a):
    return jnp.concatenate([a[i] for i in range(a.shape[0])], axis=1)


def _head_masks(hblk, hb):
    head = hblk * hb + _iota((hb, 1, 128), 0)
    lane = _iota((hb, 1, 128), 2)
    rows = (_iota((hb, 128, 1), 1) == hblk * hb + _iota((hb, 128, 1), 0) + 8).astype(f32)
    return (lane == head).astype(f32), (lane == head + 8).astype(f32), rows


def gdn_fwd(u, conv_w8, alog8, dtb8, nw8, *, hb=8):
    t_rows = u.shape[0]
    nc = t_rows // GDN_CHUNK
    in_specs, y_spec, st_spec, *_ = _gdn_specs(hb, nc, False)

    def body(*refs):
        ins, (y_ref, st_ref, t_ref), (s_scr,) = refs[:14], refs[14:17], refs[17:]
        hblk, c = pl.program_id(0), pl.program_id(1)

        @pl.when(c == 0)
        def _():
            s_scr[...] = jnp.zeros_like(s_scr)

        act, gate, shared = _gdn_load(ins, c == 0)
        s = s_scr[...]
        st_ref[0] = s
        qa, ka, va = _gdn_act(*act)
        y, s_new, t = _gdn_chunk(_heads(qa, hb), _heads(ka, hb), _heads(va, hb), _heads(gate, hb), s, None, *shared,
                                 masks=_head_masks(hblk, hb), row0=c * GDN_CHUNK)
        y_ref[...] = _wide(y).astype(bf16)
        t_ref[0] = t
        s_scr[...] = s_new

    return pl.pallas_call(
        body, name="gdn_fwd", grid=(GDN_HEADS // hb, nc),
        in_specs=in_specs,
        out_specs=(y_spec, st_spec, pl.BlockSpec((1, hb, GDN_CHUNK, GDN_CHUNK), lambda h, c: (c, h, 0, 0))),
        out_shape=(jax.ShapeDtypeStruct((t_rows, D_MODEL), bf16),
                   jax.ShapeDtypeStruct((nc, GDN_HEADS, GDN_D, GDN_D), f32),
                   jax.ShapeDtypeStruct((nc, GDN_HEADS, GDN_CHUNK, GDN_CHUNK), f32)),
        scratch_shapes=[pltpu.VMEM((hb, GDN_D, GDN_D), f32)],
        compiler_params=_params(("arbitrary", "arbitrary")),
    )(u, u, u, u, u, u, u, u, conv_w8, conv_w8, conv_w8, alog8, dtb8, nw8)


def gdn_bwd(u, conv_w8, alog8, dtb8, nw8, states, tinv, dy, du):
    t_rows = u.shape[0]
    nc = t_rows // GDN_CHUNK
    hb = GDN_HEADS
    w = hb * GDN_D
    in_specs, y_spec, st_spec, taps, row, col, ba = _gdn_specs(hb, nc, True)
    nhb = GDN_HEADS // hb

    def body(*refs):
        ins, st_ref, t_ref, dy_ref = refs[:14], refs[14], refs[15], refs[16]
        du_ref, dba_ref, dtq_ref, dtk_ref, dtv_ref, dalog_ref, ddtb_ref, dnw_ref = refs[18:26]
        ds_scr, dh_scr = refs[26:]
        hblk, cc = pl.program_id(0), pl.program_id(1)
        c = nc - 1 - cc

        @pl.when(cc == 0)
        def _():
            ds_scr[...] = jnp.zeros_like(ds_scr)
            dh_scr[...] = jnp.zeros_like(dh_scr)
            dtq_ref[...] = jnp.zeros_like(dtq_ref)
            dtk_ref[...] = jnp.zeros_like(dtk_ref)
            dtv_ref[...] = jnp.zeros_like(dtv_ref)

        @pl.when((cc == 0) & (hblk == 0))
        def _():
            dalog_ref[...] = jnp.zeros_like(dalog_ref)
            ddtb_ref[...] = jnp.zeros_like(ddtb_ref)
            dnw_ref[...] = jnp.zeros_like(dnw_ref)

        act, gate, shared = _gdn_load(ins, c == 0)
        (qa, ka, va), vjp_act = jax.vjp(_gdn_act, *act)
        chunk = functools.partial(_gdn_chunk, masks=_head_masks(hblk, hb), row0=c * GDN_CHUNK)
        _, vjp_core = jax.vjp(chunk, _heads(qa, hb), _heads(ka, hb), _heads(va, hb), _heads(gate, hb), st_ref[0],
                              t_ref[0], *shared)
        dqa, dka, dva, dgate, ds, _, dba, dalog, ddtb, dnw = vjp_core(
            (_heads(dy_ref[...].astype(f32), hb), ds_scr[...]))
        ds_scr[...] = ds
        dxq, dxk, dxv, dhq, dhk, dhv, dtq, dtk, dtv = vjp_act((_wide(dqa), _wide(dka), _wide(dva)))
        zeros = jnp.zeros((GDN_CHUNK - 8, w), f32)
        for j, (dx, dh) in enumerate(((dxq, dhq), (dxk, dhk), (dxv, dhv))):
            du_ref[:, j * w:(j + 1) * w] = (dx + jnp.concatenate([zeros, dh_scr[j]], axis=0)).astype(bf16)
            dh_scr[j] = dh
        du_ref[:, 3 * w:4 * w] = _wide(dgate).astype(bf16)
        dba_ref[0] = dba
        for dt_ref, dtaps in ((dtq_ref, dtq), (dtk_ref, dtk), (dtv_ref, dtv)):
            for j in range(4):
                dt_ref[j:j + 1, :] += dtaps[j]
        dalog_ref[0:1, :] += dalog
        ddtb_ref[0:1, :] += ddtb
        dnw_ref[0:1, :] += dnw

    out_specs = (pl.BlockSpec((GDN_CHUNK, 4 * w), lambda h, c: (nc - 1 - c, C_GQ // (4 * w))),
                 pl.BlockSpec((1, GDN_CHUNK, 128), lambda h, c: (h, nc - 1 - c, 0)),
                 taps(0), taps(0), taps(0), row, row, row)
    out_shape = (jax.ShapeDtypeStruct(du.shape, du.dtype),
                 jax.ShapeDtypeStruct((nhb, t_rows, 128), f32),
                 jax.ShapeDtypeStruct((8, D_MODEL), f32), jax.ShapeDtypeStruct((8, D_MODEL), f32),
                 jax.ShapeDtypeStruct((8, D_MODEL), f32),
                 jax.ShapeDtypeStruct((8, 128), f32), jax.ShapeDtypeStruct((8, 128), f32), jax.ShapeDtypeStruct((8, 128), f32))
    return pl.pallas_call(
        body, name="gdn_bwd", grid=(nhb, nc),
        in_specs=in_specs + [st_spec, pl.BlockSpec((1, hb, GDN_CHUNK, GDN_CHUNK), lambda h, c: (nc - 1 - c, h, 0, 0)),
                             y_spec, ANY],
        out_specs=out_specs, out_shape=out_shape, input_output_aliases={17: 0},
        scratch_shapes=[pltpu.VMEM((hb, GDN_D, GDN_D), f32), pltpu.VMEM((3, 8, w), f32)],
        compiler_params=_params(("arbitrary", "arbitrary")),
    )(u, u, u, u, u, u, u, u, conv_w8, conv_w8, conv_w8, alog8, dtb8, nw8, states, tinv, dy, du)


def _ssd_act(xs_r, b_r, c_r, hx, hbm, hcm, tx, tb, tc, bx, bb, bc, *, row0):
    valid = (row0 + _iota((SSD_CHUNK, 1), 0)) >= PAD
    act = lambda x, h, t, b: jnp.where(valid, _silu(_conv4(x, h, t) + b), 0.0)
    return act(xs_r, hx, tx, bx), act(b_r, hbm, tb, bb), act(c_r, hcm, tc, bc)


def _ssd_core(xs, bm, cm, z, nw, lanes, rows, h, dtp16, adt16, acum16, acum16_t, dsk):
    n = SSD_CHUNK
    pick = lambda x, m: jnp.sum(x * m, axis=1, keepdims=True)
    lane_r = _iota((1, 256), 1) >> 6
    dtp = jnp.zeros((n, 256), f32)
    adt = jnp.zeros((n, 256), f32)
    acum = jnp.zeros((n, 256), f32)
    dlane = jnp.zeros((1, 256), f32)
    ccols = []
    for r in range(SSD_HPG):
        ccols.append(pick(acum16, lanes[r]))
        dtp = jnp.where(lane_r == r, pick(dtp16, lanes[r]), dtp)
        adt = jnp.where(lane_r == r, pick(adt16, lanes[r]), adt)
        acum = jnp.where(lane_r == r, ccols[r], acum)
        dlane = jnp.where(lane_r == r, pick(dsk, lanes[r]), dlane)

    ri, ci = _iota((n, n), 0), _iota((n, n), 1)
    incl = ci <= ri
    al = jnp.sum(adt, axis=0, keepdims=True)
    xdt = xs * dtp
    cb = _dot(cm, bm, NT)
    y = _dot(cm, h) * jnp.exp(acum) + dlane * xs
    for r in range(SSD_HPG):
        ai = jnp.broadcast_to(ccols[r], (n, n))
        aj = jnp.broadcast_to(jnp.sum(acum16_t * rows[r], axis=0, keepdims=True), (n, n))
        lm = jnp.where(incl, jnp.exp(jnp.where(incl, ai - aj, 0.0)), 0.0)
        y = y + _dot(cb * lm, jnp.where(lane_r == r, xdt, 0.0))
    h_out = h * jnp.exp(al) + _dot(bm, jnp.exp(al - acum) * xdt, TN)
    y = y * _silu(z)
    y = y * lax.rsqrt(jnp.mean(y * y, axis=1, keepdims=True) + RMS_EPS) * nw
    return y, h_out


def _ssd_chunk(xs, bm, cm, z, nw, h, dt, dtb, alog, dsk, *, row0):
    valid = (row0 + _iota((SSD_CHUNK, 1), 0)) >= PAD
    dtp16 = jnp.where(valid, _softplus(dt + dtb), 0.0)
    adt16 = -jnp.exp(alog) * dtp16
    acum16 = _cumsum_rows(adt16)
    lanes = tuple((_iota((SSD_GROUPS, 1, 128), 2) == _iota((SSD_GROUPS, 1, 128), 0) * SSD_HPG + r).astype(f32)
                  for r in range(SSD_HPG))
    rows = tuple((_iota((SSD_GROUPS, 128, 1), 1) == _iota((SSD_GROUPS, 128, 1), 0) * SSD_HPG + r).astype(f32)
                 for r in range(SSD_HPG))
    core = jax.vmap(_ssd_core, in_axes=(0,) * 8 + (None,) * 5)
    return core(xs, bm, cm, z, nw, lanes, rows, h, dtp16, adt16, acum16, acum16.T, dsk)


def _ssd_specs(nc, rev):
    n = SSD_CHUNK

    def cidx(c):
        return (nc - 1 - c) if rev else c

    def col(base, w):
        return pl.BlockSpec((n, w), lambda c: (cidx(c), base // w))

    def halo(base, w):
        return pl.BlockSpec((8, w), lambda c: (jnp.maximum(cidx(c) * (n // 8) - 1, 0), base // w))

    def taps(base, w):
        return pl.BlockSpec((8, w), lambda c: (0, base // w))

    row = pl.BlockSpec((8, 128), lambda c: (0, 0))
    in_specs = [col(C_SX, 1024), col(C_SB, 512), col(C_SC, 512), halo(C_SX, 1024), halo(C_SB, 512), halo(C_SC, 512),
                col(C_SZ, 1024), col(C_SDT, 128), taps(0, 1024), taps(1024, 512), taps(1536, 512), row, row, row,
                taps(0, 1024)]
    y = pl.BlockSpec((n, D_MODEL), lambda c: (cidx(c), 0))
    st = pl.BlockSpec((1, SSD_GROUPS, SSD_N, 256), lambda c: (cidx(c), 0, 0, 0))
    return in_specs, y, st, col, taps, row


def _ssd_load(refs, first):
    xs, bm, cm, hx, hbm, hcm, z, dt, tx, tb, tc, dtb, alog, dsk, nw = refs

    def halo(r):
        return jnp.where(first, 0.0, r[...])

    def taps(r):
        return tuple(r[j:j + 1, :] for j in range(4))

    act = (xs[...], bm[...], cm[...], halo(hx), halo(hbm), halo(hcm), taps(tx), taps(tb), taps(tc),
           tx[4:5, :], tb[4:5, :], tc[4:5, :])
    return act, (z[...], nw[0:1, :]), (dt[...], dtb[0:1, :], alog[0:1, :], dsk[0:1, :])


def _groups(a, w):
    return jnp.stack([a[:, i * w:(i + 1) * w] for i in range(SSD_GROUPS)])


def ssd_fwd(u, conv_w8, dtb8, alog8, d8, nw8):
    t_rows = u.shape[0]
    nc = t_rows // SSD_CHUNK
    in_specs, y_spec, st_spec, *_ = _ssd_specs(nc, False)

    def body(*refs):
        ins, (y_ref, st_ref), (h_scr,) = refs[:15], refs[15:17], refs[17:]
        c = pl.program_id(0)

        @pl.when(c == 0)
        def _():
            h_scr[...] = jnp.zeros_like(h_scr)

        act, (z, nw), shared = _ssd_load(ins, c == 0)
        h = h_scr[...]
        st_ref[0] = h
        xs, bm, cm = _ssd_act(*act, row0=c * SSD_CHUNK)
        y, h_new = _ssd_chunk(_groups(xs, 256), _groups(bm, 128), _groups(cm, 128), _groups(z, 256),
                              _groups(nw, 256), h, *shared, row0=c * SSD_CHUNK)
        y_ref[...] = _wide(y).astype(bf16)
        h_scr[...] = h_new

    return pl.pallas_call(
        body, name="ssd_fwd", grid=(nc,), in_specs=in_specs, out_specs=(y_spec, st_spec),
        out_shape=(jax.ShapeDtypeStruct((t_rows, D_MODEL), bf16),
                   jax.ShapeDtypeStruct((nc, SSD_GROUPS, SSD_N, 256), f32)),
        scratch_shapes=[pltpu.VMEM((SSD_GROUPS, SSD_N, 256), f32)],
        compiler_params=_params(("arbitrary",)),
    )(u, u, u, u, u, u, u, u, conv_w8, conv_w8, conv_w8, dtb8, alog8, d8, nw8)


def ssd_bwd(u, conv_w8, dtb8, alog8, d8, nw8, states, dy, du):
    t_rows = u.shape[0]
    nc = t_rows // SSD_CHUNK
    n = SSD_CHUNK
    in_specs, y_spec, st_spec, col, taps, row = _ssd_specs(nc, True)

    def body(*refs):
        ins, st_ref, dy_ref = refs[:15], refs[15], refs[16]
        du_ref, ddt_ref, dtx_ref, dtb_ref, dtc_ref, ddtb_ref, dalog_ref, ddsk_ref, dnw_ref = refs[18:27]
        dh_scr, hx_scr, hb_scr, hc_scr = refs[27:]
        cc = pl.program_id(0)
        c = nc - 1 - cc

        @pl.when(cc == 0)
        def _():
            for r in (dh_scr, hx_scr, hb_scr, hc_scr, dtx_ref, dtb_ref, dtc_ref, dnw_ref, ddtb_ref, dalog_ref, ddsk_ref):
                r[...] = jnp.zeros_like(r)

        act, (z, nw), shared = _ssd_load(ins, c == 0)
        (xs, bm, cm), vjp_act = jax.vjp(functools.partial(_ssd_act, row0=c * n), *act)
        _, vjp_core = jax.vjp(functools.partial(_ssd_chunk, row0=c * n), _groups(xs, 256), _groups(bm, 128),
                              _groups(cm, 128), _groups(z, 256), _groups(nw, 256), st_ref[0], *shared)
        dxa, dba, dca, dz, dnw, dh, ddt, ddtb, dalog, ddsk = vjp_core(
            (_groups(dy_ref[...].astype(f32), 256), dh_scr[...]))
        dh_scr[...] = dh
        dxs, dbm, dcm, dhx, dhb, dhc, dtx, dtb, dtc, dbx, dbb, dbc = vjp_act((_wide(dxa), _wide(dba), _wide(dca)))
        du_ref[:, 0:D_MODEL] = _wide(dz).astype(bf16)
        for dx, dhalo, scr, lo in ((dxs, dhx, hx_scr, C_SX), (dbm, dhb, hb_scr, C_SB), (dcm, dhc, hc_scr, C_SC)):
            zeros = jnp.zeros((n - 8, dx.shape[1]), f32)
            du_ref[:, lo - C_SZ:lo - C_SZ + dx.shape[1]] = (dx + jnp.concatenate([zeros, scr[...]], axis=0)).astype(bf16)
            scr[...] = dhalo
        ddt_ref[...] = ddt
        for ref, dtaps, dbias in ((dtx_ref, dtx, dbx), (dtb_ref, dtb, dbb), (dtc_ref, dtc, dbc)):
            for j in range(4):
                ref[j:j + 1, :] += dtaps[j]
            ref[4:5, :] += dbias
        ddtb_ref[0:1, :] += ddtb
        dalog_ref[0:1, :] += dalog
        ddsk_ref[0:1, :] += ddsk
        dnw_ref[0:1, :] += _wide(dnw)

    def out_col(w):
        return pl.BlockSpec((n, w), lambda c: (nc - 1 - c, 0))

    out_specs = (pl.BlockSpec((n, 3 * D_MODEL), lambda c: (nc - 1 - c, C_SZ // (3 * D_MODEL))), out_col(128),
                 taps(0, D_MODEL), taps(0, 512), taps(0, 512), row, row, row, taps(0, D_MODEL))
    out_shape = (jax.ShapeDtypeStruct(du.shape, du.dtype),
                 jax.ShapeDtypeStruct((t_rows, 128), f32),
                 jax.ShapeDtypeStruct((8, D_MODEL), f32), jax.ShapeDtypeStruct((8, 512), f32),
                 jax.ShapeDtypeStruct((8, 512), f32),
                 jax.ShapeDtypeStruct((8, 128), f32), jax.ShapeDtypeStruct((8, 128), f32),
                 jax.ShapeDtypeStruct((8, 128), f32), jax.ShapeDtypeStruct((8, D_MODEL), f32))
    return pl.pallas_call(
        body, name="ssd_bwd", grid=(nc,), in_specs=in_specs + [st_spec, y_spec, ANY],
        out_specs=out_specs, out_shape=out_shape, input_output_aliases={17: 0},
        scratch_shapes=[pltpu.VMEM((SSD_GROUPS, SSD_N, 256), f32), pltpu.VMEM((8, D_MODEL), f32),
                        pltpu.VMEM((8, 512), f32), pltpu.VMEM((8, 512), f32)],
        compiler_params=_params(("arbitrary",)),
    )(u, u, u, u, u, u, u, u, conv_w8, conv_w8, conv_w8, dtb8, alog8, d8, nw8, states, dy, du)


NEG = -1e30


def _swa_core(q, kc, kp, km, vc, vp, vm, sink, *, n):
    rows = SWA_REP * SWA_W
    ri, ci = _iota((rows, SWA_W), 0) & (SWA_W - 1), _iota((rows, SWA_W), 1)
    causal = ci <= ri
    m_cur = causal & ((n >= 1) | ((ci >= PAD) & (ri >= PAD)))
    m_prev = (n >= 2) & (ci > ri)
    m_meta = (n >= 1) & (ci >= PAD)
    q = q * (SWA_D ** -0.5)
    sc = jnp.where(m_cur, _dot(q, kc, NT), NEG)
    sp = jnp.where(m_prev, _dot(q, kp, NT), NEG)
    sm = jnp.where(m_meta, _dot(q, km, NT), NEG)
    mx = jnp.maximum(jnp.maximum(jnp.max(sc, axis=1, keepdims=True), jnp.max(sp, axis=1, keepdims=True)),
                     jnp.maximum(jnp.max(sm, axis=1, keepdims=True), sink))
    mx = lax.stop_gradient(mx)
    ec, ep, em = jnp.exp(sc - mx), jnp.exp(sp - mx), jnp.exp(sm - mx)
    den = (jnp.sum(ec, axis=1, keepdims=True) + jnp.sum(ep, axis=1, keepdims=True)
           + jnp.sum(em, axis=1, keepdims=True) + jnp.exp(sink - mx))
    return (_dot(ec, vc) + _dot(ep, vp) + _dot(em, vm)) / den


def _swa_block(q16, kc, kp, km, vc, vp, vm, sink16, *, n):
    rows = SWA_REP * SWA_W
    lane = _iota((1, 128), 1)
    rep = _iota((rows, 1), 0) >> 7
    cols = []
    for h in range(SWA_KV_HEADS):
        col = jnp.zeros((rows, 1), f32)
        for r in range(SWA_REP):
            s = jnp.sum(jnp.where(lane == h * SWA_REP + r, sink16, 0.0), axis=1, keepdims=True)
            col = jnp.where(rep == r, s, col)
        cols.append(col)
    o = jax.vmap(functools.partial(_swa_core, n=n))(q16.reshape(SWA_KV_HEADS, rows, SWA_D), kc, kp, km, vc, vp, vm,
                                                    jnp.concatenate([col[None] for col in cols], axis=0))
    return o.reshape(q16.shape)


def _swa_specs(nb, rev):
    def bidx(n):
        return (nb - 1 - n) if rev else n

    q = pl.BlockSpec((SWA_Q_HEADS, SWA_W, SWA_D), lambda n: (0, bidx(n), 0))
    cur = pl.BlockSpec((SWA_KV_HEADS, SWA_W, SWA_D), lambda n: (0, bidx(n), 0))
    prev = pl.BlockSpec((SWA_KV_HEADS, SWA_W, SWA_D), lambda n: (0, jnp.maximum(bidx(n) - 1, 0), 0))
    meta = pl.BlockSpec((SWA_KV_HEADS, SWA_W, SWA_D), lambda n: (0, 0, 0))
    row = pl.BlockSpec((8, 128), lambda n: (0, 0))
    return [q, cur, prev, meta, cur, prev, meta, row], q, cur, row


def swa_fwd(q, k, v, sink8):
    t_rows = q.shape[1]
    nb = t_rows // SWA_W
    in_specs, q_spec, _, _ = _swa_specs(nb, False)

    def body(q_ref, kc, kp, km, vc, vp, vm, sink_ref, o_ref):
        o_ref[...] = _swa_block(q_ref[...], kc[...], kp[...], km[...], vc[...], vp[...], vm[...], sink_ref[0:1, :],
                                n=pl.program_id(0)).astype(bf16)

    return pl.pallas_call(
        body, name="swa_fwd", grid=(nb,), in_specs=in_specs, out_specs=q_spec,
        out_shape=jax.ShapeDtypeStruct(q.shape, bf16),
        compiler_params=_params(("arbitrary",)),
    )(q, k, k, k, v, v, v, sink8)


def swa_bwd(q, k, v, sink8, do):
    t_rows = q.shape[1]
    nb = t_rows // SWA_W
    in_specs, q_spec, kv_spec, row = _swa_specs(nb, True)

    def body(q_ref, kc, kp, km, vc, vp, vm, sink_ref, do_ref, dq_ref, dk_ref, dv_ref, dsink_ref,
             dkp_scr, dvp_scr, dkm_scr, dvm_scr):
        nn = pl.program_id(0)
        n = nb - 1 - nn

        @pl.when(nn == 0)
        def _():
            for r in (dkp_scr, dvp_scr, dkm_scr, dvm_scr, dsink_ref):
                r[...] = jnp.zeros_like(r)

        fn = functools.partial(_swa_block, n=n)
        _, vjp = jax.vjp(fn, q_ref[...], kc[...], kp[...], km[...], vc[...], vp[...], vm[...], sink_ref[0:1, :])
        dq, dkc, dkp, dkm, dvc, dvp, dvm, dsink = vjp(do_ref[...].astype(f32))
        dq_ref[...] = dq.astype(bf16)
        dkm_scr[...] += dkm
        dvm_scr[...] += dvm
        first = n == 0
        dk_ref[...] = (dkc + dkp_scr[...] + jnp.where(first, dkm_scr[...], 0.0)).astype(bf16)
        dv_ref[...] = (dvc + dvp_scr[...] + jnp.where(first, dvm_scr[...], 0.0)).astype(bf16)
        dkp_scr[...] = dkp
        dvp_scr[...] = dvp
        dsink_ref[0:1, :] += dsink

    kv_shape = jax.ShapeDtypeStruct(k.shape, bf16)
    return pl.pallas_call(
        body, name="swa_bwd", grid=(nb,), in_specs=in_specs + [q_spec],
        out_specs=(q_spec, kv_spec, kv_spec, row),
        out_shape=(jax.ShapeDtypeStruct(q.shape, bf16), kv_shape, kv_shape, jax.ShapeDtypeStruct((8, 128), f32)),
        scratch_shapes=[pltpu.VMEM((SWA_KV_HEADS, SWA_W, SWA_D), f32)] * 4,
        compiler_params=_params(("arbitrary",)),
    )(q, k, k, k, v, v, v, sink8, do)


def _tile(dim, prefs):
    for p in prefs:
        if dim % p == 0:
            return p
    return dim


def mm(a, b, *, out_dtype, name, resid=None, relu_grad_of=None, relu2_out=False, ta=False, tb=False):
    assert resid is None or relu_grad_of is None
    k, m = (a.shape if ta else a.shape[::-1])
    n = b.shape[0] if tb else b.shape[1]
    rhs_stays = k * 2 * 1024 > MM_OPERAND_BYTES
    assert not (ta and rhs_stays)
    if rhs_stays:
        tn = _tile(n, tuple(p for p in (512, 256, 128) if p * k * 2 <= MM_RESIDENT_BYTES))
        tm = _tile(m, tuple(p for p in (512, 384, 256, 128) if p * k * 2 <= MM_OPERAND_BYTES // 2))
        grid = (n // tn, m // tm)
        ij = lambda o, i: (i, o)
    else:
        tm = _tile(m, tuple(p for p in (1408, 1024, 512, 384, 256, 128) if p * k * 2 <= MM_OPERAND_BYTES))
        tn = _tile(n, tuple(p for p in (512, 256, 128) if p * k * 2 <= MM_OPERAND_BYTES // 2))
        grid = (m // tm, n // tn)
        ij = lambda o, i: (o, i)

    extra = resid if resid is not None else relu_grad_of

    def body(*refs):
        a_ref, b_ref = refs[:2]
        if ta:
            at_scr = refs[-1]
            refs = refs[:-1]

            @pl.when(pl.program_id(1) == 0)
            def _():
                at_scr[...] = a_ref[...].T

            lhs = at_scr[...]
        else:
            lhs = a_ref[...]
        o = _dot(lhs, b_ref[...], NT if tb else NN)
        if resid is not None:
            o = o + refs[2][...]
        if relu_grad_of is not None:
            o = o * (2.0 * jnp.maximum(refs[2][...], 0.0))
        if relu2_out:
            refs[-2][...] = o.astype(out_dtype)
            r = jnp.maximum(o, 0.0)
            refs[-1][...] = (r * r).astype(bf16)
        else:
            refs[-1][...] = o.astype(out_dtype)

    in_specs = [pl.BlockSpec((k, tm), lambda o, i: (0, ij(o, i)[0])) if ta
                else pl.BlockSpec((tm, k), lambda o, i: (ij(o, i)[0], 0)),
                pl.BlockSpec((tn, k), lambda o, i: (ij(o, i)[1], 0)) if tb
                else pl.BlockSpec((k, tn), lambda o, i: (0, ij(o, i)[1]))]
    args = [a, b]
    if extra is not None:
        in_specs.append(pl.BlockSpec((tm, tn), ij))
        args.append(extra)
    out_blk = pl.BlockSpec((tm, tn), ij)
    out = jax.ShapeDtypeStruct((m, n), out_dtype)
    return pl.pallas_call(
        body, name=name, grid=grid, in_specs=in_specs,
        out_specs=(out_blk, out_blk) if relu2_out else out_blk,
        out_shape=(out, jax.ShapeDtypeStruct((m, n), bf16)) if relu2_out else out,
        scratch_shapes=[pltpu.VMEM((tm, k), bf16)] if ta else [],
        compiler_params=_params(("parallel", "arbitrary" if ta else "parallel")),
    )(*args)


def _rows(t_rows):
    return _tile(t_rows, (384, 256, 128))


def _rmsnorm(h, w):
    return h * lax.rsqrt(jnp.mean(h * h, axis=1, keepdims=True) + RMS_EPS) * w


def rmsnorm_fwd(h, w8, *, name):
    t_rows, d = h.shape
    tr = _rows(t_rows)

    def body(h_ref, w_ref, o_ref):
        o_ref[...] = _rmsnorm(h_ref[...], w_ref[0:1, :]).astype(bf16)

    blk = pl.BlockSpec((tr, d), lambda i: (i, 0))
    return pl.pallas_call(
        body, name=name, grid=(t_rows // tr,), in_specs=[blk, pl.BlockSpec((8, d), lambda i: (0, 0))], out_specs=blk,
        out_shape=jax.ShapeDtypeStruct((t_rows, d), bf16), compiler_params=_params(("arbitrary",)),
    )(h, w8)


def rmsnorm_bwd(h, w8, dhn, dres, *, name):
    t_rows, d = h.shape
    tr = _rows(t_rows)

    def body(h_ref, w_ref, dhn_ref, dres_ref, dh_ref, dw_ref):
        @pl.when(pl.program_id(0) == 0)
        def _():
            dw_ref[...] = jnp.zeros_like(dw_ref)

        _, vjp = jax.vjp(_rmsnorm, h_ref[...], w_ref[0:1, :])
        dh, dw = vjp(dhn_ref[...])
        dh_ref[...] = dh + dres_ref[...]
        dw_ref[0:1, :] += dw

    blk = pl.BlockSpec((tr, d), lambda i: (i, 0))
    wblk = pl.BlockSpec((8, d), lambda i: (0, 0))
    return pl.pallas_call(
        body, name=name, grid=(t_rows // tr,), in_specs=[blk, wblk, blk, blk], out_specs=(blk, wblk),
        out_shape=(jax.ShapeDtypeStruct((t_rows, d), f32), jax.ShapeDtypeStruct((8, d), f32)),
        compiler_params=_params(("arbitrary",)),
    )(h, w8, dhn, dres)


def _merge(pg, ps, pw, la, lb, lc):
    return jax.nn.sigmoid(la) * pg + jax.nn.sigmoid(lb) * ps + jax.nn.sigmoid(lc) * pw


def _merge_specs(t_rows):
    tr = _rows(t_rows)
    blk = pl.BlockSpec((tr, D_MODEL), lambda i: (i, 0))
    gate = [pl.BlockSpec((tr, D_MODEL), functools.partial(lambda i, j: (i, j), j=C_GATE // D_MODEL + j)) for j in range(3)]
    return tr, blk, gate


def merge_fwd(pg, ps, pw, u):
    t_rows = pg.shape[0]
    tr, blk, gate = _merge_specs(t_rows)

    def body(pg_ref, ps_ref, pw_ref, la, lb, lc, o_ref):
        o_ref[...] = _merge(pg_ref[...], ps_ref[...], pw_ref[...], la[...], lb[...], lc[...]).astype(bf16)

    return pl.pallas_call(
        body, name="merge_fwd", grid=(t_rows // tr,), in_specs=[blk, blk, blk] + gate, out_specs=blk,
        out_shape=jax.ShapeDtypeStruct((t_rows, D_MODEL), bf16), compiler_params=_params(("arbitrary",)),
    )(pg, ps, pw, u, u, u)


def merge_bwd(pg, ps, pw, u, dmerged, du):
    t_rows = pg.shape[0]
    tr, blk, gate = _merge_specs(t_rows)

    def body(pg_ref, ps_ref, pw_ref, la, lb, lc, dm_ref, _, dpg_ref, dps_ref, dpw_ref, dl_ref):
        _, vjp = jax.vjp(_merge, pg_ref[...], ps_ref[...], pw_ref[...], la[...], lb[...], lc[...])
        dpg, dps, dpw, dla, dlb, dlc = vjp(dm_ref[...])
        dpg_ref[...] = dpg.astype(bf16)
        dps_ref[...] = dps.astype(bf16)
        dpw_ref[...] = dpw.astype(bf16)
        for j, dl in enumerate((dla, dlb, dlc)):
            dl_ref[:, j * D_MODEL:(j + 1) * D_MODEL] = dl.astype(bf16)

    act = jax.ShapeDtypeStruct((t_rows, D_MODEL), bf16)
    return pl.pallas_call(
        body, name="merge_bwd", grid=(t_rows // tr,), in_specs=[blk, blk, blk] + gate + [blk, ANY],
        out_specs=(blk, blk, blk, pl.BlockSpec((tr, 3 * D_MODEL), lambda i: (i, C_GATE // (3 * D_MODEL)))),
        out_shape=(act, act, act, jax.ShapeDtypeStruct(du.shape, du.dtype)),
        input_output_aliases={7: 3},
        compiler_params=_params(("arbitrary",)),
    )(pg, ps, pw, u, u, u, dmerged, du)


def relu2_fwd(a):
    t_rows, d = a.shape
    tr = _rows(t_rows)

    def body(a_ref, o_ref):
        r = jnp.maximum(a_ref[...], 0.0)
        o_ref[...] = (r * r).astype(bf16)

    blk = pl.BlockSpec((tr, d), lambda i: (i, 0))
    return pl.pallas_call(
        body, name="relu2_fwd", grid=(t_rows // tr,), in_specs=[blk], out_specs=blk,
        out_shape=jax.ShapeDtypeStruct((t_rows, d), bf16), compiler_params=_params(("arbitrary",)),
    )(a)


def relu2_bwd(a, dr):
    t_rows, d = a.shape
    tr = _rows(t_rows)

    def body(a_ref, dr_ref, o_ref):
        o_ref[...] = (dr_ref[...] * 2.0 * jnp.maximum(a_ref[...], 0.0)).astype(bf16)

    blk = pl.BlockSpec((tr, d), lambda i: (i, 0))
    return pl.pallas_call(
        body, name="relu2_bwd", grid=(t_rows // tr,), in_specs=[blk, blk], out_specs=blk,
        out_shape=jax.ShapeDtypeStruct((t_rows, d), bf16), compiler_params=_params(("arbitrary",)),
    )(a, dr)


def loss_head(h, w8, target):
    t_rows, d = h.shape
    tr = HEAD_ROWS

    def loss_fn(hb, w, tgt):
        err = _rmsnorm(hb, w) - tgt
        return 0.5 * jnp.sum(err * err) / d

    def body(h_ref, w_ref, t_ref, loss_ref, dh_ref, dw_ref):
        i = pl.program_id(0)

        @pl.when(i == 0)
        def _():
            loss_ref[...] = jnp.zeros_like(loss_ref)
            dw_ref[...] = jnp.zeros_like(dw_ref)
            dh_ref[...] = jnp.zeros_like(dh_ref)

        @pl.when(i > 0)
        def _():
            val, (dh, dw) = jax.value_and_grad(loss_fn, argnums=(0, 1))(h_ref[...], w_ref[0:1, :], t_ref[...])
            loss_ref[...] += val
            dh_ref[...] = dh
            dw_ref[0:1, :] += dw

    blk = pl.BlockSpec((tr, d), lambda i: (i, 0))
    wblk = pl.BlockSpec((8, d), lambda i: (0, 0))
    return pl.pallas_call(
        body, name="loss_head", grid=(t_rows // tr,),
        in_specs=[blk, wblk, pl.BlockSpec((tr, d), lambda i: (jnp.maximum(i - 1, 0), 0))],
        out_specs=(pl.BlockSpec((8, 128), lambda i: (0, 0)), blk, wblk),
        out_shape=(jax.ShapeDtypeStruct((8, 128), f32), jax.ShapeDtypeStruct((t_rows, d), f32),
                   jax.ShapeDtypeStruct((8, d), f32)),
        compiler_params=_params(("arbitrary",)),
    )(h, w8, target)


def adamw(w, m, v, partials, row_off, *, name):
    rows, d = w.shape
    layers = len(partials)
    per = rows // layers
    tr = _tile(per, tuple(p for p in (512, 256, 128, 64, 16, 8) if p * d * 4 <= BLOCK_BYTES))
    assert row_off % tr == 0
    off, nblk = row_off // tr, per // tr
    c1 = 1.0 - ADAM_B1 ** ADAM_STEP
    c2 = 1.0 - ADAM_B2 ** ADAM_STEP

    def body(w_ref, m_ref, v_ref, *refs):
        p_refs, (g_ref, d_ref, mo_ref, vo_ref) = refs[:2 * layers], refs[2 * layers:]
        g = p_refs[0][...] + p_refs[1][...]
        for l in range(1, layers):
            g = jnp.where(pl.program_id(0) >= l * nblk, p_refs[2 * l][...] + p_refs[2 * l + 1][...], g)
        m_new = ADAM_B1 * m_ref[...] + (1.0 - ADAM_B1) * g
        v_new = ADAM_B2 * v_ref[...] + (1.0 - ADAM_B2) * (g * g)
        g_ref[...] = g
        d_ref[...] = -ADAM_LR * ((m_new / c1) / (jnp.sqrt(v_new / c2) + ADAM_EPS) + ADAM_WD * w_ref[...])
        mo_ref[...] = m_new
        vo_ref[...] = v_new

    blk = pl.BlockSpec((tr, d), lambda i: (i, 0))
    pblks = [pl.BlockSpec((tr, d), functools.partial(lambda i, l: (off + jnp.clip(i - l * nblk, 0, nblk - 1), 0), l=l))
             for l in range(layers) for _ in range(2)]
    out = jax.ShapeDtypeStruct((rows, d), f32)
    return pl.pallas_call(
        body, name=name, grid=(rows // tr,), in_specs=[blk, blk, blk] + pblks, out_specs=(blk,) * 4,
        out_shape=(out,) * 4, compiler_params=_params(("arbitrary",)),
    )(w, m, v, *[p for pair in partials for p in pair])


def reduce4(parts, *, name, own=None, me=None):
    _, rows, d = parts.shape
    tr = _tile(rows, tuple(p for p in (512, 256, 128, 64, 8) if p * d * 4 <= BLOCK_BYTES))

    def body(*refs):
        p_ref, o_ref = refs[0], refs[-1]
        acc = None
        for s in range(4):
            term = p_ref[s].astype(f32)
            if own is not None:
                term = jnp.where(refs[2][0] == s, refs[1][...].astype(f32), term)
            acc = term if acc is None else acc + term
        o_ref[...] = acc

    in_specs = [pl.BlockSpec((4, tr, d), lambda i: (0, i, 0))]
    args = [parts]
    if own is not None:
        in_specs += [pl.BlockSpec((tr, d), lambda i: (i, 0)), pl.BlockSpec(memory_space=pltpu.SMEM)]
        args += [own, me]
    return pl.pallas_call(
        body, name=name, grid=(rows // tr,), in_specs=in_specs,
        out_specs=pl.BlockSpec((tr, d), lambda i: (i, 0)), out_shape=jax.ShapeDtypeStruct((rows, d), f32),
        compiler_params=_params(("arbitrary",)),
    )(*args)


ANY = pl.BlockSpec(memory_space=pl.ANY)
MESH = pl.DeviceIdType.MESH
CHIP_FLIPS = ((0, 1), (1, 0), (1, 1))


def chip_exchange(bufs, scatter, *, name, after=None):
    nb = len(bufs)
    extra = [] if after is None else [after]

    def body(*refs):
        ins, outs = refs[:nb], refs[nb + len(extra):2 * nb + len(extra)]
        send_sems, recv_sems, local_sems = refs[2 * nb + len(extra):]
        x, y, c = lax.axis_index("x"), lax.axis_index("y"), lax.axis_index("c")
        me = 2 * x + y
        local = [pltpu.make_async_copy(ins[j].at[me] if scatter[j] else ins[j], outs[j].at[me], local_sems.at[j])
                 for j in range(nb)]
        for cp in local:
            cp.start()
        sends, recvs = [], []
        for k, (fx, fy) in enumerate(CHIP_FLIPS):
            px = 1 - x if fx else x
            py = 1 - y if fy else y
            chip = 2 * px + py
            for j in range(nb):
                src = ins[j].at[chip] if scatter[j] else ins[j]
                sems = dict(send_sem=send_sems.at[nb * k + j], recv_sem=recv_sems.at[nb * k + j],
                            device_id=(px, py, c), device_id_type=MESH)
                sends.append(pltpu.make_async_remote_copy(src_ref=src, dst_ref=outs[j].at[me], **sems))
                recvs.append(pltpu.make_async_remote_copy(src_ref=src, dst_ref=outs[j].at[chip], **sems))
        for cp in sends:
            cp.start()
        for cp in recvs:
            cp.wait_recv()
        for cp in sends:
            cp.wait_send()
        for cp in local:
            cp.wait()

    out_shape = tuple(jax.ShapeDtypeStruct(b.shape if s else (4,) + b.shape, b.dtype) for b, s in zip(bufs, scatter))
    return pl.pallas_call(
        body, name=name, in_specs=[ANY] * (nb + len(extra)), out_specs=(ANY,) * nb, out_shape=out_shape,
        scratch_shapes=[pltpu.SemaphoreType.DMA((3 * nb,)), pltpu.SemaphoreType.DMA((3 * nb,)),
                        pltpu.SemaphoreType.DMA((nb,))],
        compiler_params=pltpu.CompilerParams(has_side_effects=True),
    )(*bufs, *extra)


def sibling_swap(bufs, *, name):
    nb = len(bufs)

    def body(*refs):
        ins, outs, (send_sems, recv_sems) = refs[:nb], refs[nb:2 * nb], refs[2 * nb:]
        peer = (lax.axis_index("x"), lax.axis_index("y"), 1 - lax.axis_index("c"))
        copies = [pltpu.make_async_remote_copy(src_ref=ins[j], dst_ref=outs[j], send_sem=send_sems.at[j],
                                               recv_sem=recv_sems.at[j], device_id=peer, device_id_type=MESH)
                  for j in range(nb)]
        for cp in copies:
            cp.start()
        for cp in copies:
            cp.wait_recv()
        for cp in copies:
            cp.wait_send()

    return pl.pallas_call(
        body, name=name, in_specs=[ANY] * nb, out_specs=(ANY,) * nb,
        out_shape=tuple(jax.ShapeDtypeStruct(b.shape, b.dtype) for b in bufs),
        scratch_shapes=[pltpu.SemaphoreType.DMA((nb,)), pltpu.SemaphoreType.DMA((nb,))],
        compiler_params=pltpu.CompilerParams(has_side_effects=True),
    )(*bufs)


HBM = pl.BlockSpec(memory_space=pltpu.HBM)
SEM = pl.BlockSpec(memory_space=pltpu.SEMAPHORE)
DATAFLOW = pltpu.SideEffectType.DATAFLOW_SIDE_EFFECTING


def _exchange_copies(srcs, lands, send_sems, recv_sems, scatter):
    x, y, c = lax.axis_index("x"), lax.axis_index("y"), lax.axis_index("c")
    me = 2 * x + y
    nb = len(srcs)
    pairs = []
    for k, (fx, fy) in enumerate(CHIP_FLIPS):
        px = 1 - x if fx else x
        py = 1 - y if fy else y
        chip = 2 * px + py
        for j in range(nb):
            src = srcs[j].at[chip] if scatter[j] else srcs[j]
            sems = dict(send_sem=send_sems.at[nb * k + j], recv_sem=recv_sems.at[nb * k + j],
                        device_id=(px, py, c), device_id_type=MESH)
            pairs.append((pltpu.make_async_remote_copy(src_ref=src, dst_ref=lands[j].at[me], **sems),
                          pltpu.make_async_remote_copy(src_ref=src, dst_ref=lands[j].at[chip], **sems)))
    return pairs


def exchange_start(bufs, scatter, after, *, name):
    nb = len(bufs)
    slabs = [b.shape[1:] if s else b.shape for b, s in zip(bufs, scatter)]
    lands = [lax.empty((4,) + shp, b.dtype) for b, shp in zip(bufs, slabs)]

    def body(*refs):
        srcs, zones = refs[:nb], refs[nb:2 * nb]
        send_sems, recv_sems = refs[2 * nb + 1:2 * nb + 3]
        token = refs[-1]
        for send, _ in _exchange_copies(srcs, zones, send_sems, recv_sems, scatter):
            send.start()
        token[...] = jnp.zeros_like(token)

    hbm = lambda a: pltpu.with_memory_space_constraint(a, pltpu.HBM)
    out = pl.pallas_call(
        body, name=name, in_specs=[HBM] * (2 * nb) + [ANY],
        out_specs=(SEM, SEM) + (HBM,) * (2 * nb) + (pl.BlockSpec(memory_space=pltpu.VMEM),),
        out_shape=(pltpu.SemaphoreType.DMA((3 * nb,)), pltpu.SemaphoreType.DMA((3 * nb,)))
        + tuple(pltpu.HBM(a.shape, a.dtype) for a in list(bufs) + lands) + (jax.ShapeDtypeStruct((8, 128), f32),),
        input_output_aliases={i: 2 + i for i in range(2 * nb)},
        compiler_params=pltpu.CompilerParams(has_side_effects=DATAFLOW),
    )(*[hbm(a) for a in list(bufs) + lands], after)
    return (out[:2], out[2:2 + nb], out[2 + nb:2 + 2 * nb], scatter), out[-1]


def exchange_wait(state, after, *, name):
    (send_sems, recv_sems), srcs, lands, scatter = state
    nb = len(srcs)

    def body(*refs):
        src_refs, zones = refs[:nb], refs[nb:2 * nb]
        s_sems, r_sems = refs[2 * nb:2 * nb + 2]
        for send, recv in _exchange_copies(src_refs, zones, s_sems, r_sems, scatter):
            send.wait_send()
            recv.wait_recv()

    out = pl.pallas_call(
        body, name=name, in_specs=[HBM] * (2 * nb) + [SEM, SEM, ANY], out_specs=(HBM,) * (2 * nb),
        out_shape=tuple(pltpu.HBM(a.shape, a.dtype) for a in list(srcs) + list(lands)),
        input_output_aliases={i: i for i in range(2 * nb)},
        compiler_params=pltpu.CompilerParams(has_side_effects=DATAFLOW),
    )(*srcs, *lands, send_sems, recv_sems, after)
    return out[nb:]


BIG = (
    ("w_proj_gdn", 256), ("w_proj_ssd", 256), ("w_proj_swa", 256), ("w_out", 256), ("w_up", 1024), ("w_down", 1024))
BIG_OFF = {}
_o = 0
for _n, _r in BIG:
    BIG_OFF[_n] = _o
    _o += _r
BIG_ROWS = _o
W_IN_SHARD = IN_W // 4

W_NAMES = ('meta_tokens', 'norm1_w', 'w_in', 'gdn_conv_w', 'gdn_a_log', 'gdn_dt_bias', 'gdn_norm_w', 'ssd_conv_w',
           'ssd_conv_b', 'ssd_dt_bias', 'ssd_a_log', 'ssd_d', 'ssd_norm_w', 'swa_sinks', 'w_proj_gdn', 'w_proj_ssd',
           'w_proj_swa', 'w_out', 'norm2_w', 'w_up', 'w_down', 'final_norm_w')
SMALL_NAMES = tuple(n for n in W_NAMES if n not in BIG_OFF and n != "w_in")
SMALL_SHARDED = ("meta_tokens", "gdn_conv_w", "ssd_conv_w")


def _pad_rows(a, rows):
    return jnp.pad(a, ((0, rows - a.shape[0]), (0, 0)))


def _pack_rows(parts, dtype):
    flat = jnp.concatenate([p.reshape(-1).astype(dtype) for p in parts])
    n = -(-flat.shape[0] // 8192) * 8192
    return jnp.pad(flat, (0, n - flat.shape[0])).reshape(-1, D_MODEL)


def _unpack_rows(packed, shapes):
    flat, out, o = packed.reshape(-1), [], 0
    for s in shapes:
        n = 1
        for d in s:
            n *= d
        out.append(flat[o:o + n].reshape(s))
        o += n
    return out


def _split_chips(full, axis):
    s = full.shape
    a = full.reshape(s[:axis] + (4, s[axis] // 4) + s[axis + 1:])
    return jnp.moveaxis(a, axis, 0)


def _join_chips(parts, axis):
    a = jnp.moveaxis(parts, 0, axis)
    s = a.shape
    return a.reshape(s[:axis] + (s[axis] * s[axis + 1],) + s[axis + 2:])


BIG_AXIS = {"w_in": 2, "w_proj_gdn": 1, "w_proj_ssd": 1, "w_proj_swa": 1, "w_out": 1, "w_up": 2, "w_down": 1}


def _w_in_to_padded(w):
    z = lambda n: jnp.zeros(w.shape[:-1] + (n,), w.dtype)
    return jnp.concatenate([w[..., 8736:11808], w[..., 4112:7184], w[..., 7200:8736], w[..., 4096:4112], z(112),
                            w[..., 7184:7200], z(112 + C_MID_END - C_SDT - 128), w[..., 0:4096]], axis=-1)


def _w_in_from_padded(p):
    return jnp.concatenate([p[..., C_GQ:IN_WP], p[..., C_BA:C_BA + 16], p[..., C_SZ:C_WQ], p[..., C_SDT:C_SDT + 16],
                            p[..., C_WQ:C_BA], p[..., 0:C_SZ]], axis=-1)


def _row8(v, lane0=0, width=128):
    return jnp.pad(v[None, :], ((0, 7), (lane0, width - lane0 - v.shape[0])))


def _head_major(a, heads):
    return a.reshape(a.shape[0], heads, SWA_D).transpose(1, 0, 2)


def _from_head_major(a):
    return a.transpose(1, 0, 2).reshape(a.shape[1], -1)


def _layer_fwd(h, p, l, late=None):
    tag = f"l{l}"
    hn = rmsnorm_fwd(h, p["n1"], name=f"norm1_fwd_{tag}")
    u = mm(hn, p["w_in"], out_dtype=f32, name=f"mm_in_{tag}")
    yg, stg, tg = gdn_fwd(u, p["gcw"], p["galog"], p["gdtb"], p["gnw"])
    ys, sts = ssd_fwd(u, p["scw"], p["sdtb"], p["salog"], p["sd"], p["snw"])
    qh = _head_major(u[:, C_WQ:C_WK], SWA_Q_HEADS)
    kh = _head_major(u[:, C_WK:C_WV], SWA_KV_HEADS)
    vh = _head_major(u[:, C_WV:C_BA], SWA_KV_HEADS)
    yw = _from_head_major(swa_fwd(qh, kh, vh, p["sink"]))
    if late is not None:
        p.update(late(yw))
    pg = mm(yg, p["wpg"], out_dtype=f32, name=f"mm_pg_{tag}")
    ps = mm(ys, p["wps"], out_dtype=f32, name=f"mm_ps_{tag}")
    pw = mm(yw, p["wpw"], out_dtype=f32, name=f"mm_pw_{tag}")
    merged = merge_fwd(pg, ps, pw, u)
    h2 = mm(merged, p["wout"], out_dtype=f32, resid=h, name=f"mm_out_{tag}")
    hn2 = rmsnorm_fwd(h2, p["n2"], name=f"norm2_fwd_{tag}")
    a, r = mm(hn2, p["wup"], out_dtype=f32, relu2_out=True, name=f"mm_up_{tag}")
    h3 = mm(r, p["wdown"], out_dtype=f32, resid=h2, name=f"mm_down_{tag}")
    saved = dict(h=h, hn=hn, u=u, yg=yg, stg=stg, tg=tg, ys=ys, sts=sts, qh=qh, kh=kh, vh=vh, yw=yw, pg=pg, ps=ps, pw=pw,
                 merged=merged, h2=h2, hn2=hn2, a=a, r=r)
    return h3, saved


def _layer_bwd(dh3, p, s, l, send_big, send_w_in):
    tag = f"l{l}"
    g = {}

    def wgrad(act, d, name):
        return mm(act, d, ta=True, out_dtype=bf16, name=f"wg_{name}_{tag}")

    da = mm(dh3, p["wdown"], tb=True, out_dtype=bf16, relu_grad_of=s["a"], name=f"dg_down_{tag}")
    g["w_down"] = wgrad(s["r"], dh3, "down")
    dhn2 = mm(da, p["wup"], tb=True, out_dtype=f32, name=f"dg_up_{tag}")
    g["w_up"] = wgrad(s["hn2"], da, "up")
    dh2, g["norm2_w"] = rmsnorm_bwd(s["h2"], p["n2"], dhn2, dh3, name=f"norm2_bwd_{tag}")
    dmerged = mm(dh2, p["wout"], tb=True, out_dtype=f32, name=f"dg_out_{tag}")
    g["w_out"] = wgrad(s["merged"], dh2, "out")
    du = lax.empty((dh3.shape[0], IN_WP), bf16)
    dpg, dps, dpw, du = merge_bwd(s["pg"], s["ps"], s["pw"], s["u"], dmerged, du)
    dyg = mm(dpg, p["wpg"], tb=True, out_dtype=f32, name=f"dg_pg_{tag}")
    dys = mm(dps, p["wps"], tb=True, out_dtype=f32, name=f"dg_ps_{tag}")
    dyw = mm(dpw, p["wpw"], tb=True, out_dtype=f32, name=f"dg_pw_{tag}")
    g["w_proj_gdn"] = wgrad(s["yg"], dpg, "pg")
    g["w_proj_ssd"] = wgrad(s["ys"], dps, "ps")
    g["w_proj_swa"] = wgrad(s["yw"], dpw, "pw")
    sent = send_big(jnp.concatenate([_split_chips(g.pop(n), BIG_AXIS[n] - 1).reshape(4, r, D_MODEL)
                                     for n, r in BIG], axis=1))

    (du, dba, dtq, dtk, dtv, g["gdn_a_log"], g["gdn_dt_bias"], g["gdn_norm_w"]) = gdn_bwd(
        s["u"], p["gcw"] + sent, p["galog"], p["gdtb"], p["gnw"], s["stg"], s["tg"], dyg, du)
    g["gdn_conv_w"] = jnp.concatenate([dtq, dtk, dtv], axis=1)[:4]
    (du, ddt, dtx, dtb, dtc, g["ssd_dt_bias"], g["ssd_a_log"], g["ssd_d"], g["ssd_norm_w"]) = ssd_bwd(
        s["u"], p["scw"], p["sdtb"], p["salog"], p["sd"], p["snw"], s["sts"], dys, du)
    dconv = jnp.concatenate([dtx, dtb, dtc], axis=1)
    g["ssd_conv_w"], g["ssd_conv_b"] = dconv[:4], dconv[4]
    dqh, dkh, dvh, g["swa_sinks"] = swa_bwd(s["qh"], s["kh"], s["vh"], p["sink"], _head_major(dyw, SWA_Q_HEADS))
    mid = jnp.concatenate([_from_head_major(dqh), _from_head_major(dkh), _from_head_major(dvh), dba[0].astype(bf16),
                           ddt.astype(bf16), jnp.zeros((du.shape[0], C_MID_END - C_SDT - 128), bf16)], axis=1)
    du = lax.dynamic_update_slice(du, mid, (0, C_WQ))
    sent = send_w_in(_split_chips(_w_in_from_padded(wgrad(s["hn"], du, "in")), 1))
    dhn = mm(du, p["w_in_t"], out_dtype=f32, name=f"dg_in_{tag}")
    dh, g["norm1_w"] = rmsnorm_bwd(s["h"], p["n1"] + sent, dhn, dh2, name=f"norm1_bwd_{tag}")
    return dh, g


def kernel(x, meta_tokens, norm1_w, w_in, gdn_conv_w, gdn_a_log, gdn_dt_bias, gdn_norm_w, ssd_conv_w, ssd_conv_b, ssd_dt_bias, ssd_a_log, ssd_d, ssd_norm_w, swa_sinks, w_proj_gdn, w_proj_ssd, w_proj_swa, w_out, norm2_w, w_up, w_down, final_norm_w, loss_target, m_meta_tokens, m_norm1_w, m_w_in, m_gdn_conv_w, m_gdn_a_log, m_gdn_dt_bias, m_gdn_norm_w, m_ssd_conv_w, m_ssd_conv_b, m_ssd_dt_bias, m_ssd_a_log, m_ssd_d, m_ssd_norm_w, m_swa_sinks, m_w_proj_gdn, m_w_proj_ssd, m_w_proj_swa, m_w_out, m_norm2_w, m_w_up, m_w_down, m_final_norm_w, v_meta_tokens, v_norm1_w, v_w_in, v_gdn_conv_w, v_gdn_a_log, v_gdn_dt_bias, v_gdn_norm_w, v_ssd_conv_w, v_ssd_conv_b, v_ssd_dt_bias, v_ssd_a_log, v_ssd_d, v_ssd_norm_w, v_swa_sinks, v_w_proj_gdn, v_w_proj_ssd, v_w_proj_swa, v_w_out, v_norm2_w, v_w_up, v_w_down, v_final_norm_w):
    given = dict(locals())
    depth = norm1_w.shape[0]
    me = 2 * lax.axis_index("x") + lax.axis_index("y")

    me1 = jnp.reshape(me, (1,)).astype(jnp.int32)
    is_me = (jnp.arange(4, dtype=jnp.int32) == me)[:, None, None]

    def weight_slabs(l):
        return (w_in[l].astype(bf16),
                jnp.concatenate([given[n][l].reshape(-1, D_MODEL).astype(bf16) for n, _ in BIG]))

    slabs = [weight_slabs(l) for l in range(depth)]
    wsmall = _pack_rows([given[n] for n in SMALL_SHARDED], f32)
    ga0, gsmall = chip_exchange([slabs[0][0], wsmall], (False, False), name="gather_first")
    gathers, started = {}, jnp.zeros((), f32)
    for l in range(depth):
        for j in range(2):
            if (l, j) != (0, 0):
                gathers[l, j], token = exchange_start([slabs[l][j]], (False,), gsmall, name=f"gather_start_l{l}_{j}")
                started = started + token[0, 0]
    shard_shapes = [given[n].shape for n in SMALL_SHARDED]
    per_chip = [_unpack_rows(gsmall[s], shard_shapes) for s in range(4)]
    full = {n: jnp.concatenate([per_chip[s][i] for s in range(4)], axis=-1) for i, n in enumerate(SMALL_SHARDED)}

    def landed(l, j, after):
        (zone,) = exchange_wait(gathers[l, j], after, name=f"gather_wait_l{l}_{j}")
        return jnp.where(is_me, slabs[l][j][None], zone)

    def first_operands(l, ga, order):
        w_in_p = _w_in_to_padded(_join_chips(ga, 1))
        return dict(
            n1=_row8(norm1_w[l], width=D_MODEL) + order, n2=_row8(norm2_w[l], width=D_MODEL),
            w_in=w_in_p, w_in_t=w_in_p.T,
            gcw=jnp.pad(full["gdn_conv_w"][l], ((0, 4), (0, 0))),
            galog=_row8(gdn_a_log[l], 8), gdtb=_row8(gdn_dt_bias[l], 8), gnw=_row8(gdn_norm_w[l]),
            scw=jnp.pad(jnp.concatenate([full["ssd_conv_w"][l], ssd_conv_b[l][None]], axis=0), ((0, 3), (0, 0))),
            sdtb=_row8(ssd_dt_bias[l]), salog=_row8(ssd_a_log[l]), sd=_row8(ssd_d[l]),
            snw=_row8(ssd_norm_w[l], width=D_MODEL), sink=_row8(swa_sinks[l]))

    def late_operands(l, after):
        gb = landed(l, 1, after)
        w = {}
        for n, r in BIG:
            parts = gb[:, BIG_OFF[n]:BIG_OFF[n] + r].reshape((4,) + given[n].shape[1:])
            w[n] = _join_chips(parts, BIG_AXIS[n] - 1)
        return dict(wpg=w["w_proj_gdn"], wps=w["w_proj_ssd"], wpw=w["w_proj_swa"], wout=w["w_out"],
                    wup=w["w_up"], wdown=w["w_down"])

    h = jnp.concatenate([jnp.zeros((PAD, D_MODEL), f32), full["meta_tokens"], x[0]], axis=0)
    layers, saved = [], []
    for l in range(depth):
        p = first_operands(0, ga0, started) if l == 0 else first_operands(l, landed(l, 0, h), 0.0)
        h, s = _layer_fwd(h, p, l, late=functools.partial(late_operands, l))
        layers.append(p)
        saved.append(s)
    loss8, dh, dfw8 = loss_head(h, _row8(final_norm_w, width=D_MODEL), loss_target[0])
    grads = {"final_norm_w": dfw8[0]}
    per_layer, grad_slabs, scatters = [None] * depth, {}, {}

    def send(l, j, slab):
        grad_slabs[l, j] = slab
        scatters[l, j], token = exchange_start([slab], (True,), loss8, name=f"scatter_start_l{l}_{j}")
        return token[0, 0]

    for l in reversed(range(depth)):
        dh, per_layer[l] = _layer_bwd(dh, layers[l], saved[l], l, functools.partial(send, l, 1),
                                      functools.partial(send, l, 0))
    grad_x = dh[HEAD_ROWS:][None]
    grads["meta_tokens"] = dh[PAD:HEAD_ROWS]
    lane = {"gdn_a_log": (8, 8), "gdn_dt_bias": (8, 8), "gdn_norm_w": (0, 128), "ssd_dt_bias": (0, 16),
            "ssd_a_log": (0, 16), "ssd_d": (0, 16), "swa_sinks": (0, 16)}
    for n in per_layer[0]:
        parts = [per_layer[l][n] for l in range(depth)]
        if n in lane:
            parts = [q[0, lane[n][0]:lane[n][0] + lane[n][1]] for q in parts]
        elif n in ("norm1_w", "norm2_w", "ssd_norm_w"):
            parts = [q[0] for q in parts]
        grads[n] = jnp.stack(parts)
    loss = lax.psum(loss8[0, 0], ("x", "y", "c"))

    gs = _pack_rows([grads[n] for n in SMALL_NAMES], f32)
    def chip_sum(l, j, after):
        (zone,) = exchange_wait(scatters[l, j], after, name=f"scatter_wait_l{l}_{j}")
        own = lax.dynamic_index_in_dim(grad_slabs[l, j], me, 0, keepdims=False)
        return reduce4(zone, own=own, me=me1, name=f"sum_chips_l{l}_{j}")

    early = [(l, j) for l in range(depth) for j in range(2) if (l, j) != (0, 0)]
    mine = {lj: chip_sum(*lj, dh) for lj in early}
    sibs = dict(zip(early, sibling_swap([mine[lj] for lj in early], name="swap_cores_early")))
    out = {}
    for n, r in BIG:
        shp = given[n].shape
        res = adamw(*[given[pre + n].reshape(depth * r, D_MODEL) for pre in ("", "m_", "v_")],
                    [(mine[l, 1], sibs[l, 1]) for l in range(depth)], BIG_OFF[n], name=f"adamw_{n}")
        out[n] = [a.reshape(shp) for a in res]
    mine[0, 0] = chip_sum(0, 0, res[1])
    (rs,) = chip_exchange([gs], (False,), after=mine[0, 0], name="gather_small_grads")
    ps_ = reduce4(rs, name="sum_chips_small")
    sibs[0, 0], ss = sibling_swap([mine[0, 0], ps_], name="swap_cores_last")
    w_in_rows = depth * D_MODEL
    res = adamw(*[given[pre + "w_in"].reshape(w_in_rows, W_IN_SHARD) for pre in ("", "m_", "v_")],
                [(mine[l, 0], sibs[l, 0]) for l in range(depth)], 0, name="adamw_w_in")
    out["w_in"] = [a.reshape(w_in.shape) for a in res]
    full_shapes = [grads[n].shape for n in SMALL_NAMES]
    mine_s, sib_s = _unpack_rows(ps_, full_shapes), _unpack_rows(ss, full_shapes)

    def local(parts):
        loc = []
        for n, a in zip(SMALL_NAMES, parts):
            if n in SMALL_SHARDED:
                sz = a.shape[-1] // 4
                a = lax.dynamic_slice_in_dim(a, me * sz, sz, axis=a.ndim - 1)
            loc.append(a)
        return _pack_rows(loc, f32)

    res = adamw(_pack_rows([given[n] for n in SMALL_NAMES], f32), _pack_rows([given["m_" + n] for n in SMALL_NAMES], f32),
                _pack_rows([given["v_" + n] for n in SMALL_NAMES], f32), [(local(mine_s), local(sib_s))], 0,
                name="adamw_small")
    local_shapes = [given[n].shape for n in SMALL_NAMES]
    unpacked = [_unpack_rows(a, local_shapes) for a in res]
    for i, n in enumerate(SMALL_NAMES):
        out[n] = [unpacked[j][i] for j in range(4)]

    return (loss, grad_x) + tuple(out[n][j] for j in range(4) for n in W_NAMES)
```

```python
import functools

import jax
import jax.numpy as jnp
from jax import lax
from jax.experimental import pallas as pl
from jax.experimental.pallas import tpu as pltpu

f32 = jnp.float32
bf16 = jnp.bfloat16
HI = lax.Precision.HIGHEST

D_MODEL = 1024
N_META = 16
PAD = 112
HEAD_ROWS = PAD + N_META
RMS_EPS = 1e-6
L2_EPS = 1e-6
D_FF = 4 * D_MODEL

GDN_HEADS = 8
GDN_D = 128
GDN_CHUNK = 64
SSD_HEADS = 16
SSD_P = 64
SSD_GROUPS = 4
SSD_HPG = 4
SSD_N = 128
SSD_CHUNK = 128
SWA_Q_HEADS = 16
SWA_KV_HEADS = 4
SWA_REP = 4
SWA_D = 64
SWA_W = 128

C_GATE = 0
C_SZ, C_SX, C_SB, C_SC = 3072, 4096, 5120, 5632
C_WQ, C_WK, C_WV = 6144, 7168, 7424
C_BA = 7680
C_SDT = 7808
C_MID_END = 8192
C_GQ, C_GK, C_GV, C_GG = 8192, 9216, 10240, 11264
IN_WP = 12288
IN_W = 11808

ADAM_LR, ADAM_B1, ADAM_B2, ADAM_EPS, ADAM_WD, ADAM_STEP = 0.001, 0.9, 0.999, 1e-08, 0.01, 10

VMEM_LIMIT = 56 * 1024 * 1024
BLOCK_BYTES = 3 << 19
MM_OPERAND_BYTES = 9 << 20
MM_RESIDENT_BYTES = 13 << 20

NN = (((1,), (0,)), ((), ()))
NT = (((1,), (1,)), ((), ()))
TN = (((0,), (0,)), ((), ()))


def _dot(a, b, dims=NN):
    return lax.dot_general(a.astype(bf16), b.astype(bf16), dims, preferred_element_type=f32)


def _dotx(a, b, dims=NN):
    return lax.dot_general(a, b, dims, preferred_element_type=f32, precision=lax.Precision.HIGH)


def _iota(shape, axis):
    return lax.broadcasted_iota(jnp.int32, shape, axis)


def _softplus(x):
    return jnp.maximum(x, 0.0) + jnp.log1p(jnp.exp(-jnp.abs(x)))


def _silu(x):
    return x * jax.nn.sigmoid(x)


def _params(sem):
    return pltpu.CompilerParams(dimension_semantics=sem, vmem_limit_bytes=VMEM_LIMIT)


@functools.partial(jax.custom_vjp, nondiff_argnums=(1,))
def _window(x_ext, off):
    n = x_ext.shape[0] - 8
    if off == 8:
        return x_ext[8:]
    return pltpu.roll(x_ext, 8 - off, 0)[8:]


def _window_fwd(x_ext, off):
    return _window(x_ext, off), None


def _window_bwd(off, _, g):
    n, w = g.shape
    g_ext = jnp.concatenate([jnp.zeros((8, w), g.dtype), g], axis=0)
    if off == 8:
        return (g_ext,)
    return (pltpu.roll(g_ext, n + off, 0),)


_window.defvjp(_window_fwd, _window_bwd)


def _conv4(x, halo, taps):
    x_ext = jnp.concatenate([halo, x], axis=0)
    y = taps[3] * x
    for j in range(3):
        y = y + taps[j] * _window(x_ext, 5 + j)
    return y


def _blockinv_impl(a):
    n = a.shape[0]
    ri, ci = _iota((n, n), 0), _iota((n, n), 1)
    t = (ri == ci).astype(f32)
    k = 0
    while (1 << k) < n:
        sel = ((ri >> (k + 1)) == (ci >> (k + 1))) & (((ri >> k) & 1) == 1) & (((ci >> k) & 1) == 0)
        o = jnp.where(sel, a, 0.0)
        t = t - _dotx(_dotx(t, o), t)
        k += 1
    return t


@jax.custom_vjp
def _blockinv(a):
    return _blockinv_impl(a)


def _blockinv_fwd(a):
    t = _blockinv_impl(a)
    return t, t


def _blockinv_bwd(t, dt):
    return (-_dotx(_dotx(t, dt, TN), t, NT),)


_blockinv.defvjp(_blockinv_fwd, _blockinv_bwd)


@jax.custom_vjp
def _blockinv_given(a, t):
    return t


_blockinv_given.defvjp(lambda a, t: (t, t), lambda t, dt: _blockinv_bwd(t, dt) + (jnp.zeros_like(t),))


def _scan_rows(x, reverse):
    n = x.shape[0]
    row = _iota(x.shape, 0)
    s = 1
    while s < n:
        if reverse:
            x = x + jnp.where(row < n - s, pltpu.roll(x, n - s, 0), 0.0)
        else:
            x = x + jnp.where(row >= s, pltpu.roll(x, s, 0), 0.0)
        s *= 2
    return x


@jax.custom_vjp
def _cumsum_rows(x):
    return _scan_rows(x, False)


_cumsum_rows.defvjp(lambda x: (_scan_rows(x, False), None), lambda _, g: (_scan_rows(g, True),))


def _gdn_act(xq, xk, xv, hq, hk, hv, tq, tk, tv):
    return _silu(_conv4(xq, hq, tq)), _silu(_conv4(xk, hk, tk)), _silu(_conv4(xv, hv, tv))


def _gdn_core(q, k, v, gate, mb, mg, mr, s, t_given, beta16, g16, gam16, gam16_t, nw):
    c = GDN_CHUNK
    q = q * lax.rsqrt(jnp.sum(q * q, axis=1, keepdims=True) + L2_EPS) * (GDN_D ** -0.5)
    k = k * lax.rsqrt(jnp.sum(k * k, axis=1, keepdims=True) + L2_EPS)

    pick = lambda x, m: jnp.sum(x * m, axis=1, keepdims=True)
    beta = pick(beta16, mb)
    g = jnp.broadcast_to(pick(g16, mg), (c, GDN_D))
    gam1 = pick(gam16, mg)
    gam = jnp.broadcast_to(gam1, (c, GDN_D))
    gam_j = jnp.broadcast_to(jnp.sum(gam16_t * mr, axis=0, keepdims=True), (c, c))

    ri, ci = _iota((c, c), 0), _iota((c, c), 1)
    incl = ci <= ri
    decay = jnp.where(incl, jnp.exp(jnp.where(incl, jnp.broadcast_to(gam1, (c, c)) - gam_j, 0.0)), 0.0)

    kb = k * beta
    a = jnp.where(ci < ri, _dot(kb, k, NT) * decay, 0.0)
    t = _blockinv(a) if t_given is None else _blockinv_given(a, t_given)
    egam = jnp.exp(gam)
    u = _dotx(t, v * beta)
    w = _dotx(t, kb * egam)
    attn = _dot(q, k, NT) * decay
    gl = jnp.sum(g, axis=0, keepdims=True)
    kt = k * jnp.exp(gl - gam)
    v_new = u - _dot(w, s)
    o = _dot(q * egam, s) + _dot(attn, v_new)
    s_out = s * jnp.exp(gl) + _dot(kt, v_new, TN)

    y = o * lax.rsqrt(jnp.mean(o * o, axis=1, keepdims=True) + RMS_EPS) * nw * _silu(gate)
    return y, s_out, t


def _gdn_chunk(q, k, v, gate, s, t_given, ba, alog, dtb, nw, *, masks, row0):
    valid = (row0 + _iota((GDN_CHUNK, 1), 0)) >= PAD
    beta16 = jnp.where(valid, jax.nn.sigmoid(ba), 0.0)
    g16 = jnp.where(valid, -jnp.exp(alog) * _softplus(ba + dtb), 0.0)
    gam16 = _cumsum_rows(g16)
    core = jax.vmap(_gdn_core, in_axes=(0,) * 8 + (None if t_given is None else 0,) + (None,) * 5)
    y, s_out, t = core(q, k, v, gate, *masks, s, t_given, beta16, g16, gam16, gam16.T, nw)
    return (y, s_out, t) if t_given is None else (y, s_out)


def _gdn_specs(hb, nc, rev):
    w = hb * GDN_D
    cw = D_MODEL // w

    def cidx(c):
        return (nc - 1 - c) if rev else c

    def col(base):
        return pl.BlockSpec((GDN_CHUNK, w), lambda h, c: (cidx(c), base // w + h))

    def halo(base):
        return pl.BlockSpec((8, w), lambda h, c: (jnp.maximum(cidx(c) * (GDN_CHUNK // 8) - 1, 0), base // w + h))

    def taps(base):
        return pl.BlockSpec((8, w), lambda h, c: (0, base // w + h))

    ba = pl.BlockSpec((GDN_CHUNK, 128), lambda h, c: (cidx(c), C_BA // 128))
    row = pl.BlockSpec((8, 128), lambda h, c: (0, 0))
    y = pl.BlockSpec((GDN_CHUNK, w), lambda h, c: (cidx(c), h))
    st = pl.BlockSpec((1, hb, GDN_D, GDN_D), lambda h, c: (cidx(c), h, 0, 0))
    in_specs = [col(C_GQ), col(C_GK), col(C_GV), halo(C_GQ), halo(C_GK), halo(C_GV), col(C_GG), ba,
                taps(0), taps(1024), taps(2048), row, row, row]
    return in_specs, y, st, taps, row, col, ba


def _gdn_load(refs, first):
    xq, xk, xv, hq, hk, hv, gate, ba, tq, tk, tv, alog, dtb, nw = refs

    def halo(r):
        return jnp.where(first, 0.0, r[...])

    def taps(r):
        return tuple(r[j:j + 1, :] for j in range(4))

    act = (xq[...], xk[...], xv[...], halo(hq), halo(hk), halo(hv), taps(tq), taps(tk), taps(tv))
    return act, gate[...], (ba[...], alog[0:1, :], dtb[0:1, :], nw[0:1, :])


def _heads(a, hb):
    return jnp.stack([a[:, i * GDN_D:(i + 1) * GDN_D] for i in range(hb)])


def _wide(a):
    return jnp.concatenate([a[i] for i in range(a.shape[0])], axis=1)


def _head_masks(hblk, hb):
    head = hblk * hb + _iota((hb, 1, 128), 0)
    lane = _iota((hb, 1, 128), 2)
    rows = (_iota((hb, 128, 1), 1) == hblk * hb + _iota((hb, 128, 1), 0) + 8).astype(f32)
    return (lane == head).astype(f32), (lane == head + 8).astype(f32), rows


def gdn_fwd(u, conv_w8, alog8, dtb8, nw8, *, hb=8):
    t_rows = u.shape[0]
    nc = t_rows // GDN_CHUNK
    in_specs, y_spec, st_spec, *_ = _gdn_specs(hb, nc, False)

    def body(*refs):
        ins, (y_ref, st_ref, t_ref), (s_scr,) = refs[:14], refs[14:17], refs[17:]
        hblk, c = pl.program_id(0), pl.program_id(1)

        @pl.when(c == 0)
        def _():
            s_scr[...] = jnp.zeros_like(s_scr)

        act, gate, shared = _gdn_load(ins, c == 0)
        s = s_scr[...]
        st_ref[0] = s
        qa, ka, va = _gdn_act(*act)
        y, s_new, t = _gdn_chunk(_heads(qa, hb), _heads(ka, hb), _heads(va, hb), _heads(gate, hb), s, None, *shared,
                                 masks=_head_masks(hblk, hb), row0=c * GDN_CHUNK)
        y_ref[...] = _wide(y).astype(bf16)
        t_ref[0] = t
        s_scr[...] = s_new

    return pl.pallas_call(
        body, name="gdn_fwd", grid=(GDN_HEADS // hb, nc),
        in_specs=in_specs,
        out_specs=(y_spec, st_spec, pl.BlockSpec((1, hb, GDN_CHUNK, GDN_CHUNK), lambda h, c: (c, h, 0, 0))),
        out_shape=(jax.ShapeDtypeStruct((t_rows, D_MODEL), bf16),
                   jax.ShapeDtypeStruct((nc, GDN_HEADS, GDN_D, GDN_D), f32),
                   jax.ShapeDtypeStruct((nc, GDN_HEADS, GDN_CHUNK, GDN_CHUNK), f32)),
        scratch_shapes=[pltpu.VMEM((hb, GDN_D, GDN_D), f32)],
        compiler_params=_params(("arbitrary", "arbitrary")),
    )(u, u, u, u, u, u, u, u, conv_w8, conv_w8, conv_w8, alog8, dtb8, nw8)


def gdn_bwd(u, conv_w8, alog8, dtb8, nw8, states, tinv, dy, du):
    t_rows = u.shape[0]
    nc = t_rows // GDN_CHUNK
    hb = GDN_HEADS
    w = hb * GDN_D
    in_specs, y_spec, st_spec, taps, row, col, ba = _gdn_specs(hb, nc, True)
    nhb = GDN_HEADS // hb

    def body(*refs):
        ins, st_ref, t_ref, dy_ref = refs[:14], refs[14], refs[15], refs[16]
        du_ref, dba_ref, dtq_ref, dtk_ref, dtv_ref, dalog_ref, ddtb_ref, dnw_ref = refs[18:26]
        ds_scr, dh_scr = refs[26:]
        hblk, cc = pl.program_id(0), pl.program_id(1)
        c = nc - 1 - cc

        @pl.when(cc == 0)
        def _():
            ds_scr[...] = jnp.zeros_like(ds_scr)
            dh_scr[...] = jnp.zeros_like(dh_scr)
            dtq_ref[...] = jnp.zeros_like(dtq_ref)
            dtk_ref[...] = jnp.zeros_like(dtk_ref)
            dtv_ref[...] = jnp.zeros_like(dtv_ref)

        @pl.when((cc == 0) & (hblk == 0))
        def _():
            dalog_ref[...] = jnp.zeros_like(dalog_ref)
            ddtb_ref[...] = jnp.zeros_like(ddtb_ref)
            dnw_ref[...] = jnp.zeros_like(dnw_ref)

        act, gate, shared = _gdn_load(ins, c == 0)
        (qa, ka, va), vjp_act = jax.vjp(_gdn_act, *act)
        chunk = functools.partial(_gdn_chunk, masks=_head_masks(hblk, hb), row0=c * GDN_CHUNK)
        _, vjp_core = jax.vjp(chunk, _heads(qa, hb), _heads(ka, hb), _heads(va, hb), _heads(gate, hb), st_ref[0],
                              t_ref[0], *shared)
        dqa, dka, dva, dgate, ds, _, dba, dalog, ddtb, dnw = vjp_core(
            (_heads(dy_ref[...].astype(f32), hb), ds_scr[...]))
        ds_scr[...] = ds
        dxq, dxk, dxv, dhq, dhk, dhv, dtq, dtk, dtv = vjp_act((_wide(dqa), _wide(dka), _wide(dva)))
        zeros = jnp.zeros((GDN_CHUNK - 8, w), f32)
        for j, (dx, dh) in enumerate(((dxq, dhq), (dxk, dhk), (dxv, dhv))):
            du_ref[:, j * w:(j + 1) * w] = (dx + jnp.concatenate([zeros, dh_scr[j]], axis=0)).astype(bf16)
            dh_scr[j] = dh
        du_ref[:, 3 * w:4 * w] = _wide(dgate).astype(bf16)
        dba_ref[0] = dba
        for dt_ref, dtaps in ((dtq_ref, dtq), (dtk_ref, dtk), (dtv_ref, dtv)):
            for j in range(4):
                dt_ref[j:j + 1, :] += dtaps[j]
        dalog_ref[0:1, :] += dalog
        ddtb_ref[0:1, :] += ddtb
        dnw_ref[0:1, :] += dnw

    out_specs = (pl.BlockSpec((GDN_CHUNK, 4 * w), lambda h, c: (nc - 1 - c, C_GQ // (4 * w))),
                 pl.BlockSpec((1, GDN_CHUNK, 128), lambda h, c: (h, nc - 1 - c, 0)),
                 taps(0), taps(0), taps(0), row, row, row)
    out_shape = (jax.ShapeDtypeStruct(du.shape, du.dtype),
                 jax.ShapeDtypeStruct((nhb, t_rows, 128), f32),
                 jax.ShapeDtypeStruct((8, D_MODEL), f32), jax.ShapeDtypeStruct((8, D_MODEL), f32),
                 jax.ShapeDtypeStruct((8, D_MODEL), f32),
                 jax.ShapeDtypeStruct((8, 128), f32), jax.ShapeDtypeStruct((8, 128), f32), jax.ShapeDtypeStruct((8, 128), f32))
    return pl.pallas_call(
        body, name="gdn_bwd", grid=(nhb, nc),
        in_specs=in_specs + [st_spec, pl.BlockSpec((1, hb, GDN_CHUNK, GDN_CHUNK), lambda h, c: (nc - 1 - c, h, 0, 0)),
                             y_spec, ANY],
        out_specs=out_specs, out_shape=out_shape, input_output_aliases={17: 0},
        scratch_shapes=[pltpu.VMEM((hb, GDN_D, GDN_D), f32), pltpu.VMEM((3, 8, w), f32)],
        compiler_params=_params(("arbitrary", "arbitrary")),
    )(u, u, u, u, u, u, u, u, conv_w8, conv_w8, conv_w8, alog8, dtb8, nw8, states, tinv, dy, du)


def _ssd_act(xs_r, b_r, c_r, hx, hbm, hcm, tx, tb, tc, bx, bb, bc, *, row0):
    valid = (row0 + _iota((SSD_CHUNK, 1), 0)) >= PAD
    act = lambda x, h, t, b: jnp.where(valid, _silu(_conv4(x, h, t) + b), 0.0)
    return act(xs_r, hx, tx, bx), act(b_r, hbm, tb, bb), act(c_r, hcm, tc, bc)


def _ssd_core(xs, bm, cm, z, nw, lanes, rows, h, dtp16, adt16, acum16, acum16_t, dsk):
    n = SSD_CHUNK
    pick = lambda x, m: jnp.sum(x * m, axis=1, keepdims=True)
    lane_r = _iota((1, 256), 1) >> 6
    dtp = jnp.zeros((n, 256), f32)
    adt = jnp.zeros((n, 256), f32)
    acum = jnp.zeros((n, 256), f32)
    dlane = jnp.zeros((1, 256), f32)
    ccols = []
    for r in range(SSD_HPG):
        ccols.append(pick(acum16, lanes[r]))
        dtp = jnp.where(lane_r == r, pick(dtp16, lanes[r]), dtp)
        adt = jnp.where(lane_r == r, pick(adt16, lanes[r]), adt)
        acum = jnp.where(lane_r == r, ccols[r], acum)
        dlane = jnp.where(lane_r == r, pick(dsk, lanes[r]), dlane)

    ri, ci = _iota((n, n), 0), _iota((n, n), 1)
    incl = ci <= ri
    al = jnp.sum(adt, axis=0, keepdims=True)
    xdt = xs * dtp
    cb = _dot(cm, bm, NT)
    y = _dot(cm, h) * jnp.exp(acum) + dlane * xs
    for r in range(SSD_HPG):
        ai = jnp.broadcast_to(ccols[r], (n, n))
        aj = jnp.broadcast_to(jnp.sum(acum16_t * rows[r], axis=0, keepdims=True), (n, n))
        lm = jnp.where(incl, jnp.exp(jnp.where(incl, ai - aj, 0.0)), 0.0)
        y = y + _dot(cb * lm, jnp.where(lane_r == r, xdt, 0.0))
    h_out = h * jnp.exp(al) + _dot(bm, jnp.exp(al - acum) * xdt, TN)
    y = y * _silu(z)
    y = y * lax.rsqrt(jnp.mean(y * y, axis=1, keepdims=True) + RMS_EPS) * nw
    return y, h_out


def _ssd_chunk(xs, bm, cm, z, nw, h, dt, dtb, alog, dsk, *, row0):
    valid = (row0 + _iota((SSD_CHUNK, 1), 0)) >= PAD
    dtp16 = jnp.where(valid, _softplus(dt + dtb), 0.0)
    adt16 = -jnp.exp(alog) * dtp16
    acum16 = _cumsum_rows(adt16)
    lanes = tuple((_iota((SSD_GROUPS, 1, 128), 2) == _iota((SSD_GROUPS, 1, 128), 0) * SSD_HPG + r).astype(f32)
                  for r in range(SSD_HPG))
    rows = tuple((_iota((SSD_GROUPS, 128, 1), 1) == _iota((SSD_GROUPS, 128, 1), 0) * SSD_HPG + r).astype(f32)
                 for r in range(SSD_HPG))
    core = jax.vmap(_ssd_core, in_axes=(0,) * 8 + (None,) * 5)
    return core(xs, bm, cm, z, nw, lanes, rows, h, dtp16, adt16, acum16, acum16.T, dsk)


def _ssd_specs(nc, rev):
    n = SSD_CHUNK

    def cidx(c):
        return (nc - 1 - c) if rev else c

    def col(base, w):
        return pl.BlockSpec((n, w), lambda c: (cidx(c), base // w))

    def halo(base, w):
        return pl.BlockSpec((8, w), lambda c: (jnp.maximum(cidx(c) * (n // 8) - 1, 0), base // w))

    def taps(base, w):
        return pl.BlockSpec((8, w), lambda c: (0, base // w))

    row = pl.BlockSpec((8, 128), lambda c: (0, 0))
    in_specs = [col(C_SX, 1024), col(C_SB, 512), col(C_SC, 512), halo(C_SX, 1024), halo(C_SB, 512), halo(C_SC, 512),
                col(C_SZ, 1024), col(C_SDT, 128), taps(0, 1024), taps(1024, 512), taps(1536, 512), row, row, row,
                taps(0, 1024)]
    y = pl.BlockSpec((n, D_MODEL), lambda c: (cidx(c), 0))
    st = pl.BlockSpec((1, SSD_GROUPS, SSD_N, 256), lambda c: (cidx(c), 0, 0, 0))
    return in_specs, y, st, col, taps, row


def _ssd_load(refs, first):
    xs, bm, cm, hx, hbm, hcm, z, dt, tx, tb, tc, dtb, alog, dsk, nw = refs

    def halo(r):
        return jnp.where(first, 0.0, r[...])

    def taps(r):
        return tuple(r[j:j + 1, :] for j in range(4))

    act = (xs[...], bm[...], cm[...], halo(hx), halo(hbm), halo(hcm), taps(tx), taps(tb), taps(tc),
           tx[4:5, :], tb[4:5, :], tc[4:5, :])
    return act, (z[...], nw[0:1, :]), (dt[...], dtb[0:1, :], alog[0:1, :], dsk[0:1, :])


def _groups(a, w):
    return jnp.stack([a[:, i * w:(i + 1) * w] for i in range(SSD_GROUPS)])


def ssd_fwd(u, conv_w8, dtb8, alog8, d8, nw8):
    t_rows = u.shape[0]
    nc = t_rows // SSD_CHUNK
    in_specs, y_spec, st_spec, *_ = _ssd_specs(nc, False)

    def body(*refs):
        ins, (y_ref, st_ref), (h_scr,) = refs[:15], refs[15:17], refs[17:]
        c = pl.program_id(0)

        @pl.when(c == 0)
        def _():
            h_scr[...] = jnp.zeros_like(h_scr)

        act, (z, nw), shared = _ssd_load(ins, c == 0)
        h = h_scr[...]
        st_ref[0] = h
        xs, bm, cm = _ssd_act(*act, row0=c * SSD_CHUNK)
        y, h_new = _ssd_chunk(_groups(xs, 256), _groups(bm, 128), _groups(cm, 128), _groups(z, 256),
                              _groups(nw, 256), h, *shared, row0=c * SSD_CHUNK)
        y_ref[...] = _wide(y).astype(bf16)
        h_scr[...] = h_new

    return pl.pallas_call(
        body, name="ssd_fwd", grid=(nc,), in_specs=in_specs, out_specs=(y_spec, st_spec),
        out_shape=(jax.ShapeDtypeStruct((t_rows, D_MODEL), bf16),
                   jax.ShapeDtypeStruct((nc, SSD_GROUPS, SSD_N, 256), f32)),
        scratch_shapes=[pltpu.VMEM((SSD_GROUPS, SSD_N, 256), f32)],
        compiler_params=_params(("arbitrary",)),
    )(u, u, u, u, u, u, u, u, conv_w8, conv_w8, conv_w8, dtb8, alog8, d8, nw8)


def ssd_bwd(u, conv_w8, dtb8, alog8, d8, nw8, states, dy, du):
    t_rows = u.shape[0]
    nc = t_rows // SSD_CHUNK
    n = SSD_CHUNK
    in_specs, y_spec, st_spec, col, taps, row = _ssd_specs(nc, True)

    def body(*refs):
        ins, st_ref, dy_ref = refs[:15], refs[15], refs[16]
        du_ref, ddt_ref, dtx_ref, dtb_ref, dtc_ref, ddtb_ref, dalog_ref, ddsk_ref, dnw_ref = refs[18:27]
        dh_scr, hx_scr, hb_scr, hc_scr = refs[27:]
        cc = pl.program_id(0)
        c = nc - 1 - cc

        @pl.when(cc == 0)
        def _():
            for r in (dh_scr, hx_scr, hb_scr, hc_scr, dtx_ref, dtb_ref, dtc_ref, dnw_ref, ddtb_ref, dalog_ref, ddsk_ref):
                r[...] = jnp.zeros_like(r)

        act, (z, nw), shared = _ssd_load(ins, c == 0)
        (xs, bm, cm), vjp_act = jax.vjp(functools.partial(_ssd_act, row0=c * n), *act)
        _, vjp_core = jax.vjp(functools.partial(_ssd_chunk, row0=c * n), _groups(xs, 256), _groups(bm, 128),
                              _groups(cm, 128), _groups(z, 256), _groups(nw, 256), st_ref[0], *shared)
        dxa, dba, dca, dz, dnw, dh, ddt, ddtb, dalog, ddsk = vjp_core(
            (_groups(dy_ref[...].astype(f32), 256), dh_scr[...]))
        dh_scr[...] = dh
        dxs, dbm, dcm, dhx, dhb, dhc, dtx, dtb, dtc, dbx, dbb, dbc = vjp_act((_wide(dxa), _wide(dba), _wide(dca)))
        du_ref[:, 0:D_MODEL] = _wide(dz).astype(bf16)
        for dx, dhalo, scr, lo in ((dxs, dhx, hx_scr, C_SX), (dbm, dhb, hb_scr, C_SB), (dcm, dhc, hc_scr, C_SC)):
            zeros = jnp.zeros((n - 8, dx.shape[1]), f32)
            du_ref[:, lo - C_SZ:lo - C_SZ + dx.shape[1]] = (dx + jnp.concatenate([zeros, scr[...]], axis=0)).astype(bf16)
            scr[...] = dhalo
        ddt_ref[...] = ddt
        for ref, dtaps, dbias in ((dtx_ref, dtx, dbx), (dtb_ref, dtb, dbb), (dtc_ref, dtc, dbc)):
            for j in range(4):
                ref[j:j + 1, :] += dtaps[j]
            ref[4:5, :] += dbias
        ddtb_ref[0:1, :] += ddtb
        dalog_ref[0:1, :] += dalog
        ddsk_ref[0:1, :] += ddsk
        dnw_ref[0:1, :] += _wide(dnw)

    def out_col(w):
        return pl.BlockSpec((n, w), lambda c: (nc - 1 - c, 0))

    out_specs = (pl.BlockSpec((n, 3 * D_MODEL), lambda c: (nc - 1 - c, C_SZ // (3 * D_MODEL))), out_col(128),
                 taps(0, D_MODEL), taps(0, 512), taps(0, 512), row, row, row, taps(0, D_MODEL))
    out_shape = (jax.ShapeDtypeStruct(du.shape, du.dtype),
                 jax.ShapeDtypeStruct((t_rows, 128), f32),
                 jax.ShapeDtypeStruct((8, D_MODEL), f32), jax.ShapeDtypeStruct((8, 512), f32),
                 jax.ShapeDtypeStruct((8, 512), f32),
                 jax.ShapeDtypeStruct((8, 128), f32), jax.ShapeDtypeStruct((8, 128), f32),
                 jax.ShapeDtypeStruct((8, 128), f32), jax.ShapeDtypeStruct((8, D_MODEL), f32))
    return pl.pallas_call(
        body, name="ssd_bwd", grid=(nc,), in_specs=in_specs + [st_spec, y_spec, ANY],
        out_specs=out_specs, out_shape=out_shape, input_output_aliases={17: 0},
        scratch_shapes=[pltpu.VMEM((SSD_GROUPS, SSD_N, 256), f32), pltpu.VMEM((8, D_MODEL), f32),
                        pltpu.VMEM((8, 512), f32), pltpu.VMEM((8, 512), f32)],
        compiler_params=_params(("arbitrary",)),
    )(u, u, u, u, u, u, u, u, conv_w8, conv_w8, conv_w8, dtb8, alog8, d8, nw8, states, dy, du)


NEG = -1e30


def _swa_core(q, kc, kp, km, vc, vp, vm, sink, *, n):
    rows = SWA_REP * SWA_W
    ri, ci = _iota((rows, SWA_W), 0) & (SWA_W - 1), _iota((rows, SWA_W), 1)
    causal = ci <= ri
    m_band = (causal & ((n >= 1) | ((ci >= PAD) & (ri >= PAD)))) | ((ci > ri) & (n >= 2))
    m_meta = (n >= 1) & (ci >= PAD)
    q = q * (SWA_D ** -0.5)
    s = jnp.where(m_band, jnp.where(causal, _dot(q, kc, NT), _dot(q, kp, NT)), NEG)
    sm = jnp.where(m_meta, _dot(q, km, NT), NEG)
    mx = jnp.maximum(jnp.maximum(jnp.max(s, axis=1, keepdims=True), jnp.max(sm, axis=1, keepdims=True)), sink)
    mx = lax.stop_gradient(mx)
    e, em = jnp.exp(s - mx), jnp.exp(sm - mx)
    den = jnp.sum(e, axis=1, keepdims=True) + jnp.sum(em, axis=1, keepdims=True) + jnp.exp(sink - mx)
    return (_dot(jnp.where(causal, e, 0.0), vc) + _dot(jnp.where(causal, 0.0, e), vp) + _dot(em, vm)) / den


def _swa_block(q16, kc, kp, km, vc, vp, vm, sink16, *, n):
    rows = SWA_REP * SWA_W
    lane = _iota((1, 128), 1)
    rep = _iota((rows, 1), 0) >> 7
    cols = []
    for h in range(SWA_KV_HEADS):
        col = jnp.zeros((rows, 1), f32)
        for r in range(SWA_REP):
            s = jnp.sum(jnp.where(lane == h * SWA_REP + r, sink16, 0.0), axis=1, keepdims=True)
            col = jnp.where(rep == r, s, col)
        cols.append(col)
    o = jax.vmap(functools.partial(_swa_core, n=n))(q16.reshape(SWA_KV_HEADS, rows, SWA_D), kc, kp, km, vc, vp, vm,
                                                    jnp.concatenate([col[None] for col in cols], axis=0))
    return o.reshape(q16.shape)


def _swa_specs(nb, rev):
    def bidx(n):
        return (nb - 1 - n) if rev else n

    q = pl.BlockSpec((SWA_Q_HEADS, SWA_W, SWA_D), lambda n: (0, bidx(n), 0))
    cur = pl.BlockSpec((SWA_KV_HEADS, SWA_W, SWA_D), lambda n: (0, bidx(n), 0))
    prev = pl.BlockSpec((SWA_KV_HEADS, SWA_W, SWA_D), lambda n: (0, jnp.maximum(bidx(n) - 1, 0), 0))
    meta = pl.BlockSpec((SWA_KV_HEADS, SWA_W, SWA_D), lambda n: (0, 0, 0))
    row = pl.BlockSpec((8, 128), lambda n: (0, 0))
    return [q, cur, prev, meta, cur, prev, meta, row], q, cur, row


def swa_fwd(q, k, v, sink8):
    t_rows = q.shape[1]
    nb = t_rows // SWA_W
    in_specs, q_spec, _, _ = _swa_specs(nb, False)

    def body(q_ref, kc, kp, km, vc, vp, vm, sink_ref, o_ref):
        o_ref[...] = _swa_block(q_ref[...], kc[...], kp[...], km[...], vc[...], vp[...], vm[...], sink_ref[0:1, :],
                                n=pl.program_id(0)).astype(bf16)

    return pl.pallas_call(
        body, name="swa_fwd", grid=(nb,), in_specs=in_specs, out_specs=q_spec,
        out_shape=jax.ShapeDtypeStruct(q.shape, bf16),
        compiler_params=_params(("arbitrary",)),
    )(q, k, k, k, v, v, v, sink8)


def swa_bwd(q, k, v, sink8, do):
    t_rows = q.shape[1]
    nb = t_rows // SWA_W
    in_specs, q_spec, kv_spec, row = _swa_specs(nb, True)

    def body(q_ref, kc, kp, km, vc, vp, vm, sink_ref, do_ref, dq_ref, dk_ref, dv_ref, dsink_ref,
             dkp_scr, dvp_scr, dkm_scr, dvm_scr):
        nn = pl.program_id(0)
        n = nb - 1 - nn

        @pl.when(nn == 0)
        def _():
            for r in (dkp_scr, dvp_scr, dkm_scr, dvm_scr, dsink_ref):
                r[...] = jnp.zeros_like(r)

        fn = functools.partial(_swa_block, n=n)
        _, vjp = jax.vjp(fn, q_ref[...], kc[...], kp[...], km[...], vc[...], vp[...], vm[...], sink_ref[0:1, :])
        dq, dkc, dkp, dkm, dvc, dvp, dvm, dsink = vjp(do_ref[...].astype(f32))
        dq_ref[...] = dq.astype(bf16)
        dkm_scr[...] += dkm
        dvm_scr[...] += dvm
        first = n == 0
        dk_ref[...] = (dkc + dkp_scr[...] + jnp.where(first, dkm_scr[...], 0.0)).astype(bf16)
        dv_ref[...] = (dvc + dvp_scr[...] + jnp.where(first, dvm_scr[...], 0.0)).astype(bf16)
        dkp_scr[...] = dkp
        dvp_scr[...] = dvp
        dsink_ref[0:1, :] += dsink

    kv_shape = jax.ShapeDtypeStruct(k.shape, bf16)
    return pl.pallas_call(
        body, name="swa_bwd", grid=(nb,), in_specs=in_specs + [q_spec],
        out_specs=(q_spec, kv_spec, kv_spec, row),
        out_shape=(jax.ShapeDtypeStruct(q.shape, bf16), kv_shape, kv_shape, jax.ShapeDtypeStruct((8, 128), f32)),
        scratch_shapes=[pltpu.VMEM((SWA_KV_HEADS, SWA_W, SWA_D), f32)] * 4,
        compiler_params=_params(("arbitrary",)),
    )(q, k, k, k, v, v, v, sink8, do)


def _tile(dim, prefs):
    for p in prefs:
        if dim % p == 0:
            return p
    return dim


def mm(a, b, *, out_dtype, name, resid=None, relu_grad_of=None, relu2_out=False, ta=False, tb=False):
    assert resid is None or relu_grad_of is None
    k, m = (a.shape if ta else a.shape[::-1])
    n = b.shape[0] if tb else b.shape[1]
    rhs_stays = k * 2 * 1024 > MM_OPERAND_BYTES
    assert not (ta and rhs_stays)
    if rhs_stays:
        tn = _tile(n, tuple(p for p in (512, 256, 128) if p * k * 2 <= MM_RESIDENT_BYTES))
        tm = _tile(m, tuple(p for p in (512, 384, 256, 128) if p * k * 2 <= MM_OPERAND_BYTES // 2))
        grid = (n // tn, m // tm)
        ij = lambda o, i: (i, o)
    elif k * n * b.dtype.itemsize <= MM_OPERAND_BYTES:
        tn = n
        tm = _tile(m, tuple(p for p in (1408, 1024, 512, 384, 256, 128)
                            if p * k * 2 <= MM_OPERAND_BYTES and p * n * 4 <= MM_OPERAND_BYTES * 2 // 3))
        grid = (m // tm, 1)
        ij = lambda o, i: (o, i)
    else:
        tm = _tile(m, tuple(p for p in (1408, 1024, 512, 384, 256, 128) if p * k * 2 <= MM_OPERAND_BYTES))
        tn = _tile(n, tuple(p for p in (512, 256, 128) if p * k * 2 <= MM_OPERAND_BYTES // 2))
        grid = (m // tm, n // tn)
        ij = lambda o, i: (o, i)

    extra = resid if resid is not None else relu_grad_of

    def body(*refs):
        a_ref, b_ref = refs[:2]
        if ta:
            at_scr = refs[-1]
            refs = refs[:-1]

            @pl.when(pl.program_id(1) == 0)
            def _():
                at_scr[...] = a_ref[...].T

            lhs = at_scr[...]
        else:
            lhs = a_ref[...]
        o = _dot(lhs, b_ref[...], NT if tb else NN)
        if resid is not None:
            o = o + refs[2][...]
        if relu_grad_of is not None:
            o = o * (2.0 * jnp.maximum(refs[2][...], 0.0))
        if relu2_out:
            refs[-2][...] = o.astype(out_dtype)
            r = jnp.maximum(o, 0.0)
            refs[-1][...] = (r * r).astype(bf16)
        else:
            refs[-1][...] = o.astype(out_dtype)

    in_specs = [pl.BlockSpec((k, tm), lambda o, i: (0, ij(o, i)[0])) if ta
                else pl.BlockSpec((tm, k), lambda o, i: (ij(o, i)[0], 0)),
                pl.BlockSpec((tn, k), lambda o, i: (ij(o, i)[1], 0)) if tb
                else pl.BlockSpec((k, tn), lambda o, i: (0, ij(o, i)[1]))]
    args = [a, b]
    if extra is not None:
        in_specs.append(pl.BlockSpec((tm, tn), ij))
        args.append(extra)
    out_blk = pl.BlockSpec((tm, tn), ij)
    out = jax.ShapeDtypeStruct((m, n), out_dtype)
    return pl.pallas_call(
        body, name=name, grid=grid, in_specs=in_specs,
        out_specs=(out_blk, out_blk) if relu2_out else out_blk,
        out_shape=(out, jax.ShapeDtypeStruct((m, n), bf16)) if relu2_out else out,
        scratch_shapes=[pltpu.VMEM((tm, k), bf16)] if ta else [],
        compiler_params=_params(("parallel", "arbitrary" if ta else "parallel")),
    )(*args)


def _rows(t_rows):
    return _tile(t_rows, (384, 256, 128))


def _rmsnorm(h, w):
    return h * lax.rsqrt(jnp.mean(h * h, axis=1, keepdims=True) + RMS_EPS) * w


def rmsnorm_fwd(h, w8, *, name):
    t_rows, d = h.shape
    tr = _rows(t_rows)

    def body(h_ref, w_ref, o_ref):
        o_ref[...] = _rmsnorm(h_ref[...], w_ref[0:1, :]).astype(bf16)

    blk = pl.BlockSpec((tr, d), lambda i: (i, 0))
    return pl.pallas_call(
        body, name=name, grid=(t_rows // tr,), in_specs=[blk, pl.BlockSpec((8, d), lambda i: (0, 0))], out_specs=blk,
        out_shape=jax.ShapeDtypeStruct((t_rows, d), bf16), compiler_params=_params(("arbitrary",)),
    )(h, w8)


def rmsnorm_bwd(h, w8, dhn, dres, *, name):
    t_rows, d = h.shape
    tr = _rows(t_rows)

    def body(h_ref, w_ref, dhn_ref, dres_ref, dh_ref, dw_ref):
        @pl.when(pl.program_id(0) == 0)
        def _():
            dw_ref[...] = jnp.zeros_like(dw_ref)

        _, vjp = jax.vjp(_rmsnorm, h_ref[...], w_ref[0:1, :])
        dh, dw = vjp(dhn_ref[...])
        dh_ref[...] = dh + dres_ref[...]
        dw_ref[0:1, :] += dw

    blk = pl.BlockSpec((tr, d), lambda i: (i, 0))
    wblk = pl.BlockSpec((8, d), lambda i: (0, 0))
    return pl.pallas_call(
        body, name=name, grid=(t_rows // tr,), in_specs=[blk, wblk, blk, blk], out_specs=(blk, wblk),
        out_shape=(jax.ShapeDtypeStruct((t_rows, d), f32), jax.ShapeDtypeStruct((8, d), f32)),
        compiler_params=_params(("arbitrary",)),
    )(h, w8, dhn, dres)


def _merge(pg, ps, pw, la, lb, lc):
    return jax.nn.sigmoid(la) * pg + jax.nn.sigmoid(lb) * ps + jax.nn.sigmoid(lc) * pw


def _merge_specs(t_rows):
    tr = _rows(t_rows)
    blk = pl.BlockSpec((tr, D_MODEL), lambda i: (i, 0))
    gate = [pl.BlockSpec((tr, D_MODEL), functools.partial(lambda i, j: (i, j), j=C_GATE // D_MODEL + j)) for j in range(3)]
    return tr, blk, gate


def merge_fwd(pg, ps, pw, u):
    t_rows = pg.shape[0]
    tr, blk, gate = _merge_specs(t_rows)

    def body(pg_ref, ps_ref, pw_ref, la, lb, lc, o_ref):
        o_ref[...] = _merge(pg_ref[...], ps_ref[...], pw_ref[...], la[...], lb[...], lc[...]).astype(bf16)

    return pl.pallas_call(
        body, name="merge_fwd", grid=(t_rows // tr,), in_specs=[blk, blk, blk] + gate, out_specs=blk,
        out_shape=jax.ShapeDtypeStruct((t_rows, D_MODEL), bf16), compiler_params=_params(("arbitrary",)),
    )(pg, ps, pw, u, u, u)


def merge_bwd(pg, ps, pw, u, dmerged, du):
    t_rows = pg.shape[0]
    tr, blk, gate = _merge_specs(t_rows)

    def body(pg_ref, ps_ref, pw_ref, la, lb, lc, dm_ref, _, dpg_ref, dps_ref, dpw_ref, dl_ref):
        _, vjp = jax.vjp(_merge, pg_ref[...], ps_ref[...], pw_ref[...], la[...], lb[...], lc[...])
        dpg, dps, dpw, dla, dlb, dlc = vjp(dm_ref[...])
        dpg_ref[...] = dpg.astype(bf16)
        dps_ref[...] = dps.astype(bf16)
        dpw_ref[...] = dpw.astype(bf16)
        for j, dl in enumerate((dla, dlb, dlc)):
            dl_ref[:, j * D_MODEL:(j + 1) * D_MODEL] = dl.astype(bf16)

    act = jax.ShapeDtypeStruct((t_rows, D_MODEL), bf16)
    return pl.pallas_call(
        body, name="merge_bwd", grid=(t_rows // tr,), in_specs=[blk, blk, blk] + gate + [blk, ANY],
        out_specs=(blk, blk, blk, pl.BlockSpec((tr, 3 * D_MODEL), lambda i: (i, C_GATE // (3 * D_MODEL)))),
        out_shape=(act, act, act, jax.ShapeDtypeStruct(du.shape, du.dtype)),
        input_output_aliases={7: 3},
        compiler_params=_params(("arbitrary",)),
    )(pg, ps, pw, u, u, u, dmerged, du)


def relu2_fwd(a):
    t_rows, d = a.shape
    tr = _rows(t_rows)

    def body(a_ref, o_ref):
        r = jnp.maximum(a_ref[...], 0.0)
        o_ref[...] = (r * r).astype(bf16)

    blk = pl.BlockSpec((tr, d), lambda i: (i, 0))
    return pl.pallas_call(
        body, name="relu2_fwd", grid=(t_rows // tr,), in_specs=[blk], out_specs=blk,
        out_shape=jax.ShapeDtypeStruct((t_rows, d), bf16), compiler_params=_params(("arbitrary",)),
    )(a)


def relu2_bwd(a, dr):
    t_rows, d = a.shape
    tr = _rows(t_rows)

    def body(a_ref, dr_ref, o_ref):
        o_ref[...] = (dr_ref[...] * 2.0 * jnp.maximum(a_ref[...], 0.0)).astype(bf16)

    blk = pl.BlockSpec((tr, d), lambda i: (i, 0))
    return pl.pallas_call(
        body, name="relu2_bwd", grid=(t_rows // tr,), in_specs=[blk, blk], out_specs=blk,
        out_shape=jax.ShapeDtypeStruct((t_rows, d), bf16), compiler_params=_params(("arbitrary",)),
    )(a, dr)


def loss_head(h, w8, target):
    t_rows, d = h.shape
    tr = HEAD_ROWS

    def loss_fn(hb, w, tgt):
        err = _rmsnorm(hb, w) - tgt
        return 0.5 * jnp.sum(err * err) / d

    def body(h_ref, w_ref, t_ref, loss_ref, dh_ref, dw_ref):
        i = pl.program_id(0)

        @pl.when(i == 0)
        def _():
            loss_ref[...] = jnp.zeros_like(loss_ref)
            dw_ref[...] = jnp.zeros_like(dw_ref)
            dh_ref[...] = jnp.zeros_like(dh_ref)

        @pl.when(i > 0)
        def _():
            val, (dh, dw) = jax.value_and_grad(loss_fn, argnums=(0, 1))(h_ref[...], w_ref[0:1, :], t_ref[...])
            loss_ref[...] += val
            dh_ref[...] = dh
            dw_ref[0:1, :] += dw

    blk = pl.BlockSpec((tr, d), lambda i: (i, 0))
    wblk = pl.BlockSpec((8, d), lambda i: (0, 0))
    return pl.pallas_call(
        body, name="loss_head", grid=(t_rows // tr,),
        in_specs=[blk, wblk, pl.BlockSpec((tr, d), lambda i: (jnp.maximum(i - 1, 0), 0))],
        out_specs=(pl.BlockSpec((8, 128), lambda i: (0, 0)), blk, wblk),
        out_shape=(jax.ShapeDtypeStruct((8, 128), f32), jax.ShapeDtypeStruct((t_rows, d), f32),
                   jax.ShapeDtypeStruct((8, d), f32)),
        compiler_params=_params(("arbitrary",)),
    )(h, w8, target)


def adamw(w, m, v, partials, row_off, *, name):
    rows, d = w.shape
    layers = len(partials)
    per = rows // layers
    tr = _tile(per, tuple(p for p in (512, 256, 128, 64, 16, 8) if p * d * 4 <= BLOCK_BYTES))
    assert row_off % tr == 0
    off, nblk = row_off // tr, per // tr
    c1 = 1.0 - ADAM_B1 ** ADAM_STEP
    c2 = 1.0 - ADAM_B2 ** ADAM_STEP

    def body(w_ref, m_ref, v_ref, *refs):
        p_refs, (g_ref, d_ref, mo_ref, vo_ref) = refs[:2 * layers], refs[2 * layers:]
        g = p_refs[0][...] + p_refs[1][...]
        for l in range(1, layers):
            g = jnp.where(pl.program_id(0) >= l * nblk, p_refs[2 * l][...] + p_refs[2 * l + 1][...], g)
        m_new = ADAM_B1 * m_ref[...] + (1.0 - ADAM_B1) * g
        v_new = ADAM_B2 * v_ref[...] + (1.0 - ADAM_B2) * (g * g)
        g_ref[...] = g
        d_ref[...] = -ADAM_LR * ((m_new / c1) / (jnp.sqrt(v_new / c2) + ADAM_EPS) + ADAM_WD * w_ref[...])
        mo_ref[...] = m_new
        vo_ref[...] = v_new

    blk = pl.BlockSpec((tr, d), lambda i: (i, 0))
    pblks = [pl.BlockSpec((tr, d), functools.partial(lambda i, l: (off + jnp.clip(i - l * nblk, 0, nblk - 1), 0), l=l))
             for l in range(layers) for _ in range(2)]
    out = jax.ShapeDtypeStruct((rows, d), f32)
    return pl.pallas_call(
        body, name=name, grid=(rows // tr,), in_specs=[blk, blk, blk] + pblks, out_specs=(blk,) * 4,
        out_shape=(out,) * 4, compiler_params=_params(("arbitrary",)),
    )(w, m, v, *[p for pair in partials for p in pair])


def reduce4(parts, *, name, own=None, me=None):
    _, rows, d = parts.shape
    tr = _tile(rows, tuple(p for p in (512, 256, 128, 64, 8) if p * d * 4 <= BLOCK_BYTES))

    def body(*refs):
        p_ref, o_ref = refs[0], refs[-1]
        acc = None
        for s in range(4):
            term = p_ref[s].astype(f32)
            if own is not None:
                term = jnp.where(refs[2][0] == s, refs[1][...].astype(f32), term)
            acc = term if acc is None else acc + term
        o_ref[...] = acc

    in_specs = [pl.BlockSpec((4, tr, d), lambda i: (0, i, 0))]
    args = [parts]
    if own is not None:
        in_specs += [pl.BlockSpec((tr, d), lambda i: (i, 0)), pl.BlockSpec(memory_space=pltpu.SMEM)]
        args += [own, me]
    return pl.pallas_call(
        body, name=name, grid=(rows // tr,), in_specs=in_specs,
        out_specs=pl.BlockSpec((tr, d), lambda i: (i, 0)), out_shape=jax.ShapeDtypeStruct((rows, d), f32),
        compiler_params=_params(("arbitrary",)),
    )(*args)


ANY = pl.BlockSpec(memory_space=pl.ANY)
MESH = pl.DeviceIdType.MESH
CHIP_FLIPS = ((0, 1), (1, 0), (1, 1))


def chip_exchange(bufs, scatter, *, name, after=None):
    nb = len(bufs)
    extra = [] if after is None else [after]

    def body(*refs):
        ins, outs = refs[:nb], refs[nb + len(extra):2 * nb + len(extra)]
        send_sems, recv_sems, local_sems = refs[2 * nb + len(extra):]
        x, y, c = lax.axis_index("x"), lax.axis_index("y"), lax.axis_index("c")
        me = 2 * x + y
        local = [pltpu.make_async_copy(ins[j].at[me] if scatter[j] else ins[j], outs[j].at[me], local_sems.at[j])
                 for j in range(nb)]
        for cp in local:
            cp.start()
        sends, recvs = [], []
        for k, (fx, fy) in enumerate(CHIP_FLIPS):
            px = 1 - x if fx else x
            py = 1 - y if fy else y
            chip = 2 * px + py
            for j in range(nb):
                src = ins[j].at[chip] if scatter[j] else ins[j]
                sems = dict(send_sem=send_sems.at[nb * k + j], recv_sem=recv_sems.at[nb * k + j],
                            device_id=(px, py, c), device_id_type=MESH)
                sends.append(pltpu.make_async_remote_copy(src_ref=src, dst_ref=outs[j].at[me], **sems))
                recvs.append(pltpu.make_async_remote_copy(src_ref=src, dst_ref=outs[j].at[chip], **sems))
        for cp in sends:
            cp.start()
        for cp in recvs:
            cp.wait_recv()
        for cp in sends:
            cp.wait_send()
        for cp in local:
            cp.wait()

    out_shape = tuple(jax.ShapeDtypeStruct(b.shape if s else (4,) + b.shape, b.dtype) for b, s in zip(bufs, scatter))
    return pl.pallas_call(
        body, name=name, in_specs=[ANY] * (nb + len(extra)), out_specs=(ANY,) * nb, out_shape=out_shape,
        scratch_shapes=[pltpu.SemaphoreType.DMA((3 * nb,)), pltpu.SemaphoreType.DMA((3 * nb,)),
                        pltpu.SemaphoreType.DMA((nb,))],
        compiler_params=pltpu.CompilerParams(has_side_effects=True),
    )(*bufs, *extra)


def gather_two_level(big, small, *, name):
    half = big.shape[0] // 2

    def body(big_ref, small_ref, obig_ref, osmall_ref, send_sems, recv_sems, local_sems):
        x, y, c = lax.axis_index("x"), lax.axis_index("y"), lax.axis_index("c")
        me = 2 * x + y
        mine = pl.ds(pl.multiple_of(c * half, half), half)
        theirs = pl.ds(pl.multiple_of((1 - c) * half, half), half)
        local = [pltpu.make_async_copy(big_ref, obig_ref.at[me], local_sems.at[0]),
                 pltpu.make_async_copy(small_ref, osmall_ref.at[me], local_sems.at[1])]
        for cp in local:
            cp.start()

        def copy(k, src, dst, to):
            return pltpu.make_async_remote_copy(src_ref=src, dst_ref=dst, send_sem=send_sems.at[k],
                                                recv_sem=recv_sems.at[k], device_id=to, device_id_type=MESH)

        sends, landed, passed, small_in = [], [], [], []
        for k, (fx, fy) in enumerate(CHIP_FLIPS):
            px = 1 - x if fx else x
            py = 1 - y if fy else y
            chip = 2 * px + py
            sends.append(copy(k, big_ref.at[mine], obig_ref.at[me, mine], (px, py, c)))
            landed.append(copy(k, big_ref.at[mine], obig_ref.at[chip, mine], (px, py, c)))
            sends.append(copy(3 + k, small_ref, osmall_ref.at[me], (px, py, c)))
            small_in.append(copy(3 + k, small_ref, osmall_ref.at[chip], (px, py, c)))
            passed.append((copy(6 + k, obig_ref.at[chip, mine], obig_ref.at[chip, mine], (x, y, 1 - c)),
                           copy(6 + k, obig_ref.at[chip, theirs], obig_ref.at[chip, theirs], (x, y, 1 - c))))
        for cp in sends:
            cp.start()
        for k in range(3):
            landed[k].wait_recv()
            passed[k][0].start()
        for k in range(3):
            passed[k][1].wait_recv()
            small_in[k].wait_recv()
        for cp in sends + [p[0] for p in passed]:
            cp.wait_send()
        for cp in local:
            cp.wait()

    return pl.pallas_call(
        body, name=name, in_specs=[ANY, ANY], out_specs=(ANY, ANY),
        out_shape=(jax.ShapeDtypeStruct((4,) + big.shape, big.dtype),
                   jax.ShapeDtypeStruct((4,) + small.shape, small.dtype)),
        scratch_shapes=[pltpu.SemaphoreType.DMA((9,)), pltpu.SemaphoreType.DMA((9,)), pltpu.SemaphoreType.DMA((2,))],
        compiler_params=pltpu.CompilerParams(has_side_effects=True),
    )(big, small)


def sibling_swap(bufs, *, name):
    nb = len(bufs)

    def body(*refs):
        ins, outs, (send_sems, recv_sems) = refs[:nb], refs[nb:2 * nb], refs[2 * nb:]
        peer = (lax.axis_index("x"), lax.axis_index("y"), 1 - lax.axis_index("c"))
        copies = [pltpu.make_async_remote_copy(src_ref=ins[j], dst_ref=outs[j], send_sem=send_sems.at[j],
                                               recv_sem=recv_sems.at[j], device_id=peer, device_id_type=MESH)
                  for j in range(nb)]
        for cp in copies:
            cp.start()
        for cp in copies:
            cp.wait_recv()
        for cp in copies:
            cp.wait_send()

    return pl.pallas_call(
        body, name=name, in_specs=[ANY] * nb, out_specs=(ANY,) * nb,
        out_shape=tuple(jax.ShapeDtypeStruct(b.shape, b.dtype) for b in bufs),
        scratch_shapes=[pltpu.SemaphoreType.DMA((nb,)), pltpu.SemaphoreType.DMA((nb,))],
        compiler_params=pltpu.CompilerParams(has_side_effects=True),
    )(*bufs)


HBM = pl.BlockSpec(memory_space=pltpu.HBM)
SEM = pl.BlockSpec(memory_space=pltpu.SEMAPHORE)
DATAFLOW = pltpu.SideEffectType.DATAFLOW_SIDE_EFFECTING


def _exchange_copies(srcs, lands, send_sems, recv_sems, scatter):
    x, y, c = lax.axis_index("x"), lax.axis_index("y"), lax.axis_index("c")
    me = 2 * x + y
    nb = len(srcs)
    pairs = []
    for k, (fx, fy) in enumerate(CHIP_FLIPS):
        px = 1 - x if fx else x
        py = 1 - y if fy else y
        chip = 2 * px + py
        for j in range(nb):
            src = srcs[j].at[chip] if scatter[j] else srcs[j]
            sems = dict(send_sem=send_sems.at[nb * k + j], recv_sem=recv_sems.at[nb * k + j],
                        device_id=(px, py, c), device_id_type=MESH)
            pairs.append((pltpu.make_async_remote_copy(src_ref=src, dst_ref=lands[j].at[me], **sems),
                          pltpu.make_async_remote_copy(src_ref=src, dst_ref=lands[j].at[chip], **sems)))
    return pairs


def exchange_start(bufs, scatter, after, *, name):
    nb = len(bufs)
    slabs = [b.shape[1:] if s else b.shape for b, s in zip(bufs, scatter)]
    lands = [lax.empty((4,) + shp, b.dtype) for b, shp in zip(bufs, slabs)]

    def body(*refs):
        srcs, zones = refs[:nb], refs[nb:2 * nb]
        send_sems, recv_sems = refs[2 * nb + 1:2 * nb + 3]
        token = refs[-1]
        for send, _ in _exchange_copies(srcs, zones, send_sems, recv_sems, scatter):
            send.start()
        token[...] = jnp.zeros_like(token)

    hbm = lambda a: pltpu.with_memory_space_constraint(a, pltpu.HBM)
    out = pl.pallas_call(
        body, name=name, in_specs=[HBM] * (2 * nb) + [ANY],
        out_specs=(SEM, SEM) + (HBM,) * (2 * nb) + (pl.BlockSpec(memory_space=pltpu.VMEM),),
        out_shape=(pltpu.SemaphoreType.DMA((3 * nb,)), pltpu.SemaphoreType.DMA((3 * nb,)))
        + tuple(pltpu.HBM(a.shape, a.dtype) for a in list(bufs) + lands) + (jax.ShapeDtypeStruct((8, 128), f32),),
        input_output_aliases={i: 2 + i for i in range(2 * nb)},
        compiler_params=pltpu.CompilerParams(has_side_effects=DATAFLOW),
    )(*[hbm(a) for a in list(bufs) + lands], after)
    return (out[:2], out[2:2 + nb], out[2 + nb:2 + 2 * nb], scatter), out[-1]


def exchange_wait(state, after, *, name):
    (send_sems, recv_sems), srcs, lands, scatter = state
    nb = len(srcs)

    def body(*refs):
        src_refs, zones = refs[:nb], refs[nb:2 * nb]
        s_sems, r_sems = refs[2 * nb:2 * nb + 2]
        for send, recv in _exchange_copies(src_refs, zones, s_sems, r_sems, scatter):
            send.wait_send()
            recv.wait_recv()

    out = pl.pallas_call(
        body, name=name, in_specs=[HBM] * (2 * nb) + [SEM, SEM, ANY], out_specs=(HBM,) * (2 * nb),
        out_shape=tuple(pltpu.HBM(a.shape, a.dtype) for a in list(srcs) + list(lands)),
        input_output_aliases={i: i for i in range(2 * nb)},
        compiler_params=pltpu.CompilerParams(has_side_effects=DATAFLOW),
    )(*srcs, *lands, send_sems, recv_sems, after)
    return out[nb:]


BIG = (
    ("w_proj_gdn", 256), ("w_proj_ssd", 256), ("w_proj_swa", 256), ("w_out", 256), ("w_up", 1024), ("w_down", 1024))
BIG_OFF = {}
_o = 0
for _n, _r in BIG:
    BIG_OFF[_n] = _o
    _o += _r
BIG_ROWS = _o
W_IN_SHARD = IN_W // 4

W_NAMES = ('meta_tokens', 'norm1_w', 'w_in', 'gdn_conv_w', 'gdn_a_log', 'gdn_dt_bias', 'gdn_norm_w', 'ssd_conv_w',
           'ssd_conv_b', 'ssd_dt_bias', 'ssd_a_log', 'ssd_d', 'ssd_norm_w', 'swa_sinks', 'w_proj_gdn', 'w_proj_ssd',
           'w_proj_swa', 'w_out', 'norm2_w', 'w_up', 'w_down', 'final_norm_w')
SMALL_NAMES = tuple(n for n in W_NAMES if n not in BIG_OFF and n != "w_in")
SMALL_SHARDED = ("meta_tokens", "gdn_conv_w", "ssd_conv_w")


def _pad_rows(a, rows):
    return jnp.pad(a, ((0, rows - a.shape[0]), (0, 0)))


def _pack_rows(parts, dtype):
    flat = jnp.concatenate([p.reshape(-1).astype(dtype) for p in parts])
    n = -(-flat.shape[0] // 8192) * 8192
    return jnp.pad(flat, (0, n - flat.shape[0])).reshape(-1, D_MODEL)


def _unpack_rows(packed, shapes):
    flat, out, o = packed.reshape(-1), [], 0
    for s in shapes:
        n = 1
        for d in s:
            n *= d
        out.append(flat[o:o + n].reshape(s))
        o += n
    return out


def _split_chips(full, axis):
    s = full.shape
    a = full.reshape(s[:axis] + (4, s[axis] // 4) + s[axis + 1:])
    return jnp.moveaxis(a, axis, 0)


def _join_chips(parts, axis):
    a = jnp.moveaxis(parts, 0, axis)
    s = a.shape
    return a.reshape(s[:axis] + (s[axis] * s[axis + 1],) + s[axis + 2:])


BIG_AXIS = {"w_in": 2, "w_proj_gdn": 1, "w_proj_ssd": 1, "w_proj_swa": 1, "w_out": 1, "w_up": 2, "w_down": 1}


def _w_in_to_padded(w):
    z = lambda n: jnp.zeros(w.shape[:-1] + (n,), w.dtype)
    return jnp.concatenate([w[..., 8736:11808], w[..., 4112:7184], w[..., 7200:8736], w[..., 4096:4112], z(112),
                            w[..., 7184:7200], z(112 + C_MID_END - C_SDT - 128), w[..., 0:4096]], axis=-1)


def _w_in_from_padded(p):
    return jnp.concatenate([p[..., C_GQ:IN_WP], p[..., C_BA:C_BA + 16], p[..., C_SZ:C_WQ], p[..., C_SDT:C_SDT + 16],
                            p[..., C_WQ:C_BA], p[..., 0:C_SZ]], axis=-1)


def _row8(v, lane0=0, width=128):
    return jnp.pad(v[None, :], ((0, 7), (lane0, width - lane0 - v.shape[0])))


def _head_major(a, heads):
    return a.reshape(a.shape[0], heads, SWA_D).transpose(1, 0, 2)


def _from_head_major(a):
    return a.transpose(1, 0, 2).reshape(a.shape[1], -1)


def _layer_fwd(h, p, l, late=None):
    tag = f"l{l}"
    hn = rmsnorm_fwd(h, p["n1"], name=f"norm1_fwd_{tag}")
    u = mm(hn, p["w_in"], out_dtype=f32, name=f"mm_in_{tag}")
    yg, stg, tg = gdn_fwd(u, p["gcw"], p["galog"], p["gdtb"], p["gnw"])
    ys, sts = ssd_fwd(u, p["scw"], p["sdtb"], p["salog"], p["sd"], p["snw"])
    qh = _head_major(u[:, C_WQ:C_WK], SWA_Q_HEADS)
    kh = _head_major(u[:, C_WK:C_WV], SWA_KV_HEADS)
    vh = _head_major(u[:, C_WV:C_BA], SWA_KV_HEADS)
    yw = _from_head_major(swa_fwd(qh, kh, vh, p["sink"]))
    if late is not None:
        p.update(late(yw))
    pg = mm(yg, p["wpg"], out_dtype=f32, name=f"mm_pg_{tag}")
    ps = mm(ys, p["wps"], out_dtype=f32, name=f"mm_ps_{tag}")
    pw = mm(yw, p["wpw"], out_dtype=f32, name=f"mm_pw_{tag}")
    merged = merge_fwd(pg, ps, pw, u)
    h2 = mm(merged, p["wout"], out_dtype=f32, resid=h, name=f"mm_out_{tag}")
    hn2 = rmsnorm_fwd(h2, p["n2"], name=f"norm2_fwd_{tag}")
    a, r = mm(hn2, p["wup"], out_dtype=f32, relu2_out=True, name=f"mm_up_{tag}")
    h3 = mm(r, p["wdown"], out_dtype=f32, resid=h2, name=f"mm_down_{tag}")
    saved = dict(h=h, hn=hn, u=u, yg=yg, stg=stg, tg=tg, ys=ys, sts=sts, qh=qh, kh=kh, vh=vh, yw=yw, pg=pg, ps=ps, pw=pw,
                 merged=merged, h2=h2, hn2=hn2, a=a, r=r)
    return h3, saved


def _layer_bwd(dh3, p, s, l, send_big, send_w_in):
    tag = f"l{l}"
    g = {}

    def wgrad(act, d, name):
        return mm(act, d, ta=True, out_dtype=bf16, name=f"wg_{name}_{tag}")

    da = mm(dh3, p["wdown"], tb=True, out_dtype=bf16, relu_grad_of=s["a"], name=f"dg_down_{tag}")
    g["w_down"] = wgrad(s["r"], dh3, "down")
    dhn2 = mm(da, p["wup"], tb=True, out_dtype=f32, name=f"dg_up_{tag}")
    g["w_up"] = wgrad(s["hn2"], da, "up")
    dh2, g["norm2_w"] = rmsnorm_bwd(s["h2"], p["n2"], dhn2, dh3, name=f"norm2_bwd_{tag}")
    dmerged = mm(dh2, p["wout"], tb=True, out_dtype=f32, name=f"dg_out_{tag}")
    g["w_out"] = wgrad(s["merged"], dh2, "out")
    du = lax.empty((dh3.shape[0], IN_WP), bf16)
    dpg, dps, dpw, du = merge_bwd(s["pg"], s["ps"], s["pw"], s["u"], dmerged, du)
    dyg = mm(dpg, p["wpg"], tb=True, out_dtype=f32, name=f"dg_pg_{tag}")
    dys = mm(dps, p["wps"], tb=True, out_dtype=f32, name=f"dg_ps_{tag}")
    dyw = mm(dpw, p["wpw"], tb=True, out_dtype=f32, name=f"dg_pw_{tag}")
    g["w_proj_gdn"] = wgrad(s["yg"], dpg, "pg")
    g["w_proj_ssd"] = wgrad(s["ys"], dps, "ps")
    g["w_proj_swa"] = wgrad(s["yw"], dpw, "pw")
    sent = send_big(jnp.concatenate([_split_chips(g.pop(n), BIG_AXIS[n] - 1).reshape(4, r, D_MODEL)
                                     for n, r in BIG], axis=1))

    (du, dba, dtq, dtk, dtv, g["gdn_a_log"], g["gdn_dt_bias"], g["gdn_norm_w"]) = gdn_bwd(
        s["u"], p["gcw"] + sent, p["galog"], p["gdtb"], p["gnw"], s["stg"], s["tg"], dyg, du)
    g["gdn_conv_w"] = jnp.concatenate([dtq, dtk, dtv], axis=1)[:4]
    (du, ddt, dtx, dtb, dtc, g["ssd_dt_bias"], g["ssd_a_log"], g["ssd_d"], g["ssd_norm_w"]) = ssd_bwd(
        s["u"], p["scw"], p["sdtb"], p["salog"], p["sd"], p["snw"], s["sts"], dys, du)
    dconv = jnp.concatenate([dtx, dtb, dtc], axis=1)
    g["ssd_conv_w"], g["ssd_conv_b"] = dconv[:4], dconv[4]
    dqh, dkh, dvh, g["swa_sinks"] = swa_bwd(s["qh"], s["kh"], s["vh"], p["sink"], _head_major(dyw, SWA_Q_HEADS))
    mid = jnp.concatenate([_from_head_major(dqh), _from_head_major(dkh), _from_head_major(dvh), dba[0].astype(bf16),
                           ddt.astype(bf16), jnp.zeros((du.shape[0], C_MID_END - C_SDT - 128), bf16)], axis=1)
    du = lax.dynamic_update_slice(du, mid, (0, C_WQ))
    sent = send_w_in(_split_chips(_w_in_from_padded(wgrad(s["hn"], du, "in")), 1))
    dhn = mm(du, p["w_in_t"], out_dtype=f32, name=f"dg_in_{tag}")
    dh, g["norm1_w"] = rmsnorm_bwd(s["h"], p["n1"] + sent, dhn, dh2, name=f"norm1_bwd_{tag}")
    return dh, g


def kernel(x, meta_tokens, norm1_w, w_in, gdn_conv_w, gdn_a_log, gdn_dt_bias, gdn_norm_w, ssd_conv_w, ssd_conv_b, ssd_dt_bias, ssd_a_log, ssd_d, ssd_norm_w, swa_sinks, w_proj_gdn, w_proj_ssd, w_proj_swa, w_out, norm2_w, w_up, w_down, final_norm_w, loss_target, m_meta_tokens, m_norm1_w, m_w_in, m_gdn_conv_w, m_gdn_a_log, m_gdn_dt_bias, m_gdn_norm_w, m_ssd_conv_w, m_ssd_conv_b, m_ssd_dt_bias, m_ssd_a_log, m_ssd_d, m_ssd_norm_w, m_swa_sinks, m_w_proj_gdn, m_w_proj_ssd, m_w_proj_swa, m_w_out, m_norm2_w, m_w_up, m_w_down, m_final_norm_w, v_meta_tokens, v_norm1_w, v_w_in, v_gdn_conv_w, v_gdn_a_log, v_gdn_dt_bias, v_gdn_norm_w, v_ssd_conv_w, v_ssd_conv_b, v_ssd_dt_bias, v_ssd_a_log, v_ssd_d, v_ssd_norm_w, v_swa_sinks, v_w_proj_gdn, v_w_proj_ssd, v_w_proj_swa, v_w_out, v_norm2_w, v_w_up, v_w_down, v_final_norm_w):
    given = dict(locals())
    depth = norm1_w.shape[0]
    me = 2 * lax.axis_index("x") + lax.axis_index("y")

    me1 = jnp.reshape(me, (1,)).astype(jnp.int32)
    is_me = (jnp.arange(4, dtype=jnp.int32) == me)[:, None, None]

    def weight_slabs(l):
        return (w_in[l].astype(bf16),
                jnp.concatenate([given[n][l].reshape(-1, D_MODEL).astype(bf16) for n, _ in BIG]))

    slabs = [weight_slabs(l) for l in range(depth)]
    wsmall = _pack_rows([given[n] for n in SMALL_SHARDED], f32)
    ga0, gsmall = gather_two_level(slabs[0][0], wsmall, name="gather_first")
    gathers, started = {}, jnp.zeros((), f32)
    for l in range(depth):
        for j in range(2):
            if (l, j) != (0, 0):
                gathers[l, j], token = exchange_start([slabs[l][j]], (False,), gsmall, name=f"gather_start_l{l}_{j}")
                started = started + token[0, 0]
    shard_shapes = [given[n].shape for n in SMALL_SHARDED]
    per_chip = [_unpack_rows(gsmall[s], shard_shapes) for s in range(4)]
    full = {n: jnp.concatenate([per_chip[s][i] for s in range(4)], axis=-1) for i, n in enumerate(SMALL_SHARDED)}

    def landed(l, j, after):
        (zone,) = exchange_wait(gathers[l, j], after, name=f"gather_wait_l{l}_{j}")
        return jnp.where(is_me, slabs[l][j][None], zone)

    def first_operands(l, ga, order):
        w_in_p = _w_in_to_padded(_join_chips(ga, 1))
        return dict(
            n1=_row8(norm1_w[l], width=D_MODEL) + order, n2=_row8(norm2_w[l], width=D_MODEL),
            w_in=w_in_p, w_in_t=w_in_p.T,
            gcw=jnp.pad(full["gdn_conv_w"][l], ((0, 4), (0, 0))),
            galog=_row8(gdn_a_log[l], 8), gdtb=_row8(gdn_dt_bias[l], 8), gnw=_row8(gdn_norm_w[l]),
            scw=jnp.pad(jnp.concatenate([full["ssd_conv_w"][l], ssd_conv_b[l][None]], axis=0), ((0, 3), (0, 0))),
            sdtb=_row8(ssd_dt_bias[l]), salog=_row8(ssd_a_log[l]), sd=_row8(ssd_d[l]),
            snw=_row8(ssd_norm_w[l], width=D_MODEL), sink=_row8(swa_sinks[l]))

    def late_operands(l, after):
        gb = landed(l, 1, after)
        w = {}
        for n, r in BIG:
            parts = gb[:, BIG_OFF[n]:BIG_OFF[n] + r].reshape((4,) + given[n].shape[1:])
            w[n] = _join_chips(parts, BIG_AXIS[n] - 1)
        return dict(wpg=w["w_proj_gdn"], wps=w["w_proj_ssd"], wpw=w["w_proj_swa"], wout=w["w_out"],
                    wup=w["w_up"], wdown=w["w_down"])

    h = jnp.concatenate([jnp.zeros((PAD, D_MODEL), f32), full["meta_tokens"], x[0]], axis=0)
    layers, saved = [], []
    for l in range(depth):
        p = first_operands(0, ga0, started) if l == 0 else first_operands(l, landed(l, 0, h), 0.0)
        h, s = _layer_fwd(h, p, l, late=functools.partial(late_operands, l))
        layers.append(p)
        saved.append(s)
    loss8, dh, dfw8 = loss_head(h, _row8(final_norm_w, width=D_MODEL), loss_target[0])
    grads = {"final_norm_w": dfw8[0]}
    per_layer, grad_slabs, scatters = [None] * depth, {}, {}

    def send(l, j, slab):
        grad_slabs[l, j] = slab
        scatters[l, j], token = exchange_start([slab], (True,), loss8, name=f"scatter_start_l{l}_{j}")
        return token[0, 0]

    for l in reversed(range(depth)):
        dh, per_layer[l] = _layer_bwd(dh, layers[l], saved[l], l, functools.partial(send, l, 1),
                                      functools.partial(send, l, 0))
    grad_x = dh[HEAD_ROWS:][None]
    grads["meta_tokens"] = dh[PAD:HEAD_ROWS]
    lane = {"gdn_a_log": (8, 8), "gdn_dt_bias": (8, 8), "gdn_norm_w": (0, 128), "ssd_dt_bias": (0, 16),
            "ssd_a_log": (0, 16), "ssd_d": (0, 16), "swa_sinks": (0, 16)}
    for n in per_layer[0]:
        parts = [per_layer[l][n] for l in range(depth)]
        if n in lane:
            parts = [q[0, lane[n][0]:lane[n][0] + lane[n][1]] for q in parts]
        elif n in ("norm1_w", "norm2_w", "ssd_norm_w"):
            parts = [q[0] for q in parts]
        grads[n] = jnp.stack(parts)
    loss = lax.psum(loss8[0, 0], ("x", "y", "c"))

    gs = _pack_rows([grads[n] for n in SMALL_NAMES], f32)
    def chip_sum(l, j, after):
        (zone,) = exchange_wait(scatters[l, j], after, name=f"scatter_wait_l{l}_{j}")
        own = lax.dynamic_index_in_dim(grad_slabs[l, j], me, 0, keepdims=False)
        return reduce4(zone, own=own, me=me1, name=f"sum_chips_l{l}_{j}")

    early = [(l, j) for l in range(depth) for j in range(2) if (l, j) != (0, 0)]
    mine = {lj: chip_sum(*lj, dh) for lj in early}
    sibs = dict(zip(early, sibling_swap([mine[lj] for lj in early], name="swap_cores_early")))
    out = {}
    for n, r in BIG:
        shp = given[n].shape
        res = adamw(*[given[pre + n].reshape(depth * r, D_MODEL) for pre in ("", "m_", "v_")],
                    [(mine[l, 1], sibs[l, 1]) for l in range(depth)], BIG_OFF[n], name=f"adamw_{n}")
        out[n] = [a.reshape(shp) for a in res]
    mine[0, 0] = chip_sum(0, 0, res[1])
    (rs,) = chip_exchange([gs], (False,), after=mine[0, 0], name="gather_small_grads")
    ps_ = reduce4(rs, name="sum_chips_small")
    sibs[0, 0], ss = sibling_swap([mine[0, 0], ps_], name="swap_cores_last")
    w_in_rows = depth * D_MODEL
    res = adamw(*[given[pre + "w_in"].reshape(w_in_rows, W_IN_SHARD) for pre in ("", "m_", "v_")],
                [(mine[l, 0], sibs[l, 0]) for l in range(depth)], 0, name="adamw_w_in")
    out["w_in"] = [a.reshape(w_in.shape) for a in res]
    full_shapes = [grads[n].shape for n in SMALL_NAMES]
    mine_s, sib_s = _unpack_rows(ps_, full_shapes), _unpack_rows(ss, full_shapes)

    def local(parts):
        loc = []
        for n, a in zip(SMALL_NAMES, parts):
            if n in SMALL_SHARDED:
                sz = a.shape[-1] // 4
                a = lax.dynamic_slice_in_dim(a, me * sz, sz, axis=a.ndim - 1)
            loc.append(a)
        return _pack_rows(loc, f32)

    res = adamw(_pack_rows([given[n] for n in SMALL_NAMES], f32), _pack_rows([given["m_" + n] for n in SMALL_NAMES], f32),
                _pack_rows([given["v_" + n] for n in SMALL_NAMES], f32), [(local(mine_s), local(sib_s))], 0,
                name="adamw_small")
    local_shapes = [given[n].shape for n in SMALL_NAMES]
    unpacked = [_unpack_rows(a, local_shapes) for a in res]
    for i, n in enumerate(SMALL_NAMES):
        out[n] = [unpacked[j][i] for j in range(4)]

    return (loss, grad_x) + tuple(out[n][j] for j in range(4) for n in W_NAMES)
```

```python
import functools

import jax
import jax.numpy as jnp
from jax import lax
from jax.experimental import pallas as pl
from jax.experimental.pallas import tpu as pltpu

f32 = jnp.float32
bf16 = jnp.bfloat16
HI = lax.Precision.HIGHEST

D_MODEL = 1024
N_META = 16
PAD = 112
HEAD_ROWS = PAD + N_META
RMS_EPS = 1e-6
L2_EPS = 1e-6
D_FF = 4 * D_MODEL

GDN_HEADS = 8
GDN_D = 128
GDN_CHUNK = 64
SSD_HEADS = 16
SSD_P = 64
SSD_GROUPS = 4
SSD_HPG = 4
SSD_N = 128
SSD_CHUNK = 128
SWA_Q_HEADS = 16
SWA_KV_HEADS = 4
SWA_REP = 4
SWA_D = 64
SWA_W = 128

C_GATE = 0
C_SZ, C_SX, C_SB, C_SC = 3072, 4096, 5120, 5632
C_WQ, C_WK, C_WV = 6144, 7168, 7424
C_BA = 7680
C_SDT = 7808
C_MID_END = 8192
C_GQ, C_GK, C_GV, C_GG = 8192, 9216, 10240, 11264
IN_WP = 12288
IN_W = 11808

ADAM_LR, ADAM_B1, ADAM_B2, ADAM_EPS, ADAM_WD, ADAM_STEP = 0.001, 0.9, 0.999, 1e-08, 0.01, 10

VMEM_LIMIT = 56 * 1024 * 1024
BLOCK_BYTES = 3 << 19
MM_OPERAND_BYTES = 9 << 20
MM_RESIDENT_BYTES = 13 << 20

NN = (((1,), (0,)), ((), ()))
NT = (((1,), (1,)), ((), ()))
TN = (((0,), (0,)), ((), ()))


def _dot(a, b, dims=NN):
    return lax.dot_general(a.astype(bf16), b.astype(bf16), dims, preferred_element_type=f32)


def _dotx(a, b, dims=NN):
    return lax.dot_general(a, b, dims, preferred_element_type=f32, precision=lax.Precision.HIGH)


def _iota(shape, axis):
    return lax.broadcasted_iota(jnp.int32, shape, axis)


def _softplus(x):
    return jnp.maximum(x, 0.0) + jnp.log1p(jnp.exp(-jnp.abs(x)))


def _silu(x):
    return x * jax.nn.sigmoid(x)


def _params(sem):
    return pltpu.CompilerParams(dimension_semantics=sem, vmem_limit_bytes=VMEM_LIMIT)


@functools.partial(jax.custom_vjp, nondiff_argnums=(1,))
def _window(x_ext, off):
    n = x_ext.shape[0] - 8
    if off == 8:
        return x_ext[8:]
    return pltpu.roll(x_ext, 8 - off, 0)[8:]


def _window_fwd(x_ext, off):
    return _window(x_ext, off), None


def _window_bwd(off, _, g):
    n, w = g.shape
    g_ext = jnp.concatenate([jnp.zeros((8, w), g.dtype), g], axis=0)
    if off == 8:
        return (g_ext,)
    return (pltpu.roll(g_ext, n + off, 0),)


_window.defvjp(_window_fwd, _window_bwd)


def _conv4(x, halo, taps):
    x_ext = jnp.concatenate([halo, x], axis=0)
    y = taps[3] * x
    for j in range(3):
        y = y + taps[j] * _window(x_ext, 5 + j)
    return y


def _blockinv_impl(a):
    n = a.shape[0]
    ri, ci = _iota((n, n), 0), _iota((n, n), 1)
    t = (ri == ci).astype(f32)
    k = 0
    while (1 << k) < n:
        sel = ((ri >> (k + 1)) == (ci >> (k + 1))) & (((ri >> k) & 1) == 1) & (((ci >> k) & 1) == 0)
        o = jnp.where(sel, a, 0.0)
        t = t - _dotx(_dotx(t, o), t)
        k += 1
    return t


@jax.custom_vjp
def _blockinv(a):
    return _blockinv_impl(a)


def _blockinv_fwd(a):
    t = _blockinv_impl(a)
    return t, t


def _blockinv_bwd(t, dt):
    return (-_dotx(_dotx(t, dt, TN), t, NT),)


_blockinv.defvjp(_blockinv_fwd, _blockinv_bwd)


@jax.custom_vjp
def _blockinv_given(a, t):
    return t


_blockinv_given.defvjp(lambda a, t: (t, t), lambda t, dt: _blockinv_bwd(t, dt) + (jnp.zeros_like(t),))


def _scan_rows(x, reverse):
    n = x.shape[0]
    row = _iota(x.shape, 0)
    s = 1
    while s < n:
        if reverse:
            x = x + jnp.where(row < n - s, pltpu.roll(x, n - s, 0), 0.0)
        else:
            x = x + jnp.where(row >= s, pltpu.roll(x, s, 0), 0.0)
        s *= 2
    return x


@jax.custom_vjp
def _cumsum_rows(x):
    return _scan_rows(x, False)


_cumsum_rows.defvjp(lambda x: (_scan_rows(x, False), None), lambda _, g: (_scan_rows(g, True),))


def _gdn_act(xq, xk, xv, hq, hk, hv, tq, tk, tv):
    return _silu(_conv4(xq, hq, tq)), _silu(_conv4(xk, hk, tk)), _silu(_conv4(xv, hv, tv))


def _gdn_core(q, k, v, gate, mb, mg, mr, s, t_given, beta16, g16, gam16, gam16_t, nw):
    c = GDN_CHUNK
    q = q * lax.rsqrt(jnp.sum(q * q, axis=1, keepdims=True) + L2_EPS) * (GDN_D ** -0.5)
    k = k * lax.rsqrt(jnp.sum(k * k, axis=1, keepdims=True) + L2_EPS)

    pick = lambda x, m: jnp.sum(x * m, axis=1, keepdims=True)
    beta = pick(beta16, mb)
    g = jnp.broadcast_to(pick(g16, mg), (c, GDN_D))
    gam1 = pick(gam16, mg)
    gam = jnp.broadcast_to(gam1, (c, GDN_D))
    gam_j = jnp.broadcast_to(jnp.sum(gam16_t * mr, axis=0, keepdims=True), (c, c))

    ri, ci = _iota((c, c), 0), _iota((c, c), 1)
    incl = ci <= ri
    decay = jnp.where(incl, jnp.exp(jnp.where(incl, jnp.broadcast_to(gam1, (c, c)) - gam_j, 0.0)), 0.0)

    kb = k * beta
    a = jnp.where(ci < ri, _dot(kb, k, NT) * decay, 0.0)
    t = _blockinv(a) if t_given is None else _blockinv_given(a, t_given)
    egam = jnp.exp(gam)
    u = _dotx(t, v * beta)
    w = _dotx(t, kb * egam)
    attn = _dot(q, k, NT) * decay
    gl = jnp.sum(g, axis=0, keepdims=True)
    kt = k * jnp.exp(gl - gam)
    v_new = u - _dot(w, s)
    o = _dot(q * egam, s) + _dot(attn, v_new)
    s_out = s * jnp.exp(gl) + _dot(kt, v_new, TN)

    y = o * lax.rsqrt(jnp.mean(o * o, axis=1, keepdims=True) + RMS_EPS) * nw * _silu(gate)
    return y, s_out, t


def _gdn_chunk(q, k, v, gate, s, t_given, ba, alog, dtb, nw, *, masks, row0):
    valid = (row0 + _iota((GDN_CHUNK, 1), 0)) >= PAD
    beta16 = jnp.where(valid, jax.nn.sigmoid(ba), 0.0)
    g16 = jnp.where(valid, -jnp.exp(alog) * _softplus(ba + dtb), 0.0)
    gam16 = _cumsum_rows(g16)
    core = jax.vmap(_gdn_core, in_axes=(0,) * 8 + (None if t_given is None else 0,) + (None,) * 5)
    y, s_out, t = core(q, k, v, gate, *masks, s, t_given, beta16, g16, gam16, gam16.T, nw)
    return (y, s_out, t) if t_given is None else (y, s_out)


def _gdn_specs(hb, nc, rev):
    w = hb * GDN_D
    cw = D_MODEL // w

    def cidx(c):
        return (nc - 1 - c) if rev else c

    def col(base):
        return pl.BlockSpec((GDN_CHUNK, w), lambda h, c: (cidx(c), base // w + h))

    def halo(base):
        return pl.BlockSpec((8, w), lambda h, c: (jnp.maximum(cidx(c) * (GDN_CHUNK // 8) - 1, 0), base // w + h))

    def taps(base):
        return pl.BlockSpec((8, w), lambda h, c: (0, base // w + h))

    ba = pl.BlockSpec((GDN_CHUNK, 128), lambda h, c: (cidx(c), C_BA // 128))
    row = pl.BlockSpec((8, 128), lambda h, c: (0, 0))
    y = pl.BlockSpec((GDN_CHUNK, w), lambda h, c: (cidx(c), h))
    st = pl.BlockSpec((1, hb, GDN_D, GDN_D), lambda h, c: (cidx(c), h, 0, 0))
    in_specs = [col(C_GQ), col(C_GK), col(C_GV), halo(C_GQ), halo(C_GK), halo(C_GV), col(C_GG), ba,
                taps(0), taps(1024), taps(2048), row, row, row]
    return in_specs, y, st, taps, row, col, ba


def _gdn_load(refs, first):
    xq, xk, xv, hq, hk, hv, gate, ba, tq, tk, tv, alog, dtb, nw = refs

    def halo(r):
        return jnp.where(first, 0.0, r[...])

    def taps(r):
        return tuple(r[j:j + 1, :] for j in range(4))

    act = (xq[...], xk[...], xv[...], halo(hq), halo(hk), halo(hv), taps(tq), taps(tk), taps(tv))
    return act, gate[...], (ba[...], alog[0:1, :], dtb[0:1, :], nw[0:1, :])


def _heads(a, hb):
    return jnp.stack([a[:, i * GDN_D:(i + 1) * GDN_D] for i in range(hb)])


def _wide(a):
    return jnp.concatenate([a[i] for i in range(a.shape[0])], axis=1)


def _head_masks(hblk, hb):
    head = hblk * hb + _iota((hb, 1, 128), 0)
    lane = _iota((hb, 1, 128), 2)
    rows = (_iota((hb, 128, 1), 1) == hblk * hb + _iota((hb, 128, 1), 0) + 8).astype(f32)
    return (lane == head).astype(f32), (lane == head + 8).astype(f32), rows


def gdn_fwd(u, conv_w8, alog8, dtb8, nw8, *, hb=8):
    t_rows = u.shape[0]
    nc = t_rows // GDN_CHUNK
    in_specs, y_spec, st_spec, *_ = _gdn_specs(hb, nc, False)

    def body(*refs):
        ins, (y_ref, st_ref, t_ref), (s_scr,) = refs[:14], refs[14:17], refs[17:]
        hblk, c = pl.program_id(0), pl.program_id(1)

        @pl.when(c == 0)
        def _():
            s_scr[...] = jnp.zeros_like(s_scr)

        act, gate, shared = _gdn_load(ins, c == 0)
        s = s_scr[...]
        st_ref[0] = s
        qa, ka, va = _gdn_act(*act)
        y, s_new, t = _gdn_chunk(_heads(qa, hb), _heads(ka, hb), _heads(va, hb), _heads(gate, hb), s, None, *shared,
                                 masks=_head_masks(hblk, hb), row0=c * GDN_CHUNK)
        y_ref[...] = _wide(y).astype(bf16)
        t_ref[0] = t
        s_scr[...] = s_new

    return pl.pallas_call(
        body, name="gdn_fwd", grid=(GDN_HEADS // hb, nc),
        in_specs=in_specs,
        out_specs=(y_spec, st_spec, pl.BlockSpec((1, hb, GDN_CHUNK, GDN_CHUNK), lambda h, c: (c, h, 0, 0))),
        out_shape=(jax.ShapeDtypeStruct((t_rows, D_MODEL), bf16),
                   jax.ShapeDtypeStruct((nc, GDN_HEADS, GDN_D, GDN_D), f32),
                   jax.ShapeDtypeStruct((nc, GDN_HEADS, GDN_CHUNK, GDN_CHUNK), f32)),
        scratch_shapes=[pltpu.VMEM((hb, GDN_D, GDN_D), f32)],
        compiler_params=_params(("arbitrary", "arbitrary")),
    )(u, u, u, u, u, u, u, u, conv_w8, conv_w8, conv_w8, alog8, dtb8, nw8)


def gdn_bwd(u, conv_w8, alog8, dtb8, nw8, states, tinv, dy, du):
    t_rows = u.shape[0]
    nc = t_rows // GDN_CHUNK
    hb = GDN_HEADS
    w = hb * GDN_D
    in_specs, y_spec, st_spec, taps, row, col, ba = _gdn_specs(hb, nc, True)
    nhb = GDN_HEADS // hb

    def body(*refs):
        ins, st_ref, t_ref, dy_ref = refs[:14], refs[14], refs[15], refs[16]
        du_ref, dba_ref, dtq_ref, dtk_ref, dtv_ref, dalog_ref, ddtb_ref, dnw_ref = refs[18:26]
        ds_scr, dh_scr = refs[26:]
        hblk, cc = pl.program_id(0), pl.program_id(1)
        c = nc - 1 - cc

        @pl.when(cc == 0)
        def _():
            ds_scr[...] = jnp.zeros_like(ds_scr)
            dh_scr[...] = jnp.zeros_like(dh_scr)
            dtq_ref[...] = jnp.zeros_like(dtq_ref)
            dtk_ref[...] = jnp.zeros_like(dtk_ref)
            dtv_ref[...] = jnp.zeros_like(dtv_ref)

        @pl.when((cc == 0) & (hblk == 0))
        def _():
            dalog_ref[...] = jnp.zeros_like(dalog_ref)
            ddtb_ref[...] = jnp.zeros_like(ddtb_ref)
            dnw_ref[...] = jnp.zeros_like(dnw_ref)

        act, gate, shared = _gdn_load(ins, c == 0)
        (qa, ka, va), vjp_act = jax.vjp(_gdn_act, *act)
        chunk = functools.partial(_gdn_chunk, masks=_head_masks(hblk, hb), row0=c * GDN_CHUNK)
        _, vjp_core = jax.vjp(chunk, _heads(qa, hb), _heads(ka, hb), _heads(va, hb), _heads(gate, hb), st_ref[0],
                              t_ref[0], *shared)
        dqa, dka, dva, dgate, ds, _, dba, dalog, ddtb, dnw = vjp_core(
            (_heads(dy_ref[...].astype(f32), hb), ds_scr[...]))
        ds_scr[...] = ds
        dxq, dxk, dxv, dhq, dhk, dhv, dtq, dtk, dtv = vjp_act((_wide(dqa), _wide(dka), _wide(dva)))
        zeros = jnp.zeros((GDN_CHUNK - 8, w), f32)
        for j, (dx, dh) in enumerate(((dxq, dhq), (dxk, dhk), (dxv, dhv))):
            du_ref[:, j * w:(j + 1) * w] = (dx + jnp.concatenate([zeros, dh_scr[j]], axis=0)).astype(bf16)
            dh_scr[j] = dh
        du_ref[:, 3 * w:4 * w] = _wide(dgate).astype(bf16)
        dba_ref[0] = dba
        for dt_ref, dtaps in ((dtq_ref, dtq), (dtk_ref, dtk), (dtv_ref, dtv)):
            for j in range(4):
                dt_ref[j:j + 1, :] += dtaps[j]
        dalog_ref[0:1, :] += dalog
        ddtb_ref[0:1, :] += ddtb
        dnw_ref[0:1, :] += dnw

    out_specs = (pl.BlockSpec((GDN_CHUNK, 4 * w), lambda h, c: (nc - 1 - c, C_GQ // (4 * w))),
                 pl.BlockSpec((1, GDN_CHUNK, 128), lambda h, c: (h, nc - 1 - c, 0)),
                 taps(0), taps(0), taps(0), row, row, row)
    out_shape = (jax.ShapeDtypeStruct(du.shape, du.dtype),
                 jax.ShapeDtypeStruct((nhb, t_rows, 128), f32),
                 jax.ShapeDtypeStruct((8, D_MODEL), f32), jax.ShapeDtypeStruct((8, D_MODEL), f32),
                 jax.ShapeDtypeStruct((8, D_MODEL), f32),
                 jax.ShapeDtypeStruct((8, 128), f32), jax.ShapeDtypeStruct((8, 128), f32), jax.ShapeDtypeStruct((8, 128), f32))
    return pl.pallas_call(
        body, name="gdn_bwd", grid=(nhb, nc),
        in_specs=in_specs + [st_spec, pl.BlockSpec((1, hb, GDN_CHUNK, GDN_CHUNK), lambda h, c: (nc - 1 - c, h, 0, 0)),
                             y_spec, ANY],
        out_specs=out_specs, out_shape=out_shape, input_output_aliases={17: 0},
        scratch_shapes=[pltpu.VMEM((hb, GDN_D, GDN_D), f32), pltpu.VMEM((3, 8, w), f32)],
        compiler_params=_params(("arbitrary", "arbitrary")),
    )(u, u, u, u, u, u, u, u, conv_w8, conv_w8, conv_w8, alog8, dtb8, nw8, states, tinv, dy, du)


def _ssd_act(xs_r, b_r, c_r, hx, hbm, hcm, tx, tb, tc, bx, bb, bc, *, row0):
    valid = (row0 + _iota((SSD_CHUNK, 1), 0)) >= PAD
    act = lambda x, h, t, b: jnp.where(valid, _silu(_conv4(x, h, t) + b), 0.0)
    return act(xs_r, hx, tx, bx), act(b_r, hbm, tb, bb), act(c_r, hcm, tc, bc)


def _ssd_core(xs, bm, cm, z, nw, lanes, rows, h, dtp16, adt16, acum16, acum16_t, dsk):
    n = SSD_CHUNK
    pick = lambda x, m: jnp.sum(x * m, axis=1, keepdims=True)
    lane_r = _iota((1, 256), 1) >> 6
    dtp = jnp.zeros((n, 256), f32)
    adt = jnp.zeros((n, 256), f32)
    acum = jnp.zeros((n, 256), f32)
    dlane = jnp.zeros((1, 256), f32)
    ccols = []
    for r in range(SSD_HPG):
        ccols.append(pick(acum16, lanes[r]))
        dtp = jnp.where(lane_r == r, pick(dtp16, lanes[r]), dtp)
        adt = jnp.where(lane_r == r, pick(adt16, lanes[r]), adt)
        acum = jnp.where(lane_r == r, ccols[r], acum)
        dlane = jnp.where(lane_r == r, pick(dsk, lanes[r]), dlane)

    ri, ci = _iota((n, n), 0), _iota((n, n), 1)
    incl = ci <= ri
    al = jnp.sum(adt, axis=0, keepdims=True)
    xdt = xs * dtp
    cb = _dot(cm, bm, NT)
    y = _dot(cm, h) * jnp.exp(acum) + dlane * xs
    for r in range(SSD_HPG):
        ai = jnp.broadcast_to(ccols[r], (n, n))
        aj = jnp.broadcast_to(jnp.sum(acum16_t * rows[r], axis=0, keepdims=True), (n, n))
        lm = jnp.where(incl, jnp.exp(jnp.where(incl, ai - aj, 0.0)), 0.0)
        y = y + _dot(cb * lm, jnp.where(lane_r == r, xdt, 0.0))
    h_out = h * jnp.exp(al) + _dot(bm, jnp.exp(al - acum) * xdt, TN)
    y = y * _silu(z)
    y = y * lax.rsqrt(jnp.mean(y * y, axis=1, keepdims=True) + RMS_EPS) * nw
    return y, h_out


def _ssd_chunk(xs, bm, cm, z, nw, h, dt, dtb, alog, dsk, *, row0):
    valid = (row0 + _iota((SSD_CHUNK, 1), 0)) >= PAD
    dtp16 = jnp.where(valid, _softplus(dt + dtb), 0.0)
    adt16 = -jnp.exp(alog) * dtp16
    acum16 = _cumsum_rows(adt16)
    lanes = tuple((_iota((SSD_GROUPS, 1, 128), 2) == _iota((SSD_GROUPS, 1, 128), 0) * SSD_HPG + r).astype(f32)
                  for r in range(SSD_HPG))
    rows = tuple((_iota((SSD_GROUPS, 128, 1), 1) == _iota((SSD_GROUPS, 128, 1), 0) * SSD_HPG + r).astype(f32)
                 for r in range(SSD_HPG))
    core = jax.vmap(_ssd_core, in_axes=(0,) * 8 + (None,) * 5)
    return core(xs, bm, cm, z, nw, lanes, rows, h, dtp16, adt16, acum16, acum16.T, dsk)


def _ssd_specs(nc, rev):
    n = SSD_CHUNK

    def cidx(c):
        return (nc - 1 - c) if rev else c

    def col(base, w):
        return pl.BlockSpec((n, w), lambda c: (cidx(c), base // w))

    def halo(base, w):
        return pl.BlockSpec((8, w), lambda c: (jnp.maximum(cidx(c) * (n // 8) - 1, 0), base // w))

    def taps(base, w):
        return pl.BlockSpec((8, w), lambda c: (0, base // w))

    row = pl.BlockSpec((8, 128), lambda c: (0, 0))
    in_specs = [col(C_SX, 1024), col(C_SB, 512), col(C_SC, 512), halo(C_SX, 1024), halo(C_SB, 512), halo(C_SC, 512),
                col(C_SZ, 1024), col(C_SDT, 128), taps(0, 1024), taps(1024, 512), taps(1536, 512), row, row, row,
                taps(0, 1024)]
    y = pl.BlockSpec((n, D_MODEL), lambda c: (cidx(c), 0))
    st = pl.BlockSpec((1, SSD_GROUPS, SSD_N, 256), lambda c: (cidx(c), 0, 0, 0))
    return in_specs, y, st, col, taps, row


def _ssd_load(refs, first):
    xs, bm, cm, hx, hbm, hcm, z, dt, tx, tb, tc, dtb, alog, dsk, nw = refs

    def halo(r):
        return jnp.where(first, 0.0, r[...])

    def taps(r):
        return tuple(r[j:j + 1, :] for j in range(4))

    act = (xs[...], bm[...], cm[...], halo(hx), halo(hbm), halo(hcm), taps(tx), taps(tb), taps(tc),
           tx[4:5, :], tb[4:5, :], tc[4:5, :])
    return act, (z[...], nw[0:1, :]), (dt[...], dtb[0:1, :], alog[0:1, :], dsk[0:1, :])


def _groups(a, w):
    return jnp.stack([a[:, i * w:(i + 1) * w] for i in range(SSD_GROUPS)])


def ssd_fwd(u, conv_w8, dtb8, alog8, d8, nw8):
    t_rows = u.shape[0]
    nc = t_rows // SSD_CHUNK
    in_specs, y_spec, st_spec, *_ = _ssd_specs(nc, False)

    def body(*refs):
        ins, (y_ref, st_ref), (h_scr,) = refs[:15], refs[15:17], refs[17:]
        c = pl.program_id(0)

        @pl.when(c == 0)
        def _():
            h_scr[...] = jnp.zeros_like(h_scr)

        act, (z, nw), shared = _ssd_load(ins, c == 0)
        h = h_scr[...]
        st_ref[0] = h
        xs, bm, cm = _ssd_act(*act, row0=c * SSD_CHUNK)
        y, h_new = _ssd_chunk(_groups(xs, 256), _groups(bm, 128), _groups(cm, 128), _groups(z, 256),
                              _groups(nw, 256), h, *shared, row0=c * SSD_CHUNK)
        y_ref[...] = _wide(y).astype(bf16)
        h_scr[...] = h_new

    return pl.pallas_call(
        body, name="ssd_fwd", grid=(nc,), in_specs=in_specs, out_specs=(y_spec, st_spec),
        out_shape=(jax.ShapeDtypeStruct((t_rows, D_MODEL), bf16),
                   jax.ShapeDtypeStruct((nc, SSD_GROUPS, SSD_N, 256), f32)),
        scratch_shapes=[pltpu.VMEM((SSD_GROUPS, SSD_N, 256), f32)],
        compiler_params=_params(("arbitrary",)),
    )(u, u, u, u, u, u, u, u, conv_w8, conv_w8, conv_w8, dtb8, alog8, d8, nw8)


def ssd_bwd(u, conv_w8, dtb8, alog8, d8, nw8, states, dy, du):
    t_rows = u.shape[0]
    nc = t_rows // SSD_CHUNK
    n = SSD_CHUNK
    in_specs, y_spec, st_spec, col, taps, row = _ssd_specs(nc, True)

    def body(*refs):
        ins, st_ref, dy_ref = refs[:15], refs[15], refs[16]
        du_ref, ddt_ref, dtx_ref, dtb_ref, dtc_ref, ddtb_ref, dalog_ref, ddsk_ref, dnw_ref = refs[18:27]
        dh_scr, hx_scr, hb_scr, hc_scr = refs[27:]
        cc = pl.program_id(0)
        c = nc - 1 - cc

        @pl.when(cc == 0)
        def _():
            for r in (dh_scr, hx_scr, hb_scr, hc_scr, dtx_ref, dtb_ref, dtc_ref, dnw_ref, ddtb_ref, dalog_ref, ddsk_ref):
                r[...] = jnp.zeros_like(r)

        act, (z, nw), shared = _ssd_load(ins, c == 0)
        (xs, bm, cm), vjp_act = jax.vjp(functools.partial(_ssd_act, row0=c * n), *act)
        _, vjp_core = jax.vjp(functools.partial(_ssd_chunk, row0=c * n), _groups(xs, 256), _groups(bm, 128),
                              _groups(cm, 128), _groups(z, 256), _groups(nw, 256), st_ref[0], *shared)
        dxa, dba, dca, dz, dnw, dh, ddt, ddtb, dalog, ddsk = vjp_core(
            (_groups(dy_ref[...].astype(f32), 256), dh_scr[...]))
        dh_scr[...] = dh
        dxs, dbm, dcm, dhx, dhb, dhc, dtx, dtb, dtc, dbx, dbb, dbc = vjp_act((_wide(dxa), _wide(dba), _wide(dca)))
        du_ref[:, 0:D_MODEL] = _wide(dz).astype(bf16)
        for dx, dhalo, scr, lo in ((dxs, dhx, hx_scr, C_SX), (dbm, dhb, hb_scr, C_SB), (dcm, dhc, hc_scr, C_SC)):
            zeros = jnp.zeros((n - 8, dx.shape[1]), f32)
            du_ref[:, lo - C_SZ:lo - C_SZ + dx.shape[1]] = (dx + jnp.concatenate([zeros, scr[...]], axis=0)).astype(bf16)
            scr[...] = dhalo
        ddt_ref[...] = ddt
        for ref, dtaps, dbias in ((dtx_ref, dtx, dbx), (dtb_ref, dtb, dbb), (dtc_ref, dtc, dbc)):
            for j in range(4):
                ref[j:j + 1, :] += dtaps[j]
            ref[4:5, :] += dbias
        ddtb_ref[0:1, :] += ddtb
        dalog_ref[0:1, :] += dalog
        ddsk_ref[0:1, :] += ddsk
        dnw_ref[0:1, :] += _wide(dnw)

    def out_col(w):
        return pl.BlockSpec((n, w), lambda c: (nc - 1 - c, 0))

    out_specs = (pl.BlockSpec((n, 3 * D_MODEL), lambda c: (nc - 1 - c, C_SZ // (3 * D_MODEL))), out_col(128),
                 taps(0, D_MODEL), taps(0, 512), taps(0, 512), row, row, row, taps(0, D_MODEL))
    out_shape = (jax.ShapeDtypeStruct(du.shape, du.dtype),
                 jax.ShapeDtypeStruct((t_rows, 128), f32),
                 jax.ShapeDtypeStruct((8, D_MODEL), f32), jax.ShapeDtypeStruct((8, 512), f32),
                 jax.ShapeDtypeStruct((8, 512), f32),
                 jax.ShapeDtypeStruct((8, 128), f32), jax.ShapeDtypeStruct((8, 128), f32),
                 jax.ShapeDtypeStruct((8, 128), f32), jax.ShapeDtypeStruct((8, D_MODEL), f32))
    return pl.pallas_call(
        body, name="ssd_bwd", grid=(nc,), in_specs=in_specs + [st_spec, y_spec, ANY],
        out_specs=out_specs, out_shape=out_shape, input_output_aliases={17: 0},
        scratch_shapes=[pltpu.VMEM((SSD_GROUPS, SSD_N, 256), f32), pltpu.VMEM((8, D_MODEL), f32),
                        pltpu.VMEM((8, 512), f32), pltpu.VMEM((8, 512), f32)],
        compiler_params=_params(("arbitrary",)),
    )(u, u, u, u, u, u, u, u, conv_w8, conv_w8, conv_w8, dtb8, alog8, d8, nw8, states, dy, du)


NEG = -1e30


def _swa_core(q, kc, kp, km, vc, vp, vm, sink, *, n):
    rows = SWA_REP * SWA_W
    ri, ci = _iota((rows, SWA_W), 0) & (SWA_W - 1), _iota((rows, SWA_W), 1)
    causal = ci <= ri
    m_band = (causal & ((n >= 1) | ((ci >= PAD) & (ri >= PAD)))) | ((ci > ri) & (n >= 2))
    m_meta = (n >= 1) & (ci >= PAD)
    q = q * (SWA_D ** -0.5)
    s = jnp.where(m_band, jnp.where(causal, _dot(q, kc, NT), _dot(q, kp, NT)), NEG)
    sm = jnp.where(m_meta, _dot(q, km, NT), NEG)
    mx = jnp.maximum(jnp.maximum(jnp.max(s, axis=1, keepdims=True), jnp.max(sm, axis=1, keepdims=True)), sink)
    mx = lax.stop_gradient(mx)
    e, em = jnp.exp(s - mx), jnp.exp(sm - mx)
    den = jnp.sum(e, axis=1, keepdims=True) + jnp.sum(em, axis=1, keepdims=True) + jnp.exp(sink - mx)
    return (_dot(jnp.where(causal, e, 0.0), vc) + _dot(jnp.where(causal, 0.0, e), vp) + _dot(em, vm)) / den


def _swa_block(q16, kc, kp, km, vc, vp, vm, sink16, *, n):
    rows = SWA_REP * SWA_W
    lane = _iota((1, 128), 1)
    rep = _iota((rows, 1), 0) >> 7
    cols = []
    for h in range(SWA_KV_HEADS):
        col = jnp.zeros((rows, 1), f32)
        for r in range(SWA_REP):
            s = jnp.sum(jnp.where(lane == h * SWA_REP + r, sink16, 0.0), axis=1, keepdims=True)
            col = jnp.where(rep == r, s, col)
        cols.append(col)
    o = jax.vmap(functools.partial(_swa_core, n=n))(q16.reshape(SWA_KV_HEADS, rows, SWA_D), kc, kp, km, vc, vp, vm,
                                                    jnp.concatenate([col[None] for col in cols], axis=0))
    return o.reshape(q16.shape)


def _swa_specs(nb, rev):
    def bidx(n):
        return (nb - 1 - n) if rev else n

    kvw = SWA_KV_HEADS * SWA_D
    q = pl.BlockSpec((SWA_W, D_MODEL), lambda n: (bidx(n), C_WQ // D_MODEL))

    def kv(base, blk):
        return pl.BlockSpec((SWA_W, kvw), lambda n: (blk(bidx(n)), base // kvw))

    cur, prev, meta = (lambda n: n), (lambda n: jnp.maximum(n - 1, 0)), (lambda n: 0)
    row = pl.BlockSpec((8, 128), lambda n: (0, 0))
    in_specs = [q] + [kv(C_WK, b) for b in (cur, prev, meta)] + [kv(C_WV, b) for b in (cur, prev, meta)] + [row]
    return in_specs, pl.BlockSpec((SWA_W, D_MODEL), lambda n: (bidx(n), 0)), row


def _swa_heads(a):
    return jnp.stack([a[:, i * SWA_D:(i + 1) * SWA_D] for i in range(a.shape[1] // SWA_D)])


def swa_fwd(u, sink8):
    t_rows = u.shape[0]
    nb = t_rows // SWA_W
    in_specs, o_spec, _ = _swa_specs(nb, False)

    def body(q_ref, kc, kp, km, vc, vp, vm, sink_ref, o_ref):
        o = _swa_block(*[_swa_heads(r[...]) for r in (q_ref, kc, kp, km, vc, vp, vm)], sink_ref[0:1, :],
                       n=pl.program_id(0))
        o_ref[...] = _wide(o).astype(bf16)

    return pl.pallas_call(
        body, name="swa_fwd", grid=(nb,), in_specs=in_specs, out_specs=o_spec,
        out_shape=jax.ShapeDtypeStruct((t_rows, D_MODEL), bf16),
        compiler_params=_params(("arbitrary",)),
    )(u, u, u, u, u, u, u, sink8)


def swa_bwd(u, sink8, do, du):
    t_rows = u.shape[0]
    nb = t_rows // SWA_W
    in_specs, o_spec, row = _swa_specs(nb, True)
    width = C_BA - C_WQ

    def body(q_ref, kc, kp, km, vc, vp, vm, sink_ref, do_ref, _, du_ref, dsink_ref,
             dkp_scr, dvp_scr, dkm_scr, dvm_scr):
        nn = pl.program_id(0)
        n = nb - 1 - nn

        @pl.when(nn == 0)
        def _():
            for r in (dkp_scr, dvp_scr, dkm_scr, dvm_scr, dsink_ref):
                r[...] = jnp.zeros_like(r)

        fn = functools.partial(_swa_block, n=n)
        _, vjp = jax.vjp(fn, *[_swa_heads(r[...]) for r in (q_ref, kc, kp, km, vc, vp, vm)], sink_ref[0:1, :])
        dq, dkc, dkp, dkm, dvc, dvp, dvm, dsink = vjp(_swa_heads(do_ref[...]))
        dkm_scr[...] += dkm
        dvm_scr[...] += dvm
        first = n == 0
        dk = dkc + dkp_scr[...] + jnp.where(first, dkm_scr[...], 0.0)
        dv = dvc + dvp_scr[...] + jnp.where(first, dvm_scr[...], 0.0)
        du_ref[:, 0:D_MODEL] = _wide(dq).astype(bf16)
        du_ref[:, C_WK - C_WQ:C_WV - C_WQ] = _wide(dk).astype(bf16)
        du_ref[:, C_WV - C_WQ:width] = _wide(dv).astype(bf16)
        dkp_scr[...] = dkp
        dvp_scr[...] = dvp
        dsink_ref[0:1, :] += dsink

    return pl.pallas_call(
        body, name="swa_bwd", grid=(nb,), in_specs=in_specs + [o_spec, ANY],
        out_specs=(pl.BlockSpec((SWA_W, width), lambda n: (nb - 1 - n, C_WQ // width)), row),
        out_shape=(jax.ShapeDtypeStruct(du.shape, du.dtype), jax.ShapeDtypeStruct((8, 128), f32)),
        input_output_aliases={9: 0},
        scratch_shapes=[pltpu.VMEM((SWA_KV_HEADS, SWA_W, SWA_D), f32)] * 4,
        compiler_params=_params(("arbitrary",)),
    )(u, u, u, u, u, u, u, sink8, do, du)


def _tile(dim, prefs):
    for p in prefs:
        if dim % p == 0:
            return p
    return dim


def _row_tile(rows, d):
    for p in range(min(rows, BLOCK_BYTES // (4 * d)) // 8 * 8, 0, -8):
        if rows % p == 0:
            return p
    return rows


def mm(a, b, *, out_dtype, name, resid=None, relu_grad_of=None, relu2_out=False, ta=False, tb=False):
    assert resid is None or relu_grad_of is None
    k, m = (a.shape if ta else a.shape[::-1])
    n = b.shape[0] if tb else b.shape[1]
    rhs_stays = k * 2 * 1024 > MM_OPERAND_BYTES
    assert not (ta and rhs_stays)
    if rhs_stays:
        tn = _tile(n, tuple(p for p in (512, 256, 128) if p * k * 2 <= MM_RESIDENT_BYTES))
        tm = _tile(m, tuple(p for p in (512, 384, 256, 128) if p * k * 2 <= MM_OPERAND_BYTES // 2))
        grid = (n // tn, m // tm)
        ij = lambda o, i: (i, o)
    elif k * n * b.dtype.itemsize <= MM_OPERAND_BYTES:
        tn = n
        tm = _tile(m, tuple(p for p in (1408, 1024, 512, 384, 256, 128)
                            if p * k * 2 <= MM_OPERAND_BYTES and p * n * 4 <= MM_OPERAND_BYTES * 2 // 3))
        grid = (m // tm, 1)
        ij = lambda o, i: (o, i)
    else:
        tm = _tile(m, tuple(p for p in (1408, 1024, 512, 384, 256, 128) if p * k * 2 <= MM_OPERAND_BYTES))
        tn = _tile(n, tuple(p for p in (512, 256, 128) if p * k * 2 <= MM_OPERAND_BYTES // 2))
        grid = (m // tm, n // tn)
        ij = lambda o, i: (o, i)

    extra = resid if resid is not None else relu_grad_of

    def body(*refs):
        a_ref, b_ref = refs[:2]
        if ta:
            at_scr = refs[-1]
            refs = refs[:-1]

            @pl.when(pl.program_id(1) == 0)
            def _():
                at_scr[...] = a_ref[...].T

            lhs = at_scr[...]
        else:
            lhs = a_ref[...]
        o = _dot(lhs, b_ref[...], NT if tb else NN)
        if resid is not None:
            o = o + refs[2][...]
        if relu_grad_of is not None:
            o = o * (2.0 * jnp.maximum(refs[2][...], 0.0))
        if relu2_out:
            refs[-2][...] = o.astype(out_dtype)
            r = jnp.maximum(o, 0.0)
            refs[-1][...] = (r * r).astype(bf16)
        else:
            refs[-1][...] = o.astype(out_dtype)

    in_specs = [pl.BlockSpec((k, tm), lambda o, i: (0, ij(o, i)[0])) if ta
                else pl.BlockSpec((tm, k), lambda o, i: (ij(o, i)[0], 0)),
                pl.BlockSpec((tn, k), lambda o, i: (ij(o, i)[1], 0)) if tb
                else pl.BlockSpec((k, tn), lambda o, i: (0, ij(o, i)[1]))]
    args = [a, b]
    if extra is not None:
        in_specs.append(pl.BlockSpec((tm, tn), ij))
        args.append(extra)
    out_blk = pl.BlockSpec((tm, tn), ij)
    out = jax.ShapeDtypeStruct((m, n), out_dtype)
    return pl.pallas_call(
        body, name=name, grid=grid, in_specs=in_specs,
        out_specs=(out_blk, out_blk) if relu2_out else out_blk,
        out_shape=(out, jax.ShapeDtypeStruct((m, n), bf16)) if relu2_out else out,
        scratch_shapes=[pltpu.VMEM((tm, k), bf16)] if ta else [],
        compiler_params=_params(("parallel", "arbitrary" if ta else "parallel")),
    )(*args)


def _rows(t_rows):
    return _tile(t_rows, (384, 256, 128))


def _rmsnorm(h, w):
    return h * lax.rsqrt(jnp.mean(h * h, axis=1, keepdims=True) + RMS_EPS) * w


def rmsnorm_fwd(h, w8, *, name):
    t_rows, d = h.shape
    tr = _rows(t_rows)

    def body(h_ref, w_ref, o_ref):
        o_ref[...] = _rmsnorm(h_ref[...], w_ref[0:1, :]).astype(bf16)

    blk = pl.BlockSpec((tr, d), lambda i: (i, 0))
    return pl.pallas_call(
        body, name=name, grid=(t_rows // tr,), in_specs=[blk, pl.BlockSpec((8, d), lambda i: (0, 0))], out_specs=blk,
        out_shape=jax.ShapeDtypeStruct((t_rows, d), bf16), compiler_params=_params(("arbitrary",)),
    )(h, w8)


def rmsnorm_bwd(h, w8, dhn, dres, *, name):
    t_rows, d = h.shape
    tr = _rows(t_rows)

    def body(h_ref, w_ref, dhn_ref, dres_ref, dh_ref, dw_ref):
        @pl.when(pl.program_id(0) == 0)
        def _():
            dw_ref[...] = jnp.zeros_like(dw_ref)

        _, vjp = jax.vjp(_rmsnorm, h_ref[...], w_ref[0:1, :])
        dh, dw = vjp(dhn_ref[...])
        dh_ref[...] = dh + dres_ref[...]
        dw_ref[0:1, :] += dw

    blk = pl.BlockSpec((tr, d), lambda i: (i, 0))
    wblk = pl.BlockSpec((8, d), lambda i: (0, 0))
    return pl.pallas_call(
        body, name=name, grid=(t_rows // tr,), in_specs=[blk, wblk, blk, blk], out_specs=(blk, wblk),
        out_shape=(jax.ShapeDtypeStruct((t_rows, d), f32), jax.ShapeDtypeStruct((8, d), f32)),
        compiler_params=_params(("arbitrary",)),
    )(h, w8, dhn, dres)


def _merge(pg, ps, pw, la, lb, lc):
    return jax.nn.sigmoid(la) * pg + jax.nn.sigmoid(lb) * ps + jax.nn.sigmoid(lc) * pw


def _merge_specs(t_rows):
    tr = _rows(t_rows)
    blk = pl.BlockSpec((tr, D_MODEL), lambda i: (i, 0))
    gate = [pl.BlockSpec((tr, D_MODEL), functools.partial(lambda i, j: (i, j), j=C_GATE // D_MODEL + j)) for j in range(3)]
    return tr, blk, gate


def merge_fwd(pg, ps, pw, u):
    t_rows = pg.shape[0]
    tr, blk, gate = _merge_specs(t_rows)

    def body(pg_ref, ps_ref, pw_ref, la, lb, lc, o_ref):
        o_ref[...] = _merge(pg_ref[...], ps_ref[...], pw_ref[...], la[...], lb[...], lc[...]).astype(bf16)

    return pl.pallas_call(
        body, name="merge_fwd", grid=(t_rows // tr,), in_specs=[blk, blk, blk] + gate, out_specs=blk,
        out_shape=jax.ShapeDtypeStruct((t_rows, D_MODEL), bf16), compiler_params=_params(("arbitrary",)),
    )(pg, ps, pw, u, u, u)


def merge_bwd(pg, ps, pw, u, dmerged, du):
    t_rows = pg.shape[0]
    tr, blk, gate = _merge_specs(t_rows)

    def body(pg_ref, ps_ref, pw_ref, la, lb, lc, dm_ref, _, dpg_ref, dps_ref, dpw_ref, dl_ref):
        _, vjp = jax.vjp(_merge, pg_ref[...], ps_ref[...], pw_ref[...], la[...], lb[...], lc[...])
        dpg, dps, dpw, dla, dlb, dlc = vjp(dm_ref[...])
        dpg_ref[...] = dpg.astype(bf16)
        dps_ref[...] = dps.astype(bf16)
        dpw_ref[...] = dpw.astype(bf16)
        for j, dl in enumerate((dla, dlb, dlc)):
            dl_ref[:, j * D_MODEL:(j + 1) * D_MODEL] = dl.astype(bf16)

    act = jax.ShapeDtypeStruct((t_rows, D_MODEL), bf16)
    return pl.pallas_call(
        body, name="merge_bwd", grid=(t_rows // tr,), in_specs=[blk, blk, blk] + gate + [blk, ANY],
        out_specs=(blk, blk, blk, pl.BlockSpec((tr, 3 * D_MODEL), lambda i: (i, C_GATE // (3 * D_MODEL)))),
        out_shape=(act, act, act, jax.ShapeDtypeStruct(du.shape, du.dtype)),
        input_output_aliases={7: 3},
        compiler_params=_params(("arbitrary",)),
    )(pg, ps, pw, u, u, u, dmerged, du)


def relu2_fwd(a):
    t_rows, d = a.shape
    tr = _rows(t_rows)

    def body(a_ref, o_ref):
        r = jnp.maximum(a_ref[...], 0.0)
        o_ref[...] = (r * r).astype(bf16)

    blk = pl.BlockSpec((tr, d), lambda i: (i, 0))
    return pl.pallas_call(
        body, name="relu2_fwd", grid=(t_rows // tr,), in_specs=[blk], out_specs=blk,
        out_shape=jax.ShapeDtypeStruct((t_rows, d), bf16), compiler_params=_params(("arbitrary",)),
    )(a)


def relu2_bwd(a, dr):
    t_rows, d = a.shape
    tr = _rows(t_rows)

    def body(a_ref, dr_ref, o_ref):
        o_ref[...] = (dr_ref[...] * 2.0 * jnp.maximum(a_ref[...], 0.0)).astype(bf16)

    blk = pl.BlockSpec((tr, d), lambda i: (i, 0))
    return pl.pallas_call(
        body, name="relu2_bwd", grid=(t_rows // tr,), in_specs=[blk, blk], out_specs=blk,
        out_shape=jax.ShapeDtypeStruct((t_rows, d), bf16), compiler_params=_params(("arbitrary",)),
    )(a, dr)


def loss_head(h, w8, target):
    t_rows, d = h.shape
    tr = HEAD_ROWS

    def loss_fn(hb, w, tgt):
        err = _rmsnorm(hb, w) - tgt
        return 0.5 * jnp.sum(err * err) / d

    def body(h_ref, w_ref, t_ref, loss_ref, dh_ref, dw_ref):
        i = pl.program_id(0)

        @pl.when(i == 0)
        def _():
            loss_ref[...] = jnp.zeros_like(loss_ref)
            dw_ref[...] = jnp.zeros_like(dw_ref)
            dh_ref[...] = jnp.zeros_like(dh_ref)

        @pl.when(i > 0)
        def _():
            val, (dh, dw) = jax.value_and_grad(loss_fn, argnums=(0, 1))(h_ref[...], w_ref[0:1, :], t_ref[...])
            loss_ref[...] += val
            dh_ref[...] = dh
            dw_ref[0:1, :] += dw

    blk = pl.BlockSpec((tr, d), lambda i: (i, 0))
    wblk = pl.BlockSpec((8, d), lambda i: (0, 0))
    return pl.pallas_call(
        body, name="loss_head", grid=(t_rows // tr,),
        in_specs=[blk, wblk, pl.BlockSpec((tr, d), lambda i: (jnp.maximum(i - 1, 0), 0))],
        out_specs=(pl.BlockSpec((8, 128), lambda i: (0, 0)), blk, wblk),
        out_shape=(jax.ShapeDtypeStruct((8, 128), f32), jax.ShapeDtypeStruct((t_rows, d), f32),
                   jax.ShapeDtypeStruct((8, d), f32)),
        compiler_params=_params(("arbitrary",)),
    )(h, w8, target)


def adamw(w, m, v, partials, row_off, *, name):
    rows, d = w.shape
    layers = len(partials)
    per = rows // layers
    tr = _row_tile(per, d)
    assert row_off % tr == 0
    off, nblk = row_off // tr, per // tr
    c1 = 1.0 - ADAM_B1 ** ADAM_STEP
    c2 = 1.0 - ADAM_B2 ** ADAM_STEP

    def body(w_ref, m_ref, v_ref, *refs):
        p_refs, (g_ref, d_ref, mo_ref, vo_ref) = refs[:2 * layers], refs[2 * layers:]
        g = p_refs[0][...] + p_refs[1][...]
        for l in range(1, layers):
            g = jnp.where(pl.program_id(0) >= l * nblk, p_refs[2 * l][...] + p_refs[2 * l + 1][...], g)
        m_new = ADAM_B1 * m_ref[...] + (1.0 - ADAM_B1) * g
        v_new = ADAM_B2 * v_ref[...] + (1.0 - ADAM_B2) * (g * g)
        g_ref[...] = g
        d_ref[...] = -ADAM_LR * ((m_new / c1) / (jnp.sqrt(v_new / c2) + ADAM_EPS) + ADAM_WD * w_ref[...])
        mo_ref[...] = m_new
        vo_ref[...] = v_new

    blk = pl.BlockSpec((tr, d), lambda i: (i, 0))
    pblks = [pl.BlockSpec((tr, d), functools.partial(lambda i, l: (off + jnp.clip(i - l * nblk, 0, nblk - 1), 0), l=l))
             for l in range(layers) for _ in range(2)]
    out = jax.ShapeDtypeStruct((rows, d), f32)
    return pl.pallas_call(
        body, name=name, grid=(rows // tr,), in_specs=[blk, blk, blk] + pblks, out_specs=(blk,) * 4,
        out_shape=(out,) * 4, compiler_params=_params(("arbitrary",)),
    )(w, m, v, *[p for pair in partials for p in pair])


def reduce4(parts, *, name, own=None, me=None):
    _, rows, d = parts.shape
    tr = _row_tile(rows, d)

    def body(*refs):
        p_ref, o_ref = refs[0], refs[-1]
        acc = None
        for s in range(4):
            term = p_ref[s].astype(f32)
            if own is not None:
                term = jnp.where(refs[2][0] == s, refs[1][...].astype(f32), term)
            acc = term if acc is None else acc + term
        o_ref[...] = acc

    in_specs = [pl.BlockSpec((4, tr, d), lambda i: (0, i, 0))]
    args = [parts]
    if own is not None:
        in_specs += [pl.BlockSpec((tr, d), lambda i: (i, 0)), pl.BlockSpec(memory_space=pltpu.SMEM)]
        args += [own, me]
    return pl.pallas_call(
        body, name=name, grid=(rows // tr,), in_specs=in_specs,
        out_specs=pl.BlockSpec((tr, d), lambda i: (i, 0)), out_shape=jax.ShapeDtypeStruct((rows, d), f32),
        compiler_params=_params(("arbitrary",)),
    )(*args)


ANY = pl.BlockSpec(memory_space=pl.ANY)
MESH = pl.DeviceIdType.MESH
CHIP_FLIPS = ((0, 1), (1, 0), (1, 1))


def chip_exchange(bufs, scatter, *, name, after=None):
    nb = len(bufs)
    extra = [] if after is None else [after]

    def body(*refs):
        ins, outs = refs[:nb], refs[nb + len(extra):2 * nb + len(extra)]
        send_sems, recv_sems, local_sems = refs[2 * nb + len(extra):]
        x, y, c = lax.axis_index("x"), lax.axis_index("y"), lax.axis_index("c")
        me = 2 * x + y
        local = [pltpu.make_async_copy(ins[j].at[me] if scatter[j] else ins[j], outs[j].at[me], local_sems.at[j])
                 for j in range(nb)]
        for cp in local:
            cp.start()
        sends, recvs = [], []
        for k, (fx, fy) in enumerate(CHIP_FLIPS):
            px = 1 - x if fx else x
            py = 1 - y if fy else y
            chip = 2 * px + py
            for j in range(nb):
                src = ins[j].at[chip] if scatter[j] else ins[j]
                sems = dict(send_sem=send_sems.at[nb * k + j], recv_sem=recv_sems.at[nb * k + j],
                            device_id=(px, py, c), device_id_type=MESH)
                sends.append(pltpu.make_async_remote_copy(src_ref=src, dst_ref=outs[j].at[me], **sems))
                recvs.append(pltpu.make_async_remote_copy(src_ref=src, dst_ref=outs[j].at[chip], **sems))
        for cp in sends:
            cp.start()
        for cp in recvs:
            cp.wait_recv()
        for cp in sends:
            cp.wait_send()
        for cp in local:
            cp.wait()

    out_shape = tuple(jax.ShapeDtypeStruct(b.shape if s else (4,) + b.shape, b.dtype) for b, s in zip(bufs, scatter))
    return pl.pallas_call(
        body, name=name, in_specs=[ANY] * (nb + len(extra)), out_specs=(ANY,) * nb, out_shape=out_shape,
        scratch_shapes=[pltpu.SemaphoreType.DMA((3 * nb,)), pltpu.SemaphoreType.DMA((3 * nb,)),
                        pltpu.SemaphoreType.DMA((nb,))],
        compiler_params=pltpu.CompilerParams(has_side_effects=True),
    )(*bufs, *extra)


def gather_two_level(big, small, *, name):
    half = big.shape[1] // 2

    def body(big_ref, small_ref, obig_ref, osmall_ref, send_sems, recv_sems, local_sems):
        x, y, c = lax.axis_index("x"), lax.axis_index("y"), lax.axis_index("c")
        me = 2 * x + y
        mine = (slice(None), pl.ds(pl.multiple_of(c * half, half), half))
        theirs = (slice(None), pl.ds(pl.multiple_of((1 - c) * half, half), half))
        local = [pltpu.make_async_copy(big_ref, obig_ref.at[me], local_sems.at[0]),
                 pltpu.make_async_copy(small_ref, osmall_ref.at[me], local_sems.at[1])]
        for cp in local:
            cp.start()

        def copy(k, src, dst, to):
            return pltpu.make_async_remote_copy(src_ref=src, dst_ref=dst, send_sem=send_sems.at[k],
                                                recv_sem=recv_sems.at[k], device_id=to, device_id_type=MESH)

        sends, landed, passed, small_in = [], [], [], []
        for k, (fx, fy) in enumerate(CHIP_FLIPS):
            px = 1 - x if fx else x
            py = 1 - y if fy else y
            chip = 2 * px + py
            sends.append(copy(k, big_ref.at[mine], obig_ref.at[(me,) + mine], (px, py, c)))
            landed.append(copy(k, big_ref.at[mine], obig_ref.at[(chip,) + mine], (px, py, c)))
            sends.append(copy(3 + k, small_ref, osmall_ref.at[me], (px, py, c)))
            small_in.append(copy(3 + k, small_ref, osmall_ref.at[chip], (px, py, c)))
            passed.append((copy(6 + k, obig_ref.at[(chip,) + mine], obig_ref.at[(chip,) + mine], (x, y, 1 - c)),
                           copy(6 + k, obig_ref.at[(chip,) + theirs], obig_ref.at[(chip,) + theirs], (x, y, 1 - c))))
        for cp in sends:
            cp.start()
        for k in range(3):
            landed[k].wait_recv()
            passed[k][0].start()
        for k in range(3):
            passed[k][1].wait_recv()
            small_in[k].wait_recv()
        for cp in sends + [p[0] for p in passed]:
            cp.wait_send()
        for cp in local:
            cp.wait()

    return pl.pallas_call(
        body, name=name, in_specs=[ANY, ANY], out_specs=(ANY, ANY),
        out_shape=(jax.ShapeDtypeStruct((4,) + big.shape, big.dtype),
                   jax.ShapeDtypeStruct((4,) + small.shape, small.dtype)),
        scratch_shapes=[pltpu.SemaphoreType.DMA((9,)), pltpu.SemaphoreType.DMA((9,)), pltpu.SemaphoreType.DMA((2,))],
        compiler_params=pltpu.CompilerParams(has_side_effects=True),
    )(big, small)


def sibling_swap(bufs, *, name):
    nb = len(bufs)

    def body(*refs):
        ins, outs, (send_sems, recv_sems) = refs[:nb], refs[nb:2 * nb], refs[2 * nb:]
        peer = (lax.axis_index("x"), lax.axis_index("y"), 1 - lax.axis_index("c"))
        copies = [pltpu.make_async_remote_copy(src_ref=ins[j], dst_ref=outs[j], send_sem=send_sems.at[j],
                                               recv_sem=recv_sems.at[j], device_id=peer, device_id_type=MESH)
                  for j in range(nb)]
        for cp in copies:
            cp.start()
        for cp in copies:
            cp.wait_recv()
        for cp in copies:
            cp.wait_send()

    return pl.pallas_call(
        body, name=name, in_specs=[ANY] * nb, out_specs=(ANY,) * nb,
        out_shape=tuple(jax.ShapeDtypeStruct(b.shape, b.dtype) for b in bufs),
        scratch_shapes=[pltpu.SemaphoreType.DMA((nb,)), pltpu.SemaphoreType.DMA((nb,))],
        compiler_params=pltpu.CompilerParams(has_side_effects=True),
    )(*bufs)


HBM = pl.BlockSpec(memory_space=pltpu.HBM)
SEM = pl.BlockSpec(memory_space=pltpu.SEMAPHORE)
DATAFLOW = pltpu.SideEffectType.DATAFLOW_SIDE_EFFECTING


def _exchange_copies(srcs, lands, send_sems, recv_sems, scatter):
    x, y, c = lax.axis_index("x"), lax.axis_index("y"), lax.axis_index("c")
    me = 2 * x + y
    nb = len(srcs)
    pairs = []
    for k, (fx, fy) in enumerate(CHIP_FLIPS):
        px = 1 - x if fx else x
        py = 1 - y if fy else y
        chip = 2 * px + py
        for j in range(nb):
            src = srcs[j].at[chip] if scatter[j] else srcs[j]
            sems = dict(send_sem=send_sems.at[nb * k + j], recv_sem=recv_sems.at[nb * k + j],
                        device_id=(px, py, c), device_id_type=MESH)
            pairs.append((pltpu.make_async_remote_copy(src_ref=src, dst_ref=lands[j].at[me], **sems),
                          pltpu.make_async_remote_copy(src_ref=src, dst_ref=lands[j].at[chip], **sems)))
    return pairs


def exchange_start(bufs, scatter, after, *, name):
    nb = len(bufs)
    slabs = [b.shape[1:] if s else b.shape for b, s in zip(bufs, scatter)]
    lands = [lax.empty((4,) + shp, b.dtype) for b, shp in zip(bufs, slabs)]

    def body(*refs):
        srcs, zones = refs[:nb], refs[nb:2 * nb]
        send_sems, recv_sems = refs[2 * nb + 1:2 * nb + 3]
        token = refs[-1]
        for send, _ in _exchange_copies(srcs, zones, send_sems, recv_sems, scatter):
            send.start()
        token[...] = jnp.zeros_like(token)

    hbm = lambda a: pltpu.with_memory_space_constraint(a, pltpu.HBM)
    out = pl.pallas_call(
        body, name=name, in_specs=[HBM] * (2 * nb) + [ANY],
        out_specs=(SEM, SEM) + (HBM,) * (2 * nb) + (pl.BlockSpec(memory_space=pltpu.VMEM),),
        out_shape=(pltpu.SemaphoreType.DMA((3 * nb,)), pltpu.SemaphoreType.DMA((3 * nb,)))
        + tuple(pltpu.HBM(a.shape, a.dtype) for a in list(bufs) + lands) + (jax.ShapeDtypeStruct((8, 128), f32),),
        input_output_aliases={i: 2 + i for i in range(2 * nb)},
        compiler_params=pltpu.CompilerParams(has_side_effects=DATAFLOW),
    )(*[hbm(a) for a in list(bufs) + lands], after)
    return (out[:2], out[2:2 + nb], out[2 + nb:2 + 2 * nb], scatter), out[-1]


def exchange_wait(state, after, *, name):
    (send_sems, recv_sems), srcs, lands, scatter = state
    nb = len(srcs)

    def body(*refs):
        src_refs, zones = refs[:nb], refs[nb:2 * nb]
        s_sems, r_sems = refs[2 * nb:2 * nb + 2]
        for send, recv in _exchange_copies(src_refs, zones, s_sems, r_sems, scatter):
            send.wait_send()
            recv.wait_recv()

    out = pl.pallas_call(
        body, name=name, in_specs=[HBM] * (2 * nb) + [SEM, SEM, ANY], out_specs=(HBM,) * (2 * nb),
        out_shape=tuple(pltpu.HBM(a.shape, a.dtype) for a in list(srcs) + list(lands)),
        input_output_aliases={i: i for i in range(2 * nb)},
        compiler_params=pltpu.CompilerParams(has_side_effects=DATAFLOW),
    )(*srcs, *lands, send_sems, recv_sems, after)
    return out[nb:]


BIG = (
    ("w_proj_gdn", 256), ("w_proj_ssd", 256), ("w_proj_swa", 256), ("w_out", 256), ("w_up", 1024), ("w_down", 1024))
BIG_OFF = {}
_o = 0
for _n, _r in BIG:
    BIG_OFF[_n] = _o
    _o += _r
BIG_ROWS = _o
W_IN_SHARD = IN_W // 4

W_NAMES = ('meta_tokens', 'norm1_w', 'w_in', 'gdn_conv_w', 'gdn_a_log', 'gdn_dt_bias', 'gdn_norm_w', 'ssd_conv_w',
           'ssd_conv_b', 'ssd_dt_bias', 'ssd_a_log', 'ssd_d', 'ssd_norm_w', 'swa_sinks', 'w_proj_gdn', 'w_proj_ssd',
           'w_proj_swa', 'w_out', 'norm2_w', 'w_up', 'w_down', 'final_norm_w')
SMALL_NAMES = tuple(n for n in W_NAMES if n not in BIG_OFF and n != "w_in")
SMALL_SHARDED = ("meta_tokens", "gdn_conv_w", "ssd_conv_w")


def _pad_rows(a, rows):
    return jnp.pad(a, ((0, rows - a.shape[0]), (0, 0)))


def _pack_rows(parts, dtype):
    flat = jnp.concatenate([p.reshape(-1).astype(dtype) for p in parts])
    n = -(-flat.shape[0] // 8192) * 8192
    return jnp.pad(flat, (0, n - flat.shape[0])).reshape(-1, D_MODEL)


def _unpack_rows(packed, shapes):
    flat, out, o = packed.reshape(-1), [], 0
    for s in shapes:
        n = 1
        for d in s:
            n *= d
        out.append(flat[o:o + n].reshape(s))
        o += n
    return out


def _split_chips(full, axis):
    s = full.shape
    a = full.reshape(s[:axis] + (4, s[axis] // 4) + s[axis + 1:])
    return jnp.moveaxis(a, axis, 0)


def _join_chips(parts, axis):
    a = jnp.moveaxis(parts, 0, axis)
    s = a.shape
    return a.reshape(s[:axis] + (s[axis] * s[axis + 1],) + s[axis + 2:])


BIG_AXIS = {"w_in": 2, "w_proj_gdn": 1, "w_proj_ssd": 1, "w_proj_swa": 1, "w_out": 1, "w_up": 2, "w_down": 1}


def _w_in_to_padded(w):
    z = lambda n: jnp.zeros((n,) + w.shape[1:], w.dtype)
    return jnp.concatenate([w[8736:11808], w[4112:7184], w[7200:8736], w[4096:4112], z(112),
                            w[7184:7200], z(112 + C_MID_END - C_SDT - 128), w[0:4096]], axis=0)


def _w_in_from_padded(p):
    return jnp.concatenate([p[C_GQ:IN_WP], p[C_BA:C_BA + 16], p[C_SZ:C_WQ], p[C_SDT:C_SDT + 16], p[C_WQ:C_BA],
                            p[0:C_SZ]], axis=0)


def _row8(v, lane0=0, width=128):
    return jnp.pad(v[None, :], ((0, 7), (lane0, width - lane0 - v.shape[0])))


def _layer_fwd(h, p, l, late=None):
    tag = f"l{l}"
    hn = rmsnorm_fwd(h, p["n1"], name=f"norm1_fwd_{tag}")
    u = mm(hn, p["w_in_t"], tb=True, out_dtype=f32, name=f"mm_in_{tag}")
    yg, stg, tg = gdn_fwd(u, p["gcw"], p["galog"], p["gdtb"], p["gnw"])
    ys, sts = ssd_fwd(u, p["scw"], p["sdtb"], p["salog"], p["sd"], p["snw"])
    yw = swa_fwd(u, p["sink"])
    if late is not None:
        p.update(late(yw))
    pg = mm(yg, p["wpg"], out_dtype=f32, name=f"mm_pg_{tag}")
    ps = mm(ys, p["wps"], out_dtype=f32, name=f"mm_ps_{tag}")
    pw = mm(yw, p["wpw"], out_dtype=f32, name=f"mm_pw_{tag}")
    merged = merge_fwd(pg, ps, pw, u)
    h2 = mm(merged, p["wout"], out_dtype=f32, resid=h, name=f"mm_out_{tag}")
    hn2 = rmsnorm_fwd(h2, p["n2"], name=f"norm2_fwd_{tag}")
    a, r = mm(hn2, p["wup"], out_dtype=f32, relu2_out=True, name=f"mm_up_{tag}")
    h3 = mm(r, p["wdown"], out_dtype=f32, resid=h2, name=f"mm_down_{tag}")
    saved = dict(h=h, hn=hn, u=u, yg=yg, stg=stg, tg=tg, ys=ys, sts=sts, yw=yw, pg=pg, ps=ps, pw=pw,
                 merged=merged, h2=h2, hn2=hn2, a=a, r=r)
    return h3, saved


def _layer_bwd(dh3, p, s, l, send_big, send_w_in):
    tag = f"l{l}"
    g = {}

    def wgrad(act, d, name):
        return mm(act, d, ta=True, out_dtype=bf16, name=f"wg_{name}_{tag}")

    da = mm(dh3, p["wdown"], tb=True, out_dtype=bf16, relu_grad_of=s["a"], name=f"dg_down_{tag}")
    g["w_down"] = wgrad(s["r"], dh3, "down")
    dhn2 = mm(da, p["wup"], tb=True, out_dtype=f32, name=f"dg_up_{tag}")
    g["w_up"] = wgrad(s["hn2"], da, "up")
    dh2, g["norm2_w"] = rmsnorm_bwd(s["h2"], p["n2"], dhn2, dh3, name=f"norm2_bwd_{tag}")
    dmerged = mm(dh2, p["wout"], tb=True, out_dtype=f32, name=f"dg_out_{tag}")
    g["w_out"] = wgrad(s["merged"], dh2, "out")
    du = lax.empty((dh3.shape[0], IN_WP), bf16)
    dpg, dps, dpw, du = merge_bwd(s["pg"], s["ps"], s["pw"], s["u"], dmerged, du)
    dyg = mm(dpg, p["wpg"], tb=True, out_dtype=f32, name=f"dg_pg_{tag}")
    dys = mm(dps, p["wps"], tb=True, out_dtype=f32, name=f"dg_ps_{tag}")
    dyw = mm(dpw, p["wpw"], tb=True, out_dtype=f32, name=f"dg_pw_{tag}")
    g["w_proj_gdn"] = wgrad(s["yg"], dpg, "pg")
    g["w_proj_ssd"] = wgrad(s["ys"], dps, "ps")
    g["w_proj_swa"] = wgrad(s["yw"], dpw, "pw")
    sent = send_big(jnp.concatenate([_split_chips(g.pop(n), BIG_AXIS[n] - 1).reshape(4, r, D_MODEL)
                                     for n, r in BIG], axis=1))

    (du, dba, dtq, dtk, dtv, g["gdn_a_log"], g["gdn_dt_bias"], g["gdn_norm_w"]) = gdn_bwd(
        s["u"], p["gcw"] + sent, p["galog"], p["gdtb"], p["gnw"], s["stg"], s["tg"], dyg, du)
    g["gdn_conv_w"] = jnp.concatenate([dtq, dtk, dtv], axis=1)[:4]
    (du, ddt, dtx, dtb, dtc, g["ssd_dt_bias"], g["ssd_a_log"], g["ssd_d"], g["ssd_norm_w"]) = ssd_bwd(
        s["u"], p["scw"], p["sdtb"], p["salog"], p["sd"], p["snw"], s["sts"], dys, du)
    dconv = jnp.concatenate([dtx, dtb, dtc], axis=1)
    g["ssd_conv_w"], g["ssd_conv_b"] = dconv[:4], dconv[4]
    du, g["swa_sinks"] = swa_bwd(s["u"], p["sink"], dyw, du)
    mid = jnp.concatenate([dba[0].astype(bf16), ddt.astype(bf16),
                           jnp.zeros((du.shape[0], C_MID_END - C_SDT - 128), bf16)], axis=1)
    du = lax.dynamic_update_slice(du, mid, (0, C_BA))
    sent = send_w_in(_w_in_from_padded(wgrad(du, s["hn"], "in")).reshape(4, W_IN_SHARD, D_MODEL))
    dhn = mm(du, p["w_in_t"], out_dtype=f32, name=f"dg_in_{tag}")
    dh, g["norm1_w"] = rmsnorm_bwd(s["h"], p["n1"] + sent, dhn, dh2, name=f"norm1_bwd_{tag}")
    return dh, g


def kernel(x, meta_tokens, norm1_w, w_in, gdn_conv_w, gdn_a_log, gdn_dt_bias, gdn_norm_w, ssd_conv_w, ssd_conv_b, ssd_dt_bias, ssd_a_log, ssd_d, ssd_norm_w, swa_sinks, w_proj_gdn, w_proj_ssd, w_proj_swa, w_out, norm2_w, w_up, w_down, final_norm_w, loss_target, m_meta_tokens, m_norm1_w, m_w_in, m_gdn_conv_w, m_gdn_a_log, m_gdn_dt_bias, m_gdn_norm_w, m_ssd_conv_w, m_ssd_conv_b, m_ssd_dt_bias, m_ssd_a_log, m_ssd_d, m_ssd_norm_w, m_swa_sinks, m_w_proj_gdn, m_w_proj_ssd, m_w_proj_swa, m_w_out, m_norm2_w, m_w_up, m_w_down, m_final_norm_w, v_meta_tokens, v_norm1_w, v_w_in, v_gdn_conv_w, v_gdn_a_log, v_gdn_dt_bias, v_gdn_norm_w, v_ssd_conv_w, v_ssd_conv_b, v_ssd_dt_bias, v_ssd_a_log, v_ssd_d, v_ssd_norm_w, v_swa_sinks, v_w_proj_gdn, v_w_proj_ssd, v_w_proj_swa, v_w_out, v_norm2_w, v_w_up, v_w_down, v_final_norm_w):
    given = dict(locals())
    depth = norm1_w.shape[0]
    me = 2 * lax.axis_index("x") + lax.axis_index("y")

    me1 = jnp.reshape(me, (1,)).astype(jnp.int32)
    is_me = (jnp.arange(4, dtype=jnp.int32) == me)[:, None, None]
    w_in_t = jnp.swapaxes(w_in, 1, 2)

    def weight_slabs(l):
        return (w_in_t[l].astype(bf16),
                jnp.concatenate([given[n][l].reshape(-1, D_MODEL).astype(bf16) for n, _ in BIG]))

    slabs = [weight_slabs(l) for l in range(depth)]
    wsmall = _pack_rows([given[n] for n in SMALL_SHARDED], f32)
    ga0, gsmall = gather_two_level(slabs[0][0], wsmall, name="gather_first")
    gathers, started = {}, jnp.zeros((), f32)
    for l in range(depth):
        for j in range(2):
            if (l, j) != (0, 0):
                gathers[l, j], token = exchange_start([slabs[l][j]], (False,), gsmall, name=f"gather_start_l{l}_{j}")
                started = started + token[0, 0]
    shard_shapes = [given[n].shape for n in SMALL_SHARDED]
    per_chip = [_unpack_rows(gsmall[s], shard_shapes) for s in range(4)]
    full = {n: jnp.concatenate([per_chip[s][i] for s in range(4)], axis=-1) for i, n in enumerate(SMALL_SHARDED)}

    def landed(l, j, after):
        (zone,) = exchange_wait(gathers[l, j], after, name=f"gather_wait_l{l}_{j}")
        return jnp.where(is_me, slabs[l][j][None], zone)

    def first_operands(l, ga, order):
        return dict(
            n1=_row8(norm1_w[l], width=D_MODEL) + order, n2=_row8(norm2_w[l], width=D_MODEL),
            w_in_t=_w_in_to_padded(ga.reshape(IN_W, D_MODEL)),
            gcw=jnp.pad(full["gdn_conv_w"][l], ((0, 4), (0, 0))),
            galog=_row8(gdn_a_log[l], 8), gdtb=_row8(gdn_dt_bias[l], 8), gnw=_row8(gdn_norm_w[l]),
            scw=jnp.pad(jnp.concatenate([full["ssd_conv_w"][l], ssd_conv_b[l][None]], axis=0), ((0, 3), (0, 0))),
            sdtb=_row8(ssd_dt_bias[l]), salog=_row8(ssd_a_log[l]), sd=_row8(ssd_d[l]),
            snw=_row8(ssd_norm_w[l], width=D_MODEL), sink=_row8(swa_sinks[l]))

    def late_operands(l, after):
        gb = landed(l, 1, after)
        w = {}
        for n, r in BIG:
            parts = gb[:, BIG_OFF[n]:BIG_OFF[n] + r].reshape((4,) + given[n].shape[1:])
            w[n] = _join_chips(parts, BIG_AXIS[n] - 1)
        return dict(wpg=w["w_proj_gdn"], wps=w["w_proj_ssd"], wpw=w["w_proj_swa"], wout=w["w_out"],
                    wup=w["w_up"], wdown=w["w_down"])

    h = jnp.concatenate([jnp.zeros((PAD, D_MODEL), f32), full["meta_tokens"], x[0]], axis=0)
    layers, saved = [], []
    for l in range(depth):
        p = first_operands(0, ga0, started) if l == 0 else first_operands(l, landed(l, 0, h), 0.0)
        h, s = _layer_fwd(h, p, l, late=functools.partial(late_operands, l))
        layers.append(p)
        saved.append(s)
    loss8, dh, dfw8 = loss_head(h, _row8(final_norm_w, width=D_MODEL), loss_target[0])
    grads = {"final_norm_w": dfw8[0]}
    per_layer, grad_slabs, scatters = [None] * depth, {}, {}

    def send(l, j, slab):
        grad_slabs[l, j] = slab
        scatters[l, j], token = exchange_start([slab], (True,), loss8, name=f"scatter_start_l{l}_{j}")
        return token[0, 0]

    for l in reversed(range(depth)):
        dh, per_layer[l] = _layer_bwd(dh, layers[l], saved[l], l, functools.partial(send, l, 1),
                                      functools.partial(send, l, 0))
    grad_x = dh[HEAD_ROWS:][None]
    grads["meta_tokens"] = dh[PAD:HEAD_ROWS]
    lane = {"gdn_a_log": (8, 8), "gdn_dt_bias": (8, 8), "gdn_norm_w": (0, 128), "ssd_dt_bias": (0, 16),
            "ssd_a_log": (0, 16), "ssd_d": (0, 16), "swa_sinks": (0, 16)}
    for n in per_layer[0]:
        parts = [per_layer[l][n] for l in range(depth)]
        if n in lane:
            parts = [q[0, lane[n][0]:lane[n][0] + lane[n][1]] for q in parts]
        elif n in ("norm1_w", "norm2_w", "ssd_norm_w"):
            parts = [q[0] for q in parts]
        grads[n] = jnp.stack(parts)
    loss = lax.psum(loss8[0, 0], ("x", "y", "c"))

    gs = _pack_rows([grads[n] for n in SMALL_NAMES], f32)
    def chip_sum(l, j, after):
        (zone,) = exchange_wait(scatters[l, j], after, name=f"scatter_wait_l{l}_{j}")
        own = lax.dynamic_index_in_dim(grad_slabs[l, j], me, 0, keepdims=False)
        return reduce4(zone, own=own, me=me1, name=f"sum_chips_l{l}_{j}")

    early = [(l, j) for l in range(depth) for j in range(2) if (l, j) != (0, 0)]
    mine = {lj: chip_sum(*lj, dh) for lj in early}
    sibs = dict(zip(early, sibling_swap([mine[lj] for lj in early], name="swap_cores_early")))
    out = {}
    for n, r in BIG:
        shp = given[n].shape
        res = adamw(*[given[pre + n].reshape(depth * r, D_MODEL) for pre in ("", "m_", "v_")],
                    [(mine[l, 1], sibs[l, 1]) for l in range(depth)], BIG_OFF[n], name=f"adamw_{n}")
        out[n] = [a.reshape(shp) for a in res]
    mine[0, 0] = chip_sum(0, 0, res[1])
    (rs,) = chip_exchange([gs], (False,), after=mine[0, 0], name="gather_small_grads")
    ps_ = reduce4(rs, name="sum_chips_small")
    sibs[0, 0], ss = sibling_swap([mine[0, 0], ps_], name="swap_cores_last")
    res = adamw(*[jnp.swapaxes(given[pre + "w_in"], 1, 2).reshape(depth * W_IN_SHARD, D_MODEL)
                  for pre in ("", "m_", "v_")],
                [(mine[l, 0], sibs[l, 0]) for l in range(depth)], 0, name="adamw_w_in")
    out["w_in"] = [jnp.swapaxes(a.reshape(w_in_t.shape), 1, 2) for a in res]
    full_shapes = [grads[n].shape for n in SMALL_NAMES]
    mine_s, sib_s = _unpack_rows(ps_, full_shapes), _unpack_rows(ss, full_shapes)

    def local(parts):
        loc = []
        for n, a in zip(SMALL_NAMES, parts):
            if n in SMALL_SHARDED:
                sz = a.shape[-1] // 4
                a = lax.dynamic_slice_in_dim(a, me * sz, sz, axis=a.ndim - 1)
            loc.append(a)
        return _pack_rows(loc, f32)

    res = adamw(_pack_rows([given[n] for n in SMALL_NAMES], f32), _pack_rows([given["m_" + n] for n in SMALL_NAMES], f32),
                _pack_rows([given["v_" + n] for n in SMALL_NAMES], f32), [(local(mine_s), local(sib_s))], 0,
                name="adamw_small")
    local_shapes = [given[n].shape for n in SMALL_NAMES]
    unpacked = [_unpack_rows(a, local_shapes) for a in res]
    for i, n in enumerate(SMALL_NAMES):
        out[n] = [unpacked[j][i] for j in range(4)]

    return (loss, grad_x) + tuple(out[n][j] for j in range(4) for n in W_NAMES)
```

```python
import functools

import jax
import jax.numpy as jnp
from jax import lax
from jax.experimental import pallas as pl
from jax.experimental.pallas import tpu as pltpu

f32 = jnp.float32
bf16 = jnp.bfloat16
HI = lax.Precision.HIGHEST

D_MODEL = 1024
N_META = 16
PAD = 112
HEAD_ROWS = PAD + N_META
RMS_EPS = 1e-6
L2_EPS = 1e-6
D_FF = 4 * D_MODEL

GDN_HEADS = 8
GDN_D = 128
GDN_CHUNK = 64
SSD_HEADS = 16
SSD_P = 64
SSD_GROUPS = 4
SSD_HPG = 4
SSD_N = 128
SSD_CHUNK = 128
SWA_Q_HEADS = 16
SWA_KV_HEADS = 4
SWA_REP = 4
SWA_D = 64
SWA_W = 128

C_GATE = 0
C_SZ, C_SX, C_SB, C_SC = 3072, 4096, 5120, 5632
C_WQ, C_WK, C_WV = 6144, 7168, 7424
C_BA = 7680
C_SDT = 7808
C_MID_END = 8192
C_GQ, C_GK, C_GV, C_GG = 8192, 9216, 10240, 11264
IN_WP = 12288
IN_W = 11808

ADAM_LR, ADAM_B1, ADAM_B2, ADAM_EPS, ADAM_WD, ADAM_STEP = 0.001, 0.9, 0.999, 1e-08, 0.01, 10

VMEM_LIMIT = 56 * 1024 * 1024
BLOCK_BYTES = 3 << 19
MM_OPERAND_BYTES = 9 << 20
MM_RESIDENT_BYTES = 13 << 20

NN = (((1,), (0,)), ((), ()))
NT = (((1,), (1,)), ((), ()))
TN = (((0,), (0,)), ((), ()))


def _dot(a, b, dims=NN):
    return lax.dot_general(a.astype(bf16), b.astype(bf16), dims, preferred_element_type=f32)


def _dotx(a, b, dims=NN):
    return lax.dot_general(a, b, dims, preferred_element_type=f32, precision=lax.Precision.HIGH)


def _iota(shape, axis):
    return lax.broadcasted_iota(jnp.int32, shape, axis)


def _softplus(x):
    return jnp.maximum(x, 0.0) + jnp.log1p(jnp.exp(-jnp.abs(x)))


_sigmoid = jax.nn.sigmoid


def _silu(x):
    return x * _sigmoid(x)


def _params(sem):
    return pltpu.CompilerParams(dimension_semantics=sem, vmem_limit_bytes=VMEM_LIMIT)


@functools.partial(jax.custom_vjp, nondiff_argnums=(1,))
def _window(x_ext, off):
    n = x_ext.shape[0] - 8
    if off == 8:
        return x_ext[8:]
    return pltpu.roll(x_ext, 8 - off, 0)[8:]


def _window_fwd(x_ext, off):
    return _window(x_ext, off), None


def _window_bwd(off, _, g):
    n, w = g.shape
    g_ext = jnp.concatenate([jnp.zeros((8, w), g.dtype), g], axis=0)
    if off == 8:
        return (g_ext,)
    return (pltpu.roll(g_ext, n + off, 0),)


_window.defvjp(_window_fwd, _window_bwd)


def _conv4(x, halo, taps):
    x_ext = jnp.concatenate([halo, x], axis=0)
    y = taps[3] * x
    for j in range(3):
        y = y + taps[j] * _window(x_ext, 5 + j)
    return y


def _blockinv_impl(a):
    n = a.shape[0]
    ri, ci = _iota((n, n), 0), _iota((n, n), 1)
    t = (ri == ci).astype(f32) - jnp.where(((ri >> 1) == (ci >> 1)) & (ri > ci), a, 0.0)
    k = 1
    while (1 << k) < n:
        sel = ((ri >> (k + 1)) == (ci >> (k + 1))) & (((ri >> k) & 1) == 1) & (((ci >> k) & 1) == 0)
        o = jnp.where(sel, a, 0.0)
        t = t - _dotx(_dotx(t, o), t)
        k += 1
    return t


@jax.custom_vjp
def _blockinv(a):
    return _blockinv_impl(a)


def _blockinv_fwd(a):
    t = _blockinv_impl(a)
    return t, t


def _blockinv_bwd(t, dt):
    return (-_dotx(_dotx(t, dt, TN), t, NT),)


_blockinv.defvjp(_blockinv_fwd, _blockinv_bwd)


@jax.custom_vjp
def _blockinv_given(a, t):
    return t


_blockinv_given.defvjp(lambda a, t: (t, t), lambda t, dt: _blockinv_bwd(t, dt) + (jnp.zeros_like(t),))


def _scan_rows(x, reverse):
    n = x.shape[0]
    row = _iota(x.shape, 0)
    s = 1
    while s < n:
        if reverse:
            x = x + jnp.where(row < n - s, pltpu.roll(x, n - s, 0), 0.0)
        else:
            x = x + jnp.where(row >= s, pltpu.roll(x, s, 0), 0.0)
        s *= 2
    return x


@jax.custom_vjp
def _cumsum_rows(x):
    return _scan_rows(x, False)


_cumsum_rows.defvjp(lambda x: (_scan_rows(x, False), None), lambda _, g: (_scan_rows(g, True),))


def _gdn_act(xq, xk, xv, hq, hk, hv, tq, tk, tv):
    return _silu(_conv4(xq, hq, tq)), _silu(_conv4(xk, hk, tk)), _silu(_conv4(xv, hv, tv))


def _gdn_core(q, k, v, gate, mb, mg, mr, s, t_given, beta16, g16, gam16, gam16_t, nw):
    c = GDN_CHUNK
    q = q * lax.rsqrt(jnp.sum(q * q, axis=1, keepdims=True) + L2_EPS) * (GDN_D ** -0.5)
    k = k * lax.rsqrt(jnp.sum(k * k, axis=1, keepdims=True) + L2_EPS)

    pick = lambda x, m: jnp.sum(x * m, axis=1, keepdims=True)
    beta = pick(beta16, mb)
    g = jnp.broadcast_to(pick(g16, mg), (c, GDN_D))
    gam1 = pick(gam16, mg)
    gam = jnp.broadcast_to(gam1, (c, GDN_D))
    gam_j = jnp.broadcast_to(jnp.sum(gam16_t * mr, axis=0, keepdims=True), (c, c))

    ri, ci = _iota((c, c), 0), _iota((c, c), 1)
    incl = ci <= ri
    decay = jnp.where(incl, jnp.exp(jnp.where(incl, jnp.broadcast_to(gam1, (c, c)) - gam_j, 0.0)), 0.0)

    kb = k * beta
    a = jnp.where(ci < ri, _dot(kb, k, NT) * decay, 0.0)
    t = _blockinv(a) if t_given is None else _blockinv_given(a, t_given)
    egam = jnp.exp(gam)
    u = _dotx(t, v * beta)
    w = _dotx(t, kb * egam)
    attn = _dot(q, k, NT) * decay
    gl = jnp.sum(g, axis=0, keepdims=True)
    kt = k * jnp.exp(gl - gam)
    v_new = u - _dot(w, s)
    o = _dot(q * egam, s) + _dot(attn, v_new)
    s_out = s * jnp.exp(gl) + _dot(kt, v_new, TN)

    y = o * lax.rsqrt(jnp.mean(o * o, axis=1, keepdims=True) + RMS_EPS) * nw * _silu(gate)
    return y, s_out, t


def _gdn_chunk(q, k, v, gate, s, t_given, ba, alog, dtb, nw, *, masks, row0):
    valid = (row0 + _iota((GDN_CHUNK, 1), 0)) >= PAD
    beta16 = jnp.where(valid, _sigmoid(ba), 0.0)
    g16 = jnp.where(valid, -jnp.exp(alog) * _softplus(ba + dtb), 0.0)
    gam16 = _cumsum_rows(g16)
    core = jax.vmap(_gdn_core, in_axes=(0,) * 8 + (None if t_given is None else 0,) + (None,) * 5)
    y, s_out, t = core(q, k, v, gate, *masks, s, t_given, beta16, g16, gam16, gam16.T, nw)
    return (y, s_out, t) if t_given is None else (y, s_out)


def _gdn_specs(hb, nc, rev):
    w = hb * GDN_D
    cw = D_MODEL // w

    def cidx(c):
        return (nc - 1 - c) if rev else c

    def col(base):
        return pl.BlockSpec((GDN_CHUNK, w), lambda h, c: (cidx(c), base // w + h))

    def halo(base):
        return pl.BlockSpec((8, w), lambda h, c: (jnp.maximum(cidx(c) * (GDN_CHUNK // 8) - 1, 0), base // w + h))

    def taps(base):
        return pl.BlockSpec((8, w), lambda h, c: (0, base // w + h))

    ba = pl.BlockSpec((GDN_CHUNK, 128), lambda h, c: (cidx(c), C_BA // 128))
    row = pl.BlockSpec((8, 128), lambda h, c: (0, 0))
    y = pl.BlockSpec((GDN_CHUNK, w), lambda h, c: (cidx(c), h))
    st = pl.BlockSpec((1, hb, GDN_D, GDN_D), lambda h, c: (cidx(c), h, 0, 0))
    in_specs = [col(C_GQ), col(C_GK), col(C_GV), halo(C_GQ), halo(C_GK), halo(C_GV), col(C_GG), ba,
                taps(0), taps(1024), taps(2048), row, row, row]
    return in_specs, y, st, taps, row, col, ba


def _gdn_load(refs, first):
    xq, xk, xv, hq, hk, hv, gate, ba, tq, tk, tv, alog, dtb, nw = refs

    def halo(r):
        return jnp.where(first, 0.0, r[...])

    def taps(r):
        return tuple(r[j:j + 1, :] for j in range(4))

    act = (xq[...], xk[...], xv[...], halo(hq), halo(hk), halo(hv), taps(tq), taps(tk), taps(tv))
    return act, gate[...], (ba[...], alog[0:1, :], dtb[0:1, :], nw[0:1, :])


def _heads(a, hb):
    return jnp.stack([a[:, i * GDN_D:(i + 1) * GDN_D] for i in range(hb)])


def _wide(a):
    return jnp.concatenate([a[i] for i in range(a.shape[0])], axis=1)


def _head_masks(hblk, hb):
    head = hblk * hb + _iota((hb, 1, 128), 0)
    lane = _iota((hb, 1, 128), 2)
    rows = (_iota((hb, 128, 1), 1) == hblk * hb + _iota((hb, 128, 1), 0) + 8).astype(f32)
    return (lane == head).astype(f32), (lane == head + 8).astype(f32), rows


def gdn_fwd(u, conv_w8, alog8, dtb8, nw8, *, hb=8):
    t_rows = u.shape[0]
    nc = t_rows // GDN_CHUNK
    in_specs, y_spec, st_spec, *_ = _gdn_specs(hb, nc, False)

    def body(*refs):
        ins, (y_ref, st_ref, t_ref), (s_scr,) = refs[:14], refs[14:17], refs[17:]
        hblk, c = pl.program_id(0), pl.program_id(1)

        @pl.when(c == 0)
        def _():
            s_scr[...] = jnp.zeros_like(s_scr)

        act, gate, shared = _gdn_load(ins, c == 0)
        s = s_scr[...]
        st_ref[0] = s
        qa, ka, va = _gdn_act(*act)
        y, s_new, t = _gdn_chunk(_heads(qa, hb), _heads(ka, hb), _heads(va, hb), _heads(gate, hb), s, None, *shared,
                                 masks=_head_masks(hblk, hb), row0=c * GDN_CHUNK)
        y_ref[...] = _wide(y).astype(bf16)
        t_ref[0] = t
        s_scr[...] = s_new

    return pl.pallas_call(
        body, name="gdn_fwd", grid=(GDN_HEADS // hb, nc),
        in_specs=in_specs,
        out_specs=(y_spec, st_spec, pl.BlockSpec((1, hb, GDN_CHUNK, GDN_CHUNK), lambda h, c: (c, h, 0, 0))),
        out_shape=(jax.ShapeDtypeStruct((t_rows, D_MODEL), bf16),
                   jax.ShapeDtypeStruct((nc, GDN_HEADS, GDN_D, GDN_D), f32),
                   jax.ShapeDtypeStruct((nc, GDN_HEADS, GDN_CHUNK, GDN_CHUNK), f32)),
        scratch_shapes=[pltpu.VMEM((hb, GDN_D, GDN_D), f32)],
        compiler_params=_params(("arbitrary", "arbitrary")),
    )(u, u, u, u, u, u, u, u, conv_w8, conv_w8, conv_w8, alog8, dtb8, nw8)


def gdn_bwd(u, conv_w8, alog8, dtb8, nw8, states, tinv, dy, du):
    t_rows = u.shape[0]
    nc = t_rows // GDN_CHUNK
    hb = GDN_HEADS
    w = hb * GDN_D
    in_specs, y_spec, st_spec, taps, row, col, ba = _gdn_specs(hb, nc, True)
    nhb = GDN_HEADS // hb

    def body(*refs):
        ins, st_ref, t_ref, dy_ref = refs[:14], refs[14], refs[15], refs[16]
        du_ref, dba_ref, dtq_ref, dtk_ref, dtv_ref, dalog_ref, ddtb_ref, dnw_ref = refs[18:26]
        ds_scr, dh_scr = refs[26:]
        hblk, cc = pl.program_id(0), pl.program_id(1)
        c = nc - 1 - cc

        @pl.when(cc == 0)
        def _():
            ds_scr[...] = jnp.zeros_like(ds_scr)
            dh_scr[...] = jnp.zeros_like(dh_scr)
            dtq_ref[...] = jnp.zeros_like(dtq_ref)
            dtk_ref[...] = jnp.zeros_like(dtk_ref)
            dtv_ref[...] = jnp.zeros_like(dtv_ref)

        @pl.when((cc == 0) & (hblk == 0))
        def _():
            dalog_ref[...] = jnp.zeros_like(dalog_ref)
            ddtb_ref[...] = jnp.zeros_like(ddtb_ref)
            dnw_ref[...] = jnp.zeros_like(dnw_ref)

        act, gate, shared = _gdn_load(ins, c == 0)
        (qa, ka, va), vjp_act = jax.vjp(_gdn_act, *act)
        chunk = functools.partial(_gdn_chunk, masks=_head_masks(hblk, hb), row0=c * GDN_CHUNK)
        _, vjp_core = jax.vjp(chunk, _heads(qa, hb), _heads(ka, hb), _heads(va, hb), _heads(gate, hb), st_ref[0],
                              t_ref[0], *shared)
        dqa, dka, dva, dgate, ds, _, dba, dalog, ddtb, dnw = vjp_core(
            (_heads(dy_ref[...].astype(f32), hb), ds_scr[...]))
        ds_scr[...] = ds
        dxq, dxk, dxv, dhq, dhk, dhv, dtq, dtk, dtv = vjp_act((_wide(dqa), _wide(dka), _wide(dva)))
        zeros = jnp.zeros((GDN_CHUNK - 8, w), f32)
        for j, (dx, dh) in enumerate(((dxq, dhq), (dxk, dhk), (dxv, dhv))):
            du_ref[:, j * w:(j + 1) * w] = (dx + jnp.concatenate([zeros, dh_scr[j]], axis=0)).astype(bf16)
            dh_scr[j] = dh
        du_ref[:, 3 * w:4 * w] = _wide(dgate).astype(bf16)
        dba_ref[0] = dba
        for dt_ref, dtaps in ((dtq_ref, dtq), (dtk_ref, dtk), (dtv_ref, dtv)):
            for j in range(4):
                dt_ref[j:j + 1, :] += dtaps[j]
        dalog_ref[0:1, :] += dalog
        ddtb_ref[0:1, :] += ddtb
        dnw_ref[0:1, :] += dnw

    out_specs = (pl.BlockSpec((GDN_CHUNK, 4 * w), lambda h, c: (nc - 1 - c, C_GQ // (4 * w))),
                 pl.BlockSpec((1, GDN_CHUNK, 128), lambda h, c: (h, nc - 1 - c, 0)),
                 taps(0), taps(0), taps(0), row, row, row)
    out_shape = (jax.ShapeDtypeStruct(du.shape, du.dtype),
                 jax.ShapeDtypeStruct((nhb, t_rows, 128), f32),
                 jax.ShapeDtypeStruct((8, D_MODEL), f32), jax.ShapeDtypeStruct((8, D_MODEL), f32),
                 jax.ShapeDtypeStruct((8, D_MODEL), f32),
                 jax.ShapeDtypeStruct((8, 128), f32), jax.ShapeDtypeStruct((8, 128), f32), jax.ShapeDtypeStruct((8, 128), f32))
    return pl.pallas_call(
        body, name="gdn_bwd", grid=(nhb, nc),
        in_specs=in_specs + [st_spec, pl.BlockSpec((1, hb, GDN_CHUNK, GDN_CHUNK), lambda h, c: (nc - 1 - c, h, 0, 0)),
                             y_spec, ANY],
        out_specs=out_specs, out_shape=out_shape, input_output_aliases={17: 0},
        scratch_shapes=[pltpu.VMEM((hb, GDN_D, GDN_D), f32), pltpu.VMEM((3, 8, w), f32)],
        compiler_params=_params(("arbitrary", "arbitrary")),
    )(u, u, u, u, u, u, u, u, conv_w8, conv_w8, conv_w8, alog8, dtb8, nw8, states, tinv, dy, du)


def _ssd_act(xs_r, b_r, c_r, hx, hbm, hcm, tx, tb, tc, bx, bb, bc, *, row0):
    valid = (row0 + _iota((SSD_CHUNK, 1), 0)) >= PAD
    act = lambda x, h, t, b: jnp.where(valid, _silu(_conv4(x, h, t) + b), 0.0)
    return act(xs_r, hx, tx, bx), act(b_r, hbm, tb, bb), act(c_r, hcm, tc, bc)


def _ssd_core(xs, bm, cm, z, nw, lanes, rows, h, dtp16, adt16, acum16, acum16_t, dsk):
    n = SSD_CHUNK
    pick = lambda x, m: jnp.sum(x * m, axis=1, keepdims=True)
    lane_r = _iota((1, 256), 1) >> 6
    dtp = jnp.zeros((n, 256), f32)
    adt = jnp.zeros((n, 256), f32)
    acum = jnp.zeros((n, 256), f32)
    dlane = jnp.zeros((1, 256), f32)
    ccols = []
    for r in range(SSD_HPG):
        ccols.append(pick(acum16, lanes[r]))
        dtp = jnp.where(lane_r == r, pick(dtp16, lanes[r]), dtp)
        adt = jnp.where(lane_r == r, pick(adt16, lanes[r]), adt)
        acum = jnp.where(lane_r == r, ccols[r], acum)
        dlane = jnp.where(lane_r == r, pick(dsk, lanes[r]), dlane)

    ri, ci = _iota((n, n), 0), _iota((n, n), 1)
    incl = ci <= ri
    al = jnp.sum(adt, axis=0, keepdims=True)
    xdt = xs * dtp
    cb = _dot(cm, bm, NT)
    y = _dot(cm, h) * jnp.exp(acum) + dlane * xs
    for r in range(SSD_HPG):
        ai = jnp.broadcast_to(ccols[r], (n, n))
        aj = jnp.broadcast_to(jnp.sum(acum16_t * rows[r], axis=0, keepdims=True), (n, n))
        lm = jnp.where(incl, jnp.exp(jnp.where(incl, ai - aj, 0.0)), 0.0)
        y = y + _dot(cb * lm, jnp.where(lane_r == r, xdt, 0.0))
    h_out = h * jnp.exp(al) + _dot(bm, jnp.exp(al - acum) * xdt, TN)
    y = y * _silu(z)
    y = y * lax.rsqrt(jnp.mean(y * y, axis=1, keepdims=True) + RMS_EPS) * nw
    return y, h_out


def _ssd_chunk(xs, bm, cm, z, nw, h, dt, dtb, alog, dsk, *, row0):
    valid = (row0 + _iota((SSD_CHUNK, 1), 0)) >= PAD
    dtp16 = jnp.where(valid, _softplus(dt + dtb), 0.0)
    adt16 = -jnp.exp(alog) * dtp16
    acum16 = _cumsum_rows(adt16)
    lanes = tuple((_iota((SSD_GROUPS, 1, 128), 2) == _iota((SSD_GROUPS, 1, 128), 0) * SSD_HPG + r).astype(f32)
                  for r in range(SSD_HPG))
    rows = tuple((_iota((SSD_GROUPS, 128, 1), 1) == _iota((SSD_GROUPS, 128, 1), 0) * SSD_HPG + r).astype(f32)
                 for r in range(SSD_HPG))
    core = jax.vmap(_ssd_core, in_axes=(0,) * 8 + (None,) * 5)
    return core(xs, bm, cm, z, nw, lanes, rows, h, dtp16, adt16, acum16, acum16.T, dsk)


def _ssd_specs(nc, rev):
    n = SSD_CHUNK

    def cidx(c):
        return (nc - 1 - c) if rev else c

    def col(base, w):
        return pl.BlockSpec((n, w), lambda c: (cidx(c), base // w))

    def halo(base, w):
        return pl.BlockSpec((8, w), lambda c: (jnp.maximum(cidx(c) * (n // 8) - 1, 0), base // w))

    def taps(base, w):
        return pl.BlockSpec((8, w), lambda c: (0, base // w))

    row = pl.BlockSpec((8, 128), lambda c: (0, 0))
    in_specs = [col(C_SX, 1024), col(C_SB, 512), col(C_SC, 512), halo(C_SX, 1024), halo(C_SB, 512), halo(C_SC, 512),
                col(C_SZ, 1024), col(C_SDT, 128), taps(0, 1024), taps(1024, 512), taps(1536, 512), row, row, row,
                taps(0, 1024)]
    y = pl.BlockSpec((n, D_MODEL), lambda c: (cidx(c), 0))
    st = pl.BlockSpec((1, SSD_GROUPS, SSD_N, 256), lambda c: (cidx(c), 0, 0, 0))
    return in_specs, y, st, col, taps, row


def _ssd_load(refs, first):
    xs, bm, cm, hx, hbm, hcm, z, dt, tx, tb, tc, dtb, alog, dsk, nw = refs

    def halo(r):
        return jnp.where(first, 0.0, r[...])

    def taps(r):
        return tuple(r[j:j + 1, :] for j in range(4))

    act = (xs[...], bm[...], cm[...], halo(hx), halo(hbm), halo(hcm), taps(tx), taps(tb), taps(tc),
           tx[4:5, :], tb[4:5, :], tc[4:5, :])
    return act, (z[...], nw[0:1, :]), (dt[...], dtb[0:1, :], alog[0:1, :], dsk[0:1, :])


def _groups(a, w):
    return jnp.stack([a[:, i * w:(i + 1) * w] for i in range(SSD_GROUPS)])


def ssd_fwd(u, conv_w8, dtb8, alog8, d8, nw8):
    t_rows = u.shape[0]
    nc = t_rows // SSD_CHUNK
    in_specs, y_spec, st_spec, *_ = _ssd_specs(nc, False)

    def body(*refs):
        ins, (y_ref, st_ref), (h_scr,) = refs[:15], refs[15:17], refs[17:]
        c = pl.program_id(0)

        @pl.when(c == 0)
        def _():
            h_scr[...] = jnp.zeros_like(h_scr)

        act, (z, nw), shared = _ssd_load(ins, c == 0)
        h = h_scr[...]
        st_ref[0] = h
        xs, bm, cm = _ssd_act(*act, row0=c * SSD_CHUNK)
        y, h_new = _ssd_chunk(_groups(xs, 256), _groups(bm, 128), _groups(cm, 128), _groups(z, 256),
                              _groups(nw, 256), h, *shared, row0=c * SSD_CHUNK)
        y_ref[...] = _wide(y).astype(bf16)
        h_scr[...] = h_new

    return pl.pallas_call(
        body, name="ssd_fwd", grid=(nc,), in_specs=in_specs, out_specs=(y_spec, st_spec),
        out_shape=(jax.ShapeDtypeStruct((t_rows, D_MODEL), bf16),
                   jax.ShapeDtypeStruct((nc, SSD_GROUPS, SSD_N, 256), f32)),
        scratch_shapes=[pltpu.VMEM((SSD_GROUPS, SSD_N, 256), f32)],
        compiler_params=_params(("arbitrary",)),
    )(u, u, u, u, u, u, u, u, conv_w8, conv_w8, conv_w8, dtb8, alog8, d8, nw8)


def ssd_bwd(u, conv_w8, dtb8, alog8, d8, nw8, states, dy, du):
    t_rows = u.shape[0]
    nc = t_rows // SSD_CHUNK
    n = SSD_CHUNK
    in_specs, y_spec, st_spec, col, taps, row = _ssd_specs(nc, True)

    def body(*refs):
        ins, st_ref, dy_ref = refs[:15], refs[15], refs[16]
        du_ref, ddt_ref, dtx_ref, dtb_ref, dtc_ref, ddtb_ref, dalog_ref, ddsk_ref, dnw_ref = refs[18:27]
        dh_scr, hx_scr, hb_scr, hc_scr = refs[27:]
        cc = pl.program_id(0)
        c = nc - 1 - cc

        @pl.when(cc == 0)
        def _():
            for r in (dh_scr, hx_scr, hb_scr, hc_scr, dtx_ref, dtb_ref, dtc_ref, dnw_ref, ddtb_ref, dalog_ref, ddsk_ref):
                r[...] = jnp.zeros_like(r)

        act, (z, nw), shared = _ssd_load(ins, c == 0)
        (xs, bm, cm), vjp_act = jax.vjp(functools.partial(_ssd_act, row0=c * n), *act)
        _, vjp_core = jax.vjp(functools.partial(_ssd_chunk, row0=c * n), _groups(xs, 256), _groups(bm, 128),
                              _groups(cm, 128), _groups(z, 256), _groups(nw, 256), st_ref[0], *shared)
        dxa, dba, dca, dz, dnw, dh, ddt, ddtb, dalog, ddsk = vjp_core(
            (_groups(dy_ref[...].astype(f32), 256), dh_scr[...]))
        dh_scr[...] = dh
        dxs, dbm, dcm, dhx, dhb, dhc, dtx, dtb, dtc, dbx, dbb, dbc = vjp_act((_wide(dxa), _wide(dba), _wide(dca)))
        du_ref[:, 0:D_MODEL] = _wide(dz).astype(bf16)
        for dx, dhalo, scr, lo in ((dxs, dhx, hx_scr, C_SX), (dbm, dhb, hb_scr, C_SB), (dcm, dhc, hc_scr, C_SC)):
            zeros = jnp.zeros((n - 8, dx.shape[1]), f32)
            du_ref[:, lo - C_SZ:lo - C_SZ + dx.shape[1]] = (dx + jnp.concatenate([zeros, scr[...]], axis=0)).astype(bf16)
            scr[...] = dhalo
        ddt_ref[...] = ddt
        for ref, dtaps, dbias in ((dtx_ref, dtx, dbx), (dtb_ref, dtb, dbb), (dtc_ref, dtc, dbc)):
            for j in range(4):
                ref[j:j + 1, :] += dtaps[j]
            ref[4:5, :] += dbias
        ddtb_ref[0:1, :] += ddtb
        dalog_ref[0:1, :] += dalog
        ddsk_ref[0:1, :] += ddsk
        dnw_ref[0:1, :] += _wide(dnw)

    def out_col(w):
        return pl.BlockSpec((n, w), lambda c: (nc - 1 - c, 0))

    out_specs = (pl.BlockSpec((n, 3 * D_MODEL), lambda c: (nc - 1 - c, C_SZ // (3 * D_MODEL))), out_col(128),
                 taps(0, D_MODEL), taps(0, 512), taps(0, 512), row, row, row, taps(0, D_MODEL))
    out_shape = (jax.ShapeDtypeStruct(du.shape, du.dtype),
                 jax.ShapeDtypeStruct((t_rows, 128), f32),
                 jax.ShapeDtypeStruct((8, D_MODEL), f32), jax.ShapeDtypeStruct((8, 512), f32),
                 jax.ShapeDtypeStruct((8, 512), f32),
                 jax.ShapeDtypeStruct((8, 128), f32), jax.ShapeDtypeStruct((8, 128), f32),
                 jax.ShapeDtypeStruct((8, 128), f32), jax.ShapeDtypeStruct((8, D_MODEL), f32))
    return pl.pallas_call(
        body, name="ssd_bwd", grid=(nc,), in_specs=in_specs + [st_spec, y_spec, ANY],
        out_specs=out_specs, out_shape=out_shape, input_output_aliases={17: 0},
        scratch_shapes=[pltpu.VMEM((SSD_GROUPS, SSD_N, 256), f32), pltpu.VMEM((8, D_MODEL), f32),
                        pltpu.VMEM((8, 512), f32), pltpu.VMEM((8, 512), f32)],
        compiler_params=_params(("arbitrary",)),
    )(u, u, u, u, u, u, u, u, conv_w8, conv_w8, conv_w8, dtb8, alog8, d8, nw8, states, dy, du)


NEG = -1e30


def _swa_core(q, kc, kp, km, vc, vp, vm, sink, *, n):
    rows = SWA_REP * SWA_W
    ri, ci = _iota((rows, SWA_W), 0) & (SWA_W - 1), _iota((rows, SWA_W), 1)
    causal = ci <= ri
    m_band = (causal & ((n >= 1) | ((ci >= PAD) & (ri >= PAD)))) | ((ci > ri) & (n >= 2))
    m_meta = (n >= 1) & (ci >= PAD)
    q = q * (SWA_D ** -0.5)
    s = jnp.where(m_band, jnp.where(causal, _dot(q, kc, NT), _dot(q, kp, NT)), NEG)
    sm = jnp.where(m_meta, _dot(q, km, NT), NEG)
    mx = jnp.maximum(jnp.maximum(jnp.max(s, axis=1, keepdims=True), jnp.max(sm, axis=1, keepdims=True)), sink)
    mx = lax.stop_gradient(mx)
    e, em = jnp.exp(s - mx), jnp.exp(sm - mx)
    den = jnp.sum(e, axis=1, keepdims=True) + jnp.sum(em, axis=1, keepdims=True) + jnp.exp(sink - mx)
    return (_dot(jnp.where(causal, e, 0.0), vc) + _dot(jnp.where(causal, 0.0, e), vp) + _dot(em, vm)) / den


def _swa_block(q16, kc, kp, km, vc, vp, vm, sink16, *, n):
    rows = SWA_REP * SWA_W
    lane = _iota((1, 128), 1)
    rep = _iota((rows, 1), 0) >> 7
    cols = []
    for h in range(SWA_KV_HEADS):
        col = jnp.zeros((rows, 1), f32)
        for r in range(SWA_REP):
            s = jnp.sum(jnp.where(lane == h * SWA_REP + r, sink16, 0.0), axis=1, keepdims=True)
            col = jnp.where(rep == r, s, col)
        cols.append(col)
    o = jax.vmap(functools.partial(_swa_core, n=n))(q16.reshape(SWA_KV_HEADS, rows, SWA_D), kc, kp, km, vc, vp, vm,
                                                    jnp.concatenate([col[None] for col in cols], axis=0))
    return o.reshape(q16.shape)


def _swa_specs(nb, rev):
    def bidx(n):
        return (nb - 1 - n) if rev else n

    kvw = SWA_KV_HEADS * SWA_D
    q = pl.BlockSpec((SWA_W, D_MODEL), lambda n: (bidx(n), C_WQ // D_MODEL))

    def kv(base, blk):
        return pl.BlockSpec((SWA_W, kvw), lambda n: (blk(bidx(n)), base // kvw))

    cur, prev, meta = (lambda n: n), (lambda n: jnp.maximum(n - 1, 0)), (lambda n: 0)
    row = pl.BlockSpec((8, 128), lambda n: (0, 0))
    in_specs = [q] + [kv(C_WK, b) for b in (cur, prev, meta)] + [kv(C_WV, b) for b in (cur, prev, meta)] + [row]
    return in_specs, pl.BlockSpec((SWA_W, D_MODEL), lambda n: (bidx(n), 0)), row


def _swa_heads(a):
    return jnp.stack([a[:, i * SWA_D:(i + 1) * SWA_D] for i in range(a.shape[1] // SWA_D)])


def swa_fwd(u, sink8):
    t_rows = u.shape[0]
    nb = t_rows // SWA_W
    in_specs, o_spec, _ = _swa_specs(nb, False)

    def body(q_ref, kc, kp, km, vc, vp, vm, sink_ref, o_ref):
        o = _swa_block(*[_swa_heads(r[...]) for r in (q_ref, kc, kp, km, vc, vp, vm)], sink_ref[0:1, :],
                       n=pl.program_id(0))
        o_ref[...] = _wide(o).astype(bf16)

    return pl.pallas_call(
        body, name="swa_fwd", grid=(nb,), in_specs=in_specs, out_specs=o_spec,
        out_shape=jax.ShapeDtypeStruct((t_rows, D_MODEL), bf16),
        compiler_params=_params(("arbitrary",)),
    )(u, u, u, u, u, u, u, sink8)


def swa_bwd(u, sink8, do, du):
    t_rows = u.shape[0]
    nb = t_rows // SWA_W
    in_specs, o_spec, row = _swa_specs(nb, True)
    width = C_BA - C_WQ

    def body(q_ref, kc, kp, km, vc, vp, vm, sink_ref, do_ref, _, du_ref, dsink_ref,
             dkp_scr, dvp_scr, dkm_scr, dvm_scr):
        nn = pl.program_id(0)
        n = nb - 1 - nn

        @pl.when(nn == 0)
        def _():
            for r in (dkp_scr, dvp_scr, dkm_scr, dvm_scr, dsink_ref):
                r[...] = jnp.zeros_like(r)

        fn = functools.partial(_swa_block, n=n)
        _, vjp = jax.vjp(fn, *[_swa_heads(r[...]) for r in (q_ref, kc, kp, km, vc, vp, vm)], sink_ref[0:1, :])
        dq, dkc, dkp, dkm, dvc, dvp, dvm, dsink = vjp(_swa_heads(do_ref[...]))
        dkm_scr[...] += dkm
        dvm_scr[...] += dvm
        first = n == 0
        dk = dkc + dkp_scr[...] + jnp.where(first, dkm_scr[...], 0.0)
        dv = dvc + dvp_scr[...] + jnp.where(first, dvm_scr[...], 0.0)
        du_ref[:, 0:D_MODEL] = _wide(dq).astype(bf16)
        du_ref[:, C_WK - C_WQ:C_WV - C_WQ] = _wide(dk).astype(bf16)
        du_ref[:, C_WV - C_WQ:width] = _wide(dv).astype(bf16)
        dkp_scr[...] = dkp
        dvp_scr[...] = dvp
        dsink_ref[0:1, :] += dsink

    return pl.pallas_call(
        body, name="swa_bwd", grid=(nb,), in_specs=in_specs + [o_spec, ANY],
        out_specs=(pl.BlockSpec((SWA_W, width), lambda n: (nb - 1 - n, C_WQ // width)), row),
        out_shape=(jax.ShapeDtypeStruct(du.shape, du.dtype), jax.ShapeDtypeStruct((8, 128), f32)),
        input_output_aliases={9: 0},
        scratch_shapes=[pltpu.VMEM((SWA_KV_HEADS, SWA_W, SWA_D), f32)] * 4,
        compiler_params=_params(("arbitrary",)),
    )(u, u, u, u, u, u, u, sink8, do, du)


def _tile(dim, prefs):
    for p in prefs:
        if dim % p == 0:
            return p
    return dim


def _row_tile(rows, d):
    for p in range(min(rows, BLOCK_BYTES // (4 * d)) // 8 * 8, 0, -8):
        if rows % p == 0:
            return p
    return rows


def mm(a, b, *, out_dtype, name, resid=None, relu_grad_of=None, relu2_out=False, ta=False, tb=False):
    assert resid is None or relu_grad_of is None
    k, m = (a.shape if ta else a.shape[::-1])
    n = b.shape[0] if tb else b.shape[1]
    rhs_stays = k * 2 * 1024 > MM_OPERAND_BYTES
    assert not (ta and rhs_stays)
    if rhs_stays:
        tn = _tile(n, tuple(p for p in (512, 256, 128) if p * k * 2 <= MM_RESIDENT_BYTES))
        tm = _tile(m, tuple(p for p in (512, 384, 256, 128) if p * k * 2 <= MM_OPERAND_BYTES // 2))
        grid = (n // tn, m // tm)
        ij = lambda o, i: (i, o)
    elif k * n * b.dtype.itemsize <= MM_OPERAND_BYTES:
        tn = n
        tm = _tile(m, tuple(p for p in (1408, 1024, 512, 384, 256, 128)
                            if p * k * 2 <= MM_OPERAND_BYTES and p * n * 4 <= MM_OPERAND_BYTES * 2 // 3))
        grid = (m // tm, 1)
        ij = lambda o, i: (o, i)
    else:
        tm = _tile(m, tuple(p for p in (1408, 1024, 512, 384, 256, 128) if p * k * 2 <= MM_OPERAND_BYTES))
        tn = _tile(n, tuple(p for p in (1024, 512, 256, 128) if p * k * 2 <= MM_OPERAND_BYTES // 2))
        grid = (m // tm, n // tn)
        ij = lambda o, i: (o, i)

    extra = resid if resid is not None else relu_grad_of

    def body(*refs):
        a_ref, b_ref = refs[:2]
        if ta:
            at_scr = refs[-1]
            refs = refs[:-1]

            @pl.when(pl.program_id(1) == 0)
            def _():
                at_scr[...] = a_ref[...].T

            lhs = at_scr[...]
        else:
            lhs = a_ref[...]
        o = _dot(lhs, b_ref[...], NT if tb else NN)
        if resid is not None:
            o = o + refs[2][...]
        if relu_grad_of is not None:
            o = o * (2.0 * jnp.maximum(refs[2][...], 0.0))
        if relu2_out:
            refs[-2][...] = o.astype(out_dtype)
            r = jnp.maximum(o, 0.0)
            refs[-1][...] = (r * r).astype(bf16)
        else:
            refs[-1][...] = o.astype(out_dtype)

    in_specs = [pl.BlockSpec((k, tm), lambda o, i: (0, ij(o, i)[0])) if ta
                else pl.BlockSpec((tm, k), lambda o, i: (ij(o, i)[0], 0)),
                pl.BlockSpec((tn, k), lambda o, i: (ij(o, i)[1], 0)) if tb
                else pl.BlockSpec((k, tn), lambda o, i: (0, ij(o, i)[1]))]
    args = [a, b]
    if extra is not None:
        in_specs.append(pl.BlockSpec((tm, tn), ij))
        args.append(extra)
    out_blk = pl.BlockSpec((tm, tn), ij)
    out = jax.ShapeDtypeStruct((m, n), out_dtype)
    return pl.pallas_call(
        body, name=name, grid=grid, in_specs=in_specs,
        out_specs=(out_blk, out_blk) if relu2_out else out_blk,
        out_shape=(out, jax.ShapeDtypeStruct((m, n), bf16)) if relu2_out else out,
        scratch_shapes=[pltpu.VMEM((tm, k), bf16)] if ta else [],
        compiler_params=_params(("parallel", "arbitrary" if ta else "parallel")),
    )(*args)


def _rows(t_rows):
    return _tile(t_rows, (384, 256, 128))


def _rmsnorm(h, w):
    return h * lax.rsqrt(jnp.mean(h * h, axis=1, keepdims=True) + RMS_EPS) * w


def rmsnorm_fwd(h, w8, *, name):
    t_rows, d = h.shape
    tr = _rows(t_rows)

    def body(h_ref, w_ref, o_ref):
        o_ref[...] = _rmsnorm(h_ref[...], w_ref[0:1, :]).astype(bf16)

    blk = pl.BlockSpec((tr, d), lambda i: (i, 0))
    return pl.pallas_call(
        body, name=name, grid=(t_rows // tr,), in_specs=[blk, pl.BlockSpec((8, d), lambda i: (0, 0))], out_specs=blk,
        out_shape=jax.ShapeDtypeStruct((t_rows, d), bf16), compiler_params=_params(("arbitrary",)),
    )(h, w8)


def rmsnorm_bwd(h, w8, dhn, dres, *, name):
    t_rows, d = h.shape
    tr = _rows(t_rows)

    def body(h_ref, w_ref, dhn_ref, dres_ref, dh_ref, dw_ref):
        @pl.when(pl.program_id(0) == 0)
        def _():
            dw_ref[...] = jnp.zeros_like(dw_ref)

        _, vjp = jax.vjp(_rmsnorm, h_ref[...], w_ref[0:1, :])
        dh, dw = vjp(dhn_ref[...])
        dh_ref[...] = dh + dres_ref[...]
        dw_ref[0:1, :] += dw

    blk = pl.BlockSpec((tr, d), lambda i: (i, 0))
    wblk = pl.BlockSpec((8, d), lambda i: (0, 0))
    return pl.pallas_call(
        body, name=name, grid=(t_rows // tr,), in_specs=[blk, wblk, blk, blk], out_specs=(blk, wblk),
        out_shape=(jax.ShapeDtypeStruct((t_rows, d), f32), jax.ShapeDtypeStruct((8, d), f32)),
        compiler_params=_params(("arbitrary",)),
    )(h, w8, dhn, dres)


def _merge(pg, ps, pw, la, lb, lc):
    return _sigmoid(la) * pg + _sigmoid(lb) * ps + _sigmoid(lc) * pw


def _merge_specs(t_rows):
    tr = _rows(t_rows)
    blk = pl.BlockSpec((tr, D_MODEL), lambda i: (i, 0))
    gate = [pl.BlockSpec((tr, D_MODEL), functools.partial(lambda i, j: (i, j), j=C_GATE // D_MODEL + j)) for j in range(3)]
    return tr, blk, gate


def merge_fwd(pg, ps, pw, u):
    t_rows = pg.shape[0]
    tr, blk, gate = _merge_specs(t_rows)

    def body(pg_ref, ps_ref, pw_ref, la, lb, lc, o_ref):
        o_ref[...] = _merge(pg_ref[...], ps_ref[...], pw_ref[...], la[...], lb[...], lc[...]).astype(bf16)

    return pl.pallas_call(
        body, name="merge_fwd", grid=(t_rows // tr,), in_specs=[blk, blk, blk] + gate, out_specs=blk,
        out_shape=jax.ShapeDtypeStruct((t_rows, D_MODEL), bf16), compiler_params=_params(("arbitrary",)),
    )(pg, ps, pw, u, u, u)


def merge_bwd(pg, ps, pw, u, dmerged, du):
    t_rows = pg.shape[0]
    tr, blk, gate = _merge_specs(t_rows)

    def body(pg_ref, ps_ref, pw_ref, la, lb, lc, dm_ref, _, dpg_ref, dps_ref, dpw_ref, dl_ref):
        _, vjp = jax.vjp(_merge, pg_ref[...], ps_ref[...], pw_ref[...], la[...], lb[...], lc[...])
        dpg, dps, dpw, dla, dlb, dlc = vjp(dm_ref[...])
        dpg_ref[...] = dpg.astype(bf16)
        dps_ref[...] = dps.astype(bf16)
        dpw_ref[...] = dpw.astype(bf16)
        for j, dl in enumerate((dla, dlb, dlc)):
            dl_ref[:, j * D_MODEL:(j + 1) * D_MODEL] = dl.astype(bf16)

    act = jax.ShapeDtypeStruct((t_rows, D_MODEL), bf16)
    return pl.pallas_call(
        body, name="merge_bwd", grid=(t_rows // tr,), in_specs=[blk, blk, blk] + gate + [blk, ANY],
        out_specs=(blk, blk, blk, pl.BlockSpec((tr, 3 * D_MODEL), lambda i: (i, C_GATE // (3 * D_MODEL)))),
        out_shape=(act, act, act, jax.ShapeDtypeStruct(du.shape, du.dtype)),
        input_output_aliases={7: 3},
        compiler_params=_params(("arbitrary",)),
    )(pg, ps, pw, u, u, u, dmerged, du)


def relu2_fwd(a):
    t_rows, d = a.shape
    tr = _rows(t_rows)

    def body(a_ref, o_ref):
        r = jnp.maximum(a_ref[...], 0.0)
        o_ref[...] = (r * r).astype(bf16)

    blk = pl.BlockSpec((tr, d), lambda i: (i, 0))
    return pl.pallas_call(
        body, name="relu2_fwd", grid=(t_rows // tr,), in_specs=[blk], out_specs=blk,
        out_shape=jax.ShapeDtypeStruct((t_rows, d), bf16), compiler_params=_params(("arbitrary",)),
    )(a)


def relu2_bwd(a, dr):
    t_rows, d = a.shape
    tr = _rows(t_rows)

    def body(a_ref, dr_ref, o_ref):
        o_ref[...] = (dr_ref[...] * 2.0 * jnp.maximum(a_ref[...], 0.0)).astype(bf16)

    blk = pl.BlockSpec((tr, d), lambda i: (i, 0))
    return pl.pallas_call(
        body, name="relu2_bwd", grid=(t_rows // tr,), in_specs=[blk, blk], out_specs=blk,
        out_shape=jax.ShapeDtypeStruct((t_rows, d), bf16), compiler_params=_params(("arbitrary",)),
    )(a, dr)


def loss_head(h, w8, target):
    t_rows, d = h.shape
    tr = HEAD_ROWS

    def loss_fn(hb, w, tgt):
        err = _rmsnorm(hb, w) - tgt
        return 0.5 * jnp.sum(err * err) / d

    def body(h_ref, w_ref, t_ref, loss_ref, dh_ref, dw_ref):
        i = pl.program_id(0)

        @pl.when(i == 0)
        def _():
            loss_ref[...] = jnp.zeros_like(loss_ref)
            dw_ref[...] = jnp.zeros_like(dw_ref)
            dh_ref[...] = jnp.zeros_like(dh_ref)

        @pl.when(i > 0)
        def _():
            val, (dh, dw) = jax.value_and_grad(loss_fn, argnums=(0, 1))(h_ref[...], w_ref[0:1, :], t_ref[...])
            loss_ref[...] += val
            dh_ref[...] = dh
            dw_ref[0:1, :] += dw

    blk = pl.BlockSpec((tr, d), lambda i: (i, 0))
    wblk = pl.BlockSpec((8, d), lambda i: (0, 0))
    return pl.pallas_call(
        body, name="loss_head", grid=(t_rows // tr,),
        in_specs=[blk, wblk, pl.BlockSpec((tr, d), lambda i: (jnp.maximum(i - 1, 0), 0))],
        out_specs=(pl.BlockSpec((8, 128), lambda i: (0, 0)), blk, wblk),
        out_shape=(jax.ShapeDtypeStruct((8, 128), f32), jax.ShapeDtypeStruct((t_rows, d), f32),
                   jax.ShapeDtypeStruct((8, d), f32)),
        compiler_params=_params(("arbitrary",)),
    )(h, w8, target)


def adamw(w, m, v, partials, row_off, *, name):
    rows, d = w.shape
    layers = len(partials)
    per = rows // layers
    tr = _row_tile(per, d)
    assert row_off % tr == 0
    off, nblk = row_off // tr, per // tr
    c1 = 1.0 - ADAM_B1 ** ADAM_STEP
    c2 = 1.0 - ADAM_B2 ** ADAM_STEP

    def body(w_ref, m_ref, v_ref, *refs):
        p_refs, (g_ref, d_ref, mo_ref, vo_ref) = refs[:2 * layers], refs[2 * layers:]
        g = p_refs[0][...] + p_refs[1][...]
        for l in range(1, layers):
            g = jnp.where(pl.program_id(0) >= l * nblk, p_refs[2 * l][...] + p_refs[2 * l + 1][...], g)
        m_new = ADAM_B1 * m_ref[...] + (1.0 - ADAM_B1) * g
        v_new = ADAM_B2 * v_ref[...] + (1.0 - ADAM_B2) * (g * g)
        g_ref[...] = g
        d_ref[...] = -ADAM_LR * ((m_new / c1) / (jnp.sqrt(v_new / c2) + ADAM_EPS) + ADAM_WD * w_ref[...])
        mo_ref[...] = m_new
        vo_ref[...] = v_new

    blk = pl.BlockSpec((tr, d), lambda i: (i, 0))
    pblks = [pl.BlockSpec((tr, d), functools.partial(lambda i, l: (off + jnp.clip(i - l * nblk, 0, nblk - 1), 0), l=l))
             for l in range(layers) for _ in range(2)]
    out = jax.ShapeDtypeStruct((rows, d), f32)
    return pl.pallas_call(
        body, name=name, grid=(rows // tr,), in_specs=[blk, blk, blk] + pblks, out_specs=(blk,) * 4,
        out_shape=(out,) * 4, compiler_params=_params(("arbitrary",)),
    )(w, m, v, *[p for pair in partials for p in pair])


def reduce4(parts, *, name, own=None, me=None):
    _, rows, d = parts.shape
    tr = _row_tile(rows, d)

    def body(*refs):
        p_ref, o_ref = refs[0], refs[-1]
        acc = None
        for s in range(4):
            term = p_ref[s].astype(f32)
            if own is not None:
                term = jnp.where(refs[2][0] == s, refs[1][...].astype(f32), term)
            acc = term if acc is None else acc + term
        o_ref[...] = acc

    in_specs = [pl.BlockSpec((4, tr, d), lambda i: (0, i, 0))]
    args = [parts]
    if own is not None:
        in_specs += [pl.BlockSpec((tr, d), lambda i: (i, 0)), pl.BlockSpec(memory_space=pltpu.SMEM)]
        args += [own, me]
    return pl.pallas_call(
        body, name=name, grid=(rows // tr,), in_specs=in_specs,
        out_specs=pl.BlockSpec((tr, d), lambda i: (i, 0)), out_shape=jax.ShapeDtypeStruct((rows, d), f32),
        compiler_params=_params(("arbitrary",)),
    )(*args)


ANY = pl.BlockSpec(memory_space=pl.ANY)
MESH = pl.DeviceIdType.MESH
CHIP_FLIPS = ((0, 1), (1, 0), (1, 1))


def chip_exchange(bufs, scatter, *, name, after=None):
    nb = len(bufs)
    extra = [] if after is None else [after]

    def body(*refs):
        ins, outs = refs[:nb], refs[nb + len(extra):2 * nb + len(extra)]
        send_sems, recv_sems, local_sems = refs[2 * nb + len(extra):]
        x, y, c = lax.axis_index("x"), lax.axis_index("y"), lax.axis_index("c")
        me = 2 * x + y
        local = [pltpu.make_async_copy(ins[j].at[me] if scatter[j] else ins[j], outs[j].at[me], local_sems.at[j])
                 for j in range(nb)]
        for cp in local:
            cp.start()
        sends, recvs = [], []
        for k, (fx, fy) in enumerate(CHIP_FLIPS):
            px = 1 - x if fx else x
            py = 1 - y if fy else y
            chip = 2 * px + py
            for j in range(nb):
                src = ins[j].at[chip] if scatter[j] else ins[j]
                sems = dict(send_sem=send_sems.at[nb * k + j], recv_sem=recv_sems.at[nb * k + j],
                            device_id=(px, py, c), device_id_type=MESH)
                sends.append(pltpu.make_async_remote_copy(src_ref=src, dst_ref=outs[j].at[me], **sems))
                recvs.append(pltpu.make_async_remote_copy(src_ref=src, dst_ref=outs[j].at[chip], **sems))
        for cp in sends:
            cp.start()
        for cp in recvs:
            cp.wait_recv()
        for cp in sends:
            cp.wait_send()
        for cp in local:
            cp.wait()

    out_shape = tuple(jax.ShapeDtypeStruct(b.shape if s else (4,) + b.shape, b.dtype) for b, s in zip(bufs, scatter))
    return pl.pallas_call(
        body, name=name, in_specs=[ANY] * (nb + len(extra)), out_specs=(ANY,) * nb, out_shape=out_shape,
        scratch_shapes=[pltpu.SemaphoreType.DMA((3 * nb,)), pltpu.SemaphoreType.DMA((3 * nb,)),
                        pltpu.SemaphoreType.DMA((nb,))],
        compiler_params=pltpu.CompilerParams(has_side_effects=True),
    )(*bufs, *extra)


def gather_two_level(big, small, *, name):
    half = big.shape[1] // 2

    def body(big_ref, small_ref, obig_ref, osmall_ref, send_sems, recv_sems, local_sems):
        x, y, c = lax.axis_index("x"), lax.axis_index("y"), lax.axis_index("c")
        me = 2 * x + y
        mine = (slice(None), pl.ds(pl.multiple_of(c * half, half), half))
        theirs = (slice(None), pl.ds(pl.multiple_of((1 - c) * half, half), half))
        local = [pltpu.make_async_copy(big_ref, obig_ref.at[me], local_sems.at[0]),
                 pltpu.make_async_copy(small_ref, osmall_ref.at[me], local_sems.at[1])]
        for cp in local:
            cp.start()

        def copy(k, src, dst, to):
            return pltpu.make_async_remote_copy(src_ref=src, dst_ref=dst, send_sem=send_sems.at[k],
                                                recv_sem=recv_sems.at[k], device_id=to, device_id_type=MESH)

        sends, landed, passed, small_in = [], [], [], []
        for k, (fx, fy) in enumerate(CHIP_FLIPS):
            px = 1 - x if fx else x
            py = 1 - y if fy else y
            chip = 2 * px + py
            sends.append(copy(k, big_ref.at[mine], obig_ref.at[(me,) + mine], (px, py, c)))
            landed.append(copy(k, big_ref.at[mine], obig_ref.at[(chip,) + mine], (px, py, c)))
            sends.append(copy(3 + k, small_ref, osmall_ref.at[me], (px, py, c)))
            small_in.append(copy(3 + k, small_ref, osmall_ref.at[chip], (px, py, c)))
            passed.append((copy(6 + k, obig_ref.at[(chip,) + mine], obig_ref.at[(chip,) + mine], (x, y, 1 - c)),
                           copy(6 + k, obig_ref.at[(chip,) + theirs], obig_ref.at[(chip,) + theirs], (x, y, 1 - c))))
        for cp in sends:
            cp.start()
        for k in range(3):
            landed[k].wait_recv()
            passed[k][0].start()
        for k in range(3):
            passed[k][1].wait_recv()
            small_in[k].wait_recv()
        for cp in sends + [p[0] for p in passed]:
            cp.wait_send()
        for cp in local:
            cp.wait()

    return pl.pallas_call(
        body, name=name, in_specs=[ANY, ANY], out_specs=(ANY, ANY),
        out_shape=(jax.ShapeDtypeStruct((4,) + big.shape, big.dtype),
                   jax.ShapeDtypeStruct((4,) + small.shape, small.dtype)),
        scratch_shapes=[pltpu.SemaphoreType.DMA((9,)), pltpu.SemaphoreType.DMA((9,)), pltpu.SemaphoreType.DMA((2,))],
        compiler_params=pltpu.CompilerParams(has_side_effects=True),
    )(big, small)


def sibling_swap(bufs, *, name):
    nb = len(bufs)

    def body(*refs):
        ins, outs, (send_sems, recv_sems) = refs[:nb], refs[nb:2 * nb], refs[2 * nb:]
        peer = (lax.axis_index("x"), lax.axis_index("y"), 1 - lax.axis_index("c"))
        copies = [pltpu.make_async_remote_copy(src_ref=ins[j], dst_ref=outs[j], send_sem=send_sems.at[j],
                                               recv_sem=recv_sems.at[j], device_id=peer, device_id_type=MESH)
                  for j in range(nb)]
        for cp in copies:
            cp.start()
        for cp in copies:
            cp.wait_recv()
        for cp in copies:
            cp.wait_send()

    return pl.pallas_call(
        body, name=name, in_specs=[ANY] * nb, out_specs=(ANY,) * nb,
        out_shape=tuple(jax.ShapeDtypeStruct(b.shape, b.dtype) for b in bufs),
        scratch_shapes=[pltpu.SemaphoreType.DMA((nb,)), pltpu.SemaphoreType.DMA((nb,))],
        compiler_params=pltpu.CompilerParams(has_side_effects=True),
    )(*bufs)


HBM = pl.BlockSpec(memory_space=pltpu.HBM)
SEM = pl.BlockSpec(memory_space=pltpu.SEMAPHORE)
DATAFLOW = pltpu.SideEffectType.DATAFLOW_SIDE_EFFECTING


def _exchange_copies(srcs, lands, send_sems, recv_sems, scatter):
    x, y, c = lax.axis_index("x"), lax.axis_index("y"), lax.axis_index("c")
    me = 2 * x + y
    nb = len(srcs)
    pairs = []
    for k, (fx, fy) in enumerate(CHIP_FLIPS):
        px = 1 - x if fx else x
        py = 1 - y if fy else y
        chip = 2 * px + py
        for j in range(nb):
            src = srcs[j].at[chip] if scatter[j] else srcs[j]
            sems = dict(send_sem=send_sems.at[nb * k + j], recv_sem=recv_sems.at[nb * k + j],
                        device_id=(px, py, c), device_id_type=MESH)
            pairs.append((pltpu.make_async_remote_copy(src_ref=src, dst_ref=lands[j].at[me], **sems),
                          pltpu.make_async_remote_copy(src_ref=src, dst_ref=lands[j].at[chip], **sems)))
    return pairs


def exchange_start(bufs, scatter, after, *, name):
    nb = len(bufs)
    slabs = [b.shape[1:] if s else b.shape for b, s in zip(bufs, scatter)]
    lands = [lax.empty((4,) + shp, b.dtype) for b, shp in zip(bufs, slabs)]

    def body(*refs):
        srcs, zones = refs[:nb], refs[nb:2 * nb]
        send_sems, recv_sems = refs[2 * nb + 1:2 * nb + 3]
        token = refs[-1]
        for send, _ in _exchange_copies(srcs, zones, send_sems, recv_sems, scatter):
            send.start()
        token[...] = jnp.zeros_like(token)

    hbm = lambda a: pltpu.with_memory_space_constraint(a, pltpu.HBM)
    out = pl.pallas_call(
        body, name=name, in_specs=[HBM] * (2 * nb) + [ANY],
        out_specs=(SEM, SEM) + (HBM,) * (2 * nb) + (pl.BlockSpec(memory_space=pltpu.VMEM),),
        out_shape=(pltpu.SemaphoreType.DMA((3 * nb,)), pltpu.SemaphoreType.DMA((3 * nb,)))
        + tuple(pltpu.HBM(a.shape, a.dtype) for a in list(bufs) + lands) + (jax.ShapeDtypeStruct((8, 128), f32),),
        input_output_aliases={i: 2 + i for i in range(2 * nb)},
        compiler_params=pltpu.CompilerParams(has_side_effects=DATAFLOW),
    )(*[hbm(a) for a in list(bufs) + lands], after)
    return (out[:2], out[2:2 + nb], out[2 + nb:2 + 2 * nb], scatter), out[-1]


def exchange_wait(state, after, *, name):
    (send_sems, recv_sems), srcs, lands, scatter = state
    nb = len(srcs)

    def body(*refs):
        src_refs, zones = refs[:nb], refs[nb:2 * nb]
        s_sems, r_sems = refs[2 * nb:2 * nb + 2]
        for send, recv in _exchange_copies(src_refs, zones, s_sems, r_sems, scatter):
            send.wait_send()
            recv.wait_recv()

    out = pl.pallas_call(
        body, name=name, in_specs=[HBM] * (2 * nb) + [SEM, SEM, ANY], out_specs=(HBM,) * (2 * nb),
        out_shape=tuple(pltpu.HBM(a.shape, a.dtype) for a in list(srcs) + list(lands)),
        input_output_aliases={i: i for i in range(2 * nb)},
        compiler_params=pltpu.CompilerParams(has_side_effects=DATAFLOW),
    )(*srcs, *lands, send_sems, recv_sems, after)
    return out[nb:]


BIG = (
    ("w_proj_gdn", 256), ("w_proj_ssd", 256), ("w_proj_swa", 256), ("w_out", 256), ("w_up", 1024), ("w_down", 1024))
BIG_OFF = {}
_o = 0
for _n, _r in BIG:
    BIG_OFF[_n] = _o
    _o += _r
BIG_ROWS = _o
W_IN_SHARD = IN_W // 4

W_NAMES = ('meta_tokens', 'norm1_w', 'w_in', 'gdn_conv_w', 'gdn_a_log', 'gdn_dt_bias', 'gdn_norm_w', 'ssd_conv_w',
           'ssd_conv_b', 'ssd_dt_bias', 'ssd_a_log', 'ssd_d', 'ssd_norm_w', 'swa_sinks', 'w_proj_gdn', 'w_proj_ssd',
           'w_proj_swa', 'w_out', 'norm2_w', 'w_up', 'w_down', 'final_norm_w')
SMALL_NAMES = tuple(n for n in W_NAMES if n not in BIG_OFF and n != "w_in")
SMALL_SHARDED = ("meta_tokens", "gdn_conv_w", "ssd_conv_w")


def _pad_rows(a, rows):
    return jnp.pad(a, ((0, rows - a.shape[0]), (0, 0)))


def _pack_rows(parts, dtype):
    flat = jnp.concatenate([p.reshape(-1).astype(dtype) for p in parts])
    n = -(-flat.shape[0] // 8192) * 8192
    return jnp.pad(flat, (0, n - flat.shape[0])).reshape(-1, D_MODEL)


def _unpack_rows(packed, shapes):
    flat, out, o = packed.reshape(-1), [], 0
    for s in shapes:
        n = 1
        for d in s:
            n *= d
        out.append(flat[o:o + n].reshape(s))
        o += n
    return out


def _split_chips(full, axis):
    s = full.shape
    a = full.reshape(s[:axis] + (4, s[axis] // 4) + s[axis + 1:])
    return jnp.moveaxis(a, axis, 0)


def _join_chips(parts, axis):
    a = jnp.moveaxis(parts, 0, axis)
    s = a.shape
    return a.reshape(s[:axis] + (s[axis] * s[axis + 1],) + s[axis + 2:])


BIG_AXIS = {"w_in": 2, "w_proj_gdn": 1, "w_proj_ssd": 1, "w_proj_swa": 1, "w_out": 1, "w_up": 2, "w_down": 1}


def _w_in_to_padded(w):
    z = lambda n: jnp.zeros((n,) + w.shape[1:], w.dtype)
    return jnp.concatenate([w[8736:11808], w[4112:7184], w[7200:8736], w[4096:4112], z(112),
                            w[7184:7200], z(112 + C_MID_END - C_SDT - 128), w[0:4096]], axis=0)


def _w_in_from_padded(p):
    return jnp.concatenate([p[C_GQ:IN_WP], p[C_BA:C_BA + 16], p[C_SZ:C_WQ], p[C_SDT:C_SDT + 16], p[C_WQ:C_BA],
                            p[0:C_SZ]], axis=0)


def _row8(v, lane0=0, width=128):
    return jnp.pad(v[None, :], ((0, 7), (lane0, width - lane0 - v.shape[0])))


def _layer_fwd(h, p, l, late=None):
    tag = f"l{l}"
    hn = rmsnorm_fwd(h, p["n1"], name=f"norm1_fwd_{tag}")
    u = mm(hn, p["w_in_t"], tb=True, out_dtype=f32, name=f"mm_in_{tag}")
    yg, stg, tg = gdn_fwd(u, p["gcw"], p["galog"], p["gdtb"], p["gnw"])
    ys, sts = ssd_fwd(u, p["scw"], p["sdtb"], p["salog"], p["sd"], p["snw"])
    yw = swa_fwd(u, p["sink"])
    if late is not None:
        p.update(late(yw))
    pg = mm(yg, p["wpg"], out_dtype=f32, name=f"mm_pg_{tag}")
    ps = mm(ys, p["wps"], out_dtype=f32, name=f"mm_ps_{tag}")
    pw = mm(yw, p["wpw"], out_dtype=f32, name=f"mm_pw_{tag}")
    merged = merge_fwd(pg, ps, pw, u)
    h2 = mm(merged, p["wout"], out_dtype=f32, resid=h, name=f"mm_out_{tag}")
    hn2 = rmsnorm_fwd(h2, p["n2"], name=f"norm2_fwd_{tag}")
    a, r = mm(hn2, p["wup"], out_dtype=f32, relu2_out=True, name=f"mm_up_{tag}")
    h3 = mm(r, p["wdown"], out_dtype=f32, resid=h2, name=f"mm_down_{tag}")
    saved = dict(h=h, hn=hn, u=u, yg=yg, stg=stg, tg=tg, ys=ys, sts=sts, yw=yw, pg=pg, ps=ps, pw=pw,
                 merged=merged, h2=h2, hn2=hn2, a=a, r=r)
    return h3, saved


def _layer_bwd(dh3, p, s, l, send_big, send_w_in):
    tag = f"l{l}"
    g = {}

    def wgrad(act, d, name):
        return mm(act, d, ta=True, out_dtype=bf16, name=f"wg_{name}_{tag}")

    da = mm(dh3, p["wdown"], tb=True, out_dtype=bf16, relu_grad_of=s["a"], name=f"dg_down_{tag}")
    g["w_down"] = wgrad(s["r"], dh3, "down")
    dhn2 = mm(da, p["wup"], tb=True, out_dtype=f32, name=f"dg_up_{tag}")
    g["w_up"] = wgrad(s["hn2"], da, "up")
    dh2, g["norm2_w"] = rmsnorm_bwd(s["h2"], p["n2"], dhn2, dh3, name=f"norm2_bwd_{tag}")
    dmerged = mm(dh2, p["wout"], tb=True, out_dtype=f32, name=f"dg_out_{tag}")
    g["w_out"] = wgrad(s["merged"], dh2, "out")
    du = lax.empty((dh3.shape[0], IN_WP), bf16)
    dpg, dps, dpw, du = merge_bwd(s["pg"], s["ps"], s["pw"], s["u"], dmerged, du)
    dyg = mm(dpg, p["wpg"], tb=True, out_dtype=f32, name=f"dg_pg_{tag}")
    dys = mm(dps, p["wps"], tb=True, out_dtype=f32, name=f"dg_ps_{tag}")
    dyw = mm(dpw, p["wpw"], tb=True, out_dtype=f32, name=f"dg_pw_{tag}")
    g["w_proj_gdn"] = wgrad(s["yg"], dpg, "pg")
    g["w_proj_ssd"] = wgrad(s["ys"], dps, "ps")
    g["w_proj_swa"] = wgrad(s["yw"], dpw, "pw")
    sent = send_big(jnp.concatenate([_split_chips(g.pop(n), BIG_AXIS[n] - 1).reshape(4, r, D_MODEL)
                                     for n, r in BIG], axis=1))

    (du, dba, dtq, dtk, dtv, g["gdn_a_log"], g["gdn_dt_bias"], g["gdn_norm_w"]) = gdn_bwd(
        s["u"], p["gcw"] + sent, p["galog"], p["gdtb"], p["gnw"], s["stg"], s["tg"], dyg, du)
    g["gdn_conv_w"] = jnp.concatenate([dtq, dtk, dtv], axis=1)[:4]
    (du, ddt, dtx, dtb, dtc, g["ssd_dt_bias"], g["ssd_a_log"], g["ssd_d"], g["ssd_norm_w"]) = ssd_bwd(
        s["u"], p["scw"], p["sdtb"], p["salog"], p["sd"], p["snw"], s["sts"], dys, du)
    dconv = jnp.concatenate([dtx, dtb, dtc], axis=1)
    g["ssd_conv_w"], g["ssd_conv_b"] = dconv[:4], dconv[4]
    du, g["swa_sinks"] = swa_bwd(s["u"], p["sink"], dyw, du)
    mid = jnp.concatenate([dba[0].astype(bf16), ddt.astype(bf16),
                           jnp.zeros((du.shape[0], C_MID_END - C_SDT - 128), bf16)], axis=1)
    du = lax.dynamic_update_slice(du, mid, (0, C_BA))
    sent = send_w_in(_w_in_from_padded(wgrad(du, s["hn"], "in")).reshape(4, W_IN_SHARD, D_MODEL))
    dhn = mm(du, p["w_in_t"], out_dtype=f32, name=f"dg_in_{tag}")
    dh, g["norm1_w"] = rmsnorm_bwd(s["h"], p["n1"] + sent, dhn, dh2, name=f"norm1_bwd_{tag}")
    return dh, g


def kernel(x, meta_tokens, norm1_w, w_in, gdn_conv_w, gdn_a_log, gdn_dt_bias, gdn_norm_w, ssd_conv_w, ssd_conv_b, ssd_dt_bias, ssd_a_log, ssd_d, ssd_norm_w, swa_sinks, w_proj_gdn, w_proj_ssd, w_proj_swa, w_out, norm2_w, w_up, w_down, final_norm_w, loss_target, m_meta_tokens, m_norm1_w, m_w_in, m_gdn_conv_w, m_gdn_a_log, m_gdn_dt_bias, m_gdn_norm_w, m_ssd_conv_w, m_ssd_conv_b, m_ssd_dt_bias, m_ssd_a_log, m_ssd_d, m_ssd_norm_w, m_swa_sinks, m_w_proj_gdn, m_w_proj_ssd, m_w_proj_swa, m_w_out, m_norm2_w, m_w_up, m_w_down, m_final_norm_w, v_meta_tokens, v_norm1_w, v_w_in, v_gdn_conv_w, v_gdn_a_log, v_gdn_dt_bias, v_gdn_norm_w, v_ssd_conv_w, v_ssd_conv_b, v_ssd_dt_bias, v_ssd_a_log, v_ssd_d, v_ssd_norm_w, v_swa_sinks, v_w_proj_gdn, v_w_proj_ssd, v_w_proj_swa, v_w_out, v_norm2_w, v_w_up, v_w_down, v_final_norm_w):
    given = dict(locals())
    depth = norm1_w.shape[0]
    me = 2 * lax.axis_index("x") + lax.axis_index("y")

    me1 = jnp.reshape(me, (1,)).astype(jnp.int32)
    is_me = (jnp.arange(4, dtype=jnp.int32) == me)[:, None, None]
    w_in_t = jnp.swapaxes(w_in, 1, 2)

    def weight_slabs(l):
        return (w_in_t[l].astype(bf16),
                jnp.concatenate([given[n][l].reshape(-1, D_MODEL).astype(bf16) for n, _ in BIG]))

    slabs = [weight_slabs(l) for l in range(depth)]
    wsmall = _pack_rows([given[n] for n in SMALL_SHARDED], f32)
    ga0, gsmall = gather_two_level(slabs[0][0], wsmall, name="gather_first")
    gathers, started = {}, jnp.zeros((), f32)
    for l in range(depth):
        for j in range(2):
            if (l, j) != (0, 0):
                gathers[l, j], token = exchange_start([slabs[l][j]], (False,), gsmall, name=f"gather_start_l{l}_{j}")
                started = started + token[0, 0]
    shard_shapes = [given[n].shape for n in SMALL_SHARDED]
    per_chip = [_unpack_rows(gsmall[s], shard_shapes) for s in range(4)]
    full = {n: jnp.concatenate([per_chip[s][i] for s in range(4)], axis=-1) for i, n in enumerate(SMALL_SHARDED)}

    def landed(l, j, after):
        (zone,) = exchange_wait(gathers[l, j], after, name=f"gather_wait_l{l}_{j}")
        return lax.dynamic_update_slice(zone, slabs[l][j][None], (me, 0, 0))

    def first_operands(l, ga, order):
        return dict(
            n1=_row8(norm1_w[l], width=D_MODEL) + order, n2=_row8(norm2_w[l], width=D_MODEL),
            w_in_t=_w_in_to_padded(ga.reshape(IN_W, D_MODEL)),
            gcw=jnp.pad(full["gdn_conv_w"][l], ((0, 4), (0, 0))),
            galog=_row8(gdn_a_log[l], 8), gdtb=_row8(gdn_dt_bias[l], 8), gnw=_row8(gdn_norm_w[l]),
            scw=jnp.pad(jnp.concatenate([full["ssd_conv_w"][l], ssd_conv_b[l][None]], axis=0), ((0, 3), (0, 0))),
            sdtb=_row8(ssd_dt_bias[l]), salog=_row8(ssd_a_log[l]), sd=_row8(ssd_d[l]),
            snw=_row8(ssd_norm_w[l], width=D_MODEL), sink=_row8(swa_sinks[l]))

    def late_operands(l, after):
        gb = landed(l, 1, after)
        w = {}
        for n, r in BIG:
            parts = gb[:, BIG_OFF[n]:BIG_OFF[n] + r].reshape((4,) + given[n].shape[1:])
            w[n] = _join_chips(parts, BIG_AXIS[n] - 1)
        return dict(wpg=w["w_proj_gdn"], wps=w["w_proj_ssd"], wpw=w["w_proj_swa"], wout=w["w_out"],
                    wup=w["w_up"], wdown=w["w_down"])

    h = jnp.concatenate([jnp.zeros((PAD, D_MODEL), f32), full["meta_tokens"], x[0]], axis=0)
    layers, saved = [], []
    for l in range(depth):
        p = first_operands(0, ga0, started) if l == 0 else first_operands(l, landed(l, 0, h), 0.0)
        h, s = _layer_fwd(h, p, l, late=functools.partial(late_operands, l))
        layers.append(p)
        saved.append(s)
    loss8, dh, dfw8 = loss_head(h, _row8(final_norm_w, width=D_MODEL), loss_target[0])
    grads = {"final_norm_w": dfw8[0]}
    per_layer, grad_slabs, scatters = [None] * depth, {}, {}

    def send(l, j, slab):
        grad_slabs[l, j] = slab
        scatters[l, j], token = exchange_start([slab], (True,), loss8, name=f"scatter_start_l{l}_{j}")
        return token[0, 0]

    for l in reversed(range(depth)):
        dh, per_layer[l] = _layer_bwd(dh, layers[l], saved[l], l, functools.partial(send, l, 1),
                                      functools.partial(send, l, 0))
    grad_x = dh[HEAD_ROWS:][None]
    grads["meta_tokens"] = dh[PAD:HEAD_ROWS]
    lane = {"gdn_a_log": (8, 8), "gdn_dt_bias": (8, 8), "gdn_norm_w": (0, 128), "ssd_dt_bias": (0, 16),
            "ssd_a_log": (0, 16), "ssd_d": (0, 16), "swa_sinks": (0, 16)}
    for n in per_layer[0]:
        parts = [per_layer[l][n] for l in range(depth)]
        if n in lane:
            parts = [q[0, lane[n][0]:lane[n][0] + lane[n][1]] for q in parts]
        elif n in ("norm1_w", "norm2_w", "ssd_norm_w"):
            parts = [q[0] for q in parts]
        grads[n] = jnp.stack(parts)
    loss = lax.psum(loss8[0, 0], ("x", "y", "c"))

    gs = _pack_rows([grads[n] for n in SMALL_NAMES], f32)
    def chip_sum(l, j, after):
        (zone,) = exchange_wait(scatters[l, j], after, name=f"scatter_wait_l{l}_{j}")
        own = lax.dynamic_index_in_dim(grad_slabs[l, j], me, 0, keepdims=False)
        return reduce4(zone, own=own, me=me1, name=f"sum_chips_l{l}_{j}")

    early = [(l, j) for l in range(depth) for j in range(2) if (l, j) != (0, 0)]
    mine = {lj: chip_sum(*lj, dh) for lj in early}
    sibs = dict(zip(early, sibling_swap([mine[lj] for lj in early], name="swap_cores_early")))
    out = {}
    for n, r in BIG:
        shp = given[n].shape
        res = adamw(*[given[pre + n].reshape(depth * r, D_MODEL) for pre in ("", "m_", "v_")],
                    [(mine[l, 1], sibs[l, 1]) for l in range(depth)], BIG_OFF[n], name=f"adamw_{n}")
        out[n] = [a.reshape(shp) for a in res]
    mine[0, 0] = chip_sum(0, 0, res[1])
    (rs,) = chip_exchange([gs], (False,), after=mine[0, 0], name="gather_small_grads")
    ps_ = reduce4(rs, name="sum_chips_small")
    sibs[0, 0], ss = sibling_swap([mine[0, 0], ps_], name="swap_cores_last")
    res = adamw(*[jnp.swapaxes(given[pre + "w_in"], 1, 2).reshape(depth * W_IN_SHARD, D_MODEL)
                  for pre in ("", "m_", "v_")],
                [(mine[l, 0], sibs[l, 0]) for l in range(depth)], 0, name="adamw_w_in")
    out["w_in"] = [jnp.swapaxes(a.reshape(w_in_t.shape), 1, 2) for a in res]
    full_shapes = [grads[n].shape for n in SMALL_NAMES]
    mine_s, sib_s = _unpack_rows(ps_, full_shapes), _unpack_rows(ss, full_shapes)

    def local(parts):
        loc = []
        for n, a in zip(SMALL_NAMES, parts):
            if n in SMALL_SHARDED:
                sz = a.shape[-1] // 4
                a = lax.dynamic_slice_in_dim(a, me * sz, sz, axis=a.ndim - 1)
            loc.append(a)
        return _pack_rows(loc, f32)

    res = adamw(_pack_rows([given[n] for n in SMALL_NAMES], f32), _pack_rows([given["m_" + n] for n in SMALL_NAMES], f32),
                _pack_rows([given["v_" + n] for n in SMALL_NAMES], f32), [(local(mine_s), local(sib_s))], 0,
                name="adamw_small")
    local_shapes = [given[n].shape for n in SMALL_NAMES]
    unpacked = [_unpack_rows(a, local_shapes) for a in res]
    for i, n in enumerate(SMALL_NAMES):
        out[n] = [unpacked[j][i] for j in range(4)]

    return (loss, grad_x) + tuple(out[n][j] for j in range(4) for n in W_NAMES)
```

```python
import functools

import jax
import jax.numpy as jnp
from jax import lax
from jax.experimental import pallas as pl
from jax.experimental.pallas import tpu as pltpu

f32 = jnp.float32
bf16 = jnp.bfloat16

D_MODEL = 1024
N_META = 16
PAD = 112
HEAD_ROWS = PAD + N_META
RMS_EPS = 1e-6
L2_EPS = 1e-6
D_FF = 4 * D_MODEL

GDN_HEADS = 8
GDN_D = 128
GDN_CHUNK = 64
SSD_HEADS = 16
SSD_P = 64
SSD_GROUPS = 4
SSD_HPG = 4
SSD_N = 128
SSD_CHUNK = 128
SWA_Q_HEADS = 16
SWA_KV_HEADS = 4
SWA_REP = 4
SWA_D = 64
SWA_W = 128

C_GATE = 0
C_SZ, C_SX, C_SB, C_SC = 3072, 4096, 5120, 5632
C_WQ, C_WK, C_WV = 6144, 7168, 7424
C_BA = 7680
C_SDT = 7808
C_MID_END = 8192
C_GQ, C_GK, C_GV, C_GG = 8192, 9216, 10240, 11264
IN_WP = 12288
IN_W = 11808

ADAM_LR, ADAM_B1, ADAM_B2, ADAM_EPS, ADAM_WD, ADAM_STEP = 0.001, 0.9, 0.999, 1e-08, 0.01, 10

VMEM_LIMIT = 56 * 1024 * 1024
BLOCK_BYTES = 3 << 19
MM_OPERAND_BYTES = 9 << 20
MM_RESIDENT_BYTES = 13 << 20

NN = (((1,), (0,)), ((), ()))
NT = (((1,), (1,)), ((), ()))
TN = (((0,), (0,)), ((), ()))


def _dot(a, b, dims=NN):
    return lax.dot_general(a.astype(bf16), b.astype(bf16), dims, preferred_element_type=f32)


def _dotx(a, b, dims=NN):
    return lax.dot_general(a, b, dims, preferred_element_type=f32, precision=lax.Precision.HIGH)


def _iota(shape, axis):
    return lax.broadcasted_iota(jnp.int32, shape, axis)


def _softplus(x):
    return jnp.maximum(x, 0.0) + jnp.log1p(jnp.exp(-jnp.abs(x)))


_sigmoid = jax.nn.sigmoid


def _silu(x):
    return x * _sigmoid(x)


def _params(sem):
    return pltpu.CompilerParams(dimension_semantics=sem, vmem_limit_bytes=VMEM_LIMIT)


@functools.partial(jax.custom_vjp, nondiff_argnums=(1,))
def _window(x_ext, off):
    if off == 8:
        return x_ext[8:]
    return pltpu.roll(x_ext, 8 - off, 0)[8:]


def _window_fwd(x_ext, off):
    return _window(x_ext, off), None


def _window_bwd(off, _, g):
    n, w = g.shape
    g_ext = jnp.concatenate([jnp.zeros((8, w), g.dtype), g], axis=0)
    if off == 8:
        return (g_ext,)
    return (pltpu.roll(g_ext, n + off, 0),)


_window.defvjp(_window_fwd, _window_bwd)


def _conv4(x, halo, taps):
    x_ext = jnp.concatenate([halo, x], axis=0)
    y = taps[3] * x
    for j in range(3):
        y = y + taps[j] * _window(x_ext, 5 + j)
    return y


def _blockinv_impl(a):
    n = a.shape[0]
    ri, ci = _iota((n, n), 0), _iota((n, n), 1)
    t = (ri == ci).astype(f32) - jnp.where(((ri >> 1) == (ci >> 1)) & (ri > ci), a, 0.0)
    k = 1
    while (1 << k) < n:
        sel = ((ri >> (k + 1)) == (ci >> (k + 1))) & (((ri >> k) & 1) == 1) & (((ci >> k) & 1) == 0)
        o = jnp.where(sel, a, 0.0)
        t = t - _dotx(_dotx(t, o), t)
        k += 1
    return t


@jax.custom_vjp
def _blockinv(a):
    return _blockinv_impl(a)


def _blockinv_fwd(a):
    t = _blockinv_impl(a)
    return t, t


def _blockinv_bwd(t, dt):
    return (-_dotx(_dotx(t, dt, TN), t, NT),)


_blockinv.defvjp(_blockinv_fwd, _blockinv_bwd)


@jax.custom_vjp
def _blockinv_given(a, t):
    return t


_blockinv_given.defvjp(lambda a, t: (t, t), lambda t, dt: _blockinv_bwd(t, dt) + (jnp.zeros_like(t),))


def _scan_rows(x, reverse):
    n = x.shape[0]
    row = _iota(x.shape, 0)
    s = 1
    while s < n:
        if reverse:
            x = x + jnp.where(row < n - s, pltpu.roll(x, n - s, 0), 0.0)
        else:
            x = x + jnp.where(row >= s, pltpu.roll(x, s, 0), 0.0)
        s *= 2
    return x


@jax.custom_vjp
def _cumsum_rows(x):
    return _scan_rows(x, False)


_cumsum_rows.defvjp(lambda x: (_scan_rows(x, False), None), lambda _, g: (_scan_rows(g, True),))


def _gdn_act(xq, xk, xv, hq, hk, hv, tq, tk, tv):
    return _silu(_conv4(xq, hq, tq)), _silu(_conv4(xk, hk, tk)), _silu(_conv4(xv, hv, tv))


def _gdn_core(q, k, v, gate, mb, mg, mr, s, t_given, beta16, g16, gam16, gam16_t, nw):
    c = GDN_CHUNK
    q = q * lax.rsqrt(jnp.sum(q * q, axis=1, keepdims=True) + L2_EPS) * (GDN_D ** -0.5)
    k = k * lax.rsqrt(jnp.sum(k * k, axis=1, keepdims=True) + L2_EPS)

    pick = lambda x, m: jnp.sum(x * m, axis=1, keepdims=True)
    beta = pick(beta16, mb)
    g = jnp.broadcast_to(pick(g16, mg), (c, GDN_D))
    gam1 = pick(gam16, mg)
    gam = jnp.broadcast_to(gam1, (c, GDN_D))
    gam_j = jnp.broadcast_to(jnp.sum(gam16_t * mr, axis=0, keepdims=True), (c, c))

    ri, ci = _iota((c, c), 0), _iota((c, c), 1)
    incl = ci <= ri
    decay = jnp.where(incl, jnp.exp(jnp.where(incl, jnp.broadcast_to(gam1, (c, c)) - gam_j, 0.0)), 0.0)

    kb = k * beta
    a = jnp.where(ci < ri, _dot(kb, k, NT) * decay, 0.0)
    t = _blockinv(a) if t_given is None else _blockinv_given(a, t_given)
    egam = jnp.exp(gam)
    u = _dotx(t, v * beta)
    w = _dotx(t, kb * egam)
    attn = _dot(q, k, NT) * decay
    gl = jnp.sum(g, axis=0, keepdims=True)
    kt = k * jnp.exp(gl - gam)
    v_new = u - _dot(w, s)
    o = _dot(q * egam, s) + _dot(attn, v_new)
    s_out = s * jnp.exp(gl) + _dot(kt, v_new, TN)

    y = o * lax.rsqrt(jnp.mean(o * o, axis=1, keepdims=True) + RMS_EPS) * nw * _silu(gate)
    return y, s_out, t


def _gdn_chunk(q, k, v, gate, s, t_given, ba, alog, dtb, nw, *, masks, row0):
    valid = (row0 + _iota((GDN_CHUNK, 1), 0)) >= PAD
    beta16 = jnp.where(valid, _sigmoid(ba), 0.0)
    g16 = jnp.where(valid, -jnp.exp(alog) * _softplus(ba + dtb), 0.0)
    gam16 = _cumsum_rows(g16)
    core = jax.vmap(_gdn_core, in_axes=(0,) * 8 + (None if t_given is None else 0,) + (None,) * 5)
    y, s_out, t = core(q, k, v, gate, *masks, s, t_given, beta16, g16, gam16, gam16.T, nw)
    return (y, s_out, t) if t_given is None else (y, s_out)


def _gdn_specs(hb, nc, rev):
    w = hb * GDN_D

    def cidx(c):
        return (nc - 1 - c) if rev else c

    def col(base):
        return pl.BlockSpec((GDN_CHUNK, w), lambda h, c: (cidx(c), base // w + h))

    def halo(base):
        return pl.BlockSpec((8, w), lambda h, c: (jnp.maximum(cidx(c) * (GDN_CHUNK // 8) - 1, 0), base // w + h))

    def taps(base):
        return pl.BlockSpec((8, w), lambda h, c: (0, base // w + h))

    ba = pl.BlockSpec((GDN_CHUNK, 128), lambda h, c: (cidx(c), C_BA // 128))
    row = pl.BlockSpec((8, 128), lambda h, c: (0, 0))
    y = pl.BlockSpec((GDN_CHUNK, w), lambda h, c: (cidx(c), h))
    st = pl.BlockSpec((1, hb, GDN_D, GDN_D), lambda h, c: (cidx(c), h, 0, 0))
    in_specs = [col(C_GQ), col(C_GK), col(C_GV), halo(C_GQ), halo(C_GK), halo(C_GV), col(C_GG), ba,
                taps(0), taps(1024), taps(2048), row, row, row]
    return in_specs, y, st, taps, row, col, ba


def _gdn_load(refs, first):
    xq, xk, xv, hq, hk, hv, gate, ba, tq, tk, tv, alog, dtb, nw = refs

    def halo(r):
        return jnp.where(first, 0.0, r[...])

    def taps(r):
        return tuple(r[j:j + 1, :] for j in range(4))

    act = (xq[...], xk[...], xv[...], halo(hq), halo(hk), halo(hv), taps(tq), taps(tk), taps(tv))
    return act, gate[...], (ba[...], alog[0:1, :], dtb[0:1, :], nw[0:1, :])


def _heads(a, hb):
    return jnp.stack([a[:, i * GDN_D:(i + 1) * GDN_D] for i in range(hb)])


def _wide(a):
    return jnp.concatenate([a[i] for i in range(a.shape[0])], axis=1)


def _head_masks(hblk, hb):
    head = hblk * hb + _iota((hb, 1, 128), 0)
    lane = _iota((hb, 1, 128), 2)
    rows = (_iota((hb, 128, 1), 1) == hblk * hb + _iota((hb, 128, 1), 0) + 8).astype(f32)
    return (lane == head).astype(f32), (lane == head + 8).astype(f32), rows


def gdn_fwd(u, conv_w8, alog8, dtb8, nw8, *, hb=8):
    t_rows = u.shape[0]
    nc = t_rows // GDN_CHUNK
    in_specs, y_spec, st_spec, *_ = _gdn_specs(hb, nc, False)

    def body(*refs):
        ins, (y_ref, st_ref, t_ref), (s_scr,) = refs[:14], refs[14:17], refs[17:]
        hblk, c = pl.program_id(0), pl.program_id(1)

        @pl.when(c == 0)
        def _():
            s_scr[...] = jnp.zeros_like(s_scr)

        act, gate, shared = _gdn_load(ins, c == 0)
        s = s_scr[...]
        st_ref[0] = s
        qa, ka, va = _gdn_act(*act)
        y, s_new, t = _gdn_chunk(_heads(qa, hb), _heads(ka, hb), _heads(va, hb), _heads(gate, hb), s, None, *shared,
                                 masks=_head_masks(hblk, hb), row0=c * GDN_CHUNK)
        y_ref[...] = _wide(y).astype(bf16)
        t_ref[0] = t
        s_scr[...] = s_new

    return pl.pallas_call(
        body, name="gdn_fwd", grid=(GDN_HEADS // hb, nc),
        in_specs=in_specs,
        out_specs=(y_spec, st_spec, pl.BlockSpec((1, hb, GDN_CHUNK, GDN_CHUNK), lambda h, c: (c, h, 0, 0))),
        out_shape=(jax.ShapeDtypeStruct((t_rows, D_MODEL), bf16),
                   jax.ShapeDtypeStruct((nc, GDN_HEADS, GDN_D, GDN_D), f32),
                   jax.ShapeDtypeStruct((nc, GDN_HEADS, GDN_CHUNK, GDN_CHUNK), f32)),
        scratch_shapes=[pltpu.VMEM((hb, GDN_D, GDN_D), f32)],
        compiler_params=_params(("arbitrary", "arbitrary")),
    )(u, u, u, u, u, u, u, u, conv_w8, conv_w8, conv_w8, alog8, dtb8, nw8)


def gdn_bwd(u, conv_w8, alog8, dtb8, nw8, states, tinv, dy, du):
    t_rows = u.shape[0]
    nc = t_rows // GDN_CHUNK
    hb = GDN_HEADS
    w = hb * GDN_D
    in_specs, y_spec, st_spec, taps, row, col, ba = _gdn_specs(hb, nc, True)
    nhb = GDN_HEADS // hb

    def body(*refs):
        ins, st_ref, t_ref, dy_ref = refs[:14], refs[14], refs[15], refs[16]
        du_ref, dba_ref, dtq_ref, dtk_ref, dtv_ref, dalog_ref, ddtb_ref, dnw_ref = refs[18:26]
        ds_scr, dh_scr = refs[26:]
        hblk, cc = pl.program_id(0), pl.program_id(1)
        c = nc - 1 - cc

        @pl.when(cc == 0)
        def _():
            ds_scr[...] = jnp.zeros_like(ds_scr)
            dh_scr[...] = jnp.zeros_like(dh_scr)
            dtq_ref[...] = jnp.zeros_like(dtq_ref)
            dtk_ref[...] = jnp.zeros_like(dtk_ref)
            dtv_ref[...] = jnp.zeros_like(dtv_ref)

        @pl.when((cc == 0) & (hblk == 0))
        def _():
            dalog_ref[...] = jnp.zeros_like(dalog_ref)
            ddtb_ref[...] = jnp.zeros_like(ddtb_ref)
            dnw_ref[...] = jnp.zeros_like(dnw_ref)

        act, gate, shared = _gdn_load(ins, c == 0)
        (qa, ka, va), vjp_act = jax.vjp(_gdn_act, *act)
        chunk = functools.partial(_gdn_chunk, masks=_head_masks(hblk, hb), row0=c * GDN_CHUNK)
        _, vjp_core = jax.vjp(chunk, _heads(qa, hb), _heads(ka, hb), _heads(va, hb), _heads(gate, hb), st_ref[0],
                              t_ref[0], *shared)
        dqa, dka, dva, dgate, ds, _, dba, dalog, ddtb, dnw = vjp_core(
            (_heads(dy_ref[...].astype(f32), hb), ds_scr[...]))
        ds_scr[...] = ds
        dxq, dxk, dxv, dhq, dhk, dhv, dtq, dtk, dtv = vjp_act((_wide(dqa), _wide(dka), _wide(dva)))
        zeros = jnp.zeros((GDN_CHUNK - 8, w), f32)
        for j, (dx, dh) in enumerate(((dxq, dhq), (dxk, dhk), (dxv, dhv))):
            du_ref[:, j * w:(j + 1) * w] = (dx + jnp.concatenate([zeros, dh_scr[j]], axis=0)).astype(bf16)
            dh_scr[j] = dh
        du_ref[:, 3 * w:4 * w] = _wide(dgate).astype(bf16)
        dba_ref[0] = dba
        for dt_ref, dtaps in ((dtq_ref, dtq), (dtk_ref, dtk), (dtv_ref, dtv)):
            for j in range(4):
                dt_ref[j:j + 1, :] += dtaps[j]
        dalog_ref[0:1, :] += dalog
        ddtb_ref[0:1, :] += ddtb
        dnw_ref[0:1, :] += dnw

    out_specs = (pl.BlockSpec((GDN_CHUNK, 4 * w), lambda h, c: (nc - 1 - c, C_GQ // (4 * w))),
                 pl.BlockSpec((1, GDN_CHUNK, 128), lambda h, c: (h, nc - 1 - c, 0)),
                 taps(0), taps(0), taps(0), row, row, row)
    out_shape = (jax.ShapeDtypeStruct(du.shape, du.dtype),
                 jax.ShapeDtypeStruct((nhb, t_rows, 128), f32),
                 jax.ShapeDtypeStruct((8, D_MODEL), f32), jax.ShapeDtypeStruct((8, D_MODEL), f32),
                 jax.ShapeDtypeStruct((8, D_MODEL), f32),
                 jax.ShapeDtypeStruct((8, 128), f32), jax.ShapeDtypeStruct((8, 128), f32), jax.ShapeDtypeStruct((8, 128), f32))
    return pl.pallas_call(
        body, name="gdn_bwd", grid=(nhb, nc),
        in_specs=in_specs + [st_spec, pl.BlockSpec((1, hb, GDN_CHUNK, GDN_CHUNK), lambda h, c: (nc - 1 - c, h, 0, 0)),
                             y_spec, ANY],
        out_specs=out_specs, out_shape=out_shape, input_output_aliases={17: 0},
        scratch_shapes=[pltpu.VMEM((hb, GDN_D, GDN_D), f32), pltpu.VMEM((3, 8, w), f32)],
        compiler_params=_params(("arbitrary", "arbitrary")),
    )(u, u, u, u, u, u, u, u, conv_w8, conv_w8, conv_w8, alog8, dtb8, nw8, states, tinv, dy, du)


def _ssd_act(xs_r, b_r, c_r, hx, hbm, hcm, tx, tb, tc, bx, bb, bc, *, row0):
    valid = (row0 + _iota((SSD_CHUNK, 1), 0)) >= PAD
    act = lambda x, h, t, b: jnp.where(valid, _silu(_conv4(x, h, t) + b), 0.0)
    return act(xs_r, hx, tx, bx), act(b_r, hbm, tb, bb), act(c_r, hcm, tc, bc)


def _ssd_core(xs, bm, cm, z, nw, lanes, rows, h, dtp16, adt16, acum16, acum16_t, dsk):
    n = SSD_CHUNK
    pick = lambda x, m: jnp.sum(x * m, axis=1, keepdims=True)
    lane_r = _iota((1, 256), 1) >> 6
    dtp = jnp.zeros((n, 256), f32)
    adt = jnp.zeros((n, 256), f32)
    acum = jnp.zeros((n, 256), f32)
    dlane = jnp.zeros((1, 256), f32)
    ccols = []
    for r in range(SSD_HPG):
        ccols.append(pick(acum16, lanes[r]))
        dtp = jnp.where(lane_r == r, pick(dtp16, lanes[r]), dtp)
        adt = jnp.where(lane_r == r, pick(adt16, lanes[r]), adt)
        acum = jnp.where(lane_r == r, ccols[r], acum)
        dlane = jnp.where(lane_r == r, pick(dsk, lanes[r]), dlane)

    ri, ci = _iota((n, n), 0), _iota((n, n), 1)
    incl = ci <= ri
    al = jnp.sum(adt, axis=0, keepdims=True)
    xdt = xs * dtp
    cb = _dot(cm, bm, NT)
    y = _dot(cm, h) * jnp.exp(acum) + dlane * xs
    for r in range(SSD_HPG):
        ai = jnp.broadcast_to(ccols[r], (n, n))
        aj = jnp.broadcast_to(jnp.sum(acum16_t * rows[r], axis=0, keepdims=True), (n, n))
        lm = jnp.where(incl, jnp.exp(jnp.where(incl, ai - aj, 0.0)), 0.0)
        y = y + _dot(cb * lm, jnp.where(lane_r == r, xdt, 0.0))
    h_out = h * jnp.exp(al) + _dot(bm, jnp.exp(al - acum) * xdt, TN)
    y = y * _silu(z)
    y = y * lax.rsqrt(jnp.mean(y * y, axis=1, keepdims=True) + RMS_EPS) * nw
    return y, h_out


def _ssd_chunk(xs, bm, cm, z, nw, h, dt, dtb, alog, dsk, *, row0):
    valid = (row0 + _iota((SSD_CHUNK, 1), 0)) >= PAD
    dtp16 = jnp.where(valid, _softplus(dt + dtb), 0.0)
    adt16 = -jnp.exp(alog) * dtp16
    acum16 = _cumsum_rows(adt16)
    lanes = tuple((_iota((SSD_GROUPS, 1, 128), 2) == _iota((SSD_GROUPS, 1, 128), 0) * SSD_HPG + r).astype(f32)
                  for r in range(SSD_HPG))
    rows = tuple((_iota((SSD_GROUPS, 128, 1), 1) == _iota((SSD_GROUPS, 128, 1), 0) * SSD_HPG + r).astype(f32)
                 for r in range(SSD_HPG))
    core = jax.vmap(_ssd_core, in_axes=(0,) * 8 + (None,) * 5)
    return core(xs, bm, cm, z, nw, lanes, rows, h, dtp16, adt16, acum16, acum16.T, dsk)


def _ssd_specs(nc, rev):
    n = SSD_CHUNK

    def cidx(c):
        return (nc - 1 - c) if rev else c

    def col(base, w):
        return pl.BlockSpec((n, w), lambda c: (cidx(c), base // w))

    def halo(base, w):
        return pl.BlockSpec((8, w), lambda c: (jnp.maximum(cidx(c) * (n // 8) - 1, 0), base // w))

    def taps(base, w):
        return pl.BlockSpec((8, w), lambda c: (0, base // w))

    row = pl.BlockSpec((8, 128), lambda c: (0, 0))
    in_specs = [col(C_SX, 1024), col(C_SB, 512), col(C_SC, 512), halo(C_SX, 1024), halo(C_SB, 512), halo(C_SC, 512),
                col(C_SZ, 1024), col(C_SDT, 128), taps(0, 1024), taps(1024, 512), taps(1536, 512), row, row, row,
                taps(0, 1024)]
    y = pl.BlockSpec((n, D_MODEL), lambda c: (cidx(c), 0))
    st = pl.BlockSpec((1, SSD_GROUPS, SSD_N, 256), lambda c: (cidx(c), 0, 0, 0))
    return in_specs, y, st, col, taps, row


def _ssd_load(refs, first):
    xs, bm, cm, hx, hbm, hcm, z, dt, tx, tb, tc, dtb, alog, dsk, nw = refs

    def halo(r):
        return jnp.where(first, 0.0, r[...])

    def taps(r):
        return tuple(r[j:j + 1, :] for j in range(4))

    act = (xs[...], bm[...], cm[...], halo(hx), halo(hbm), halo(hcm), taps(tx), taps(tb), taps(tc),
           tx[4:5, :], tb[4:5, :], tc[4:5, :])
    return act, (z[...], nw[0:1, :]), (dt[...], dtb[0:1, :], alog[0:1, :], dsk[0:1, :])


def _groups(a, w):
    return jnp.stack([a[:, i * w:(i + 1) * w] for i in range(SSD_GROUPS)])


def ssd_fwd(u, conv_w8, dtb8, alog8, d8, nw8):
    t_rows = u.shape[0]
    nc = t_rows // SSD_CHUNK
    in_specs, y_spec, st_spec, *_ = _ssd_specs(nc, False)

    def body(*refs):
        ins, (y_ref, st_ref), (h_scr,) = refs[:15], refs[15:17], refs[17:]
        c = pl.program_id(0)

        @pl.when(c == 0)
        def _():
            h_scr[...] = jnp.zeros_like(h_scr)

        act, (z, nw), shared = _ssd_load(ins, c == 0)
        h = h_scr[...]
        st_ref[0] = h
        xs, bm, cm = _ssd_act(*act, row0=c * SSD_CHUNK)
        y, h_new = _ssd_chunk(_groups(xs, 256), _groups(bm, 128), _groups(cm, 128), _groups(z, 256),
                              _groups(nw, 256), h, *shared, row0=c * SSD_CHUNK)
        y_ref[...] = _wide(y).astype(bf16)
        h_scr[...] = h_new

    return pl.pallas_call(
        body, name="ssd_fwd", grid=(nc,), in_specs=in_specs, out_specs=(y_spec, st_spec),
        out_shape=(jax.ShapeDtypeStruct((t_rows, D_MODEL), bf16),
                   jax.ShapeDtypeStruct((nc, SSD_GROUPS, SSD_N, 256), f32)),
        scratch_shapes=[pltpu.VMEM((SSD_GROUPS, SSD_N, 256), f32)],
        compiler_params=_params(("arbitrary",)),
    )(u, u, u, u, u, u, u, u, conv_w8, conv_w8, conv_w8, dtb8, alog8, d8, nw8)


def ssd_bwd(u, conv_w8, dtb8, alog8, d8, nw8, states, dy, du):
    t_rows = u.shape[0]
    nc = t_rows // SSD_CHUNK
    n = SSD_CHUNK
    in_specs, y_spec, st_spec, col, taps, row = _ssd_specs(nc, True)

    def body(*refs):
        ins, st_ref, dy_ref = refs[:15], refs[15], refs[16]
        du_ref, ddt_ref, dtx_ref, dtb_ref, dtc_ref, ddtb_ref, dalog_ref, ddsk_ref, dnw_ref = refs[18:27]
        dh_scr, hx_scr, hb_scr, hc_scr = refs[27:]
        cc = pl.program_id(0)
        c = nc - 1 - cc

        @pl.when(cc == 0)
        def _():
            for r in (dh_scr, hx_scr, hb_scr, hc_scr, dtx_ref, dtb_ref, dtc_ref, dnw_ref, ddtb_ref, dalog_ref, ddsk_ref):
                r[...] = jnp.zeros_like(r)

        act, (z, nw), shared = _ssd_load(ins, c == 0)
        (xs, bm, cm), vjp_act = jax.vjp(functools.partial(_ssd_act, row0=c * n), *act)
        _, vjp_core = jax.vjp(functools.partial(_ssd_chunk, row0=c * n), _groups(xs, 256), _groups(bm, 128),
                              _groups(cm, 128), _groups(z, 256), _groups(nw, 256), st_ref[0], *shared)
        dxa, dba, dca, dz, dnw, dh, ddt, ddtb, dalog, ddsk = vjp_core(
            (_groups(dy_ref[...].astype(f32), 256), dh_scr[...]))
        dh_scr[...] = dh
        dxs, dbm, dcm, dhx, dhb, dhc, dtx, dtb, dtc, dbx, dbb, dbc = vjp_act((_wide(dxa), _wide(dba), _wide(dca)))
        du_ref[:, 0:D_MODEL] = _wide(dz).astype(bf16)
        for dx, dhalo, scr, lo in ((dxs, dhx, hx_scr, C_SX), (dbm, dhb, hb_scr, C_SB), (dcm, dhc, hc_scr, C_SC)):
            zeros = jnp.zeros((n - 8, dx.shape[1]), f32)
            du_ref[:, lo - C_SZ:lo - C_SZ + dx.shape[1]] = (dx + jnp.concatenate([zeros, scr[...]], axis=0)).astype(bf16)
            scr[...] = dhalo
        ddt_ref[...] = ddt
        for ref, dtaps, dbias in ((dtx_ref, dtx, dbx), (dtb_ref, dtb, dbb), (dtc_ref, dtc, dbc)):
            for j in range(4):
                ref[j:j + 1, :] += dtaps[j]
            ref[4:5, :] += dbias
        ddtb_ref[0:1, :] += ddtb
        dalog_ref[0:1, :] += dalog
        ddsk_ref[0:1, :] += ddsk
        dnw_ref[0:1, :] += _wide(dnw)

    def out_col(w):
        return pl.BlockSpec((n, w), lambda c: (nc - 1 - c, 0))

    out_specs = (pl.BlockSpec((n, 3 * D_MODEL), lambda c: (nc - 1 - c, C_SZ // (3 * D_MODEL))), out_col(128),
                 taps(0, D_MODEL), taps(0, 512), taps(0, 512), row, row, row, taps(0, D_MODEL))
    out_shape = (jax.ShapeDtypeStruct(du.shape, du.dtype),
                 jax.ShapeDtypeStruct((t_rows, 128), f32),
                 jax.ShapeDtypeStruct((8, D_MODEL), f32), jax.ShapeDtypeStruct((8, 512), f32),
                 jax.ShapeDtypeStruct((8, 512), f32),
                 jax.ShapeDtypeStruct((8, 128), f32), jax.ShapeDtypeStruct((8, 128), f32),
                 jax.ShapeDtypeStruct((8, 128), f32), jax.ShapeDtypeStruct((8, D_MODEL), f32))
    return pl.pallas_call(
        body, name="ssd_bwd", grid=(nc,), in_specs=in_specs + [st_spec, y_spec, ANY],
        out_specs=out_specs, out_shape=out_shape, input_output_aliases={17: 0},
        scratch_shapes=[pltpu.VMEM((SSD_GROUPS, SSD_N, 256), f32), pltpu.VMEM((8, D_MODEL), f32),
                        pltpu.VMEM((8, 512), f32), pltpu.VMEM((8, 512), f32)],
        compiler_params=_params(("arbitrary",)),
    )(u, u, u, u, u, u, u, u, conv_w8, conv_w8, conv_w8, dtb8, alog8, d8, nw8, states, dy, du)


NEG = -1e30


def _swa_core(q, kc, kp, km, vc, vp, vm, sink, *, n):
    rows = SWA_REP * SWA_W
    ri, ci = _iota((rows, SWA_W), 0) & (SWA_W - 1), _iota((rows, SWA_W), 1)
    causal = ci <= ri
    m_band = (causal & ((n >= 1) | ((ci >= PAD) & (ri >= PAD)))) | ((ci > ri) & (n >= 2))
    m_meta = (n >= 1) & (ci >= PAD)
    q = q * (SWA_D ** -0.5)
    s = jnp.where(m_band, jnp.where(causal, _dot(q, kc, NT), _dot(q, kp, NT)), NEG)
    sm = jnp.where(m_meta, _dot(q, km, NT), NEG)
    mx = jnp.maximum(jnp.maximum(jnp.max(s, axis=1, keepdims=True), jnp.max(sm, axis=1, keepdims=True)), sink)
    mx = lax.stop_gradient(mx)
    e, em = jnp.exp(s - mx), jnp.exp(sm - mx)
    den = jnp.sum(e, axis=1, keepdims=True) + jnp.sum(em, axis=1, keepdims=True) + jnp.exp(sink - mx)
    return (_dot(jnp.where(causal, e, 0.0), vc) + _dot(jnp.where(causal, 0.0, e), vp) + _dot(em, vm)) / den


def _swa_block(q16, kc, kp, km, vc, vp, vm, sink16, *, n):
    rows = SWA_REP * SWA_W
    lane = _iota((1, 128), 1)
    rep = _iota((rows, 1), 0) >> 7
    cols = []
    for h in range(SWA_KV_HEADS):
        col = jnp.zeros((rows, 1), f32)
        for r in range(SWA_REP):
            s = jnp.sum(jnp.where(lane == h * SWA_REP + r, sink16, 0.0), axis=1, keepdims=True)
            col = jnp.where(rep == r, s, col)
        cols.append(col)
    o = jax.vmap(functools.partial(_swa_core, n=n))(q16.reshape(SWA_KV_HEADS, rows, SWA_D), kc, kp, km, vc, vp, vm,
                                                    jnp.concatenate([col[None] for col in cols], axis=0))
    return o.reshape(q16.shape)


def _swa_specs(nb, rev):
    def bidx(n):
        return (nb - 1 - n) if rev else n

    kvw = SWA_KV_HEADS * SWA_D
    q = pl.BlockSpec((SWA_W, D_MODEL), lambda n: (bidx(n), C_WQ // D_MODEL))

    def kv(base, blk):
        return pl.BlockSpec((SWA_W, kvw), lambda n: (blk(bidx(n)), base // kvw))

    cur, prev, meta = (lambda n: n), (lambda n: jnp.maximum(n - 1, 0)), (lambda n: 0)
    row = pl.BlockSpec((8, 128), lambda n: (0, 0))
    in_specs = [q] + [kv(C_WK, b) for b in (cur, prev, meta)] + [kv(C_WV, b) for b in (cur, prev, meta)] + [row]
    return in_specs, pl.BlockSpec((SWA_W, D_MODEL), lambda n: (bidx(n), 0)), row


def _swa_heads(a):
    return jnp.stack([a[:, i * SWA_D:(i + 1) * SWA_D] for i in range(a.shape[1] // SWA_D)])


def swa_fwd(u, sink8):
    t_rows = u.shape[0]
    nb = t_rows // SWA_W
    in_specs, o_spec, _ = _swa_specs(nb, False)

    def body(q_ref, kc, kp, km, vc, vp, vm, sink_ref, o_ref):
        o = _swa_block(*[_swa_heads(r[...]) for r in (q_ref, kc, kp, km, vc, vp, vm)], sink_ref[0:1, :],
                       n=pl.program_id(0))
        o_ref[...] = _wide(o).astype(bf16)

    return pl.pallas_call(
        body, name="swa_fwd", grid=(nb,), in_specs=in_specs, out_specs=o_spec,
        out_shape=jax.ShapeDtypeStruct((t_rows, D_MODEL), bf16),
        compiler_params=_params(("arbitrary",)),
    )(u, u, u, u, u, u, u, sink8)


def swa_bwd(u, sink8, do, du):
    t_rows = u.shape[0]
    nb = t_rows // SWA_W
    in_specs, o_spec, row = _swa_specs(nb, True)
    width = C_BA - C_WQ

    def body(q_ref, kc, kp, km, vc, vp, vm, sink_ref, do_ref, _, du_ref, dsink_ref,
             dkp_scr, dvp_scr, dkm_scr, dvm_scr):
        nn = pl.program_id(0)
        n = nb - 1 - nn

        @pl.when(nn == 0)
        def _():
            for r in (dkp_scr, dvp_scr, dkm_scr, dvm_scr, dsink_ref):
                r[...] = jnp.zeros_like(r)

        fn = functools.partial(_swa_block, n=n)
        _, vjp = jax.vjp(fn, *[_swa_heads(r[...]) for r in (q_ref, kc, kp, km, vc, vp, vm)], sink_ref[0:1, :])
        dq, dkc, dkp, dkm, dvc, dvp, dvm, dsink = vjp(_swa_heads(do_ref[...]))
        dkm_scr[...] += dkm
        dvm_scr[...] += dvm
        first = n == 0
        dk = dkc + dkp_scr[...] + jnp.where(first, dkm_scr[...], 0.0)
        dv = dvc + dvp_scr[...] + jnp.where(first, dvm_scr[...], 0.0)
        du_ref[:, 0:D_MODEL] = _wide(dq).astype(bf16)
        du_ref[:, C_WK - C_WQ:C_WV - C_WQ] = _wide(dk).astype(bf16)
        du_ref[:, C_WV - C_WQ:width] = _wide(dv).astype(bf16)
        dkp_scr[...] = dkp
        dvp_scr[...] = dvp
        dsink_ref[0:1, :] += dsink

    return pl.pallas_call(
        body, name="swa_bwd", grid=(nb,), in_specs=in_specs + [o_spec, ANY],
        out_specs=(pl.BlockSpec((SWA_W, width), lambda n: (nb - 1 - n, C_WQ // width)), row),
        out_shape=(jax.ShapeDtypeStruct(du.shape, du.dtype), jax.ShapeDtypeStruct((8, 128), f32)),
        input_output_aliases={9: 0},
        scratch_shapes=[pltpu.VMEM((SWA_KV_HEADS, SWA_W, SWA_D), f32)] * 4,
        compiler_params=_params(("arbitrary",)),
    )(u, u, u, u, u, u, u, sink8, do, du)


def _tile(dim, prefs):
    for p in prefs:
        if dim % p == 0:
            return p
    return dim


def _row_tile(rows, d):
    for p in range(min(rows, BLOCK_BYTES // (4 * d)) // 8 * 8, 0, -8):
        if rows % p == 0:
            return p
    return rows


def mm(a, b, *, out_dtype, name, resid=None, relu_grad_of=None, relu2_out=False, ta=False, tb=False, norm_w8=None):
    assert resid is None or relu_grad_of is None
    k, m = (a.shape if ta else a.shape[::-1])
    n = b.shape[0] if tb else b.shape[1]
    rhs_stays = k * 2 * 1024 > MM_OPERAND_BYTES
    if rhs_stays:
        tn = _tile(n, tuple(p for p in (512, 256, 128) if p * k * 2 <= MM_RESIDENT_BYTES))
        tm = _tile(m, tuple(p for p in (512, 384, 256, 128) if p * k * 2 <= MM_OPERAND_BYTES // 2))
        grid = (n // tn, m // tm)
        ij = lambda o, i: (i, o)
    elif k * n * b.dtype.itemsize <= MM_OPERAND_BYTES:
        tn = n
        tm = _tile(m, tuple(p for p in (1408, 1024, 512, 384, 256, 128)
                            if p * k * 2 <= MM_OPERAND_BYTES and p * n * 4 <= MM_OPERAND_BYTES * 2 // 3))
        grid = (m // tm, 1)
        ij = lambda o, i: (o, i)
    else:
        tm = _tile(m, tuple(p for p in (1408, 1024, 512, 384, 256, 128) if p * k * 2 <= MM_OPERAND_BYTES))
        tn = _tile(n, tuple(p for p in (1024, 512, 256, 128) if p * k * 2 <= MM_OPERAND_BYTES // 2))
        grid = (m // tm, n // tn)
        ij = lambda o, i: (o, i)

    extra = resid if resid is not None else relu_grad_of
    staged = ta or norm_w8 is not None
    assert not (ta and norm_w8 is not None) and not (staged and rhs_stays)
    n_in = 2 + (extra is not None) + (norm_w8 is not None)
    n_out = 1 + relu2_out + (norm_w8 is not None)

    def body(*refs):
        ins, outs, scr = refs[:n_in], refs[n_in:n_in + n_out], refs[n_in + n_out:]
        a_ref, b_ref = ins[:2]
        if staged:
            @pl.when(pl.program_id(1) == 0)
            def _():
                if ta:
                    scr[0][...] = a_ref[...].T
                else:
                    hn = _rmsnorm(a_ref[...], ins[-1][0:1, :]).astype(bf16)
                    scr[0][...] = hn
                    outs[-1][...] = hn

            lhs = scr[0][...]
        else:
            lhs = a_ref[...]
        o = _dot(lhs, b_ref[...], NT if tb else NN)
        if resid is not None:
            o = o + ins[2][...]
        if relu_grad_of is not None:
            o = o * (2.0 * jnp.maximum(ins[2][...], 0.0))
        outs[0][...] = o.astype(out_dtype)
        if relu2_out:
            r = jnp.maximum(o, 0.0)
            outs[1][...] = (r * r).astype(bf16)

    in_specs = [pl.BlockSpec((k, tm), lambda o, i: (0, ij(o, i)[0])) if ta
                else pl.BlockSpec((tm, k), lambda o, i: (ij(o, i)[0], 0)),
                pl.BlockSpec((tn, k), lambda o, i: (ij(o, i)[1], 0)) if tb
                else pl.BlockSpec((k, tn), lambda o, i: (0, ij(o, i)[1]))]
    args = [a, b]
    if extra is not None:
        in_specs.append(pl.BlockSpec((tm, tn), ij))
        args.append(extra)
    out_blk = pl.BlockSpec((tm, tn), ij)
    out_specs = [out_blk] * (1 + relu2_out)
    out_shape = [jax.ShapeDtypeStruct((m, n), out_dtype)] + [jax.ShapeDtypeStruct((m, n), bf16)] * relu2_out
    if norm_w8 is not None:
        in_specs.append(pl.BlockSpec((8, k), lambda o, i: (0, 0)))
        args.append(norm_w8)
        out_specs.append(pl.BlockSpec((tm, k), lambda o, i: (ij(o, i)[0], 0)))
        out_shape.append(jax.ShapeDtypeStruct((m, k), bf16))
    res = pl.pallas_call(
        body, name=name, grid=grid, in_specs=in_specs, out_specs=tuple(out_specs), out_shape=tuple(out_shape),
        scratch_shapes=[pltpu.VMEM((tm, k), bf16)] if staged else [],
        compiler_params=_params(("parallel", "arbitrary" if staged else "parallel")),
    )(*args)
    return res[0] if len(res) == 1 else res


def _rows(t_rows):
    return _tile(t_rows, (384, 256, 128))


def _rmsnorm(h, w):
    return h * lax.rsqrt(jnp.mean(h * h, axis=1, keepdims=True) + RMS_EPS) * w


def rmsnorm_bwd(h, w8, dhn, dres, *, name):
    t_rows, d = h.shape
    tr = _rows(t_rows)

    def body(h_ref, w_ref, dhn_ref, dres_ref, dh_ref, dw_ref):
        @pl.when(pl.program_id(0) == 0)
        def _():
            dw_ref[...] = jnp.zeros_like(dw_ref)

        _, vjp = jax.vjp(_rmsnorm, h_ref[...], w_ref[0:1, :])
        dh, dw = vjp(dhn_ref[...])
        dh_ref[...] = dh + dres_ref[...]
        dw_ref[0:1, :] += dw

    blk = pl.BlockSpec((tr, d), lambda i: (i, 0))
    wblk = pl.BlockSpec((8, d), lambda i: (0, 0))
    return pl.pallas_call(
        body, name=name, grid=(t_rows // tr,), in_specs=[blk, wblk, blk, blk], out_specs=(blk, wblk),
        out_shape=(jax.ShapeDtypeStruct((t_rows, d), f32), jax.ShapeDtypeStruct((8, d), f32)),
        compiler_params=_params(("arbitrary",)),
    )(h, w8, dhn, dres)


def _merge(pg, ps, pw, la, lb, lc):
    return _sigmoid(la) * pg + _sigmoid(lb) * ps + _sigmoid(lc) * pw


def _merge_specs(t_rows):
    tr = _rows(t_rows)
    blk = pl.BlockSpec((tr, D_MODEL), lambda i: (i, 0))
    gate = [pl.BlockSpec((tr, D_MODEL), functools.partial(lambda i, j: (i, j), j=C_GATE // D_MODEL + j)) for j in range(3)]
    return tr, blk, gate


def merge_fwd(pg, ps, pw, u):
    t_rows = pg.shape[0]
    tr, blk, gate = _merge_specs(t_rows)

    def body(pg_ref, ps_ref, pw_ref, la, lb, lc, o_ref):
        o_ref[...] = _merge(pg_ref[...], ps_ref[...], pw_ref[...], la[...], lb[...], lc[...]).astype(bf16)

    return pl.pallas_call(
        body, name="merge_fwd", grid=(t_rows // tr,), in_specs=[blk, blk, blk] + gate, out_specs=blk,
        out_shape=jax.ShapeDtypeStruct((t_rows, D_MODEL), bf16), compiler_params=_params(("arbitrary",)),
    )(pg, ps, pw, u, u, u)


def merge_bwd(pg, ps, pw, u, dmerged, du):
    t_rows = pg.shape[0]
    tr, blk, gate = _merge_specs(t_rows)

    def body(pg_ref, ps_ref, pw_ref, la, lb, lc, dm_ref, _, dpg_ref, dps_ref, dpw_ref, dl_ref):
        _, vjp = jax.vjp(_merge, pg_ref[...], ps_ref[...], pw_ref[...], la[...], lb[...], lc[...])
        dpg, dps, dpw, dla, dlb, dlc = vjp(dm_ref[...])
        dpg_ref[...] = dpg.astype(bf16)
        dps_ref[...] = dps.astype(bf16)
        dpw_ref[...] = dpw.astype(bf16)
        for j, dl in enumerate((dla, dlb, dlc)):
            dl_ref[:, j * D_MODEL:(j + 1) * D_MODEL] = dl.astype(bf16)

    act = jax.ShapeDtypeStruct((t_rows, D_MODEL), bf16)
    return pl.pallas_call(
        body, name="merge_bwd", grid=(t_rows // tr,), in_specs=[blk, blk, blk] + gate + [blk, ANY],
        out_specs=(blk, blk, blk, pl.BlockSpec((tr, 3 * D_MODEL), lambda i: (i, C_GATE // (3 * D_MODEL)))),
        out_shape=(act, act, act, jax.ShapeDtypeStruct(du.shape, du.dtype)),
        input_output_aliases={7: 3},
        compiler_params=_params(("arbitrary",)),
    )(pg, ps, pw, u, u, u, dmerged, du)


def loss_head(h, w8, target):
    t_rows, d = h.shape
    tr = HEAD_ROWS

    def loss_fn(hb, w, tgt):
        err = _rmsnorm(hb, w) - tgt
        return 0.5 * jnp.sum(err * err) / d

    def body(h_ref, w_ref, t_ref, loss_ref, dh_ref, dw_ref):
        i = pl.program_id(0)

        @pl.when(i == 0)
        def _():
            loss_ref[...] = jnp.zeros_like(loss_ref)
            dw_ref[...] = jnp.zeros_like(dw_ref)
            dh_ref[...] = jnp.zeros_like(dh_ref)

        @pl.when(i > 0)
        def _():
            val, (dh, dw) = jax.value_and_grad(loss_fn, argnums=(0, 1))(h_ref[...], w_ref[0:1, :], t_ref[...])
            loss_ref[...] += val
            dh_ref[...] = dh
            dw_ref[0:1, :] += dw

    blk = pl.BlockSpec((tr, d), lambda i: (i, 0))
    wblk = pl.BlockSpec((8, d), lambda i: (0, 0))
    return pl.pallas_call(
        body, name="loss_head", grid=(t_rows // tr,),
        in_specs=[blk, wblk, pl.BlockSpec((tr, d), lambda i: (jnp.maximum(i - 1, 0), 0))],
        out_specs=(pl.BlockSpec((8, 128), lambda i: (0, 0)), blk, wblk),
        out_shape=(jax.ShapeDtypeStruct((8, 128), f32), jax.ShapeDtypeStruct((t_rows, d), f32),
                   jax.ShapeDtypeStruct((8, d), f32)),
        compiler_params=_params(("arbitrary",)),
    )(h, w8, target)


def adamw(w, m, v, partials, row_off, *, name):
    rows, d = w.shape
    layers = len(partials)
    per = rows // layers
    tr = _row_tile(per, d)
    assert row_off % tr == 0
    off, nblk = row_off // tr, per // tr
    c1 = 1.0 - ADAM_B1 ** ADAM_STEP
    c2 = 1.0 - ADAM_B2 ** ADAM_STEP

    def body(w_ref, m_ref, v_ref, *refs):
        p_refs, (g_ref, d_ref, mo_ref, vo_ref) = refs[:2 * layers], refs[2 * layers:]
        g = p_refs[0][...] + p_refs[1][...]
        for l in range(1, layers):
            g = jnp.where(pl.program_id(0) >= l * nblk, p_refs[2 * l][...] + p_refs[2 * l + 1][...], g)
        m_new = ADAM_B1 * m_ref[...] + (1.0 - ADAM_B1) * g
        v_new = ADAM_B2 * v_ref[...] + (1.0 - ADAM_B2) * (g * g)
        g_ref[...] = g
        d_ref[...] = -ADAM_LR * ((m_new / c1) / (jnp.sqrt(v_new / c2) + ADAM_EPS) + ADAM_WD * w_ref[...])
        mo_ref[...] = m_new
        vo_ref[...] = v_new

    blk = pl.BlockSpec((tr, d), lambda i: (i, 0))
    pblks = [pl.BlockSpec((tr, d), functools.partial(lambda i, l: (off + jnp.clip(i - l * nblk, 0, nblk - 1), 0), l=l))
             for l in range(layers) for _ in range(2)]
    out = jax.ShapeDtypeStruct((rows, d), f32)
    return pl.pallas_call(
        body, name=name, grid=(rows // tr,), in_specs=[blk, blk, blk] + pblks, out_specs=(blk,) * 4,
        out_shape=(out,) * 4, compiler_params=_params(("arbitrary",)),
    )(w, m, v, *[p for pair in partials for p in pair])


def reduce4(parts, *, name, own=None, me=None):
    _, rows, d = parts.shape
    tr = _row_tile(rows, d)

    def body(*refs):
        p_ref, o_ref = refs[0], refs[-1]
        acc = None
        for s in range(4):
            term = p_ref[s].astype(f32)
            if own is not None:
                term = jnp.where(refs[2][0] == s, refs[1][...].astype(f32), term)
            acc = term if acc is None else acc + term
        o_ref[...] = acc

    in_specs = [pl.BlockSpec((4, tr, d), lambda i: (0, i, 0))]
    args = [parts]
    if own is not None:
        in_specs += [pl.BlockSpec((tr, d), lambda i: (i, 0)), pl.BlockSpec(memory_space=pltpu.SMEM)]
        args += [own, me]
    return pl.pallas_call(
        body, name=name, grid=(rows // tr,), in_specs=in_specs,
        out_specs=pl.BlockSpec((tr, d), lambda i: (i, 0)), out_shape=jax.ShapeDtypeStruct((rows, d), f32),
        compiler_params=_params(("arbitrary",)),
    )(*args)


ANY = pl.BlockSpec(memory_space=pl.ANY)
MESH = pl.DeviceIdType.MESH
CHIP_FLIPS = ((0, 1), (1, 0), (1, 1))


def chip_exchange(bufs, scatter, *, name, after=None):
    nb = len(bufs)
    extra = [] if after is None else [after]

    def body(*refs):
        ins, outs = refs[:nb], refs[nb + len(extra):2 * nb + len(extra)]
        send_sems, recv_sems, local_sems = refs[2 * nb + len(extra):]
        x, y, c = lax.axis_index("x"), lax.axis_index("y"), lax.axis_index("c")
        me = 2 * x + y
        local = [pltpu.make_async_copy(ins[j].at[me] if scatter[j] else ins[j], outs[j].at[me], local_sems.at[j])
                 for j in range(nb)]
        for cp in local:
            cp.start()
        sends, recvs = [], []
        for k, (fx, fy) in enumerate(CHIP_FLIPS):
            px = 1 - x if fx else x
            py = 1 - y if fy else y
            chip = 2 * px + py
            for j in range(nb):
                src = ins[j].at[chip] if scatter[j] else ins[j]
                sems = dict(send_sem=send_sems.at[nb * k + j], recv_sem=recv_sems.at[nb * k + j],
                            device_id=(px, py, c), device_id_type=MESH)
                sends.append(pltpu.make_async_remote_copy(src_ref=src, dst_ref=outs[j].at[me], **sems))
                recvs.append(pltpu.make_async_remote_copy(src_ref=src, dst_ref=outs[j].at[chip], **sems))
        for cp in sends:
            cp.start()
        for cp in recvs:
            cp.wait_recv()
        for cp in sends:
            cp.wait_send()
        for cp in local:
            cp.wait()

    out_shape = tuple(jax.ShapeDtypeStruct(b.shape if s else (4,) + b.shape, b.dtype) for b, s in zip(bufs, scatter))
    return pl.pallas_call(
        body, name=name, in_specs=[ANY] * (nb + len(extra)), out_specs=(ANY,) * nb, out_shape=out_shape,
        scratch_shapes=[pltpu.SemaphoreType.DMA((3 * nb,)), pltpu.SemaphoreType.DMA((3 * nb,)),
                        pltpu.SemaphoreType.DMA((nb,))],
        compiler_params=pltpu.CompilerParams(has_side_effects=True),
    )(*bufs, *extra)


def gather_two_level(big, small, *, name):
    half = big.shape[1] // 2

    def body(big_ref, small_ref, obig_ref, osmall_ref, send_sems, recv_sems, local_sems):
        x, y, c = lax.axis_index("x"), lax.axis_index("y"), lax.axis_index("c")
        me = 2 * x + y
        mine = (slice(None), pl.ds(pl.multiple_of(c * half, half), half))
        theirs = (slice(None), pl.ds(pl.multiple_of((1 - c) * half, half), half))
        local = [pltpu.make_async_copy(big_ref, obig_ref.at[me], local_sems.at[0]),
                 pltpu.make_async_copy(small_ref, osmall_ref.at[me], local_sems.at[1])]
        for cp in local:
            cp.start()

        def copy(k, src, dst, to):
            return pltpu.make_async_remote_copy(src_ref=src, dst_ref=dst, send_sem=send_sems.at[k],
                                                recv_sem=recv_sems.at[k], device_id=to, device_id_type=MESH)

        sends, landed, passed, small_in = [], [], [], []
        for k, (fx, fy) in enumerate(CHIP_FLIPS):
            px = 1 - x if fx else x
            py = 1 - y if fy else y
            chip = 2 * px + py
            sends.append(copy(k, big_ref.at[mine], obig_ref.at[(me,) + mine], (px, py, c)))
            landed.append(copy(k, big_ref.at[mine], obig_ref.at[(chip,) + mine], (px, py, c)))
            sends.append(copy(3 + k, small_ref, osmall_ref.at[me], (px, py, c)))
            small_in.append(copy(3 + k, small_ref, osmall_ref.at[chip], (px, py, c)))
            passed.append((copy(6 + k, obig_ref.at[(chip,) + mine], obig_ref.at[(chip,) + mine], (x, y, 1 - c)),
                           copy(6 + k, obig_ref.at[(chip,) + theirs], obig_ref.at[(chip,) + theirs], (x, y, 1 - c))))
        for cp in sends:
            cp.start()
        for k in range(3):
            landed[k].wait_recv()
            passed[k][0].start()
        for k in range(3):
            passed[k][1].wait_recv()
            small_in[k].wait_recv()
        for cp in sends + [p[0] for p in passed]:
            cp.wait_send()
        for cp in local:
            cp.wait()

    return pl.pallas_call(
        body, name=name, in_specs=[ANY, ANY], out_specs=(ANY, ANY),
        out_shape=(jax.ShapeDtypeStruct((4,) + big.shape, big.dtype),
                   jax.ShapeDtypeStruct((4,) + small.shape, small.dtype)),
        scratch_shapes=[pltpu.SemaphoreType.DMA((9,)), pltpu.SemaphoreType.DMA((9,)), pltpu.SemaphoreType.DMA((2,))],
        compiler_params=pltpu.CompilerParams(has_side_effects=True),
    )(big, small)


def sibling_swap(bufs, *, name):
    nb = len(bufs)

    def body(*refs):
        ins, outs, (send_sems, recv_sems) = refs[:nb], refs[nb:2 * nb], refs[2 * nb:]
        peer = (lax.axis_index("x"), lax.axis_index("y"), 1 - lax.axis_index("c"))
        copies = [pltpu.make_async_remote_copy(src_ref=ins[j], dst_ref=outs[j], send_sem=send_sems.at[j],
                                               recv_sem=recv_sems.at[j], device_id=peer, device_id_type=MESH)
                  for j in range(nb)]
        for cp in copies:
            cp.start()
        for cp in copies:
            cp.wait_recv()
        for cp in copies:
            cp.wait_send()

    return pl.pallas_call(
        body, name=name, in_specs=[ANY] * nb, out_specs=(ANY,) * nb,
        out_shape=tuple(jax.ShapeDtypeStruct(b.shape, b.dtype) for b in bufs),
        scratch_shapes=[pltpu.SemaphoreType.DMA((nb,)), pltpu.SemaphoreType.DMA((nb,))],
        compiler_params=pltpu.CompilerParams(has_side_effects=True),
    )(*bufs)


HBM = pl.BlockSpec(memory_space=pltpu.HBM)
SEM = pl.BlockSpec(memory_space=pltpu.SEMAPHORE)
DATAFLOW = pltpu.SideEffectType.DATAFLOW_SIDE_EFFECTING


def _exchange_copies(srcs, lands, send_sems, recv_sems, scatter):
    x, y, c = lax.axis_index("x"), lax.axis_index("y"), lax.axis_index("c")
    me = 2 * x + y
    nb = len(srcs)
    pairs = []
    for k, (fx, fy) in enumerate(CHIP_FLIPS):
        px = 1 - x if fx else x
        py = 1 - y if fy else y
        chip = 2 * px + py
        for j in range(nb):
            src = srcs[j].at[chip] if scatter[j] else srcs[j]
            sems = dict(send_sem=send_sems.at[nb * k + j], recv_sem=recv_sems.at[nb * k + j],
                        device_id=(px, py, c), device_id_type=MESH)
            pairs.append((pltpu.make_async_remote_copy(src_ref=src, dst_ref=lands[j].at[me], **sems),
                          pltpu.make_async_remote_copy(src_ref=src, dst_ref=lands[j].at[chip], **sems)))
    return pairs


def exchange_start(bufs, scatter, after, *, name):
    nb = len(bufs)
    slabs = [b.shape[1:] if s else b.shape for b, s in zip(bufs, scatter)]
    lands = [lax.empty((4,) + shp, b.dtype) for b, shp in zip(bufs, slabs)]

    def body(*refs):
        srcs, zones = refs[:nb], refs[nb:2 * nb]
        send_sems, recv_sems = refs[2 * nb + 1:2 * nb + 3]
        token = refs[-1]
        for send, _ in _exchange_copies(srcs, zones, send_sems, recv_sems, scatter):
            send.start()
        token[...] = jnp.zeros_like(token)

    hbm = lambda a: pltpu.with_memory_space_constraint(a, pltpu.HBM)
    out = pl.pallas_call(
        body, name=name, in_specs=[HBM] * (2 * nb) + [ANY],
        out_specs=(SEM, SEM) + (HBM,) * (2 * nb) + (pl.BlockSpec(memory_space=pltpu.VMEM),),
        out_shape=(pltpu.SemaphoreType.DMA((3 * nb,)), pltpu.SemaphoreType.DMA((3 * nb,)))
        + tuple(pltpu.HBM(a.shape, a.dtype) for a in list(bufs) + lands) + (jax.ShapeDtypeStruct((8, 128), f32),),
        input_output_aliases={i: 2 + i for i in range(2 * nb)},
        compiler_params=pltpu.CompilerParams(has_side_effects=DATAFLOW),
    )(*[hbm(a) for a in list(bufs) + lands], after)
    return (out[:2], out[2:2 + nb], out[2 + nb:2 + 2 * nb], scatter), out[-1]


def exchange_wait(state, after, *, name):
    (send_sems, recv_sems), srcs, lands, scatter = state
    nb = len(srcs)

    def body(*refs):
        src_refs, zones = refs[:nb], refs[nb:2 * nb]
        s_sems, r_sems = refs[2 * nb:2 * nb + 2]
        for send, recv in _exchange_copies(src_refs, zones, s_sems, r_sems, scatter):
            send.wait_send()
            recv.wait_recv()

    out = pl.pallas_call(
        body, name=name, in_specs=[HBM] * (2 * nb) + [SEM, SEM, ANY], out_specs=(HBM,) * (2 * nb),
        out_shape=tuple(pltpu.HBM(a.shape, a.dtype) for a in list(srcs) + list(lands)),
        input_output_aliases={i: i for i in range(2 * nb)},
        compiler_params=pltpu.CompilerParams(has_side_effects=DATAFLOW),
    )(*srcs, *lands, send_sems, recv_sems, after)
    return out[nb:]


BIG = (
    ("w_proj_gdn", 256), ("w_proj_ssd", 256), ("w_proj_swa", 256), ("w_out", 256), ("w_up", 1024), ("w_down", 1024))
BIG_OFF = {}
_o = 0
for _n, _r in BIG:
    BIG_OFF[_n] = _o
    _o += _r
BIG_ROWS = _o
W_IN_SHARD = IN_W // 4

W_NAMES = ('meta_tokens', 'norm1_w', 'w_in', 'gdn_conv_w', 'gdn_a_log', 'gdn_dt_bias', 'gdn_norm_w', 'ssd_conv_w',
           'ssd_conv_b', 'ssd_dt_bias', 'ssd_a_log', 'ssd_d', 'ssd_norm_w', 'swa_sinks', 'w_proj_gdn', 'w_proj_ssd',
           'w_proj_swa', 'w_out', 'norm2_w', 'w_up', 'w_down', 'final_norm_w')
SMALL_NAMES = tuple(n for n in W_NAMES if n not in BIG_OFF and n != "w_in")
SMALL_SHARDED = ("meta_tokens", "gdn_conv_w", "ssd_conv_w")


def _pack_rows(parts, dtype):
    flat = jnp.concatenate([p.reshape(-1).astype(dtype) for p in parts])
    n = -(-flat.shape[0] // 8192) * 8192
    return jnp.pad(flat, (0, n - flat.shape[0])).reshape(-1, D_MODEL)


def _unpack_rows(packed, shapes):
    flat, out, o = packed.reshape(-1), [], 0
    for s in shapes:
        n = 1
        for d in s:
            n *= d
        out.append(flat[o:o + n].reshape(s))
        o += n
    return out


def _split_chips(full, axis):
    s = full.shape
    a = full.reshape(s[:axis] + (4, s[axis] // 4) + s[axis + 1:])
    return jnp.moveaxis(a, axis, 0)


def _join_chips(parts, axis):
    a = jnp.moveaxis(parts, 0, axis)
    s = a.shape
    return a.reshape(s[:axis] + (s[axis] * s[axis + 1],) + s[axis + 2:])


BIG_AXIS = {"w_proj_gdn": 1, "w_proj_ssd": 1, "w_proj_swa": 1, "w_out": 1, "w_up": 2, "w_down": 1}


def _w_in_to_padded(w):
    z = lambda n: jnp.zeros((n,) + w.shape[1:], w.dtype)
    return jnp.concatenate([w[8736:11808], w[4112:7184], w[7200:8736], w[4096:4112], z(112),
                            w[7184:7200], z(112 + C_MID_END - C_SDT - 128), w[0:4096]], axis=0)


def _w_in_from_padded(p):
    return jnp.concatenate([p[C_GQ:IN_WP], p[C_BA:C_BA + 16], p[C_SZ:C_WQ], p[C_SDT:C_SDT + 16], p[C_WQ:C_BA],
                            p[0:C_SZ]], axis=0)


def _row8(v, lane0=0, width=128):
    return jnp.pad(v[None, :], ((0, 7), (lane0, width - lane0 - v.shape[0])))


def _layer_fwd(h, p, l, late=None):
    tag = f"l{l}"
    u, hn = mm(h, p["w_in_t"], tb=True, out_dtype=f32, norm_w8=p["n1"], name=f"mm_in_{tag}")
    yg, stg, tg = gdn_fwd(u, p["gcw"], p["galog"], p["gdtb"], p["gnw"])
    ys, sts = ssd_fwd(u, p["scw"], p["sdtb"], p["salog"], p["sd"], p["snw"])
    yw = swa_fwd(u, p["sink"])
    if late is not None:
        p.update(late(yw))
    pg = mm(yg, p["wpg"], out_dtype=f32, name=f"mm_pg_{tag}")
    ps = mm(ys, p["wps"], out_dtype=f32, name=f"mm_ps_{tag}")
    pw = mm(yw, p["wpw"], out_dtype=f32, name=f"mm_pw_{tag}")
    merged = merge_fwd(pg, ps, pw, u)
    h2 = mm(merged, p["wout"], out_dtype=f32, resid=h, name=f"mm_out_{tag}")
    a, r, hn2 = mm(h2, p["wup"], out_dtype=f32, relu2_out=True, norm_w8=p["n2"], name=f"mm_up_{tag}")
    h3 = mm(r, p["wdown"], out_dtype=f32, resid=h2, name=f"mm_down_{tag}")
    saved = dict(h=h, hn=hn, u=u, yg=yg, stg=stg, tg=tg, ys=ys, sts=sts, yw=yw, pg=pg, ps=ps, pw=pw,
                 merged=merged, h2=h2, hn2=hn2, a=a, r=r)
    return h3, saved


def _layer_bwd(dh3, p, s, l, send_big, send_w_in):
    tag = f"l{l}"
    g = {}

    def wgrad(act, d, name):
        return mm(act, d, ta=True, out_dtype=bf16, name=f"wg_{name}_{tag}")

    da = mm(dh3, p["wdown"], tb=True, out_dtype=bf16, relu_grad_of=s["a"], name=f"dg_down_{tag}")
    g["w_down"] = wgrad(s["r"], dh3, "down")
    dhn2 = mm(da, p["wup"], tb=True, out_dtype=f32, name=f"dg_up_{tag}")
    g["w_up"] = wgrad(s["hn2"], da, "up")
    dh2, g["norm2_w"] = rmsnorm_bwd(s["h2"], p["n2"], dhn2, dh3, name=f"norm2_bwd_{tag}")
    dmerged = mm(dh2, p["wout"], tb=True, out_dtype=f32, name=f"dg_out_{tag}")
    g["w_out"] = wgrad(s["merged"], dh2, "out")
    du = lax.empty((dh3.shape[0], IN_WP), bf16)
    dpg, dps, dpw, du = merge_bwd(s["pg"], s["ps"], s["pw"], s["u"], dmerged, du)
    dyg = mm(dpg, p["wpg"], tb=True, out_dtype=f32, name=f"dg_pg_{tag}")
    dys = mm(dps, p["wps"], tb=True, out_dtype=f32, name=f"dg_ps_{tag}")
    dyw = mm(dpw, p["wpw"], tb=True, out_dtype=f32, name=f"dg_pw_{tag}")
    g["w_proj_gdn"] = wgrad(s["yg"], dpg, "pg")
    g["w_proj_ssd"] = wgrad(s["ys"], dps, "ps")
    g["w_proj_swa"] = wgrad(s["yw"], dpw, "pw")
    sent = send_big(jnp.concatenate([_split_chips(g.pop(n), BIG_AXIS[n] - 1).reshape(4, r, D_MODEL)
                                     for n, r in BIG], axis=1))

    (du, dba, dtq, dtk, dtv, g["gdn_a_log"], g["gdn_dt_bias"], g["gdn_norm_w"]) = gdn_bwd(
        s["u"], p["gcw"] + sent, p["galog"], p["gdtb"], p["gnw"], s["stg"], s["tg"], dyg, du)
    g["gdn_conv_w"] = jnp.concatenate([dtq, dtk, dtv], axis=1)[:4]
    (du, ddt, dtx, dtb, dtc, g["ssd_dt_bias"], g["ssd_a_log"], g["ssd_d"], g["ssd_norm_w"]) = ssd_bwd(
        s["u"], p["scw"], p["sdtb"], p["salog"], p["sd"], p["snw"], s["sts"], dys, du)
    dconv = jnp.concatenate([dtx, dtb, dtc], axis=1)
    g["ssd_conv_w"], g["ssd_conv_b"] = dconv[:4], dconv[4]
    du, g["swa_sinks"] = swa_bwd(s["u"], p["sink"], dyw, du)
    mid = jnp.concatenate([dba[0].astype(bf16), ddt.astype(bf16),
                           jnp.zeros((du.shape[0], C_MID_END - C_SDT - 128), bf16)], axis=1)
    du = lax.dynamic_update_slice(du, mid, (0, C_BA))
    sent = send_w_in(_w_in_from_padded(wgrad(du, s["hn"], "in")).reshape(4, W_IN_SHARD, D_MODEL))
    dhn = mm(du, p["w_in_t"], out_dtype=f32, name=f"dg_in_{tag}")
    dh, g["norm1_w"] = rmsnorm_bwd(s["h"], p["n1"] + sent, dhn, dh2, name=f"norm1_bwd_{tag}")
    return dh, g


def kernel(x, meta_tokens, norm1_w, w_in, gdn_conv_w, gdn_a_log, gdn_dt_bias, gdn_norm_w, ssd_conv_w, ssd_conv_b, ssd_dt_bias, ssd_a_log, ssd_d, ssd_norm_w, swa_sinks, w_proj_gdn, w_proj_ssd, w_proj_swa, w_out, norm2_w, w_up, w_down, final_norm_w, loss_target, m_meta_tokens, m_norm1_w, m_w_in, m_gdn_conv_w, m_gdn_a_log, m_gdn_dt_bias, m_gdn_norm_w, m_ssd_conv_w, m_ssd_conv_b, m_ssd_dt_bias, m_ssd_a_log, m_ssd_d, m_ssd_norm_w, m_swa_sinks, m_w_proj_gdn, m_w_proj_ssd, m_w_proj_swa, m_w_out, m_norm2_w, m_w_up, m_w_down, m_final_norm_w, v_meta_tokens, v_norm1_w, v_w_in, v_gdn_conv_w, v_gdn_a_log, v_gdn_dt_bias, v_gdn_norm_w, v_ssd_conv_w, v_ssd_conv_b, v_ssd_dt_bias, v_ssd_a_log, v_ssd_d, v_ssd_norm_w, v_swa_sinks, v_w_proj_gdn, v_w_proj_ssd, v_w_proj_swa, v_w_out, v_norm2_w, v_w_up, v_w_down, v_final_norm_w):
    given = dict(locals())
    depth = norm1_w.shape[0]
    me = 2 * lax.axis_index("x") + lax.axis_index("y")

    me1 = jnp.reshape(me, (1,)).astype(jnp.int32)
    w_in_t = jnp.swapaxes(w_in, 1, 2)

    def weight_slabs(l):
        return (w_in_t[l].astype(bf16),
                jnp.concatenate([given[n][l].reshape(-1, D_MODEL).astype(bf16) for n, _ in BIG]))

    slabs = [weight_slabs(l) for l in range(depth)]
    wsmall = _pack_rows([given[n] for n in SMALL_SHARDED], f32)
    ga0, gsmall = gather_two_level(slabs[0][0], wsmall, name="gather_first")
    gathers, started = {}, jnp.zeros((), f32)
    for l in range(depth):
        for j in range(2):
            if (l, j) != (0, 0):
                gathers[l, j], token = exchange_start([slabs[l][j]], (False,), gsmall, name=f"gather_start_l{l}_{j}")
                started = started + token[0, 0]
    shard_shapes = [given[n].shape for n in SMALL_SHARDED]
    per_chip = [_unpack_rows(gsmall[s], shard_shapes) for s in range(4)]
    full = {n: jnp.concatenate([per_chip[s][i] for s in range(4)], axis=-1) for i, n in enumerate(SMALL_SHARDED)}

    def landed(l, j, after):
        (zone,) = exchange_wait(gathers[l, j], after, name=f"gather_wait_l{l}_{j}")
        return lax.dynamic_update_slice(zone, slabs[l][j][None], (me, 0, 0))

    def first_operands(l, ga, order):
        return dict(
            n1=_row8(norm1_w[l], width=D_MODEL) + order, n2=_row8(norm2_w[l], width=D_MODEL),
            w_in_t=_w_in_to_padded(ga.reshape(IN_W, D_MODEL)),
            gcw=jnp.pad(full["gdn_conv_w"][l], ((0, 4), (0, 0))),
            galog=_row8(gdn_a_log[l], 8), gdtb=_row8(gdn_dt_bias[l], 8), gnw=_row8(gdn_norm_w[l]),
            scw=jnp.pad(jnp.concatenate([full["ssd_conv_w"][l], ssd_conv_b[l][None]], axis=0), ((0, 3), (0, 0))),
            sdtb=_row8(ssd_dt_bias[l]), salog=_row8(ssd_a_log[l]), sd=_row8(ssd_d[l]),
            snw=_row8(ssd_norm_w[l], width=D_MODEL), sink=_row8(swa_sinks[l]))

    def late_operands(l, after):
        gb = landed(l, 1, after)
        w = {}
        for n, r in BIG:
            parts = gb[:, BIG_OFF[n]:BIG_OFF[n] + r].reshape((4,) + given[n].shape[1:])
            w[n] = _join_chips(parts, BIG_AXIS[n] - 1)
        return dict(wpg=w["w_proj_gdn"], wps=w["w_proj_ssd"], wpw=w["w_proj_swa"], wout=w["w_out"],
                    wup=w["w_up"], wdown=w["w_down"])

    h = jnp.concatenate([jnp.zeros((PAD, D_MODEL), f32), full["meta_tokens"], x[0]], axis=0)
    layers, saved = [], []
    for l in range(depth):
        p = first_operands(0, ga0, started) if l == 0 else first_operands(l, landed(l, 0, h), 0.0)
        h, s = _layer_fwd(h, p, l, late=functools.partial(late_operands, l))
        layers.append(p)
        saved.append(s)
    loss8, dh, dfw8 = loss_head(h, _row8(final_norm_w, width=D_MODEL), loss_target[0])
    grads = {"final_norm_w": dfw8[0]}
    per_layer, grad_slabs, scatters = [None] * depth, {}, {}

    def send(l, j, slab):
        grad_slabs[l, j] = slab
        scatters[l, j], token = exchange_start([slab], (True,), loss8, name=f"scatter_start_l{l}_{j}")
        return token[0, 0]

    for l in reversed(range(depth)):
        dh, per_layer[l] = _layer_bwd(dh, layers[l], saved[l], l, functools.partial(send, l, 1),
                                      functools.partial(send, l, 0))
    grad_x = dh[HEAD_ROWS:][None]
    grads["meta_tokens"] = dh[PAD:HEAD_ROWS]
    lane = {"gdn_a_log": (8, 8), "gdn_dt_bias": (8, 8), "gdn_norm_w": (0, 128), "ssd_dt_bias": (0, 16),
            "ssd_a_log": (0, 16), "ssd_d": (0, 16), "swa_sinks": (0, 16)}
    for n in per_layer[0]:
        parts = [per_layer[l][n] for l in range(depth)]
        if n in lane:
            parts = [q[0, lane[n][0]:lane[n][0] + lane[n][1]] for q in parts]
        elif n in ("norm1_w", "norm2_w", "ssd_norm_w"):
            parts = [q[0] for q in parts]
        grads[n] = jnp.stack(parts)
    loss = lax.psum(loss8[0, 0], ("x", "y", "c"))

    gs = _pack_rows([grads[n] for n in SMALL_NAMES], f32)
    def chip_sum(l, j, after):
        (zone,) = exchange_wait(scatters[l, j], after, name=f"scatter_wait_l{l}_{j}")
        own = lax.dynamic_index_in_dim(grad_slabs[l, j], me, 0, keepdims=False)
        return reduce4(zone, own=own, me=me1, name=f"sum_chips_l{l}_{j}")

    early = [(l, j) for l in range(depth) for j in range(2) if (l, j) != (0, 0)]
    mine = {lj: chip_sum(*lj, dh) for lj in early}
    sibs = dict(zip(early, sibling_swap([mine[lj] for lj in early], name="swap_cores_early")))
    out = {}
    for n, r in BIG:
        shp = given[n].shape
        res = adamw(*[given[pre + n].reshape(depth * r, D_MODEL) for pre in ("", "m_", "v_")],
                    [(mine[l, 1], sibs[l, 1]) for l in range(depth)], BIG_OFF[n], name=f"adamw_{n}")
        out[n] = [a.reshape(shp) for a in res]
    mine[0, 0] = chip_sum(0, 0, res[1])
    (rs,) = chip_exchange([gs], (False,), after=mine[0, 0], name="gather_small_grads")
    ps_ = reduce4(rs, name="sum_chips_small")
    sibs[0, 0], ss = sibling_swap([mine[0, 0], ps_], name="swap_cores_last")
    res = adamw(*[jnp.swapaxes(given[pre + "w_in"], 1, 2).reshape(depth * W_IN_SHARD, D_MODEL)
                  for pre in ("", "m_", "v_")],
                [(mine[l, 0], sibs[l, 0]) for l in range(depth)], 0, name="adamw_w_in")
    out["w_in"] = [jnp.swapaxes(a.reshape(w_in_t.shape), 1, 2) for a in res]
    full_shapes = [grads[n].shape for n in SMALL_NAMES]
    mine_s, sib_s = _unpack_rows(ps_, full_shapes), _unpack_rows(ss, full_shapes)

    def local(parts):
        loc = []
        for n, a in zip(SMALL_NAMES, parts):
            if n in SMALL_SHARDED:
                sz = a.shape[-1] // 4
                a = lax.dynamic_slice_in_dim(a, me * sz, sz, axis=a.ndim - 1)
            loc.append(a)
        return _pack_rows(loc, f32)

    res = adamw(_pack_rows([given[n] for n in SMALL_NAMES], f32), _pack_rows([given["m_" + n] for n in SMALL_NAMES], f32),
                _pack_rows([given["v_" + n] for n in SMALL_NAMES], f32), [(local(mine_s), local(sib_s))], 0,
                name="adamw_small")
    local_shapes = [given[n].shape for n in SMALL_NAMES]
    unpacked = [_unpack_rows(a, local_shapes) for a in res]
    for i, n in enumerate(SMALL_NAMES):
        out[n] = [unpacked[j][i] for j in range(4)]

    return (loss, grad_x) + tuple(out[n][j] for j in range(4) for n in W_NAMES)
```

```python
import functools

import jax
import jax.numpy as jnp
from jax import lax
from jax.experimental import pallas as pl
from jax.experimental.pallas import tpu as pltpu

f32 = jnp.float32
bf16 = jnp.bfloat16

D_MODEL = 1024
N_META = 16
PAD = 112
HEAD_ROWS = PAD + N_META
RMS_EPS = 1e-6
L2_EPS = 1e-6
D_FF = 4 * D_MODEL

GDN_HEADS = 8
GDN_D = 128
GDN_CHUNK = 64
SSD_HEADS = 16
SSD_P = 64
SSD_GROUPS = 4
SSD_HPG = 4
SSD_N = 128
SSD_CHUNK = 128
SWA_Q_HEADS = 16
SWA_KV_HEADS = 4
SWA_REP = 4
SWA_D = 64
SWA_W = 128

C_GATE = 0
C_SZ, C_SX, C_SB, C_SC = 3072, 4096, 5120, 5632
C_WQ, C_WK, C_WV = 6144, 7168, 7424
C_BA = 7680
C_SDT = 7808
C_MID_END = 8192
C_GQ, C_GK, C_GV, C_GG = 8192, 9216, 10240, 11264
IN_WP = 12288
IN_W = 11808

ADAM_LR, ADAM_B1, ADAM_B2, ADAM_EPS, ADAM_WD, ADAM_STEP = 0.001, 0.9, 0.999, 1e-08, 0.01, 10

VMEM_LIMIT = 56 * 1024 * 1024
BLOCK_BYTES = 3 << 19
MM_OPERAND_BYTES = 9 << 20
MM_RESIDENT_BYTES = 13 << 20

NN = (((1,), (0,)), ((), ()))
NT = (((1,), (1,)), ((), ()))
TN = (((0,), (0,)), ((), ()))


def _dot(a, b, dims=NN):
    return lax.dot_general(a.astype(bf16), b.astype(bf16), dims, preferred_element_type=f32)


def _dotx(a, b, dims=NN):
    return lax.dot_general(a, b, dims, preferred_element_type=f32, precision=lax.Precision.HIGH)


def _iota(shape, axis):
    return lax.broadcasted_iota(jnp.int32, shape, axis)


def _softplus(x):
    return jnp.maximum(x, 0.0) + jnp.log1p(jnp.exp(-jnp.abs(x)))


_sigmoid = jax.nn.sigmoid


def _silu(x):
    return x * _sigmoid(x)


def _params(sem):
    return pltpu.CompilerParams(dimension_semantics=sem, vmem_limit_bytes=VMEM_LIMIT)


@functools.partial(jax.custom_vjp, nondiff_argnums=(1,))
def _window(x_ext, off):
    if off == 8:
        return x_ext[8:]
    return pltpu.roll(x_ext, 8 - off, 0)[8:]


def _window_fwd(x_ext, off):
    return _window(x_ext, off), None


def _window_bwd(off, _, g):
    n, w = g.shape
    g_ext = jnp.concatenate([jnp.zeros((8, w), g.dtype), g], axis=0)
    if off == 8:
        return (g_ext,)
    return (pltpu.roll(g_ext, n + off, 0),)


_window.defvjp(_window_fwd, _window_bwd)


def _conv4(x, halo, taps):
    x_ext = jnp.concatenate([halo, x], axis=0)
    y = taps[3] * x
    for j in range(3):
        y = y + taps[j] * _window(x_ext, 5 + j)
    return y


def _blockinv_impl(a):
    n = a.shape[0]
    ri, ci = _iota((n, n), 0), _iota((n, n), 1)
    t = (ri == ci).astype(f32) - jnp.where(((ri >> 1) == (ci >> 1)) & (ri > ci), a, 0.0)
    k = 1
    while (1 << k) < n:
        sel = ((ri >> (k + 1)) == (ci >> (k + 1))) & (((ri >> k) & 1) == 1) & (((ci >> k) & 1) == 0)
        o = jnp.where(sel, a, 0.0)
        t = t - _dotx(_dotx(t, o), t)
        k += 1
    return t


@jax.custom_vjp
def _blockinv(a):
    return _blockinv_impl(a)


def _blockinv_fwd(a):
    t = _blockinv_impl(a)
    return t, t


def _blockinv_bwd(t, dt):
    return (-_dotx(_dotx(t, dt, TN), t, NT),)


_blockinv.defvjp(_blockinv_fwd, _blockinv_bwd)


@jax.custom_vjp
def _blockinv_given(a, t):
    return t


_blockinv_given.defvjp(lambda a, t: (t, t), lambda t, dt: _blockinv_bwd(t, dt) + (jnp.zeros_like(t),))


def _scan_rows(x, reverse):
    n = x.shape[0]
    row = _iota(x.shape, 0)
    s = 1
    while s < n:
        if reverse:
            x = x + jnp.where(row < n - s, pltpu.roll(x, n - s, 0), 0.0)
        else:
            x = x + jnp.where(row >= s, pltpu.roll(x, s, 0), 0.0)
        s *= 2
    return x


@jax.custom_vjp
def _cumsum_rows(x):
    return _scan_rows(x, False)


_cumsum_rows.defvjp(lambda x: (_scan_rows(x, False), None), lambda _, g: (_scan_rows(g, True),))


def _gdn_act(xq, xk, xv, hq, hk, hv, tq, tk, tv):
    return _silu(_conv4(xq, hq, tq)), _silu(_conv4(xk, hk, tk)), _silu(_conv4(xv, hv, tv))


def _gdn_core(q, k, v, gate, mb, mg, mr, s, t_given, beta16, g16, gam16, gam16_t, nw):
    c = GDN_CHUNK
    q = q * lax.rsqrt(jnp.sum(q * q, axis=1, keepdims=True) + L2_EPS) * (GDN_D ** -0.5)
    k = k * lax.rsqrt(jnp.sum(k * k, axis=1, keepdims=True) + L2_EPS)

    pick = lambda x, m: jnp.sum(x * m, axis=1, keepdims=True)
    beta = pick(beta16, mb)
    g = jnp.broadcast_to(pick(g16, mg), (c, GDN_D))
    gam1 = pick(gam16, mg)
    gam = jnp.broadcast_to(gam1, (c, GDN_D))
    gam_j = jnp.broadcast_to(jnp.sum(gam16_t * mr, axis=0, keepdims=True), (c, c))

    ri, ci = _iota((c, c), 0), _iota((c, c), 1)
    incl = ci <= ri
    decay = jnp.where(incl, jnp.exp(jnp.where(incl, jnp.broadcast_to(gam1, (c, c)) - gam_j, 0.0)), 0.0)

    kb = k * beta
    a = jnp.where(ci < ri, _dot(kb, k, NT) * decay, 0.0)
    t = _blockinv(a) if t_given is None else _blockinv_given(a, t_given)
    egam = jnp.exp(gam)
    u = _dotx(t, v * beta)
    w = _dotx(t, kb * egam)
    attn = _dot(q, k, NT) * decay
    gl = jnp.sum(g, axis=0, keepdims=True)
    kt = k * jnp.exp(gl - gam)
    v_new = u - _dot(w, s)
    o = _dot(q * egam, s) + _dot(attn, v_new)
    s_out = s * jnp.exp(gl) + _dot(kt, v_new, TN)

    y = o * lax.rsqrt(jnp.mean(o * o, axis=1, keepdims=True) + RMS_EPS) * nw * _silu(gate)
    return y, s_out, t


def _gdn_chunk(q, k, v, gate, s, t_given, ba, alog, dtb, nw, *, masks, row0):
    valid = (row0 + _iota((GDN_CHUNK, 1), 0)) >= PAD
    beta16 = jnp.where(valid, _sigmoid(ba), 0.0)
    g16 = jnp.where(valid, -jnp.exp(alog) * _softplus(ba + dtb), 0.0)
    gam16 = _cumsum_rows(g16)
    core = jax.vmap(_gdn_core, in_axes=(0,) * 8 + (None if t_given is None else 0,) + (None,) * 5)
    y, s_out, t = core(q, k, v, gate, *masks, s, t_given, beta16, g16, gam16, gam16.T, nw)
    return (y, s_out, t) if t_given is None else (y, s_out)


def _gdn_specs(hb, nc, rev):
    w = hb * GDN_D

    def cidx(c):
        return (nc - 1 - c) if rev else c

    def col(base):
        return pl.BlockSpec((GDN_CHUNK, w), lambda h, c: (cidx(c), base // w + h))

    def halo(base):
        return pl.BlockSpec((8, w), lambda h, c: (jnp.maximum(cidx(c) * (GDN_CHUNK // 8) - 1, 0), base // w + h))

    def taps(base):
        return pl.BlockSpec((8, w), lambda h, c: (0, base // w + h))

    ba = pl.BlockSpec((GDN_CHUNK, 128), lambda h, c: (cidx(c), C_BA // 128))
    row = pl.BlockSpec((8, 128), lambda h, c: (0, 0))
    y = pl.BlockSpec((GDN_CHUNK, w), lambda h, c: (cidx(c), h))
    st = pl.BlockSpec((1, hb, GDN_D, GDN_D), lambda h, c: (cidx(c), h, 0, 0))
    in_specs = [col(C_GQ), col(C_GK), col(C_GV), halo(C_GQ), halo(C_GK), halo(C_GV), col(C_GG), ba,
                taps(0), taps(1024), taps(2048), row, row, row]
    return in_specs, y, st, taps, row, col, ba


def _gdn_load(refs, first):
    xq, xk, xv, hq, hk, hv, gate, ba, tq, tk, tv, alog, dtb, nw = refs

    def halo(r):
        return jnp.where(first, 0.0, r[...])

    def taps(r):
        return tuple(r[j:j + 1, :] for j in range(4))

    act = (xq[...], xk[...], xv[...], halo(hq), halo(hk), halo(hv), taps(tq), taps(tk), taps(tv))
    return act, gate[...], (ba[...], alog[0:1, :], dtb[0:1, :], nw[0:1, :])


def _heads(a, hb):
    return jnp.stack([a[:, i * GDN_D:(i + 1) * GDN_D] for i in range(hb)])


def _wide(a):
    return jnp.concatenate([a[i] for i in range(a.shape[0])], axis=1)


def _head_masks(hblk, hb):
    head = hblk * hb + _iota((hb, 1, 128), 0)
    lane = _iota((hb, 1, 128), 2)
    rows = (_iota((hb, 128, 1), 1) == hblk * hb + _iota((hb, 128, 1), 0) + 8).astype(f32)
    return (lane == head).astype(f32), (lane == head + 8).astype(f32), rows


def gdn_fwd(u, conv_w8, alog8, dtb8, nw8, *, hb=8):
    t_rows = u.shape[0]
    nc = t_rows // GDN_CHUNK
    in_specs, y_spec, st_spec, *_ = _gdn_specs(hb, nc, False)

    def body(*refs):
        ins, (y_ref, st_ref, t_ref), (s_scr,) = refs[:14], refs[14:17], refs[17:]
        hblk, c = pl.program_id(0), pl.program_id(1)

        @pl.when(c == 0)
        def _():
            s_scr[...] = jnp.zeros_like(s_scr)

        act, gate, shared = _gdn_load(ins, c == 0)
        s = s_scr[...]
        st_ref[0] = s
        qa, ka, va = _gdn_act(*act)
        y, s_new, t = _gdn_chunk(_heads(qa, hb), _heads(ka, hb), _heads(va, hb), _heads(gate, hb), s, None, *shared,
                                 masks=_head_masks(hblk, hb), row0=c * GDN_CHUNK)
        y_ref[...] = _wide(y).astype(bf16)
        t_ref[0] = t
        s_scr[...] = s_new

    return pl.pallas_call(
        body, name="gdn_fwd", grid=(GDN_HEADS // hb, nc),
        in_specs=in_specs,
        out_specs=(y_spec, st_spec, pl.BlockSpec((1, hb, GDN_CHUNK, GDN_CHUNK), lambda h, c: (c, h, 0, 0))),
        out_shape=(jax.ShapeDtypeStruct((t_rows, D_MODEL), bf16),
                   jax.ShapeDtypeStruct((nc, GDN_HEADS, GDN_D, GDN_D), f32),
                   jax.ShapeDtypeStruct((nc, GDN_HEADS, GDN_CHUNK, GDN_CHUNK), f32)),
        scratch_shapes=[pltpu.VMEM((hb, GDN_D, GDN_D), f32)],
        compiler_params=_params(("arbitrary", "arbitrary")),
    )(u, u, u, u, u, u, u, u, conv_w8, conv_w8, conv_w8, alog8, dtb8, nw8)


def gdn_bwd(u, conv_w8, alog8, dtb8, nw8, states, tinv, dy, du):
    t_rows = u.shape[0]
    nc = t_rows // GDN_CHUNK
    hb = GDN_HEADS
    w = hb * GDN_D
    in_specs, y_spec, st_spec, taps, row, col, ba = _gdn_specs(hb, nc, True)
    nhb = GDN_HEADS // hb

    def body(*refs):
        ins, st_ref, t_ref, dy_ref = refs[:14], refs[14], refs[15], refs[16]
        du_ref, dba_ref, dtq_ref, dtk_ref, dtv_ref, dalog_ref, ddtb_ref, dnw_ref = refs[18:26]
        ds_scr, dh_scr = refs[26:]
        hblk, cc = pl.program_id(0), pl.program_id(1)
        c = nc - 1 - cc

        @pl.when(cc == 0)
        def _():
            ds_scr[...] = jnp.zeros_like(ds_scr)
            dh_scr[...] = jnp.zeros_like(dh_scr)
            dtq_ref[...] = jnp.zeros_like(dtq_ref)
            dtk_ref[...] = jnp.zeros_like(dtk_ref)
            dtv_ref[...] = jnp.zeros_like(dtv_ref)

        @pl.when((cc == 0) & (hblk == 0))
        def _():
            dalog_ref[...] = jnp.zeros_like(dalog_ref)
            ddtb_ref[...] = jnp.zeros_like(ddtb_ref)
            dnw_ref[...] = jnp.zeros_like(dnw_ref)

        act, gate, shared = _gdn_load(ins, c == 0)
        (qa, ka, va), vjp_act = jax.vjp(_gdn_act, *act)
        chunk = functools.partial(_gdn_chunk, masks=_head_masks(hblk, hb), row0=c * GDN_CHUNK)
        _, vjp_core = jax.vjp(chunk, _heads(qa, hb), _heads(ka, hb), _heads(va, hb), _heads(gate, hb), st_ref[0],
                              t_ref[0], *shared)
        dqa, dka, dva, dgate, ds, _, dba, dalog, ddtb, dnw = vjp_core(
            (_heads(dy_ref[...].astype(f32), hb), ds_scr[...]))
        ds_scr[...] = ds
        dxq, dxk, dxv, dhq, dhk, dhv, dtq, dtk, dtv = vjp_act((_wide(dqa), _wide(dka), _wide(dva)))
        zeros = jnp.zeros((GDN_CHUNK - 8, w), f32)
        for j, (dx, dh) in enumerate(((dxq, dhq), (dxk, dhk), (dxv, dhv))):
            du_ref[:, j * w:(j + 1) * w] = (dx + jnp.concatenate([zeros, dh_scr[j]], axis=0)).astype(bf16)
            dh_scr[j] = dh
        du_ref[:, 3 * w:4 * w] = _wide(dgate).astype(bf16)
        dba_ref[0] = dba
        for dt_ref, dtaps in ((dtq_ref, dtq), (dtk_ref, dtk), (dtv_ref, dtv)):
            for j in range(4):
                dt_ref[j:j + 1, :] += dtaps[j]
        dalog_ref[0:1, :] += dalog
        ddtb_ref[0:1, :] += ddtb
        dnw_ref[0:1, :] += dnw

    out_specs = (pl.BlockSpec((GDN_CHUNK, 4 * w), lambda h, c: (nc - 1 - c, C_GQ // (4 * w))),
                 pl.BlockSpec((1, GDN_CHUNK, 128), lambda h, c: (h, nc - 1 - c, 0)),
                 taps(0), taps(0), taps(0), row, row, row)
    out_shape = (jax.ShapeDtypeStruct(du.shape, du.dtype),
                 jax.ShapeDtypeStruct((nhb, t_rows, 128), f32),
                 jax.ShapeDtypeStruct((8, D_MODEL), f32), jax.ShapeDtypeStruct((8, D_MODEL), f32),
                 jax.ShapeDtypeStruct((8, D_MODEL), f32),
                 jax.ShapeDtypeStruct((8, 128), f32), jax.ShapeDtypeStruct((8, 128), f32), jax.ShapeDtypeStruct((8, 128), f32))
    return pl.pallas_call(
        body, name="gdn_bwd", grid=(nhb, nc),
        in_specs=in_specs + [st_spec, pl.BlockSpec((1, hb, GDN_CHUNK, GDN_CHUNK), lambda h, c: (nc - 1 - c, h, 0, 0)),
                             y_spec, ANY],
        out_specs=out_specs, out_shape=out_shape, input_output_aliases={17: 0},
        scratch_shapes=[pltpu.VMEM((hb, GDN_D, GDN_D), f32), pltpu.VMEM((3, 8, w), f32)],
        compiler_params=_params(("arbitrary", "arbitrary")),
    )(u, u, u, u, u, u, u, u, conv_w8, conv_w8, conv_w8, alog8, dtb8, nw8, states, tinv, dy, du)


def _ssd_act(xs_r, b_r, c_r, hx, hbm, hcm, tx, tb, tc, bx, bb, bc, *, row0):
    valid = (row0 + _iota((SSD_CHUNK, 1), 0)) >= PAD
    act = lambda x, h, t, b: jnp.where(valid, _silu(_conv4(x, h, t) + b), 0.0)
    return act(xs_r, hx, tx, bx), act(b_r, hbm, tb, bb), act(c_r, hcm, tc, bc)


def _ssd_core(xs, bm, cm, z, nw, lanes, rows, h, dtp16, adt16, acum16, acum16_t, dsk):
    n = SSD_CHUNK
    pick = lambda x, m: jnp.sum(x * m, axis=1, keepdims=True)
    lane_r = _iota((1, 256), 1) >> 6
    dtp = jnp.zeros((n, 256), f32)
    adt = jnp.zeros((n, 256), f32)
    acum = jnp.zeros((n, 256), f32)
    dlane = jnp.zeros((1, 256), f32)
    ccols = []
    for r in range(SSD_HPG):
        ccols.append(pick(acum16, lanes[r]))
        dtp = jnp.where(lane_r == r, pick(dtp16, lanes[r]), dtp)
        adt = jnp.where(lane_r == r, pick(adt16, lanes[r]), adt)
        acum = jnp.where(lane_r == r, ccols[r], acum)
        dlane = jnp.where(lane_r == r, pick(dsk, lanes[r]), dlane)

    ri, ci = _iota((n, n), 0), _iota((n, n), 1)
    incl = ci <= ri
    al = jnp.sum(adt, axis=0, keepdims=True)
    xdt = xs * dtp
    cb = _dot(cm, bm, NT)
    y = _dot(cm, h) * jnp.exp(acum) + dlane * xs
    for r in range(SSD_HPG):
        ai = jnp.broadcast_to(ccols[r], (n, n))
        aj = jnp.broadcast_to(jnp.sum(acum16_t * rows[r], axis=0, keepdims=True), (n, n))
        lm = jnp.where(incl, jnp.exp(jnp.where(incl, ai - aj, 0.0)), 0.0)
        y = y + _dot(cb * lm, jnp.where(lane_r == r, xdt, 0.0))
    h_out = h * jnp.exp(al) + _dot(bm, jnp.exp(al - acum) * xdt, TN)
    y = y * _silu(z)
    y = y * lax.rsqrt(jnp.mean(y * y, axis=1, keepdims=True) + RMS_EPS) * nw
    return y, h_out


def _ssd_chunk(xs, bm, cm, z, nw, h, dt, dtb, alog, dsk, *, row0):
    valid = (row0 + _iota((SSD_CHUNK, 1), 0)) >= PAD
    dtp16 = jnp.where(valid, _softplus(dt + dtb), 0.0)
    adt16 = -jnp.exp(alog) * dtp16
    acum16 = _cumsum_rows(adt16)
    lanes = tuple((_iota((SSD_GROUPS, 1, 128), 2) == _iota((SSD_GROUPS, 1, 128), 0) * SSD_HPG + r).astype(f32)
                  for r in range(SSD_HPG))
    rows = tuple((_iota((SSD_GROUPS, 128, 1), 1) == _iota((SSD_GROUPS, 128, 1), 0) * SSD_HPG + r).astype(f32)
                 for r in range(SSD_HPG))
    core = jax.vmap(_ssd_core, in_axes=(0,) * 8 + (None,) * 5)
    return core(xs, bm, cm, z, nw, lanes, rows, h, dtp16, adt16, acum16, acum16.T, dsk)


def _ssd_specs(nc, rev):
    n = SSD_CHUNK

    def cidx(c):
        return (nc - 1 - c) if rev else c

    def col(base, w):
        return pl.BlockSpec((n, w), lambda c: (cidx(c), base // w))

    def halo(base, w):
        return pl.BlockSpec((8, w), lambda c: (jnp.maximum(cidx(c) * (n // 8) - 1, 0), base // w))

    def taps(base, w):
        return pl.BlockSpec((8, w), lambda c: (0, base // w))

    row = pl.BlockSpec((8, 128), lambda c: (0, 0))
    in_specs = [col(C_SX, 1024), col(C_SB, 512), col(C_SC, 512), halo(C_SX, 1024), halo(C_SB, 512), halo(C_SC, 512),
                col(C_SZ, 1024), col(C_SDT, 128), taps(0, 1024), taps(1024, 512), taps(1536, 512), row, row, row,
                taps(0, 1024)]
    y = pl.BlockSpec((n, D_MODEL), lambda c: (cidx(c), 0))
    st = pl.BlockSpec((1, SSD_GROUPS, SSD_N, 256), lambda c: (cidx(c), 0, 0, 0))
    return in_specs, y, st, col, taps, row


def _ssd_load(refs, first):
    xs, bm, cm, hx, hbm, hcm, z, dt, tx, tb, tc, dtb, alog, dsk, nw = refs

    def halo(r):
        return jnp.where(first, 0.0, r[...])

    def taps(r):
        return tuple(r[j:j + 1, :] for j in range(4))

    act = (xs[...], bm[...], cm[...], halo(hx), halo(hbm), halo(hcm), taps(tx), taps(tb), taps(tc),
           tx[4:5, :], tb[4:5, :], tc[4:5, :])
    return act, (z[...], nw[0:1, :]), (dt[...], dtb[0:1, :], alog[0:1, :], dsk[0:1, :])


def _groups(a, w):
    return jnp.stack([a[:, i * w:(i + 1) * w] for i in range(SSD_GROUPS)])


def ssd_fwd(u, conv_w8, dtb8, alog8, d8, nw8):
    t_rows = u.shape[0]
    nc = t_rows // SSD_CHUNK
    in_specs, y_spec, st_spec, *_ = _ssd_specs(nc, False)

    def body(*refs):
        ins, (y_ref, st_ref), (h_scr,) = refs[:15], refs[15:17], refs[17:]
        c = pl.program_id(0)

        @pl.when(c == 0)
        def _():
            h_scr[...] = jnp.zeros_like(h_scr)

        act, (z, nw), shared = _ssd_load(ins, c == 0)
        h = h_scr[...]
        st_ref[0] = h
        xs, bm, cm = _ssd_act(*act, row0=c * SSD_CHUNK)
        y, h_new = _ssd_chunk(_groups(xs, 256), _groups(bm, 128), _groups(cm, 128), _groups(z, 256),
                              _groups(nw, 256), h, *shared, row0=c * SSD_CHUNK)
        y_ref[...] = _wide(y).astype(bf16)
        h_scr[...] = h_new

    return pl.pallas_call(
        body, name="ssd_fwd", grid=(nc,), in_specs=in_specs, out_specs=(y_spec, st_spec),
        out_shape=(jax.ShapeDtypeStruct((t_rows, D_MODEL), bf16),
                   jax.ShapeDtypeStruct((nc, SSD_GROUPS, SSD_N, 256), f32)),
        scratch_shapes=[pltpu.VMEM((SSD_GROUPS, SSD_N, 256), f32)],
        compiler_params=_params(("arbitrary",)),
    )(u, u, u, u, u, u, u, u, conv_w8, conv_w8, conv_w8, dtb8, alog8, d8, nw8)


def ssd_bwd(u, conv_w8, dtb8, alog8, d8, nw8, states, dy, du):
    t_rows = u.shape[0]
    nc = t_rows // SSD_CHUNK
    n = SSD_CHUNK
    in_specs, y_spec, st_spec, col, taps, row = _ssd_specs(nc, True)

    def body(*refs):
        ins, st_ref, dy_ref = refs[:15], refs[15], refs[16]
        du_ref, ddt_ref, dtx_ref, dtb_ref, dtc_ref, ddtb_ref, dalog_ref, ddsk_ref, dnw_ref = refs[18:27]
        dh_scr, hx_scr, hb_scr, hc_scr = refs[27:]
        cc = pl.program_id(0)
        c = nc - 1 - cc

        @pl.when(cc == 0)
        def _():
            for r in (dh_scr, hx_scr, hb_scr, hc_scr, dtx_ref, dtb_ref, dtc_ref, dnw_ref, ddtb_ref, dalog_ref, ddsk_ref):
                r[...] = jnp.zeros_like(r)

        act, (z, nw), shared = _ssd_load(ins, c == 0)
        (xs, bm, cm), vjp_act = jax.vjp(functools.partial(_ssd_act, row0=c * n), *act)
        _, vjp_core = jax.vjp(functools.partial(_ssd_chunk, row0=c * n), _groups(xs, 256), _groups(bm, 128),
                              _groups(cm, 128), _groups(z, 256), _groups(nw, 256), st_ref[0], *shared)
        dxa, dba, dca, dz, dnw, dh, ddt, ddtb, dalog, ddsk = vjp_core(
            (_groups(dy_ref[...].astype(f32), 256), dh_scr[...]))
        dh_scr[...] = dh
        dxs, dbm, dcm, dhx, dhb, dhc, dtx, dtb, dtc, dbx, dbb, dbc = vjp_act((_wide(dxa), _wide(dba), _wide(dca)))
        du_ref[:, 0:D_MODEL] = _wide(dz).astype(bf16)
        for dx, dhalo, scr, lo in ((dxs, dhx, hx_scr, C_SX), (dbm, dhb, hb_scr, C_SB), (dcm, dhc, hc_scr, C_SC)):
            zeros = jnp.zeros((n - 8, dx.shape[1]), f32)
            du_ref[:, lo - C_SZ:lo - C_SZ + dx.shape[1]] = (dx + jnp.concatenate([zeros, scr[...]], axis=0)).astype(bf16)
            scr[...] = dhalo
        ddt_ref[...] = ddt
        for ref, dtaps, dbias in ((dtx_ref, dtx, dbx), (dtb_ref, dtb, dbb), (dtc_ref, dtc, dbc)):
            for j in range(4):
                ref[j:j + 1, :] += dtaps[j]
            ref[4:5, :] += dbias
        ddtb_ref[0:1, :] += ddtb
        dalog_ref[0:1, :] += dalog
        ddsk_ref[0:1, :] += ddsk
        dnw_ref[0:1, :] += _wide(dnw)

    def out_col(w):
        return pl.BlockSpec((n, w), lambda c: (nc - 1 - c, 0))

    out_specs = (pl.BlockSpec((n, 3 * D_MODEL), lambda c: (nc - 1 - c, C_SZ // (3 * D_MODEL))), out_col(128),
                 taps(0, D_MODEL), taps(0, 512), taps(0, 512), row, row, row, taps(0, D_MODEL))
    out_shape = (jax.ShapeDtypeStruct(du.shape, du.dtype),
                 jax.ShapeDtypeStruct((t_rows, 128), f32),
                 jax.ShapeDtypeStruct((8, D_MODEL), f32), jax.ShapeDtypeStruct((8, 512), f32),
                 jax.ShapeDtypeStruct((8, 512), f32),
                 jax.ShapeDtypeStruct((8, 128), f32), jax.ShapeDtypeStruct((8, 128), f32),
                 jax.ShapeDtypeStruct((8, 128), f32), jax.ShapeDtypeStruct((8, D_MODEL), f32))
    return pl.pallas_call(
        body, name="ssd_bwd", grid=(nc,), in_specs=in_specs + [st_spec, y_spec, ANY],
        out_specs=out_specs, out_shape=out_shape, input_output_aliases={17: 0},
        scratch_shapes=[pltpu.VMEM((SSD_GROUPS, SSD_N, 256), f32), pltpu.VMEM((8, D_MODEL), f32),
                        pltpu.VMEM((8, 512), f32), pltpu.VMEM((8, 512), f32)],
        compiler_params=_params(("arbitrary",)),
    )(u, u, u, u, u, u, u, u, conv_w8, conv_w8, conv_w8, dtb8, alog8, d8, nw8, states, dy, du)


NEG = -1e30


def _swa_core(q, kc, kp, km, vc, vp, vm, sink, *, n):
    rows = SWA_REP * SWA_W
    ri, ci = _iota((rows, SWA_W), 0) & (SWA_W - 1), _iota((rows, SWA_W), 1)
    causal = ci <= ri
    m_band = (causal & ((n >= 1) | ((ci >= PAD) & (ri >= PAD)))) | ((ci > ri) & (n >= 2))
    m_meta = (n >= 1) & (ci >= PAD)
    q = q * (SWA_D ** -0.5)
    s = jnp.where(m_band, jnp.where(causal, _dot(q, kc, NT), _dot(q, kp, NT)), NEG)
    sm = jnp.where(m_meta, _dot(q, km, NT), NEG)
    mx = jnp.maximum(jnp.maximum(jnp.max(s, axis=1, keepdims=True), jnp.max(sm, axis=1, keepdims=True)), sink)
    mx = lax.stop_gradient(mx)
    e, em = jnp.exp(s - mx), jnp.exp(sm - mx)
    den = jnp.sum(e, axis=1, keepdims=True) + jnp.sum(em, axis=1, keepdims=True) + jnp.exp(sink - mx)
    return (_dot(jnp.where(causal, e, 0.0), vc) + _dot(jnp.where(causal, 0.0, e), vp) + _dot(em, vm)) / den


def _swa_block(q16, kc, kp, km, vc, vp, vm, sink16, *, n):
    rows = SWA_REP * SWA_W
    lane = _iota((1, 128), 1)
    rep = _iota((rows, 1), 0) >> 7
    cols = []
    for h in range(SWA_KV_HEADS):
        col = jnp.zeros((rows, 1), f32)
        for r in range(SWA_REP):
            s = jnp.sum(jnp.where(lane == h * SWA_REP + r, sink16, 0.0), axis=1, keepdims=True)
            col = jnp.where(rep == r, s, col)
        cols.append(col)
    o = jax.vmap(functools.partial(_swa_core, n=n))(q16.reshape(SWA_KV_HEADS, rows, SWA_D), kc, kp, km, vc, vp, vm,
                                                    jnp.concatenate([col[None] for col in cols], axis=0))
    return o.reshape(q16.shape)


def _swa_specs(nb, rev):
    def bidx(n):
        return (nb - 1 - n) if rev else n

    kvw = SWA_KV_HEADS * SWA_D
    q = pl.BlockSpec((SWA_W, D_MODEL), lambda n: (bidx(n), C_WQ // D_MODEL))

    def kv(base, blk):
        return pl.BlockSpec((SWA_W, kvw), lambda n: (blk(bidx(n)), base // kvw))

    cur, prev, meta = (lambda n: n), (lambda n: jnp.maximum(n - 1, 0)), (lambda n: 0)
    row = pl.BlockSpec((8, 128), lambda n: (0, 0))
    in_specs = [q] + [kv(C_WK, b) for b in (cur, prev, meta)] + [kv(C_WV, b) for b in (cur, prev, meta)] + [row]
    return in_specs, pl.BlockSpec((SWA_W, D_MODEL), lambda n: (bidx(n), 0)), row


def _swa_heads(a):
    return jnp.stack([a[:, i * SWA_D:(i + 1) * SWA_D] for i in range(a.shape[1] // SWA_D)])


def swa_fwd(u, sink8):
    t_rows = u.shape[0]
    nb = t_rows // SWA_W
    in_specs, o_spec, _ = _swa_specs(nb, False)

    def body(q_ref, kc, kp, km, vc, vp, vm, sink_ref, o_ref):
        o = _swa_block(*[_swa_heads(r[...]) for r in (q_ref, kc, kp, km, vc, vp, vm)], sink_ref[0:1, :],
                       n=pl.program_id(0))
        o_ref[...] = _wide(o).astype(bf16)

    return pl.pallas_call(
        body, name="swa_fwd", grid=(nb,), in_specs=in_specs, out_specs=o_spec,
        out_shape=jax.ShapeDtypeStruct((t_rows, D_MODEL), bf16),
        compiler_params=_params(("arbitrary",)),
    )(u, u, u, u, u, u, u, sink8)


def swa_bwd(u, sink8, do, du):
    t_rows = u.shape[0]
    nb = t_rows // SWA_W
    in_specs, o_spec, row = _swa_specs(nb, True)
    width = C_BA - C_WQ

    def body(q_ref, kc, kp, km, vc, vp, vm, sink_ref, do_ref, _, du_ref, dsink_ref,
             dkp_scr, dvp_scr, dkm_scr, dvm_scr):
        nn = pl.program_id(0)
        n = nb - 1 - nn

        @pl.when(nn == 0)
        def _():
            for r in (dkp_scr, dvp_scr, dkm_scr, dvm_scr, dsink_ref):
                r[...] = jnp.zeros_like(r)

        fn = functools.partial(_swa_block, n=n)
        _, vjp = jax.vjp(fn, *[_swa_heads(r[...]) for r in (q_ref, kc, kp, km, vc, vp, vm)], sink_ref[0:1, :])
        dq, dkc, dkp, dkm, dvc, dvp, dvm, dsink = vjp(_swa_heads(do_ref[...]))
        dkm_scr[...] += dkm
        dvm_scr[...] += dvm
        first = n == 0
        dk = dkc + dkp_scr[...] + jnp.where(first, dkm_scr[...], 0.0)
        dv = dvc + dvp_scr[...] + jnp.where(first, dvm_scr[...], 0.0)
        du_ref[:, 0:D_MODEL] = _wide(dq).astype(bf16)
        du_ref[:, C_WK - C_WQ:C_WV - C_WQ] = _wide(dk).astype(bf16)
        du_ref[:, C_WV - C_WQ:width] = _wide(dv).astype(bf16)
        dkp_scr[...] = dkp
        dvp_scr[...] = dvp
        dsink_ref[0:1, :] += dsink

    return pl.pallas_call(
        body, name="swa_bwd", grid=(nb,), in_specs=in_specs + [o_spec, ANY],
        out_specs=(pl.BlockSpec((SWA_W, width), lambda n: (nb - 1 - n, C_WQ // width)), row),
        out_shape=(jax.ShapeDtypeStruct(du.shape, du.dtype), jax.ShapeDtypeStruct((8, 128), f32)),
        input_output_aliases={9: 0},
        scratch_shapes=[pltpu.VMEM((SWA_KV_HEADS, SWA_W, SWA_D), f32)] * 4,
        compiler_params=_params(("arbitrary",)),
    )(u, u, u, u, u, u, u, sink8, do, du)


def _tile(dim, prefs):
    for p in prefs:
        if dim % p == 0:
            return p
    return dim


def _row_tile(rows, d):
    for p in range(min(rows, BLOCK_BYTES // (4 * d)) // 8 * 8, 0, -8):
        if rows % p == 0:
            return p
    return rows


def mm(a, b, *, out_dtype, name, resid=None, relu_grad_of=None, relu2_out=False, ta=False, tb=False, norm_w8=None):
    assert resid is None or relu_grad_of is None
    k, m = (a.shape if ta else a.shape[::-1])
    n = b.shape[0] if tb else b.shape[1]
    rhs_stays = k * 2 * 1024 > MM_OPERAND_BYTES
    if rhs_stays:
        tn = _tile(n, tuple(p for p in (512, 256, 128) if p * k * 2 <= MM_RESIDENT_BYTES))
        tm = _tile(m, tuple(p for p in (512, 384, 256, 128) if p * k * 2 <= MM_OPERAND_BYTES))
        grid = (n // tn, m // tm)
        ij = lambda o, i: (i, o)
    elif k * n * b.dtype.itemsize <= MM_OPERAND_BYTES and (n <= 1024 or k > 1024):
        tn = n
        tm = _tile(m, tuple(p for p in (1408, 1024, 512, 384, 256, 128)
                            if p * k * 2 <= MM_OPERAND_BYTES and p * n * 4 <= MM_OPERAND_BYTES * 2 // 3))
        grid = (m // tm, 1)
        ij = lambda o, i: (o, i)
    else:
        tm = _tile(m, tuple(p for p in (1408, 1024, 512, 384, 256, 128) if p * k * 2 <= MM_OPERAND_BYTES))
        tn = _tile(n, tuple(p for p in (1024, 512, 256, 128) if p * k * 2 <= MM_OPERAND_BYTES // 2))
        grid = (m // tm, n // tn)
        ij = lambda o, i: (o, i)

    extra = resid if resid is not None else relu_grad_of
    staged = ta or norm_w8 is not None
    assert not (ta and norm_w8 is not None) and not (staged and rhs_stays)
    n_in = 2 + (extra is not None) + (norm_w8 is not None)
    n_out = 1 + relu2_out + (norm_w8 is not None)

    def body(*refs):
        ins, outs, scr = refs[:n_in], refs[n_in:n_in + n_out], refs[n_in + n_out:]
        a_ref, b_ref = ins[:2]
        if staged:
            @pl.when(pl.program_id(1) == 0)
            def _():
                if ta:
                    scr[0][...] = a_ref[...].T
                else:
                    hn = _rmsnorm(a_ref[...], ins[-1][0:1, :]).astype(bf16)
                    scr[0][...] = hn
                    outs[-1][...] = hn

            lhs = scr[0][...]
        else:
            lhs = a_ref[...]
        o = _dot(lhs, b_ref[...], NT if tb else NN)
        if resid is not None:
            o = o + ins[2][...]
        if relu_grad_of is not None:
            o = o * (2.0 * jnp.maximum(ins[2][...], 0.0))
        outs[0][...] = o.astype(out_dtype)
        if relu2_out:
            r = jnp.maximum(o, 0.0)
            outs[1][...] = (r * r).astype(bf16)

    in_specs = [pl.BlockSpec((k, tm), lambda o, i: (0, ij(o, i)[0])) if ta
                else pl.BlockSpec((tm, k), lambda o, i: (ij(o, i)[0], 0)),
                pl.BlockSpec((tn, k), lambda o, i: (ij(o, i)[1], 0)) if tb
                else pl.BlockSpec((k, tn), lambda o, i: (0, ij(o, i)[1]))]
    args = [a, b]
    if extra is not None:
        in_specs.append(pl.BlockSpec((tm, tn), ij))
        args.append(extra)
    out_blk = pl.BlockSpec((tm, tn), ij)
    out_specs = [out_blk] * (1 + relu2_out)
    out_shape = [jax.ShapeDtypeStruct((m, n), out_dtype)] + [jax.ShapeDtypeStruct((m, n), bf16)] * relu2_out
    if norm_w8 is not None:
        in_specs.append(pl.BlockSpec((8, k), lambda o, i: (0, 0)))
        args.append(norm_w8)
        out_specs.append(pl.BlockSpec((tm, k), lambda o, i: (ij(o, i)[0], 0)))
        out_shape.append(jax.ShapeDtypeStruct((m, k), bf16))
    res = pl.pallas_call(
        body, name=name, grid=grid, in_specs=in_specs, out_specs=tuple(out_specs), out_shape=tuple(out_shape),
        scratch_shapes=[pltpu.VMEM((tm, k), bf16)] if staged else [],
        compiler_params=_params(("parallel", "arbitrary" if staged else "parallel")),
    )(*args)
    return res[0] if len(res) == 1 else res


def _rows(t_rows):
    return _tile(t_rows, (384, 256, 128))


def _rmsnorm(h, w):
    return h * lax.rsqrt(jnp.mean(h * h, axis=1, keepdims=True) + RMS_EPS) * w


def rmsnorm_bwd(h, w8, dhn, dres, *, name):
    t_rows, d = h.shape
    tr = _rows(t_rows)

    def body(h_ref, w_ref, dhn_ref, dres_ref, dh_ref, dw_ref):
        @pl.when(pl.program_id(0) == 0)
        def _():
            dw_ref[...] = jnp.zeros_like(dw_ref)

        _, vjp = jax.vjp(_rmsnorm, h_ref[...], w_ref[0:1, :])
        dh, dw = vjp(dhn_ref[...])
        dh_ref[...] = dh + dres_ref[...]
        dw_ref[0:1, :] += dw

    blk = pl.BlockSpec((tr, d), lambda i: (i, 0))
    wblk = pl.BlockSpec((8, d), lambda i: (0, 0))
    return pl.pallas_call(
        body, name=name, grid=(t_rows // tr,), in_specs=[blk, wblk, blk, blk], out_specs=(blk, wblk),
        out_shape=(jax.ShapeDtypeStruct((t_rows, d), f32), jax.ShapeDtypeStruct((8, d), f32)),
        compiler_params=_params(("arbitrary",)),
    )(h, w8, dhn, dres)


def _merge(pg, ps, pw, la, lb, lc):
    return _sigmoid(la) * pg + _sigmoid(lb) * ps + _sigmoid(lc) * pw


def _merge_specs(t_rows):
    tr = _rows(t_rows)
    blk = pl.BlockSpec((tr, D_MODEL), lambda i: (i, 0))
    gate = [pl.BlockSpec((tr, D_MODEL), functools.partial(lambda i, j: (i, j), j=C_GATE // D_MODEL + j)) for j in range(3)]
    return tr, blk, gate


def merge_fwd(pg, ps, pw, u):
    t_rows = pg.shape[0]
    tr, blk, gate = _merge_specs(t_rows)

    def body(pg_ref, ps_ref, pw_ref, la, lb, lc, o_ref):
        o_ref[...] = _merge(pg_ref[...], ps_ref[...], pw_ref[...], la[...], lb[...], lc[...]).astype(bf16)

    return pl.pallas_call(
        body, name="merge_fwd", grid=(t_rows // tr,), in_specs=[blk, blk, blk] + gate, out_specs=blk,
        out_shape=jax.ShapeDtypeStruct((t_rows, D_MODEL), bf16), compiler_params=_params(("arbitrary",)),
    )(pg, ps, pw, u, u, u)


def merge_bwd(pg, ps, pw, u, dmerged, du):
    t_rows = pg.shape[0]
    tr, blk, gate = _merge_specs(t_rows)

    def body(pg_ref, ps_ref, pw_ref, la, lb, lc, dm_ref, _, dpg_ref, dps_ref, dpw_ref, dl_ref):
        _, vjp = jax.vjp(_merge, pg_ref[...], ps_ref[...], pw_ref[...], la[...], lb[...], lc[...])
        dpg, dps, dpw, dla, dlb, dlc = vjp(dm_ref[...])
        dpg_ref[...] = dpg.astype(bf16)
        dps_ref[...] = dps.astype(bf16)
        dpw_ref[...] = dpw.astype(bf16)
        for j, dl in enumerate((dla, dlb, dlc)):
            dl_ref[:, j * D_MODEL:(j + 1) * D_MODEL] = dl.astype(bf16)

    act = jax.ShapeDtypeStruct((t_rows, D_MODEL), bf16)
    return pl.pallas_call(
        body, name="merge_bwd", grid=(t_rows // tr,), in_specs=[blk, blk, blk] + gate + [blk, ANY],
        out_specs=(blk, blk, blk, pl.BlockSpec((tr, 3 * D_MODEL), lambda i: (i, C_GATE // (3 * D_MODEL)))),
        out_shape=(act, act, act, jax.ShapeDtypeStruct(du.shape, du.dtype)),
        input_output_aliases={7: 3},
        compiler_params=_params(("arbitrary",)),
    )(pg, ps, pw, u, u, u, dmerged, du)


def loss_head(h, w8, target):
    t_rows, d = h.shape
    tr = HEAD_ROWS

    def loss_fn(hb, w, tgt):
        err = _rmsnorm(hb, w) - tgt
        return 0.5 * jnp.sum(err * err) / d

    def body(h_ref, w_ref, t_ref, loss_ref, dh_ref, dw_ref):
        i = pl.program_id(0)

        @pl.when(i == 0)
        def _():
            loss_ref[...] = jnp.zeros_like(loss_ref)
            dw_ref[...] = jnp.zeros_like(dw_ref)
            dh_ref[...] = jnp.zeros_like(dh_ref)

        @pl.when(i > 0)
        def _():
            val, (dh, dw) = jax.value_and_grad(loss_fn, argnums=(0, 1))(h_ref[...], w_ref[0:1, :], t_ref[...])
            loss_ref[...] += val
            dh_ref[...] = dh
            dw_ref[0:1, :] += dw

    blk = pl.BlockSpec((tr, d), lambda i: (i, 0))
    wblk = pl.BlockSpec((8, d), lambda i: (0, 0))
    return pl.pallas_call(
        body, name="loss_head", grid=(t_rows // tr,),
        in_specs=[blk, wblk, pl.BlockSpec((tr, d), lambda i: (jnp.maximum(i - 1, 0), 0))],
        out_specs=(pl.BlockSpec((8, 128), lambda i: (0, 0)), blk, wblk),
        out_shape=(jax.ShapeDtypeStruct((8, 128), f32), jax.ShapeDtypeStruct((t_rows, d), f32),
                   jax.ShapeDtypeStruct((8, d), f32)),
        compiler_params=_params(("arbitrary",)),
    )(h, w8, target)


def adamw(w, m, v, partials, row_off, *, name):
    rows, d = w.shape
    layers = len(partials)
    per = rows // layers
    tr = _row_tile(per, d)
    assert row_off % tr == 0
    off, nblk = row_off // tr, per // tr
    c1 = 1.0 - ADAM_B1 ** ADAM_STEP
    c2 = 1.0 - ADAM_B2 ** ADAM_STEP

    def body(w_ref, m_ref, v_ref, *refs):
        p_refs, (g_ref, d_ref, mo_ref, vo_ref) = refs[:2 * layers], refs[2 * layers:]
        g = p_refs[0][...] + p_refs[1][...]
        for l in range(1, layers):
            g = jnp.where(pl.program_id(0) >= l * nblk, p_refs[2 * l][...] + p_refs[2 * l + 1][...], g)
        m_new = ADAM_B1 * m_ref[...] + (1.0 - ADAM_B1) * g
        v_new = ADAM_B2 * v_ref[...] + (1.0 - ADAM_B2) * (g * g)
        g_ref[...] = g
        d_ref[...] = -ADAM_LR * ((m_new / c1) / (jnp.sqrt(v_new / c2) + ADAM_EPS) + ADAM_WD * w_ref[...])
        mo_ref[...] = m_new
        vo_ref[...] = v_new

    blk = pl.BlockSpec((tr, d), lambda i: (i, 0))
    pblks = [pl.BlockSpec((tr, d), functools.partial(lambda i, l: (off + jnp.clip(i - l * nblk, 0, nblk - 1), 0), l=l))
             for l in range(layers) for _ in range(2)]
    out = jax.ShapeDtypeStruct((rows, d), f32)
    return pl.pallas_call(
        body, name=name, grid=(rows // tr,), in_specs=[blk, blk, blk] + pblks, out_specs=(blk,) * 4,
        out_shape=(out,) * 4, compiler_params=_params(("arbitrary",)),
    )(w, m, v, *[p for pair in partials for p in pair])


def reduce4(parts, *, name, own=None, me=None):
    _, rows, d = parts.shape
    tr = _row_tile(rows, d)

    def body(*refs):
        p_ref, o_ref = refs[0], refs[-1]
        acc = None
        for s in range(4):
            term = p_ref[s].astype(f32)
            if own is not None:
                term = jnp.where(refs[2][0] == s, refs[1][...].astype(f32), term)
            acc = term if acc is None else acc + term
        o_ref[...] = acc

    in_specs = [pl.BlockSpec((4, tr, d), lambda i: (0, i, 0))]
    args = [parts]
    if own is not None:
        in_specs += [pl.BlockSpec((tr, d), lambda i: (i, 0)), pl.BlockSpec(memory_space=pltpu.SMEM)]
        args += [own, me]
    return pl.pallas_call(
        body, name=name, grid=(rows // tr,), in_specs=in_specs,
        out_specs=pl.BlockSpec((tr, d), lambda i: (i, 0)), out_shape=jax.ShapeDtypeStruct((rows, d), f32),
        compiler_params=_params(("arbitrary",)),
    )(*args)


ANY = pl.BlockSpec(memory_space=pl.ANY)
MESH = pl.DeviceIdType.MESH
CHIP_FLIPS = ((0, 1), (1, 0), (1, 1))


def chip_exchange(bufs, scatter, *, name, after=None):
    nb = len(bufs)
    extra = [] if after is None else [after]

    def body(*refs):
        ins, outs = refs[:nb], refs[nb + len(extra):2 * nb + len(extra)]
        send_sems, recv_sems, local_sems = refs[2 * nb + len(extra):]
        x, y, c = lax.axis_index("x"), lax.axis_index("y"), lax.axis_index("c")
        me = 2 * x + y
        local = [pltpu.make_async_copy(ins[j].at[me] if scatter[j] else ins[j], outs[j].at[me], local_sems.at[j])
                 for j in range(nb)]
        for cp in local:
            cp.start()
        sends, recvs = [], []
        for k, (fx, fy) in enumerate(CHIP_FLIPS):
            px = 1 - x if fx else x
            py = 1 - y if fy else y
            chip = 2 * px + py
            for j in range(nb):
                src = ins[j].at[chip] if scatter[j] else ins[j]
                sems = dict(send_sem=send_sems.at[nb * k + j], recv_sem=recv_sems.at[nb * k + j],
                            device_id=(px, py, c), device_id_type=MESH)
                sends.append(pltpu.make_async_remote_copy(src_ref=src, dst_ref=outs[j].at[me], **sems))
                recvs.append(pltpu.make_async_remote_copy(src_ref=src, dst_ref=outs[j].at[chip], **sems))
        for cp in sends:
            cp.start()
        for cp in recvs:
            cp.wait_recv()
        for cp in sends:
            cp.wait_send()
        for cp in local:
            cp.wait()

    out_shape = tuple(jax.ShapeDtypeStruct(b.shape if s else (4,) + b.shape, b.dtype) for b, s in zip(bufs, scatter))
    return pl.pallas_call(
        body, name=name, in_specs=[ANY] * (nb + len(extra)), out_specs=(ANY,) * nb, out_shape=out_shape,
        scratch_shapes=[pltpu.SemaphoreType.DMA((3 * nb,)), pltpu.SemaphoreType.DMA((3 * nb,)),
                        pltpu.SemaphoreType.DMA((nb,))],
        compiler_params=pltpu.CompilerParams(has_side_effects=True),
    )(*bufs, *extra)


def gather_two_level(big, small, *, name):
    half = big.shape[1] // 2

    def body(big_ref, small_ref, obig_ref, osmall_ref, send_sems, recv_sems, local_sems):
        x, y, c = lax.axis_index("x"), lax.axis_index("y"), lax.axis_index("c")
        me = 2 * x + y
        mine = (slice(None), pl.ds(pl.multiple_of(c * half, half), half))
        theirs = (slice(None), pl.ds(pl.multiple_of((1 - c) * half, half), half))
        local = [pltpu.make_async_copy(big_ref, obig_ref.at[me], local_sems.at[0]),
                 pltpu.make_async_copy(small_ref, osmall_ref.at[me], local_sems.at[1])]
        for cp in local:
            cp.start()

        def copy(k, src, dst, to):
            return pltpu.make_async_remote_copy(src_ref=src, dst_ref=dst, send_sem=send_sems.at[k],
                                                recv_sem=recv_sems.at[k], device_id=to, device_id_type=MESH)

        sends, landed, passed, small_in = [], [], [], []
        for k, (fx, fy) in enumerate(CHIP_FLIPS):
            px = 1 - x if fx else x
            py = 1 - y if fy else y
            chip = 2 * px + py
            sends.append(copy(k, big_ref.at[mine], obig_ref.at[(me,) + mine], (px, py, c)))
            landed.append(copy(k, big_ref.at[mine], obig_ref.at[(chip,) + mine], (px, py, c)))
            sends.append(copy(3 + k, small_ref, osmall_ref.at[me], (px, py, c)))
            small_in.append(copy(3 + k, small_ref, osmall_ref.at[chip], (px, py, c)))
            passed.append((copy(6 + k, obig_ref.at[(chip,) + mine], obig_ref.at[(chip,) + mine], (x, y, 1 - c)),
                           copy(6 + k, obig_ref.at[(chip,) + theirs], obig_ref.at[(chip,) + theirs], (x, y, 1 - c))))
        for cp in sends:
            cp.start()
        for k in range(3):
            landed[k].wait_recv()
            passed[k][0].start()
        for k in range(3):
            passed[k][1].wait_recv()
            small_in[k].wait_recv()
        for cp in sends + [p[0] for p in passed]:
            cp.wait_send()
        for cp in local:
            cp.wait()

    return pl.pallas_call(
        body, name=name, in_specs=[ANY, ANY], out_specs=(ANY, ANY),
        out_shape=(jax.ShapeDtypeStruct((4,) + big.shape, big.dtype),
                   jax.ShapeDtypeStruct((4,) + small.shape, small.dtype)),
        scratch_shapes=[pltpu.SemaphoreType.DMA((9,)), pltpu.SemaphoreType.DMA((9,)), pltpu.SemaphoreType.DMA((2,))],
        compiler_params=pltpu.CompilerParams(has_side_effects=True),
    )(big, small)


def sibling_swap(bufs, *, name):
    nb = len(bufs)

    def body(*refs):
        ins, outs, (send_sems, recv_sems) = refs[:nb], refs[nb:2 * nb], refs[2 * nb:]
        peer = (lax.axis_index("x"), lax.axis_index("y"), 1 - lax.axis_index("c"))
        copies = [pltpu.make_async_remote_copy(src_ref=ins[j], dst_ref=outs[j], send_sem=send_sems.at[j],
                                               recv_sem=recv_sems.at[j], device_id=peer, device_id_type=MESH)
                  for j in range(nb)]
        for cp in copies:
            cp.start()
        for cp in copies:
            cp.wait_recv()
        for cp in copies:
            cp.wait_send()

    return pl.pallas_call(
        body, name=name, in_specs=[ANY] * nb, out_specs=(ANY,) * nb,
        out_shape=tuple(jax.ShapeDtypeStruct(b.shape, b.dtype) for b in bufs),
        scratch_shapes=[pltpu.SemaphoreType.DMA((nb,)), pltpu.SemaphoreType.DMA((nb,))],
        compiler_params=pltpu.CompilerParams(has_side_effects=True),
    )(*bufs)


HBM = pl.BlockSpec(memory_space=pltpu.HBM)
SEM = pl.BlockSpec(memory_space=pltpu.SEMAPHORE)
DATAFLOW = pltpu.SideEffectType.DATAFLOW_SIDE_EFFECTING


def _exchange_copies(srcs, lands, send_sems, recv_sems, scatter):
    x, y, c = lax.axis_index("x"), lax.axis_index("y"), lax.axis_index("c")
    me = 2 * x + y
    nb = len(srcs)
    pairs = []
    for k, (fx, fy) in enumerate(CHIP_FLIPS):
        px = 1 - x if fx else x
        py = 1 - y if fy else y
        chip = 2 * px + py
        for j in range(nb):
            src = srcs[j].at[chip] if scatter[j] else srcs[j]
            sems = dict(send_sem=send_sems.at[nb * k + j], recv_sem=recv_sems.at[nb * k + j],
                        device_id=(px, py, c), device_id_type=MESH)
            pairs.append((pltpu.make_async_remote_copy(src_ref=src, dst_ref=lands[j].at[me], **sems),
                          pltpu.make_async_remote_copy(src_ref=src, dst_ref=lands[j].at[chip], **sems)))
    return pairs


def exchange_start(bufs, scatter, after, *, name):
    nb = len(bufs)
    slabs = [b.shape[1:] if s else b.shape for b, s in zip(bufs, scatter)]
    lands = [lax.empty((4,) + shp, b.dtype) for b, shp in zip(bufs, slabs)]

    def body(*refs):
        srcs, zones = refs[:nb], refs[nb:2 * nb]
        send_sems, recv_sems = refs[2 * nb + 1:2 * nb + 3]
        token = refs[-1]
        for send, _ in _exchange_copies(srcs, zones, send_sems, recv_sems, scatter):
            send.start()
        token[...] = jnp.zeros_like(token)

    hbm = lambda a: pltpu.with_memory_space_constraint(a, pltpu.HBM)
    out = pl.pallas_call(
        body, name=name, in_specs=[HBM] * (2 * nb) + [ANY],
        out_specs=(SEM, SEM) + (HBM,) * (2 * nb) + (pl.BlockSpec(memory_space=pltpu.VMEM),),
        out_shape=(pltpu.SemaphoreType.DMA((3 * nb,)), pltpu.SemaphoreType.DMA((3 * nb,)))
        + tuple(pltpu.HBM(a.shape, a.dtype) for a in list(bufs) + lands) + (jax.ShapeDtypeStruct((8, 128), f32),),
        input_output_aliases={i: 2 + i for i in range(2 * nb)},
        compiler_params=pltpu.CompilerParams(has_side_effects=DATAFLOW),
    )(*[hbm(a) for a in list(bufs) + lands], after)
    return (out[:2], out[2:2 + nb], out[2 + nb:2 + 2 * nb], scatter), out[-1]


def exchange_wait(state, after, *, name):
    (send_sems, recv_sems), srcs, lands, scatter = state
    nb = len(srcs)

    def body(*refs):
        src_refs, zones = refs[:nb], refs[nb:2 * nb]
        s_sems, r_sems = refs[2 * nb:2 * nb + 2]
        for send, recv in _exchange_copies(src_refs, zones, s_sems, r_sems, scatter):
            send.wait_send()
            recv.wait_recv()

    out = pl.pallas_call(
        body, name=name, in_specs=[HBM] * (2 * nb) + [SEM, SEM, ANY], out_specs=(HBM,) * (2 * nb),
        out_shape=tuple(pltpu.HBM(a.shape, a.dtype) for a in list(srcs) + list(lands)),
        input_output_aliases={i: i for i in range(2 * nb)},
        compiler_params=pltpu.CompilerParams(has_side_effects=DATAFLOW),
    )(*srcs, *lands, send_sems, recv_sems, after)
    return out[nb:]


BIG = (
    ("w_proj_gdn", 256), ("w_proj_ssd", 256), ("w_proj_swa", 256), ("w_out", 256), ("w_up", 1024), ("w_down", 1024))
BIG_OFF = {}
_o = 0
for _n, _r in BIG:
    BIG_OFF[_n] = _o
    _o += _r
BIG_ROWS = _o
W_IN_SHARD = IN_W // 4

W_NAMES = ('meta_tokens', 'norm1_w', 'w_in', 'gdn_conv_w', 'gdn_a_log', 'gdn_dt_bias', 'gdn_norm_w', 'ssd_conv_w',
           'ssd_conv_b', 'ssd_dt_bias', 'ssd_a_log', 'ssd_d', 'ssd_norm_w', 'swa_sinks', 'w_proj_gdn', 'w_proj_ssd',
           'w_proj_swa', 'w_out', 'norm2_w', 'w_up', 'w_down', 'final_norm_w')
SMALL_NAMES = tuple(n for n in W_NAMES if n not in BIG_OFF and n != "w_in")
SMALL_SHARDED = ("meta_tokens", "gdn_conv_w", "ssd_conv_w")


def _pack_rows(parts, dtype):
    flat = jnp.concatenate([p.reshape(-1).astype(dtype) for p in parts])
    n = -(-flat.shape[0] // 8192) * 8192
    return jnp.pad(flat, (0, n - flat.shape[0])).reshape(-1, D_MODEL)


def _unpack_rows(packed, shapes):
    flat, out, o = packed.reshape(-1), [], 0
    for s in shapes:
        n = 1
        for d in s:
            n *= d
        out.append(flat[o:o + n].reshape(s))
        o += n
    return out


def _split_chips(full, axis):
    s = full.shape
    a = full.reshape(s[:axis] + (4, s[axis] // 4) + s[axis + 1:])
    return jnp.moveaxis(a, axis, 0)


def _join_chips(parts, axis):
    a = jnp.moveaxis(parts, 0, axis)
    s = a.shape
    return a.reshape(s[:axis] + (s[axis] * s[axis + 1],) + s[axis + 2:])


BIG_AXIS = {"w_proj_gdn": 1, "w_proj_ssd": 1, "w_proj_swa": 1, "w_out": 1, "w_up": 2, "w_down": 1}


def _w_in_to_padded(w):
    z = lambda n: jnp.zeros((n,) + w.shape[1:], w.dtype)
    return jnp.concatenate([w[8736:11808], w[4112:7184], w[7200:8736], w[4096:4112], z(112),
                            w[7184:7200], z(112 + C_MID_END - C_SDT - 128), w[0:4096]], axis=0)


def _w_in_from_padded(p):
    return jnp.concatenate([p[C_GQ:IN_WP], p[C_BA:C_BA + 16], p[C_SZ:C_WQ], p[C_SDT:C_SDT + 16], p[C_WQ:C_BA],
                            p[0:C_SZ]], axis=0)


def _row8(v, lane0=0, width=128):
    return jnp.pad(v[None, :], ((0, 7), (lane0, width - lane0 - v.shape[0])))


def _layer_fwd(h, p, l, late=None):
    tag = f"l{l}"
    u, hn = mm(h, p["w_in_t"], tb=True, out_dtype=f32, norm_w8=p["n1"], name=f"mm_in_{tag}")
    yg, stg, tg = gdn_fwd(u, p["gcw"], p["galog"], p["gdtb"], p["gnw"])
    ys, sts = ssd_fwd(u, p["scw"], p["sdtb"], p["salog"], p["sd"], p["snw"])
    yw = swa_fwd(u, p["sink"])
    if late is not None:
        p.update(late(yw))
    pg = mm(yg, p["wpg"], out_dtype=f32, name=f"mm_pg_{tag}")
    ps = mm(ys, p["wps"], out_dtype=f32, name=f"mm_ps_{tag}")
    pw = mm(yw, p["wpw"], out_dtype=f32, name=f"mm_pw_{tag}")
    merged = merge_fwd(pg, ps, pw, u)
    h2 = mm(merged, p["wout"], out_dtype=f32, resid=h, name=f"mm_out_{tag}")
    a, r, hn2 = mm(h2, p["wup"], out_dtype=f32, relu2_out=True, norm_w8=p["n2"], name=f"mm_up_{tag}")
    h3 = mm(r, p["wdown"], out_dtype=f32, resid=h2, name=f"mm_down_{tag}")
    saved = dict(h=h, hn=hn, u=u, yg=yg, stg=stg, tg=tg, ys=ys, sts=sts, yw=yw, pg=pg, ps=ps, pw=pw,
                 merged=merged, h2=h2, hn2=hn2, a=a, r=r)
    return h3, saved


def _layer_bwd(dh3, p, s, l, send_big, send_w_in):
    tag = f"l{l}"
    g = {}

    def wgrad(act, d, name):
        return mm(act, d, ta=True, out_dtype=bf16, name=f"wg_{name}_{tag}")

    da = mm(dh3, p["wdown"], tb=True, out_dtype=bf16, relu_grad_of=s["a"], name=f"dg_down_{tag}")
    g["w_down"] = wgrad(s["r"], dh3, "down")
    dhn2 = mm(da, p["wup"], tb=True, out_dtype=f32, name=f"dg_up_{tag}")
    g["w_up"] = wgrad(s["hn2"], da, "up")
    dh2, g["norm2_w"] = rmsnorm_bwd(s["h2"], p["n2"], dhn2, dh3, name=f"norm2_bwd_{tag}")
    dmerged = mm(dh2, p["wout"], tb=True, out_dtype=f32, name=f"dg_out_{tag}")
    g["w_out"] = wgrad(s["merged"], dh2, "out")
    du = lax.empty((dh3.shape[0], IN_WP), bf16)
    dpg, dps, dpw, du = merge_bwd(s["pg"], s["ps"], s["pw"], s["u"], dmerged, du)
    dyg = mm(dpg, p["wpg"], tb=True, out_dtype=f32, name=f"dg_pg_{tag}")
    dys = mm(dps, p["wps"], tb=True, out_dtype=f32, name=f"dg_ps_{tag}")
    dyw = mm(dpw, p["wpw"], tb=True, out_dtype=f32, name=f"dg_pw_{tag}")
    g["w_proj_gdn"] = wgrad(s["yg"], dpg, "pg")
    g["w_proj_ssd"] = wgrad(s["ys"], dps, "ps")
    g["w_proj_swa"] = wgrad(s["yw"], dpw, "pw")
    sent = send_big(jnp.concatenate([_split_chips(g.pop(n), BIG_AXIS[n] - 1).reshape(4, r, D_MODEL)
                                     for n, r in BIG], axis=1))

    (du, dba, dtq, dtk, dtv, g["gdn_a_log"], g["gdn_dt_bias"], g["gdn_norm_w"]) = gdn_bwd(
        s["u"], p["gcw"] + sent, p["galog"], p["gdtb"], p["gnw"], s["stg"], s["tg"], dyg, du)
    g["gdn_conv_w"] = jnp.concatenate([dtq, dtk, dtv], axis=1)[:4]
    (du, ddt, dtx, dtb, dtc, g["ssd_dt_bias"], g["ssd_a_log"], g["ssd_d"], g["ssd_norm_w"]) = ssd_bwd(
        s["u"], p["scw"], p["sdtb"], p["salog"], p["sd"], p["snw"], s["sts"], dys, du)
    dconv = jnp.concatenate([dtx, dtb, dtc], axis=1)
    g["ssd_conv_w"], g["ssd_conv_b"] = dconv[:4], dconv[4]
    du, g["swa_sinks"] = swa_bwd(s["u"], p["sink"], dyw, du)
    mid = jnp.concatenate([dba[0].astype(bf16), ddt.astype(bf16),
                           jnp.zeros((du.shape[0], C_MID_END - C_SDT - 128), bf16)], axis=1)
    du = lax.dynamic_update_slice(du, mid, (0, C_BA))
    sent = send_w_in(_w_in_from_padded(wgrad(du, s["hn"], "in")).reshape(4, W_IN_SHARD, D_MODEL))
    dhn = mm(du, p["w_in_t"], out_dtype=f32, name=f"dg_in_{tag}")
    dh, g["norm1_w"] = rmsnorm_bwd(s["h"], p["n1"] + sent, dhn, dh2, name=f"norm1_bwd_{tag}")
    return dh, g


def kernel(x, meta_tokens, norm1_w, w_in, gdn_conv_w, gdn_a_log, gdn_dt_bias, gdn_norm_w, ssd_conv_w, ssd_conv_b, ssd_dt_bias, ssd_a_log, ssd_d, ssd_norm_w, swa_sinks, w_proj_gdn, w_proj_ssd, w_proj_swa, w_out, norm2_w, w_up, w_down, final_norm_w, loss_target, m_meta_tokens, m_norm1_w, m_w_in, m_gdn_conv_w, m_gdn_a_log, m_gdn_dt_bias, m_gdn_norm_w, m_ssd_conv_w, m_ssd_conv_b, m_ssd_dt_bias, m_ssd_a_log, m_ssd_d, m_ssd_norm_w, m_swa_sinks, m_w_proj_gdn, m_w_proj_ssd, m_w_proj_swa, m_w_out, m_norm2_w, m_w_up, m_w_down, m_final_norm_w, v_meta_tokens, v_norm1_w, v_w_in, v_gdn_conv_w, v_gdn_a_log, v_gdn_dt_bias, v_gdn_norm_w, v_ssd_conv_w, v_ssd_conv_b, v_ssd_dt_bias, v_ssd_a_log, v_ssd_d, v_ssd_norm_w, v_swa_sinks, v_w_proj_gdn, v_w_proj_ssd, v_w_proj_swa, v_w_out, v_norm2_w, v_w_up, v_w_down, v_final_norm_w):
    given = dict(locals())
    depth = norm1_w.shape[0]
    me = 2 * lax.axis_index("x") + lax.axis_index("y")

    me1 = jnp.reshape(me, (1,)).astype(jnp.int32)
    w_in_t = jnp.swapaxes(w_in, 1, 2)

    def weight_slabs(l):
        return (w_in_t[l].astype(bf16),
                jnp.concatenate([given[n][l].reshape(-1, D_MODEL).astype(bf16) for n, _ in BIG]))

    slabs = [weight_slabs(l) for l in range(depth)]
    wsmall = _pack_rows([given[n] for n in SMALL_SHARDED], f32)
    ga0, gsmall = gather_two_level(slabs[0][0], wsmall, name="gather_first")
    gathers, started = {}, jnp.zeros((), f32)
    for l in range(depth):
        for j in range(2):
            if (l, j) != (0, 0):
                gathers[l, j], token = exchange_start([slabs[l][j]], (False,), gsmall, name=f"gather_start_l{l}_{j}")
                started = started + token[0, 0]
    shard_shapes = [given[n].shape for n in SMALL_SHARDED]
    per_chip = [_unpack_rows(gsmall[s], shard_shapes) for s in range(4)]
    full = {n: jnp.concatenate([per_chip[s][i] for s in range(4)], axis=-1) for i, n in enumerate(SMALL_SHARDED)}

    def landed(l, j, after):
        (zone,) = exchange_wait(gathers[l, j], after, name=f"gather_wait_l{l}_{j}")
        return lax.dynamic_update_slice(zone, slabs[l][j][None], (me, 0, 0))

    def first_operands(l, ga, order):
        return dict(
            n1=_row8(norm1_w[l], width=D_MODEL) + order, n2=_row8(norm2_w[l], width=D_MODEL),
            w_in_t=_w_in_to_padded(ga.reshape(IN_W, D_MODEL)),
            gcw=jnp.pad(full["gdn_conv_w"][l], ((0, 4), (0, 0))),
            galog=_row8(gdn_a_log[l], 8), gdtb=_row8(gdn_dt_bias[l], 8), gnw=_row8(gdn_norm_w[l]),
            scw=jnp.pad(jnp.concatenate([full["ssd_conv_w"][l], ssd_conv_b[l][None]], axis=0), ((0, 3), (0, 0))),
            sdtb=_row8(ssd_dt_bias[l]), salog=_row8(ssd_a_log[l]), sd=_row8(ssd_d[l]),
            snw=_row8(ssd_norm_w[l], width=D_MODEL), sink=_row8(swa_sinks[l]))

    def late_operands(l, after):
        gb = landed(l, 1, after)
        w = {}
        for n, r in BIG:
            parts = gb[:, BIG_OFF[n]:BIG_OFF[n] + r].reshape((4,) + given[n].shape[1:])
            w[n] = _join_chips(parts, BIG_AXIS[n] - 1)
        return dict(wpg=w["w_proj_gdn"], wps=w["w_proj_ssd"], wpw=w["w_proj_swa"], wout=w["w_out"],
                    wup=w["w_up"], wdown=w["w_down"])

    h = jnp.concatenate([jnp.zeros((PAD, D_MODEL), f32), full["meta_tokens"], x[0]], axis=0)
    layers, saved = [], []
    for l in range(depth):
        p = first_operands(0, ga0, started) if l == 0 else first_operands(l, landed(l, 0, h), 0.0)
        h, s = _layer_fwd(h, p, l, late=functools.partial(late_operands, l))
        layers.append(p)
        saved.append(s)
    loss8, dh, dfw8 = loss_head(h, _row8(final_norm_w, width=D_MODEL), loss_target[0])
    grads = {"final_norm_w": dfw8[0]}
    per_layer, grad_slabs, scatters = [None] * depth, {}, {}

    def send(l, j, slab):
        grad_slabs[l, j] = slab
        scatters[l, j], token = exchange_start([slab], (True,), loss8, name=f"scatter_start_l{l}_{j}")
        return token[0, 0]

    for l in reversed(range(depth)):
        dh, per_layer[l] = _layer_bwd(dh, layers[l], saved[l], l, functools.partial(send, l, 1),
                                      functools.partial(send, l, 0))
    grad_x = dh[HEAD_ROWS:][None]
    grads["meta_tokens"] = dh[PAD:HEAD_ROWS]
    lane = {"gdn_a_log": (8, 8), "gdn_dt_bias": (8, 8), "gdn_norm_w": (0, 128), "ssd_dt_bias": (0, 16),
            "ssd_a_log": (0, 16), "ssd_d": (0, 16), "swa_sinks": (0, 16)}
    for n in per_layer[0]:
        parts = [per_layer[l][n] for l in range(depth)]
        if n in lane:
            parts = [q[0, lane[n][0]:lane[n][0] + lane[n][1]] for q in parts]
        elif n in ("norm1_w", "norm2_w", "ssd_norm_w"):
            parts = [q[0] for q in parts]
        grads[n] = jnp.stack(parts)
    loss = lax.psum(loss8[0, 0], ("x", "y", "c"))

    gs = _pack_rows([grads[n] for n in SMALL_NAMES], f32)
    def chip_sum(l, j, after):
        (zone,) = exchange_wait(scatters[l, j], after, name=f"scatter_wait_l{l}_{j}")
        own = lax.dynamic_index_in_dim(grad_slabs[l, j], me, 0, keepdims=False)
        return reduce4(zone, own=own, me=me1, name=f"sum_chips_l{l}_{j}")

    early = [(l, j) for l in range(depth) for j in range(2) if (l, j) != (0, 0)]
    mine = {lj: chip_sum(*lj, dh) for lj in early}
    sibs = dict(zip(early, sibling_swap([mine[lj] for lj in early], name="swap_cores_early")))
    out = {}
    for n, r in BIG:
        shp = given[n].shape
        res = adamw(*[given[pre + n].reshape(depth * r, D_MODEL) for pre in ("", "m_", "v_")],
                    [(mine[l, 1], sibs[l, 1]) for l in range(depth)], BIG_OFF[n], name=f"adamw_{n}")
        out[n] = [a.reshape(shp) for a in res]
    mine[0, 0] = chip_sum(0, 0, res[1])
    (rs,) = chip_exchange([gs], (False,), after=mine[0, 0], name="gather_small_grads")
    ps_ = reduce4(rs, name="sum_chips_small")
    sibs[0, 0], ss = sibling_swap([mine[0, 0], ps_], name="swap_cores_last")
    res = adamw(*[jnp.swapaxes(given[pre + "w_in"], 1, 2).reshape(depth * W_IN_SHARD, D_MODEL)
                  for pre in ("", "m_", "v_")],
                [(mine[l, 0], sibs[l, 0]) for l in range(depth)], 0, name="adamw_w_in")
    out["w_in"] = [jnp.swapaxes(a.reshape(w_in_t.shape), 1, 2) for a in res]
    full_shapes = [grads[n].shape for n in SMALL_NAMES]
    mine_s, sib_s = _unpack_rows(ps_, full_shapes), _unpack_rows(ss, full_shapes)

    def local(parts):
        loc = []
        for n, a in zip(SMALL_NAMES, parts):
            if n in SMALL_SHARDED:
                sz = a.shape[-1] // 4
                a = lax.dynamic_slice_in_dim(a, me * sz, sz, axis=a.ndim - 1)
            loc.append(a)
        return _pack_rows(loc, f32)

    res = adamw(_pack_rows([given[n] for n in SMALL_NAMES], f32), _pack_rows([given["m_" + n] for n in SMALL_NAMES], f32),
                _pack_rows([given["v_" + n] for n in SMALL_NAMES], f32), [(local(mine_s), local(sib_s))], 0,
                name="adamw_small")
    local_shapes = [given[n].shape for n in SMALL_NAMES]
    unpacked = [_unpack_rows(a, local_shapes) for a in res]
    for i, n in enumerate(SMALL_NAMES):
        out[n] = [unpacked[j][i] for j in range(4)]

    return (loss, grad_x) + tuple(out[n][j] for j in range(4) for n in W_NAMES)
```

```python
import functools

import jax
import jax.numpy as jnp
from jax import lax
from jax.experimental import pallas as pl
from jax.experimental.pallas import tpu as pltpu

f32 = jnp.float32
bf16 = jnp.bfloat16

D_MODEL = 1024
N_META = 16
PAD = 112
HEAD_ROWS = PAD + N_META
RMS_EPS = 1e-6
L2_EPS = 1e-6
D_FF = 4 * D_MODEL

GDN_HEADS = 8
GDN_D = 128
GDN_CHUNK = 64
SSD_HEADS = 16
SSD_P = 64
SSD_GROUPS = 4
SSD_HPG = 4
SSD_N = 128
SSD_CHUNK = 128
SWA_Q_HEADS = 16
SWA_KV_HEADS = 4
SWA_REP = 4
SWA_D = 64
SWA_W = 128

C_GATE = 0
C_SZ, C_SX, C_SB, C_SC = 3072, 4096, 5120, 5632
C_WQ, C_WK, C_WV = 6144, 7168, 7424
C_BA = 7680
C_SDT = 7808
C_MID_END = 8192
C_GQ, C_GK, C_GV, C_GG = 8192, 9216, 10240, 11264
IN_WP = 12288
IN_W = 11808

ADAM_LR, ADAM_B1, ADAM_B2, ADAM_EPS, ADAM_WD, ADAM_STEP = 0.001, 0.9, 0.999, 1e-08, 0.01, 10

VMEM_LIMIT = 56 * 1024 * 1024
BLOCK_BYTES = 3 << 19
MM_OPERAND_BYTES = 9 << 20
MM_RESIDENT_BYTES = 13 << 20

NN = (((1,), (0,)), ((), ()))
NT = (((1,), (1,)), ((), ()))
TN = (((0,), (0,)), ((), ()))


def _dot(a, b, dims=NN):
    return lax.dot_general(a.astype(bf16), b.astype(bf16), dims, preferred_element_type=f32)


def _dotx(a, b, dims=NN):
    return lax.dot_general(a, b, dims, preferred_element_type=f32, precision=lax.Precision.HIGH)


def _iota(shape, axis):
    return lax.broadcasted_iota(jnp.int32, shape, axis)


def _softplus(x):
    return jnp.maximum(x, 0.0) + jnp.log1p(jnp.exp(-jnp.abs(x)))


_sigmoid = jax.nn.sigmoid


def _silu(x):
    return x * _sigmoid(x)


def _params(sem):
    return pltpu.CompilerParams(dimension_semantics=sem, vmem_limit_bytes=VMEM_LIMIT)


@functools.partial(jax.custom_vjp, nondiff_argnums=(1,))
def _window(x_ext, off):
    if off == 8:
        return x_ext[8:]
    return pltpu.roll(x_ext, 8 - off, 0)[8:]


def _window_fwd(x_ext, off):
    return _window(x_ext, off), None


def _window_bwd(off, _, g):
    n, w = g.shape
    g_ext = jnp.concatenate([jnp.zeros((8, w), g.dtype), g], axis=0)
    if off == 8:
        return (g_ext,)
    return (pltpu.roll(g_ext, n + off, 0),)


_window.defvjp(_window_fwd, _window_bwd)


def _conv4(x, halo, taps):
    x_ext = jnp.concatenate([halo, x], axis=0)
    y = taps[3] * x
    for j in range(3):
        y = y + taps[j] * _window(x_ext, 5 + j)
    return y


def _blockinv_impl(a):
    n = a.shape[0]
    ri, ci = _iota((n, n), 0), _iota((n, n), 1)
    t = (ri == ci).astype(f32) - jnp.where(((ri >> 1) == (ci >> 1)) & (ri > ci), a, 0.0)
    k = 1
    while (1 << k) < n:
        sel = ((ri >> (k + 1)) == (ci >> (k + 1))) & (((ri >> k) & 1) == 1) & (((ci >> k) & 1) == 0)
        o = jnp.where(sel, a, 0.0)
        t = t - _dotx(_dotx(t, o), t)
        k += 1
    return t


@jax.custom_vjp
def _blockinv(a):
    return _blockinv_impl(a)


def _blockinv_fwd(a):
    t = _blockinv_impl(a)
    return t, t


def _blockinv_bwd(t, dt):
    return (-_dotx(_dotx(t, dt, TN), t, NT),)


_blockinv.defvjp(_blockinv_fwd, _blockinv_bwd)


@jax.custom_vjp
def _blockinv_given(a, t):
    return t


_blockinv_given.defvjp(lambda a, t: (t, t), lambda t, dt: _blockinv_bwd(t, dt) + (jnp.zeros_like(t),))


def _scan_rows(x, reverse):
    n = x.shape[0]
    row = _iota(x.shape, 0)
    s = 1
    while s < n:
        if reverse:
            x = x + jnp.where(row < n - s, pltpu.roll(x, n - s, 0), 0.0)
        else:
            x = x + jnp.where(row >= s, pltpu.roll(x, s, 0), 0.0)
        s *= 2
    return x


@jax.custom_vjp
def _cumsum_rows(x):
    return _scan_rows(x, False)


_cumsum_rows.defvjp(lambda x: (_scan_rows(x, False), None), lambda _, g: (_scan_rows(g, True),))


def _gdn_act(xq, xk, xv, hq, hk, hv, tq, tk, tv):
    return _silu(_conv4(xq, hq, tq)), _silu(_conv4(xk, hk, tk)), _silu(_conv4(xv, hv, tv))


def _gdn_core(q, k, v, gate, mb, mg, mr, s, t_given, beta16, g16, gam16, gam16_t, nw):
    c = GDN_CHUNK
    q = q * lax.rsqrt(jnp.sum(q * q, axis=1, keepdims=True) + L2_EPS) * (GDN_D ** -0.5)
    k = k * lax.rsqrt(jnp.sum(k * k, axis=1, keepdims=True) + L2_EPS)

    pick = lambda x, m: jnp.sum(x * m, axis=1, keepdims=True)
    beta = pick(beta16, mb)
    g = jnp.broadcast_to(pick(g16, mg), (c, GDN_D))
    gam1 = pick(gam16, mg)
    gam = jnp.broadcast_to(gam1, (c, GDN_D))
    gam_j = jnp.broadcast_to(jnp.sum(gam16_t * mr, axis=0, keepdims=True), (c, c))

    ri, ci = _iota((c, c), 0), _iota((c, c), 1)
    incl = ci <= ri
    decay = jnp.where(incl, jnp.exp(jnp.where(incl, jnp.broadcast_to(gam1, (c, c)) - gam_j, 0.0)), 0.0)

    kb = k * beta
    a = jnp.where(ci < ri, _dot(kb, k, NT) * decay, 0.0)
    t = _blockinv(a) if t_given is None else _blockinv_given(a, t_given)
    egam = jnp.exp(gam)
    u = _dotx(t, v * beta)
    w = _dotx(t, kb * egam)
    attn = _dot(q, k, NT) * decay
    gl = jnp.sum(g, axis=0, keepdims=True)
    kt = k * jnp.exp(gl - gam)
    v_new = u - _dot(w, s)
    o = _dot(q * egam, s) + _dot(attn, v_new)
    s_out = s * jnp.exp(gl) + _dot(kt, v_new, TN)

    y = o * lax.rsqrt(jnp.mean(o * o, axis=1, keepdims=True) + RMS_EPS) * nw * _silu(gate)
    return y, s_out, t


def _gdn_chunk(q, k, v, gate, s, t_given, ba, alog, dtb, nw, *, masks, row0):
    valid = (row0 + _iota((GDN_CHUNK, 1), 0)) >= PAD
    beta16 = jnp.where(valid, _sigmoid(ba), 0.0)
    g16 = jnp.where(valid, -jnp.exp(alog) * _softplus(ba + dtb), 0.0)
    gam16 = _cumsum_rows(g16)
    core = jax.vmap(_gdn_core, in_axes=(0,) * 8 + (None if t_given is None else 0,) + (None,) * 5)
    y, s_out, t = core(q, k, v, gate, *masks, s, t_given, beta16, g16, gam16, gam16.T, nw)
    return (y, s_out, t) if t_given is None else (y, s_out)


def _gdn_specs(hb, nc, rev):
    w = hb * GDN_D

    def cidx(c):
        return (nc - 1 - c) if rev else c

    def col(base):
        return pl.BlockSpec((GDN_CHUNK, w), lambda h, c: (cidx(c), base // w + h))

    def halo(base):
        return pl.BlockSpec((8, w), lambda h, c: (jnp.maximum(cidx(c) * (GDN_CHUNK // 8) - 1, 0), base // w + h))

    def taps(base):
        return pl.BlockSpec((8, w), lambda h, c: (0, base // w + h))

    ba = pl.BlockSpec((GDN_CHUNK, 128), lambda h, c: (cidx(c), C_BA // 128))
    row = pl.BlockSpec((8, 128), lambda h, c: (0, 0))
    y = pl.BlockSpec((GDN_CHUNK, w), lambda h, c: (cidx(c), h))
    st = pl.BlockSpec((1, hb, GDN_D, GDN_D), lambda h, c: (cidx(c), h, 0, 0))
    in_specs = [col(C_GQ), col(C_GK), col(C_GV), halo(C_GQ), halo(C_GK), halo(C_GV), col(C_GG), ba,
                taps(0), taps(1024), taps(2048), row, row, row]
    return in_specs, y, st, taps, row, col, ba


def _gdn_load(refs, first):
    xq, xk, xv, hq, hk, hv, gate, ba, tq, tk, tv, alog, dtb, nw = refs

    def halo(r):
        return jnp.where(first, 0.0, r[...])

    def taps(r):
        return tuple(r[j:j + 1, :] for j in range(4))

    act = (xq[...], xk[...], xv[...], halo(hq), halo(hk), halo(hv), taps(tq), taps(tk), taps(tv))
    return act, gate[...], (ba[...], alog[0:1, :], dtb[0:1, :], nw[0:1, :])


def _heads(a, hb):
    return jnp.stack([a[:, i * GDN_D:(i + 1) * GDN_D] for i in range(hb)])


def _wide(a):
    return jnp.concatenate([a[i] for i in range(a.shape[0])], axis=1)


def _head_masks(hblk, hb):
    head = hblk * hb + _iota((hb, 1, 128), 0)
    lane = _iota((hb, 1, 128), 2)
    rows = (_iota((hb, 128, 1), 1) == hblk * hb + _iota((hb, 128, 1), 0) + 8).astype(f32)
    return (lane == head).astype(f32), (lane == head + 8).astype(f32), rows


def gdn_fwd(u, conv_w8, alog8, dtb8, nw8, *, hb=8):
    t_rows = u.shape[0]
    nc = t_rows // GDN_CHUNK
    in_specs, y_spec, st_spec, *_ = _gdn_specs(hb, nc, False)

    def body(*refs):
        ins, (y_ref, st_ref, t_ref), (s_scr,) = refs[:14], refs[14:17], refs[17:]
        hblk, c = pl.program_id(0), pl.program_id(1)

        @pl.when(c == 0)
        def _():
            s_scr[...] = jnp.zeros_like(s_scr)

        act, gate, shared = _gdn_load(ins, c == 0)
        s = s_scr[...]
        st_ref[0] = s
        qa, ka, va = _gdn_act(*act)
        y, s_new, t = _gdn_chunk(_heads(qa, hb), _heads(ka, hb), _heads(va, hb), _heads(gate, hb), s, None, *shared,
                                 masks=_head_masks(hblk, hb), row0=c * GDN_CHUNK)
        y_ref[...] = _wide(y).astype(bf16)
        t_ref[0] = t
        s_scr[...] = s_new

    return pl.pallas_call(
        body, name="gdn_fwd", grid=(GDN_HEADS // hb, nc),
        in_specs=in_specs,
        out_specs=(y_spec, st_spec, pl.BlockSpec((1, hb, GDN_CHUNK, GDN_CHUNK), lambda h, c: (c, h, 0, 0))),
        out_shape=(jax.ShapeDtypeStruct((t_rows, D_MODEL), bf16),
                   jax.ShapeDtypeStruct((nc, GDN_HEADS, GDN_D, GDN_D), f32),
                   jax.ShapeDtypeStruct((nc, GDN_HEADS, GDN_CHUNK, GDN_CHUNK), f32)),
        scratch_shapes=[pltpu.VMEM((hb, GDN_D, GDN_D), f32)],
        compiler_params=_params(("arbitrary", "arbitrary")),
    )(u, u, u, u, u, u, u, u, conv_w8, conv_w8, conv_w8, alog8, dtb8, nw8)


def gdn_bwd(u, conv_w8, alog8, dtb8, nw8, states, tinv, dy, du):
    t_rows = u.shape[0]
    nc = t_rows // GDN_CHUNK
    hb = GDN_HEADS
    w = hb * GDN_D
    in_specs, y_spec, st_spec, taps, row, col, ba = _gdn_specs(hb, nc, True)
    nhb = GDN_HEADS // hb

    def body(*refs):
        ins, st_ref, t_ref, dy_ref = refs[:14], refs[14], refs[15], refs[16]
        du_ref, dba_ref, dtq_ref, dtk_ref, dtv_ref, dalog_ref, ddtb_ref, dnw_ref = refs[18:26]
        ds_scr, dh_scr = refs[26:]
        hblk, cc = pl.program_id(0), pl.program_id(1)
        c = nc - 1 - cc

        @pl.when(cc == 0)
        def _():
            ds_scr[...] = jnp.zeros_like(ds_scr)
            dh_scr[...] = jnp.zeros_like(dh_scr)
            dtq_ref[...] = jnp.zeros_like(dtq_ref)
            dtk_ref[...] = jnp.zeros_like(dtk_ref)
            dtv_ref[...] = jnp.zeros_like(dtv_ref)

        @pl.when((cc == 0) & (hblk == 0))
        def _():
            dalog_ref[...] = jnp.zeros_like(dalog_ref)
            ddtb_ref[...] = jnp.zeros_like(ddtb_ref)
            dnw_ref[...] = jnp.zeros_like(dnw_ref)

        act, gate, shared = _gdn_load(ins, c == 0)
        (qa, ka, va), vjp_act = jax.vjp(_gdn_act, *act)
        chunk = functools.partial(_gdn_chunk, masks=_head_masks(hblk, hb), row0=c * GDN_CHUNK)
        _, vjp_core = jax.vjp(chunk, _heads(qa, hb), _heads(ka, hb), _heads(va, hb), _heads(gate, hb), st_ref[0],
                              t_ref[0], *shared)
        dqa, dka, dva, dgate, ds, _, dba, dalog, ddtb, dnw = vjp_core(
            (_heads(dy_ref[...].astype(f32), hb), ds_scr[...]))
        ds_scr[...] = ds
        dxq, dxk, dxv, dhq, dhk, dhv, dtq, dtk, dtv = vjp_act((_wide(dqa), _wide(dka), _wide(dva)))
        zeros = jnp.zeros((GDN_CHUNK - 8, w), f32)
        for j, (dx, dh) in enumerate(((dxq, dhq), (dxk, dhk), (dxv, dhv))):
            du_ref[:, j * w:(j + 1) * w] = (dx + jnp.concatenate([zeros, dh_scr[j]], axis=0)).astype(bf16)
            dh_scr[j] = dh
        du_ref[:, 3 * w:4 * w] = _wide(dgate).astype(bf16)
        dba_ref[0] = dba
        for dt_ref, dtaps in ((dtq_ref, dtq), (dtk_ref, dtk), (dtv_ref, dtv)):
            for j in range(4):
                dt_ref[j:j + 1, :] += dtaps[j]
        dalog_ref[0:1, :] += dalog
        ddtb_ref[0:1, :] += ddtb
        dnw_ref[0:1, :] += dnw

    out_specs = (pl.BlockSpec((GDN_CHUNK, 4 * w), lambda h, c: (nc - 1 - c, C_GQ // (4 * w))),
                 pl.BlockSpec((1, GDN_CHUNK, 128), lambda h, c: (h, nc - 1 - c, 0)),
                 taps(0), taps(0), taps(0), row, row, row)
    out_shape = (jax.ShapeDtypeStruct(du.shape, du.dtype),
                 jax.ShapeDtypeStruct((nhb, t_rows, 128), f32),
                 jax.ShapeDtypeStruct((8, D_MODEL), f32), jax.ShapeDtypeStruct((8, D_MODEL), f32),
                 jax.ShapeDtypeStruct((8, D_MODEL), f32),
                 jax.ShapeDtypeStruct((8, 128), f32), jax.ShapeDtypeStruct((8, 128), f32), jax.ShapeDtypeStruct((8, 128), f32))
    return pl.pallas_call(
        body, name="gdn_bwd", grid=(nhb, nc),
        in_specs=in_specs + [st_spec, pl.BlockSpec((1, hb, GDN_CHUNK, GDN_CHUNK), lambda h, c: (nc - 1 - c, h, 0, 0)),
                             y_spec, ANY],
        out_specs=out_specs, out_shape=out_shape, input_output_aliases={17: 0},
        scratch_shapes=[pltpu.VMEM((hb, GDN_D, GDN_D), f32), pltpu.VMEM((3, 8, w), f32)],
        compiler_params=_params(("arbitrary", "arbitrary")),
    )(u, u, u, u, u, u, u, u, conv_w8, conv_w8, conv_w8, alog8, dtb8, nw8, states, tinv, dy, du)


def _ssd_act(xs_r, b_r, c_r, hx, hbm, hcm, tx, tb, tc, bx, bb, bc, *, row0):
    valid = (row0 + _iota((SSD_CHUNK, 1), 0)) >= PAD
    act = lambda x, h, t, b: jnp.where(valid, _silu(_conv4(x, h, t) + b), 0.0)
    return act(xs_r, hx, tx, bx), act(b_r, hbm, tb, bb), act(c_r, hcm, tc, bc)


def _ssd_core(xs, bm, cm, z, nw, lanes, rows, h, dtp16, adt16, acum16, acum16_t, dsk):
    n = SSD_CHUNK
    pick = lambda x, m: jnp.sum(x * m, axis=1, keepdims=True)
    lane_r = _iota((1, 256), 1) >> 6
    dtp = jnp.zeros((n, 256), f32)
    adt = jnp.zeros((n, 256), f32)
    acum = jnp.zeros((n, 256), f32)
    dlane = jnp.zeros((1, 256), f32)
    ccols = []
    for r in range(SSD_HPG):
        ccols.append(pick(acum16, lanes[r]))
        dtp = jnp.where(lane_r == r, pick(dtp16, lanes[r]), dtp)
        adt = jnp.where(lane_r == r, pick(adt16, lanes[r]), adt)
        acum = jnp.where(lane_r == r, ccols[r], acum)
        dlane = jnp.where(lane_r == r, pick(dsk, lanes[r]), dlane)

    ri, ci = _iota((n, n), 0), _iota((n, n), 1)
    incl = ci <= ri
    al = jnp.sum(adt, axis=0, keepdims=True)
    xdt = xs * dtp
    cb = _dot(cm, bm, NT)
    y = _dot(cm, h) * jnp.exp(acum) + dlane * xs
    for r in range(SSD_HPG):
        ai = jnp.broadcast_to(ccols[r], (n, n))
        aj = jnp.broadcast_to(jnp.sum(acum16_t * rows[r], axis=0, keepdims=True), (n, n))
        lm = jnp.where(incl, jnp.exp(jnp.where(incl, ai - aj, 0.0)), 0.0)
        y = y + _dot(cb * lm, jnp.where(lane_r == r, xdt, 0.0))
    h_out = h * jnp.exp(al) + _dot(bm, jnp.exp(al - acum) * xdt, TN)
    y = y * _silu(z)
    y = y * lax.rsqrt(jnp.mean(y * y, axis=1, keepdims=True) + RMS_EPS) * nw
    return y, h_out


def _ssd_chunk(xs, bm, cm, z, nw, h, dt, dtb, alog, dsk, *, row0):
    valid = (row0 + _iota((SSD_CHUNK, 1), 0)) >= PAD
    dtp16 = jnp.where(valid, _softplus(dt + dtb), 0.0)
    adt16 = -jnp.exp(alog) * dtp16
    acum16 = _cumsum_rows(adt16)
    lanes = tuple((_iota((SSD_GROUPS, 1, 128), 2) == _iota((SSD_GROUPS, 1, 128), 0) * SSD_HPG + r).astype(f32)
                  for r in range(SSD_HPG))
    rows = tuple((_iota((SSD_GROUPS, 128, 1), 1) == _iota((SSD_GROUPS, 128, 1), 0) * SSD_HPG + r).astype(f32)
                 for r in range(SSD_HPG))
    core = jax.vmap(_ssd_core, in_axes=(0,) * 8 + (None,) * 5)
    return core(xs, bm, cm, z, nw, lanes, rows, h, dtp16, adt16, acum16, acum16.T, dsk)


def _ssd_specs(nc, rev):
    n = SSD_CHUNK

    def cidx(c):
        return (nc - 1 - c) if rev else c

    def col(base, w):
        return pl.BlockSpec((n, w), lambda c: (cidx(c), base // w))

    def halo(base, w):
        return pl.BlockSpec((8, w), lambda c: (jnp.maximum(cidx(c) * (n // 8) - 1, 0), base // w))

    def taps(base, w):
        return pl.BlockSpec((8, w), lambda c: (0, base // w))

    row = pl.BlockSpec((8, 128), lambda c: (0, 0))
    in_specs = [col(C_SX, 1024), col(C_SB, 512), col(C_SC, 512), halo(C_SX, 1024), halo(C_SB, 512), halo(C_SC, 512),
                col(C_SZ, 1024), col(C_SDT, 128), taps(0, 1024), taps(1024, 512), taps(1536, 512), row, row, row,
                taps(0, 1024)]
    y = pl.BlockSpec((n, D_MODEL), lambda c: (cidx(c), 0))
    st = pl.BlockSpec((1, SSD_GROUPS, SSD_N, 256), lambda c: (cidx(c), 0, 0, 0))
    return in_specs, y, st, col, taps, row


def _ssd_load(refs, first):
    xs, bm, cm, hx, hbm, hcm, z, dt, tx, tb, tc, dtb, alog, dsk, nw = refs

    def halo(r):
        return jnp.where(first, 0.0, r[...])

    def taps(r):
        return tuple(r[j:j + 1, :] for j in range(4))

    act = (xs[...], bm[...], cm[...], halo(hx), halo(hbm), halo(hcm), taps(tx), taps(tb), taps(tc),
           tx[4:5, :], tb[4:5, :], tc[4:5, :])
    return act, (z[...], nw[0:1, :]), (dt[...], dtb[0:1, :], alog[0:1, :], dsk[0:1, :])


def _groups(a, w):
    return jnp.stack([a[:, i * w:(i + 1) * w] for i in range(SSD_GROUPS)])


def ssd_fwd(u, conv_w8, dtb8, alog8, d8, nw8):
    t_rows = u.shape[0]
    nc = t_rows // SSD_CHUNK
    in_specs, y_spec, st_spec, *_ = _ssd_specs(nc, False)

    def body(*refs):
        ins, (y_ref, st_ref), (h_scr,) = refs[:15], refs[15:17], refs[17:]
        c = pl.program_id(0)

        @pl.when(c == 0)
        def _():
            h_scr[...] = jnp.zeros_like(h_scr)

        act, (z, nw), shared = _ssd_load(ins, c == 0)
        h = h_scr[...]
        st_ref[0] = h
        xs, bm, cm = _ssd_act(*act, row0=c * SSD_CHUNK)
        y, h_new = _ssd_chunk(_groups(xs, 256), _groups(bm, 128), _groups(cm, 128), _groups(z, 256),
                              _groups(nw, 256), h, *shared, row0=c * SSD_CHUNK)
        y_ref[...] = _wide(y).astype(bf16)
        h_scr[...] = h_new

    return pl.pallas_call(
        body, name="ssd_fwd", grid=(nc,), in_specs=in_specs, out_specs=(y_spec, st_spec),
        out_shape=(jax.ShapeDtypeStruct((t_rows, D_MODEL), bf16),
                   jax.ShapeDtypeStruct((nc, SSD_GROUPS, SSD_N, 256), f32)),
        scratch_shapes=[pltpu.VMEM((SSD_GROUPS, SSD_N, 256), f32)],
        compiler_params=_params(("arbitrary",)),
    )(u, u, u, u, u, u, u, u, conv_w8, conv_w8, conv_w8, dtb8, alog8, d8, nw8)


def ssd_bwd(u, conv_w8, dtb8, alog8, d8, nw8, states, dy, du):
    t_rows = u.shape[0]
    nc = t_rows // SSD_CHUNK
    n = SSD_CHUNK
    in_specs, y_spec, st_spec, col, taps, row = _ssd_specs(nc, True)

    def body(*refs):
        ins, st_ref, dy_ref = refs[:15], refs[15], refs[16]
        du_ref, ddt_ref, dtx_ref, dtb_ref, dtc_ref, ddtb_ref, dalog_ref, ddsk_ref, dnw_ref = refs[18:27]
        dh_scr, hx_scr, hb_scr, hc_scr = refs[27:]
        cc = pl.program_id(0)
        c = nc - 1 - cc

        @pl.when(cc == 0)
        def _():
            for r in (dh_scr, hx_scr, hb_scr, hc_scr, dtx_ref, dtb_ref, dtc_ref, dnw_ref, ddtb_ref, dalog_ref, ddsk_ref):
                r[...] = jnp.zeros_like(r)

        act, (z, nw), shared = _ssd_load(ins, c == 0)
        (xs, bm, cm), vjp_act = jax.vjp(functools.partial(_ssd_act, row0=c * n), *act)
        _, vjp_core = jax.vjp(functools.partial(_ssd_chunk, row0=c * n), _groups(xs, 256), _groups(bm, 128),
                              _groups(cm, 128), _groups(z, 256), _groups(nw, 256), st_ref[0], *shared)
        dxa, dba, dca, dz, dnw, dh, ddt, ddtb, dalog, ddsk = vjp_core(
            (_groups(dy_ref[...].astype(f32), 256), dh_scr[...]))
        dh_scr[...] = dh
        dxs, dbm, dcm, dhx, dhb, dhc, dtx, dtb, dtc, dbx, dbb, dbc = vjp_act((_wide(dxa), _wide(dba), _wide(dca)))
        du_ref[:, 0:D_MODEL] = _wide(dz).astype(bf16)
        for dx, dhalo, scr, lo in ((dxs, dhx, hx_scr, C_SX), (dbm, dhb, hb_scr, C_SB), (dcm, dhc, hc_scr, C_SC)):
            zeros = jnp.zeros((n - 8, dx.shape[1]), f32)
            du_ref[:, lo - C_SZ:lo - C_SZ + dx.shape[1]] = (dx + jnp.concatenate([zeros, scr[...]], axis=0)).astype(bf16)
            scr[...] = dhalo
        ddt_ref[...] = ddt
        for ref, dtaps, dbias in ((dtx_ref, dtx, dbx), (dtb_ref, dtb, dbb), (dtc_ref, dtc, dbc)):
            for j in range(4):
                ref[j:j + 1, :] += dtaps[j]
            ref[4:5, :] += dbias
        ddtb_ref[0:1, :] += ddtb
        dalog_ref[0:1, :] += dalog
        ddsk_ref[0:1, :] += ddsk
        dnw_ref[0:1, :] += _wide(dnw)

    def out_col(w):
        return pl.BlockSpec((n, w), lambda c: (nc - 1 - c, 0))

    out_specs = (pl.BlockSpec((n, 3 * D_MODEL), lambda c: (nc - 1 - c, C_SZ // (3 * D_MODEL))), out_col(128),
                 taps(0, D_MODEL), taps(0, 512), taps(0, 512), row, row, row, taps(0, D_MODEL))
    out_shape = (jax.ShapeDtypeStruct(du.shape, du.dtype),
                 jax.ShapeDtypeStruct((t_rows, 128), f32),
                 jax.ShapeDtypeStruct((8, D_MODEL), f32), jax.ShapeDtypeStruct((8, 512), f32),
                 jax.ShapeDtypeStruct((8, 512), f32),
                 jax.ShapeDtypeStruct((8, 128), f32), jax.ShapeDtypeStruct((8, 128), f32),
                 jax.ShapeDtypeStruct((8, 128), f32), jax.ShapeDtypeStruct((8, D_MODEL), f32))
    return pl.pallas_call(
        body, name="ssd_bwd", grid=(nc,), in_specs=in_specs + [st_spec, y_spec, ANY],
        out_specs=out_specs, out_shape=out_shape, input_output_aliases={17: 0},
        scratch_shapes=[pltpu.VMEM((SSD_GROUPS, SSD_N, 256), f32), pltpu.VMEM((8, D_MODEL), f32),
                        pltpu.VMEM((8, 512), f32), pltpu.VMEM((8, 512), f32)],
        compiler_params=_params(("arbitrary",)),
    )(u, u, u, u, u, u, u, u, conv_w8, conv_w8, conv_w8, dtb8, alog8, d8, nw8, states, dy, du)


NEG = -1e30


def _swa_core(q, kc, kp, km, vc, vp, vm, sink, *, n):
    rows = SWA_REP * SWA_W
    ri, ci = _iota((rows, SWA_W), 0) & (SWA_W - 1), _iota((rows, SWA_W), 1)
    causal = ci <= ri
    m_band = (causal & ((n >= 1) | ((ci >= PAD) & (ri >= PAD)))) | ((ci > ri) & (n >= 2))
    m_meta = (n >= 1) & (ci >= PAD)
    q = q * (SWA_D ** -0.5)
    s = jnp.where(m_band, jnp.where(causal, _dot(q, kc, NT), _dot(q, kp, NT)), NEG)
    sm = jnp.where(m_meta, _dot(q, km, NT), NEG)
    mx = jnp.maximum(jnp.maximum(jnp.max(s, axis=1, keepdims=True), jnp.max(sm, axis=1, keepdims=True)), sink)
    mx = lax.stop_gradient(mx)
    e, em = jnp.exp(s - mx), jnp.exp(sm - mx)
    den = jnp.sum(e, axis=1, keepdims=True) + jnp.sum(em, axis=1, keepdims=True) + jnp.exp(sink - mx)
    return (_dot(jnp.where(causal, e, 0.0), vc) + _dot(jnp.where(causal, 0.0, e), vp) + _dot(em, vm)) / den


def _swa_block(q16, kc, kp, km, vc, vp, vm, sink16, *, n):
    rows = SWA_REP * SWA_W
    lane = _iota((1, 128), 1)
    rep = _iota((rows, 1), 0) >> 7
    cols = []
    for h in range(SWA_KV_HEADS):
        col = jnp.zeros((rows, 1), f32)
        for r in range(SWA_REP):
            s = jnp.sum(jnp.where(lane == h * SWA_REP + r, sink16, 0.0), axis=1, keepdims=True)
            col = jnp.where(rep == r, s, col)
        cols.append(col)
    o = jax.vmap(functools.partial(_swa_core, n=n))(q16.reshape(SWA_KV_HEADS, rows, SWA_D), kc, kp, km, vc, vp, vm,
                                                    jnp.concatenate([col[None] for col in cols], axis=0))
    return o.reshape(q16.shape)


def _swa_specs(nb, rev):
    def bidx(n):
        return (nb - 1 - n) if rev else n

    kvw = SWA_KV_HEADS * SWA_D
    q = pl.BlockSpec((SWA_W, D_MODEL), lambda n: (bidx(n), C_WQ // D_MODEL))

    def kv(base, blk):
        return pl.BlockSpec((SWA_W, kvw), lambda n: (blk(bidx(n)), base // kvw))

    cur, prev, meta = (lambda n: n), (lambda n: jnp.maximum(n - 1, 0)), (lambda n: 0)
    row = pl.BlockSpec((8, 128), lambda n: (0, 0))
    in_specs = [q] + [kv(C_WK, b) for b in (cur, prev, meta)] + [kv(C_WV, b) for b in (cur, prev, meta)] + [row]
    return in_specs, pl.BlockSpec((SWA_W, D_MODEL), lambda n: (bidx(n), 0)), row


def _swa_heads(a):
    return jnp.stack([a[:, i * SWA_D:(i + 1) * SWA_D] for i in range(a.shape[1] // SWA_D)])


def swa_fwd(u, sink8):
    t_rows = u.shape[0]
    nb = t_rows // SWA_W
    in_specs, o_spec, _ = _swa_specs(nb, False)

    def body(q_ref, kc, kp, km, vc, vp, vm, sink_ref, o_ref):
        o = _swa_block(*[_swa_heads(r[...]) for r in (q_ref, kc, kp, km, vc, vp, vm)], sink_ref[0:1, :],
                       n=pl.program_id(0))
        o_ref[...] = _wide(o).astype(bf16)

    return pl.pallas_call(
        body, name="swa_fwd", grid=(nb,), in_specs=in_specs, out_specs=o_spec,
        out_shape=jax.ShapeDtypeStruct((t_rows, D_MODEL), bf16),
        compiler_params=_params(("arbitrary",)),
    )(u, u, u, u, u, u, u, sink8)


def swa_bwd(u, sink8, do, du):
    t_rows = u.shape[0]
    nb = t_rows // SWA_W
    in_specs, o_spec, row = _swa_specs(nb, True)
    width = C_BA - C_WQ

    def body(q_ref, kc, kp, km, vc, vp, vm, sink_ref, do_ref, _, du_ref, dsink_ref,
             dkp_scr, dvp_scr, dkm_scr, dvm_scr):
        nn = pl.program_id(0)
        n = nb - 1 - nn

        @pl.when(nn == 0)
        def _():
            for r in (dkp_scr, dvp_scr, dkm_scr, dvm_scr, dsink_ref):
                r[...] = jnp.zeros_like(r)

        fn = functools.partial(_swa_block, n=n)
        _, vjp = jax.vjp(fn, *[_swa_heads(r[...]) for r in (q_ref, kc, kp, km, vc, vp, vm)], sink_ref[0:1, :])
        dq, dkc, dkp, dkm, dvc, dvp, dvm, dsink = vjp(_swa_heads(do_ref[...]))
        dkm_scr[...] += dkm
        dvm_scr[...] += dvm
        first = n == 0
        dk = dkc + dkp_scr[...] + jnp.where(first, dkm_scr[...], 0.0)
        dv = dvc + dvp_scr[...] + jnp.where(first, dvm_scr[...], 0.0)
        du_ref[:, 0:D_MODEL] = _wide(dq).astype(bf16)
        du_ref[:, C_WK - C_WQ:C_WV - C_WQ] = _wide(dk).astype(bf16)
        du_ref[:, C_WV - C_WQ:width] = _wide(dv).astype(bf16)
        dkp_scr[...] = dkp
        dvp_scr[...] = dvp
        dsink_ref[0:1, :] += dsink

    return pl.pallas_call(
        body, name="swa_bwd", grid=(nb,), in_specs=in_specs + [o_spec, ANY],
        out_specs=(pl.BlockSpec((SWA_W, width), lambda n: (nb - 1 - n, C_WQ // width)), row),
        out_shape=(jax.ShapeDtypeStruct(du.shape, du.dtype), jax.ShapeDtypeStruct((8, 128), f32)),
        input_output_aliases={9: 0},
        scratch_shapes=[pltpu.VMEM((SWA_KV_HEADS, SWA_W, SWA_D), f32)] * 4,
        compiler_params=_params(("arbitrary",)),
    )(u, u, u, u, u, u, u, sink8, do, du)


def _tile(dim, prefs):
    for p in prefs:
        if dim % p == 0:
            return p
    return dim


def _row_tile(rows, d):
    for p in range(min(rows, BLOCK_BYTES // (4 * d)) // 8 * 8, 0, -8):
        if rows % p == 0:
            return p
    return rows


def mm(a, b, *, out_dtype, name, resid=None, relu_grad_of=None, relu2_out=False, ta=False, tb=False, norm_w8=None):
    assert resid is None or relu_grad_of is None
    k, m = (a.shape if ta else a.shape[::-1])
    n = b.shape[0] if tb else b.shape[1]
    rhs_stays = k * 2 * 1024 > MM_OPERAND_BYTES
    if rhs_stays:
        tn = _tile(n, tuple(p for p in (512, 256, 128) if p * k * 2 <= MM_RESIDENT_BYTES))
        tm = _tile(m, tuple(p for p in (512, 384, 256, 128) if p * k * 2 <= MM_OPERAND_BYTES))
        grid = (n // tn, m // tm)
        ij = lambda o, i: (i, o)
    elif k * n * b.dtype.itemsize <= MM_OPERAND_BYTES:
        tn = n
        tm = _tile(m, tuple(p for p in (1408, 1024, 512, 384, 256, 128)
                            if p * k * 2 <= MM_OPERAND_BYTES and p * n * 4 <= MM_OPERAND_BYTES * 2 // 3))
        grid = (m // tm, 1)
        ij = lambda o, i: (o, i)
    else:
        tm = _tile(m, tuple(p for p in (1408, 1024, 512, 384, 256, 128) if p * k * 2 <= MM_OPERAND_BYTES))
        tn = _tile(n, tuple(p for p in (1024, 512, 256, 128) if p * k * 2 <= MM_OPERAND_BYTES // 2))
        grid = (m // tm, n // tn)
        ij = lambda o, i: (o, i)

    extra = resid if resid is not None else relu_grad_of
    staged = ta or norm_w8 is not None
    assert not (ta and norm_w8 is not None) and not (staged and rhs_stays)
    n_in = 2 + (extra is not None) + (norm_w8 is not None)
    n_out = 1 + relu2_out + (norm_w8 is not None)

    def body(*refs):
        ins, outs, scr = refs[:n_in], refs[n_in:n_in + n_out], refs[n_in + n_out:]
        a_ref, b_ref = ins[:2]
        if staged:
            @pl.when(pl.program_id(1) == 0)
            def _():
                if ta:
                    scr[0][...] = a_ref[...].T
                else:
                    hn = _rmsnorm(a_ref[...], ins[-1][0:1, :]).astype(bf16)
                    scr[0][...] = hn
                    outs[-1][...] = hn

            lhs = scr[0][...]
        else:
            lhs = a_ref[...]
        o = _dot(lhs, b_ref[...], NT if tb else NN)
        if resid is not None:
            o = o + ins[2][...]
        if relu_grad_of is not None:
            o = o * (2.0 * jnp.maximum(ins[2][...], 0.0))
        outs[0][...] = o.astype(out_dtype)
        if relu2_out:
            r = jnp.maximum(o, 0.0)
            outs[1][...] = (r * r).astype(bf16)

    in_specs = [pl.BlockSpec((k, tm), lambda o, i: (0, ij(o, i)[0])) if ta
                else pl.BlockSpec((tm, k), lambda o, i: (ij(o, i)[0], 0)),
                pl.BlockSpec((tn, k), lambda o, i: (ij(o, i)[1], 0)) if tb
                else pl.BlockSpec((k, tn), lambda o, i: (0, ij(o, i)[1]))]
    args = [a, b]
    if extra is not None:
        in_specs.append(pl.BlockSpec((tm, tn), ij))
        args.append(extra)
    out_blk = pl.BlockSpec((tm, tn), ij)
    out_specs = [out_blk] * (1 + relu2_out)
    out_shape = [jax.ShapeDtypeStruct((m, n), out_dtype)] + [jax.ShapeDtypeStruct((m, n), bf16)] * relu2_out
    if norm_w8 is not None:
        in_specs.append(pl.BlockSpec((8, k), lambda o, i: (0, 0)))
        args.append(norm_w8)
        out_specs.append(pl.BlockSpec((tm, k), lambda o, i: (ij(o, i)[0], 0)))
        out_shape.append(jax.ShapeDtypeStruct((m, k), bf16))
    res = pl.pallas_call(
        body, name=name, grid=grid, in_specs=in_specs, out_specs=tuple(out_specs), out_shape=tuple(out_shape),
        scratch_shapes=[pltpu.VMEM((tm, k), bf16)] if staged else [],
        compiler_params=_params(("parallel", "arbitrary" if staged else "parallel")),
    )(*args)
    return res[0] if len(res) == 1 else res


def _rows(t_rows):
    return _tile(t_rows, (384, 256, 128))


def _rmsnorm(h, w):
    return h * lax.rsqrt(jnp.mean(h * h, axis=1, keepdims=True) + RMS_EPS) * w


def rmsnorm_bwd(h, w8, dhn, dres, *, name):
    t_rows, d = h.shape
    tr = _rows(t_rows)

    def body(h_ref, w_ref, dhn_ref, dres_ref, dh_ref, dw_ref):
        @pl.when(pl.program_id(0) == 0)
        def _():
            dw_ref[...] = jnp.zeros_like(dw_ref)

        _, vjp = jax.vjp(_rmsnorm, h_ref[...], w_ref[0:1, :])
        dh, dw = vjp(dhn_ref[...])
        dh_ref[...] = dh + dres_ref[...]
        dw_ref[0:1, :] += dw

    blk = pl.BlockSpec((tr, d), lambda i: (i, 0))
    wblk = pl.BlockSpec((8, d), lambda i: (0, 0))
    return pl.pallas_call(
        body, name=name, grid=(t_rows // tr,), in_specs=[blk, wblk, blk, blk], out_specs=(blk, wblk),
        out_shape=(jax.ShapeDtypeStruct((t_rows, d), f32), jax.ShapeDtypeStruct((8, d), f32)),
        compiler_params=_params(("arbitrary",)),
    )(h, w8, dhn, dres)


def _merge(pg, ps, pw, la, lb, lc):
    return _sigmoid(la) * pg + _sigmoid(lb) * ps + _sigmoid(lc) * pw


def _merge_specs(t_rows):
    tr = _rows(t_rows)
    blk = pl.BlockSpec((tr, D_MODEL), lambda i: (i, 0))
    gate = [pl.BlockSpec((tr, D_MODEL), functools.partial(lambda i, j: (i, j), j=C_GATE // D_MODEL + j)) for j in range(3)]
    return tr, blk, gate


def merge_fwd(pg, ps, pw, u):
    t_rows = pg.shape[0]
    tr, blk, gate = _merge_specs(t_rows)

    def body(pg_ref, ps_ref, pw_ref, la, lb, lc, o_ref):
        o_ref[...] = _merge(pg_ref[...], ps_ref[...], pw_ref[...], la[...], lb[...], lc[...]).astype(bf16)

    return pl.pallas_call(
        body, name="merge_fwd", grid=(t_rows // tr,), in_specs=[blk, blk, blk] + gate, out_specs=blk,
        out_shape=jax.ShapeDtypeStruct((t_rows, D_MODEL), bf16), compiler_params=_params(("arbitrary",)),
    )(pg, ps, pw, u, u, u)


def merge_bwd(pg, ps, pw, u, dmerged, du):
    t_rows = pg.shape[0]
    tr, blk, gate = _merge_specs(t_rows)

    def body(pg_ref, ps_ref, pw_ref, la, lb, lc, dm_ref, _, dpg_ref, dps_ref, dpw_ref, dl_ref):
        _, vjp = jax.vjp(_merge, pg_ref[...], ps_ref[...], pw_ref[...], la[...], lb[...], lc[...])
        dpg, dps, dpw, dla, dlb, dlc = vjp(dm_ref[...])
        dpg_ref[...] = dpg.astype(bf16)
        dps_ref[...] = dps.astype(bf16)
        dpw_ref[...] = dpw.astype(bf16)
        for j, dl in enumerate((dla, dlb, dlc)):
            dl_ref[:, j * D_MODEL:(j + 1) * D_MODEL] = dl.astype(bf16)

    act = jax.ShapeDtypeStruct((t_rows, D_MODEL), bf16)
    return pl.pallas_call(
        body, name="merge_bwd", grid=(t_rows // tr,), in_specs=[blk, blk, blk] + gate + [blk, ANY],
        out_specs=(blk, blk, blk, pl.BlockSpec((tr, 3 * D_MODEL), lambda i: (i, C_GATE // (3 * D_MODEL)))),
        out_shape=(act, act, act, jax.ShapeDtypeStruct(du.shape, du.dtype)),
        input_output_aliases={7: 3},
        compiler_params=_params(("arbitrary",)),
    )(pg, ps, pw, u, u, u, dmerged, du)


def loss_head(h, w8, target):
    t_rows, d = h.shape
    tr = HEAD_ROWS

    def loss_fn(hb, w, tgt):
        err = _rmsnorm(hb, w) - tgt
        return 0.5 * jnp.sum(err * err) / d

    def body(h_ref, w_ref, t_ref, loss_ref, dh_ref, dw_ref):
        i = pl.program_id(0)

        @pl.when(i == 0)
        def _():
            loss_ref[...] = jnp.zeros_like(loss_ref)
            dw_ref[...] = jnp.zeros_like(dw_ref)
            dh_ref[...] = jnp.zeros_like(dh_ref)

        @pl.when(i > 0)
        def _():
            val, (dh, dw) = jax.value_and_grad(loss_fn, argnums=(0, 1))(h_ref[...], w_ref[0:1, :], t_ref[...])
            loss_ref[...] += val
            dh_ref[...] = dh
            dw_ref[0:1, :] += dw

    blk = pl.BlockSpec((tr, d), lambda i: (i, 0))
    wblk = pl.BlockSpec((8, d), lambda i: (0, 0))
    return pl.pallas_call(
        body, name="loss_head", grid=(t_rows // tr,),
        in_specs=[blk, wblk, pl.BlockSpec((tr, d), lambda i: (jnp.maximum(i - 1, 0), 0))],
        out_specs=(pl.BlockSpec((8, 128), lambda i: (0, 0)), blk, wblk),
        out_shape=(jax.ShapeDtypeStruct((8, 128), f32), jax.ShapeDtypeStruct((t_rows, d), f32),
                   jax.ShapeDtypeStruct((8, d), f32)),
        compiler_params=_params(("arbitrary",)),
    )(h, w8, target)


def adamw(w, m, v, partials, row_off, *, name):
    rows, d = w.shape
    layers = len(partials)
    per = rows // layers
    tr = _row_tile(per, d)
    assert row_off % tr == 0
    off, nblk = row_off // tr, per // tr
    c1 = 1.0 - ADAM_B1 ** ADAM_STEP
    c2 = 1.0 - ADAM_B2 ** ADAM_STEP

    def body(w_ref, m_ref, v_ref, *refs):
        p_refs, (g_ref, d_ref, mo_ref, vo_ref) = refs[:2 * layers], refs[2 * layers:]
        g = p_refs[0][...] + p_refs[1][...]
        for l in range(1, layers):
            g = jnp.where(pl.program_id(0) >= l * nblk, p_refs[2 * l][...] + p_refs[2 * l + 1][...], g)
        m_new = ADAM_B1 * m_ref[...] + (1.0 - ADAM_B1) * g
        v_new = ADAM_B2 * v_ref[...] + (1.0 - ADAM_B2) * (g * g)
        g_ref[...] = g
        d_ref[...] = -ADAM_LR * ((m_new / c1) / (jnp.sqrt(v_new / c2) + ADAM_EPS) + ADAM_WD * w_ref[...])
        mo_ref[...] = m_new
        vo_ref[...] = v_new

    blk = pl.BlockSpec((tr, d), lambda i: (i, 0))
    pblks = [pl.BlockSpec((tr, d), functools.partial(lambda i, l: (off + jnp.clip(i - l * nblk, 0, nblk - 1), 0), l=l))
             for l in range(layers) for _ in range(2)]
    out = jax.ShapeDtypeStruct((rows, d), f32)
    return pl.pallas_call(
        body, name=name, grid=(rows // tr,), in_specs=[blk, blk, blk] + pblks, out_specs=(blk,) * 4,
        out_shape=(out,) * 4, compiler_params=_params(("arbitrary",)),
    )(w, m, v, *[p for pair in partials for p in pair])


def reduce4(parts, *, name, own=None, me=None):
    _, rows, d = parts.shape
    tr = _row_tile(rows, d)

    def body(*refs):
        p_ref, o_ref = refs[0], refs[-1]
        acc = None
        for s in range(4):
            term = p_ref[s].astype(f32)
            if own is not None:
                term = jnp.where(refs[2][0] == s, refs[1][...].astype(f32), term)
            acc = term if acc is None else acc + term
        o_ref[...] = acc

    in_specs = [pl.BlockSpec((4, tr, d), lambda i: (0, i, 0))]
    args = [parts]
    if own is not None:
        in_specs += [pl.BlockSpec((tr, d), lambda i: (i, 0)), pl.BlockSpec(memory_space=pltpu.SMEM)]
        args += [own, me]
    return pl.pallas_call(
        body, name=name, grid=(rows // tr,), in_specs=in_specs,
        out_specs=pl.BlockSpec((tr, d), lambda i: (i, 0)), out_shape=jax.ShapeDtypeStruct((rows, d), f32),
        compiler_params=_params(("arbitrary",)),
    )(*args)


ANY = pl.BlockSpec(memory_space=pl.ANY)
MESH = pl.DeviceIdType.MESH
CHIP_FLIPS = ((0, 1), (1, 0), (1, 1))


def chip_exchange(bufs, scatter, *, name, after=None):
    nb = len(bufs)
    extra = [] if after is None else [after]

    def body(*refs):
        ins, outs = refs[:nb], refs[nb + len(extra):2 * nb + len(extra)]
        send_sems, recv_sems, local_sems = refs[2 * nb + len(extra):]
        x, y, c = lax.axis_index("x"), lax.axis_index("y"), lax.axis_index("c")
        me = 2 * x + y
        local = [pltpu.make_async_copy(ins[j].at[me] if scatter[j] else ins[j], outs[j].at[me], local_sems.at[j])
                 for j in range(nb)]
        for cp in local:
            cp.start()
        sends, recvs = [], []
        for k, (fx, fy) in enumerate(CHIP_FLIPS):
            px = 1 - x if fx else x
            py = 1 - y if fy else y
            chip = 2 * px + py
            for j in range(nb):
                src = ins[j].at[chip] if scatter[j] else ins[j]
                sems = dict(send_sem=send_sems.at[nb * k + j], recv_sem=recv_sems.at[nb * k + j],
                            device_id=(px, py, c), device_id_type=MESH)
                sends.append(pltpu.make_async_remote_copy(src_ref=src, dst_ref=outs[j].at[me], **sems))
                recvs.append(pltpu.make_async_remote_copy(src_ref=src, dst_ref=outs[j].at[chip], **sems))
        for cp in sends:
            cp.start()
        for cp in recvs:
            cp.wait_recv()
        for cp in sends:
            cp.wait_send()
        for cp in local:
            cp.wait()

    out_shape = tuple(jax.ShapeDtypeStruct(b.shape if s else (4,) + b.shape, b.dtype) for b, s in zip(bufs, scatter))
    return pl.pallas_call(
        body, name=name, in_specs=[ANY] * (nb + len(extra)), out_specs=(ANY,) * nb, out_shape=out_shape,
        scratch_shapes=[pltpu.SemaphoreType.DMA((3 * nb,)), pltpu.SemaphoreType.DMA((3 * nb,)),
                        pltpu.SemaphoreType.DMA((nb,))],
        compiler_params=pltpu.CompilerParams(has_side_effects=True),
    )(*bufs, *extra)


def gather_two_level(big, small, *, name):
    half = big.shape[1] // 2

    def body(big_ref, small_ref, obig_ref, osmall_ref, send_sems, recv_sems, local_sems):
        x, y, c = lax.axis_index("x"), lax.axis_index("y"), lax.axis_index("c")
        me = 2 * x + y
        mine = (slice(None), pl.ds(pl.multiple_of(c * half, half), half))
        theirs = (slice(None), pl.ds(pl.multiple_of((1 - c) * half, half), half))
        local = [pltpu.make_async_copy(big_ref, obig_ref.at[me], local_sems.at[0]),
                 pltpu.make_async_copy(small_ref, osmall_ref.at[me], local_sems.at[1])]
        for cp in local:
            cp.start()

        def copy(k, src, dst, to):
            return pltpu.make_async_remote_copy(src_ref=src, dst_ref=dst, send_sem=send_sems.at[k],
                                                recv_sem=recv_sems.at[k], device_id=to, device_id_type=MESH)

        sends, landed, passed, small_in = [], [], [], []
        for k, (fx, fy) in enumerate(CHIP_FLIPS):
            px = 1 - x if fx else x
            py = 1 - y if fy else y
            chip = 2 * px + py
            sends.append(copy(k, big_ref.at[mine], obig_ref.at[(me,) + mine], (px, py, c)))
            landed.append(copy(k, big_ref.at[mine], obig_ref.at[(chip,) + mine], (px, py, c)))
            sends.append(copy(3 + k, small_ref, osmall_ref.at[me], (px, py, c)))
            small_in.append(copy(3 + k, small_ref, osmall_ref.at[chip], (px, py, c)))
            passed.append((copy(6 + k, obig_ref.at[(chip,) + mine], obig_ref.at[(chip,) + mine], (x, y, 1 - c)),
                           copy(6 + k, obig_ref.at[(chip,) + theirs], obig_ref.at[(chip,) + theirs], (x, y, 1 - c))))
        for cp in sends:
            cp.start()
        for k in range(3):
            landed[k].wait_recv()
            passed[k][0].start()
        for k in range(3):
            passed[k][1].wait_recv()
            small_in[k].wait_recv()
        for cp in sends + [p[0] for p in passed]:
            cp.wait_send()
        for cp in local:
            cp.wait()

    return pl.pallas_call(
        body, name=name, in_specs=[ANY, ANY], out_specs=(ANY, ANY),
        out_shape=(jax.ShapeDtypeStruct((4,) + big.shape, big.dtype),
                   jax.ShapeDtypeStruct((4,) + small.shape, small.dtype)),
        scratch_shapes=[pltpu.SemaphoreType.DMA((9,)), pltpu.SemaphoreType.DMA((9,)), pltpu.SemaphoreType.DMA((2,))],
        compiler_params=pltpu.CompilerParams(has_side_effects=True),
    )(big, small)


def sibling_swap(bufs, *, name):
    nb = len(bufs)

    def body(*refs):
        ins, outs, (send_sems, recv_sems) = refs[:nb], refs[nb:2 * nb], refs[2 * nb:]
        peer = (lax.axis_index("x"), lax.axis_index("y"), 1 - lax.axis_index("c"))
        copies = [pltpu.make_async_remote_copy(src_ref=ins[j], dst_ref=outs[j], send_sem=send_sems.at[j],
                                               recv_sem=recv_sems.at[j], device_id=peer, device_id_type=MESH)
                  for j in range(nb)]
        for cp in copies:
            cp.start()
        for cp in copies:
            cp.wait_recv()
        for cp in copies:
            cp.wait_send()

    return pl.pallas_call(
        body, name=name, in_specs=[ANY] * nb, out_specs=(ANY,) * nb,
        out_shape=tuple(jax.ShapeDtypeStruct(b.shape, b.dtype) for b in bufs),
        scratch_shapes=[pltpu.SemaphoreType.DMA((nb,)), pltpu.SemaphoreType.DMA((nb,))],
        compiler_params=pltpu.CompilerParams(has_side_effects=True),
    )(*bufs)


HBM = pl.BlockSpec(memory_space=pltpu.HBM)
SEM = pl.BlockSpec(memory_space=pltpu.SEMAPHORE)
DATAFLOW = pltpu.SideEffectType.DATAFLOW_SIDE_EFFECTING


def _exchange_copies(srcs, lands, send_sems, recv_sems, scatter):
    x, y, c = lax.axis_index("x"), lax.axis_index("y"), lax.axis_index("c")
    me = 2 * x + y
    nb = len(srcs)
    pairs = []
    for k, (fx, fy) in enumerate(CHIP_FLIPS):
        px = 1 - x if fx else x
        py = 1 - y if fy else y
        chip = 2 * px + py
        for j in range(nb):
            src = srcs[j].at[chip] if scatter[j] else srcs[j]
            sems = dict(send_sem=send_sems.at[nb * k + j], recv_sem=recv_sems.at[nb * k + j],
                        device_id=(px, py, c), device_id_type=MESH)
            pairs.append((pltpu.make_async_remote_copy(src_ref=src, dst_ref=lands[j].at[me], **sems),
                          pltpu.make_async_remote_copy(src_ref=src, dst_ref=lands[j].at[chip], **sems)))
    return pairs


def exchange_start(bufs, scatter, after, *, name):
    nb = len(bufs)
    slabs = [b.shape[1:] if s else b.shape for b, s in zip(bufs, scatter)]
    lands = [lax.empty((4,) + shp, b.dtype) for b, shp in zip(bufs, slabs)]

    def body(*refs):
        srcs, zones = refs[:nb], refs[nb:2 * nb]
        send_sems, recv_sems = refs[2 * nb + 1:2 * nb + 3]
        token = refs[-1]
        for send, _ in _exchange_copies(srcs, zones, send_sems, recv_sems, scatter):
            send.start()
        token[...] = jnp.zeros_like(token)

    hbm = lambda a: pltpu.with_memory_space_constraint(a, pltpu.HBM)
    out = pl.pallas_call(
        body, name=name, in_specs=[HBM] * (2 * nb) + [ANY],
        out_specs=(SEM, SEM) + (HBM,) * (2 * nb) + (pl.BlockSpec(memory_space=pltpu.VMEM),),
        out_shape=(pltpu.SemaphoreType.DMA((3 * nb,)), pltpu.SemaphoreType.DMA((3 * nb,)))
        + tuple(pltpu.HBM(a.shape, a.dtype) for a in list(bufs) + lands) + (jax.ShapeDtypeStruct((8, 128), f32),),
        input_output_aliases={i: 2 + i for i in range(2 * nb)},
        compiler_params=pltpu.CompilerParams(has_side_effects=DATAFLOW),
    )(*[hbm(a) for a in list(bufs) + lands], after)
    return (out[:2], out[2:2 + nb], out[2 + nb:2 + 2 * nb], scatter), out[-1]


def exchange_wait(state, after, *, name):
    (send_sems, recv_sems), srcs, lands, scatter = state
    nb = len(srcs)

    def body(*refs):
        src_refs, zones = refs[:nb], refs[nb:2 * nb]
        s_sems, r_sems = refs[2 * nb:2 * nb + 2]
        for send, recv in _exchange_copies(src_refs, zones, s_sems, r_sems, scatter):
            send.wait_send()
            recv.wait_recv()

    out = pl.pallas_call(
        body, name=name, in_specs=[HBM] * (2 * nb) + [SEM, SEM, ANY], out_specs=(HBM,) * (2 * nb),
        out_shape=tuple(pltpu.HBM(a.shape, a.dtype) for a in list(srcs) + list(lands)),
        input_output_aliases={i: i for i in range(2 * nb)},
        compiler_params=pltpu.CompilerParams(has_side_effects=DATAFLOW),
    )(*srcs, *lands, send_sems, recv_sems, after)
    return out[nb:]


BIG = (
    ("w_proj_gdn", 256), ("w_proj_ssd", 256), ("w_proj_swa", 256), ("w_out", 256), ("w_up", 1024), ("w_down", 1024))
BIG_OFF = {}
_o = 0
for _n, _r in BIG:
    BIG_OFF[_n] = _o
    _o += _r
BIG_ROWS = _o
W_IN_SHARD = IN_W // 4

W_NAMES = ('meta_tokens', 'norm1_w', 'w_in', 'gdn_conv_w', 'gdn_a_log', 'gdn_dt_bias', 'gdn_norm_w', 'ssd_conv_w',
           'ssd_conv_b', 'ssd_dt_bias', 'ssd_a_log', 'ssd_d', 'ssd_norm_w', 'swa_sinks', 'w_proj_gdn', 'w_proj_ssd',
           'w_proj_swa', 'w_out', 'norm2_w', 'w_up', 'w_down', 'final_norm_w')
SMALL_NAMES = tuple(n for n in W_NAMES if n not in BIG_OFF and n != "w_in")
SMALL_SHARDED = ("meta_tokens", "gdn_conv_w", "ssd_conv_w")


def _pack_rows(parts, dtype):
    flat = jnp.concatenate([p.reshape(-1).astype(dtype) for p in parts])
    n = -(-flat.shape[0] // 8192) * 8192
    return jnp.pad(flat, (0, n - flat.shape[0])).reshape(-1, D_MODEL)


def _unpack_rows(packed, shapes):
    flat, out, o = packed.reshape(-1), [], 0
    for s in shapes:
        n = 1
        for d in s:
            n *= d
        out.append(flat[o:o + n].reshape(s))
        o += n
    return out


def _split_chips(full, axis):
    s = full.shape
    a = full.reshape(s[:axis] + (4, s[axis] // 4) + s[axis + 1:])
    return jnp.moveaxis(a, axis, 0)


def _join_chips(parts, axis):
    a = jnp.moveaxis(parts, 0, axis)
    s = a.shape
    return a.reshape(s[:axis] + (s[axis] * s[axis + 1],) + s[axis + 2:])


BIG_AXIS = {"w_proj_gdn": 1, "w_proj_ssd": 1, "w_proj_swa": 1, "w_out": 1, "w_up": 2, "w_down": 1}


def _w_in_to_padded(w):
    z = lambda n: jnp.zeros((n,) + w.shape[1:], w.dtype)
    return jnp.concatenate([w[8736:11808], w[4112:7184], w[7200:8736], w[4096:4112], z(112),
                            w[7184:7200], z(112 + C_MID_END - C_SDT - 128), w[0:4096]], axis=0)


def _w_in_from_padded(p):
    return jnp.concatenate([p[C_GQ:IN_WP], p[C_BA:C_BA + 16], p[C_SZ:C_WQ], p[C_SDT:C_SDT + 16], p[C_WQ:C_BA],
                            p[0:C_SZ]], axis=0)


def _row8(v, lane0=0, width=128):
    return jnp.pad(v[None, :], ((0, 7), (lane0, width - lane0 - v.shape[0])))


def _layer_fwd(h, p, l, late=None):
    tag = f"l{l}"
    u, hn = mm(h, p["w_in_t"], tb=True, out_dtype=f32, norm_w8=p["n1"], name=f"mm_in_{tag}")
    yg, stg, tg = gdn_fwd(u, p["gcw"], p["galog"], p["gdtb"], p["gnw"])
    ys, sts = ssd_fwd(u, p["scw"], p["sdtb"], p["salog"], p["sd"], p["snw"])
    yw = swa_fwd(u, p["sink"])
    if late is not None:
        p.update(late(yw))
    pg = mm(yg, p["wpg"], out_dtype=f32, name=f"mm_pg_{tag}")
    ps = mm(ys, p["wps"], out_dtype=f32, name=f"mm_ps_{tag}")
    pw = mm(yw, p["wpw"], out_dtype=f32, name=f"mm_pw_{tag}")
    merged = merge_fwd(pg, ps, pw, u)
    h2 = mm(merged, p["wout"], out_dtype=f32, resid=h, name=f"mm_out_{tag}")
    a, r, hn2 = mm(h2, p["wup"], out_dtype=f32, relu2_out=True, norm_w8=p["n2"], name=f"mm_up_{tag}")
    h3 = mm(r, p["wdown"], out_dtype=f32, resid=h2, name=f"mm_down_{tag}")
    saved = dict(h=h, hn=hn, u=u, yg=yg, stg=stg, tg=tg, ys=ys, sts=sts, yw=yw, pg=pg, ps=ps, pw=pw,
                 merged=merged, h2=h2, hn2=hn2, a=a, r=r)
    return h3, saved


def _layer_bwd(dh3, p, s, l, send_big, send_w_in):
    tag = f"l{l}"
    g = {}

    def wgrad(act, d, name):
        return mm(act, d, ta=True, out_dtype=bf16, name=f"wg_{name}_{tag}")

    da = mm(dh3, p["wdown"], tb=True, out_dtype=bf16, relu_grad_of=s["a"], name=f"dg_down_{tag}")
    g["w_down"] = wgrad(s["r"], dh3, "down")
    dhn2 = mm(da, p["wup"], tb=True, out_dtype=f32, name=f"dg_up_{tag}")
    g["w_up"] = wgrad(s["hn2"], da, "up")
    dh2, g["norm2_w"] = rmsnorm_bwd(s["h2"], p["n2"], dhn2, dh3, name=f"norm2_bwd_{tag}")
    dmerged = mm(dh2, p["wout"], tb=True, out_dtype=f32, name=f"dg_out_{tag}")
    g["w_out"] = wgrad(s["merged"], dh2, "out")
    du = lax.empty((dh3.shape[0], IN_WP), bf16)
    dpg, dps, dpw, du = merge_bwd(s["pg"], s["ps"], s["pw"], s["u"], dmerged, du)
    dyg = mm(dpg, p["wpg"], tb=True, out_dtype=f32, name=f"dg_pg_{tag}")
    dys = mm(dps, p["wps"], tb=True, out_dtype=f32, name=f"dg_ps_{tag}")
    dyw = mm(dpw, p["wpw"], tb=True, out_dtype=f32, name=f"dg_pw_{tag}")
    g["w_proj_gdn"] = wgrad(s["yg"], dpg, "pg")
    g["w_proj_ssd"] = wgrad(s["ys"], dps, "ps")
    g["w_proj_swa"] = wgrad(s["yw"], dpw, "pw")
    sent = send_big(jnp.concatenate([_split_chips(g.pop(n), BIG_AXIS[n] - 1).reshape(4, r, D_MODEL)
                                     for n, r in BIG], axis=1))

    (du, dba, dtq, dtk, dtv, g["gdn_a_log"], g["gdn_dt_bias"], g["gdn_norm_w"]) = gdn_bwd(
        s["u"], p["gcw"] + sent, p["galog"], p["gdtb"], p["gnw"], s["stg"], s["tg"], dyg, du)
    g["gdn_conv_w"] = jnp.concatenate([dtq, dtk, dtv], axis=1)[:4]
    (du, ddt, dtx, dtb, dtc, g["ssd_dt_bias"], g["ssd_a_log"], g["ssd_d"], g["ssd_norm_w"]) = ssd_bwd(
        s["u"], p["scw"], p["sdtb"], p["salog"], p["sd"], p["snw"], s["sts"], dys, du)
    dconv = jnp.concatenate([dtx, dtb, dtc], axis=1)
    g["ssd_conv_w"], g["ssd_conv_b"] = dconv[:4], dconv[4]
    du, g["swa_sinks"] = swa_bwd(s["u"], p["sink"], dyw, du)
    mid = jnp.concatenate([dba[0].astype(bf16), ddt.astype(bf16),
                           jnp.zeros((du.shape[0], C_MID_END - C_SDT - 128), bf16)], axis=1)
    du = lax.dynamic_update_slice(du, mid, (0, C_BA))
    sent = send_w_in(_w_in_from_padded(wgrad(du, s["hn"], "in")).reshape(4, W_IN_SHARD, D_MODEL))
    dhn = mm(du, p["w_in_t"], out_dtype=f32, name=f"dg_in_{tag}")
    dh, g["norm1_w"] = rmsnorm_bwd(s["h"], p["n1"] + sent, dhn, dh2, name=f"norm1_bwd_{tag}")
    return dh, g


def kernel(x, meta_tokens, norm1_w, w_in, gdn_conv_w, gdn_a_log, gdn_dt_bias, gdn_norm_w, ssd_conv_w, ssd_conv_b, ssd_dt_bias, ssd_a_log, ssd_d, ssd_norm_w, swa_sinks, w_proj_gdn, w_proj_ssd, w_proj_swa, w_out, norm2_w, w_up, w_down, final_norm_w, loss_target, m_meta_tokens, m_norm1_w, m_w_in, m_gdn_conv_w, m_gdn_a_log, m_gdn_dt_bias, m_gdn_norm_w, m_ssd_conv_w, m_ssd_conv_b, m_ssd_dt_bias, m_ssd_a_log, m_ssd_d, m_ssd_norm_w, m_swa_sinks, m_w_proj_gdn, m_w_proj_ssd, m_w_proj_swa, m_w_out, m_norm2_w, m_w_up, m_w_down, m_final_norm_w, v_meta_tokens, v_norm1_w, v_w_in, v_gdn_conv_w, v_gdn_a_log, v_gdn_dt_bias, v_gdn_norm_w, v_ssd_conv_w, v_ssd_conv_b, v_ssd_dt_bias, v_ssd_a_log, v_ssd_d, v_ssd_norm_w, v_swa_sinks, v_w_proj_gdn, v_w_proj_ssd, v_w_proj_swa, v_w_out, v_norm2_w, v_w_up, v_w_down, v_final_norm_w):
    given = dict(locals())
    depth = norm1_w.shape[0]
    me = 2 * lax.axis_index("x") + lax.axis_index("y")

    me1 = jnp.reshape(me, (1,)).astype(jnp.int32)
    w_in_t = jnp.swapaxes(w_in, 1, 2)

    def weight_slabs(l):
        return (w_in_t[l].astype(bf16),
                jnp.concatenate([given[n][l].reshape(-1, D_MODEL).astype(bf16) for n, _ in BIG]))

    slabs = [weight_slabs(l) for l in range(depth)]
    wsmall = _pack_rows([given[n] for n in SMALL_SHARDED], f32)
    ga0, gsmall = gather_two_level(slabs[0][0], wsmall, name="gather_first")
    gathers, started = {}, jnp.zeros((), f32)
    for l in range(depth):
        for j in range(2):
            if (l, j) != (0, 0):
                gathers[l, j], token = exchange_start([slabs[l][j]], (False,), gsmall, name=f"gather_start_l{l}_{j}")
                started = started + token[0, 0]
    shard_shapes = [given[n].shape for n in SMALL_SHARDED]
    per_chip = [_unpack_rows(gsmall[s], shard_shapes) for s in range(4)]
    full = {n: jnp.concatenate([per_chip[s][i] for s in range(4)], axis=-1) for i, n in enumerate(SMALL_SHARDED)}

    def landed(l, j, after):
        (zone,) = exchange_wait(gathers[l, j], after, name=f"gather_wait_l{l}_{j}")
        return lax.dynamic_update_slice(zone, slabs[l][j][None], (me, 0, 0))

    def first_operands(l, ga, order):
        return dict(
            n1=_row8(norm1_w[l], width=D_MODEL) + order, n2=_row8(norm2_w[l], width=D_MODEL),
            w_in_t=_w_in_to_padded(ga.reshape(IN_W, D_MODEL)),
            gcw=jnp.pad(full["gdn_conv_w"][l], ((0, 4), (0, 0))),
            galog=_row8(gdn_a_log[l], 8), gdtb=_row8(gdn_dt_bias[l], 8), gnw=_row8(gdn_norm_w[l]),
            scw=jnp.pad(jnp.concatenate([full["ssd_conv_w"][l], ssd_conv_b[l][None]], axis=0), ((0, 3), (0, 0))),
            sdtb=_row8(ssd_dt_bias[l]), salog=_row8(ssd_a_log[l]), sd=_row8(ssd_d[l]),
            snw=_row8(ssd_norm_w[l], width=D_MODEL), sink=_row8(swa_sinks[l]))

    def late_operands(l, after):
        gb = landed(l, 1, after)
        w = {}
        for n, r in BIG:
            parts = gb[:, BIG_OFF[n]:BIG_OFF[n] + r].reshape((4,) + given[n].shape[1:])
            w[n] = _join_chips(parts, BIG_AXIS[n] - 1)
        return dict(wpg=w["w_proj_gdn"], wps=w["w_proj_ssd"], wpw=w["w_proj_swa"], wout=w["w_out"],
                    wup=w["w_up"], wdown=w["w_down"])

    h = jnp.concatenate([jnp.zeros((PAD, D_MODEL), f32), full["meta_tokens"], x[0]], axis=0)
    layers, saved = [], []
    for l in range(depth):
        p = first_operands(0, ga0, started) if l == 0 else first_operands(l, landed(l, 0, h), 0.0)
        h, s = _layer_fwd(h, p, l, late=functools.partial(late_operands, l))
        layers.append(p)
        saved.append(s)
    loss8, dh, dfw8 = loss_head(h, _row8(final_norm_w, width=D_MODEL), loss_target[0])
    grads = {"final_norm_w": dfw8[0]}
    per_layer, grad_slabs, scatters = [None] * depth, {}, {}

    def send(l, j, slab):
        grad_slabs[l, j] = slab
        scatters[l, j], token = exchange_start([slab], (True,), loss8, name=f"scatter_start_l{l}_{j}")
        return token[0, 0]

    for l in reversed(range(depth)):
        dh, per_layer[l] = _layer_bwd(dh, layers[l], saved[l], l, functools.partial(send, l, 1),
                                      functools.partial(send, l, 0))
    grad_x = dh[HEAD_ROWS:][None]
    grads["meta_tokens"] = dh[PAD:HEAD_ROWS]
    lane = {"gdn_a_log": (8, 8), "gdn_dt_bias": (8, 8), "gdn_norm_w": (0, 128), "ssd_dt_bias": (0, 16),
            "ssd_a_log": (0, 16), "ssd_d": (0, 16), "swa_sinks": (0, 16)}
    for n in per_layer[0]:
        parts = [per_layer[l][n] for l in range(depth)]
        if n in lane:
            parts = [q[0, lane[n][0]:lane[n][0] + lane[n][1]] for q in parts]
        elif n in ("norm1_w", "norm2_w", "ssd_norm_w"):
            parts = [q[0] for q in parts]
        grads[n] = jnp.stack(parts)
    loss = lax.psum(loss8[0, 0], ("x", "y", "c"))

    gs = _pack_rows([grads[n] for n in SMALL_NAMES], f32)
    def chip_sum(l, j, after):
        (zone,) = exchange_wait(scatters[l, j], after, name=f"scatter_wait_l{l}_{j}")
        own = lax.dynamic_index_in_dim(grad_slabs[l, j], me, 0, keepdims=False)
        return reduce4(zone, own=own, me=me1, name=f"sum_chips_l{l}_{j}")

    early = [(l, j) for l in range(depth) for j in range(2) if (l, j) != (0, 0)]
    mine = {lj: chip_sum(*lj, dh) for lj in early}
    sibs = dict(zip(early, sibling_swap([mine[lj] for lj in early], name="swap_cores_early")))
    out = {}
    for n, r in BIG:
        shp = given[n].shape
        res = adamw(*[given[pre + n].reshape(depth * r, D_MODEL) for pre in ("", "m_", "v_")],
                    [(mine[l, 1], sibs[l, 1]) for l in range(depth)], BIG_OFF[n], name=f"adamw_{n}")
        out[n] = [a.reshape(shp) for a in res]
    mine[0, 0] = chip_sum(0, 0, res[1])
    (rs,) = chip_exchange([gs], (False,), after=mine[0, 0], name="gather_small_grads")
    ps_ = reduce4(rs, name="sum_chips_small")
    sibs[0, 0], ss = sibling_swap([mine[0, 0], ps_], name="swap_cores_last")
    res = adamw(*[jnp.swapaxes(given[pre + "w_in"], 1, 2).reshape(depth * W_IN_SHARD, D_MODEL)
                  for pre in ("", "m_", "v_")],
                [(mine[l, 0], sibs[l, 0]) for l in range(depth)], 0, name="adamw_w_in")
    out["w_in"] = [jnp.swapaxes(a.reshape(w_in_t.shape), 1, 2) for a in res]
    full_shapes = [grads[n].shape for n in SMALL_NAMES]
    mine_s, sib_s = _unpack_rows(ps_, full_shapes), _unpack_rows(ss, full_shapes)

    def local(parts):
        loc = []
        for n, a in zip(SMALL_NAMES, parts):
            if n in SMALL_SHARDED:
                sz = a.shape[-1] // 4
                a = lax.dynamic_slice_in_dim(a, me * sz, sz, axis=a.ndim - 1)
            loc.append(a)
        return _pack_rows(loc, f32)

    res = adamw(_pack_rows([given[n] for n in SMALL_NAMES], f32), _pack_rows([given["m_" + n] for n in SMALL_NAMES], f32),
                _pack_rows([given["v_" + n] for n in SMALL_NAMES], f32), [(local(mine_s), local(sib_s))], 0,
                name="adamw_small")
    local_shapes = [given[n].shape for n in SMALL_NAMES]
    unpacked = [_unpack_rows(a, local_shapes) for a in res]
    for i, n in enumerate(SMALL_NAMES):
        out[n] = [unpacked[j][i] for j in range(4)]

    return (loss, grad_x) + tuple(out[n][j] for j in range(4) for n in W_NAMES)
```

```python
import functools

import jax
import jax.numpy as jnp
from jax import lax
from jax.experimental import pallas as pl
from jax.experimental.pallas import tpu as pltpu

f32 = jnp.float32
bf16 = jnp.bfloat16

D_MODEL = 1024
N_META = 16
PAD = 112
HEAD_ROWS = PAD + N_META
RMS_EPS = 1e-6
L2_EPS = 1e-6
D_FF = 4 * D_MODEL

GDN_HEADS = 8
GDN_D = 128
GDN_CHUNK = 64
SSD_HEADS = 16
SSD_P = 64
SSD_GROUPS = 4
SSD_HPG = 4
SSD_N = 128
SSD_CHUNK = 128
SWA_Q_HEADS = 16
SWA_KV_HEADS = 4
SWA_REP = 4
SWA_D = 64
SWA_W = 128

C_GATE = 0
C_SZ, C_SX, C_SB, C_SC = 3072, 4096, 5120, 5632
C_WQ, C_WK, C_WV = 6144, 7168, 7424
C_BA = 7680
C_SDT = 7808
C_MID_END = 8192
C_GQ, C_GK, C_GV, C_GG = 8192, 9216, 10240, 11264
IN_WP = 12288
IN_W = 11808

ADAM_LR, ADAM_B1, ADAM_B2, ADAM_EPS, ADAM_WD, ADAM_STEP = 0.001, 0.9, 0.999, 1e-08, 0.01, 10

VMEM_LIMIT = 56 * 1024 * 1024
BLOCK_BYTES = 3 << 19
MM_OPERAND_BYTES = 9 << 20
MM_RESIDENT_BYTES = 13 << 20

NN = (((1,), (0,)), ((), ()))
NT = (((1,), (1,)), ((), ()))
TN = (((0,), (0,)), ((), ()))


def _dot(a, b, dims=NN):
    return lax.dot_general(a.astype(bf16), b.astype(bf16), dims, preferred_element_type=f32)


def _dotx(a, b, dims=NN):
    return lax.dot_general(a, b, dims, preferred_element_type=f32, precision=lax.Precision.HIGH)


def _iota(shape, axis):
    return lax.broadcasted_iota(jnp.int32, shape, axis)


def _softplus(x):
    return jnp.maximum(x, 0.0) + jnp.log1p(jnp.exp(-jnp.abs(x)))


_sigmoid = jax.nn.sigmoid


def _silu(x):
    return x * _sigmoid(x)


def _params(sem):
    return pltpu.CompilerParams(dimension_semantics=sem, vmem_limit_bytes=VMEM_LIMIT)


@functools.partial(jax.custom_vjp, nondiff_argnums=(1,))
def _window(x_ext, off):
    if off == 8:
        return x_ext[8:]
    return pltpu.roll(x_ext, 8 - off, 0)[8:]


def _window_fwd(x_ext, off):
    return _window(x_ext, off), None


def _window_bwd(off, _, g):
    n, w = g.shape
    g_ext = jnp.concatenate([jnp.zeros((8, w), g.dtype), g], axis=0)
    if off == 8:
        return (g_ext,)
    return (pltpu.roll(g_ext, n + off, 0),)


_window.defvjp(_window_fwd, _window_bwd)


def _conv4(x, halo, taps):
    x_ext = jnp.concatenate([halo, x], axis=0)
    y = taps[3] * x
    for j in range(3):
        y = y + taps[j] * _window(x_ext, 5 + j)
    return y


def _blockinv_impl(a):
    n = a.shape[0]
    ri, ci = _iota((n, n), 0), _iota((n, n), 1)
    t = (ri == ci).astype(f32) - jnp.where(((ri >> 1) == (ci >> 1)) & (ri > ci), a, 0.0)
    k = 1
    while (1 << k) < n:
        sel = ((ri >> (k + 1)) == (ci >> (k + 1))) & (((ri >> k) & 1) == 1) & (((ci >> k) & 1) == 0)
        o = jnp.where(sel, a, 0.0)
        t = t - _dotx(_dotx(t, o), t)
        k += 1
    return t


@jax.custom_vjp
def _blockinv(a):
    return _blockinv_impl(a)


def _blockinv_fwd(a):
    t = _blockinv_impl(a)
    return t, t


def _blockinv_bwd(t, dt):
    return (-_dotx(_dotx(t, dt, TN), t, NT),)


_blockinv.defvjp(_blockinv_fwd, _blockinv_bwd)


@jax.custom_vjp
def _blockinv_given(a, t):
    return t


_blockinv_given.defvjp(lambda a, t: (t, t), lambda t, dt: _blockinv_bwd(t, dt) + (jnp.zeros_like(t),))


def _scan_rows(x, reverse):
    n = x.shape[0]
    row = _iota(x.shape, 0)
    s = 1
    while s < n:
        if reverse:
            x = x + jnp.where(row < n - s, pltpu.roll(x, n - s, 0), 0.0)
        else:
            x = x + jnp.where(row >= s, pltpu.roll(x, s, 0), 0.0)
        s *= 2
    return x


@jax.custom_vjp
def _cumsum_rows(x):
    return _scan_rows(x, False)


_cumsum_rows.defvjp(lambda x: (_scan_rows(x, False), None), lambda _, g: (_scan_rows(g, True),))


def _gdn_act(xq, xk, xv, hq, hk, hv, tq, tk, tv):
    return _silu(_conv4(xq, hq, tq)), _silu(_conv4(xk, hk, tk)), _silu(_conv4(xv, hv, tv))


def _gdn_core(q, k, v, gate, mb, mg, mr, s, t_given, beta16, g16, gam16, gam16_t, nw):
    c = GDN_CHUNK
    q = q * lax.rsqrt(jnp.sum(q * q, axis=1, keepdims=True) + L2_EPS) * (GDN_D ** -0.5)
    k = k * lax.rsqrt(jnp.sum(k * k, axis=1, keepdims=True) + L2_EPS)

    pick = lambda x, m: jnp.sum(x * m, axis=1, keepdims=True)
    beta = pick(beta16, mb)
    g = jnp.broadcast_to(pick(g16, mg), (c, GDN_D))
    gam1 = pick(gam16, mg)
    gam = jnp.broadcast_to(gam1, (c, GDN_D))
    gam_j = jnp.broadcast_to(jnp.sum(gam16_t * mr, axis=0, keepdims=True), (c, c))

    ri, ci = _iota((c, c), 0), _iota((c, c), 1)
    incl = ci <= ri
    decay = jnp.where(incl, jnp.exp(jnp.where(incl, jnp.broadcast_to(gam1, (c, c)) - gam_j, 0.0)), 0.0)

    kb = k * beta
    a = jnp.where(ci < ri, _dot(kb, k, NT) * decay, 0.0)
    t = _blockinv(a) if t_given is None else _blockinv_given(a, t_given)
    egam = jnp.exp(gam)
    u = _dotx(t, v * beta)
    w = _dotx(t, kb * egam)
    attn = _dot(q, k, NT) * decay
    gl = jnp.sum(g, axis=0, keepdims=True)
    kt = k * jnp.exp(gl - gam)
    v_new = u - _dot(w, s)
    o = _dot(q * egam, s) + _dot(attn, v_new)
    s_out = s * jnp.exp(gl) + _dot(kt, v_new, TN)

    y = o * lax.rsqrt(jnp.mean(o * o, axis=1, keepdims=True) + RMS_EPS) * nw * _silu(gate)
    return y, s_out, t


def _gdn_chunk(q, k, v, gate, s, t_given, ba, alog, dtb, nw, *, masks, row0):
    valid = (row0 + _iota((GDN_CHUNK, 1), 0)) >= PAD
    beta16 = jnp.where(valid, _sigmoid(ba), 0.0)
    g16 = jnp.where(valid, -jnp.exp(alog) * _softplus(ba + dtb), 0.0)
    gam16 = _cumsum_rows(g16)
    core = jax.vmap(_gdn_core, in_axes=(0,) * 8 + (None if t_given is None else 0,) + (None,) * 5)
    y, s_out, t = core(q, k, v, gate, *masks, s, t_given, beta16, g16, gam16, gam16.T, nw)
    return (y, s_out, t) if t_given is None else (y, s_out)


def _gdn_specs(hb, nc, rev):
    w = hb * GDN_D

    def cidx(c):
        return (nc - 1 - c) if rev else c

    def col(base):
        return pl.BlockSpec((GDN_CHUNK, w), lambda h, c: (cidx(c), base // w + h))

    def halo(base):
        return pl.BlockSpec((8, w), lambda h, c: (jnp.maximum(cidx(c) * (GDN_CHUNK // 8) - 1, 0), base // w + h))

    def taps(base):
        return pl.BlockSpec((8, w), lambda h, c: (0, base // w + h))

    ba = pl.BlockSpec((GDN_CHUNK, 128), lambda h, c: (cidx(c), C_BA // 128))
    row = pl.BlockSpec((8, 128), lambda h, c: (0, 0))
    y = pl.BlockSpec((GDN_CHUNK, w), lambda h, c: (cidx(c), h))
    st = pl.BlockSpec((1, hb, GDN_D, GDN_D), lambda h, c: (cidx(c), h, 0, 0))
    in_specs = [col(C_GQ), col(C_GK), col(C_GV), halo(C_GQ), halo(C_GK), halo(C_GV), col(C_GG), ba,
                taps(0), taps(1024), taps(2048), row, row, row]
    return in_specs, y, st, taps, row, col, ba


def _gdn_load(refs, first):
    xq, xk, xv, hq, hk, hv, gate, ba, tq, tk, tv, alog, dtb, nw = refs

    def halo(r):
        return jnp.where(first, 0.0, r[...])

    def taps(r):
        return tuple(r[j:j + 1, :] for j in range(4))

    act = (xq[...], xk[...], xv[...], halo(hq), halo(hk), halo(hv), taps(tq), taps(tk), taps(tv))
    return act, gate[...], (ba[...], alog[0:1, :], dtb[0:1, :], nw[0:1, :])


def _heads(a, hb):
    return jnp.stack([a[:, i * GDN_D:(i + 1) * GDN_D] for i in range(hb)])


def _wide(a):
    return jnp.concatenate([a[i] for i in range(a.shape[0])], axis=1)


def _head_masks(hblk, hb):
    head = hblk * hb + _iota((hb, 1, 128), 0)
    lane = _iota((hb, 1, 128), 2)
    rows = (_iota((hb, 128, 1), 1) == hblk * hb + _iota((hb, 128, 1), 0) + 8).astype(f32)
    return (lane == head).astype(f32), (lane == head + 8).astype(f32), rows


def gdn_fwd(u, conv_w8, alog8, dtb8, nw8, *, hb=8):
    t_rows = u.shape[0]
    nc = t_rows // GDN_CHUNK
    in_specs, y_spec, st_spec, *_ = _gdn_specs(hb, nc, False)

    def body(*refs):
        ins, (y_ref, st_ref, t_ref), (s_scr,) = refs[:14], refs[14:17], refs[17:]
        hblk, c = pl.program_id(0), pl.program_id(1)

        @pl.when(c == 0)
        def _():
            s_scr[...] = jnp.zeros_like(s_scr)

        act, gate, shared = _gdn_load(ins, c == 0)
        s = s_scr[...]
        st_ref[0] = s
        qa, ka, va = _gdn_act(*act)
        y, s_new, t = _gdn_chunk(_heads(qa, hb), _heads(ka, hb), _heads(va, hb), _heads(gate, hb), s, None, *shared,
                                 masks=_head_masks(hblk, hb), row0=c * GDN_CHUNK)
        y_ref[...] = _wide(y).astype(bf16)
        t_ref[0] = t
        s_scr[...] = s_new

    return pl.pallas_call(
        body, name="gdn_fwd", grid=(GDN_HEADS // hb, nc),
        in_specs=in_specs,
        out_specs=(y_spec, st_spec, pl.BlockSpec((1, hb, GDN_CHUNK, GDN_CHUNK), lambda h, c: (c, h, 0, 0))),
        out_shape=(jax.ShapeDtypeStruct((t_rows, D_MODEL), bf16),
                   jax.ShapeDtypeStruct((nc, GDN_HEADS, GDN_D, GDN_D), f32),
                   jax.ShapeDtypeStruct((nc, GDN_HEADS, GDN_CHUNK, GDN_CHUNK), f32)),
        scratch_shapes=[pltpu.VMEM((hb, GDN_D, GDN_D), f32)],
        compiler_params=_params(("arbitrary", "arbitrary")),
    )(u, u, u, u, u, u, u, u, conv_w8, conv_w8, conv_w8, alog8, dtb8, nw8)


def gdn_bwd(u, conv_w8, alog8, dtb8, nw8, states, tinv, dy, du):
    t_rows = u.shape[0]
    nc = t_rows // GDN_CHUNK
    hb = GDN_HEADS
    w = hb * GDN_D
    in_specs, y_spec, st_spec, taps, row, col, ba = _gdn_specs(hb, nc, True)
    nhb = GDN_HEADS // hb

    def body(*refs):
        ins, st_ref, t_ref, dy_ref = refs[:14], refs[14], refs[15], refs[16]
        du_ref, dba_ref, dtq_ref, dtk_ref, dtv_ref, dalog_ref, ddtb_ref, dnw_ref = refs[18:26]
        ds_scr, dh_scr = refs[26:]
        hblk, cc = pl.program_id(0), pl.program_id(1)
        c = nc - 1 - cc

        @pl.when(cc == 0)
        def _():
            ds_scr[...] = jnp.zeros_like(ds_scr)
            dh_scr[...] = jnp.zeros_like(dh_scr)
            dtq_ref[...] = jnp.zeros_like(dtq_ref)
            dtk_ref[...] = jnp.zeros_like(dtk_ref)
            dtv_ref[...] = jnp.zeros_like(dtv_ref)

        @pl.when((cc == 0) & (hblk == 0))
        def _():
            dalog_ref[...] = jnp.zeros_like(dalog_ref)
            ddtb_ref[...] = jnp.zeros_like(ddtb_ref)
            dnw_ref[...] = jnp.zeros_like(dnw_ref)

        act, gate, shared = _gdn_load(ins, c == 0)
        (qa, ka, va), vjp_act = jax.vjp(_gdn_act, *act)
        chunk = functools.partial(_gdn_chunk, masks=_head_masks(hblk, hb), row0=c * GDN_CHUNK)
        _, vjp_core = jax.vjp(chunk, _heads(qa, hb), _heads(ka, hb), _heads(va, hb), _heads(gate, hb), st_ref[0],
                              t_ref[0], *shared)
        dqa, dka, dva, dgate, ds, _, dba, dalog, ddtb, dnw = vjp_core(
            (_heads(dy_ref[...].astype(f32), hb), ds_scr[...]))
        ds_scr[...] = ds
        dxq, dxk, dxv, dhq, dhk, dhv, dtq, dtk, dtv = vjp_act((_wide(dqa), _wide(dka), _wide(dva)))
        zeros = jnp.zeros((GDN_CHUNK - 8, w), f32)
        for j, (dx, dh) in enumerate(((dxq, dhq), (dxk, dhk), (dxv, dhv))):
            du_ref[:, j * w:(j + 1) * w] = (dx + jnp.concatenate([zeros, dh_scr[j]], axis=0)).astype(bf16)
            dh_scr[j] = dh
        du_ref[:, 3 * w:4 * w] = _wide(dgate).astype(bf16)
        dba_ref[0] = dba
        for dt_ref, dtaps in ((dtq_ref, dtq), (dtk_ref, dtk), (dtv_ref, dtv)):
            for j in range(4):
                dt_ref[j:j + 1, :] += dtaps[j]
        dalog_ref[0:1, :] += dalog
        ddtb_ref[0:1, :] += ddtb
        dnw_ref[0:1, :] += dnw

    out_specs = (pl.BlockSpec((GDN_CHUNK, 4 * w), lambda h, c: (nc - 1 - c, C_GQ // (4 * w))),
                 pl.BlockSpec((1, GDN_CHUNK, 128), lambda h, c: (h, nc - 1 - c, 0)),
                 taps(0), taps(0), taps(0), row, row, row)
    out_shape = (jax.ShapeDtypeStruct(du.shape, du.dtype),
                 jax.ShapeDtypeStruct((nhb, t_rows, 128), f32),
                 jax.ShapeDtypeStruct((8, D_MODEL), f32), jax.ShapeDtypeStruct((8, D_MODEL), f32),
                 jax.ShapeDtypeStruct((8, D_MODEL), f32),
                 jax.ShapeDtypeStruct((8, 128), f32), jax.ShapeDtypeStruct((8, 128), f32), jax.ShapeDtypeStruct((8, 128), f32))
    return pl.pallas_call(
        body, name="gdn_bwd", grid=(nhb, nc),
        in_specs=in_specs + [st_spec, pl.BlockSpec((1, hb, GDN_CHUNK, GDN_CHUNK), lambda h, c: (nc - 1 - c, h, 0, 0)),
                             y_spec, ANY],
        out_specs=out_specs, out_shape=out_shape, input_output_aliases={17: 0},
        scratch_shapes=[pltpu.VMEM((hb, GDN_D, GDN_D), f32), pltpu.VMEM((3, 8, w), f32)],
        compiler_params=_params(("arbitrary", "arbitrary")),
    )(u, u, u, u, u, u, u, u, conv_w8, conv_w8, conv_w8, alog8, dtb8, nw8, states, tinv, dy, du)


def _ssd_act(xs_r, b_r, c_r, hx, hbm, hcm, tx, tb, tc, bx, bb, bc, *, row0):
    valid = (row0 + _iota((SSD_CHUNK, 1), 0)) >= PAD
    act = lambda x, h, t, b: jnp.where(valid, _silu(_conv4(x, h, t) + b), 0.0)
    return act(xs_r, hx, tx, bx), act(b_r, hbm, tb, bb), act(c_r, hcm, tc, bc)


def _ssd_core(xs, bm, cm, z, nw, lanes, rows, h, dtp16, adt16, acum16, acum16_t, dsk):
    n = SSD_CHUNK
    pick = lambda x, m: jnp.sum(x * m, axis=1, keepdims=True)
    lane_r = _iota((1, 256), 1) >> 6
    dtp = jnp.zeros((n, 256), f32)
    adt = jnp.zeros((n, 256), f32)
    acum = jnp.zeros((n, 256), f32)
    dlane = jnp.zeros((1, 256), f32)
    ccols = []
    for r in range(SSD_HPG):
        ccols.append(pick(acum16, lanes[r]))
        dtp = jnp.where(lane_r == r, pick(dtp16, lanes[r]), dtp)
        adt = jnp.where(lane_r == r, pick(adt16, lanes[r]), adt)
        acum = jnp.where(lane_r == r, ccols[r], acum)
        dlane = jnp.where(lane_r == r, pick(dsk, lanes[r]), dlane)

    ri, ci = _iota((n, n), 0), _iota((n, n), 1)
    incl = ci <= ri
    al = jnp.sum(adt, axis=0, keepdims=True)
    xdt = xs * dtp
    cb = _dot(cm, bm, NT)
    y = _dot(cm, h) * jnp.exp(acum) + dlane * xs
    for r in range(SSD_HPG):
        ai = jnp.broadcast_to(ccols[r], (n, n))
        aj = jnp.broadcast_to(jnp.sum(acum16_t * rows[r], axis=0, keepdims=True), (n, n))
        lm = jnp.where(incl, jnp.exp(jnp.where(incl, ai - aj, 0.0)), 0.0)
        y = y + _dot(cb * lm, jnp.where(lane_r == r, xdt, 0.0))
    h_out = h * jnp.exp(al) + _dot(bm, jnp.exp(al - acum) * xdt, TN)
    y = y * _silu(z)
    y = y * lax.rsqrt(jnp.mean(y * y, axis=1, keepdims=True) + RMS_EPS) * nw
    return y, h_out


def _ssd_chunk(xs, bm, cm, z, nw, h, dt, dtb, alog, dsk, *, row0):
    valid = (row0 + _iota((SSD_CHUNK, 1), 0)) >= PAD
    dtp16 = jnp.where(valid, _softplus(dt + dtb), 0.0)
    adt16 = -jnp.exp(alog) * dtp16
    acum16 = _cumsum_rows(adt16)
    lanes = tuple((_iota((SSD_GROUPS, 1, 128), 2) == _iota((SSD_GROUPS, 1, 128), 0) * SSD_HPG + r).astype(f32)
                  for r in range(SSD_HPG))
    rows = tuple((_iota((SSD_GROUPS, 128, 1), 1) == _iota((SSD_GROUPS, 128, 1), 0) * SSD_HPG + r).astype(f32)
                 for r in range(SSD_HPG))
    core = jax.vmap(_ssd_core, in_axes=(0,) * 8 + (None,) * 5)
    return core(xs, bm, cm, z, nw, lanes, rows, h, dtp16, adt16, acum16, acum16.T, dsk)


def _ssd_specs(nc, rev):
    n = SSD_CHUNK

    def cidx(c):
        return (nc - 1 - c) if rev else c

    def col(base, w):
        return pl.BlockSpec((n, w), lambda c: (cidx(c), base // w))

    def halo(base, w):
        return pl.BlockSpec((8, w), lambda c: (jnp.maximum(cidx(c) * (n // 8) - 1, 0), base // w))

    def taps(base, w):
        return pl.BlockSpec((8, w), lambda c: (0, base // w))

    row = pl.BlockSpec((8, 128), lambda c: (0, 0))
    in_specs = [col(C_SX, 1024), col(C_SB, 512), col(C_SC, 512), halo(C_SX, 1024), halo(C_SB, 512), halo(C_SC, 512),
                col(C_SZ, 1024), col(C_SDT, 128), taps(0, 1024), taps(1024, 512), taps(1536, 512), row, row, row,
                taps(0, 1024)]
    y = pl.BlockSpec((n, D_MODEL), lambda c: (cidx(c), 0))
    st = pl.BlockSpec((1, SSD_GROUPS, SSD_N, 256), lambda c: (cidx(c), 0, 0, 0))
    return in_specs, y, st, col, taps, row


def _ssd_load(refs, first):
    xs, bm, cm, hx, hbm, hcm, z, dt, tx, tb, tc, dtb, alog, dsk, nw = refs

    def halo(r):
        return jnp.where(first, 0.0, r[...])

    def taps(r):
        return tuple(r[j:j + 1, :] for j in range(4))

    act = (xs[...], bm[...], cm[...], halo(hx), halo(hbm), halo(hcm), taps(tx), taps(tb), taps(tc),
           tx[4:5, :], tb[4:5, :], tc[4:5, :])
    return act, (z[...], nw[0:1, :]), (dt[...], dtb[0:1, :], alog[0:1, :], dsk[0:1, :])


def _groups(a, w):
    return jnp.stack([a[:, i * w:(i + 1) * w] for i in range(SSD_GROUPS)])


def ssd_fwd(u, conv_w8, dtb8, alog8, d8, nw8):
    t_rows = u.shape[0]
    nc = t_rows // SSD_CHUNK
    in_specs, y_spec, st_spec, *_ = _ssd_specs(nc, False)

    def body(*refs):
        ins, (y_ref, st_ref), (h_scr,) = refs[:15], refs[15:17], refs[17:]
        c = pl.program_id(0)

        @pl.when(c == 0)
        def _():
            h_scr[...] = jnp.zeros_like(h_scr)

        act, (z, nw), shared = _ssd_load(ins, c == 0)
        h = h_scr[...]
        st_ref[0] = h
        xs, bm, cm = _ssd_act(*act, row0=c * SSD_CHUNK)
        y, h_new = _ssd_chunk(_groups(xs, 256), _groups(bm, 128), _groups(cm, 128), _groups(z, 256),
                              _groups(nw, 256), h, *shared, row0=c * SSD_CHUNK)
        y_ref[...] = _wide(y).astype(bf16)
        h_scr[...] = h_new

    return pl.pallas_call(
        body, name="ssd_fwd", grid=(nc,), in_specs=in_specs, out_specs=(y_spec, st_spec),
        out_shape=(jax.ShapeDtypeStruct((t_rows, D_MODEL), bf16),
                   jax.ShapeDtypeStruct((nc, SSD_GROUPS, SSD_N, 256), f32)),
        scratch_shapes=[pltpu.VMEM((SSD_GROUPS, SSD_N, 256), f32)],
        compiler_params=_params(("arbitrary",)),
    )(u, u, u, u, u, u, u, u, conv_w8, conv_w8, conv_w8, dtb8, alog8, d8, nw8)


def ssd_bwd(u, conv_w8, dtb8, alog8, d8, nw8, states, dy, du):
    t_rows = u.shape[0]
    nc = t_rows // SSD_CHUNK
    n = SSD_CHUNK
    in_specs, y_spec, st_spec, col, taps, row = _ssd_specs(nc, True)

    def body(*refs):
        ins, st_ref, dy_ref = refs[:15], refs[15], refs[16]
        du_ref, ddt_ref, dtx_ref, dtb_ref, dtc_ref, ddtb_ref, dalog_ref, ddsk_ref, dnw_ref = refs[18:27]
        dh_scr, hx_scr, hb_scr, hc_scr = refs[27:]
        cc = pl.program_id(0)
        c = nc - 1 - cc

        @pl.when(cc == 0)
        def _():
            for r in (dh_scr, hx_scr, hb_scr, hc_scr, dtx_ref, dtb_ref, dtc_ref, dnw_ref, ddtb_ref, dalog_ref, ddsk_ref):
                r[...] = jnp.zeros_like(r)

        act, (z, nw), shared = _ssd_load(ins, c == 0)
        (xs, bm, cm), vjp_act = jax.vjp(functools.partial(_ssd_act, row0=c * n), *act)
        _, vjp_core = jax.vjp(functools.partial(_ssd_chunk, row0=c * n), _groups(xs, 256), _groups(bm, 128),
                              _groups(cm, 128), _groups(z, 256), _groups(nw, 256), st_ref[0], *shared)
        dxa, dba, dca, dz, dnw, dh, ddt, ddtb, dalog, ddsk = vjp_core(
            (_groups(dy_ref[...].astype(f32), 256), dh_scr[...]))
        dh_scr[...] = dh
        dxs, dbm, dcm, dhx, dhb, dhc, dtx, dtb, dtc, dbx, dbb, dbc = vjp_act((_wide(dxa), _wide(dba), _wide(dca)))
        du_ref[:, 0:D_MODEL] = _wide(dz).astype(bf16)
        for dx, dhalo, scr, lo in ((dxs, dhx, hx_scr, C_SX), (dbm, dhb, hb_scr, C_SB), (dcm, dhc, hc_scr, C_SC)):
            zeros = jnp.zeros((n - 8, dx.shape[1]), f32)
            du_ref[:, lo - C_SZ:lo - C_SZ + dx.shape[1]] = (dx + jnp.concatenate([zeros, scr[...]], axis=0)).astype(bf16)
            scr[...] = dhalo
        ddt_ref[...] = ddt
        for ref, dtaps, dbias in ((dtx_ref, dtx, dbx), (dtb_ref, dtb, dbb), (dtc_ref, dtc, dbc)):
            for j in range(4):
                ref[j:j + 1, :] += dtaps[j]
            ref[4:5, :] += dbias
        ddtb_ref[0:1, :] += ddtb
        dalog_ref[0:1, :] += dalog
        ddsk_ref[0:1, :] += ddsk
        dnw_ref[0:1, :] += _wide(dnw)

    def out_col(w):
        return pl.BlockSpec((n, w), lambda c: (nc - 1 - c, 0))

    out_specs = (pl.BlockSpec((n, 3 * D_MODEL), lambda c: (nc - 1 - c, C_SZ // (3 * D_MODEL))), out_col(128),
                 taps(0, D_MODEL), taps(0, 512), taps(0, 512), row, row, row, taps(0, D_MODEL))
    out_shape = (jax.ShapeDtypeStruct(du.shape, du.dtype),
                 jax.ShapeDtypeStruct((t_rows, 128), f32),
                 jax.ShapeDtypeStruct((8, D_MODEL), f32), jax.ShapeDtypeStruct((8, 512), f32),
                 jax.ShapeDtypeStruct((8, 512), f32),
                 jax.ShapeDtypeStruct((8, 128), f32), jax.ShapeDtypeStruct((8, 128), f32),
                 jax.ShapeDtypeStruct((8, 128), f32), jax.ShapeDtypeStruct((8, D_MODEL), f32))
    return pl.pallas_call(
        body, name="ssd_bwd", grid=(nc,), in_specs=in_specs + [st_spec, y_spec, ANY],
        out_specs=out_specs, out_shape=out_shape, input_output_aliases={17: 0},
        scratch_shapes=[pltpu.VMEM((SSD_GROUPS, SSD_N, 256), f32), pltpu.VMEM((8, D_MODEL), f32),
                        pltpu.VMEM((8, 512), f32), pltpu.VMEM((8, 512), f32)],
        compiler_params=_params(("arbitrary",)),
    )(u, u, u, u, u, u, u, u, conv_w8, conv_w8, conv_w8, dtb8, alog8, d8, nw8, states, dy, du)


NEG = -1e30


def _swa_core(q, kc, kp, km, vc, vp, vm, sink, *, n):
    rows = SWA_REP * SWA_W
    ri, ci = _iota((rows, SWA_W), 0) & (SWA_W - 1), _iota((rows, SWA_W), 1)
    causal = ci <= ri
    m_band = (causal & ((n >= 1) | ((ci >= PAD) & (ri >= PAD)))) | ((ci > ri) & (n >= 2))
    m_meta = (n >= 1) & (ci >= PAD)
    q = q * (SWA_D ** -0.5)
    s = jnp.where(m_band, jnp.where(causal, _dot(q, kc, NT), _dot(q, kp, NT)), NEG)
    sm = jnp.where(m_meta, _dot(q, km, NT), NEG)
    mx = jnp.maximum(jnp.maximum(jnp.max(s, axis=1, keepdims=True), jnp.max(sm, axis=1, keepdims=True)), sink)
    mx = lax.stop_gradient(mx)
    e, em = jnp.exp(s - mx), jnp.exp(sm - mx)
    den = jnp.sum(e, axis=1, keepdims=True) + jnp.sum(em, axis=1, keepdims=True) + jnp.exp(sink - mx)
    return (_dot(jnp.where(causal, e, 0.0), vc) + _dot(jnp.where(causal, 0.0, e), vp) + _dot(em, vm)) / den


def _swa_block(q16, kc, kp, km, vc, vp, vm, sink16, *, n):
    rows = SWA_REP * SWA_W
    lane = _iota((1, 128), 1)
    cols = []
    for h in range(SWA_KV_HEADS):
        sinks = [jnp.sum(jnp.where(lane == h * SWA_REP + r, sink16, 0.0), axis=1, keepdims=True) for r in range(SWA_REP)]
        cols.append(jnp.concatenate([jnp.broadcast_to(s, (SWA_W, 1)) for s in sinks], axis=0))
    o = jax.vmap(functools.partial(_swa_core, n=n))(q16.reshape(SWA_KV_HEADS, rows, SWA_D), kc, kp, km, vc, vp, vm,
                                                    jnp.concatenate([col[None] for col in cols], axis=0))
    return o.reshape(q16.shape)


def _swa_specs(nb, rev):
    def bidx(n):
        return (nb - 1 - n) if rev else n

    kvw = SWA_KV_HEADS * SWA_D
    q = pl.BlockSpec((SWA_W, D_MODEL), lambda n: (bidx(n), C_WQ // D_MODEL))

    def kv(base, blk):
        return pl.BlockSpec((SWA_W, kvw), lambda n: (blk(bidx(n)), base // kvw))

    cur, prev, meta = (lambda n: n), (lambda n: jnp.maximum(n - 1, 0)), (lambda n: 0)
    row = pl.BlockSpec((8, 128), lambda n: (0, 0))
    in_specs = [q] + [kv(C_WK, b) for b in (cur, prev, meta)] + [kv(C_WV, b) for b in (cur, prev, meta)] + [row]
    return in_specs, pl.BlockSpec((SWA_W, D_MODEL), lambda n: (bidx(n), 0)), row


def _swa_heads(a):
    return jnp.stack([a[:, i * SWA_D:(i + 1) * SWA_D] for i in range(a.shape[1] // SWA_D)])


def swa_fwd(u, sink8):
    t_rows = u.shape[0]
    nb = t_rows // SWA_W
    in_specs, o_spec, _ = _swa_specs(nb, False)

    def body(q_ref, kc, kp, km, vc, vp, vm, sink_ref, o_ref):
        o = _swa_block(*[_swa_heads(r[...]) for r in (q_ref, kc, kp, km, vc, vp, vm)], sink_ref[0:1, :],
                       n=pl.program_id(0))
        o_ref[...] = _wide(o).astype(bf16)

    return pl.pallas_call(
        body, name="swa_fwd", grid=(nb,), in_specs=in_specs, out_specs=o_spec,
        out_shape=jax.ShapeDtypeStruct((t_rows, D_MODEL), bf16),
        compiler_params=_params(("arbitrary",)),
    )(u, u, u, u, u, u, u, sink8)


def swa_bwd(u, sink8, do, du):
    t_rows = u.shape[0]
    nb = t_rows // SWA_W
    in_specs, o_spec, row = _swa_specs(nb, True)
    width = C_BA - C_WQ

    def body(q_ref, kc, kp, km, vc, vp, vm, sink_ref, do_ref, _, du_ref, dsink_ref,
             dkp_scr, dvp_scr, dkm_scr, dvm_scr):
        nn = pl.program_id(0)
        n = nb - 1 - nn

        @pl.when(nn == 0)
        def _():
            for r in (dkp_scr, dvp_scr, dkm_scr, dvm_scr, dsink_ref):
                r[...] = jnp.zeros_like(r)

        fn = functools.partial(_swa_block, n=n)
        _, vjp = jax.vjp(fn, *[_swa_heads(r[...]) for r in (q_ref, kc, kp, km, vc, vp, vm)], sink_ref[0:1, :])
        dq, dkc, dkp, dkm, dvc, dvp, dvm, dsink = vjp(_swa_heads(do_ref[...]))
        dkm_scr[...] += dkm
        dvm_scr[...] += dvm
        first = n == 0
        dk = dkc + dkp_scr[...] + jnp.where(first, dkm_scr[...], 0.0)
        dv = dvc + dvp_scr[...] + jnp.where(first, dvm_scr[...], 0.0)
        du_ref[:, 0:D_MODEL] = _wide(dq).astype(bf16)
        du_ref[:, C_WK - C_WQ:C_WV - C_WQ] = _wide(dk).astype(bf16)
        du_ref[:, C_WV - C_WQ:width] = _wide(dv).astype(bf16)
        dkp_scr[...] = dkp
        dvp_scr[...] = dvp
        dsink_ref[0:1, :] += dsink

    return pl.pallas_call(
        body, name="swa_bwd", grid=(nb,), in_specs=in_specs + [o_spec, ANY],
        out_specs=(pl.BlockSpec((SWA_W, width), lambda n: (nb - 1 - n, C_WQ // width)), row),
        out_shape=(jax.ShapeDtypeStruct(du.shape, du.dtype), jax.ShapeDtypeStruct((8, 128), f32)),
        input_output_aliases={9: 0},
        scratch_shapes=[pltpu.VMEM((SWA_KV_HEADS, SWA_W, SWA_D), f32)] * 4,
        compiler_params=_params(("arbitrary",)),
    )(u, u, u, u, u, u, u, sink8, do, du)


def _tile(dim, prefs):
    for p in prefs:
        if dim % p == 0:
            return p
    return dim


def _row_tile(rows, d):
    for p in range(min(rows, BLOCK_BYTES // (4 * d)) // 8 * 8, 0, -8):
        if rows % p == 0:
            return p
    return rows


def mm(a, b, *, out_dtype, name, resid=None, relu_grad_of=None, relu2_out=False, ta=False, tb=False, norm_w8=None):
    assert resid is None or relu_grad_of is None
    k, m = (a.shape if ta else a.shape[::-1])
    n = b.shape[0] if tb else b.shape[1]
    rhs_stays = k * 2 * 1024 > MM_OPERAND_BYTES
    if rhs_stays:
        tn = _tile(n, tuple(p for p in (512, 256, 128) if p * k * 2 <= MM_RESIDENT_BYTES))
        tm = _tile(m, tuple(p for p in (512, 384, 256, 128) if p * k * 2 <= MM_OPERAND_BYTES))
        grid = (n // tn, m // tm)
        ij = lambda o, i: (i, o)
    elif k * n * b.dtype.itemsize <= MM_OPERAND_BYTES:
        tn = n
        tm = _tile(m, tuple(p for p in (1408, 1024, 512, 384, 256, 128)
                            if p * k * 2 <= MM_OPERAND_BYTES and p * n * 4 <= MM_OPERAND_BYTES * 2 // 3))
        grid = (m // tm, 1)
        ij = lambda o, i: (o, i)
    else:
        tm = _tile(m, tuple(p for p in (1408, 1024, 512, 384, 256, 128) if p * k * 2 <= MM_OPERAND_BYTES))
        tn = _tile(n, tuple(p for p in (1024, 512, 256, 128) if p * k * 2 <= MM_OPERAND_BYTES // 2))
        grid = (m // tm, n // tn)
        ij = lambda o, i: (o, i)

    extra = resid if resid is not None else relu_grad_of
    staged = ta or norm_w8 is not None
    assert not (ta and norm_w8 is not None) and not (staged and rhs_stays)
    n_in = 2 + (extra is not None) + (norm_w8 is not None)
    n_out = 1 + relu2_out + (norm_w8 is not None)

    def body(*refs):
        ins, outs, scr = refs[:n_in], refs[n_in:n_in + n_out], refs[n_in + n_out:]
        a_ref, b_ref = ins[:2]
        if staged:
            @pl.when(pl.program_id(1) == 0)
            def _():
                if ta:
                    scr[0][...] = a_ref[...].T
                else:
                    hn = _rmsnorm(a_ref[...], ins[-1][0:1, :]).astype(bf16)
                    scr[0][...] = hn
                    outs[-1][...] = hn

            lhs = scr[0][...]
        else:
            lhs = a_ref[...]
        o = _dot(lhs, b_ref[...], NT if tb else NN)
        if resid is not None:
            o = o + ins[2][...]
        if relu_grad_of is not None:
            o = o * (2.0 * jnp.maximum(ins[2][...], 0.0))
        outs[0][...] = o.astype(out_dtype)
        if relu2_out:
            r = jnp.maximum(o, 0.0)
            outs[1][...] = (r * r).astype(bf16)

    in_specs = [pl.BlockSpec((k, tm), lambda o, i: (0, ij(o, i)[0])) if ta
                else pl.BlockSpec((tm, k), lambda o, i: (ij(o, i)[0], 0)),
                pl.BlockSpec((tn, k), lambda o, i: (ij(o, i)[1], 0)) if tb
                else pl.BlockSpec((k, tn), lambda o, i: (0, ij(o, i)[1]))]
    args = [a, b]
    if extra is not None:
        in_specs.append(pl.BlockSpec((tm, tn), ij))
        args.append(extra)
    out_blk = pl.BlockSpec((tm, tn), ij)
    out_specs = [out_blk] * (1 + relu2_out)
    out_shape = [jax.ShapeDtypeStruct((m, n), out_dtype)] + [jax.ShapeDtypeStruct((m, n), bf16)] * relu2_out
    if norm_w8 is not None:
        in_specs.append(pl.BlockSpec((8, k), lambda o, i: (0, 0)))
        args.append(norm_w8)
        out_specs.append(pl.BlockSpec((tm, k), lambda o, i: (ij(o, i)[0], 0)))
        out_shape.append(jax.ShapeDtypeStruct((m, k), bf16))
    res = pl.pallas_call(
        body, name=name, grid=grid, in_specs=in_specs, out_specs=tuple(out_specs), out_shape=tuple(out_shape),
        scratch_shapes=[pltpu.VMEM((tm, k), bf16)] if staged else [],
        compiler_params=_params(("parallel", "arbitrary" if staged else "parallel")),
    )(*args)
    return res[0] if len(res) == 1 else res


def _rows(t_rows):
    return _tile(t_rows, (384, 256, 128))


def _rmsnorm(h, w):
    return h * lax.rsqrt(jnp.mean(h * h, axis=1, keepdims=True) + RMS_EPS) * w


def rmsnorm_bwd(h, w8, dhn, dres, *, name):
    t_rows, d = h.shape
    tr = _rows(t_rows)

    def body(h_ref, w_ref, dhn_ref, dres_ref, dh_ref, dw_ref):
        @pl.when(pl.program_id(0) == 0)
        def _():
            dw_ref[...] = jnp.zeros_like(dw_ref)

        _, vjp = jax.vjp(_rmsnorm, h_ref[...], w_ref[0:1, :])
        dh, dw = vjp(dhn_ref[...])
        dh_ref[...] = dh + dres_ref[...]
        dw_ref[0:1, :] += dw

    blk = pl.BlockSpec((tr, d), lambda i: (i, 0))
    wblk = pl.BlockSpec((8, d), lambda i: (0, 0))
    return pl.pallas_call(
        body, name=name, grid=(t_rows // tr,), in_specs=[blk, wblk, blk, blk], out_specs=(blk, wblk),
        out_shape=(jax.ShapeDtypeStruct((t_rows, d), f32), jax.ShapeDtypeStruct((8, d), f32)),
        compiler_params=_params(("arbitrary",)),
    )(h, w8, dhn, dres)


def _merge(pg, ps, pw, la, lb, lc):
    return _sigmoid(la) * pg + _sigmoid(lb) * ps + _sigmoid(lc) * pw


def _merge_specs(t_rows):
    tr = _rows(t_rows)
    blk = pl.BlockSpec((tr, D_MODEL), lambda i: (i, 0))
    gate = [pl.BlockSpec((tr, D_MODEL), functools.partial(lambda i, j: (i, j), j=C_GATE // D_MODEL + j)) for j in range(3)]
    return tr, blk, gate


def merge_fwd(pg, ps, pw, u):
    t_rows = pg.shape[0]
    tr, blk, gate = _merge_specs(t_rows)

    def body(pg_ref, ps_ref, pw_ref, la, lb, lc, o_ref):
        o_ref[...] = _merge(pg_ref[...], ps_ref[...], pw_ref[...], la[...], lb[...], lc[...]).astype(bf16)

    return pl.pallas_call(
        body, name="merge_fwd", grid=(t_rows // tr,), in_specs=[blk, blk, blk] + gate, out_specs=blk,
        out_shape=jax.ShapeDtypeStruct((t_rows, D_MODEL), bf16), compiler_params=_params(("arbitrary",)),
    )(pg, ps, pw, u, u, u)


def merge_bwd(pg, ps, pw, u, dmerged, du):
    t_rows = pg.shape[0]
    tr, blk, gate = _merge_specs(t_rows)

    def body(pg_ref, ps_ref, pw_ref, la, lb, lc, dm_ref, _, dpg_ref, dps_ref, dpw_ref, dl_ref):
        _, vjp = jax.vjp(_merge, pg_ref[...], ps_ref[...], pw_ref[...], la[...], lb[...], lc[...])
        dpg, dps, dpw, dla, dlb, dlc = vjp(dm_ref[...])
        dpg_ref[...] = dpg.astype(bf16)
        dps_ref[...] = dps.astype(bf16)
        dpw_ref[...] = dpw.astype(bf16)
        for j, dl in enumerate((dla, dlb, dlc)):
            dl_ref[:, j * D_MODEL:(j + 1) * D_MODEL] = dl.astype(bf16)

    act = jax.ShapeDtypeStruct((t_rows, D_MODEL), bf16)
    return pl.pallas_call(
        body, name="merge_bwd", grid=(t_rows // tr,), in_specs=[blk, blk, blk] + gate + [blk, ANY],
        out_specs=(blk, blk, blk, pl.BlockSpec((tr, 3 * D_MODEL), lambda i: (i, C_GATE // (3 * D_MODEL)))),
        out_shape=(act, act, act, jax.ShapeDtypeStruct(du.shape, du.dtype)),
        input_output_aliases={7: 3},
        compiler_params=_params(("arbitrary",)),
    )(pg, ps, pw, u, u, u, dmerged, du)


def loss_head(h, w8, target):
    t_rows, d = h.shape
    tr = HEAD_ROWS

    def loss_fn(hb, w, tgt):
        err = _rmsnorm(hb, w) - tgt
        return 0.5 * jnp.sum(err * err) / d

    def body(h_ref, w_ref, t_ref, loss_ref, dh_ref, dw_ref):
        i = pl.program_id(0)

        @pl.when(i == 0)
        def _():
            loss_ref[...] = jnp.zeros_like(loss_ref)
            dw_ref[...] = jnp.zeros_like(dw_ref)
            dh_ref[...] = jnp.zeros_like(dh_ref)

        @pl.when(i > 0)
        def _():
            val, (dh, dw) = jax.value_and_grad(loss_fn, argnums=(0, 1))(h_ref[...], w_ref[0:1, :], t_ref[...])
            loss_ref[...] += val
            dh_ref[...] = dh
            dw_ref[0:1, :] += dw

    blk = pl.BlockSpec((tr, d), lambda i: (i, 0))
    wblk = pl.BlockSpec((8, d), lambda i: (0, 0))
    return pl.pallas_call(
        body, name="loss_head", grid=(t_rows // tr,),
        in_specs=[blk, wblk, pl.BlockSpec((tr, d), lambda i: (jnp.maximum(i - 1, 0), 0))],
        out_specs=(pl.BlockSpec((8, 128), lambda i: (0, 0)), blk, wblk),
        out_shape=(jax.ShapeDtypeStruct((8, 128), f32), jax.ShapeDtypeStruct((t_rows, d), f32),
                   jax.ShapeDtypeStruct((8, d), f32)),
        compiler_params=_params(("arbitrary",)),
    )(h, w8, target)


def adamw(w, m, v, partials, row_off, *, name):
    rows, d = w.shape
    layers = len(partials)
    per = rows // layers
    tr = _row_tile(per, d)
    assert row_off % tr == 0
    off, nblk = row_off // tr, per // tr
    c1 = 1.0 - ADAM_B1 ** ADAM_STEP
    c2 = 1.0 - ADAM_B2 ** ADAM_STEP

    def body(w_ref, m_ref, v_ref, *refs):
        p_refs, (g_ref, d_ref, mo_ref, vo_ref) = refs[:2 * layers], refs[2 * layers:]
        g = p_refs[0][...] + p_refs[1][...]
        for l in range(1, layers):
            g = jnp.where(pl.program_id(0) >= l * nblk, p_refs[2 * l][...] + p_refs[2 * l + 1][...], g)
        m_new = ADAM_B1 * m_ref[...] + (1.0 - ADAM_B1) * g
        v_new = ADAM_B2 * v_ref[...] + (1.0 - ADAM_B2) * (g * g)
        g_ref[...] = g
        d_ref[...] = -ADAM_LR * ((m_new / c1) / (jnp.sqrt(v_new / c2) + ADAM_EPS) + ADAM_WD * w_ref[...])
        mo_ref[...] = m_new
        vo_ref[...] = v_new

    blk = pl.BlockSpec((tr, d), lambda i: (i, 0))
    pblks = [pl.BlockSpec((tr, d), functools.partial(lambda i, l: (off + jnp.clip(i - l * nblk, 0, nblk - 1), 0), l=l))
             for l in range(layers) for _ in range(2)]
    out = jax.ShapeDtypeStruct((rows, d), f32)
    return pl.pallas_call(
        body, name=name, grid=(rows // tr,), in_specs=[blk, blk, blk] + pblks, out_specs=(blk,) * 4,
        out_shape=(out,) * 4, compiler_params=_params(("arbitrary",)),
    )(w, m, v, *[p for pair in partials for p in pair])


def reduce4(parts, *, name, own=None, me=None):
    _, rows, d = parts.shape
    tr = _row_tile(rows, d)

    def body(*refs):
        p_ref, o_ref = refs[0], refs[-1]
        acc = None
        for s in range(4):
            term = p_ref[s].astype(f32)
            if own is not None:
                term = jnp.where(refs[2][0] == s, refs[1][...].astype(f32), term)
            acc = term if acc is None else acc + term
        o_ref[...] = acc

    in_specs = [pl.BlockSpec((4, tr, d), lambda i: (0, i, 0))]
    args = [parts]
    if own is not None:
        in_specs += [pl.BlockSpec((tr, d), lambda i: (i, 0)), pl.BlockSpec(memory_space=pltpu.SMEM)]
        args += [own, me]
    return pl.pallas_call(
        body, name=name, grid=(rows // tr,), in_specs=in_specs,
        out_specs=pl.BlockSpec((tr, d), lambda i: (i, 0)), out_shape=jax.ShapeDtypeStruct((rows, d), f32),
        compiler_params=_params(("arbitrary",)),
    )(*args)


ANY = pl.BlockSpec(memory_space=pl.ANY)
MESH = pl.DeviceIdType.MESH
CHIP_FLIPS = ((0, 1), (1, 0), (1, 1))


def chip_exchange(bufs, scatter, *, name, after=None):
    nb = len(bufs)
    extra = [] if after is None else [after]

    def body(*refs):
        ins, outs = refs[:nb], refs[nb + len(extra):2 * nb + len(extra)]
        send_sems, recv_sems, local_sems = refs[2 * nb + len(extra):]
        x, y, c = lax.axis_index("x"), lax.axis_index("y"), lax.axis_index("c")
        me = 2 * x + y
        local = [pltpu.make_async_copy(ins[j].at[me] if scatter[j] else ins[j], outs[j].at[me], local_sems.at[j])
                 for j in range(nb)]
        for cp in local:
            cp.start()
        sends, recvs = [], []
        for k, (fx, fy) in enumerate(CHIP_FLIPS):
            px = 1 - x if fx else x
            py = 1 - y if fy else y
            chip = 2 * px + py
            for j in range(nb):
                src = ins[j].at[chip] if scatter[j] else ins[j]
                sems = dict(send_sem=send_sems.at[nb * k + j], recv_sem=recv_sems.at[nb * k + j],
                            device_id=(px, py, c), device_id_type=MESH)
                sends.append(pltpu.make_async_remote_copy(src_ref=src, dst_ref=outs[j].at[me], **sems))
                recvs.append(pltpu.make_async_remote_copy(src_ref=src, dst_ref=outs[j].at[chip], **sems))
        for cp in sends:
            cp.start()
        for cp in recvs:
            cp.wait_recv()
        for cp in sends:
            cp.wait_send()
        for cp in local:
            cp.wait()

    out_shape = tuple(jax.ShapeDtypeStruct(b.shape if s else (4,) + b.shape, b.dtype) for b, s in zip(bufs, scatter))
    return pl.pallas_call(
        body, name=name, in_specs=[ANY] * (nb + len(extra)), out_specs=(ANY,) * nb, out_shape=out_shape,
        scratch_shapes=[pltpu.SemaphoreType.DMA((3 * nb,)), pltpu.SemaphoreType.DMA((3 * nb,)),
                        pltpu.SemaphoreType.DMA((nb,))],
        compiler_params=pltpu.CompilerParams(has_side_effects=True),
    )(*bufs, *extra)


def gather_two_level(big, small, *, name):
    half = big.shape[1] // 2

    def body(big_ref, small_ref, obig_ref, osmall_ref, send_sems, recv_sems, local_sems):
        x, y, c = lax.axis_index("x"), lax.axis_index("y"), lax.axis_index("c")
        me = 2 * x + y
        mine = (slice(None), pl.ds(pl.multiple_of(c * half, half), half))
        theirs = (slice(None), pl.ds(pl.multiple_of((1 - c) * half, half), half))
        local = [pltpu.make_async_copy(big_ref, obig_ref.at[me], local_sems.at[0]),
                 pltpu.make_async_copy(small_ref, osmall_ref.at[me], local_sems.at[1])]
        for cp in local:
            cp.start()

        def copy(k, src, dst, to):
            return pltpu.make_async_remote_copy(src_ref=src, dst_ref=dst, send_sem=send_sems.at[k],
                                                recv_sem=recv_sems.at[k], device_id=to, device_id_type=MESH)

        sends, landed, passed, small_in = [], [], [], []
        for k, (fx, fy) in enumerate(CHIP_FLIPS):
            px = 1 - x if fx else x
            py = 1 - y if fy else y
            chip = 2 * px + py
            sends.append(copy(k, big_ref.at[mine], obig_ref.at[(me,) + mine], (px, py, c)))
            landed.append(copy(k, big_ref.at[mine], obig_ref.at[(chip,) + mine], (px, py, c)))
            sends.append(copy(3 + k, small_ref, osmall_ref.at[me], (px, py, c)))
            small_in.append(copy(3 + k, small_ref, osmall_ref.at[chip], (px, py, c)))
            passed.append((copy(6 + k, obig_ref.at[(chip,) + mine], obig_ref.at[(chip,) + mine], (x, y, 1 - c)),
                           copy(6 + k, obig_ref.at[(chip,) + theirs], obig_ref.at[(chip,) + theirs], (x, y, 1 - c))))
        for cp in sends:
            cp.start()
        for k in range(3):
            landed[k].wait_recv()
            passed[k][0].start()
        for k in range(3):
            passed[k][1].wait_recv()
            small_in[k].wait_recv()
        for cp in sends + [p[0] for p in passed]:
            cp.wait_send()
        for cp in local:
            cp.wait()

    return pl.pallas_call(
        body, name=name, in_specs=[ANY, ANY], out_specs=(ANY, ANY),
        out_shape=(jax.ShapeDtypeStruct((4,) + big.shape, big.dtype),
                   jax.ShapeDtypeStruct((4,) + small.shape, small.dtype)),
        scratch_shapes=[pltpu.SemaphoreType.DMA((9,)), pltpu.SemaphoreType.DMA((9,)), pltpu.SemaphoreType.DMA((2,))],
        compiler_params=pltpu.CompilerParams(has_side_effects=True),
    )(big, small)


def sibling_swap(bufs, *, name):
    nb = len(bufs)

    def body(*refs):
        ins, outs, (send_sems, recv_sems) = refs[:nb], refs[nb:2 * nb], refs[2 * nb:]
        peer = (lax.axis_index("x"), lax.axis_index("y"), 1 - lax.axis_index("c"))
        copies = [pltpu.make_async_remote_copy(src_ref=ins[j], dst_ref=outs[j], send_sem=send_sems.at[j],
                                               recv_sem=recv_sems.at[j], device_id=peer, device_id_type=MESH)
                  for j in range(nb)]
        for cp in copies:
            cp.start()
        for cp in copies:
            cp.wait_recv()
        for cp in copies:
            cp.wait_send()

    return pl.pallas_call(
        body, name=name, in_specs=[ANY] * nb, out_specs=(ANY,) * nb,
        out_shape=tuple(jax.ShapeDtypeStruct(b.shape, b.dtype) for b in bufs),
        scratch_shapes=[pltpu.SemaphoreType.DMA((nb,)), pltpu.SemaphoreType.DMA((nb,))],
        compiler_params=pltpu.CompilerParams(has_side_effects=True),
    )(*bufs)


HBM = pl.BlockSpec(memory_space=pltpu.HBM)
SEM = pl.BlockSpec(memory_space=pltpu.SEMAPHORE)
DATAFLOW = pltpu.SideEffectType.DATAFLOW_SIDE_EFFECTING


def _exchange_copies(srcs, lands, send_sems, recv_sems, scatter):
    x, y, c = lax.axis_index("x"), lax.axis_index("y"), lax.axis_index("c")
    me = 2 * x + y
    nb = len(srcs)
    pairs = []
    for k, (fx, fy) in enumerate(CHIP_FLIPS):
        px = 1 - x if fx else x
        py = 1 - y if fy else y
        chip = 2 * px + py
        for j in range(nb):
            src = srcs[j].at[chip] if scatter[j] else srcs[j]
            sems = dict(send_sem=send_sems.at[nb * k + j], recv_sem=recv_sems.at[nb * k + j],
                        device_id=(px, py, c), device_id_type=MESH)
            pairs.append((pltpu.make_async_remote_copy(src_ref=src, dst_ref=lands[j].at[me], **sems),
                          pltpu.make_async_remote_copy(src_ref=src, dst_ref=lands[j].at[chip], **sems)))
    return pairs


def exchange_start(bufs, scatter, after, *, name):
    nb = len(bufs)
    slabs = [b.shape[1:] if s else b.shape for b, s in zip(bufs, scatter)]
    lands = [lax.empty((4,) + shp, b.dtype) for b, shp in zip(bufs, slabs)]

    def body(*refs):
        srcs, zones = refs[:nb], refs[nb:2 * nb]
        send_sems, recv_sems = refs[2 * nb + 1:2 * nb + 3]
        token = refs[-1]
        for send, _ in _exchange_copies(srcs, zones, send_sems, recv_sems, scatter):
            send.start()
        token[...] = jnp.zeros_like(token)

    hbm = lambda a: pltpu.with_memory_space_constraint(a, pltpu.HBM)
    out = pl.pallas_call(
        body, name=name, in_specs=[HBM] * (2 * nb) + [ANY],
        out_specs=(SEM, SEM) + (HBM,) * (2 * nb) + (pl.BlockSpec(memory_space=pltpu.VMEM),),
        out_shape=(pltpu.SemaphoreType.DMA((3 * nb,)), pltpu.SemaphoreType.DMA((3 * nb,)))
        + tuple(pltpu.HBM(a.shape, a.dtype) for a in list(bufs) + lands) + (jax.ShapeDtypeStruct((8, 128), f32),),
        input_output_aliases={i: 2 + i for i in range(2 * nb)},
        compiler_params=pltpu.CompilerParams(has_side_effects=DATAFLOW),
    )(*[hbm(a) for a in list(bufs) + lands], after)
    return (out[:2], out[2:2 + nb], out[2 + nb:2 + 2 * nb], scatter), out[-1]


def exchange_wait(state, after, *, name):
    (send_sems, recv_sems), srcs, lands, scatter = state
    nb = len(srcs)

    def body(*refs):
        src_refs, zones = refs[:nb], refs[nb:2 * nb]
        s_sems, r_sems = refs[2 * nb:2 * nb + 2]
        for send, recv in _exchange_copies(src_refs, zones, s_sems, r_sems, scatter):
            send.wait_send()
            recv.wait_recv()

    out = pl.pallas_call(
        body, name=name, in_specs=[HBM] * (2 * nb) + [SEM, SEM, ANY], out_specs=(HBM,) * (2 * nb),
        out_shape=tuple(pltpu.HBM(a.shape, a.dtype) for a in list(srcs) + list(lands)),
        input_output_aliases={i: i for i in range(2 * nb)},
        compiler_params=pltpu.CompilerParams(has_side_effects=DATAFLOW),
    )(*srcs, *lands, send_sems, recv_sems, after)
    return out[nb:]


BIG = (
    ("w_proj_gdn", 256), ("w_proj_ssd", 256), ("w_proj_swa", 256), ("w_out", 256), ("w_up", 1024), ("w_down", 1024))
BIG_OFF = {}
_o = 0
for _n, _r in BIG:
    BIG_OFF[_n] = _o
    _o += _r
BIG_ROWS = _o
W_IN_SHARD = IN_W // 4

W_NAMES = ('meta_tokens', 'norm1_w', 'w_in', 'gdn_conv_w', 'gdn_a_log', 'gdn_dt_bias', 'gdn_norm_w', 'ssd_conv_w',
           'ssd_conv_b', 'ssd_dt_bias', 'ssd_a_log', 'ssd_d', 'ssd_norm_w', 'swa_sinks', 'w_proj_gdn', 'w_proj_ssd',
           'w_proj_swa', 'w_out', 'norm2_w', 'w_up', 'w_down', 'final_norm_w')
SMALL_NAMES = tuple(n for n in W_NAMES if n not in BIG_OFF and n != "w_in")
SMALL_SHARDED = ("meta_tokens", "gdn_conv_w", "ssd_conv_w")


def _pack_rows(parts, dtype):
    flat = jnp.concatenate([p.reshape(-1).astype(dtype) for p in parts])
    n = -(-flat.shape[0] // 8192) * 8192
    return jnp.pad(flat, (0, n - flat.shape[0])).reshape(-1, D_MODEL)


def _unpack_rows(packed, shapes):
    flat, out, o = packed.reshape(-1), [], 0
    for s in shapes:
        n = 1
        for d in s:
            n *= d
        out.append(flat[o:o + n].reshape(s))
        o += n
    return out


def _split_chips(full, axis):
    s = full.shape
    a = full.reshape(s[:axis] + (4, s[axis] // 4) + s[axis + 1:])
    return jnp.moveaxis(a, axis, 0)


def _join_chips(parts, axis):
    a = jnp.moveaxis(parts, 0, axis)
    s = a.shape
    return a.reshape(s[:axis] + (s[axis] * s[axis + 1],) + s[axis + 2:])


BIG_AXIS = {"w_proj_gdn": 1, "w_proj_ssd": 1, "w_proj_swa": 1, "w_out": 1, "w_up": 2, "w_down": 1}


def _w_in_to_padded(w):
    z = lambda n: jnp.zeros((n,) + w.shape[1:], w.dtype)
    return jnp.concatenate([w[8736:11808], w[4112:7184], w[7200:8736], w[4096:4112], z(112),
                            w[7184:7200], z(112 + C_MID_END - C_SDT - 128), w[0:4096]], axis=0)


def _w_in_from_padded(p):
    return jnp.concatenate([p[C_GQ:IN_WP], p[C_BA:C_BA + 16], p[C_SZ:C_WQ], p[C_SDT:C_SDT + 16], p[C_WQ:C_BA],
                            p[0:C_SZ]], axis=0)


def _row8(v, lane0=0, width=128):
    return jnp.pad(v[None, :], ((0, 7), (lane0, width - lane0 - v.shape[0])))


def _layer_fwd(h, p, l, late=None):
    tag = f"l{l}"
    u, hn = mm(h, p["w_in_t"], tb=True, out_dtype=f32, norm_w8=p["n1"], name=f"mm_in_{tag}")
    yg, stg, tg = gdn_fwd(u, p["gcw"], p["galog"], p["gdtb"], p["gnw"])
    ys, sts = ssd_fwd(u, p["scw"], p["sdtb"], p["salog"], p["sd"], p["snw"])
    yw = swa_fwd(u, p["sink"])
    if late is not None:
        p.update(late(yw))
    pg = mm(yg, p["wpg"], out_dtype=f32, name=f"mm_pg_{tag}")
    ps = mm(ys, p["wps"], out_dtype=f32, name=f"mm_ps_{tag}")
    pw = mm(yw, p["wpw"], out_dtype=f32, name=f"mm_pw_{tag}")
    merged = merge_fwd(pg, ps, pw, u)
    h2 = mm(merged, p["wout"], out_dtype=f32, resid=h, name=f"mm_out_{tag}")
    a, r, hn2 = mm(h2, p["wup"], out_dtype=f32, relu2_out=True, norm_w8=p["n2"], name=f"mm_up_{tag}")
    h3 = mm(r, p["wdown"], out_dtype=f32, resid=h2, name=f"mm_down_{tag}")
    saved = dict(h=h, hn=hn, u=u, yg=yg, stg=stg, tg=tg, ys=ys, sts=sts, yw=yw, pg=pg, ps=ps, pw=pw,
                 merged=merged, h2=h2, hn2=hn2, a=a, r=r)
    return h3, saved


def _layer_bwd(dh3, p, s, l, send_big, send_w_in):
    tag = f"l{l}"
    g = {}

    def wgrad(act, d, name):
        return mm(act, d, ta=True, out_dtype=bf16, name=f"wg_{name}_{tag}")

    da = mm(dh3, p["wdown"], tb=True, out_dtype=bf16, relu_grad_of=s["a"], name=f"dg_down_{tag}")
    g["w_down"] = wgrad(s["r"], dh3, "down")
    dhn2 = mm(da, p["wup"], tb=True, out_dtype=f32, name=f"dg_up_{tag}")
    g["w_up"] = wgrad(s["hn2"], da, "up")
    dh2, g["norm2_w"] = rmsnorm_bwd(s["h2"], p["n2"], dhn2, dh3, name=f"norm2_bwd_{tag}")
    dmerged = mm(dh2, p["wout"], tb=True, out_dtype=f32, name=f"dg_out_{tag}")
    g["w_out"] = wgrad(s["merged"], dh2, "out")
    du = lax.empty((dh3.shape[0], IN_WP), bf16)
    dpg, dps, dpw, du = merge_bwd(s["pg"], s["ps"], s["pw"], s["u"], dmerged, du)
    dyg = mm(dpg, p["wpg"], tb=True, out_dtype=f32, name=f"dg_pg_{tag}")
    dys = mm(dps, p["wps"], tb=True, out_dtype=f32, name=f"dg_ps_{tag}")
    dyw = mm(dpw, p["wpw"], tb=True, out_dtype=f32, name=f"dg_pw_{tag}")
    g["w_proj_gdn"] = wgrad(s["yg"], dpg, "pg")
    g["w_proj_ssd"] = wgrad(s["ys"], dps, "ps")
    g["w_proj_swa"] = wgrad(s["yw"], dpw, "pw")
    sent = send_big(jnp.concatenate([_split_chips(g.pop(n), BIG_AXIS[n] - 1).reshape(4, r, D_MODEL)
                                     for n, r in BIG], axis=1))

    (du, dba, dtq, dtk, dtv, g["gdn_a_log"], g["gdn_dt_bias"], g["gdn_norm_w"]) = gdn_bwd(
        s["u"], p["gcw"] + sent, p["galog"], p["gdtb"], p["gnw"], s["stg"], s["tg"], dyg, du)
    g["gdn_conv_w"] = jnp.concatenate([dtq, dtk, dtv], axis=1)[:4]
    (du, ddt, dtx, dtb, dtc, g["ssd_dt_bias"], g["ssd_a_log"], g["ssd_d"], g["ssd_norm_w"]) = ssd_bwd(
        s["u"], p["scw"], p["sdtb"], p["salog"], p["sd"], p["snw"], s["sts"], dys, du)
    dconv = jnp.concatenate([dtx, dtb, dtc], axis=1)
    g["ssd_conv_w"], g["ssd_conv_b"] = dconv[:4], dconv[4]
    du, g["swa_sinks"] = swa_bwd(s["u"], p["sink"], dyw, du)
    mid = jnp.concatenate([dba[0].astype(bf16), ddt.astype(bf16),
                           jnp.zeros((du.shape[0], C_MID_END - C_SDT - 128), bf16)], axis=1)
    du = lax.dynamic_update_slice(du, mid, (0, C_BA))
    sent = send_w_in(_w_in_from_padded(wgrad(du, s["hn"], "in")).reshape(4, W_IN_SHARD, D_MODEL))
    dhn = mm(du, p["w_in_t"], out_dtype=f32, name=f"dg_in_{tag}")
    dh, g["norm1_w"] = rmsnorm_bwd(s["h"], p["n1"] + sent, dhn, dh2, name=f"norm1_bwd_{tag}")
    return dh, g


def kernel(x, meta_tokens, norm1_w, w_in, gdn_conv_w, gdn_a_log, gdn_dt_bias, gdn_norm_w, ssd_conv_w, ssd_conv_b, ssd_dt_bias, ssd_a_log, ssd_d, ssd_norm_w, swa_sinks, w_proj_gdn, w_proj_ssd, w_proj_swa, w_out, norm2_w, w_up, w_down, final_norm_w, loss_target, m_meta_tokens, m_norm1_w, m_w_in, m_gdn_conv_w, m_gdn_a_log, m_gdn_dt_bias, m_gdn_norm_w, m_ssd_conv_w, m_ssd_conv_b, m_ssd_dt_bias, m_ssd_a_log, m_ssd_d, m_ssd_norm_w, m_swa_sinks, m_w_proj_gdn, m_w_proj_ssd, m_w_proj_swa, m_w_out, m_norm2_w, m_w_up, m_w_down, m_final_norm_w, v_meta_tokens, v_norm1_w, v_w_in, v_gdn_conv_w, v_gdn_a_log, v_gdn_dt_bias, v_gdn_norm_w, v_ssd_conv_w, v_ssd_conv_b, v_ssd_dt_bias, v_ssd_a_log, v_ssd_d, v_ssd_norm_w, v_swa_sinks, v_w_proj_gdn, v_w_proj_ssd, v_w_proj_swa, v_w_out, v_norm2_w, v_w_up, v_w_down, v_final_norm_w):
    given = dict(locals())
    depth = norm1_w.shape[0]
    me = 2 * lax.axis_index("x") + lax.axis_index("y")

    me1 = jnp.reshape(me, (1,)).astype(jnp.int32)
    w_in_t = jnp.swapaxes(w_in, 1, 2)

    def weight_slabs(l):
        return (w_in_t[l].astype(bf16),
                jnp.concatenate([given[n][l].reshape(-1, D_MODEL).astype(bf16) for n, _ in BIG]))

    slabs = [weight_slabs(l) for l in range(depth)]
    wsmall = _pack_rows([given[n] for n in SMALL_SHARDED], f32)
    ga0, gsmall = gather_two_level(slabs[0][0], wsmall, name="gather_first")
    gathers, started = {}, jnp.zeros((), f32)
    for l in range(depth):
        for j in range(2):
            if (l, j) != (0, 0):
                gathers[l, j], token = exchange_start([slabs[l][j]], (False,), gsmall, name=f"gather_start_l{l}_{j}")
                started = started + token[0, 0]
    shard_shapes = [given[n].shape for n in SMALL_SHARDED]
    per_chip = [_unpack_rows(gsmall[s], shard_shapes) for s in range(4)]
    full = {n: jnp.concatenate([per_chip[s][i] for s in range(4)], axis=-1) for i, n in enumerate(SMALL_SHARDED)}

    def landed(l, j, after):
        (zone,) = exchange_wait(gathers[l, j], after, name=f"gather_wait_l{l}_{j}")
        return lax.dynamic_update_slice(zone, slabs[l][j][None], (me, 0, 0))

    def first_operands(l, ga, order):
        return dict(
            n1=_row8(norm1_w[l], width=D_MODEL) + order, n2=_row8(norm2_w[l], width=D_MODEL),
            w_in_t=_w_in_to_padded(ga.reshape(IN_W, D_MODEL)),
            gcw=jnp.pad(full["gdn_conv_w"][l], ((0, 4), (0, 0))),
            galog=_row8(gdn_a_log[l], 8), gdtb=_row8(gdn_dt_bias[l], 8), gnw=_row8(gdn_norm_w[l]),
            scw=jnp.pad(jnp.concatenate([full["ssd_conv_w"][l], ssd_conv_b[l][None]], axis=0), ((0, 3), (0, 0))),
            sdtb=_row8(ssd_dt_bias[l]), salog=_row8(ssd_a_log[l]), sd=_row8(ssd_d[l]),
            snw=_row8(ssd_norm_w[l], width=D_MODEL), sink=_row8(swa_sinks[l]))

    def late_operands(l, after):
        gb = landed(l, 1, after)
        w = {}
        for n, r in BIG:
            parts = gb[:, BIG_OFF[n]:BIG_OFF[n] + r].reshape((4,) + given[n].shape[1:])
            w[n] = _join_chips(parts, BIG_AXIS[n] - 1)
        return dict(wpg=w["w_proj_gdn"], wps=w["w_proj_ssd"], wpw=w["w_proj_swa"], wout=w["w_out"],
                    wup=w["w_up"], wdown=w["w_down"])

    h = jnp.concatenate([jnp.zeros((PAD, D_MODEL), f32), full["meta_tokens"], x[0]], axis=0)
    layers, saved = [], []
    for l in range(depth):
        p = first_operands(0, ga0, started) if l == 0 else first_operands(l, landed(l, 0, h), 0.0)
        h, s = _layer_fwd(h, p, l, late=functools.partial(late_operands, l))
        layers.append(p)
        saved.append(s)
    loss8, dh, dfw8 = loss_head(h, _row8(final_norm_w, width=D_MODEL), loss_target[0])
    grads = {"final_norm_w": dfw8[0]}
    per_layer, grad_slabs, scatters = [None] * depth, {}, {}

    def send(l, j, slab):
        grad_slabs[l, j] = slab
        scatters[l, j], token = exchange_start([slab], (True,), loss8, name=f"scatter_start_l{l}_{j}")
        return token[0, 0]

    for l in reversed(range(depth)):
        dh, per_layer[l] = _layer_bwd(dh, layers[l], saved[l], l, functools.partial(send, l, 1),
                                      functools.partial(send, l, 0))
    grad_x = dh[HEAD_ROWS:][None]
    grads["meta_tokens"] = dh[PAD:HEAD_ROWS]
    lane = {"gdn_a_log": (8, 8), "gdn_dt_bias": (8, 8), "gdn_norm_w": (0, 128), "ssd_dt_bias": (0, 16),
            "ssd_a_log": (0, 16), "ssd_d": (0, 16), "swa_sinks": (0, 16)}
    for n in per_layer[0]:
        parts = [per_layer[l][n] for l in range(depth)]
        if n in lane:
            parts = [q[0, lane[n][0]:lane[n][0] + lane[n][1]] for q in parts]
        elif n in ("norm1_w", "norm2_w", "ssd_norm_w"):
            parts = [q[0] for q in parts]
        grads[n] = jnp.stack(parts)
    loss = lax.psum(loss8[0, 0], ("x", "y", "c"))

    gs = _pack_rows([grads[n] for n in SMALL_NAMES], f32)
    def chip_sum(l, j, after):
        (zone,) = exchange_wait(scatters[l, j], after, name=f"scatter_wait_l{l}_{j}")
        own = lax.dynamic_index_in_dim(grad_slabs[l, j], me, 0, keepdims=False)
        return reduce4(zone, own=own, me=me1, name=f"sum_chips_l{l}_{j}")

    early = [(l, j) for l in range(depth) for j in range(2) if (l, j) != (0, 0)]
    mine = {lj: chip_sum(*lj, dh) for lj in early}
    sibs = dict(zip(early, sibling_swap([mine[lj] for lj in early], name="swap_cores_early")))
    out = {}
    for n, r in BIG:
        shp = given[n].shape
        res = adamw(*[given[pre + n].reshape(depth * r, D_MODEL) for pre in ("", "m_", "v_")],
                    [(mine[l, 1], sibs[l, 1]) for l in range(depth)], BIG_OFF[n], name=f"adamw_{n}")
        out[n] = [a.reshape(shp) for a in res]
    mine[0, 0] = chip_sum(0, 0, res[1])
    (rs,) = chip_exchange([gs], (False,), after=mine[0, 0], name="gather_small_grads")
    ps_ = reduce4(rs, name="sum_chips_small")
    sibs[0, 0], ss = sibling_swap([mine[0, 0], ps_], name="swap_cores_last")
    res = adamw(*[jnp.swapaxes(given[pre + "w_in"], 1, 2).reshape(depth * W_IN_SHARD, D_MODEL)
                  for pre in ("", "m_", "v_")],
                [(mine[l, 0], sibs[l, 0]) for l in range(depth)], 0, name="adamw_w_in")
    out["w_in"] = [jnp.swapaxes(a.reshape(w_in_t.shape), 1, 2) for a in res]
    full_shapes = [grads[n].shape for n in SMALL_NAMES]
    mine_s, sib_s = _unpack_rows(ps_, full_shapes), _unpack_rows(ss, full_shapes)

    def local(parts):
        loc = []
        for n, a in zip(SMALL_NAMES, parts):
            if n in SMALL_SHARDED:
                sz = a.shape[-1] // 4
                a = lax.dynamic_slice_in_dim(a, me * sz, sz, axis=a.ndim - 1)
            loc.append(a)
        return _pack_rows(loc, f32)

    res = adamw(_pack_rows([given[n] for n in SMALL_NAMES], f32), _pack_rows([given["m_" + n] for n in SMALL_NAMES], f32),
                _pack_rows([given["v_" + n] for n in SMALL_NAMES], f32), [(local(mine_s), local(sib_s))], 0,
                name="adamw_small")
    local_shapes = [given[n].shape for n in SMALL_NAMES]
    unpacked = [_unpack_rows(a, local_shapes) for a in res]
    for i, n in enumerate(SMALL_NAMES):
        out[n] = [unpacked[j][i] for j in range(4)]

    return (loss, grad_x) + tuple(out[n][j] for j in range(4) for n in W_NAMES)
```

```python
import functools

import jax
import jax.numpy as jnp
from jax import lax
from jax.experimental import pallas as pl
from jax.experimental.pallas import tpu as pltpu

f32 = jnp.float32
bf16 = jnp.bfloat16

D_MODEL = 1024
N_META = 16
PAD = 112
HEAD_ROWS = PAD + N_META
RMS_EPS = 1e-6
L2_EPS = 1e-6
D_FF = 4 * D_MODEL

GDN_HEADS = 8
GDN_D = 128
GDN_CHUNK = 64
SSD_HEADS = 16
SSD_P = 64
SSD_GROUPS = 4
SSD_HPG = 4
SSD_N = 128
SSD_CHUNK = 128
SWA_Q_HEADS = 16
SWA_KV_HEADS = 4
SWA_REP = 4
SWA_D = 64
SWA_W = 128

C_GATE = 0
C_SZ, C_SX, C_SB, C_SC = 3072, 4096, 5120, 5632
C_WQ, C_WK, C_WV = 6144, 7168, 7424
C_BA = 7680
C_SDT = 7808
C_MID_END = 8192
C_GQ, C_GK, C_GV, C_GG = 8192, 9216, 10240, 11264
IN_WP = 12288
IN_W = 11808

ADAM_LR, ADAM_B1, ADAM_B2, ADAM_EPS, ADAM_WD, ADAM_STEP = 0.001, 0.9, 0.999, 1e-08, 0.01, 10

VMEM_LIMIT = 56 * 1024 * 1024
BLOCK_BYTES = 3 << 19
MM_OPERAND_BYTES = 9 << 20
MM_RESIDENT_BYTES = 13 << 20

NN = (((1,), (0,)), ((), ()))
NT = (((1,), (1,)), ((), ()))
TN = (((0,), (0,)), ((), ()))


def _dot(a, b, dims=NN):
    return lax.dot_general(a.astype(bf16), b.astype(bf16), dims, preferred_element_type=f32)


def _dotx(a, b, dims=NN):
    return lax.dot_general(a, b, dims, preferred_element_type=f32, precision=lax.Precision.HIGH)


def _iota(shape, axis):
    return lax.broadcasted_iota(jnp.int32, shape, axis)


def _softplus(x):
    return jnp.maximum(x, 0.0) + jnp.log1p(jnp.exp(-jnp.abs(x)))


_sigmoid = jax.nn.sigmoid


def _silu(x):
    return x * _sigmoid(x)


def _params(sem):
    return pltpu.CompilerParams(dimension_semantics=sem, vmem_limit_bytes=VMEM_LIMIT)


@functools.partial(jax.custom_vjp, nondiff_argnums=(1,))
def _window(x_ext, off):
    if off == 8:
        return x_ext[8:]
    return pltpu.roll(x_ext, 8 - off, 0)[8:]


def _window_fwd(x_ext, off):
    return _window(x_ext, off), None


def _window_bwd(off, _, g):
    n, w = g.shape
    g_ext = jnp.concatenate([jnp.zeros((8, w), g.dtype), g], axis=0)
    if off == 8:
        return (g_ext,)
    return (pltpu.roll(g_ext, n + off, 0),)


_window.defvjp(_window_fwd, _window_bwd)


def _conv4(x, halo, taps):
    x_ext = jnp.concatenate([halo, x], axis=0)
    y = taps[3] * x
    for j in range(3):
        y = y + taps[j] * _window(x_ext, 5 + j)
    return y


def _blockinv_impl(a):
    n = a.shape[0]
    ri, ci = _iota((n, n), 0), _iota((n, n), 1)
    t = (ri == ci).astype(f32) - jnp.where(((ri >> 1) == (ci >> 1)) & (ri > ci), a, 0.0)
    k = 1
    while (1 << k) < n:
        sel = ((ri >> (k + 1)) == (ci >> (k + 1))) & (((ri >> k) & 1) == 1) & (((ci >> k) & 1) == 0)
        o = jnp.where(sel, a, 0.0)
        t = t - _dotx(_dotx(t, o), t)
        k += 1
    return t


@jax.custom_vjp
def _blockinv(a):
    return _blockinv_impl(a)


def _blockinv_fwd(a):
    t = _blockinv_impl(a)
    return t, t


def _blockinv_bwd(t, dt):
    return (-_dotx(_dotx(t, dt, TN), t, NT),)


_blockinv.defvjp(_blockinv_fwd, _blockinv_bwd)


@jax.custom_vjp
def _blockinv_given(a, t):
    return t


_blockinv_given.defvjp(lambda a, t: (t, t), lambda t, dt: _blockinv_bwd(t, dt) + (jnp.zeros_like(t),))


def _scan_rows(x, reverse):
    n = x.shape[0]
    row = _iota(x.shape, 0)
    s = 1
    while s < n:
        if reverse:
            x = x + jnp.where(row < n - s, pltpu.roll(x, n - s, 0), 0.0)
        else:
            x = x + jnp.where(row >= s, pltpu.roll(x, s, 0), 0.0)
        s *= 2
    return x


@jax.custom_vjp
def _cumsum_rows(x):
    return _scan_rows(x, False)


_cumsum_rows.defvjp(lambda x: (_scan_rows(x, False), None), lambda _, g: (_scan_rows(g, True),))


def _gdn_act(xq, xk, xv, hq, hk, hv, tq, tk, tv):
    return _silu(_conv4(xq, hq, tq)), _silu(_conv4(xk, hk, tk)), _silu(_conv4(xv, hv, tv))


def _gdn_core(q, k, v, gate, mb, mg, mr, s, t_given, beta16, g16, gam16, gam16_t, nw):
    c = GDN_CHUNK
    q = q * lax.rsqrt(jnp.sum(q * q, axis=1, keepdims=True) + L2_EPS) * (GDN_D ** -0.5)
    k = k * lax.rsqrt(jnp.sum(k * k, axis=1, keepdims=True) + L2_EPS)

    pick = lambda x, m: jnp.sum(x * m, axis=1, keepdims=True)
    beta = pick(beta16, mb)
    g = jnp.broadcast_to(pick(g16, mg), (c, GDN_D))
    gam1 = pick(gam16, mg)
    gam = jnp.broadcast_to(gam1, (c, GDN_D))
    gam_j = jnp.broadcast_to(jnp.sum(gam16_t * mr, axis=0, keepdims=True), (c, c))

    ri, ci = _iota((c, c), 0), _iota((c, c), 1)
    incl = ci <= ri
    decay = jnp.where(incl, jnp.exp(jnp.where(incl, jnp.broadcast_to(gam1, (c, c)) - gam_j, 0.0)), 0.0)

    kb = k * beta
    a = jnp.where(ci < ri, _dot(kb, k, NT) * decay, 0.0)
    t = _blockinv(a) if t_given is None else _blockinv_given(a, t_given)
    egam = jnp.exp(gam)
    u = _dotx(t, v * beta)
    w = _dotx(t, kb * egam)
    attn = _dot(q, k, NT) * decay
    gl = jnp.sum(g, axis=0, keepdims=True)
    kt = k * jnp.exp(gl - gam)
    v_new = u - _dot(w, s)
    o = _dot(q * egam, s) + _dot(attn, v_new)
    s_out = s * jnp.exp(gl) + _dot(kt, v_new, TN)

    y = o * lax.rsqrt(jnp.mean(o * o, axis=1, keepdims=True) + RMS_EPS) * nw * _silu(gate)
    return y, s_out, t


def _gdn_chunk(q, k, v, gate, s, t_given, ba, alog, dtb, nw, *, masks, row0):
    valid = (row0 + _iota((GDN_CHUNK, 1), 0)) >= PAD
    beta16 = jnp.where(valid, _sigmoid(ba), 0.0)
    g16 = jnp.where(valid, -jnp.exp(alog) * _softplus(ba + dtb), 0.0)
    gam16 = _cumsum_rows(g16)
    core = jax.vmap(_gdn_core, in_axes=(0,) * 8 + (None if t_given is None else 0,) + (None,) * 5)
    y, s_out, t = core(q, k, v, gate, *masks, s, t_given, beta16, g16, gam16, gam16.T, nw)
    return (y, s_out, t) if t_given is None else (y, s_out)


def _gdn_specs(hb, nc, rev):
    w = hb * GDN_D

    def cidx(c):
        return (nc - 1 - c) if rev else c

    def col(base):
        return pl.BlockSpec((GDN_CHUNK, w), lambda h, c: (cidx(c), base // w + h))

    def halo(base):
        return pl.BlockSpec((8, w), lambda h, c: (jnp.maximum(cidx(c) * (GDN_CHUNK // 8) - 1, 0), base // w + h))

    def taps(base):
        return pl.BlockSpec((8, w), lambda h, c: (0, base // w + h))

    ba = pl.BlockSpec((GDN_CHUNK, 128), lambda h, c: (cidx(c), C_BA // 128))
    row = pl.BlockSpec((8, 128), lambda h, c: (0, 0))
    y = pl.BlockSpec((GDN_CHUNK, w), lambda h, c: (cidx(c), h))
    st = pl.BlockSpec((1, hb, GDN_D, GDN_D), lambda h, c: (cidx(c), h, 0, 0))
    in_specs = [col(C_GQ), col(C_GK), col(C_GV), halo(C_GQ), halo(C_GK), halo(C_GV), col(C_GG), ba,
                taps(0), taps(1024), taps(2048), row, row, row]
    return in_specs, y, st, taps, row, col, ba


def _gdn_load(refs, first):
    xq, xk, xv, hq, hk, hv, gate, ba, tq, tk, tv, alog, dtb, nw = refs

    def halo(r):
        return jnp.where(first, 0.0, r[...])

    def taps(r):
        return tuple(r[j:j + 1, :] for j in range(4))

    act = (xq[...], xk[...], xv[...], halo(hq), halo(hk), halo(hv), taps(tq), taps(tk), taps(tv))
    return act, gate[...], (ba[...], alog[0:1, :], dtb[0:1, :], nw[0:1, :])


def _heads(a, hb):
    return jnp.stack([a[:, i * GDN_D:(i + 1) * GDN_D] for i in range(hb)])


def _wide(a):
    return jnp.concatenate([a[i] for i in range(a.shape[0])], axis=1)


def _head_masks(hblk, hb):
    head = hblk * hb + _iota((hb, 1, 128), 0)
    lane = _iota((hb, 1, 128), 2)
    rows = (_iota((hb, 128, 1), 1) == hblk * hb + _iota((hb, 128, 1), 0) + 8).astype(f32)
    return (lane == head).astype(f32), (lane == head + 8).astype(f32), rows


def gdn_fwd(u, conv_w8, alog8, dtb8, nw8, *, hb=8):
    t_rows = u.shape[0]
    nc = t_rows // GDN_CHUNK
    in_specs, y_spec, st_spec, *_ = _gdn_specs(hb, nc, False)

    def body(*refs):
        ins, (y_ref, st_ref, t_ref), (s_scr,) = refs[:14], refs[14:17], refs[17:]
        hblk, c = pl.program_id(0), pl.program_id(1)

        @pl.when(c == 0)
        def _():
            s_scr[...] = jnp.zeros_like(s_scr)

        act, gate, shared = _gdn_load(ins, c == 0)
        s = s_scr[...]
        st_ref[0] = s
        qa, ka, va = _gdn_act(*act)
        y, s_new, t = _gdn_chunk(_heads(qa, hb), _heads(ka, hb), _heads(va, hb), _heads(gate, hb), s, None, *shared,
                                 masks=_head_masks(hblk, hb), row0=c * GDN_CHUNK)
        y_ref[...] = _wide(y).astype(bf16)
        t_ref[0] = t
        s_scr[...] = s_new

    return pl.pallas_call(
        body, name="gdn_fwd", grid=(GDN_HEADS // hb, nc),
        in_specs=in_specs,
        out_specs=(y_spec, st_spec, pl.BlockSpec((1, hb, GDN_CHUNK, GDN_CHUNK), lambda h, c: (c, h, 0, 0))),
        out_shape=(jax.ShapeDtypeStruct((t_rows, D_MODEL), bf16),
                   jax.ShapeDtypeStruct((nc, GDN_HEADS, GDN_D, GDN_D), f32),
                   jax.ShapeDtypeStruct((nc, GDN_HEADS, GDN_CHUNK, GDN_CHUNK), f32)),
        scratch_shapes=[pltpu.VMEM((hb, GDN_D, GDN_D), f32)],
        compiler_params=_params(("arbitrary", "arbitrary")),
    )(u, u, u, u, u, u, u, u, conv_w8, conv_w8, conv_w8, alog8, dtb8, nw8)


def gdn_bwd(u, conv_w8, alog8, dtb8, nw8, states, tinv, dy, du):
    t_rows = u.shape[0]
    nc = t_rows // GDN_CHUNK
    hb = GDN_HEADS
    w = hb * GDN_D
    in_specs, y_spec, st_spec, taps, row, col, ba = _gdn_specs(hb, nc, True)
    nhb = GDN_HEADS // hb

    def body(*refs):
        ins, st_ref, t_ref, dy_ref = refs[:14], refs[14], refs[15], refs[16]
        du_ref, dba_ref, dtq_ref, dtk_ref, dtv_ref, dalog_ref, ddtb_ref, dnw_ref = refs[18:26]
        ds_scr, dh_scr = refs[26:]
        hblk, cc = pl.program_id(0), pl.program_id(1)
        c = nc - 1 - cc

        @pl.when(cc == 0)
        def _():
            ds_scr[...] = jnp.zeros_like(ds_scr)
            dh_scr[...] = jnp.zeros_like(dh_scr)
            dtq_ref[...] = jnp.zeros_like(dtq_ref)
            dtk_ref[...] = jnp.zeros_like(dtk_ref)
            dtv_ref[...] = jnp.zeros_like(dtv_ref)

        @pl.when((cc == 0) & (hblk == 0))
        def _():
            dalog_ref[...] = jnp.zeros_like(dalog_ref)
            ddtb_ref[...] = jnp.zeros_like(ddtb_ref)
            dnw_ref[...] = jnp.zeros_like(dnw_ref)

        act, gate, shared = _gdn_load(ins, c == 0)
        (qa, ka, va), vjp_act = jax.vjp(_gdn_act, *act)
        chunk = functools.partial(_gdn_chunk, masks=_head_masks(hblk, hb), row0=c * GDN_CHUNK)
        _, vjp_core = jax.vjp(chunk, _heads(qa, hb), _heads(ka, hb), _heads(va, hb), _heads(gate, hb), st_ref[0],
                              t_ref[0], *shared)
        dqa, dka, dva, dgate, ds, _, dba, dalog, ddtb, dnw = vjp_core(
            (_heads(dy_ref[...].astype(f32), hb), ds_scr[...]))
        ds_scr[...] = ds
        dxq, dxk, dxv, dhq, dhk, dhv, dtq, dtk, dtv = vjp_act((_wide(dqa), _wide(dka), _wide(dva)))
        zeros = jnp.zeros((GDN_CHUNK - 8, w), f32)
        for j, (dx, dh) in enumerate(((dxq, dhq), (dxk, dhk), (dxv, dhv))):
            du_ref[:, j * w:(j + 1) * w] = (dx + jnp.concatenate([zeros, dh_scr[j]], axis=0)).astype(bf16)
            dh_scr[j] = dh
        du_ref[:, 3 * w:4 * w] = _wide(dgate).astype(bf16)
        dba_ref[0] = dba
        for dt_ref, dtaps in ((dtq_ref, dtq), (dtk_ref, dtk), (dtv_ref, dtv)):
            for j in range(4):
                dt_ref[j:j + 1, :] += dtaps[j]
        dalog_ref[0:1, :] += dalog
        ddtb_ref[0:1, :] += ddtb
        dnw_ref[0:1, :] += dnw

    out_specs = (pl.BlockSpec((GDN_CHUNK, 4 * w), lambda h, c: (nc - 1 - c, C_GQ // (4 * w))),
                 pl.BlockSpec((1, GDN_CHUNK, 128), lambda h, c: (h, nc - 1 - c, 0)),
                 taps(0), taps(0), taps(0), row, row, row)
    out_shape = (jax.ShapeDtypeStruct(du.shape, du.dtype),
                 jax.ShapeDtypeStruct((nhb, t_rows, 128), f32),
                 jax.ShapeDtypeStruct((8, D_MODEL), f32), jax.ShapeDtypeStruct((8, D_MODEL), f32),
                 jax.ShapeDtypeStruct((8, D_MODEL), f32),
                 jax.ShapeDtypeStruct((8, 128), f32), jax.ShapeDtypeStruct((8, 128), f32), jax.ShapeDtypeStruct((8, 128), f32))
    return pl.pallas_call(
        body, name="gdn_bwd", grid=(nhb, nc),
        in_specs=in_specs + [st_spec, pl.BlockSpec((1, hb, GDN_CHUNK, GDN_CHUNK), lambda h, c: (nc - 1 - c, h, 0, 0)),
                             y_spec, ANY],
        out_specs=out_specs, out_shape=out_shape, input_output_aliases={17: 0},
        scratch_shapes=[pltpu.VMEM((hb, GDN_D, GDN_D), f32), pltpu.VMEM((3, 8, w), f32)],
        compiler_params=_params(("arbitrary", "arbitrary")),
    )(u, u, u, u, u, u, u, u, conv_w8, conv_w8, conv_w8, alog8, dtb8, nw8, states, tinv, dy, du)


def _ssd_act(xs_r, b_r, c_r, hx, hbm, hcm, tx, tb, tc, bx, bb, bc, *, row0):
    valid = (row0 + _iota((SSD_CHUNK, 1), 0)) >= PAD
    act = lambda x, h, t, b: jnp.where(valid, _silu(_conv4(x, h, t) + b), 0.0)
    return act(xs_r, hx, tx, bx), act(b_r, hbm, tb, bb), act(c_r, hcm, tc, bc)


def _ssd_core(xs, bm, cm, z, nw, lanes, rows, h, dtp16, adt16, acum16, acum16_t, dsk):
    n = SSD_CHUNK
    pick = lambda x, m: jnp.sum(x * m, axis=1, keepdims=True)
    lane_r = _iota((1, 256), 1) >> 6

    def per_lane(x16):
        cols = [pick(x16, lanes[r]) for r in range(SSD_HPG)]
        return cols, jnp.concatenate([jnp.broadcast_to(c, (c.shape[0], SSD_P)) for c in cols], axis=1)

    _, dtp = per_lane(dtp16)
    _, adt = per_lane(adt16)
    ccols, acum = per_lane(acum16)
    _, dlane = per_lane(dsk)

    ri, ci = _iota((n, n), 0), _iota((n, n), 1)
    incl = ci <= ri
    al = jnp.sum(adt, axis=0, keepdims=True)
    xdt = xs * dtp
    cb = _dot(cm, bm, NT)
    y = _dot(cm, h) * jnp.exp(acum) + dlane * xs
    for r in range(SSD_HPG):
        ai = jnp.broadcast_to(ccols[r], (n, n))
        aj = jnp.broadcast_to(jnp.sum(acum16_t * rows[r], axis=0, keepdims=True), (n, n))
        lm = jnp.where(incl, jnp.exp(jnp.where(incl, ai - aj, 0.0)), 0.0)
        y = y + _dot(cb * lm, jnp.where(lane_r == r, xdt, 0.0))
    h_out = h * jnp.exp(al) + _dot(bm, jnp.exp(al - acum) * xdt, TN)
    y = y * _silu(z)
    y = y * lax.rsqrt(jnp.mean(y * y, axis=1, keepdims=True) + RMS_EPS) * nw
    return y, h_out


def _ssd_chunk(xs, bm, cm, z, nw, h, dt, dtb, alog, dsk, *, row0):
    valid = (row0 + _iota((SSD_CHUNK, 1), 0)) >= PAD
    dtp16 = jnp.where(valid, _softplus(dt + dtb), 0.0)
    adt16 = -jnp.exp(alog) * dtp16
    acum16 = _cumsum_rows(adt16)
    lanes = tuple((_iota((SSD_GROUPS, 1, 128), 2) == _iota((SSD_GROUPS, 1, 128), 0) * SSD_HPG + r).astype(f32)
                  for r in range(SSD_HPG))
    rows = tuple((_iota((SSD_GROUPS, 128, 1), 1) == _iota((SSD_GROUPS, 128, 1), 0) * SSD_HPG + r).astype(f32)
                 for r in range(SSD_HPG))
    core = jax.vmap(_ssd_core, in_axes=(0,) * 8 + (None,) * 5)
    return core(xs, bm, cm, z, nw, lanes, rows, h, dtp16, adt16, acum16, acum16.T, dsk)


def _ssd_specs(nc, rev):
    n = SSD_CHUNK

    def cidx(c):
        return (nc - 1 - c) if rev else c

    def col(base, w):
        return pl.BlockSpec((n, w), lambda c: (cidx(c), base // w))

    def halo(base, w):
        return pl.BlockSpec((8, w), lambda c: (jnp.maximum(cidx(c) * (n // 8) - 1, 0), base // w))

    def taps(base, w):
        return pl.BlockSpec((8, w), lambda c: (0, base // w))

    row = pl.BlockSpec((8, 128), lambda c: (0, 0))
    in_specs = [col(C_SX, 1024), col(C_SB, 512), col(C_SC, 512), halo(C_SX, 1024), halo(C_SB, 512), halo(C_SC, 512),
                col(C_SZ, 1024), col(C_SDT, 128), taps(0, 1024), taps(1024, 512), taps(1536, 512), row, row, row,
                taps(0, 1024)]
    y = pl.BlockSpec((n, D_MODEL), lambda c: (cidx(c), 0))
    st = pl.BlockSpec((1, SSD_GROUPS, SSD_N, 256), lambda c: (cidx(c), 0, 0, 0))
    return in_specs, y, st, col, taps, row


def _ssd_load(refs, first):
    xs, bm, cm, hx, hbm, hcm, z, dt, tx, tb, tc, dtb, alog, dsk, nw = refs

    def halo(r):
        return jnp.where(first, 0.0, r[...])

    def taps(r):
        return tuple(r[j:j + 1, :] for j in range(4))

    act = (xs[...], bm[...], cm[...], halo(hx), halo(hbm), halo(hcm), taps(tx), taps(tb), taps(tc),
           tx[4:5, :], tb[4:5, :], tc[4:5, :])
    return act, (z[...], nw[0:1, :]), (dt[...], dtb[0:1, :], alog[0:1, :], dsk[0:1, :])


def _groups(a, w):
    return jnp.stack([a[:, i * w:(i + 1) * w] for i in range(SSD_GROUPS)])


def ssd_fwd(u, conv_w8, dtb8, alog8, d8, nw8):
    t_rows = u.shape[0]
    nc = t_rows // SSD_CHUNK
    in_specs, y_spec, st_spec, *_ = _ssd_specs(nc, False)

    def body(*refs):
        ins, (y_ref, st_ref), (h_scr,) = refs[:15], refs[15:17], refs[17:]
        c = pl.program_id(0)

        @pl.when(c == 0)
        def _():
            h_scr[...] = jnp.zeros_like(h_scr)

        act, (z, nw), shared = _ssd_load(ins, c == 0)
        h = h_scr[...]
        st_ref[0] = h
        xs, bm, cm = _ssd_act(*act, row0=c * SSD_CHUNK)
        y, h_new = _ssd_chunk(_groups(xs, 256), _groups(bm, 128), _groups(cm, 128), _groups(z, 256),
                              _groups(nw, 256), h, *shared, row0=c * SSD_CHUNK)
        y_ref[...] = _wide(y).astype(bf16)
        h_scr[...] = h_new

    return pl.pallas_call(
        body, name="ssd_fwd", grid=(nc,), in_specs=in_specs, out_specs=(y_spec, st_spec),
        out_shape=(jax.ShapeDtypeStruct((t_rows, D_MODEL), bf16),
                   jax.ShapeDtypeStruct((nc, SSD_GROUPS, SSD_N, 256), f32)),
        scratch_shapes=[pltpu.VMEM((SSD_GROUPS, SSD_N, 256), f32)],
        compiler_params=_params(("arbitrary",)),
    )(u, u, u, u, u, u, u, u, conv_w8, conv_w8, conv_w8, dtb8, alog8, d8, nw8)


def ssd_bwd(u, conv_w8, dtb8, alog8, d8, nw8, states, dy, du):
    t_rows = u.shape[0]
    nc = t_rows // SSD_CHUNK
    n = SSD_CHUNK
    in_specs, y_spec, st_spec, col, taps, row = _ssd_specs(nc, True)

    def body(*refs):
        ins, st_ref, dy_ref = refs[:15], refs[15], refs[16]
        du_ref, ddt_ref, dtx_ref, dtb_ref, dtc_ref, ddtb_ref, dalog_ref, ddsk_ref, dnw_ref = refs[18:27]
        dh_scr, hx_scr, hb_scr, hc_scr = refs[27:]
        cc = pl.program_id(0)
        c = nc - 1 - cc

        @pl.when(cc == 0)
        def _():
            for r in (dh_scr, hx_scr, hb_scr, hc_scr, dtx_ref, dtb_ref, dtc_ref, dnw_ref, ddtb_ref, dalog_ref, ddsk_ref):
                r[...] = jnp.zeros_like(r)

        act, (z, nw), shared = _ssd_load(ins, c == 0)
        (xs, bm, cm), vjp_act = jax.vjp(functools.partial(_ssd_act, row0=c * n), *act)
        _, vjp_core = jax.vjp(functools.partial(_ssd_chunk, row0=c * n), _groups(xs, 256), _groups(bm, 128),
                              _groups(cm, 128), _groups(z, 256), _groups(nw, 256), st_ref[0], *shared)
        dxa, dba, dca, dz, dnw, dh, ddt, ddtb, dalog, ddsk = vjp_core(
            (_groups(dy_ref[...].astype(f32), 256), dh_scr[...]))
        dh_scr[...] = dh
        dxs, dbm, dcm, dhx, dhb, dhc, dtx, dtb, dtc, dbx, dbb, dbc = vjp_act((_wide(dxa), _wide(dba), _wide(dca)))
        du_ref[:, 0:D_MODEL] = _wide(dz).astype(bf16)
        for dx, dhalo, scr, lo in ((dxs, dhx, hx_scr, C_SX), (dbm, dhb, hb_scr, C_SB), (dcm, dhc, hc_scr, C_SC)):
            zeros = jnp.zeros((n - 8, dx.shape[1]), f32)
            du_ref[:, lo - C_SZ:lo - C_SZ + dx.shape[1]] = (dx + jnp.concatenate([zeros, scr[...]], axis=0)).astype(bf16)
            scr[...] = dhalo
        ddt_ref[...] = ddt
        for ref, dtaps, dbias in ((dtx_ref, dtx, dbx), (dtb_ref, dtb, dbb), (dtc_ref, dtc, dbc)):
            for j in range(4):
                ref[j:j + 1, :] += dtaps[j]
            ref[4:5, :] += dbias
        ddtb_ref[0:1, :] += ddtb
        dalog_ref[0:1, :] += dalog
        ddsk_ref[0:1, :] += ddsk
        dnw_ref[0:1, :] += _wide(dnw)

    def out_col(w):
        return pl.BlockSpec((n, w), lambda c: (nc - 1 - c, 0))

    out_specs = (pl.BlockSpec((n, 3 * D_MODEL), lambda c: (nc - 1 - c, C_SZ // (3 * D_MODEL))), out_col(128),
                 taps(0, D_MODEL), taps(0, 512), taps(0, 512), row, row, row, taps(0, D_MODEL))
    out_shape = (jax.ShapeDtypeStruct(du.shape, du.dtype),
                 jax.ShapeDtypeStruct((t_rows, 128), f32),
                 jax.ShapeDtypeStruct((8, D_MODEL), f32), jax.ShapeDtypeStruct((8, 512), f32),
                 jax.ShapeDtypeStruct((8, 512), f32),
                 jax.ShapeDtypeStruct((8, 128), f32), jax.ShapeDtypeStruct((8, 128), f32),
                 jax.ShapeDtypeStruct((8, 128), f32), jax.ShapeDtypeStruct((8, D_MODEL), f32))
    return pl.pallas_call(
        body, name="ssd_bwd", grid=(nc,), in_specs=in_specs + [st_spec, y_spec, ANY],
        out_specs=out_specs, out_shape=out_shape, input_output_aliases={17: 0},
        scratch_shapes=[pltpu.VMEM((SSD_GROUPS, SSD_N, 256), f32), pltpu.VMEM((8, D_MODEL), f32),
                        pltpu.VMEM((8, 512), f32), pltpu.VMEM((8, 512), f32)],
        compiler_params=_params(("arbitrary",)),
    )(u, u, u, u, u, u, u, u, conv_w8, conv_w8, conv_w8, dtb8, alog8, d8, nw8, states, dy, du)


NEG = -1e30


def _swa_core(q, kc, kp, km, vc, vp, vm, sink, *, n):
    rows = SWA_REP * SWA_W
    ri, ci = _iota((rows, SWA_W), 0) & (SWA_W - 1), _iota((rows, SWA_W), 1)
    causal = ci <= ri
    m_band = (causal & ((n >= 1) | ((ci >= PAD) & (ri >= PAD)))) | ((ci > ri) & (n >= 2))
    m_meta = (n >= 1) & (ci >= PAD)
    q = q * (SWA_D ** -0.5)
    s = jnp.where(m_band, jnp.where(causal, _dot(q, kc, NT), _dot(q, kp, NT)), NEG)
    sm = jnp.where(m_meta, _dot(q, km, NT), NEG)
    mx = jnp.maximum(jnp.maximum(jnp.max(s, axis=1, keepdims=True), jnp.max(sm, axis=1, keepdims=True)), sink)
    mx = lax.stop_gradient(mx)
    e, em = jnp.exp(s - mx), jnp.exp(sm - mx)
    den = jnp.sum(e, axis=1, keepdims=True) + jnp.sum(em, axis=1, keepdims=True) + jnp.exp(sink - mx)
    return (_dot(jnp.where(causal, e, 0.0), vc) + _dot(jnp.where(causal, 0.0, e), vp) + _dot(em, vm)) / den


def _swa_block(q16, kc, kp, km, vc, vp, vm, sink16, *, n):
    rows = SWA_REP * SWA_W
    lane = _iota((1, 128), 1)
    cols = []
    for h in range(SWA_KV_HEADS):
        sinks = [jnp.sum(jnp.where(lane == h * SWA_REP + r, sink16, 0.0), axis=1, keepdims=True) for r in range(SWA_REP)]
        cols.append(jnp.concatenate([jnp.broadcast_to(s, (SWA_W, 1)) for s in sinks], axis=0))
    o = jax.vmap(functools.partial(_swa_core, n=n))(q16.reshape(SWA_KV_HEADS, rows, SWA_D), kc, kp, km, vc, vp, vm,
                                                    jnp.concatenate([col[None] for col in cols], axis=0))
    return o.reshape(q16.shape)


def _swa_specs(nb, rev):
    def bidx(n):
        return (nb - 1 - n) if rev else n

    kvw = SWA_KV_HEADS * SWA_D
    q = pl.BlockSpec((SWA_W, D_MODEL), lambda n: (bidx(n), C_WQ // D_MODEL))

    def kv(base, blk):
        return pl.BlockSpec((SWA_W, kvw), lambda n: (blk(bidx(n)), base // kvw))

    cur, prev, meta = (lambda n: n), (lambda n: jnp.maximum(n - 1, 0)), (lambda n: 0)
    row = pl.BlockSpec((8, 128), lambda n: (0, 0))
    in_specs = [q] + [kv(C_WK, b) for b in (cur, prev, meta)] + [kv(C_WV, b) for b in (cur, prev, meta)] + [row]
    return in_specs, pl.BlockSpec((SWA_W, D_MODEL), lambda n: (bidx(n), 0)), row


def _swa_heads(a):
    return jnp.stack([a[:, i * SWA_D:(i + 1) * SWA_D] for i in range(a.shape[1] // SWA_D)])


def swa_fwd(u, sink8):
    t_rows = u.shape[0]
    nb = t_rows // SWA_W
    in_specs, o_spec, _ = _swa_specs(nb, False)

    def body(q_ref, kc, kp, km, vc, vp, vm, sink_ref, o_ref):
        o = _swa_block(*[_swa_heads(r[...]) for r in (q_ref, kc, kp, km, vc, vp, vm)], sink_ref[0:1, :],
                       n=pl.program_id(0))
        o_ref[...] = _wide(o).astype(bf16)

    return pl.pallas_call(
        body, name="swa_fwd", grid=(nb,), in_specs=in_specs, out_specs=o_spec,
        out_shape=jax.ShapeDtypeStruct((t_rows, D_MODEL), bf16),
        compiler_params=_params(("arbitrary",)),
    )(u, u, u, u, u, u, u, sink8)


def swa_bwd(u, sink8, do, du):
    t_rows = u.shape[0]
    nb = t_rows // SWA_W
    in_specs, o_spec, row = _swa_specs(nb, True)
    width = C_BA - C_WQ

    def body(q_ref, kc, kp, km, vc, vp, vm, sink_ref, do_ref, _, du_ref, dsink_ref,
             dkp_scr, dvp_scr, dkm_scr, dvm_scr):
        nn = pl.program_id(0)
        n = nb - 1 - nn

        @pl.when(nn == 0)
        def _():
            for r in (dkp_scr, dvp_scr, dkm_scr, dvm_scr, dsink_ref):
                r[...] = jnp.zeros_like(r)

        fn = functools.partial(_swa_block, n=n)
        _, vjp = jax.vjp(fn, *[_swa_heads(r[...]) for r in (q_ref, kc, kp, km, vc, vp, vm)], sink_ref[0:1, :])
        dq, dkc, dkp, dkm, dvc, dvp, dvm, dsink = vjp(_swa_heads(do_ref[...]))
        dkm_scr[...] += dkm
        dvm_scr[...] += dvm
        first = n == 0
        dk = dkc + dkp_scr[...] + jnp.where(first, dkm_scr[...], 0.0)
        dv = dvc + dvp_scr[...] + jnp.where(first, dvm_scr[...], 0.0)
        du_ref[:, 0:D_MODEL] = _wide(dq).astype(bf16)
        du_ref[:, C_WK - C_WQ:C_WV - C_WQ] = _wide(dk).astype(bf16)
        du_ref[:, C_WV - C_WQ:width] = _wide(dv).astype(bf16)
        dkp_scr[...] = dkp
        dvp_scr[...] = dvp
        dsink_ref[0:1, :] += dsink

    return pl.pallas_call(
        body, name="swa_bwd", grid=(nb,), in_specs=in_specs + [o_spec, ANY],
        out_specs=(pl.BlockSpec((SWA_W, width), lambda n: (nb - 1 - n, C_WQ // width)), row),
        out_shape=(jax.ShapeDtypeStruct(du.shape, du.dtype), jax.ShapeDtypeStruct((8, 128), f32)),
        input_output_aliases={9: 0},
        scratch_shapes=[pltpu.VMEM((SWA_KV_HEADS, SWA_W, SWA_D), f32)] * 4,
        compiler_params=_params(("arbitrary",)),
    )(u, u, u, u, u, u, u, sink8, do, du)


def _tile(dim, prefs):
    for p in prefs:
        if dim % p == 0:
            return p
    return dim


def _row_tile(rows, d):
    for p in range(min(rows, BLOCK_BYTES // (4 * d)) // 8 * 8, 0, -8):
        if rows % p == 0:
            return p
    return rows


def mm(a, b, *, out_dtype, name, resid=None, relu_grad_of=None, relu2_out=False, ta=False, tb=False, norm_w8=None):
    assert resid is None or relu_grad_of is None
    k, m = (a.shape if ta else a.shape[::-1])
    n = b.shape[0] if tb else b.shape[1]
    rhs_stays = k * 2 * 1024 > MM_OPERAND_BYTES
    if rhs_stays:
        tn = _tile(n, tuple(p for p in (512, 256, 128) if p * k * 2 <= MM_RESIDENT_BYTES))
        tm = _tile(m, tuple(p for p in (512, 384, 256, 128) if p * k * 2 <= MM_OPERAND_BYTES))
        grid = (n // tn, m // tm)
        ij = lambda o, i: (i, o)
    elif k * n * b.dtype.itemsize <= MM_OPERAND_BYTES:
        tn = n
        tm = _tile(m, tuple(p for p in (1408, 1024, 512, 384, 256, 128)
                            if p * k * 2 <= MM_OPERAND_BYTES and p * n * 4 <= MM_OPERAND_BYTES * 2 // 3))
        grid = (m // tm, 1)
        ij = lambda o, i: (o, i)
    else:
        tm = _tile(m, tuple(p for p in (1408, 1024, 512, 384, 256, 128) if p * k * 2 <= MM_OPERAND_BYTES))
        tn = _tile(n, tuple(p for p in (1024, 512, 256, 128) if p * k * 2 <= MM_OPERAND_BYTES // 2))
        grid = (m // tm, n // tn)
        ij = lambda o, i: (o, i)

    extra = resid if resid is not None else relu_grad_of
    staged = ta or norm_w8 is not None
    assert not (ta and norm_w8 is not None) and not (staged and rhs_stays)
    n_in = 2 + (extra is not None) + (norm_w8 is not None)
    n_out = 1 + relu2_out + (norm_w8 is not None)

    def body(*refs):
        ins, outs, scr = refs[:n_in], refs[n_in:n_in + n_out], refs[n_in + n_out:]
        a_ref, b_ref = ins[:2]
        if staged:
            @pl.when(pl.program_id(1) == 0)
            def _():
                if ta:
                    scr[0][...] = a_ref[...].T
                else:
                    hn = _rmsnorm(a_ref[...], ins[-1][0:1, :]).astype(bf16)
                    scr[0][...] = hn
                    outs[-1][...] = hn

            lhs = scr[0][...]
        else:
            lhs = a_ref[...]
        o = _dot(lhs, b_ref[...], NT if tb else NN)
        if resid is not None:
            o = o + ins[2][...]
        if relu_grad_of is not None:
            o = o * (2.0 * jnp.maximum(ins[2][...], 0.0))
        outs[0][...] = o.astype(out_dtype)
        if relu2_out:
            r = jnp.maximum(o, 0.0)
            outs[1][...] = (r * r).astype(bf16)

    in_specs = [pl.BlockSpec((k, tm), lambda o, i: (0, ij(o, i)[0])) if ta
                else pl.BlockSpec((tm, k), lambda o, i: (ij(o, i)[0], 0)),
                pl.BlockSpec((tn, k), lambda o, i: (ij(o, i)[1], 0)) if tb
                else pl.BlockSpec((k, tn), lambda o, i: (0, ij(o, i)[1]))]
    args = [a, b]
    if extra is not None:
        in_specs.append(pl.BlockSpec((tm, tn), ij))
        args.append(extra)
    out_blk = pl.BlockSpec((tm, tn), ij)
    out_specs = [out_blk] * (1 + relu2_out)
    out_shape = [jax.ShapeDtypeStruct((m, n), out_dtype)] + [jax.ShapeDtypeStruct((m, n), bf16)] * relu2_out
    if norm_w8 is not None:
        in_specs.append(pl.BlockSpec((8, k), lambda o, i: (0, 0)))
        args.append(norm_w8)
        out_specs.append(pl.BlockSpec((tm, k), lambda o, i: (ij(o, i)[0], 0)))
        out_shape.append(jax.ShapeDtypeStruct((m, k), bf16))
    res = pl.pallas_call(
        body, name=name, grid=grid, in_specs=in_specs, out_specs=tuple(out_specs), out_shape=tuple(out_shape),
        scratch_shapes=[pltpu.VMEM((tm, k), bf16)] if staged else [],
        compiler_params=_params(("parallel", "arbitrary" if staged else "parallel")),
    )(*args)
    return res[0] if len(res) == 1 else res


def _rows(t_rows):
    return _tile(t_rows, (384, 256, 128))


def _rmsnorm(h, w):
    return h * lax.rsqrt(jnp.mean(h * h, axis=1, keepdims=True) + RMS_EPS) * w


def rmsnorm_bwd(h, w8, dhn, dres, *, name):
    t_rows, d = h.shape
    tr = _rows(t_rows)

    def body(h_ref, w_ref, dhn_ref, dres_ref, dh_ref, dw_ref):
        @pl.when(pl.program_id(0) == 0)
        def _():
            dw_ref[...] = jnp.zeros_like(dw_ref)

        _, vjp = jax.vjp(_rmsnorm, h_ref[...], w_ref[0:1, :])
        dh, dw = vjp(dhn_ref[...])
        dh_ref[...] = dh + dres_ref[...]
        dw_ref[0:1, :] += dw

    blk = pl.BlockSpec((tr, d), lambda i: (i, 0))
    wblk = pl.BlockSpec((8, d), lambda i: (0, 0))
    return pl.pallas_call(
        body, name=name, grid=(t_rows // tr,), in_specs=[blk, wblk, blk, blk], out_specs=(blk, wblk),
        out_shape=(jax.ShapeDtypeStruct((t_rows, d), f32), jax.ShapeDtypeStruct((8, d), f32)),
        compiler_params=_params(("arbitrary",)),
    )(h, w8, dhn, dres)


def _merge(pg, ps, pw, la, lb, lc):
    return _sigmoid(la) * pg + _sigmoid(lb) * ps + _sigmoid(lc) * pw


def _merge_specs(t_rows):
    tr = _rows(t_rows)
    blk = pl.BlockSpec((tr, D_MODEL), lambda i: (i, 0))
    gate = [pl.BlockSpec((tr, D_MODEL), functools.partial(lambda i, j: (i, j), j=C_GATE // D_MODEL + j)) for j in range(3)]
    return tr, blk, gate


def merge_fwd(pg, ps, pw, u):
    t_rows = pg.shape[0]
    tr, blk, gate = _merge_specs(t_rows)

    def body(pg_ref, ps_ref, pw_ref, la, lb, lc, o_ref):
        o_ref[...] = _merge(pg_ref[...], ps_ref[...], pw_ref[...], la[...], lb[...], lc[...]).astype(bf16)

    return pl.pallas_call(
        body, name="merge_fwd", grid=(t_rows // tr,), in_specs=[blk, blk, blk] + gate, out_specs=blk,
        out_shape=jax.ShapeDtypeStruct((t_rows, D_MODEL), bf16), compiler_params=_params(("arbitrary",)),
    )(pg, ps, pw, u, u, u)


def merge_bwd(pg, ps, pw, u, dmerged, du):
    t_rows = pg.shape[0]
    tr, blk, gate = _merge_specs(t_rows)

    def body(pg_ref, ps_ref, pw_ref, la, lb, lc, dm_ref, _, dpg_ref, dps_ref, dpw_ref, dl_ref):
        _, vjp = jax.vjp(_merge, pg_ref[...], ps_ref[...], pw_ref[...], la[...], lb[...], lc[...])
        dpg, dps, dpw, dla, dlb, dlc = vjp(dm_ref[...])
        dpg_ref[...] = dpg.astype(bf16)
        dps_ref[...] = dps.astype(bf16)
        dpw_ref[...] = dpw.astype(bf16)
        for j, dl in enumerate((dla, dlb, dlc)):
            dl_ref[:, j * D_MODEL:(j + 1) * D_MODEL] = dl.astype(bf16)

    act = jax.ShapeDtypeStruct((t_rows, D_MODEL), bf16)
    return pl.pallas_call(
        body, name="merge_bwd", grid=(t_rows // tr,), in_specs=[blk, blk, blk] + gate + [blk, ANY],
        out_specs=(blk, blk, blk, pl.BlockSpec((tr, 3 * D_MODEL), lambda i: (i, C_GATE // (3 * D_MODEL)))),
        out_shape=(act, act, act, jax.ShapeDtypeStruct(du.shape, du.dtype)),
        input_output_aliases={7: 3},
        compiler_params=_params(("arbitrary",)),
    )(pg, ps, pw, u, u, u, dmerged, du)


def loss_head(h, w8, target):
    t_rows, d = h.shape
    tr = HEAD_ROWS

    def loss_fn(hb, w, tgt):
        err = _rmsnorm(hb, w) - tgt
        return 0.5 * jnp.sum(err * err) / d

    def body(h_ref, w_ref, t_ref, loss_ref, dh_ref, dw_ref):
        i = pl.program_id(0)

        @pl.when(i == 0)
        def _():
            loss_ref[...] = jnp.zeros_like(loss_ref)
            dw_ref[...] = jnp.zeros_like(dw_ref)
            dh_ref[...] = jnp.zeros_like(dh_ref)

        @pl.when(i > 0)
        def _():
            val, (dh, dw) = jax.value_and_grad(loss_fn, argnums=(0, 1))(h_ref[...], w_ref[0:1, :], t_ref[...])
            loss_ref[...] += val
            dh_ref[...] = dh
            dw_ref[0:1, :] += dw

    blk = pl.BlockSpec((tr, d), lambda i: (i, 0))
    wblk = pl.BlockSpec((8, d), lambda i: (0, 0))
    return pl.pallas_call(
        body, name="loss_head", grid=(t_rows // tr,),
        in_specs=[blk, wblk, pl.BlockSpec((tr, d), lambda i: (jnp.maximum(i - 1, 0), 0))],
        out_specs=(pl.BlockSpec((8, 128), lambda i: (0, 0)), blk, wblk),
        out_shape=(jax.ShapeDtypeStruct((8, 128), f32), jax.ShapeDtypeStruct((t_rows, d), f32),
                   jax.ShapeDtypeStruct((8, d), f32)),
        compiler_params=_params(("arbitrary",)),
    )(h, w8, target)


def adamw(w, m, v, partials, row_off, *, name):
    rows, d = w.shape
    layers = len(partials)
    per = rows // layers
    tr = _row_tile(per, d)
    assert row_off % tr == 0
    off, nblk = row_off // tr, per // tr
    c1 = 1.0 - ADAM_B1 ** ADAM_STEP
    c2 = 1.0 - ADAM_B2 ** ADAM_STEP

    def body(w_ref, m_ref, v_ref, *refs):
        p_refs, (g_ref, d_ref, mo_ref, vo_ref) = refs[:2 * layers], refs[2 * layers:]
        g = p_refs[0][...] + p_refs[1][...]
        for l in range(1, layers):
            g = jnp.where(pl.program_id(0) >= l * nblk, p_refs[2 * l][...] + p_refs[2 * l + 1][...], g)
        m_new = ADAM_B1 * m_ref[...] + (1.0 - ADAM_B1) * g
        v_new = ADAM_B2 * v_ref[...] + (1.0 - ADAM_B2) * (g * g)
        g_ref[...] = g
        d_ref[...] = -ADAM_LR * ((m_new / c1) / (jnp.sqrt(v_new / c2) + ADAM_EPS) + ADAM_WD * w_ref[...])
        mo_ref[...] = m_new
        vo_ref[...] = v_new

    blk = pl.BlockSpec((tr, d), lambda i: (i, 0))
    pblks = [pl.BlockSpec((tr, d), functools.partial(lambda i, l: (off + jnp.clip(i - l * nblk, 0, nblk - 1), 0), l=l))
             for l in range(layers) for _ in range(2)]
    out = jax.ShapeDtypeStruct((rows, d), f32)
    return pl.pallas_call(
        body, name=name, grid=(rows // tr,), in_specs=[blk, blk, blk] + pblks, out_specs=(blk,) * 4,
        out_shape=(out,) * 4, compiler_params=_params(("arbitrary",)),
    )(w, m, v, *[p for pair in partials for p in pair])


def reduce4(parts, *, name, own=None, me=None):
    _, rows, d = parts.shape
    tr = _row_tile(rows, d)

    def body(*refs):
        p_ref, o_ref = refs[0], refs[-1]
        acc = None
        for s in range(4):
            term = p_ref[s].astype(f32)
            if own is not None:
                term = jnp.where(refs[2][0] == s, refs[1][...].astype(f32), term)
            acc = term if acc is None else acc + term
        o_ref[...] = acc

    in_specs = [pl.BlockSpec((4, tr, d), lambda i: (0, i, 0))]
    args = [parts]
    if own is not None:
        in_specs += [pl.BlockSpec((tr, d), lambda i: (i, 0)), pl.BlockSpec(memory_space=pltpu.SMEM)]
        args += [own, me]
    return pl.pallas_call(
        body, name=name, grid=(rows // tr,), in_specs=in_specs,
        out_specs=pl.BlockSpec((tr, d), lambda i: (i, 0)), out_shape=jax.ShapeDtypeStruct((rows, d), f32),
        compiler_params=_params(("arbitrary",)),
    )(*args)


ANY = pl.BlockSpec(memory_space=pl.ANY)
MESH = pl.DeviceIdType.MESH
CHIP_FLIPS = ((0, 1), (1, 0), (1, 1))


def chip_exchange(bufs, scatter, *, name, after=None):
    nb = len(bufs)
    extra = [] if after is None else [after]

    def body(*refs):
        ins, outs = refs[:nb], refs[nb + len(extra):2 * nb + len(extra)]
        send_sems, recv_sems, local_sems = refs[2 * nb + len(extra):]
        x, y, c = lax.axis_index("x"), lax.axis_index("y"), lax.axis_index("c")
        me = 2 * x + y
        local = [pltpu.make_async_copy(ins[j].at[me] if scatter[j] else ins[j], outs[j].at[me], local_sems.at[j])
                 for j in range(nb)]
        for cp in local:
            cp.start()
        sends, recvs = [], []
        for k, (fx, fy) in enumerate(CHIP_FLIPS):
            px = 1 - x if fx else x
            py = 1 - y if fy else y
            chip = 2 * px + py
            for j in range(nb):
                src = ins[j].at[chip] if scatter[j] else ins[j]
                sems = dict(send_sem=send_sems.at[nb * k + j], recv_sem=recv_sems.at[nb * k + j],
                            device_id=(px, py, c), device_id_type=MESH)
                sends.append(pltpu.make_async_remote_copy(src_ref=src, dst_ref=outs[j].at[me], **sems))
                recvs.append(pltpu.make_async_remote_copy(src_ref=src, dst_ref=outs[j].at[chip], **sems))
        for cp in sends:
            cp.start()
        for cp in recvs:
            cp.wait_recv()
        for cp in sends:
            cp.wait_send()
        for cp in local:
            cp.wait()

    out_shape = tuple(jax.ShapeDtypeStruct(b.shape if s else (4,) + b.shape, b.dtype) for b, s in zip(bufs, scatter))
    return pl.pallas_call(
        body, name=name, in_specs=[ANY] * (nb + len(extra)), out_specs=(ANY,) * nb, out_shape=out_shape,
        scratch_shapes=[pltpu.SemaphoreType.DMA((3 * nb,)), pltpu.SemaphoreType.DMA((3 * nb,)),
                        pltpu.SemaphoreType.DMA((nb,))],
        compiler_params=pltpu.CompilerParams(has_side_effects=True),
    )(*bufs, *extra)


def gather_two_level(big, small, *, name):
    half = big.shape[1] // 2

    def body(big_ref, small_ref, obig_ref, osmall_ref, send_sems, recv_sems, local_sems):
        x, y, c = lax.axis_index("x"), lax.axis_index("y"), lax.axis_index("c")
        me = 2 * x + y
        mine = (slice(None), pl.ds(pl.multiple_of(c * half, half), half))
        theirs = (slice(None), pl.ds(pl.multiple_of((1 - c) * half, half), half))
        local = [pltpu.make_async_copy(big_ref, obig_ref.at[me], local_sems.at[0]),
                 pltpu.make_async_copy(small_ref, osmall_ref.at[me], local_sems.at[1])]
        for cp in local:
            cp.start()

        def copy(k, src, dst, to):
            return pltpu.make_async_remote_copy(src_ref=src, dst_ref=dst, send_sem=send_sems.at[k],
                                                recv_sem=recv_sems.at[k], device_id=to, device_id_type=MESH)

        sends, landed, passed, small_in = [], [], [], []
        for k, (fx, fy) in enumerate(CHIP_FLIPS):
            px = 1 - x if fx else x
            py = 1 - y if fy else y
            chip = 2 * px + py
            sends.append(copy(k, big_ref.at[mine], obig_ref.at[(me,) + mine], (px, py, c)))
            landed.append(copy(k, big_ref.at[mine], obig_ref.at[(chip,) + mine], (px, py, c)))
            sends.append(copy(3 + k, small_ref, osmall_ref.at[me], (px, py, c)))
            small_in.append(copy(3 + k, small_ref, osmall_ref.at[chip], (px, py, c)))
            passed.append((copy(6 + k, obig_ref.at[(chip,) + mine], obig_ref.at[(chip,) + mine], (x, y, 1 - c)),
                           copy(6 + k, obig_ref.at[(chip,) + theirs], obig_ref.at[(chip,) + theirs], (x, y, 1 - c))))
        for cp in sends:
            cp.start()
        for k in range(3):
            landed[k].wait_recv()
            passed[k][0].start()
        for k in range(3):
            passed[k][1].wait_recv()
            small_in[k].wait_recv()
        for cp in sends + [p[0] for p in passed]:
            cp.wait_send()
        for cp in local:
            cp.wait()

    return pl.pallas_call(
        body, name=name, in_specs=[ANY, ANY], out_specs=(ANY, ANY),
        out_shape=(jax.ShapeDtypeStruct((4,) + big.shape, big.dtype),
                   jax.ShapeDtypeStruct((4,) + small.shape, small.dtype)),
        scratch_shapes=[pltpu.SemaphoreType.DMA((9,)), pltpu.SemaphoreType.DMA((9,)), pltpu.SemaphoreType.DMA((2,))],
        compiler_params=pltpu.CompilerParams(has_side_effects=True),
    )(big, small)


def sibling_swap(bufs, *, name):
    nb = len(bufs)

    def body(*refs):
        ins, outs, (send_sems, recv_sems) = refs[:nb], refs[nb:2 * nb], refs[2 * nb:]
        peer = (lax.axis_index("x"), lax.axis_index("y"), 1 - lax.axis_index("c"))
        copies = [pltpu.make_async_remote_copy(src_ref=ins[j], dst_ref=outs[j], send_sem=send_sems.at[j],
                                               recv_sem=recv_sems.at[j], device_id=peer, device_id_type=MESH)
                  for j in range(nb)]
        for cp in copies:
            cp.start()
        for cp in copies:
            cp.wait_recv()
        for cp in copies:
            cp.wait_send()

    return pl.pallas_call(
        body, name=name, in_specs=[ANY] * nb, out_specs=(ANY,) * nb,
        out_shape=tuple(jax.ShapeDtypeStruct(b.shape, b.dtype) for b in bufs),
        scratch_shapes=[pltpu.SemaphoreType.DMA((nb,)), pltpu.SemaphoreType.DMA((nb,))],
        compiler_params=pltpu.CompilerParams(has_side_effects=True),
    )(*bufs)


HBM = pl.BlockSpec(memory_space=pltpu.HBM)
SEM = pl.BlockSpec(memory_space=pltpu.SEMAPHORE)
DATAFLOW = pltpu.SideEffectType.DATAFLOW_SIDE_EFFECTING


def _exchange_copies(srcs, lands, send_sems, recv_sems, scatter):
    x, y, c = lax.axis_index("x"), lax.axis_index("y"), lax.axis_index("c")
    me = 2 * x + y
    nb = len(srcs)
    pairs = []
    for k, (fx, fy) in enumerate(CHIP_FLIPS):
        px = 1 - x if fx else x
        py = 1 - y if fy else y
        chip = 2 * px + py
        for j in range(nb):
            src = srcs[j].at[chip] if scatter[j] else srcs[j]
            sems = dict(send_sem=send_sems.at[nb * k + j], recv_sem=recv_sems.at[nb * k + j],
                        device_id=(px, py, c), device_id_type=MESH)
            pairs.append((pltpu.make_async_remote_copy(src_ref=src, dst_ref=lands[j].at[me], **sems),
                          pltpu.make_async_remote_copy(src_ref=src, dst_ref=lands[j].at[chip], **sems)))
    return pairs


def exchange_start(bufs, scatter, after, *, name):
    nb = len(bufs)
    slabs = [b.shape[1:] if s else b.shape for b, s in zip(bufs, scatter)]
    lands = [lax.empty((4,) + shp, b.dtype) for b, shp in zip(bufs, slabs)]

    def body(*refs):
        srcs, zones = refs[:nb], refs[nb:2 * nb]
        send_sems, recv_sems = refs[2 * nb + 1:2 * nb + 3]
        token = refs[-1]
        for send, _ in _exchange_copies(srcs, zones, send_sems, recv_sems, scatter):
            send.start()
        token[...] = jnp.zeros_like(token)

    hbm = lambda a: pltpu.with_memory_space_constraint(a, pltpu.HBM)
    out = pl.pallas_call(
        body, name=name, in_specs=[HBM] * (2 * nb) + [ANY],
        out_specs=(SEM, SEM) + (HBM,) * (2 * nb) + (pl.BlockSpec(memory_space=pltpu.VMEM),),
        out_shape=(pltpu.SemaphoreType.DMA((3 * nb,)), pltpu.SemaphoreType.DMA((3 * nb,)))
        + tuple(pltpu.HBM(a.shape, a.dtype) for a in list(bufs) + lands) + (jax.ShapeDtypeStruct((8, 128), f32),),
        input_output_aliases={i: 2 + i for i in range(2 * nb)},
        compiler_params=pltpu.CompilerParams(has_side_effects=DATAFLOW),
    )(*[hbm(a) for a in list(bufs) + lands], after)
    return (out[:2], out[2:2 + nb], out[2 + nb:2 + 2 * nb], scatter), out[-1]


def exchange_wait(state, after, *, name):
    (send_sems, recv_sems), srcs, lands, scatter = state
    nb = len(srcs)

    def body(*refs):
        src_refs, zones = refs[:nb], refs[nb:2 * nb]
        s_sems, r_sems = refs[2 * nb:2 * nb + 2]
        for send, recv in _exchange_copies(src_refs, zones, s_sems, r_sems, scatter):
            send.wait_send()
            recv.wait_recv()

    out = pl.pallas_call(
        body, name=name, in_specs=[HBM] * (2 * nb) + [SEM, SEM, ANY], out_specs=(HBM,) * (2 * nb),
        out_shape=tuple(pltpu.HBM(a.shape, a.dtype) for a in list(srcs) + list(lands)),
        input_output_aliases={i: i for i in range(2 * nb)},
        compiler_params=pltpu.CompilerParams(has_side_effects=DATAFLOW),
    )(*srcs, *lands, send_sems, recv_sems, after)
    return out[nb:]


BIG = (
    ("w_proj_gdn", 256), ("w_proj_ssd", 256), ("w_proj_swa", 256), ("w_out", 256), ("w_up", 1024), ("w_down", 1024))
BIG_OFF = {}
_o = 0
for _n, _r in BIG:
    BIG_OFF[_n] = _o
    _o += _r
BIG_ROWS = _o
W_IN_SHARD = IN_W // 4

W_NAMES = ('meta_tokens', 'norm1_w', 'w_in', 'gdn_conv_w', 'gdn_a_log', 'gdn_dt_bias', 'gdn_norm_w', 'ssd_conv_w',
           'ssd_conv_b', 'ssd_dt_bias', 'ssd_a_log', 'ssd_d', 'ssd_norm_w', 'swa_sinks', 'w_proj_gdn', 'w_proj_ssd',
           'w_proj_swa', 'w_out', 'norm2_w', 'w_up', 'w_down', 'final_norm_w')
SMALL_NAMES = tuple(n for n in W_NAMES if n not in BIG_OFF and n != "w_in")
SMALL_SHARDED = ("meta_tokens", "gdn_conv_w", "ssd_conv_w")


def _pack_rows(parts, dtype):
    flat = jnp.concatenate([p.reshape(-1).astype(dtype) for p in parts])
    n = -(-flat.shape[0] // 8192) * 8192
    return jnp.pad(flat, (0, n - flat.shape[0])).reshape(-1, D_MODEL)


def _unpack_rows(packed, shapes):
    flat, out, o = packed.reshape(-1), [], 0
    for s in shapes:
        n = 1
        for d in s:
            n *= d
        out.append(flat[o:o + n].reshape(s))
        o += n
    return out


def _split_chips(full, axis):
    s = full.shape
    a = full.reshape(s[:axis] + (4, s[axis] // 4) + s[axis + 1:])
    return jnp.moveaxis(a, axis, 0)


def _join_chips(parts, axis):
    a = jnp.moveaxis(parts, 0, axis)
    s = a.shape
    return a.reshape(s[:axis] + (s[axis] * s[axis + 1],) + s[axis + 2:])


BIG_AXIS = {"w_proj_gdn": 1, "w_proj_ssd": 1, "w_proj_swa": 1, "w_out": 1, "w_up": 2, "w_down": 1}


def _w_in_to_padded(w):
    z = lambda n: jnp.zeros((n,) + w.shape[1:], w.dtype)
    return jnp.concatenate([w[8736:11808], w[4112:7184], w[7200:8736], w[4096:4112], z(112),
                            w[7184:7200], z(112 + C_MID_END - C_SDT - 128), w[0:4096]], axis=0)


def _w_in_from_padded(p):
    return jnp.concatenate([p[C_GQ:IN_WP], p[C_BA:C_BA + 16], p[C_SZ:C_WQ], p[C_SDT:C_SDT + 16], p[C_WQ:C_BA],
                            p[0:C_SZ]], axis=0)


def _row8(v, lane0=0, width=128):
    return jnp.pad(v[None, :], ((0, 7), (lane0, width - lane0 - v.shape[0])))


def _layer_fwd(h, p, l, late=None):
    tag = f"l{l}"
    u, hn = mm(h, p["w_in_t"], tb=True, out_dtype=f32, norm_w8=p["n1"], name=f"mm_in_{tag}")
    yg, stg, tg = gdn_fwd(u, p["gcw"], p["galog"], p["gdtb"], p["gnw"])
    ys, sts = ssd_fwd(u, p["scw"], p["sdtb"], p["salog"], p["sd"], p["snw"])
    yw = swa_fwd(u, p["sink"])
    if late is not None:
        p.update(late(yw))
    pg = mm(yg, p["wpg"], out_dtype=f32, name=f"mm_pg_{tag}")
    ps = mm(ys, p["wps"], out_dtype=f32, name=f"mm_ps_{tag}")
    pw = mm(yw, p["wpw"], out_dtype=f32, name=f"mm_pw_{tag}")
    merged = merge_fwd(pg, ps, pw, u)
    h2 = mm(merged, p["wout"], out_dtype=f32, resid=h, name=f"mm_out_{tag}")
    a, r, hn2 = mm(h2, p["wup"], out_dtype=f32, relu2_out=True, norm_w8=p["n2"], name=f"mm_up_{tag}")
    h3 = mm(r, p["wdown"], out_dtype=f32, resid=h2, name=f"mm_down_{tag}")
    saved = dict(h=h, hn=hn, u=u, yg=yg, stg=stg, tg=tg, ys=ys, sts=sts, yw=yw, pg=pg, ps=ps, pw=pw,
                 merged=merged, h2=h2, hn2=hn2, a=a, r=r)
    return h3, saved


def _layer_bwd(dh3, p, s, l, send_big, send_w_in):
    tag = f"l{l}"
    g = {}

    def wgrad(act, d, name):
        return mm(act, d, ta=True, out_dtype=bf16, name=f"wg_{name}_{tag}")

    da = mm(dh3, p["wdown"], tb=True, out_dtype=bf16, relu_grad_of=s["a"], name=f"dg_down_{tag}")
    g["w_down"] = wgrad(s["r"], dh3, "down")
    dhn2 = mm(da, p["wup"], tb=True, out_dtype=f32, name=f"dg_up_{tag}")
    g["w_up"] = wgrad(s["hn2"], da, "up")
    dh2, g["norm2_w"] = rmsnorm_bwd(s["h2"], p["n2"], dhn2, dh3, name=f"norm2_bwd_{tag}")
    dmerged = mm(dh2, p["wout"], tb=True, out_dtype=f32, name=f"dg_out_{tag}")
    g["w_out"] = wgrad(s["merged"], dh2, "out")
    du = lax.empty((dh3.shape[0], IN_WP), bf16)
    dpg, dps, dpw, du = merge_bwd(s["pg"], s["ps"], s["pw"], s["u"], dmerged, du)
    dyg = mm(dpg, p["wpg"], tb=True, out_dtype=f32, name=f"dg_pg_{tag}")
    dys = mm(dps, p["wps"], tb=True, out_dtype=f32, name=f"dg_ps_{tag}")
    dyw = mm(dpw, p["wpw"], tb=True, out_dtype=f32, name=f"dg_pw_{tag}")
    g["w_proj_gdn"] = wgrad(s["yg"], dpg, "pg")
    g["w_proj_ssd"] = wgrad(s["ys"], dps, "ps")
    g["w_proj_swa"] = wgrad(s["yw"], dpw, "pw")
    sent = send_big(jnp.concatenate([_split_chips(g.pop(n), BIG_AXIS[n] - 1).reshape(4, r, D_MODEL)
                                     for n, r in BIG], axis=1))

    (du, dba, dtq, dtk, dtv, g["gdn_a_log"], g["gdn_dt_bias"], g["gdn_norm_w"]) = gdn_bwd(
        s["u"], p["gcw"] + sent, p["galog"], p["gdtb"], p["gnw"], s["stg"], s["tg"], dyg, du)
    g["gdn_conv_w"] = jnp.concatenate([dtq, dtk, dtv], axis=1)[:4]
    (du, ddt, dtx, dtb, dtc, g["ssd_dt_bias"], g["ssd_a_log"], g["ssd_d"], g["ssd_norm_w"]) = ssd_bwd(
        s["u"], p["scw"], p["sdtb"], p["salog"], p["sd"], p["snw"], s["sts"], dys, du)
    dconv = jnp.concatenate([dtx, dtb, dtc], axis=1)
    g["ssd_conv_w"], g["ssd_conv_b"] = dconv[:4], dconv[4]
    du, g["swa_sinks"] = swa_bwd(s["u"], p["sink"], dyw, du)
    mid = jnp.concatenate([dba[0].astype(bf16), ddt.astype(bf16),
                           jnp.zeros((du.shape[0], C_MID_END - C_SDT - 128), bf16)], axis=1)
    du = lax.dynamic_update_slice(du, mid, (0, C_BA))
    sent = send_w_in(_w_in_from_padded(wgrad(du, s["hn"], "in")).reshape(4, W_IN_SHARD, D_MODEL))
    dhn = mm(du, p["w_in_t"], out_dtype=f32, name=f"dg_in_{tag}")
    dh, g["norm1_w"] = rmsnorm_bwd(s["h"], p["n1"] + sent, dhn, dh2, name=f"norm1_bwd_{tag}")
    return dh, g


def kernel(x, meta_tokens, norm1_w, w_in, gdn_conv_w, gdn_a_log, gdn_dt_bias, gdn_norm_w, ssd_conv_w, ssd_conv_b, ssd_dt_bias, ssd_a_log, ssd_d, ssd_norm_w, swa_sinks, w_proj_gdn, w_proj_ssd, w_proj_swa, w_out, norm2_w, w_up, w_down, final_norm_w, loss_target, m_meta_tokens, m_norm1_w, m_w_in, m_gdn_conv_w, m_gdn_a_log, m_gdn_dt_bias, m_gdn_norm_w, m_ssd_conv_w, m_ssd_conv_b, m_ssd_dt_bias, m_ssd_a_log, m_ssd_d, m_ssd_norm_w, m_swa_sinks, m_w_proj_gdn, m_w_proj_ssd, m_w_proj_swa, m_w_out, m_norm2_w, m_w_up, m_w_down, m_final_norm_w, v_meta_tokens, v_norm1_w, v_w_in, v_gdn_conv_w, v_gdn_a_log, v_gdn_dt_bias, v_gdn_norm_w, v_ssd_conv_w, v_ssd_conv_b, v_ssd_dt_bias, v_ssd_a_log, v_ssd_d, v_ssd_norm_w, v_swa_sinks, v_w_proj_gdn, v_w_proj_ssd, v_w_proj_swa, v_w_out, v_norm2_w, v_w_up, v_w_down, v_final_norm_w):
    given = dict(locals())
    depth = norm1_w.shape[0]
    me = 2 * lax.axis_index("x") + lax.axis_index("y")

    me1 = jnp.reshape(me, (1,)).astype(jnp.int32)
    w_in_t = jnp.swapaxes(w_in, 1, 2)

    def weight_slabs(l):
        return (w_in_t[l].astype(bf16),
                jnp.concatenate([given[n][l].reshape(-1, D_MODEL).astype(bf16) for n, _ in BIG]))

    slabs = [weight_slabs(l) for l in range(depth)]
    wsmall = _pack_rows([given[n] for n in SMALL_SHARDED], f32)
    ga0, gsmall = gather_two_level(slabs[0][0], wsmall, name="gather_first")
    gathers, started = {}, jnp.zeros((), f32)
    for l in range(depth):
        for j in range(2):
            if (l, j) != (0, 0):
                gathers[l, j], token = exchange_start([slabs[l][j]], (False,), gsmall, name=f"gather_start_l{l}_{j}")
                started = started + token[0, 0]
    shard_shapes = [given[n].shape for n in SMALL_SHARDED]
    per_chip = [_unpack_rows(gsmall[s], shard_shapes) for s in range(4)]
    full = {n: jnp.concatenate([per_chip[s][i] for s in range(4)], axis=-1) for i, n in enumerate(SMALL_SHARDED)}

    def landed(l, j, after):
        (zone,) = exchange_wait(gathers[l, j], after, name=f"gather_wait_l{l}_{j}")
        return lax.dynamic_update_slice(zone, slabs[l][j][None], (me, 0, 0))

    def first_operands(l, ga, order):
        return dict(
            n1=_row8(norm1_w[l], width=D_MODEL) + order, n2=_row8(norm2_w[l], width=D_MODEL),
            w_in_t=_w_in_to_padded(ga.reshape(IN_W, D_MODEL)),
            gcw=jnp.pad(full["gdn_conv_w"][l], ((0, 4), (0, 0))),
            galog=_row8(gdn_a_log[l], 8), gdtb=_row8(gdn_dt_bias[l], 8), gnw=_row8(gdn_norm_w[l]),
            scw=jnp.pad(jnp.concatenate([full["ssd_conv_w"][l], ssd_conv_b[l][None]], axis=0), ((0, 3), (0, 0))),
            sdtb=_row8(ssd_dt_bias[l]), salog=_row8(ssd_a_log[l]), sd=_row8(ssd_d[l]),
            snw=_row8(ssd_norm_w[l], width=D_MODEL), sink=_row8(swa_sinks[l]))

    def late_operands(l, after):
        gb = landed(l, 1, after)
        w = {}
        for n, r in BIG:
            parts = gb[:, BIG_OFF[n]:BIG_OFF[n] + r].reshape((4,) + given[n].shape[1:])
            w[n] = _join_chips(parts, BIG_AXIS[n] - 1)
        return dict(wpg=w["w_proj_gdn"], wps=w["w_proj_ssd"], wpw=w["w_proj_swa"], wout=w["w_out"],
                    wup=w["w_up"], wdown=w["w_down"])

    h = jnp.concatenate([jnp.zeros((PAD, D_MODEL), f32), full["meta_tokens"], x[0]], axis=0)
    layers, saved = [], []
    for l in range(depth):
        p = first_operands(0, ga0, started) if l == 0 else first_operands(l, landed(l, 0, h), 0.0)
        h, s = _layer_fwd(h, p, l, late=functools.partial(late_operands, l))
        layers.append(p)
        saved.append(s)
    loss8, dh, dfw8 = loss_head(h, _row8(final_norm_w, width=D_MODEL), loss_target[0])
    grads = {"final_norm_w": dfw8[0]}
    per_layer, grad_slabs, scatters = [None] * depth, {}, {}

    def send(l, j, slab):
        grad_slabs[l, j] = slab
        scatters[l, j], token = exchange_start([slab], (True,), loss8, name=f"scatter_start_l{l}_{j}")
        return token[0, 0]

    for l in reversed(range(depth)):
        dh, per_layer[l] = _layer_bwd(dh, layers[l], saved[l], l, functools.partial(send, l, 1),
                                      functools.partial(send, l, 0))
    grad_x = dh[HEAD_ROWS:][None]
    grads["meta_tokens"] = dh[PAD:HEAD_ROWS]
    lane = {"gdn_a_log": (8, 8), "gdn_dt_bias": (8, 8), "gdn_norm_w": (0, 128), "ssd_dt_bias": (0, 16),
            "ssd_a_log": (0, 16), "ssd_d": (0, 16), "swa_sinks": (0, 16)}
    for n in per_layer[0]:
        parts = [per_layer[l][n] for l in range(depth)]
        if n in lane:
            parts = [q[0, lane[n][0]:lane[n][0] + lane[n][1]] for q in parts]
        elif n in ("norm1_w", "norm2_w", "ssd_norm_w"):
            parts = [q[0] for q in parts]
        grads[n] = jnp.stack(parts)
    loss = lax.psum(loss8[0, 0], ("x", "y", "c"))

    gs = _pack_rows([grads[n] for n in SMALL_NAMES], f32)
    def chip_sum(l, j, after):
        (zone,) = exchange_wait(scatters[l, j], after, name=f"scatter_wait_l{l}_{j}")
        own = lax.dynamic_index_in_dim(grad_slabs[l, j], me, 0, keepdims=False)
        return reduce4(zone, own=own, me=me1, name=f"sum_chips_l{l}_{j}")

    early = [(l, j) for l in range(depth) for j in range(2) if (l, j) != (0, 0)]
    mine = {lj: chip_sum(*lj, dh) for lj in early}
    sibs = dict(zip(early, sibling_swap([mine[lj] for lj in early], name="swap_cores_early")))
    out = {}
    for n, r in BIG:
        shp = given[n].shape
        res = adamw(*[given[pre + n].reshape(depth * r, D_MODEL) for pre in ("", "m_", "v_")],
                    [(mine[l, 1], sibs[l, 1]) for l in range(depth)], BIG_OFF[n], name=f"adamw_{n}")
        out[n] = [a.reshape(shp) for a in res]
    mine[0, 0] = chip_sum(0, 0, res[1])
    (rs,) = chip_exchange([gs], (False,), after=mine[0, 0], name="gather_small_grads")
    ps_ = reduce4(rs, name="sum_chips_small")
    sibs[0, 0], ss = sibling_swap([mine[0, 0], ps_], name="swap_cores_last")
    res = adamw(*[jnp.swapaxes(given[pre + "w_in"], 1, 2).reshape(depth * W_IN_SHARD, D_MODEL)
                  for pre in ("", "m_", "v_")],
                [(mine[l, 0], sibs[l, 0]) for l in range(depth)], 0, name="adamw_w_in")
    out["w_in"] = [jnp.swapaxes(a.reshape(w_in_t.shape), 1, 2) for a in res]
    full_shapes = [grads[n].shape for n in SMALL_NAMES]
    mine_s, sib_s = _unpack_rows(ps_, full_shapes), _unpack_rows(ss, full_shapes)

    def local(parts):
        loc = []
        for n, a in zip(SMALL_NAMES, parts):
            if n in SMALL_SHARDED:
                sz = a.shape[-1] // 4
                a = lax.dynamic_slice_in_dim(a, me * sz, sz, axis=a.ndim - 1)
            loc.append(a)
        return _pack_rows(loc, f32)

    res = adamw(_pack_rows([given[n] for n in SMALL_NAMES], f32), _pack_rows([given["m_" + n] for n in SMALL_NAMES], f32),
                _pack_rows([given["v_" + n] for n in SMALL_NAMES], f32), [(local(mine_s), local(sib_s))], 0,
                name="adamw_small")
    local_shapes = [given[n].shape for n in SMALL_NAMES]
    unpacked = [_unpack_rows(a, local_shapes) for a in res]
    for i, n in enumerate(SMALL_NAMES):
        out[n] = [unpacked[j][i] for j in range(4)]

    return (loss, grad_x) + tuple(out[n][j] for j in range(4) for n in W_NAMES)
```

```python
import functools

import jax
import jax.numpy as jnp
from jax import lax
from jax.experimental import pallas as pl
from jax.experimental.pallas import tpu as pltpu

f32 = jnp.float32
bf16 = jnp.bfloat16

D_MODEL = 1024
N_META = 16
PAD = 112
HEAD_ROWS = PAD + N_META
RMS_EPS = 1e-6
L2_EPS = 1e-6
D_FF = 4 * D_MODEL

GDN_HEADS = 8
GDN_D = 128
GDN_CHUNK = 64
SSD_HEADS = 16
SSD_P = 64
SSD_GROUPS = 4
SSD_HPG = 4
SSD_N = 128
SSD_CHUNK = 128
SWA_Q_HEADS = 16
SWA_KV_HEADS = 4
SWA_REP = 4
SWA_D = 64
SWA_W = 128

C_GATE = 0
C_SZ, C_SX, C_SB, C_SC = 3072, 4096, 5120, 5632
C_WQ, C_WK, C_WV = 6144, 7168, 7424
C_BA = 7680
C_SDT = 7808
C_MID_END = 8192
C_GQ, C_GK, C_GV, C_GG = 8192, 9216, 10240, 11264
IN_WP = 12288
IN_W = 11808

ADAM_LR, ADAM_B1, ADAM_B2, ADAM_EPS, ADAM_WD, ADAM_STEP = 0.001, 0.9, 0.999, 1e-08, 0.01, 10

VMEM_LIMIT = 56 * 1024 * 1024
BLOCK_BYTES = 3 << 19
MM_OPERAND_BYTES = 9 << 20
MM_RESIDENT_BYTES = 13 << 20

NN = (((1,), (0,)), ((), ()))
NT = (((1,), (1,)), ((), ()))
TN = (((0,), (0,)), ((), ()))


def _dot(a, b, dims=NN):
    return lax.dot_general(a.astype(bf16), b.astype(bf16), dims, preferred_element_type=f32)


def _dotx(a, b, dims=NN):
    return lax.dot_general(a, b, dims, preferred_element_type=f32, precision=lax.Precision.HIGH)


def _iota(shape, axis):
    return lax.broadcasted_iota(jnp.int32, shape, axis)


def _softplus(x):
    return jnp.maximum(x, 0.0) + jnp.log1p(jnp.exp(-jnp.abs(x)))


_sigmoid = jax.nn.sigmoid


def _silu(x):
    return x * _sigmoid(x)


def _params(sem):
    return pltpu.CompilerParams(dimension_semantics=sem, vmem_limit_bytes=VMEM_LIMIT)


@functools.partial(jax.custom_vjp, nondiff_argnums=(1,))
def _window(x_ext, off):
    if off == 8:
        return x_ext[8:]
    return pltpu.roll(x_ext, 8 - off, 0)[8:]


def _window_fwd(x_ext, off):
    return _window(x_ext, off), None


def _window_bwd(off, _, g):
    n, w = g.shape
    g_ext = jnp.concatenate([jnp.zeros((8, w), g.dtype), g], axis=0)
    if off == 8:
        return (g_ext,)
    return (pltpu.roll(g_ext, n + off, 0),)


_window.defvjp(_window_fwd, _window_bwd)


def _conv4(x, halo, taps):
    x_ext = jnp.concatenate([halo, x], axis=0)
    y = taps[3] * x
    for j in range(3):
        y = y + taps[j] * _window(x_ext, 5 + j)
    return y


def _blockinv_impl(a):
    n = a.shape[0]
    ri, ci = _iota((n, n), 0), _iota((n, n), 1)
    t = (ri == ci).astype(f32) - jnp.where(((ri >> 1) == (ci >> 1)) & (ri > ci), a, 0.0)
    k = 1
    while (1 << k) < n:
        sel = ((ri >> (k + 1)) == (ci >> (k + 1))) & (((ri >> k) & 1) == 1) & (((ci >> k) & 1) == 0)
        o = jnp.where(sel, a, 0.0)
        t = t - _dotx(_dotx(t, o), t)
        k += 1
    return t


@jax.custom_vjp
def _blockinv(a):
    return _blockinv_impl(a)


def _blockinv_fwd(a):
    t = _blockinv_impl(a)
    return t, t


def _blockinv_bwd(t, dt):
    return (-_dotx(_dotx(t, dt, TN), t, NT),)


_blockinv.defvjp(_blockinv_fwd, _blockinv_bwd)


@jax.custom_vjp
def _blockinv_given(a, t):
    return t


_blockinv_given.defvjp(lambda a, t: (t, t), lambda t, dt: _blockinv_bwd(t, dt) + (jnp.zeros_like(t),))


def _scan_rows(x, reverse):
    n = x.shape[0]
    row = _iota(x.shape, 0)
    s = 1
    while s < n:
        if reverse:
            x = x + jnp.where(row < n - s, pltpu.roll(x, n - s, 0), 0.0)
        else:
            x = x + jnp.where(row >= s, pltpu.roll(x, s, 0), 0.0)
        s *= 2
    return x


@jax.custom_vjp
def _cumsum_rows(x):
    return _scan_rows(x, False)


_cumsum_rows.defvjp(lambda x: (_scan_rows(x, False), None), lambda _, g: (_scan_rows(g, True),))


def _gdn_act(xq, xk, xv, hq, hk, hv, tq, tk, tv):
    return _silu(_conv4(xq, hq, tq)), _silu(_conv4(xk, hk, tk)), _silu(_conv4(xv, hv, tv))


def _gdn_core(q, k, v, gate, mb, mg, mr, s, t_given, beta16, g16, gam16, gam16_t, nw):
    c = GDN_CHUNK
    q = q * lax.rsqrt(jnp.sum(q * q, axis=1, keepdims=True) + L2_EPS) * (GDN_D ** -0.5)
    k = k * lax.rsqrt(jnp.sum(k * k, axis=1, keepdims=True) + L2_EPS)

    pick = lambda x, m: jnp.sum(x * m, axis=1, keepdims=True)
    beta = pick(beta16, mb)
    g = jnp.broadcast_to(pick(g16, mg), (c, GDN_D))
    gam1 = pick(gam16, mg)
    gam = jnp.broadcast_to(gam1, (c, GDN_D))
    gam_j = jnp.broadcast_to(jnp.sum(gam16_t * mr, axis=0, keepdims=True), (c, c))

    ri, ci = _iota((c, c), 0), _iota((c, c), 1)
    incl = ci <= ri
    decay = jnp.where(incl, jnp.exp(jnp.where(incl, jnp.broadcast_to(gam1, (c, c)) - gam_j, 0.0)), 0.0)

    kb = k * beta
    a = jnp.where(ci < ri, _dot(kb, k, NT) * decay, 0.0)
    t = _blockinv(a) if t_given is None else _blockinv_given(a, t_given)
    egam = jnp.exp(gam)
    u = _dotx(t, v * beta)
    w = _dotx(t, kb * egam)
    attn = _dot(q, k, NT) * decay
    gl = jnp.sum(g, axis=0, keepdims=True)
    kt = k * jnp.exp(gl - gam)
    v_new = u - _dot(w, s)
    o = _dot(q * egam, s) + _dot(attn, v_new)
    s_out = s * jnp.exp(gl) + _dot(kt, v_new, TN)

    y = o * lax.rsqrt(jnp.mean(o * o, axis=1, keepdims=True) + RMS_EPS) * nw * _silu(gate)
    return y, s_out, t


def _gdn_chunk(q, k, v, gate, s, t_given, ba, alog, dtb, nw, *, masks, row0):
    valid = (row0 + _iota((GDN_CHUNK, 1), 0)) >= PAD
    beta16 = jnp.where(valid, _sigmoid(ba), 0.0)
    g16 = jnp.where(valid, -jnp.exp(alog) * _softplus(ba + dtb), 0.0)
    gam16 = _cumsum_rows(g16)
    core = jax.vmap(_gdn_core, in_axes=(0,) * 8 + (None if t_given is None else 0,) + (None,) * 5)
    y, s_out, t = core(q, k, v, gate, *masks, s, t_given, beta16, g16, gam16, gam16.T, nw)
    return (y, s_out, t) if t_given is None else (y, s_out)


def _gdn_specs(hb, nc, rev):
    w = hb * GDN_D

    def cidx(c):
        return (nc - 1 - c) if rev else c

    def col(base):
        return pl.BlockSpec((GDN_CHUNK, w), lambda h, c: (cidx(c), base // w + h))

    def halo(base):
        return pl.BlockSpec((8, w), lambda h, c: (jnp.maximum(cidx(c) * (GDN_CHUNK // 8) - 1, 0), base // w + h))

    def taps(base):
        return pl.BlockSpec((8, w), lambda h, c: (0, base // w + h))

    ba = pl.BlockSpec((GDN_CHUNK, 128), lambda h, c: (cidx(c), C_BA // 128))
    row = pl.BlockSpec((8, 128), lambda h, c: (0, 0))
    y = pl.BlockSpec((GDN_CHUNK, w), lambda h, c: (cidx(c), h))
    st = pl.BlockSpec((1, hb, GDN_D, GDN_D), lambda h, c: (cidx(c), h, 0, 0))
    in_specs = [col(C_GQ), col(C_GK), col(C_GV), halo(C_GQ), halo(C_GK), halo(C_GV), col(C_GG), ba,
                taps(0), taps(1024), taps(2048), row, row, row]
    return in_specs, y, st, taps, row, col, ba


def _gdn_load(refs, first):
    xq, xk, xv, hq, hk, hv, gate, ba, tq, tk, tv, alog, dtb, nw = refs

    def halo(r):
        return jnp.where(first, 0.0, r[...])

    def taps(r):
        return tuple(r[j:j + 1, :] for j in range(4))

    act = (xq[...], xk[...], xv[...], halo(hq), halo(hk), halo(hv), taps(tq), taps(tk), taps(tv))
    return act, gate[...], (ba[...], alog[0:1, :], dtb[0:1, :], nw[0:1, :])


def _heads(a, hb):
    return jnp.stack([a[:, i * GDN_D:(i + 1) * GDN_D] for i in range(hb)])


def _wide(a):
    return jnp.concatenate([a[i] for i in range(a.shape[0])], axis=1)


def _head_masks(hblk, hb):
    head = hblk * hb + _iota((hb, 1, 128), 0)
    lane = _iota((hb, 1, 128), 2)
    rows = (_iota((hb, 128, 1), 1) == hblk * hb + _iota((hb, 128, 1), 0) + 8).astype(f32)
    return (lane == head).astype(f32), (lane == head + 8).astype(f32), rows


def gdn_fwd(u, conv_w8, alog8, dtb8, nw8, *, hb=8):
    t_rows = u.shape[0]
    nc = t_rows // GDN_CHUNK
    in_specs, y_spec, st_spec, *_ = _gdn_specs(hb, nc, False)

    def body(*refs):
        ins, (y_ref, st_ref, t_ref), (s_scr,) = refs[:14], refs[14:17], refs[17:]
        hblk, c = pl.program_id(0), pl.program_id(1)

        @pl.when(c == 0)
        def _():
            s_scr[...] = jnp.zeros_like(s_scr)

        act, gate, shared = _gdn_load(ins, c == 0)
        s = s_scr[...]
        st_ref[0] = s
        qa, ka, va = _gdn_act(*act)
        y, s_new, t = _gdn_chunk(_heads(qa, hb), _heads(ka, hb), _heads(va, hb), _heads(gate, hb), s, None, *shared,
                                 masks=_head_masks(hblk, hb), row0=c * GDN_CHUNK)
        y_ref[...] = _wide(y).astype(bf16)
        t_ref[0] = t
        s_scr[...] = s_new

    return pl.pallas_call(
        body, name="gdn_fwd", grid=(GDN_HEADS // hb, nc),
        in_specs=in_specs,
        out_specs=(y_spec, st_spec, pl.BlockSpec((1, hb, GDN_CHUNK, GDN_CHUNK), lambda h, c: (c, h, 0, 0))),
        out_shape=(jax.ShapeDtypeStruct((t_rows, D_MODEL), bf16),
                   jax.ShapeDtypeStruct((nc, GDN_HEADS, GDN_D, GDN_D), f32),
                   jax.ShapeDtypeStruct((nc, GDN_HEADS, GDN_CHUNK, GDN_CHUNK), f32)),
        scratch_shapes=[pltpu.VMEM((hb, GDN_D, GDN_D), f32)],
        compiler_params=_params(("arbitrary", "arbitrary")),
    )(u, u, u, u, u, u, u, u, conv_w8, conv_w8, conv_w8, alog8, dtb8, nw8)


def gdn_bwd(u, conv_w8, alog8, dtb8, nw8, states, tinv, dy, du):
    t_rows = u.shape[0]
    nc = t_rows // GDN_CHUNK
    hb = GDN_HEADS
    w = hb * GDN_D
    in_specs, y_spec, st_spec, taps, row, col, ba = _gdn_specs(hb, nc, True)
    nhb = GDN_HEADS // hb

    def body(*refs):
        ins, st_ref, t_ref, dy_ref = refs[:14], refs[14], refs[15], refs[16]
        du_ref, dba_ref, dtq_ref, dtk_ref, dtv_ref, dalog_ref, ddtb_ref, dnw_ref = refs[18:26]
        ds_scr, dh_scr = refs[26:]
        hblk, cc = pl.program_id(0), pl.program_id(1)
        c = nc - 1 - cc

        @pl.when(cc == 0)
        def _():
            ds_scr[...] = jnp.zeros_like(ds_scr)
            dh_scr[...] = jnp.zeros_like(dh_scr)
            dtq_ref[...] = jnp.zeros_like(dtq_ref)
            dtk_ref[...] = jnp.zeros_like(dtk_ref)
            dtv_ref[...] = jnp.zeros_like(dtv_ref)

        @pl.when((cc == 0) & (hblk == 0))
        def _():
            dalog_ref[...] = jnp.zeros_like(dalog_ref)
            ddtb_ref[...] = jnp.zeros_like(ddtb_ref)
            dnw_ref[...] = jnp.zeros_like(dnw_ref)

        act, gate, shared = _gdn_load(ins, c == 0)
        (qa, ka, va), vjp_act = jax.vjp(_gdn_act, *act)
        chunk = functools.partial(_gdn_chunk, masks=_head_masks(hblk, hb), row0=c * GDN_CHUNK)
        _, vjp_core = jax.vjp(chunk, _heads(qa, hb), _heads(ka, hb), _heads(va, hb), _heads(gate, hb), st_ref[0],
                              t_ref[0], *shared)
        dqa, dka, dva, dgate, ds, _, dba, dalog, ddtb, dnw = vjp_core(
            (_heads(dy_ref[...].astype(f32), hb), ds_scr[...]))
        ds_scr[...] = ds
        dxq, dxk, dxv, dhq, dhk, dhv, dtq, dtk, dtv = vjp_act((_wide(dqa), _wide(dka), _wide(dva)))
        zeros = jnp.zeros((GDN_CHUNK - 8, w), f32)
        for j, (dx, dh) in enumerate(((dxq, dhq), (dxk, dhk), (dxv, dhv))):
            du_ref[:, j * w:(j + 1) * w] = (dx + jnp.concatenate([zeros, dh_scr[j]], axis=0)).astype(bf16)
            dh_scr[j] = dh
        du_ref[:, 3 * w:4 * w] = _wide(dgate).astype(bf16)
        dba_ref[0] = dba
        for dt_ref, dtaps in ((dtq_ref, dtq), (dtk_ref, dtk), (dtv_ref, dtv)):
            for j in range(4):
                dt_ref[j:j + 1, :] += dtaps[j]
        dalog_ref[0:1, :] += dalog
        ddtb_ref[0:1, :] += ddtb
        dnw_ref[0:1, :] += dnw

    out_specs = (pl.BlockSpec((GDN_CHUNK, 4 * w), lambda h, c: (nc - 1 - c, C_GQ // (4 * w))),
                 pl.BlockSpec((1, GDN_CHUNK, 128), lambda h, c: (h, nc - 1 - c, 0)),
                 taps(0), taps(0), taps(0), row, row, row)
    out_shape = (jax.ShapeDtypeStruct(du.shape, du.dtype),
                 jax.ShapeDtypeStruct((nhb, t_rows, 128), f32),
                 jax.ShapeDtypeStruct((8, D_MODEL), f32), jax.ShapeDtypeStruct((8, D_MODEL), f32),
                 jax.ShapeDtypeStruct((8, D_MODEL), f32),
                 jax.ShapeDtypeStruct((8, 128), f32), jax.ShapeDtypeStruct((8, 128), f32), jax.ShapeDtypeStruct((8, 128), f32))
    return pl.pallas_call(
        body, name="gdn_bwd", grid=(nhb, nc),
        in_specs=in_specs + [st_spec, pl.BlockSpec((1, hb, GDN_CHUNK, GDN_CHUNK), lambda h, c: (nc - 1 - c, h, 0, 0)),
                             y_spec, ANY],
        out_specs=out_specs, out_shape=out_shape, input_output_aliases={17: 0},
        scratch_shapes=[pltpu.VMEM((hb, GDN_D, GDN_D), f32), pltpu.VMEM((3, 8, w), f32)],
        compiler_params=_params(("arbitrary", "arbitrary")),
    )(u, u, u, u, u, u, u, u, conv_w8, conv_w8, conv_w8, alog8, dtb8, nw8, states, tinv, dy, du)


def _ssd_act(xs_r, b_r, c_r, hx, hbm, hcm, tx, tb, tc, bx, bb, bc, *, row0):
    valid = (row0 + _iota((SSD_CHUNK, 1), 0)) >= PAD
    act = lambda x, h, t, b: jnp.where(valid, _silu(_conv4(x, h, t) + b), 0.0)
    return act(xs_r, hx, tx, bx), act(b_r, hbm, tb, bb), act(c_r, hcm, tc, bc)


def _ssd_core(xs, bm, cm, z, nw, lanes, rows, h, dtp16, adt16, acum16, acum16_t, dsk):
    n = SSD_CHUNK
    pick = lambda x, m: jnp.sum(x * m, axis=1, keepdims=True)
    lane_r = _iota((1, 256), 1) >> 6

    def per_lane(x16):
        cols = [pick(x16, lanes[r]) for r in range(SSD_HPG)]
        return cols, jnp.concatenate([jnp.broadcast_to(c, (c.shape[0], SSD_P)) for c in cols], axis=1)

    _, dtp = per_lane(dtp16)
    _, adt = per_lane(adt16)
    ccols, acum = per_lane(acum16)
    _, dlane = per_lane(dsk)

    ri, ci = _iota((n, n), 0), _iota((n, n), 1)
    incl = ci <= ri
    al = jnp.sum(adt, axis=0, keepdims=True)
    xdt = xs * dtp
    cb = _dot(cm, bm, NT)
    y = _dot(cm, h) * jnp.exp(acum) + dlane * xs
    for r in range(SSD_HPG):
        ai = jnp.broadcast_to(ccols[r], (n, n))
        aj = jnp.broadcast_to(jnp.sum(acum16_t * rows[r], axis=0, keepdims=True), (n, n))
        lm = jnp.where(incl, jnp.exp(jnp.where(incl, ai - aj, 0.0)), 0.0)
        y = y + _dot(cb * lm, jnp.where(lane_r == r, xdt, 0.0))
    h_out = h * jnp.exp(al) + _dot(bm, jnp.exp(al - acum) * xdt, TN)
    y = y * _silu(z)
    y = y * lax.rsqrt(jnp.mean(y * y, axis=1, keepdims=True) + RMS_EPS) * nw
    return y, h_out


def _ssd_chunk(xs, bm, cm, z, nw, h, dt, dtb, alog, dsk, *, row0):
    valid = (row0 + _iota((SSD_CHUNK, 1), 0)) >= PAD
    dtp16 = jnp.where(valid, _softplus(dt + dtb), 0.0)
    adt16 = -jnp.exp(alog) * dtp16
    acum16 = _cumsum_rows(adt16)
    lanes = tuple((_iota((SSD_GROUPS, 1, 128), 2) == _iota((SSD_GROUPS, 1, 128), 0) * SSD_HPG + r).astype(f32)
                  for r in range(SSD_HPG))
    rows = tuple((_iota((SSD_GROUPS, 128, 1), 1) == _iota((SSD_GROUPS, 128, 1), 0) * SSD_HPG + r).astype(f32)
                 for r in range(SSD_HPG))
    core = jax.vmap(_ssd_core, in_axes=(0,) * 8 + (None,) * 5)
    return core(xs, bm, cm, z, nw, lanes, rows, h, dtp16, adt16, acum16, acum16.T, dsk)


def _ssd_specs(nc, rev):
    n = SSD_CHUNK

    def cidx(c):
        return (nc - 1 - c) if rev else c

    def col(base, w):
        return pl.BlockSpec((n, w), lambda c: (cidx(c), base // w))

    def halo(base, w):
        return pl.BlockSpec((8, w), lambda c: (jnp.maximum(cidx(c) * (n // 8) - 1, 0), base // w))

    def taps(base, w):
        return pl.BlockSpec((8, w), lambda c: (0, base // w))

    row = pl.BlockSpec((8, 128), lambda c: (0, 0))
    in_specs = [col(C_SX, 1024), col(C_SB, 512), col(C_SC, 512), halo(C_SX, 1024), halo(C_SB, 512), halo(C_SC, 512),
                col(C_SZ, 1024), col(C_SDT, 128), taps(0, 1024), taps(1024, 512), taps(1536, 512), row, row, row,
                taps(0, 1024)]
    y = pl.BlockSpec((n, D_MODEL), lambda c: (cidx(c), 0))
    st = pl.BlockSpec((1, SSD_GROUPS, SSD_N, 256), lambda c: (cidx(c), 0, 0, 0))
    return in_specs, y, st, col, taps, row


def _ssd_load(refs, first):
    xs, bm, cm, hx, hbm, hcm, z, dt, tx, tb, tc, dtb, alog, dsk, nw = refs

    def halo(r):
        return jnp.where(first, 0.0, r[...])

    def taps(r):
        return tuple(r[j:j + 1, :] for j in range(4))

    act = (xs[...], bm[...], cm[...], halo(hx), halo(hbm), halo(hcm), taps(tx), taps(tb), taps(tc),
           tx[4:5, :], tb[4:5, :], tc[4:5, :])
    return act, (z[...], nw[0:1, :]), (dt[...], dtb[0:1, :], alog[0:1, :], dsk[0:1, :])


def _groups(a, w):
    return jnp.stack([a[:, i * w:(i + 1) * w] for i in range(SSD_GROUPS)])


def ssd_fwd(u, conv_w8, dtb8, alog8, d8, nw8):
    t_rows = u.shape[0]
    nc = t_rows // SSD_CHUNK
    in_specs, y_spec, st_spec, *_ = _ssd_specs(nc, False)

    def body(*refs):
        ins, (y_ref, st_ref), (h_scr,) = refs[:15], refs[15:17], refs[17:]
        c = pl.program_id(0)

        @pl.when(c == 0)
        def _():
            h_scr[...] = jnp.zeros_like(h_scr)

        act, (z, nw), shared = _ssd_load(ins, c == 0)
        h = h_scr[...]
        st_ref[0] = h
        xs, bm, cm = _ssd_act(*act, row0=c * SSD_CHUNK)
        y, h_new = _ssd_chunk(_groups(xs, 256), _groups(bm, 128), _groups(cm, 128), _groups(z, 256),
                              _groups(nw, 256), h, *shared, row0=c * SSD_CHUNK)
        y_ref[...] = _wide(y).astype(bf16)
        h_scr[...] = h_new

    return pl.pallas_call(
        body, name="ssd_fwd", grid=(nc,), in_specs=in_specs, out_specs=(y_spec, st_spec),
        out_shape=(jax.ShapeDtypeStruct((t_rows, D_MODEL), bf16),
                   jax.ShapeDtypeStruct((nc, SSD_GROUPS, SSD_N, 256), f32)),
        scratch_shapes=[pltpu.VMEM((SSD_GROUPS, SSD_N, 256), f32)],
        compiler_params=_params(("arbitrary",)),
    )(u, u, u, u, u, u, u, u, conv_w8, conv_w8, conv_w8, dtb8, alog8, d8, nw8)


def ssd_bwd(u, conv_w8, dtb8, alog8, d8, nw8, states, dy, du):
    t_rows = u.shape[0]
    nc = t_rows // SSD_CHUNK
    n = SSD_CHUNK
    in_specs, y_spec, st_spec, col, taps, row = _ssd_specs(nc, True)

    def body(*refs):
        ins, st_ref, dy_ref = refs[:15], refs[15], refs[16]
        du_ref, ddt_ref, dtx_ref, dtb_ref, dtc_ref, ddtb_ref, dalog_ref, ddsk_ref, dnw_ref = refs[18:27]
        dh_scr, hx_scr, hb_scr, hc_scr = refs[27:]
        cc = pl.program_id(0)
        c = nc - 1 - cc

        @pl.when(cc == 0)
        def _():
            for r in (dh_scr, hx_scr, hb_scr, hc_scr, dtx_ref, dtb_ref, dtc_ref, dnw_ref, ddtb_ref, dalog_ref, ddsk_ref):
                r[...] = jnp.zeros_like(r)

        act, (z, nw), shared = _ssd_load(ins, c == 0)
        (xs, bm, cm), vjp_act = jax.vjp(functools.partial(_ssd_act, row0=c * n), *act)
        _, vjp_core = jax.vjp(functools.partial(_ssd_chunk, row0=c * n), _groups(xs, 256), _groups(bm, 128),
                              _groups(cm, 128), _groups(z, 256), _groups(nw, 256), st_ref[0], *shared)
        dxa, dba, dca, dz, dnw, dh, ddt, ddtb, dalog, ddsk = vjp_core(
            (_groups(dy_ref[...].astype(f32), 256), dh_scr[...]))
        dh_scr[...] = dh
        dxs, dbm, dcm, dhx, dhb, dhc, dtx, dtb, dtc, dbx, dbb, dbc = vjp_act((_wide(dxa), _wide(dba), _wide(dca)))
        du_ref[:, 0:D_MODEL] = _wide(dz).astype(bf16)
        for dx, dhalo, scr, lo in ((dxs, dhx, hx_scr, C_SX), (dbm, dhb, hb_scr, C_SB), (dcm, dhc, hc_scr, C_SC)):
            zeros = jnp.zeros((n - 8, dx.shape[1]), f32)
            du_ref[:, lo - C_SZ:lo - C_SZ + dx.shape[1]] = (dx + jnp.concatenate([zeros, scr[...]], axis=0)).astype(bf16)
            scr[...] = dhalo
        ddt_ref[...] = ddt
        for ref, dtaps, dbias in ((dtx_ref, dtx, dbx), (dtb_ref, dtb, dbb), (dtc_ref, dtc, dbc)):
            for j in range(4):
                ref[j:j + 1, :] += dtaps[j]
            ref[4:5, :] += dbias
        ddtb_ref[0:1, :] += ddtb
        dalog_ref[0:1, :] += dalog
        ddsk_ref[0:1, :] += ddsk
        dnw_ref[0:1, :] += _wide(dnw)

    def out_col(w):
        return pl.BlockSpec((n, w), lambda c: (nc - 1 - c, 0))

    out_specs = (pl.BlockSpec((n, 3 * D_MODEL), lambda c: (nc - 1 - c, C_SZ // (3 * D_MODEL))), out_col(128),
                 taps(0, D_MODEL), taps(0, 512), taps(0, 512), row, row, row, taps(0, D_MODEL))
    out_shape = (jax.ShapeDtypeStruct(du.shape, du.dtype),
                 jax.ShapeDtypeStruct((t_rows, 128), f32),
                 jax.ShapeDtypeStruct((8, D_MODEL), f32), jax.ShapeDtypeStruct((8, 512), f32),
                 jax.ShapeDtypeStruct((8, 512), f32),
                 jax.ShapeDtypeStruct((8, 128), f32), jax.ShapeDtypeStruct((8, 128), f32),
                 jax.ShapeDtypeStruct((8, 128), f32), jax.ShapeDtypeStruct((8, D_MODEL), f32))
    return pl.pallas_call(
        body, name="ssd_bwd", grid=(nc,), in_specs=in_specs + [st_spec, y_spec, ANY],
        out_specs=out_specs, out_shape=out_shape, input_output_aliases={17: 0},
        scratch_shapes=[pltpu.VMEM((SSD_GROUPS, SSD_N, 256), f32), pltpu.VMEM((8, D_MODEL), f32),
                        pltpu.VMEM((8, 512), f32), pltpu.VMEM((8, 512), f32)],
        compiler_params=_params(("arbitrary",)),
    )(u, u, u, u, u, u, u, u, conv_w8, conv_w8, conv_w8, dtb8, alog8, d8, nw8, states, dy, du)


NEG = -1e30


def _swa_core(q, kc, kp, km, vc, vp, vm, sink, *, n):
    rows = SWA_REP * SWA_W
    ri, ci = _iota((rows, SWA_W), 0) & (SWA_W - 1), _iota((rows, SWA_W), 1)
    causal = ci <= ri
    m_band = (causal & ((n >= 1) | ((ci >= PAD) & (ri >= PAD)))) | ((ci > ri) & (n >= 2))
    m_meta = (n >= 1) & (ci >= PAD)
    q = q * (SWA_D ** -0.5)
    s = jnp.where(m_band, jnp.where(causal, _dot(q, kc, NT), _dot(q, kp, NT)), NEG)
    sm = jnp.where(m_meta, _dot(q, km, NT), NEG)
    mx = jnp.maximum(jnp.maximum(jnp.max(s, axis=1, keepdims=True), jnp.max(sm, axis=1, keepdims=True)), sink)
    mx = lax.stop_gradient(mx)
    e, em = jnp.exp(s - mx), jnp.exp(sm - mx)
    den = jnp.sum(e, axis=1, keepdims=True) + jnp.sum(em, axis=1, keepdims=True) + jnp.exp(sink - mx)
    return (_dot(jnp.where(causal, e, 0.0), vc) + _dot(jnp.where(causal, 0.0, e), vp) + _dot(em, vm)) / den


def _swa_block(q16, kc, kp, km, vc, vp, vm, sink16, *, n):
    rows = SWA_REP * SWA_W
    lane = _iota((1, 128), 1)
    cols = []
    for h in range(SWA_KV_HEADS):
        sinks = [jnp.sum(jnp.where(lane == h * SWA_REP + r, sink16, 0.0), axis=1, keepdims=True) for r in range(SWA_REP)]
        cols.append(jnp.concatenate([jnp.broadcast_to(s, (SWA_W, 1)) for s in sinks], axis=0))
    o = jax.vmap(functools.partial(_swa_core, n=n))(q16.reshape(SWA_KV_HEADS, rows, SWA_D), kc, kp, km, vc, vp, vm,
                                                    jnp.concatenate([col[None] for col in cols], axis=0))
    return o.reshape(q16.shape)


def _swa_specs(nb, rev):
    def bidx(n):
        return (nb - 1 - n) if rev else n

    kvw = SWA_KV_HEADS * SWA_D
    q = pl.BlockSpec((SWA_W, D_MODEL), lambda n: (bidx(n), C_WQ // D_MODEL))

    def kv(base, blk):
        return pl.BlockSpec((SWA_W, kvw), lambda n: (blk(bidx(n)), base // kvw))

    cur, prev, meta = (lambda n: n), (lambda n: jnp.maximum(n - 1, 0)), (lambda n: 0)
    row = pl.BlockSpec((8, 128), lambda n: (0, 0))
    in_specs = [q] + [kv(C_WK, b) for b in (cur, prev, meta)] + [kv(C_WV, b) for b in (cur, prev, meta)] + [row]
    return in_specs, pl.BlockSpec((SWA_W, D_MODEL), lambda n: (bidx(n), 0)), row


def _swa_heads(a):
    return jnp.stack([a[:, i * SWA_D:(i + 1) * SWA_D] for i in range(a.shape[1] // SWA_D)])


def swa_fwd(u, sink8):
    t_rows = u.shape[0]
    nb = t_rows // SWA_W
    in_specs, o_spec, _ = _swa_specs(nb, False)

    def body(q_ref, kc, kp, km, vc, vp, vm, sink_ref, o_ref):
        o = _swa_block(*[_swa_heads(r[...]) for r in (q_ref, kc, kp, km, vc, vp, vm)], sink_ref[0:1, :],
                       n=pl.program_id(0))
        o_ref[...] = _wide(o).astype(bf16)

    return pl.pallas_call(
        body, name="swa_fwd", grid=(nb,), in_specs=in_specs, out_specs=o_spec,
        out_shape=jax.ShapeDtypeStruct((t_rows, D_MODEL), bf16),
        compiler_params=_params(("arbitrary",)),
    )(u, u, u, u, u, u, u, sink8)


def swa_bwd(u, sink8, do, du):
    t_rows = u.shape[0]
    nb = t_rows // SWA_W
    in_specs, o_spec, row = _swa_specs(nb, True)
    width = C_BA - C_WQ

    def body(q_ref, kc, kp, km, vc, vp, vm, sink_ref, do_ref, _, du_ref, dsink_ref,
             dkp_scr, dvp_scr, dkm_scr, dvm_scr):
        nn = pl.program_id(0)
        n = nb - 1 - nn

        @pl.when(nn == 0)
        def _():
            for r in (dkp_scr, dvp_scr, dkm_scr, dvm_scr, dsink_ref):
                r[...] = jnp.zeros_like(r)

        fn = functools.partial(_swa_block, n=n)
        _, vjp = jax.vjp(fn, *[_swa_heads(r[...]) for r in (q_ref, kc, kp, km, vc, vp, vm)], sink_ref[0:1, :])
        dq, dkc, dkp, dkm, dvc, dvp, dvm, dsink = vjp(_swa_heads(do_ref[...]))
        dkm_scr[...] += dkm
        dvm_scr[...] += dvm
        first = n == 0
        dk = dkc + dkp_scr[...] + jnp.where(first, dkm_scr[...], 0.0)
        dv = dvc + dvp_scr[...] + jnp.where(first, dvm_scr[...], 0.0)
        du_ref[:, 0:D_MODEL] = _wide(dq).astype(bf16)
        du_ref[:, C_WK - C_WQ:C_WV - C_WQ] = _wide(dk).astype(bf16)
        du_ref[:, C_WV - C_WQ:width] = _wide(dv).astype(bf16)
        dkp_scr[...] = dkp
        dvp_scr[...] = dvp
        dsink_ref[0:1, :] += dsink

    return pl.pallas_call(
        body, name="swa_bwd", grid=(nb,), in_specs=in_specs + [o_spec, ANY],
        out_specs=(pl.BlockSpec((SWA_W, width), lambda n: (nb - 1 - n, C_WQ // width)), row),
        out_shape=(jax.ShapeDtypeStruct(du.shape, du.dtype), jax.ShapeDtypeStruct((8, 128), f32)),
        input_output_aliases={9: 0},
        scratch_shapes=[pltpu.VMEM((SWA_KV_HEADS, SWA_W, SWA_D), f32)] * 4,
        compiler_params=_params(("arbitrary",)),
    )(u, u, u, u, u, u, u, sink8, do, du)


def _tile(dim, prefs):
    for p in prefs:
        if dim % p == 0:
            return p
    return dim


def _row_tile(rows, d):
    for p in range(min(rows, BLOCK_BYTES // (4 * d)) // 8 * 8, 0, -8):
        if rows % p == 0:
            return p
    return rows


def mm(a, b, *, out_dtype, name, resid=None, relu_grad_of=None, relu2_out=False, ta=False, tb=False, norm_w8=None):
    assert resid is None or relu_grad_of is None
    k, m = (a.shape if ta else a.shape[::-1])
    n = b.shape[0] if tb else b.shape[1]
    rhs_stays = k * 2 * 1024 > MM_OPERAND_BYTES
    if rhs_stays:
        tn = _tile(n, tuple(p for p in (512, 256, 128) if p * k * 2 <= MM_RESIDENT_BYTES))
        tm = _tile(m, tuple(p for p in (512, 384, 256, 128) if p * k * 2 <= MM_OPERAND_BYTES))
        grid = (n // tn, m // tm)
        ij = lambda o, i: (i, o)
    elif k * n * b.dtype.itemsize <= MM_OPERAND_BYTES:
        tn = n
        tm = _tile(m, tuple(p for p in (1408, 1024, 512, 384, 256, 128)
                            if p * k * 2 <= MM_OPERAND_BYTES and p * n * 4 <= MM_OPERAND_BYTES * 2 // 3))
        grid = (m // tm, 1)
        ij = lambda o, i: (o, i)
    else:
        tm = _tile(m, tuple(p for p in (1408, 1024, 512, 384, 256, 128) if p * k * 2 <= MM_OPERAND_BYTES))
        tn = _tile(n, tuple(p for p in (1024, 512, 256, 128) if p * k * 2 <= MM_OPERAND_BYTES // 2))
        grid = (m // tm, n // tn)
        ij = lambda o, i: (o, i)

    extra = resid if resid is not None else relu_grad_of
    staged = ta or norm_w8 is not None
    assert not (ta and norm_w8 is not None) and not (staged and rhs_stays)
    n_in = 2 + (extra is not None) + (norm_w8 is not None)
    n_out = 1 + relu2_out + (norm_w8 is not None)

    def body(*refs):
        ins, outs, scr = refs[:n_in], refs[n_in:n_in + n_out], refs[n_in + n_out:]
        a_ref, b_ref = ins[:2]
        if staged:
            @pl.when(pl.program_id(1) == 0)
            def _():
                if ta:
                    scr[0][...] = a_ref[...].T
                else:
                    hn = _rmsnorm(a_ref[...], ins[-1][0:1, :]).astype(bf16)
                    scr[0][...] = hn
                    outs[-1][...] = hn

            lhs = scr[0][...]
        else:
            lhs = a_ref[...]
        o = _dot(lhs, b_ref[...], NT if tb else NN)
        if resid is not None:
            o = o + ins[2][...]
        if relu_grad_of is not None:
            o = o * (2.0 * jnp.maximum(ins[2][...], 0.0))
        outs[0][...] = o.astype(out_dtype)
        if relu2_out:
            r = jnp.maximum(o, 0.0)
            outs[1][...] = (r * r).astype(bf16)

    in_specs = [pl.BlockSpec((k, tm), lambda o, i: (0, ij(o, i)[0])) if ta
                else pl.BlockSpec((tm, k), lambda o, i: (ij(o, i)[0], 0)),
                pl.BlockSpec((tn, k), lambda o, i: (ij(o, i)[1], 0)) if tb
                else pl.BlockSpec((k, tn), lambda o, i: (0, ij(o, i)[1]))]
    args = [a, b]
    if extra is not None:
        in_specs.append(pl.BlockSpec((tm, tn), ij))
        args.append(extra)
    out_blk = pl.BlockSpec((tm, tn), ij)
    out_specs = [out_blk] * (1 + relu2_out)
    out_shape = [jax.ShapeDtypeStruct((m, n), out_dtype)] + [jax.ShapeDtypeStruct((m, n), bf16)] * relu2_out
    if norm_w8 is not None:
        in_specs.append(pl.BlockSpec((8, k), lambda o, i: (0, 0)))
        args.append(norm_w8)
        out_specs.append(pl.BlockSpec((tm, k), lambda o, i: (ij(o, i)[0], 0)))
        out_shape.append(jax.ShapeDtypeStruct((m, k), bf16))
    res = pl.pallas_call(
        body, name=name, grid=grid, in_specs=in_specs, out_specs=tuple(out_specs), out_shape=tuple(out_shape),
        scratch_shapes=[pltpu.VMEM((tm, k), bf16)] if staged else [],
        compiler_params=_params(("parallel", "arbitrary" if staged else "parallel")),
    )(*args)
    return res[0] if len(res) == 1 else res


def _rows(t_rows):
    return _tile(t_rows, (384, 256, 128))


def _rmsnorm(h, w):
    return h * lax.rsqrt(jnp.mean(h * h, axis=1, keepdims=True) + RMS_EPS) * w


def rmsnorm_bwd(h, w8, dhn, dres, *, name):
    t_rows, d = h.shape
    tr = _rows(t_rows)

    def body(h_ref, w_ref, dhn_ref, dres_ref, dh_ref, dw_ref):
        @pl.when(pl.program_id(0) == 0)
        def _():
            dw_ref[...] = jnp.zeros_like(dw_ref)

        _, vjp = jax.vjp(_rmsnorm, h_ref[...], w_ref[0:1, :])
        dh, dw = vjp(dhn_ref[...])
        dh_ref[...] = dh + dres_ref[...]
        dw_ref[0:1, :] += dw

    blk = pl.BlockSpec((tr, d), lambda i: (i, 0))
    wblk = pl.BlockSpec((8, d), lambda i: (0, 0))
    return pl.pallas_call(
        body, name=name, grid=(t_rows // tr,), in_specs=[blk, wblk, blk, blk], out_specs=(blk, wblk),
        out_shape=(jax.ShapeDtypeStruct((t_rows, d), f32), jax.ShapeDtypeStruct((8, d), f32)),
        compiler_params=_params(("arbitrary",)),
    )(h, w8, dhn, dres)


def _merge(pg, ps, pw, la, lb, lc):
    return _sigmoid(la) * pg + _sigmoid(lb) * ps + _sigmoid(lc) * pw


def _merge_specs(t_rows):
    tr = _rows(t_rows)
    blk = pl.BlockSpec((tr, D_MODEL), lambda i: (i, 0))
    gate = [pl.BlockSpec((tr, D_MODEL), functools.partial(lambda i, j: (i, j), j=C_GATE // D_MODEL + j)) for j in range(3)]
    return tr, blk, gate


def merge_fwd(pg, ps, pw, u):
    t_rows = pg.shape[0]
    tr, blk, gate = _merge_specs(t_rows)

    def body(pg_ref, ps_ref, pw_ref, la, lb, lc, o_ref):
        o_ref[...] = _merge(pg_ref[...], ps_ref[...], pw_ref[...], la[...], lb[...], lc[...]).astype(bf16)

    return pl.pallas_call(
        body, name="merge_fwd", grid=(t_rows // tr,), in_specs=[blk, blk, blk] + gate, out_specs=blk,
        out_shape=jax.ShapeDtypeStruct((t_rows, D_MODEL), bf16), compiler_params=_params(("arbitrary",)),
    )(pg, ps, pw, u, u, u)


def merge_bwd(pg, ps, pw, u, dmerged, du):
    t_rows = pg.shape[0]
    tr, blk, gate = _merge_specs(t_rows)

    def body(pg_ref, ps_ref, pw_ref, la, lb, lc, dm_ref, _, dpg_ref, dps_ref, dpw_ref, dl_ref):
        _, vjp = jax.vjp(_merge, pg_ref[...], ps_ref[...], pw_ref[...], la[...], lb[...], lc[...])
        dpg, dps, dpw, dla, dlb, dlc = vjp(dm_ref[...])
        dpg_ref[...] = dpg.astype(bf16)
        dps_ref[...] = dps.astype(bf16)
        dpw_ref[...] = dpw.astype(bf16)
        for j, dl in enumerate((dla, dlb, dlc)):
            dl_ref[:, j * D_MODEL:(j + 1) * D_MODEL] = dl.astype(bf16)

    act = jax.ShapeDtypeStruct((t_rows, D_MODEL), bf16)
    return pl.pallas_call(
        body, name="merge_bwd", grid=(t_rows // tr,), in_specs=[blk, blk, blk] + gate + [blk, ANY],
        out_specs=(blk, blk, blk, pl.BlockSpec((tr, 3 * D_MODEL), lambda i: (i, C_GATE // (3 * D_MODEL)))),
        out_shape=(act, act, act, jax.ShapeDtypeStruct(du.shape, du.dtype)),
        input_output_aliases={7: 3},
        compiler_params=_params(("arbitrary",)),
    )(pg, ps, pw, u, u, u, dmerged, du)


def loss_head(h, w8, target):
    t_rows, d = h.shape
    tr = HEAD_ROWS

    def loss_fn(hb, w, tgt):
        err = _rmsnorm(hb, w) - tgt
        return 0.5 * jnp.sum(err * err) / d

    def body(h_ref, w_ref, t_ref, loss_ref, dh_ref, dw_ref):
        i = pl.program_id(0)

        @pl.when(i == 0)
        def _():
            loss_ref[...] = jnp.zeros_like(loss_ref)
            dw_ref[...] = jnp.zeros_like(dw_ref)
            dh_ref[...] = jnp.zeros_like(dh_ref)

        @pl.when(i > 0)
        def _():
            val, (dh, dw) = jax.value_and_grad(loss_fn, argnums=(0, 1))(h_ref[...], w_ref[0:1, :], t_ref[...])
            loss_ref[...] += val
            dh_ref[...] = dh
            dw_ref[0:1, :] += dw

    blk = pl.BlockSpec((tr, d), lambda i: (i, 0))
    wblk = pl.BlockSpec((8, d), lambda i: (0, 0))
    return pl.pallas_call(
        body, name="loss_head", grid=(t_rows // tr,),
        in_specs=[blk, wblk, pl.BlockSpec((tr, d), lambda i: (jnp.maximum(i - 1, 0), 0))],
        out_specs=(pl.BlockSpec((8, 128), lambda i: (0, 0)), blk, wblk),
        out_shape=(jax.ShapeDtypeStruct((8, 128), f32), jax.ShapeDtypeStruct((t_rows, d), f32),
                   jax.ShapeDtypeStruct((8, d), f32)),
        compiler_params=_params(("arbitrary",)),
    )(h, w8, target)


def adamw(w, m, v, partials, row_off, *, name):
    rows, d = w.shape
    layers = len(partials)
    per = rows // layers
    tr = _row_tile(per, d)
    assert row_off % tr == 0
    off, nblk = row_off // tr, per // tr
    c1 = 1.0 - ADAM_B1 ** ADAM_STEP
    c2 = 1.0 - ADAM_B2 ** ADAM_STEP

    def body(w_ref, m_ref, v_ref, *refs):
        p_refs, (g_ref, d_ref, mo_ref, vo_ref) = refs[:2 * layers], refs[2 * layers:]
        pair = lambda l: p_refs[2 * l][...].astype(f32) + p_refs[2 * l + 1][...].astype(f32)
        g = pair(0)
        for l in range(1, layers):
            g = jnp.where(pl.program_id(0) >= l * nblk, pair(l), g)
        m_new = ADAM_B1 * m_ref[...] + (1.0 - ADAM_B1) * g
        v_new = ADAM_B2 * v_ref[...] + (1.0 - ADAM_B2) * (g * g)
        g_ref[...] = g
        d_ref[...] = -ADAM_LR * ((m_new / c1) / (jnp.sqrt(v_new / c2) + ADAM_EPS) + ADAM_WD * w_ref[...])
        mo_ref[...] = m_new
        vo_ref[...] = v_new

    blk = pl.BlockSpec((tr, d), lambda i: (i, 0))
    pblks = [pl.BlockSpec((tr, d), functools.partial(lambda i, l: (off + jnp.clip(i - l * nblk, 0, nblk - 1), 0), l=l))
             for l in range(layers) for _ in range(2)]
    out = jax.ShapeDtypeStruct((rows, d), f32)
    return pl.pallas_call(
        body, name=name, grid=(rows // tr,), in_specs=[blk, blk, blk] + pblks, out_specs=(blk,) * 4,
        out_shape=(out,) * 4, compiler_params=_params(("arbitrary",)),
    )(w, m, v, *[p for pair in partials for p in pair])


def reduce4(parts, *, name, own=None, me=None, out_dtype=f32):
    _, rows, d = parts.shape
    tr = _row_tile(rows, d)

    def body(*refs):
        p_ref, o_ref = refs[0], refs[-1]
        acc = None
        for s in range(4):
            term = p_ref[s].astype(f32)
            if own is not None:
                term = jnp.where(refs[2][0] == s, refs[1][...].astype(f32), term)
            acc = term if acc is None else acc + term
        o_ref[...] = acc.astype(out_dtype)

    in_specs = [pl.BlockSpec((4, tr, d), lambda i: (0, i, 0))]
    args = [parts]
    if own is not None:
        in_specs += [pl.BlockSpec((tr, d), lambda i: (i, 0)), pl.BlockSpec(memory_space=pltpu.SMEM)]
        args += [own, me]
    return pl.pallas_call(
        body, name=name, grid=(rows // tr,), in_specs=in_specs,
        out_specs=pl.BlockSpec((tr, d), lambda i: (i, 0)), out_shape=jax.ShapeDtypeStruct((rows, d), out_dtype),
        compiler_params=_params(("arbitrary",)),
    )(*args)


ANY = pl.BlockSpec(memory_space=pl.ANY)
MESH = pl.DeviceIdType.MESH
CHIP_FLIPS = ((0, 1), (1, 0), (1, 1))


def chip_exchange(bufs, scatter, *, name, after=None):
    nb = len(bufs)
    extra = [] if after is None else [after]

    def body(*refs):
        ins, outs = refs[:nb], refs[nb + len(extra):2 * nb + len(extra)]
        send_sems, recv_sems, local_sems = refs[2 * nb + len(extra):]
        x, y, c = lax.axis_index("x"), lax.axis_index("y"), lax.axis_index("c")
        me = 2 * x + y
        local = [pltpu.make_async_copy(ins[j].at[me] if scatter[j] else ins[j], outs[j].at[me], local_sems.at[j])
                 for j in range(nb)]
        for cp in local:
            cp.start()
        sends, recvs = [], []
        for k, (fx, fy) in enumerate(CHIP_FLIPS):
            px = 1 - x if fx else x
            py = 1 - y if fy else y
            chip = 2 * px + py
            for j in range(nb):
                src = ins[j].at[chip] if scatter[j] else ins[j]
                sems = dict(send_sem=send_sems.at[nb * k + j], recv_sem=recv_sems.at[nb * k + j],
                            device_id=(px, py, c), device_id_type=MESH)
                sends.append(pltpu.make_async_remote_copy(src_ref=src, dst_ref=outs[j].at[me], **sems))
                recvs.append(pltpu.make_async_remote_copy(src_ref=src, dst_ref=outs[j].at[chip], **sems))
        for cp in sends:
            cp.start()
        for cp in recvs:
            cp.wait_recv()
        for cp in sends:
            cp.wait_send()
        for cp in local:
            cp.wait()

    out_shape = tuple(jax.ShapeDtypeStruct(b.shape if s else (4,) + b.shape, b.dtype) for b, s in zip(bufs, scatter))
    return pl.pallas_call(
        body, name=name, in_specs=[ANY] * (nb + len(extra)), out_specs=(ANY,) * nb, out_shape=out_shape,
        scratch_shapes=[pltpu.SemaphoreType.DMA((3 * nb,)), pltpu.SemaphoreType.DMA((3 * nb,)),
                        pltpu.SemaphoreType.DMA((nb,))],
        compiler_params=pltpu.CompilerParams(has_side_effects=True),
    )(*bufs, *extra)


def gather_two_level(big, small, *, name):
    half = big.shape[1] // 2

    def body(big_ref, small_ref, obig_ref, osmall_ref, send_sems, recv_sems, local_sems):
        x, y, c = lax.axis_index("x"), lax.axis_index("y"), lax.axis_index("c")
        me = 2 * x + y
        mine = (slice(None), pl.ds(pl.multiple_of(c * half, half), half))
        theirs = (slice(None), pl.ds(pl.multiple_of((1 - c) * half, half), half))
        local = [pltpu.make_async_copy(big_ref, obig_ref.at[me], local_sems.at[0]),
                 pltpu.make_async_copy(small_ref, osmall_ref.at[me], local_sems.at[1])]
        for cp in local:
            cp.start()

        def copy(k, src, dst, to):
            return pltpu.make_async_remote_copy(src_ref=src, dst_ref=dst, send_sem=send_sems.at[k],
                                                recv_sem=recv_sems.at[k], device_id=to, device_id_type=MESH)

        sends, landed, passed, small_in = [], [], [], []
        for k, (fx, fy) in enumerate(CHIP_FLIPS):
            px = 1 - x if fx else x
            py = 1 - y if fy else y
            chip = 2 * px + py
            sends.append(copy(k, big_ref.at[mine], obig_ref.at[(me,) + mine], (px, py, c)))
            landed.append(copy(k, big_ref.at[mine], obig_ref.at[(chip,) + mine], (px, py, c)))
            sends.append(copy(3 + k, small_ref, osmall_ref.at[me], (px, py, c)))
            small_in.append(copy(3 + k, small_ref, osmall_ref.at[chip], (px, py, c)))
            passed.append((copy(6 + k, obig_ref.at[(chip,) + mine], obig_ref.at[(chip,) + mine], (x, y, 1 - c)),
                           copy(6 + k, obig_ref.at[(chip,) + theirs], obig_ref.at[(chip,) + theirs], (x, y, 1 - c))))
        for cp in sends:
            cp.start()
        for k in range(3):
            landed[k].wait_recv()
            passed[k][0].start()
        for k in range(3):
            passed[k][1].wait_recv()
            small_in[k].wait_recv()
        for cp in sends + [p[0] for p in passed]:
            cp.wait_send()
        for cp in local:
            cp.wait()

    return pl.pallas_call(
        body, name=name, in_specs=[ANY, ANY], out_specs=(ANY, ANY),
        out_shape=(jax.ShapeDtypeStruct((4,) + big.shape, big.dtype),
                   jax.ShapeDtypeStruct((4,) + small.shape, small.dtype)),
        scratch_shapes=[pltpu.SemaphoreType.DMA((9,)), pltpu.SemaphoreType.DMA((9,)), pltpu.SemaphoreType.DMA((2,))],
        compiler_params=pltpu.CompilerParams(has_side_effects=True),
    )(big, small)


def sibling_swap(bufs, *, name):
    nb = len(bufs)

    def body(*refs):
        ins, outs, (send_sems, recv_sems) = refs[:nb], refs[nb:2 * nb], refs[2 * nb:]
        peer = (lax.axis_index("x"), lax.axis_index("y"), 1 - lax.axis_index("c"))
        copies = [pltpu.make_async_remote_copy(src_ref=ins[j], dst_ref=outs[j], send_sem=send_sems.at[j],
                                               recv_sem=recv_sems.at[j], device_id=peer, device_id_type=MESH)
                  for j in range(nb)]
        for cp in copies:
            cp.start()
        for cp in copies:
            cp.wait_recv()
        for cp in copies:
            cp.wait_send()

    return pl.pallas_call(
        body, name=name, in_specs=[ANY] * nb, out_specs=(ANY,) * nb,
        out_shape=tuple(jax.ShapeDtypeStruct(b.shape, b.dtype) for b in bufs),
        scratch_shapes=[pltpu.SemaphoreType.DMA((nb,)), pltpu.SemaphoreType.DMA((nb,))],
        compiler_params=pltpu.CompilerParams(has_side_effects=True),
    )(*bufs)


HBM = pl.BlockSpec(memory_space=pltpu.HBM)
SEM = pl.BlockSpec(memory_space=pltpu.SEMAPHORE)
DATAFLOW = pltpu.SideEffectType.DATAFLOW_SIDE_EFFECTING


def _exchange_copies(srcs, lands, send_sems, recv_sems, scatter):
    x, y, c = lax.axis_index("x"), lax.axis_index("y"), lax.axis_index("c")
    me = 2 * x + y
    nb = len(srcs)
    pairs = []
    for k, (fx, fy) in enumerate(CHIP_FLIPS):
        px = 1 - x if fx else x
        py = 1 - y if fy else y
        chip = 2 * px + py
        for j in range(nb):
            src = srcs[j].at[chip] if scatter[j] else srcs[j]
            sems = dict(send_sem=send_sems.at[nb * k + j], recv_sem=recv_sems.at[nb * k + j],
                        device_id=(px, py, c), device_id_type=MESH)
            pairs.append((pltpu.make_async_remote_copy(src_ref=src, dst_ref=lands[j].at[me], **sems),
                          pltpu.make_async_remote_copy(src_ref=src, dst_ref=lands[j].at[chip], **sems)))
    return pairs


def exchange_start(bufs, scatter, after, *, name):
    nb = len(bufs)
    slabs = [b.shape[1:] if s else b.shape for b, s in zip(bufs, scatter)]
    lands = [lax.empty((4,) + shp, b.dtype) for b, shp in zip(bufs, slabs)]

    def body(*refs):
        srcs, zones = refs[:nb], refs[nb:2 * nb]
        send_sems, recv_sems = refs[2 * nb + 1:2 * nb + 3]
        token = refs[-1]
        for send, _ in _exchange_copies(srcs, zones, send_sems, recv_sems, scatter):
            send.start()
        token[...] = jnp.zeros_like(token)

    hbm = lambda a: pltpu.with_memory_space_constraint(a, pltpu.HBM)
    out = pl.pallas_call(
        body, name=name, in_specs=[HBM] * (2 * nb) + [ANY],
        out_specs=(SEM, SEM) + (HBM,) * (2 * nb) + (pl.BlockSpec(memory_space=pltpu.VMEM),),
        out_shape=(pltpu.SemaphoreType.DMA((3 * nb,)), pltpu.SemaphoreType.DMA((3 * nb,)))
        + tuple(pltpu.HBM(a.shape, a.dtype) for a in list(bufs) + lands) + (jax.ShapeDtypeStruct((8, 128), f32),),
        input_output_aliases={i: 2 + i for i in range(2 * nb)},
        compiler_params=pltpu.CompilerParams(has_side_effects=DATAFLOW),
    )(*[hbm(a) for a in list(bufs) + lands], after)
    return (out[:2], out[2:2 + nb], out[2 + nb:2 + 2 * nb], scatter), out[-1]


def exchange_wait(state, after, *, name):
    (send_sems, recv_sems), srcs, lands, scatter = state
    nb = len(srcs)

    def body(*refs):
        src_refs, zones = refs[:nb], refs[nb:2 * nb]
        s_sems, r_sems = refs[2 * nb:2 * nb + 2]
        for send, recv in _exchange_copies(src_refs, zones, s_sems, r_sems, scatter):
            send.wait_send()
            recv.wait_recv()

    out = pl.pallas_call(
        body, name=name, in_specs=[HBM] * (2 * nb) + [SEM, SEM, ANY], out_specs=(HBM,) * (2 * nb),
        out_shape=tuple(pltpu.HBM(a.shape, a.dtype) for a in list(srcs) + list(lands)),
        input_output_aliases={i: i for i in range(2 * nb)},
        compiler_params=pltpu.CompilerParams(has_side_effects=DATAFLOW),
    )(*srcs, *lands, send_sems, recv_sems, after)
    return out[nb:]


BIG = (
    ("w_proj_gdn", 256), ("w_proj_ssd", 256), ("w_proj_swa", 256), ("w_out", 256), ("w_up", 1024), ("w_down", 1024))
BIG_OFF = {}
_o = 0
for _n, _r in BIG:
    BIG_OFF[_n] = _o
    _o += _r
BIG_ROWS = _o
W_IN_SHARD = IN_W // 4

W_NAMES = ('meta_tokens', 'norm1_w', 'w_in', 'gdn_conv_w', 'gdn_a_log', 'gdn_dt_bias', 'gdn_norm_w', 'ssd_conv_w',
           'ssd_conv_b', 'ssd_dt_bias', 'ssd_a_log', 'ssd_d', 'ssd_norm_w', 'swa_sinks', 'w_proj_gdn', 'w_proj_ssd',
           'w_proj_swa', 'w_out', 'norm2_w', 'w_up', 'w_down', 'final_norm_w')
SMALL_NAMES = tuple(n for n in W_NAMES if n not in BIG_OFF and n != "w_in")
SMALL_SHARDED = ("meta_tokens", "gdn_conv_w", "ssd_conv_w")


def _pack_rows(parts, dtype):
    flat = jnp.concatenate([p.reshape(-1).astype(dtype) for p in parts])
    n = -(-flat.shape[0] // 8192) * 8192
    return jnp.pad(flat, (0, n - flat.shape[0])).reshape(-1, D_MODEL)


def _unpack_rows(packed, shapes):
    flat, out, o = packed.reshape(-1), [], 0
    for s in shapes:
        n = 1
        for d in s:
            n *= d
        out.append(flat[o:o + n].reshape(s))
        o += n
    return out


def _split_chips(full, axis):
    s = full.shape
    a = full.reshape(s[:axis] + (4, s[axis] // 4) + s[axis + 1:])
    return jnp.moveaxis(a, axis, 0)


def _join_chips(parts, axis):
    a = jnp.moveaxis(parts, 0, axis)
    s = a.shape
    return a.reshape(s[:axis] + (s[axis] * s[axis + 1],) + s[axis + 2:])


BIG_AXIS = {"w_proj_gdn": 1, "w_proj_ssd": 1, "w_proj_swa": 1, "w_out": 1, "w_up": 2, "w_down": 1}


def _w_in_to_padded(w):
    z = lambda n: jnp.zeros((n,) + w.shape[1:], w.dtype)
    return jnp.concatenate([w[8736:11808], w[4112:7184], w[7200:8736], w[4096:4112], z(112),
                            w[7184:7200], z(112 + C_MID_END - C_SDT - 128), w[0:4096]], axis=0)


def _w_in_from_padded(p):
    return jnp.concatenate([p[C_GQ:IN_WP], p[C_BA:C_BA + 16], p[C_SZ:C_WQ], p[C_SDT:C_SDT + 16], p[C_WQ:C_BA],
                            p[0:C_SZ]], axis=0)


def _row8(v, lane0=0, width=128):
    return jnp.pad(v[None, :], ((0, 7), (lane0, width - lane0 - v.shape[0])))


def _layer_fwd(h, p, l, late=None):
    tag = f"l{l}"
    u, hn = mm(h, p["w_in_t"], tb=True, out_dtype=f32, norm_w8=p["n1"], name=f"mm_in_{tag}")
    yg, stg, tg = gdn_fwd(u, p["gcw"], p["galog"], p["gdtb"], p["gnw"])
    ys, sts = ssd_fwd(u, p["scw"], p["sdtb"], p["salog"], p["sd"], p["snw"])
    yw = swa_fwd(u, p["sink"])
    if late is not None:
        p.update(late(yw))
    pg = mm(yg, p["wpg"], out_dtype=f32, name=f"mm_pg_{tag}")
    ps = mm(ys, p["wps"], out_dtype=f32, name=f"mm_ps_{tag}")
    pw = mm(yw, p["wpw"], out_dtype=f32, name=f"mm_pw_{tag}")
    merged = merge_fwd(pg, ps, pw, u)
    h2 = mm(merged, p["wout"], out_dtype=f32, resid=h, name=f"mm_out_{tag}")
    a, r, hn2 = mm(h2, p["wup"], out_dtype=f32, relu2_out=True, norm_w8=p["n2"], name=f"mm_up_{tag}")
    h3 = mm(r, p["wdown"], out_dtype=f32, resid=h2, name=f"mm_down_{tag}")
    saved = dict(h=h, hn=hn, u=u, yg=yg, stg=stg, tg=tg, ys=ys, sts=sts, yw=yw, pg=pg, ps=ps, pw=pw,
                 merged=merged, h2=h2, hn2=hn2, a=a, r=r)
    return h3, saved


def _layer_bwd(dh3, p, s, l, send_big, send_w_in):
    tag = f"l{l}"
    g = {}

    def wgrad(act, d, name):
        return mm(act, d, ta=True, out_dtype=bf16, name=f"wg_{name}_{tag}")

    da = mm(dh3, p["wdown"], tb=True, out_dtype=bf16, relu_grad_of=s["a"], name=f"dg_down_{tag}")
    g["w_down"] = wgrad(s["r"], dh3, "down")
    dhn2 = mm(da, p["wup"], tb=True, out_dtype=f32, name=f"dg_up_{tag}")
    g["w_up"] = wgrad(s["hn2"], da, "up")
    dh2, g["norm2_w"] = rmsnorm_bwd(s["h2"], p["n2"], dhn2, dh3, name=f"norm2_bwd_{tag}")
    dmerged = mm(dh2, p["wout"], tb=True, out_dtype=f32, name=f"dg_out_{tag}")
    g["w_out"] = wgrad(s["merged"], dh2, "out")
    du = lax.empty((dh3.shape[0], IN_WP), bf16)
    dpg, dps, dpw, du = merge_bwd(s["pg"], s["ps"], s["pw"], s["u"], dmerged, du)
    dyg = mm(dpg, p["wpg"], tb=True, out_dtype=f32, name=f"dg_pg_{tag}")
    dys = mm(dps, p["wps"], tb=True, out_dtype=f32, name=f"dg_ps_{tag}")
    dyw = mm(dpw, p["wpw"], tb=True, out_dtype=f32, name=f"dg_pw_{tag}")
    g["w_proj_gdn"] = wgrad(s["yg"], dpg, "pg")
    g["w_proj_ssd"] = wgrad(s["ys"], dps, "ps")
    g["w_proj_swa"] = wgrad(s["yw"], dpw, "pw")
    sent = send_big(jnp.concatenate([_split_chips(g.pop(n), BIG_AXIS[n] - 1).reshape(4, r, D_MODEL)
                                     for n, r in BIG], axis=1))

    (du, dba, dtq, dtk, dtv, g["gdn_a_log"], g["gdn_dt_bias"], g["gdn_norm_w"]) = gdn_bwd(
        s["u"], p["gcw"] + sent, p["galog"], p["gdtb"], p["gnw"], s["stg"], s["tg"], dyg, du)
    g["gdn_conv_w"] = jnp.concatenate([dtq, dtk, dtv], axis=1)[:4]
    (du, ddt, dtx, dtb, dtc, g["ssd_dt_bias"], g["ssd_a_log"], g["ssd_d"], g["ssd_norm_w"]) = ssd_bwd(
        s["u"], p["scw"], p["sdtb"], p["salog"], p["sd"], p["snw"], s["sts"], dys, du)
    dconv = jnp.concatenate([dtx, dtb, dtc], axis=1)
    g["ssd_conv_w"], g["ssd_conv_b"] = dconv[:4], dconv[4]
    du, g["swa_sinks"] = swa_bwd(s["u"], p["sink"], dyw, du)
    mid = jnp.concatenate([dba[0].astype(bf16), ddt.astype(bf16),
                           jnp.zeros((du.shape[0], C_MID_END - C_SDT - 128), bf16)], axis=1)
    du = lax.dynamic_update_slice(du, mid, (0, C_BA))
    sent = send_w_in(_w_in_from_padded(wgrad(du, s["hn"], "in")).reshape(4, W_IN_SHARD, D_MODEL))
    dhn = mm(du, p["w_in_t"], out_dtype=f32, name=f"dg_in_{tag}")
    dh, g["norm1_w"] = rmsnorm_bwd(s["h"], p["n1"] + sent, dhn, dh2, name=f"norm1_bwd_{tag}")
    return dh, g


def kernel(x, meta_tokens, norm1_w, w_in, gdn_conv_w, gdn_a_log, gdn_dt_bias, gdn_norm_w, ssd_conv_w, ssd_conv_b, ssd_dt_bias, ssd_a_log, ssd_d, ssd_norm_w, swa_sinks, w_proj_gdn, w_proj_ssd, w_proj_swa, w_out, norm2_w, w_up, w_down, final_norm_w, loss_target, m_meta_tokens, m_norm1_w, m_w_in, m_gdn_conv_w, m_gdn_a_log, m_gdn_dt_bias, m_gdn_norm_w, m_ssd_conv_w, m_ssd_conv_b, m_ssd_dt_bias, m_ssd_a_log, m_ssd_d, m_ssd_norm_w, m_swa_sinks, m_w_proj_gdn, m_w_proj_ssd, m_w_proj_swa, m_w_out, m_norm2_w, m_w_up, m_w_down, m_final_norm_w, v_meta_tokens, v_norm1_w, v_w_in, v_gdn_conv_w, v_gdn_a_log, v_gdn_dt_bias, v_gdn_norm_w, v_ssd_conv_w, v_ssd_conv_b, v_ssd_dt_bias, v_ssd_a_log, v_ssd_d, v_ssd_norm_w, v_swa_sinks, v_w_proj_gdn, v_w_proj_ssd, v_w_proj_swa, v_w_out, v_norm2_w, v_w_up, v_w_down, v_final_norm_w):
    given = dict(locals())
    depth = norm1_w.shape[0]
    me = 2 * lax.axis_index("x") + lax.axis_index("y")

    me1 = jnp.reshape(me, (1,)).astype(jnp.int32)
    w_in_t = jnp.swapaxes(w_in, 1, 2)

    def weight_slabs(l):
        return (w_in_t[l].astype(bf16),
                jnp.concatenate([given[n][l].reshape(-1, D_MODEL).astype(bf16) for n, _ in BIG]))

    slabs = [weight_slabs(l) for l in range(depth)]
    wsmall = _pack_rows([given[n] for n in SMALL_SHARDED], f32)
    ga0, gsmall = gather_two_level(slabs[0][0], wsmall, name="gather_first")
    gathers, started = {}, jnp.zeros((), f32)
    for l in range(depth):
        for j in range(2):
            if (l, j) != (0, 0):
                gathers[l, j], token = exchange_start([slabs[l][j]], (False,), gsmall, name=f"gather_start_l{l}_{j}")
                started = started + token[0, 0]
    shard_shapes = [given[n].shape for n in SMALL_SHARDED]
    per_chip = [_unpack_rows(gsmall[s], shard_shapes) for s in range(4)]
    full = {n: jnp.concatenate([per_chip[s][i] for s in range(4)], axis=-1) for i, n in enumerate(SMALL_SHARDED)}

    def landed(l, j, after):
        (zone,) = exchange_wait(gathers[l, j], after, name=f"gather_wait_l{l}_{j}")
        return lax.dynamic_update_slice(zone, slabs[l][j][None], (me, 0, 0))

    def first_operands(l, ga, order):
        return dict(
            n1=_row8(norm1_w[l], width=D_MODEL) + order, n2=_row8(norm2_w[l], width=D_MODEL),
            w_in_t=_w_in_to_padded(ga.reshape(IN_W, D_MODEL)),
            gcw=jnp.pad(full["gdn_conv_w"][l], ((0, 4), (0, 0))),
            galog=_row8(gdn_a_log[l], 8), gdtb=_row8(gdn_dt_bias[l], 8), gnw=_row8(gdn_norm_w[l]),
            scw=jnp.pad(jnp.concatenate([full["ssd_conv_w"][l], ssd_conv_b[l][None]], axis=0), ((0, 3), (0, 0))),
            sdtb=_row8(ssd_dt_bias[l]), salog=_row8(ssd_a_log[l]), sd=_row8(ssd_d[l]),
            snw=_row8(ssd_norm_w[l], width=D_MODEL), sink=_row8(swa_sinks[l]))

    def late_operands(l, after):
        gb = landed(l, 1, after)
        w = {}
        for n, r in BIG:
            parts = gb[:, BIG_OFF[n]:BIG_OFF[n] + r].reshape((4,) + given[n].shape[1:])
            w[n] = _join_chips(parts, BIG_AXIS[n] - 1)
        return dict(wpg=w["w_proj_gdn"], wps=w["w_proj_ssd"], wpw=w["w_proj_swa"], wout=w["w_out"],
                    wup=w["w_up"], wdown=w["w_down"])

    h = jnp.concatenate([jnp.zeros((PAD, D_MODEL), f32), full["meta_tokens"], x[0]], axis=0)
    layers, saved = [], []
    for l in range(depth):
        p = first_operands(0, ga0, started) if l == 0 else first_operands(l, landed(l, 0, h), 0.0)
        h, s = _layer_fwd(h, p, l, late=functools.partial(late_operands, l))
        layers.append(p)
        saved.append(s)
    loss8, dh, dfw8 = loss_head(h, _row8(final_norm_w, width=D_MODEL), loss_target[0])
    grads = {"final_norm_w": dfw8[0]}
    per_layer, grad_slabs, scatters = [None] * depth, {}, {}

    def send(l, j, slab):
        grad_slabs[l, j] = slab
        scatters[l, j], token = exchange_start([slab], (True,), loss8, name=f"scatter_start_l{l}_{j}")
        return token[0, 0]

    for l in reversed(range(depth)):
        dh, per_layer[l] = _layer_bwd(dh, layers[l], saved[l], l, functools.partial(send, l, 1),
                                      functools.partial(send, l, 0))
    grad_x = dh[HEAD_ROWS:][None]
    grads["meta_tokens"] = dh[PAD:HEAD_ROWS]
    lane = {"gdn_a_log": (8, 8), "gdn_dt_bias": (8, 8), "gdn_norm_w": (0, 128), "ssd_dt_bias": (0, 16),
            "ssd_a_log": (0, 16), "ssd_d": (0, 16), "swa_sinks": (0, 16)}
    for n in per_layer[0]:
        parts = [per_layer[l][n] for l in range(depth)]
        if n in lane:
            parts = [q[0, lane[n][0]:lane[n][0] + lane[n][1]] for q in parts]
        elif n in ("norm1_w", "norm2_w", "ssd_norm_w"):
            parts = [q[0] for q in parts]
        grads[n] = jnp.stack(parts)
    loss = lax.psum(loss8[0, 0], ("x", "y", "c"))

    gs = _pack_rows([grads[n] for n in SMALL_NAMES], f32)
    def chip_sum(l, j, after):
        (zone,) = exchange_wait(scatters[l, j], after, name=f"scatter_wait_l{l}_{j}")
        own = lax.dynamic_index_in_dim(grad_slabs[l, j], me, 0, keepdims=False)
        return reduce4(zone, own=own, me=me1, out_dtype=bf16, name=f"sum_chips_l{l}_{j}")

    early = [(l, j) for l in range(depth) for j in range(2) if (l, j) != (0, 0)]
    mine = {lj: chip_sum(*lj, dh) for lj in early}
    sibs = dict(zip(early, sibling_swap([mine[lj] for lj in early], name="swap_cores_early")))
    out = {}
    for n, r in BIG:
        shp = given[n].shape
        res = adamw(*[given[pre + n].reshape(depth * r, D_MODEL) for pre in ("", "m_", "v_")],
                    [(mine[l, 1], sibs[l, 1]) for l in range(depth)], BIG_OFF[n], name=f"adamw_{n}")
        out[n] = [a.reshape(shp) for a in res]
    mine[0, 0] = chip_sum(0, 0, res[1])
    (rs,) = chip_exchange([gs], (False,), after=mine[0, 0], name="gather_small_grads")
    ps_ = reduce4(rs, name="sum_chips_small")
    sibs[0, 0], ss = sibling_swap([mine[0, 0], ps_], name="swap_cores_last")
    res = adamw(*[jnp.swapaxes(given[pre + "w_in"], 1, 2).reshape(depth * W_IN_SHARD, D_MODEL)
                  for pre in ("", "m_", "v_")],
                [(mine[l, 0], sibs[l, 0]) for l in range(depth)], 0, name="adamw_w_in")
    out["w_in"] = [jnp.swapaxes(a.reshape(w_in_t.shape), 1, 2) for a in res]
    full_shapes = [grads[n].shape for n in SMALL_NAMES]
    mine_s, sib_s = _unpack_rows(ps_, full_shapes), _unpack_rows(ss, full_shapes)

    def local(parts):
        loc = []
        for n, a in zip(SMALL_NAMES, parts):
            if n in SMALL_SHARDED:
                sz = a.shape[-1] // 4
                a = lax.dynamic_slice_in_dim(a, me * sz, sz, axis=a.ndim - 1)
            loc.append(a)
        return _pack_rows(loc, f32)

    res = adamw(_pack_rows([given[n] for n in SMALL_NAMES], f32), _pack_rows([given["m_" + n] for n in SMALL_NAMES], f32),
                _pack_rows([given["v_" + n] for n in SMALL_NAMES], f32), [(local(mine_s), local(sib_s))], 0,
                name="adamw_small")
    local_shapes = [given[n].shape for n in SMALL_NAMES]
    unpacked = [_unpack_rows(a, local_shapes) for a in res]
    for i, n in enumerate(SMALL_NAMES):
        out[n] = [unpacked[j][i] for j in range(4)]

    return (loss, grad_x) + tuple(out[n][j] for j in range(4) for n in W_NAMES)
```

```python
import functools

import jax
import jax.numpy as jnp
from jax import lax
from jax.experimental import pallas as pl
from jax.experimental.pallas import tpu as pltpu

f32 = jnp.float32
bf16 = jnp.bfloat16

D_MODEL = 1024
N_META = 16
PAD = 112
HEAD_ROWS = PAD + N_META
RMS_EPS = 1e-6
L2_EPS = 1e-6
D_FF = 4 * D_MODEL

GDN_HEADS = 8
GDN_D = 128
GDN_CHUNK = 64
SSD_HEADS = 16
SSD_P = 64
SSD_GROUPS = 4
SSD_HPG = 4
SSD_N = 128
SSD_CHUNK = 128
SWA_Q_HEADS = 16
SWA_KV_HEADS = 4
SWA_REP = 4
SWA_D = 64
SWA_W = 128

C_GATE = 0
C_SZ, C_SX, C_SB, C_SC = 3072, 4096, 5120, 5632
C_WQ, C_WK, C_WV = 6144, 7168, 7424
C_BA = 7680
C_SDT = 7808
C_MID_END = 8192
C_GQ, C_GK, C_GV, C_GG = 8192, 9216, 10240, 11264
IN_WP = 12288
IN_W = 11808

ADAM_LR, ADAM_B1, ADAM_B2, ADAM_EPS, ADAM_WD, ADAM_STEP = 0.001, 0.9, 0.999, 1e-08, 0.01, 10

VMEM_LIMIT = 56 * 1024 * 1024
BLOCK_BYTES = 3 << 19
MM_OPERAND_BYTES = 9 << 20
MM_RESIDENT_BYTES = 13 << 20

NN = (((1,), (0,)), ((), ()))
NT = (((1,), (1,)), ((), ()))
TN = (((0,), (0,)), ((), ()))


def _dot(a, b, dims=NN):
    return lax.dot_general(a.astype(bf16), b.astype(bf16), dims, preferred_element_type=f32)


def _dotx(a, b, dims=NN):
    return lax.dot_general(a, b, dims, preferred_element_type=f32, precision=lax.Precision.HIGH)


def _iota(shape, axis):
    return lax.broadcasted_iota(jnp.int32, shape, axis)


def _softplus(x):
    return jnp.maximum(x, 0.0) + jnp.log1p(jnp.exp(-jnp.abs(x)))


_sigmoid = jax.nn.sigmoid


def _silu(x):
    return x * _sigmoid(x)


def _params(sem):
    return pltpu.CompilerParams(dimension_semantics=sem, vmem_limit_bytes=VMEM_LIMIT)


@functools.partial(jax.custom_vjp, nondiff_argnums=(1,))
def _window(x_ext, off):
    if off == 8:
        return x_ext[8:]
    return pltpu.roll(x_ext, 8 - off, 0)[8:]


def _window_fwd(x_ext, off):
    return _window(x_ext, off), None


def _window_bwd(off, _, g):
    n, w = g.shape
    g_ext = jnp.concatenate([jnp.zeros((8, w), g.dtype), g], axis=0)
    if off == 8:
        return (g_ext,)
    return (pltpu.roll(g_ext, n + off, 0),)


_window.defvjp(_window_fwd, _window_bwd)


def _conv4(x, halo, taps):
    x_ext = jnp.concatenate([halo, x], axis=0)
    y = taps[3] * x
    for j in range(3):
        y = y + taps[j] * _window(x_ext, 5 + j)
    return y


def _blockinv_impl(a):
    n = a.shape[0]
    ri, ci = _iota((n, n), 0), _iota((n, n), 1)
    t = (ri == ci).astype(f32) - jnp.where(((ri >> 1) == (ci >> 1)) & (ri > ci), a, 0.0)
    k = 1
    while (1 << k) < n:
        sel = ((ri >> (k + 1)) == (ci >> (k + 1))) & (((ri >> k) & 1) == 1) & (((ci >> k) & 1) == 0)
        o = jnp.where(sel, a, 0.0)
        t = t - _dotx(_dotx(t, o), t)
        k += 1
    return t


@jax.custom_vjp
def _blockinv(a):
    return _blockinv_impl(a)


def _blockinv_fwd(a):
    t = _blockinv_impl(a)
    return t, t


def _blockinv_bwd(t, dt):
    return (-_dotx(_dotx(t, dt, TN), t, NT),)


_blockinv.defvjp(_blockinv_fwd, _blockinv_bwd)


@jax.custom_vjp
def _blockinv_given(a, t):
    return t


_blockinv_given.defvjp(lambda a, t: (t, t), lambda t, dt: _blockinv_bwd(t, dt) + (jnp.zeros_like(t),))


def _scan_rows(x, reverse):
    n = x.shape[0]
    row = _iota(x.shape, 0)
    s = 1
    while s < n:
        if reverse:
            x = x + jnp.where(row < n - s, pltpu.roll(x, n - s, 0), 0.0)
        else:
            x = x + jnp.where(row >= s, pltpu.roll(x, s, 0), 0.0)
        s *= 2
    return x


@jax.custom_vjp
def _cumsum_rows(x):
    return _scan_rows(x, False)


_cumsum_rows.defvjp(lambda x: (_scan_rows(x, False), None), lambda _, g: (_scan_rows(g, True),))


def _gdn_act(xq, xk, xv, hq, hk, hv, tq, tk, tv):
    return _silu(_conv4(xq, hq, tq)), _silu(_conv4(xk, hk, tk)), _silu(_conv4(xv, hv, tv))


def _gdn_core(q, k, v, gate, mb, mg, mr, s, t_given, beta16, g16, gam16, gam16_t, nw):
    c = GDN_CHUNK
    q = q * lax.rsqrt(jnp.sum(q * q, axis=1, keepdims=True) + L2_EPS) * (GDN_D ** -0.5)
    k = k * lax.rsqrt(jnp.sum(k * k, axis=1, keepdims=True) + L2_EPS)

    pick = lambda x, m: jnp.sum(x * m, axis=1, keepdims=True)
    beta = pick(beta16, mb)
    g = jnp.broadcast_to(pick(g16, mg), (c, GDN_D))
    gam1 = pick(gam16, mg)
    gam = jnp.broadcast_to(gam1, (c, GDN_D))
    gam_j = jnp.broadcast_to(jnp.sum(gam16_t * mr, axis=0, keepdims=True), (c, c))

    ri, ci = _iota((c, c), 0), _iota((c, c), 1)
    incl = ci <= ri
    decay = jnp.where(incl, jnp.exp(jnp.where(incl, jnp.broadcast_to(gam1, (c, c)) - gam_j, 0.0)), 0.0)

    kb = k * beta
    a = jnp.where(ci < ri, _dot(kb, k, NT) * decay, 0.0)
    t = _blockinv(a) if t_given is None else _blockinv_given(a, t_given)
    egam = jnp.exp(gam)
    u = _dotx(t, v * beta)
    w = _dotx(t, kb * egam)
    attn = _dot(q, k, NT) * decay
    gl = jnp.sum(g, axis=0, keepdims=True)
    kt = k * jnp.exp(gl - gam)
    v_new = u - _dot(w, s)
    o = _dot(q * egam, s) + _dot(attn, v_new)
    s_out = s * jnp.exp(gl) + _dot(kt, v_new, TN)

    y = o * lax.rsqrt(jnp.mean(o * o, axis=1, keepdims=True) + RMS_EPS) * nw * _silu(gate)
    return y, s_out, t


def _gdn_chunk(q, k, v, gate, s, t_given, ba, alog, dtb, nw, *, masks, row0):
    valid = (row0 + _iota((GDN_CHUNK, 1), 0)) >= PAD
    beta16 = jnp.where(valid, _sigmoid(ba), 0.0)
    g16 = jnp.where(valid, -jnp.exp(alog) * _softplus(ba + dtb), 0.0)
    gam16 = _cumsum_rows(g16)
    core = jax.vmap(_gdn_core, in_axes=(0,) * 8 + (None if t_given is None else 0,) + (None,) * 5)
    y, s_out, t = core(q, k, v, gate, *masks, s, t_given, beta16, g16, gam16, gam16.T, nw)
    return (y, s_out, t) if t_given is None else (y, s_out)


def _gdn_specs(hb, nc, rev):
    w = hb * GDN_D

    def cidx(c):
        return (nc - 1 - c) if rev else c

    def col(base):
        return pl.BlockSpec((GDN_CHUNK, w), lambda h, c: (cidx(c), base // w + h))

    def halo(base):
        return pl.BlockSpec((8, w), lambda h, c: (jnp.maximum(cidx(c) * (GDN_CHUNK // 8) - 1, 0), base // w + h))

    def taps(base):
        return pl.BlockSpec((8, w), lambda h, c: (0, base // w + h))

    ba = pl.BlockSpec((GDN_CHUNK, 128), lambda h, c: (cidx(c), C_BA // 128))
    row = pl.BlockSpec((8, 128), lambda h, c: (0, 0))
    y = pl.BlockSpec((GDN_CHUNK, w), lambda h, c: (cidx(c), h))
    st = pl.BlockSpec((1, hb, GDN_D, GDN_D), lambda h, c: (cidx(c), h, 0, 0))
    in_specs = [col(C_GQ), col(C_GK), col(C_GV), halo(C_GQ), halo(C_GK), halo(C_GV), col(C_GG), ba,
                taps(0), taps(1024), taps(2048), row, row, row]
    return in_specs, y, st, taps, row, col, ba


def _gdn_load(refs, first):
    xq, xk, xv, hq, hk, hv, gate, ba, tq, tk, tv, alog, dtb, nw = refs

    def halo(r):
        return jnp.where(first, 0.0, r[...])

    def taps(r):
        return tuple(r[j:j + 1, :] for j in range(4))

    act = (xq[...], xk[...], xv[...], halo(hq), halo(hk), halo(hv), taps(tq), taps(tk), taps(tv))
    return act, gate[...], (ba[...], alog[0:1, :], dtb[0:1, :], nw[0:1, :])


def _heads(a, hb):
    return jnp.stack([a[:, i * GDN_D:(i + 1) * GDN_D] for i in range(hb)])


def _wide(a):
    return jnp.concatenate([a[i] for i in range(a.shape[0])], axis=1)


def _head_masks(hblk, hb):
    head = hblk * hb + _iota((hb, 1, 128), 0)
    lane = _iota((hb, 1, 128), 2)
    rows = (_iota((hb, 128, 1), 1) == hblk * hb + _iota((hb, 128, 1), 0) + 8).astype(f32)
    return (lane == head).astype(f32), (lane == head + 8).astype(f32), rows


def gdn_fwd(u, conv_w8, alog8, dtb8, nw8, *, hb=8):
    t_rows = u.shape[0]
    nc = t_rows // GDN_CHUNK
    in_specs, y_spec, st_spec, *_ = _gdn_specs(hb, nc, False)

    def body(*refs):
        ins, (y_ref, st_ref, t_ref), (s_scr,) = refs[:14], refs[14:17], refs[17:]
        hblk, c = pl.program_id(0), pl.program_id(1)

        @pl.when(c == 0)
        def _():
            s_scr[...] = jnp.zeros_like(s_scr)

        act, gate, shared = _gdn_load(ins, c == 0)
        s = s_scr[...]
        st_ref[0] = s
        qa, ka, va = _gdn_act(*act)
        y, s_new, t = _gdn_chunk(_heads(qa, hb), _heads(ka, hb), _heads(va, hb), _heads(gate, hb), s, None, *shared,
                                 masks=_head_masks(hblk, hb), row0=c * GDN_CHUNK)
        y_ref[...] = _wide(y).astype(bf16)
        t_ref[0] = t
        s_scr[...] = s_new

    return pl.pallas_call(
        body, name="gdn_fwd", grid=(GDN_HEADS // hb, nc),
        in_specs=in_specs,
        out_specs=(y_spec, st_spec, pl.BlockSpec((1, hb, GDN_CHUNK, GDN_CHUNK), lambda h, c: (c, h, 0, 0))),
        out_shape=(jax.ShapeDtypeStruct((t_rows, D_MODEL), bf16),
                   jax.ShapeDtypeStruct((nc, GDN_HEADS, GDN_D, GDN_D), f32),
                   jax.ShapeDtypeStruct((nc, GDN_HEADS, GDN_CHUNK, GDN_CHUNK), f32)),
        scratch_shapes=[pltpu.VMEM((hb, GDN_D, GDN_D), f32)],
        compiler_params=_params(("arbitrary", "arbitrary")),
    )(u, u, u, u, u, u, u, u, conv_w8, conv_w8, conv_w8, alog8, dtb8, nw8)


def gdn_bwd(u, conv_w8, alog8, dtb8, nw8, states, tinv, dy, du):
    t_rows = u.shape[0]
    nc = t_rows // GDN_CHUNK
    hb = GDN_HEADS
    w = hb * GDN_D
    in_specs, y_spec, st_spec, taps, row, col, ba = _gdn_specs(hb, nc, True)
    nhb = GDN_HEADS // hb

    def body(*refs):
        ins, st_ref, t_ref, dy_ref = refs[:14], refs[14], refs[15], refs[16]
        du_ref, dba_ref, dtq_ref, dtk_ref, dtv_ref, dalog_ref, ddtb_ref, dnw_ref = refs[18:26]
        ds_scr, dh_scr = refs[26:]
        hblk, cc = pl.program_id(0), pl.program_id(1)
        c = nc - 1 - cc

        @pl.when(cc == 0)
        def _():
            ds_scr[...] = jnp.zeros_like(ds_scr)
            dh_scr[...] = jnp.zeros_like(dh_scr)
            dtq_ref[...] = jnp.zeros_like(dtq_ref)
            dtk_ref[...] = jnp.zeros_like(dtk_ref)
            dtv_ref[...] = jnp.zeros_like(dtv_ref)

        @pl.when((cc == 0) & (hblk == 0))
        def _():
            dalog_ref[...] = jnp.zeros_like(dalog_ref)
            ddtb_ref[...] = jnp.zeros_like(ddtb_ref)
            dnw_ref[...] = jnp.zeros_like(dnw_ref)

        act, gate, shared = _gdn_load(ins, c == 0)
        (qa, ka, va), vjp_act = jax.vjp(_gdn_act, *act)
        chunk = functools.partial(_gdn_chunk, masks=_head_masks(hblk, hb), row0=c * GDN_CHUNK)
        _, vjp_core = jax.vjp(chunk, _heads(qa, hb), _heads(ka, hb), _heads(va, hb), _heads(gate, hb), st_ref[0],
                              t_ref[0], *shared)
        dqa, dka, dva, dgate, ds, _, dba, dalog, ddtb, dnw = vjp_core(
            (_heads(dy_ref[...].astype(f32), hb), ds_scr[...]))
        ds_scr[...] = ds
        dxq, dxk, dxv, dhq, dhk, dhv, dtq, dtk, dtv = vjp_act((_wide(dqa), _wide(dka), _wide(dva)))
        zeros = jnp.zeros((GDN_CHUNK - 8, w), f32)
        for j, (dx, dh) in enumerate(((dxq, dhq), (dxk, dhk), (dxv, dhv))):
            du_ref[:, j * w:(j + 1) * w] = (dx + jnp.concatenate([zeros, dh_scr[j]], axis=0)).astype(bf16)
            dh_scr[j] = dh
        du_ref[:, 3 * w:4 * w] = _wide(dgate).astype(bf16)
        dba_ref[0] = dba
        for dt_ref, dtaps in ((dtq_ref, dtq), (dtk_ref, dtk), (dtv_ref, dtv)):
            for j in range(4):
                dt_ref[j:j + 1, :] += dtaps[j]
        dalog_ref[0:1, :] += dalog
        ddtb_ref[0:1, :] += ddtb
        dnw_ref[0:1, :] += dnw

    out_specs = (pl.BlockSpec((GDN_CHUNK, 4 * w), lambda h, c: (nc - 1 - c, C_GQ // (4 * w))),
                 pl.BlockSpec((1, GDN_CHUNK, 128), lambda h, c: (h, nc - 1 - c, 0)),
                 taps(0), taps(0), taps(0), row, row, row)
    out_shape = (jax.ShapeDtypeStruct(du.shape, du.dtype),
                 jax.ShapeDtypeStruct((nhb, t_rows, 128), f32),
                 jax.ShapeDtypeStruct((8, D_MODEL), f32), jax.ShapeDtypeStruct((8, D_MODEL), f32),
                 jax.ShapeDtypeStruct((8, D_MODEL), f32),
                 jax.ShapeDtypeStruct((8, 128), f32), jax.ShapeDtypeStruct((8, 128), f32), jax.ShapeDtypeStruct((8, 128), f32))
    return pl.pallas_call(
        body, name="gdn_bwd", grid=(nhb, nc),
        in_specs=in_specs + [st_spec, pl.BlockSpec((1, hb, GDN_CHUNK, GDN_CHUNK), lambda h, c: (nc - 1 - c, h, 0, 0)),
                             y_spec, ANY],
        out_specs=out_specs, out_shape=out_shape, input_output_aliases={17: 0},
        scratch_shapes=[pltpu.VMEM((hb, GDN_D, GDN_D), f32), pltpu.VMEM((3, 8, w), f32)],
        compiler_params=_params(("arbitrary", "arbitrary")),
    )(u, u, u, u, u, u, u, u, conv_w8, conv_w8, conv_w8, alog8, dtb8, nw8, states, tinv, dy, du)


def _ssd_act(xs_r, b_r, c_r, hx, hbm, hcm, tx, tb, tc, bx, bb, bc, *, row0):
    valid = (row0 + _iota((SSD_CHUNK, 1), 0)) >= PAD
    act = lambda x, h, t, b: jnp.where(valid, _silu(_conv4(x, h, t) + b), 0.0)
    return act(xs_r, hx, tx, bx), act(b_r, hbm, tb, bb), act(c_r, hcm, tc, bc)


def _ssd_core(xs, bm, cm, z, nw, lanes, rows, h, dtp16, adt16, acum16, acum16_t, dsk):
    n = SSD_CHUNK
    pick = lambda x, m: jnp.sum(x * m, axis=1, keepdims=True)
    lane_r = _iota((1, 256), 1) >> 6

    def per_lane(x16):
        cols = [pick(x16, lanes[r]) for r in range(SSD_HPG)]
        return cols, jnp.concatenate([jnp.broadcast_to(c, (c.shape[0], SSD_P)) for c in cols], axis=1)

    _, dtp = per_lane(dtp16)
    _, adt = per_lane(adt16)
    ccols, acum = per_lane(acum16)
    _, dlane = per_lane(dsk)

    ri, ci = _iota((n, n), 0), _iota((n, n), 1)
    incl = ci <= ri
    al = jnp.sum(adt, axis=0, keepdims=True)
    xdt = xs * dtp
    cb = _dot(cm, bm, NT)
    y = _dot(cm, h) * jnp.exp(acum) + dlane * xs
    for r in range(SSD_HPG):
        ai = jnp.broadcast_to(ccols[r], (n, n))
        aj = jnp.broadcast_to(jnp.sum(acum16_t * rows[r], axis=0, keepdims=True), (n, n))
        lm = jnp.where(incl, jnp.exp(jnp.where(incl, ai - aj, 0.0)), 0.0)
        y = y + _dot(cb * lm, jnp.where(lane_r == r, xdt, 0.0))
    h_out = h * jnp.exp(al) + _dot(bm, jnp.exp(al - acum) * xdt, TN)
    y = y * _silu(z)
    y = y * lax.rsqrt(jnp.mean(y * y, axis=1, keepdims=True) + RMS_EPS) * nw
    return y, h_out


def _ssd_chunk(xs, bm, cm, z, nw, h, dt, dtb, alog, dsk, *, row0):
    valid = (row0 + _iota((SSD_CHUNK, 1), 0)) >= PAD
    dtp16 = jnp.where(valid, _softplus(dt + dtb), 0.0)
    adt16 = -jnp.exp(alog) * dtp16
    acum16 = _cumsum_rows(adt16)
    lanes = tuple((_iota((SSD_GROUPS, 1, 128), 2) == _iota((SSD_GROUPS, 1, 128), 0) * SSD_HPG + r).astype(f32)
                  for r in range(SSD_HPG))
    rows = tuple((_iota((SSD_GROUPS, 128, 1), 1) == _iota((SSD_GROUPS, 128, 1), 0) * SSD_HPG + r).astype(f32)
                 for r in range(SSD_HPG))
    core = jax.vmap(_ssd_core, in_axes=(0,) * 8 + (None,) * 5)
    return core(xs, bm, cm, z, nw, lanes, rows, h, dtp16, adt16, acum16, acum16.T, dsk)


def _ssd_specs(nc, rev):
    n = SSD_CHUNK

    def cidx(c):
        return (nc - 1 - c) if rev else c

    def col(base, w):
        return pl.BlockSpec((n, w), lambda c: (cidx(c), base // w))

    def halo(base, w):
        return pl.BlockSpec((8, w), lambda c: (jnp.maximum(cidx(c) * (n // 8) - 1, 0), base // w))

    def taps(base, w):
        return pl.BlockSpec((8, w), lambda c: (0, base // w))

    row = pl.BlockSpec((8, 128), lambda c: (0, 0))
    in_specs = [col(C_SX, 1024), col(C_SB, 512), col(C_SC, 512), halo(C_SX, 1024), halo(C_SB, 512), halo(C_SC, 512),
                col(C_SZ, 1024), col(C_SDT, 128), taps(0, 1024), taps(1024, 512), taps(1536, 512), row, row, row,
                taps(0, 1024)]
    y = pl.BlockSpec((n, D_MODEL), lambda c: (cidx(c), 0))
    st = pl.BlockSpec((1, SSD_GROUPS, SSD_N, 256), lambda c: (cidx(c), 0, 0, 0))
    return in_specs, y, st, col, taps, row


def _ssd_load(refs, first):
    xs, bm, cm, hx, hbm, hcm, z, dt, tx, tb, tc, dtb, alog, dsk, nw = refs

    def halo(r):
        return jnp.where(first, 0.0, r[...])

    def taps(r):
        return tuple(r[j:j + 1, :] for j in range(4))

    act = (xs[...], bm[...], cm[...], halo(hx), halo(hbm), halo(hcm), taps(tx), taps(tb), taps(tc),
           tx[4:5, :], tb[4:5, :], tc[4:5, :])
    return act, (z[...], nw[0:1, :]), (dt[...], dtb[0:1, :], alog[0:1, :], dsk[0:1, :])


def _groups(a, w):
    return jnp.stack([a[:, i * w:(i + 1) * w] for i in range(SSD_GROUPS)])


def ssd_fwd(u, conv_w8, dtb8, alog8, d8, nw8):
    t_rows = u.shape[0]
    nc = t_rows // SSD_CHUNK
    in_specs, y_spec, st_spec, *_ = _ssd_specs(nc, False)

    def body(*refs):
        ins, (y_ref, st_ref), (h_scr,) = refs[:15], refs[15:17], refs[17:]
        c = pl.program_id(0)

        @pl.when(c == 0)
        def _():
            h_scr[...] = jnp.zeros_like(h_scr)

        act, (z, nw), shared = _ssd_load(ins, c == 0)
        h = h_scr[...]
        st_ref[0] = h
        xs, bm, cm = _ssd_act(*act, row0=c * SSD_CHUNK)
        y, h_new = _ssd_chunk(_groups(xs, 256), _groups(bm, 128), _groups(cm, 128), _groups(z, 256),
                              _groups(nw, 256), h, *shared, row0=c * SSD_CHUNK)
        y_ref[...] = _wide(y).astype(bf16)
        h_scr[...] = h_new

    return pl.pallas_call(
        body, name="ssd_fwd", grid=(nc,), in_specs=in_specs, out_specs=(y_spec, st_spec),
        out_shape=(jax.ShapeDtypeStruct((t_rows, D_MODEL), bf16),
                   jax.ShapeDtypeStruct((nc, SSD_GROUPS, SSD_N, 256), f32)),
        scratch_shapes=[pltpu.VMEM((SSD_GROUPS, SSD_N, 256), f32)],
        compiler_params=_params(("arbitrary",)),
    )(u, u, u, u, u, u, u, u, conv_w8, conv_w8, conv_w8, dtb8, alog8, d8, nw8)


def ssd_bwd(u, conv_w8, dtb8, alog8, d8, nw8, states, dy, du):
    t_rows = u.shape[0]
    nc = t_rows // SSD_CHUNK
    n = SSD_CHUNK
    in_specs, y_spec, st_spec, col, taps, row = _ssd_specs(nc, True)

    def body(*refs):
        ins, st_ref, dy_ref = refs[:15], refs[15], refs[16]
        du_ref, ddt_ref, dtx_ref, dtb_ref, dtc_ref, ddtb_ref, dalog_ref, ddsk_ref, dnw_ref = refs[18:27]
        dh_scr, hx_scr, hb_scr, hc_scr = refs[27:]
        cc = pl.program_id(0)
        c = nc - 1 - cc

        @pl.when(cc == 0)
        def _():
            for r in (dh_scr, hx_scr, hb_scr, hc_scr, dtx_ref, dtb_ref, dtc_ref, dnw_ref, ddtb_ref, dalog_ref, ddsk_ref):
                r[...] = jnp.zeros_like(r)

        act, (z, nw), shared = _ssd_load(ins, c == 0)
        (xs, bm, cm), vjp_act = jax.vjp(functools.partial(_ssd_act, row0=c * n), *act)
        _, vjp_core = jax.vjp(functools.partial(_ssd_chunk, row0=c * n), _groups(xs, 256), _groups(bm, 128),
                              _groups(cm, 128), _groups(z, 256), _groups(nw, 256), st_ref[0], *shared)
        dxa, dba, dca, dz, dnw, dh, ddt, ddtb, dalog, ddsk = vjp_core(
            (_groups(dy_ref[...].astype(f32), 256), dh_scr[...]))
        dh_scr[...] = dh
        dxs, dbm, dcm, dhx, dhb, dhc, dtx, dtb, dtc, dbx, dbb, dbc = vjp_act((_wide(dxa), _wide(dba), _wide(dca)))
        du_ref[:, 0:D_MODEL] = _wide(dz).astype(bf16)
        for dx, dhalo, scr, lo in ((dxs, dhx, hx_scr, C_SX), (dbm, dhb, hb_scr, C_SB), (dcm, dhc, hc_scr, C_SC)):
            zeros = jnp.zeros((n - 8, dx.shape[1]), f32)
            du_ref[:, lo - C_SZ:lo - C_SZ + dx.shape[1]] = (dx + jnp.concatenate([zeros, scr[...]], axis=0)).astype(bf16)
            scr[...] = dhalo
        ddt_ref[...] = ddt
        for ref, dtaps, dbias in ((dtx_ref, dtx, dbx), (dtb_ref, dtb, dbb), (dtc_ref, dtc, dbc)):
            for j in range(4):
                ref[j:j + 1, :] += dtaps[j]
            ref[4:5, :] += dbias
        ddtb_ref[0:1, :] += ddtb
        dalog_ref[0:1, :] += dalog
        ddsk_ref[0:1, :] += ddsk
        dnw_ref[0:1, :] += _wide(dnw)

    def out_col(w):
        return pl.BlockSpec((n, w), lambda c: (nc - 1 - c, 0))

    out_specs = (pl.BlockSpec((n, 3 * D_MODEL), lambda c: (nc - 1 - c, C_SZ // (3 * D_MODEL))), out_col(128),
                 taps(0, D_MODEL), taps(0, 512), taps(0, 512), row, row, row, taps(0, D_MODEL))
    out_shape = (jax.ShapeDtypeStruct(du.shape, du.dtype),
                 jax.ShapeDtypeStruct((t_rows, 128), f32),
                 jax.ShapeDtypeStruct((8, D_MODEL), f32), jax.ShapeDtypeStruct((8, 512), f32),
                 jax.ShapeDtypeStruct((8, 512), f32),
                 jax.ShapeDtypeStruct((8, 128), f32), jax.ShapeDtypeStruct((8, 128), f32),
                 jax.ShapeDtypeStruct((8, 128), f32), jax.ShapeDtypeStruct((8, D_MODEL), f32))
    return pl.pallas_call(
        body, name="ssd_bwd", grid=(nc,), in_specs=in_specs + [st_spec, y_spec, ANY],
        out_specs=out_specs, out_shape=out_shape, input_output_aliases={17: 0},
        scratch_shapes=[pltpu.VMEM((SSD_GROUPS, SSD_N, 256), f32), pltpu.VMEM((8, D_MODEL), f32),
                        pltpu.VMEM((8, 512), f32), pltpu.VMEM((8, 512), f32)],
        compiler_params=_params(("arbitrary",)),
    )(u, u, u, u, u, u, u, u, conv_w8, conv_w8, conv_w8, dtb8, alog8, d8, nw8, states, dy, du)


NEG = -1e30


def _swa_core(q, kc, kp, km, vc, vp, vm, sink, *, n):
    rows = SWA_REP * SWA_W
    ri, ci = _iota((rows, SWA_W), 0) & (SWA_W - 1), _iota((rows, SWA_W), 1)
    causal = ci <= ri
    m_band = (causal & ((n >= 1) | ((ci >= PAD) & (ri >= PAD)))) | ((ci > ri) & (n >= 2))
    m_meta = (n >= 1) & (ci >= PAD)
    q = q * (SWA_D ** -0.5)
    s = jnp.where(m_band, jnp.where(causal, _dot(q, kc, NT), _dot(q, kp, NT)), NEG)
    sm = jnp.where(m_meta, _dot(q, km, NT), NEG)
    mx = jnp.maximum(jnp.maximum(jnp.max(s, axis=1, keepdims=True), jnp.max(sm, axis=1, keepdims=True)), sink)
    mx = lax.stop_gradient(mx)
    e, em = jnp.exp(s - mx), jnp.exp(sm - mx)
    den = jnp.sum(e, axis=1, keepdims=True) + jnp.sum(em, axis=1, keepdims=True) + jnp.exp(sink - mx)
    return (_dot(jnp.where(causal, e, 0.0), vc) + _dot(jnp.where(causal, 0.0, e), vp) + _dot(em, vm)) / den


def _swa_block(q16, kc, kp, km, vc, vp, vm, sink16, *, n):
    rows = SWA_REP * SWA_W
    lane = _iota((1, 128), 1)
    cols = []
    for h in range(SWA_KV_HEADS):
        sinks = [jnp.sum(jnp.where(lane == h * SWA_REP + r, sink16, 0.0), axis=1, keepdims=True) for r in range(SWA_REP)]
        cols.append(jnp.concatenate([jnp.broadcast_to(s, (SWA_W, 1)) for s in sinks], axis=0))
    o = jax.vmap(functools.partial(_swa_core, n=n))(q16.reshape(SWA_KV_HEADS, rows, SWA_D), kc, kp, km, vc, vp, vm,
                                                    jnp.concatenate([col[None] for col in cols], axis=0))
    return o.reshape(q16.shape)


def _swa_specs(nb, rev):
    def bidx(n):
        return (nb - 1 - n) if rev else n

    kvw = SWA_KV_HEADS * SWA_D
    q = pl.BlockSpec((SWA_W, D_MODEL), lambda n: (bidx(n), C_WQ // D_MODEL))

    def kv(base, blk):
        return pl.BlockSpec((SWA_W, kvw), lambda n: (blk(bidx(n)), base // kvw))

    cur, prev, meta = (lambda n: n), (lambda n: jnp.maximum(n - 1, 0)), (lambda n: 0)
    row = pl.BlockSpec((8, 128), lambda n: (0, 0))
    in_specs = [q] + [kv(C_WK, b) for b in (cur, prev, meta)] + [kv(C_WV, b) for b in (cur, prev, meta)] + [row]
    return in_specs, pl.BlockSpec((SWA_W, D_MODEL), lambda n: (bidx(n), 0)), row


def _swa_heads(a):
    return jnp.stack([a[:, i * SWA_D:(i + 1) * SWA_D] for i in range(a.shape[1] // SWA_D)])


def swa_fwd(u, sink8):
    t_rows = u.shape[0]
    nb = t_rows // SWA_W
    in_specs, o_spec, _ = _swa_specs(nb, False)

    def body(q_ref, kc, kp, km, vc, vp, vm, sink_ref, o_ref):
        o = _swa_block(*[_swa_heads(r[...]) for r in (q_ref, kc, kp, km, vc, vp, vm)], sink_ref[0:1, :],
                       n=pl.program_id(0))
        o_ref[...] = _wide(o).astype(bf16)

    return pl.pallas_call(
        body, name="swa_fwd", grid=(nb,), in_specs=in_specs, out_specs=o_spec,
        out_shape=jax.ShapeDtypeStruct((t_rows, D_MODEL), bf16),
        compiler_params=_params(("arbitrary",)),
    )(u, u, u, u, u, u, u, sink8)


def swa_bwd(u, sink8, do, du):
    t_rows = u.shape[0]
    nb = t_rows // SWA_W
    in_specs, o_spec, row = _swa_specs(nb, True)
    width = C_BA - C_WQ

    def body(q_ref, kc, kp, km, vc, vp, vm, sink_ref, do_ref, _, du_ref, dsink_ref,
             dkp_scr, dvp_scr, dkm_scr, dvm_scr):
        nn = pl.program_id(0)
        n = nb - 1 - nn

        @pl.when(nn == 0)
        def _():
            for r in (dkp_scr, dvp_scr, dkm_scr, dvm_scr, dsink_ref):
                r[...] = jnp.zeros_like(r)

        fn = functools.partial(_swa_block, n=n)
        _, vjp = jax.vjp(fn, *[_swa_heads(r[...]) for r in (q_ref, kc, kp, km, vc, vp, vm)], sink_ref[0:1, :])
        dq, dkc, dkp, dkm, dvc, dvp, dvm, dsink = vjp(_swa_heads(do_ref[...]))
        dkm_scr[...] += dkm
        dvm_scr[...] += dvm
        first = n == 0
        dk = dkc + dkp_scr[...] + jnp.where(first, dkm_scr[...], 0.0)
        dv = dvc + dvp_scr[...] + jnp.where(first, dvm_scr[...], 0.0)
        du_ref[:, 0:D_MODEL] = _wide(dq).astype(bf16)
        du_ref[:, C_WK - C_WQ:C_WV - C_WQ] = _wide(dk).astype(bf16)
        du_ref[:, C_WV - C_WQ:width] = _wide(dv).astype(bf16)
        dkp_scr[...] = dkp
        dvp_scr[...] = dvp
        dsink_ref[0:1, :] += dsink

    return pl.pallas_call(
        body, name="swa_bwd", grid=(nb,), in_specs=in_specs + [o_spec, ANY],
        out_specs=(pl.BlockSpec((SWA_W, width), lambda n: (nb - 1 - n, C_WQ // width)), row),
        out_shape=(jax.ShapeDtypeStruct(du.shape, du.dtype), jax.ShapeDtypeStruct((8, 128), f32)),
        input_output_aliases={9: 0},
        scratch_shapes=[pltpu.VMEM((SWA_KV_HEADS, SWA_W, SWA_D), f32)] * 4,
        compiler_params=_params(("arbitrary",)),
    )(u, u, u, u, u, u, u, sink8, do, du)


def _tile(dim, prefs):
    for p in prefs:
        if dim % p == 0:
            return p
    return dim


def _row_tile(rows, d):
    for p in range(min(rows, BLOCK_BYTES // (4 * d)) // 8 * 8, 0, -8):
        if rows % p == 0:
            return p
    return rows


def mm(a, b, *, out_dtype, name, resid=None, relu_grad_of=None, relu2_out=False, ta=False, tb=False, norm_w8=None):
    assert resid is None or relu_grad_of is None
    k, m = (a.shape if ta else a.shape[::-1])
    n = b.shape[0] if tb else b.shape[1]
    rhs_stays = k * 2 * 1024 > MM_OPERAND_BYTES
    if rhs_stays:
        tn = _tile(n, tuple(p for p in (512, 256, 128) if p * k * 2 <= MM_RESIDENT_BYTES))
        tm = _tile(m, tuple(p for p in (512, 384, 256, 128) if p * k * 2 <= MM_OPERAND_BYTES))
        grid = (n // tn, m // tm)
        ij = lambda o, i: (i, o)
    elif k * n * b.dtype.itemsize <= MM_OPERAND_BYTES:
        tn = n
        tm = _tile(m, tuple(p for p in (1408, 1024, 512, 384, 256, 128)
                            if p * k * 2 <= MM_OPERAND_BYTES and p * n * 4 <= MM_OPERAND_BYTES * 2 // 3))
        grid = (m // tm, 1)
        ij = lambda o, i: (o, i)
    else:
        tm = _tile(m, tuple(p for p in (1408, 1024, 512, 384, 256, 128) if p * k * 2 <= MM_OPERAND_BYTES))
        tn = _tile(n, tuple(p for p in (1024, 512, 256, 128) if p * k * 2 <= MM_OPERAND_BYTES // 2))
        grid = (m // tm, n // tn)
        ij = lambda o, i: (o, i)

    extra = resid if resid is not None else relu_grad_of
    staged = ta or norm_w8 is not None
    assert not (ta and norm_w8 is not None) and not (staged and rhs_stays)
    n_in = 2 + (extra is not None) + (norm_w8 is not None)
    n_out = 1 + relu2_out + (norm_w8 is not None)

    def body(*refs):
        ins, outs, scr = refs[:n_in], refs[n_in:n_in + n_out], refs[n_in + n_out:]
        a_ref, b_ref = ins[:2]
        if staged:
            @pl.when(pl.program_id(1) == 0)
            def _():
                if ta:
                    scr[0][...] = a_ref[...].T
                else:
                    hn = _rmsnorm(a_ref[...], ins[-1][0:1, :]).astype(bf16)
                    scr[0][...] = hn
                    outs[-1][...] = hn

            lhs = scr[0][...]
        else:
            lhs = a_ref[...]
        o = _dot(lhs, b_ref[...], NT if tb else NN)
        if resid is not None:
            o = o + ins[2][...]
        if relu_grad_of is not None:
            o = o * (2.0 * jnp.maximum(ins[2][...].astype(f32), 0.0))
        outs[0][...] = o.astype(out_dtype)
        if relu2_out:
            r = jnp.maximum(o, 0.0)
            outs[1][...] = (r * r).astype(bf16)

    in_specs = [pl.BlockSpec((k, tm), lambda o, i: (0, ij(o, i)[0])) if ta
                else pl.BlockSpec((tm, k), lambda o, i: (ij(o, i)[0], 0)),
                pl.BlockSpec((tn, k), lambda o, i: (ij(o, i)[1], 0)) if tb
                else pl.BlockSpec((k, tn), lambda o, i: (0, ij(o, i)[1]))]
    args = [a, b]
    if extra is not None:
        in_specs.append(pl.BlockSpec((tm, tn), ij))
        args.append(extra)
    out_blk = pl.BlockSpec((tm, tn), ij)
    out_specs = [out_blk] * (1 + relu2_out)
    out_shape = [jax.ShapeDtypeStruct((m, n), out_dtype)] + [jax.ShapeDtypeStruct((m, n), bf16)] * relu2_out
    if norm_w8 is not None:
        in_specs.append(pl.BlockSpec((8, k), lambda o, i: (0, 0)))
        args.append(norm_w8)
        out_specs.append(pl.BlockSpec((tm, k), lambda o, i: (ij(o, i)[0], 0)))
        out_shape.append(jax.ShapeDtypeStruct((m, k), bf16))
    res = pl.pallas_call(
        body, name=name, grid=grid, in_specs=in_specs, out_specs=tuple(out_specs), out_shape=tuple(out_shape),
        scratch_shapes=[pltpu.VMEM((tm, k), bf16)] if staged else [],
        compiler_params=_params(("parallel", "arbitrary" if staged else "parallel")),
    )(*args)
    return res[0] if len(res) == 1 else res


def _rows(t_rows):
    return _tile(t_rows, (384, 256, 128))


def _rmsnorm(h, w):
    return h * lax.rsqrt(jnp.mean(h * h, axis=1, keepdims=True) + RMS_EPS) * w


def rmsnorm_bwd(h, w8, dhn, dres, *, name):
    t_rows, d = h.shape
    tr = _rows(t_rows)

    def body(h_ref, w_ref, dhn_ref, dres_ref, dh_ref, dw_ref):
        @pl.when(pl.program_id(0) == 0)
        def _():
            dw_ref[...] = jnp.zeros_like(dw_ref)

        _, vjp = jax.vjp(_rmsnorm, h_ref[...], w_ref[0:1, :])
        dh, dw = vjp(dhn_ref[...])
        dh_ref[...] = dh + dres_ref[...]
        dw_ref[0:1, :] += dw

    blk = pl.BlockSpec((tr, d), lambda i: (i, 0))
    wblk = pl.BlockSpec((8, d), lambda i: (0, 0))
    return pl.pallas_call(
        body, name=name, grid=(t_rows // tr,), in_specs=[blk, wblk, blk, blk], out_specs=(blk, wblk),
        out_shape=(jax.ShapeDtypeStruct((t_rows, d), f32), jax.ShapeDtypeStruct((8, d), f32)),
        compiler_params=_params(("arbitrary",)),
    )(h, w8, dhn, dres)


def _merge(pg, ps, pw, la, lb, lc):
    return _sigmoid(la) * pg + _sigmoid(lb) * ps + _sigmoid(lc) * pw


def _merge_specs(t_rows):
    tr = _rows(t_rows)
    blk = pl.BlockSpec((tr, D_MODEL), lambda i: (i, 0))
    gate = [pl.BlockSpec((tr, D_MODEL), functools.partial(lambda i, j: (i, j), j=C_GATE // D_MODEL + j)) for j in range(3)]
    return tr, blk, gate


def merge_fwd(pg, ps, pw, u):
    t_rows = pg.shape[0]
    tr, blk, gate = _merge_specs(t_rows)

    def body(pg_ref, ps_ref, pw_ref, la, lb, lc, o_ref):
        o_ref[...] = _merge(pg_ref[...].astype(f32), ps_ref[...].astype(f32), pw_ref[...].astype(f32),
                            la[...], lb[...], lc[...]).astype(bf16)

    return pl.pallas_call(
        body, name="merge_fwd", grid=(t_rows // tr,), in_specs=[blk, blk, blk] + gate, out_specs=blk,
        out_shape=jax.ShapeDtypeStruct((t_rows, D_MODEL), bf16), compiler_params=_params(("arbitrary",)),
    )(pg, ps, pw, u, u, u)


def merge_bwd(pg, ps, pw, u, dmerged, du):
    t_rows = pg.shape[0]
    tr, blk, gate = _merge_specs(t_rows)

    def body(pg_ref, ps_ref, pw_ref, la, lb, lc, dm_ref, _, dpg_ref, dps_ref, dpw_ref, dl_ref):
        _, vjp = jax.vjp(_merge, pg_ref[...].astype(f32), ps_ref[...].astype(f32), pw_ref[...].astype(f32),
                         la[...], lb[...], lc[...])
        dpg, dps, dpw, dla, dlb, dlc = vjp(dm_ref[...])
        dpg_ref[...] = dpg.astype(bf16)
        dps_ref[...] = dps.astype(bf16)
        dpw_ref[...] = dpw.astype(bf16)
        for j, dl in enumerate((dla, dlb, dlc)):
            dl_ref[:, j * D_MODEL:(j + 1) * D_MODEL] = dl.astype(bf16)

    act = jax.ShapeDtypeStruct((t_rows, D_MODEL), bf16)
    return pl.pallas_call(
        body, name="merge_bwd", grid=(t_rows // tr,), in_specs=[blk, blk, blk] + gate + [blk, ANY],
        out_specs=(blk, blk, blk, pl.BlockSpec((tr, 3 * D_MODEL), lambda i: (i, C_GATE // (3 * D_MODEL)))),
        out_shape=(act, act, act, jax.ShapeDtypeStruct(du.shape, du.dtype)),
        input_output_aliases={7: 3},
        compiler_params=_params(("arbitrary",)),
    )(pg, ps, pw, u, u, u, dmerged, du)


def loss_head(h, w8, target):
    t_rows, d = h.shape
    tr = HEAD_ROWS

    def loss_fn(hb, w, tgt):
        err = _rmsnorm(hb, w) - tgt
        return 0.5 * jnp.sum(err * err) / d

    def body(h_ref, w_ref, t_ref, loss_ref, dh_ref, dw_ref):
        i = pl.program_id(0)

        @pl.when(i == 0)
        def _():
            loss_ref[...] = jnp.zeros_like(loss_ref)
            dw_ref[...] = jnp.zeros_like(dw_ref)
            dh_ref[...] = jnp.zeros_like(dh_ref)

        @pl.when(i > 0)
        def _():
            val, (dh, dw) = jax.value_and_grad(loss_fn, argnums=(0, 1))(h_ref[...], w_ref[0:1, :], t_ref[...])
            loss_ref[...] += val
            dh_ref[...] = dh
            dw_ref[0:1, :] += dw

    blk = pl.BlockSpec((tr, d), lambda i: (i, 0))
    wblk = pl.BlockSpec((8, d), lambda i: (0, 0))
    return pl.pallas_call(
        body, name="loss_head", grid=(t_rows // tr,),
        in_specs=[blk, wblk, pl.BlockSpec((tr, d), lambda i: (jnp.maximum(i - 1, 0), 0))],
        out_specs=(pl.BlockSpec((8, 128), lambda i: (0, 0)), blk, wblk),
        out_shape=(jax.ShapeDtypeStruct((8, 128), f32), jax.ShapeDtypeStruct((t_rows, d), f32),
                   jax.ShapeDtypeStruct((8, d), f32)),
        compiler_params=_params(("arbitrary",)),
    )(h, w8, target)


def adamw(w, m, v, partials, row_off, *, name):
    rows, d = w.shape
    layers = len(partials)
    per = rows // layers
    tr = _row_tile(per, d)
    assert row_off % tr == 0
    off, nblk = row_off // tr, per // tr
    c1 = 1.0 - ADAM_B1 ** ADAM_STEP
    c2 = 1.0 - ADAM_B2 ** ADAM_STEP

    def body(w_ref, m_ref, v_ref, *refs):
        p_refs, (g_ref, d_ref, mo_ref, vo_ref) = refs[:2 * layers], refs[2 * layers:]
        pair = lambda l: p_refs[2 * l][...].astype(f32) + p_refs[2 * l + 1][...].astype(f32)
        g = pair(0)
        for l in range(1, layers):
            g = jnp.where(pl.program_id(0) >= l * nblk, pair(l), g)
        m_new = ADAM_B1 * m_ref[...] + (1.0 - ADAM_B1) * g
        v_new = ADAM_B2 * v_ref[...] + (1.0 - ADAM_B2) * (g * g)
        g_ref[...] = g
        d_ref[...] = -ADAM_LR * ((m_new / c1) / (jnp.sqrt(v_new / c2) + ADAM_EPS) + ADAM_WD * w_ref[...])
        mo_ref[...] = m_new
        vo_ref[...] = v_new

    blk = pl.BlockSpec((tr, d), lambda i: (i, 0))
    pblks = [pl.BlockSpec((tr, d), functools.partial(lambda i, l: (off + jnp.clip(i - l * nblk, 0, nblk - 1), 0), l=l))
             for l in range(layers) for _ in range(2)]
    out = jax.ShapeDtypeStruct((rows, d), f32)
    return pl.pallas_call(
        body, name=name, grid=(rows // tr,), in_specs=[blk, blk, blk] + pblks, out_specs=(blk,) * 4,
        out_shape=(out,) * 4, compiler_params=_params(("arbitrary",)),
    )(w, m, v, *[p for pair in partials for p in pair])


def reduce4(parts, *, name, own=None, me=None, out_dtype=f32):
    _, rows, d = parts.shape
    tr = _row_tile(rows, d)

    def body(*refs):
        p_ref, o_ref = refs[0], refs[-1]
        acc = None
        for s in range(4):
            term = p_ref[s].astype(f32)
            if own is not None:
                term = jnp.where(refs[2][0] == s, refs[1][...].astype(f32), term)
            acc = term if acc is None else acc + term
        o_ref[...] = acc.astype(out_dtype)

    in_specs = [pl.BlockSpec((4, tr, d), lambda i: (0, i, 0))]
    args = [parts]
    if own is not None:
        in_specs += [pl.BlockSpec((tr, d), lambda i: (i, 0)), pl.BlockSpec(memory_space=pltpu.SMEM)]
        args += [own, me]
    return pl.pallas_call(
        body, name=name, grid=(rows // tr,), in_specs=in_specs,
        out_specs=pl.BlockSpec((tr, d), lambda i: (i, 0)), out_shape=jax.ShapeDtypeStruct((rows, d), out_dtype),
        compiler_params=_params(("arbitrary",)),
    )(*args)


ANY = pl.BlockSpec(memory_space=pl.ANY)
MESH = pl.DeviceIdType.MESH
CHIP_FLIPS = ((0, 1), (1, 0), (1, 1))


def chip_exchange(bufs, scatter, *, name, after=None):
    nb = len(bufs)
    extra = [] if after is None else [after]

    def body(*refs):
        ins, outs = refs[:nb], refs[nb + len(extra):2 * nb + len(extra)]
        send_sems, recv_sems, local_sems = refs[2 * nb + len(extra):]
        x, y, c = lax.axis_index("x"), lax.axis_index("y"), lax.axis_index("c")
        me = 2 * x + y
        local = [pltpu.make_async_copy(ins[j].at[me] if scatter[j] else ins[j], outs[j].at[me], local_sems.at[j])
                 for j in range(nb)]
        for cp in local:
            cp.start()
        sends, recvs = [], []
        for k, (fx, fy) in enumerate(CHIP_FLIPS):
            px = 1 - x if fx else x
            py = 1 - y if fy else y
            chip = 2 * px + py
            for j in range(nb):
                src = ins[j].at[chip] if scatter[j] else ins[j]
                sems = dict(send_sem=send_sems.at[nb * k + j], recv_sem=recv_sems.at[nb * k + j],
                            device_id=(px, py, c), device_id_type=MESH)
                sends.append(pltpu.make_async_remote_copy(src_ref=src, dst_ref=outs[j].at[me], **sems))
                recvs.append(pltpu.make_async_remote_copy(src_ref=src, dst_ref=outs[j].at[chip], **sems))
        for cp in sends:
            cp.start()
        for cp in recvs:
            cp.wait_recv()
        for cp in sends:
            cp.wait_send()
        for cp in local:
            cp.wait()

    out_shape = tuple(jax.ShapeDtypeStruct(b.shape if s else (4,) + b.shape, b.dtype) for b, s in zip(bufs, scatter))
    return pl.pallas_call(
        body, name=name, in_specs=[ANY] * (nb + len(extra)), out_specs=(ANY,) * nb, out_shape=out_shape,
        scratch_shapes=[pltpu.SemaphoreType.DMA((3 * nb,)), pltpu.SemaphoreType.DMA((3 * nb,)),
                        pltpu.SemaphoreType.DMA((nb,))],
        compiler_params=pltpu.CompilerParams(has_side_effects=True),
    )(*bufs, *extra)


def gather_two_level(big, small, *, name):
    half = big.shape[1] // 2

    def body(big_ref, small_ref, obig_ref, osmall_ref, send_sems, recv_sems, local_sems):
        x, y, c = lax.axis_index("x"), lax.axis_index("y"), lax.axis_index("c")
        me = 2 * x + y
        mine = (slice(None), pl.ds(pl.multiple_of(c * half, half), half))
        theirs = (slice(None), pl.ds(pl.multiple_of((1 - c) * half, half), half))
        local = [pltpu.make_async_copy(big_ref, obig_ref.at[me], local_sems.at[0]),
                 pltpu.make_async_copy(small_ref, osmall_ref.at[me], local_sems.at[1])]
        for cp in local:
            cp.start()

        def copy(k, src, dst, to):
            return pltpu.make_async_remote_copy(src_ref=src, dst_ref=dst, send_sem=send_sems.at[k],
                                                recv_sem=recv_sems.at[k], device_id=to, device_id_type=MESH)

        sends, landed, passed, small_in = [], [], [], []
        for k, (fx, fy) in enumerate(CHIP_FLIPS):
            px = 1 - x if fx else x
            py = 1 - y if fy else y
            chip = 2 * px + py
            sends.append(copy(k, big_ref.at[mine], obig_ref.at[(me,) + mine], (px, py, c)))
            landed.append(copy(k, big_ref.at[mine], obig_ref.at[(chip,) + mine], (px, py, c)))
            sends.append(copy(3 + k, small_ref, osmall_ref.at[me], (px, py, c)))
            small_in.append(copy(3 + k, small_ref, osmall_ref.at[chip], (px, py, c)))
            passed.append((copy(6 + k, obig_ref.at[(chip,) + mine], obig_ref.at[(chip,) + mine], (x, y, 1 - c)),
                           copy(6 + k, obig_ref.at[(chip,) + theirs], obig_ref.at[(chip,) + theirs], (x, y, 1 - c))))
        for cp in sends:
            cp.start()
        for k in range(3):
            landed[k].wait_recv()
            passed[k][0].start()
        for k in range(3):
            passed[k][1].wait_recv()
            small_in[k].wait_recv()
        for cp in sends + [p[0] for p in passed]:
            cp.wait_send()
        for cp in local:
            cp.wait()

    return pl.pallas_call(
        body, name=name, in_specs=[ANY, ANY], out_specs=(ANY, ANY),
        out_shape=(jax.ShapeDtypeStruct((4,) + big.shape, big.dtype),
                   jax.ShapeDtypeStruct((4,) + small.shape, small.dtype)),
        scratch_shapes=[pltpu.SemaphoreType.DMA((9,)), pltpu.SemaphoreType.DMA((9,)), pltpu.SemaphoreType.DMA((2,))],
        compiler_params=pltpu.CompilerParams(has_side_effects=True),
    )(big, small)


def sibling_swap(bufs, *, name):
    nb = len(bufs)

    def body(*refs):
        ins, outs, (send_sems, recv_sems) = refs[:nb], refs[nb:2 * nb], refs[2 * nb:]
        peer = (lax.axis_index("x"), lax.axis_index("y"), 1 - lax.axis_index("c"))
        copies = [pltpu.make_async_remote_copy(src_ref=ins[j], dst_ref=outs[j], send_sem=send_sems.at[j],
                                               recv_sem=recv_sems.at[j], device_id=peer, device_id_type=MESH)
                  for j in range(nb)]
        for cp in copies:
            cp.start()
        for cp in copies:
            cp.wait_recv()
        for cp in copies:
            cp.wait_send()

    return pl.pallas_call(
        body, name=name, in_specs=[ANY] * nb, out_specs=(ANY,) * nb,
        out_shape=tuple(jax.ShapeDtypeStruct(b.shape, b.dtype) for b in bufs),
        scratch_shapes=[pltpu.SemaphoreType.DMA((nb,)), pltpu.SemaphoreType.DMA((nb,))],
        compiler_params=pltpu.CompilerParams(has_side_effects=True),
    )(*bufs)


HBM = pl.BlockSpec(memory_space=pltpu.HBM)
SEM = pl.BlockSpec(memory_space=pltpu.SEMAPHORE)
DATAFLOW = pltpu.SideEffectType.DATAFLOW_SIDE_EFFECTING


def _exchange_copies(srcs, lands, send_sems, recv_sems, scatter):
    x, y, c = lax.axis_index("x"), lax.axis_index("y"), lax.axis_index("c")
    me = 2 * x + y
    nb = len(srcs)
    pairs = []
    for k, (fx, fy) in enumerate(CHIP_FLIPS):
        px = 1 - x if fx else x
        py = 1 - y if fy else y
        chip = 2 * px + py
        for j in range(nb):
            src = srcs[j].at[chip] if scatter[j] else srcs[j]
            sems = dict(send_sem=send_sems.at[nb * k + j], recv_sem=recv_sems.at[nb * k + j],
                        device_id=(px, py, c), device_id_type=MESH)
            pairs.append((pltpu.make_async_remote_copy(src_ref=src, dst_ref=lands[j].at[me], **sems),
                          pltpu.make_async_remote_copy(src_ref=src, dst_ref=lands[j].at[chip], **sems)))
    return pairs


def exchange_start(bufs, scatter, after, *, name):
    nb = len(bufs)
    slabs = [b.shape[1:] if s else b.shape for b, s in zip(bufs, scatter)]
    lands = [lax.empty((4,) + shp, b.dtype) for b, shp in zip(bufs, slabs)]

    def body(*refs):
        srcs, zones = refs[:nb], refs[nb:2 * nb]
        send_sems, recv_sems = refs[2 * nb + 1:2 * nb + 3]
        token = refs[-1]
        for send, _ in _exchange_copies(srcs, zones, send_sems, recv_sems, scatter):
            send.start()
        token[...] = jnp.zeros_like(token)

    hbm = lambda a: pltpu.with_memory_space_constraint(a, pltpu.HBM)
    out = pl.pallas_call(
        body, name=name, in_specs=[HBM] * (2 * nb) + [ANY],
        out_specs=(SEM, SEM) + (HBM,) * (2 * nb) + (pl.BlockSpec(memory_space=pltpu.VMEM),),
        out_shape=(pltpu.SemaphoreType.DMA((3 * nb,)), pltpu.SemaphoreType.DMA((3 * nb,)))
        + tuple(pltpu.HBM(a.shape, a.dtype) for a in list(bufs) + lands) + (jax.ShapeDtypeStruct((8, 128), f32),),
        input_output_aliases={i: 2 + i for i in range(2 * nb)},
        compiler_params=pltpu.CompilerParams(has_side_effects=DATAFLOW),
    )(*[hbm(a) for a in list(bufs) + lands], after)
    return (out[:2], out[2:2 + nb], out[2 + nb:2 + 2 * nb], scatter), out[-1]


def exchange_wait(state, after, *, name):
    (send_sems, recv_sems), srcs, lands, scatter = state
    nb = len(srcs)

    def body(*refs):
        src_refs, zones = refs[:nb], refs[nb:2 * nb]
        s_sems, r_sems = refs[2 * nb:2 * nb + 2]
        for send, recv in _exchange_copies(src_refs, zones, s_sems, r_sems, scatter):
            send.wait_send()
            recv.wait_recv()

    out = pl.pallas_call(
        body, name=name, in_specs=[HBM] * (2 * nb) + [SEM, SEM, ANY], out_specs=(HBM,) * (2 * nb),
        out_shape=tuple(pltpu.HBM(a.shape, a.dtype) for a in list(srcs) + list(lands)),
        input_output_aliases={i: i for i in range(2 * nb)},
        compiler_params=pltpu.CompilerParams(has_side_effects=DATAFLOW),
    )(*srcs, *lands, send_sems, recv_sems, after)
    return out[nb:]


BIG = (
    ("w_proj_gdn", 256), ("w_proj_ssd", 256), ("w_proj_swa", 256), ("w_out", 256), ("w_up", 1024), ("w_down", 1024))
BIG_OFF = {}
_o = 0
for _n, _r in BIG:
    BIG_OFF[_n] = _o
    _o += _r
BIG_ROWS = _o
W_IN_SHARD = IN_W // 4

W_NAMES = ('meta_tokens', 'norm1_w', 'w_in', 'gdn_conv_w', 'gdn_a_log', 'gdn_dt_bias', 'gdn_norm_w', 'ssd_conv_w',
           'ssd_conv_b', 'ssd_dt_bias', 'ssd_a_log', 'ssd_d', 'ssd_norm_w', 'swa_sinks', 'w_proj_gdn', 'w_proj_ssd',
           'w_proj_swa', 'w_out', 'norm2_w', 'w_up', 'w_down', 'final_norm_w')
SMALL_NAMES = tuple(n for n in W_NAMES if n not in BIG_OFF and n != "w_in")
SMALL_SHARDED = ("meta_tokens", "gdn_conv_w", "ssd_conv_w")


def _pack_rows(parts, dtype):
    flat = jnp.concatenate([p.reshape(-1).astype(dtype) for p in parts])
    n = -(-flat.shape[0] // 8192) * 8192
    return jnp.pad(flat, (0, n - flat.shape[0])).reshape(-1, D_MODEL)


def _unpack_rows(packed, shapes):
    flat, out, o = packed.reshape(-1), [], 0
    for s in shapes:
        n = 1
        for d in s:
            n *= d
        out.append(flat[o:o + n].reshape(s))
        o += n
    return out


def _split_chips(full, axis):
    s = full.shape
    a = full.reshape(s[:axis] + (4, s[axis] // 4) + s[axis + 1:])
    return jnp.moveaxis(a, axis, 0)


def _join_chips(parts, axis):
    a = jnp.moveaxis(parts, 0, axis)
    s = a.shape
    return a.reshape(s[:axis] + (s[axis] * s[axis + 1],) + s[axis + 2:])


BIG_AXIS = {"w_proj_gdn": 1, "w_proj_ssd": 1, "w_proj_swa": 1, "w_out": 1, "w_up": 2, "w_down": 1}


def _w_in_to_padded(w):
    z = lambda n: jnp.zeros((n,) + w.shape[1:], w.dtype)
    return jnp.concatenate([w[8736:11808], w[4112:7184], w[7200:8736], w[4096:4112], z(112),
                            w[7184:7200], z(112 + C_MID_END - C_SDT - 128), w[0:4096]], axis=0)


def _w_in_from_padded(p):
    return jnp.concatenate([p[C_GQ:IN_WP], p[C_BA:C_BA + 16], p[C_SZ:C_WQ], p[C_SDT:C_SDT + 16], p[C_WQ:C_BA],
                            p[0:C_SZ]], axis=0)


def _row8(v, lane0=0, width=128):
    return jnp.pad(v[None, :], ((0, 7), (lane0, width - lane0 - v.shape[0])))


def _layer_fwd(h, p, l, late=None):
    tag = f"l{l}"
    u, hn = mm(h, p["w_in_t"], tb=True, out_dtype=f32, norm_w8=p["n1"], name=f"mm_in_{tag}")
    yg, stg, tg = gdn_fwd(u, p["gcw"], p["galog"], p["gdtb"], p["gnw"])
    ys, sts = ssd_fwd(u, p["scw"], p["sdtb"], p["salog"], p["sd"], p["snw"])
    yw = swa_fwd(u, p["sink"])
    if late is not None:
        p.update(late(yw))
    pg = mm(yg, p["wpg"], out_dtype=bf16, name=f"mm_pg_{tag}")
    ps = mm(ys, p["wps"], out_dtype=bf16, name=f"mm_ps_{tag}")
    pw = mm(yw, p["wpw"], out_dtype=bf16, name=f"mm_pw_{tag}")
    merged = merge_fwd(pg, ps, pw, u)
    h2 = mm(merged, p["wout"], out_dtype=f32, resid=h, name=f"mm_out_{tag}")
    a, r, hn2 = mm(h2, p["wup"], out_dtype=bf16, relu2_out=True, norm_w8=p["n2"], name=f"mm_up_{tag}")
    h3 = mm(r, p["wdown"], out_dtype=f32, resid=h2, name=f"mm_down_{tag}")
    saved = dict(h=h, hn=hn, u=u, yg=yg, stg=stg, tg=tg, ys=ys, sts=sts, yw=yw, pg=pg, ps=ps, pw=pw,
                 merged=merged, h2=h2, hn2=hn2, a=a, r=r)
    return h3, saved


def _layer_bwd(dh3, p, s, l, send_big, send_w_in):
    tag = f"l{l}"
    g = {}

    def wgrad(act, d, name):
        return mm(act, d, ta=True, out_dtype=bf16, name=f"wg_{name}_{tag}")

    da = mm(dh3, p["wdown"], tb=True, out_dtype=bf16, relu_grad_of=s["a"], name=f"dg_down_{tag}")
    g["w_down"] = wgrad(s["r"], dh3, "down")
    dhn2 = mm(da, p["wup"], tb=True, out_dtype=f32, name=f"dg_up_{tag}")
    g["w_up"] = wgrad(s["hn2"], da, "up")
    dh2, g["norm2_w"] = rmsnorm_bwd(s["h2"], p["n2"], dhn2, dh3, name=f"norm2_bwd_{tag}")
    dmerged = mm(dh2, p["wout"], tb=True, out_dtype=f32, name=f"dg_out_{tag}")
    g["w_out"] = wgrad(s["merged"], dh2, "out")
    du = lax.empty((dh3.shape[0], IN_WP), bf16)
    dpg, dps, dpw, du = merge_bwd(s["pg"], s["ps"], s["pw"], s["u"], dmerged, du)
    dyg = mm(dpg, p["wpg"], tb=True, out_dtype=f32, name=f"dg_pg_{tag}")
    dys = mm(dps, p["wps"], tb=True, out_dtype=f32, name=f"dg_ps_{tag}")
    dyw = mm(dpw, p["wpw"], tb=True, out_dtype=f32, name=f"dg_pw_{tag}")
    g["w_proj_gdn"] = wgrad(s["yg"], dpg, "pg")
    g["w_proj_ssd"] = wgrad(s["ys"], dps, "ps")
    g["w_proj_swa"] = wgrad(s["yw"], dpw, "pw")
    sent = send_big(jnp.concatenate([_split_chips(g.pop(n), BIG_AXIS[n] - 1).reshape(4, r, D_MODEL)
                                     for n, r in BIG], axis=1))

    (du, dba, dtq, dtk, dtv, g["gdn_a_log"], g["gdn_dt_bias"], g["gdn_norm_w"]) = gdn_bwd(
        s["u"], p["gcw"] + sent, p["galog"], p["gdtb"], p["gnw"], s["stg"], s["tg"], dyg, du)
    g["gdn_conv_w"] = jnp.concatenate([dtq, dtk, dtv], axis=1)[:4]
    (du, ddt, dtx, dtb, dtc, g["ssd_dt_bias"], g["ssd_a_log"], g["ssd_d"], g["ssd_norm_w"]) = ssd_bwd(
        s["u"], p["scw"], p["sdtb"], p["salog"], p["sd"], p["snw"], s["sts"], dys, du)
    dconv = jnp.concatenate([dtx, dtb, dtc], axis=1)
    g["ssd_conv_w"], g["ssd_conv_b"] = dconv[:4], dconv[4]
    du, g["swa_sinks"] = swa_bwd(s["u"], p["sink"], dyw, du)
    mid = jnp.concatenate([dba[0].astype(bf16), ddt.astype(bf16),
                           jnp.zeros((du.shape[0], C_MID_END - C_SDT - 128), bf16)], axis=1)
    du = lax.dynamic_update_slice(du, mid, (0, C_BA))
    sent = send_w_in(_w_in_from_padded(wgrad(du, s["hn"], "in")).reshape(4, W_IN_SHARD, D_MODEL))
    dhn = mm(du, p["w_in_t"], out_dtype=f32, name=f"dg_in_{tag}")
    dh, g["norm1_w"] = rmsnorm_bwd(s["h"], p["n1"] + sent, dhn, dh2, name=f"norm1_bwd_{tag}")
    return dh, g


def kernel(x, meta_tokens, norm1_w, w_in, gdn_conv_w, gdn_a_log, gdn_dt_bias, gdn_norm_w, ssd_conv_w, ssd_conv_b, ssd_dt_bias, ssd_a_log, ssd_d, ssd_norm_w, swa_sinks, w_proj_gdn, w_proj_ssd, w_proj_swa, w_out, norm2_w, w_up, w_down, final_norm_w, loss_target, m_meta_tokens, m_norm1_w, m_w_in, m_gdn_conv_w, m_gdn_a_log, m_gdn_dt_bias, m_gdn_norm_w, m_ssd_conv_w, m_ssd_conv_b, m_ssd_dt_bias, m_ssd_a_log, m_ssd_d, m_ssd_norm_w, m_swa_sinks, m_w_proj_gdn, m_w_proj_ssd, m_w_proj_swa, m_w_out, m_norm2_w, m_w_up, m_w_down, m_final_norm_w, v_meta_tokens, v_norm1_w, v_w_in, v_gdn_conv_w, v_gdn_a_log, v_gdn_dt_bias, v_gdn_norm_w, v_ssd_conv_w, v_ssd_conv_b, v_ssd_dt_bias, v_ssd_a_log, v_ssd_d, v_ssd_norm_w, v_swa_sinks, v_w_proj_gdn, v_w_proj_ssd, v_w_proj_swa, v_w_out, v_norm2_w, v_w_up, v_w_down, v_final_norm_w):
    given = dict(locals())
    depth = norm1_w.shape[0]
    me = 2 * lax.axis_index("x") + lax.axis_index("y")

    me1 = jnp.reshape(me, (1,)).astype(jnp.int32)
    w_in_t = jnp.swapaxes(w_in, 1, 2)

    def weight_slabs(l):
        return (w_in_t[l].astype(bf16),
                jnp.concatenate([given[n][l].reshape(-1, D_MODEL).astype(bf16) for n, _ in BIG]))

    slabs = [weight_slabs(l) for l in range(depth)]
    wsmall = _pack_rows([given[n] for n in SMALL_SHARDED], f32)
    ga0, gsmall = gather_two_level(slabs[0][0], wsmall, name="gather_first")
    gathers, started = {}, jnp.zeros((), f32)
    for l in range(depth):
        for j in range(2):
            if (l, j) != (0, 0):
                gathers[l, j], token = exchange_start([slabs[l][j]], (False,), gsmall, name=f"gather_start_l{l}_{j}")
                started = started + token[0, 0]
    shard_shapes = [given[n].shape for n in SMALL_SHARDED]
    per_chip = [_unpack_rows(gsmall[s], shard_shapes) for s in range(4)]
    full = {n: jnp.concatenate([per_chip[s][i] for s in range(4)], axis=-1) for i, n in enumerate(SMALL_SHARDED)}

    def landed(l, j, after):
        (zone,) = exchange_wait(gathers[l, j], after, name=f"gather_wait_l{l}_{j}")
        return lax.dynamic_update_slice(zone, slabs[l][j][None], (me, 0, 0))

    def first_operands(l, ga, order):
        return dict(
            n1=_row8(norm1_w[l], width=D_MODEL) + order, n2=_row8(norm2_w[l], width=D_MODEL),
            w_in_t=_w_in_to_padded(ga.reshape(IN_W, D_MODEL)),
            gcw=jnp.pad(full["gdn_conv_w"][l], ((0, 4), (0, 0))),
            galog=_row8(gdn_a_log[l], 8), gdtb=_row8(gdn_dt_bias[l], 8), gnw=_row8(gdn_norm_w[l]),
            scw=jnp.pad(jnp.concatenate([full["ssd_conv_w"][l], ssd_conv_b[l][None]], axis=0), ((0, 3), (0, 0))),
            sdtb=_row8(ssd_dt_bias[l]), salog=_row8(ssd_a_log[l]), sd=_row8(ssd_d[l]),
            snw=_row8(ssd_norm_w[l], width=D_MODEL), sink=_row8(swa_sinks[l]))

    def late_operands(l, after):
        gb = landed(l, 1, after)
        w = {}
        for n, r in BIG:
            parts = gb[:, BIG_OFF[n]:BIG_OFF[n] + r].reshape((4,) + given[n].shape[1:])
            w[n] = _join_chips(parts, BIG_AXIS[n] - 1)
        return dict(wpg=w["w_proj_gdn"], wps=w["w_proj_ssd"], wpw=w["w_proj_swa"], wout=w["w_out"],
                    wup=w["w_up"], wdown=w["w_down"])

    h = jnp.concatenate([jnp.zeros((PAD, D_MODEL), f32), full["meta_tokens"], x[0]], axis=0)
    layers, saved = [], []
    for l in range(depth):
        p = first_operands(0, ga0, started) if l == 0 else first_operands(l, landed(l, 0, h), 0.0)
        h, s = _layer_fwd(h, p, l, late=functools.partial(late_operands, l))
        layers.append(p)
        saved.append(s)
    loss8, dh, dfw8 = loss_head(h, _row8(final_norm_w, width=D_MODEL), loss_target[0])
    grads = {"final_norm_w": dfw8[0]}
    per_layer, grad_slabs, scatters = [None] * depth, {}, {}

    def send(l, j, slab):
        grad_slabs[l, j] = slab
        scatters[l, j], token = exchange_start([slab], (True,), loss8, name=f"scatter_start_l{l}_{j}")
        return token[0, 0]

    for l in reversed(range(depth)):
        dh, per_layer[l] = _layer_bwd(dh, layers[l], saved[l], l, functools.partial(send, l, 1),
                                      functools.partial(send, l, 0))
    grad_x = dh[HEAD_ROWS:][None]
    grads["meta_tokens"] = dh[PAD:HEAD_ROWS]
    lane = {"gdn_a_log": (8, 8), "gdn_dt_bias": (8, 8), "gdn_norm_w": (0, 128), "ssd_dt_bias": (0, 16),
            "ssd_a_log": (0, 16), "ssd_d": (0, 16), "swa_sinks": (0, 16)}
    for n in per_layer[0]:
        parts = [per_layer[l][n] for l in range(depth)]
        if n in lane:
            parts = [q[0, lane[n][0]:lane[n][0] + lane[n][1]] for q in parts]
        elif n in ("norm1_w", "norm2_w", "ssd_norm_w"):
            parts = [q[0] for q in parts]
        grads[n] = jnp.stack(parts)
    loss = lax.psum(loss8[0, 0], ("x", "y", "c"))

    gs = _pack_rows([grads[n] for n in SMALL_NAMES], f32)
    def chip_sum(l, j, after):
        (zone,) = exchange_wait(scatters[l, j], after, name=f"scatter_wait_l{l}_{j}")
        own = lax.dynamic_index_in_dim(grad_slabs[l, j], me, 0, keepdims=False)
        return reduce4(zone, own=own, me=me1, out_dtype=bf16, name=f"sum_chips_l{l}_{j}")

    early = [(l, j) for l in range(depth) for j in range(2) if (l, j) != (0, 0)]
    mine = {lj: chip_sum(*lj, dh) for lj in early}
    sibs = dict(zip(early, sibling_swap([mine[lj] for lj in early], name="swap_cores_early")))
    out = {}
    for n, r in BIG:
        shp = given[n].shape
        res = adamw(*[given[pre + n].reshape(depth * r, D_MODEL) for pre in ("", "m_", "v_")],
                    [(mine[l, 1], sibs[l, 1]) for l in range(depth)], BIG_OFF[n], name=f"adamw_{n}")
        out[n] = [a.reshape(shp) for a in res]
    mine[0, 0] = chip_sum(0, 0, res[1])
    (rs,) = chip_exchange([gs], (False,), after=mine[0, 0], name="gather_small_grads")
    ps_ = reduce4(rs, name="sum_chips_small")
    sibs[0, 0], ss = sibling_swap([mine[0, 0], ps_], name="swap_cores_last")
    res = adamw(*[jnp.swapaxes(given[pre + "w_in"], 1, 2).reshape(depth * W_IN_SHARD, D_MODEL)
                  for pre in ("", "m_", "v_")],
                [(mine[l, 0], sibs[l, 0]) for l in range(depth)], 0, name="adamw_w_in")
    out["w_in"] = [jnp.swapaxes(a.reshape(w_in_t.shape), 1, 2) for a in res]
    full_shapes = [grads[n].shape for n in SMALL_NAMES]
    mine_s, sib_s = _unpack_rows(ps_, full_shapes), _unpack_rows(ss, full_shapes)

    def local(parts):
        loc = []
        for n, a in zip(SMALL_NAMES, parts):
            if n in SMALL_SHARDED:
                sz = a.shape[-1] // 4
                a = lax.dynamic_slice_in_dim(a, me * sz, sz, axis=a.ndim - 1)
            loc.append(a)
        return _pack_rows(loc, f32)

    res = adamw(_pack_rows([given[n] for n in SMALL_NAMES], f32), _pack_rows([given["m_" + n] for n in SMALL_NAMES], f32),
                _pack_rows([given["v_" + n] for n in SMALL_NAMES], f32), [(local(mine_s), local(sib_s))], 0,
                name="adamw_small")
    local_shapes = [given[n].shape for n in SMALL_NAMES]
    unpacked = [_unpack_rows(a, local_shapes) for a in res]
    for i, n in enumerate(SMALL_NAMES):
        out[n] = [unpacked[j][i] for j in range(4)]

    return (loss, grad_x) + tuple(out[n][j] for j in range(4) for n in W_NAMES)
```
